```python
import math
import numpy as np
import jax
import jax.numpy as jnp
from jax import lax

D_MODEL = 1024
BATCH = 8
SEQ = 8192
DEPTH = 1

CTX_LEN = 256
GRID_W = 64
EPS = 1e-6
CONV_K = 3
CHUNK = 64
N_DIR = 2
N_BRANCH = 2
GDN_HEADS = 8
GDN_DK = 128
GDN_DV = 128
GDN_QK = GDN_HEADS * GDN_DK
GDN_WIDTH = GDN_HEADS * GDN_DV
SSM_INNER = 2 * D_MODEL
SSM_HEAD_DIM = 64
SSM_HEADS = SSM_INNER // SSM_HEAD_DIM
SSM_GROUPS = 4
SSM_STATE = 128
SSM_XBC = SSM_INNER + 2 * SSM_GROUPS * SSM_STATE
D_FF = ((8 * D_MODEL // 3 + 255) // 256) * 256
IN_SIZES = (
    2 * GDN_QK + GDN_WIDTH,
    GDN_WIDTH,
    N_DIR * GDN_HEADS,
    N_DIR * GDN_HEADS,
    SSM_INNER,
    SSM_XBC,
    N_DIR * SSM_HEADS,
    N_BRANCH * D_MODEL,
)
D_IN_PROJ = sum(IN_SIZES)

kernel_name = "bidir_gdn_mamba2_griffin_merge_prefix_ctx"


def _split_cols(t, sizes):
    idx = np.cumsum(np.array(sizes))[:-1].tolist()
    return jnp.split(t, idx, axis=-1)


def _rmsnorm(x, w):
    x32 = x.astype(jnp.float32)
    y = x32 * lax.rsqrt(jnp.mean(x32 * x32, axis=-1, keepdims=True) + EPS)
    return y.astype(x.dtype) * w


def _l2norm(x):
    x32 = x.astype(jnp.float32)
    return (x32 * lax.rsqrt(jnp.sum(x32 * x32, axis=-1, keepdims=True) + EPS)).astype(x.dtype)


def _flip(t):
    return jnp.flip(t, axis=1)


def _dwconv_centred(u, w, b):
    pad = CONV_K // 2
    length = u.shape[-2]
    up = jnp.pad(u, [(0, 0)] * (u.ndim - 2) + [(pad, pad), (0, 0)])
    out = b
    for j in range(CONV_K):
        out = out + up[..., j:j + length, :] * w[j]
    return out


def _short_conv(u, w, b, latent):
    if latent:
        bsz, length, ch = u.shape
        rows = length // GRID_W
        return _dwconv_centred(u.reshape(bsz, rows, GRID_W, ch), w, b).reshape(bsz, length, ch)
    return _dwconv_centred(u, w, b)


def _to_chunks(t):
    bsz, length = t.shape[:2]
    return jnp.moveaxis(t.reshape(bsz, length // CHUNK, CHUNK, *t.shape[2:]), 1, 0)


def _from_chunks(t):
    t = jnp.moveaxis(t, 0, 1)
    return t.reshape(t.shape[0], t.shape[1] * t.shape[2], *t.shape[3:])


def _gdn_chunked(q, k, v, g, beta, s0):
    out_dtype = v.dtype
    q, k, v, g, beta = (t.astype(jnp.float32) for t in (q, k, v, g, beta))
    idx = jnp.arange(CHUNK)
    incl = idx[:, None] >= idx[None, :]
    strict = idx[:, None] > idx[None, :]
    eye = jnp.eye(CHUNK, dtype=jnp.float32)

    def step(s, inp):
        qc, kc, vc, gc, bc = inp
        qh, kh, vh = (jnp.swapaxes(t, 1, 2) for t in (qc, kc, vc))
        gcum = jnp.cumsum(jnp.swapaxes(gc, 1, 2), axis=-1)
        bh = jnp.swapaxes(bc, 1, 2)[..., None]
        decay = jnp.exp(jnp.where(incl, gcum[..., :, None] - gcum[..., None, :], -jnp.inf))
        kb = kh * bh
        lower = jnp.where(strict, jnp.einsum('bhid,bhjd->bhij', kb, kh) * decay, 0.0) + eye
        rhs = jnp.concatenate([kb * jnp.exp(gcum)[..., None], vh * bh], axis=-1)
        sol = lax.linalg.triangular_solve(lower, rhs, left_side=True, lower=True, unit_diagonal=True)
        w_c, u_c = sol[..., :GDN_DK], sol[..., GDN_DK:]
        v_new = u_c - jnp.einsum('bhck,bhkv->bhcv', w_c, s)
        attn = jnp.einsum('bhik,bhjk->bhij', qh, kh) * decay
        o = (jnp.einsum('bhck,bhkv->bhcv', qh * jnp.exp(gcum)[..., None], s)
             + jnp.einsum('bhij,bhjv->bhiv', attn, v_new))
        g_last = gcum[..., -1:]
        s = (s * jnp.exp(g_last)[..., None]
             + jnp.einsum('bhck,bhcv->bhkv', kh * jnp.exp(g_last - gcum)[..., None], v_new))
        return s, jnp.swapaxes(o, 1, 2)

    s_fin, o = lax.scan(step, s0, tuple(_to_chunks(t) for t in (q, k, v, g, beta)))
    return _from_chunks(o).astype(out_dtype), s_fin


def _ssd_chunked(x, dt, a, bm, cm, h0):
    out_dtype = x.dtype
    x, dt, bm, cm = (t.astype(jnp.float32) for t in (x, dt, bm, cm))
    bsz = x.shape[0]
    rep = SSM_HEADS // SSM_GROUPS
    idx = jnp.arange(CHUNK)
    incl = idx[:, None] >= idx[None, :]

    def step(h, inp):
        xc, dtc, bc, cc = inp
        acum = jnp.cumsum(jnp.swapaxes(dtc * a, 1, 2), axis=-1).reshape(bsz, SSM_GROUPS, rep, CHUNK)
        seg = jnp.exp(jnp.where(incl, acum[..., :, None] - acum[..., None, :], -jnp.inf))
        xdt = (xc * dtc[..., None]).reshape(bsz, CHUNK, SSM_GROUPS, rep, SSM_HEAD_DIM)
        cb = jnp.einsum('bign,bjgn->bgij', cc, bc)
        y_diag = jnp.einsum('bgij,bgrij,bjgrp->bigrp', cb, seg, xdt)
        hg = h.reshape(bsz, SSM_GROUPS, rep, SSM_HEAD_DIM, SSM_STATE)
        y_off = jnp.einsum('bign,bgrpn,bgri->bigrp', cc, hg, jnp.exp(acum))
        a_last = acum[..., -1:]
        new = jnp.einsum('bjgn,bgrj,bjgrp->bgrpn', bc, jnp.exp(a_last - acum), xdt)
        hg = hg * jnp.exp(a_last)[..., None] + new
        y = (y_diag + y_off).reshape(bsz, CHUNK, SSM_HEADS, SSM_HEAD_DIM)
        return hg.reshape(h.shape), y

    h_fin, y = lax.scan(step, h0, tuple(_to_chunks(t) for t in (x, dt, bm, cm)))
    return _from_chunks(y).astype(out_dtype), h_fin


def _gdn_bidir(q, k, v, g, beta, s_f, s_b):
    o_f, s_f = _gdn_chunked(q, k, v, g[:, :, 0], beta[:, :, 0], s_f)
    o_b, s_b = _gdn_chunked(_flip(q), _flip(k), _flip(v), _flip(g[:, :, 1]), _flip(beta[:, :, 1]), s_b)
    return o_f + _flip(o_b), s_f, s_b


def _ssd_bidir(xs, dt, a_log, bm, cm, h_f, h_b):
    a = -jnp.exp(a_log.astype(jnp.float32))
    y_f, h_f = _ssd_chunked(xs, dt[:, :, 0], a[0], bm, cm, h_f)
    y_b, h_b = _ssd_chunked(_flip(xs), _flip(dt[:, :, 1]), a[1], _flip(bm), _flip(cm), h_b)
    return y_f + _flip(y_b), h_f, h_b


def _mixer_inputs(h, latent, w_in, gdn_conv_w, gdn_conv_b, gdn_a_log, gdn_dt_bias,
                  ssm_conv_w, ssm_conv_b, ssm_dt_bias):
    bsz, length, _ = h.shape
    qkv, z_gdn, a_gdn, b_gdn, z_ssm, xbc, dt_raw, br_gate = _split_cols(h @ w_in, IN_SIZES)
    qkv = jax.nn.silu(_short_conv(qkv, gdn_conv_w, gdn_conv_b, latent))
    q, k, v = _split_cols(qkv, (GDN_QK, GDN_QK, GDN_WIDTH))
    q = _l2norm(q.reshape(bsz, length, GDN_HEADS, GDN_DK)) * GDN_DK ** -0.5
    k = _l2norm(k.reshape(bsz, length, GDN_HEADS, GDN_DK))
    v = v.reshape(bsz, length, GDN_HEADS, GDN_DV)
    g = -jnp.exp(gdn_a_log) * jax.nn.softplus(a_gdn.reshape(bsz, length, N_DIR, GDN_HEADS) + gdn_dt_bias)
    beta = jax.nn.sigmoid(b_gdn.reshape(bsz, length, N_DIR, GDN_HEADS))
    xbc = jax.nn.silu(_short_conv(xbc, ssm_conv_w, ssm_conv_b, latent))
    xs, bm, cm = _split_cols(xbc, (SSM_INNER, SSM_GROUPS * SSM_STATE, SSM_GROUPS * SSM_STATE))
    xs = xs.reshape(bsz, length, SSM_HEADS, SSM_HEAD_DIM)
    bm = bm.reshape(bsz, length, SSM_GROUPS, SSM_STATE)
    cm = cm.reshape(bsz, length, SSM_GROUPS, SSM_STATE)
    dt = jax.nn.softplus(dt_raw.reshape(bsz, length, N_DIR, SSM_HEADS) + ssm_dt_bias)
    return q, k, v, g, beta, z_gdn, xs, bm, cm, dt, z_ssm, br_gate


def _mixer_output(o_gdn, z_gdn, y_ssm, xs, z_ssm, br_gate, gdn_norm_w, ssm_d, ssm_norm_w,
                  w_br_gdn, w_br_ssm, w_out):
    bsz, length = o_gdn.shape[:2]
    o = _rmsnorm(o_gdn, gdn_norm_w) * jax.nn.silu(z_gdn.reshape(bsz, length, GDN_HEADS, GDN_DV))
    p_gdn = o.reshape(bsz, length, GDN_WIDTH) @ w_br_gdn
    y = (y_ssm + ssm_d[:, None] * xs).reshape(bsz, length, SSM_INNER) * jax.nn.silu(z_ssm)
    y = _rmsnorm(y.reshape(bsz, length, SSM_GROUPS, SSM_INNER // SSM_GROUPS),
                 ssm_norm_w.reshape(SSM_GROUPS, SSM_INNER // SSM_GROUPS)).reshape(bsz, length, SSM_INNER)
    p_ssm = y @ w_br_ssm
    gate_gdn, gate_ssm = jnp.split(br_gate, N_BRANCH, axis=-1)
    merged = jax.nn.sigmoid(gate_gdn) * p_gdn + jax.nn.sigmoid(gate_ssm) * p_ssm
    return merged @ w_out


def _swiglu(h, w_ffn_in, w_ffn_out):
    gate, up = jnp.split(h @ w_ffn_in, 2, axis=-1)
    return (jax.nn.silu(gate) * up) @ w_ffn_out


def _fwd_setup_inputs(seed: int = 0) -> dict:
    key = jax.random.key(seed)
    ks = jax.random.split(key, 26)

    def nrm(i, shape, scale):
        return scale * jax.random.normal(ks[i], shape, jnp.float32)

    def gain(i, shape):
        return 1.0 + 0.02 * jax.random.normal(ks[i], shape, jnp.float32)

    def a_log(i, shape):
        return jnp.log(jax.random.uniform(ks[i], shape, jnp.float32, 1.0, 16.0))

    def dt_bias(i, shape):
        dt = jnp.exp(jax.random.uniform(ks[i], shape, jnp.float32, math.log(1e-3), math.log(1e-1)))
        return dt + jnp.log(-jnp.expm1(-dt))

    return {
        "x": nrm(0, (BATCH, SEQ, D_MODEL), 1.0),
        "c": nrm(1, (BATCH, D_MODEL), 1.0),
        "ctx": nrm(2, (BATCH, CTX_LEN, D_MODEL), 1.0),
        "c_ctx": nrm(3, (D_MODEL,), 1.0),
        "ada_w": nrm(4, (DEPTH, D_MODEL, 6 * D_MODEL), 0.5 * D_MODEL ** -0.5),
        "ada_b": nrm(5, (DEPTH, 6 * D_MODEL), 0.01),
        "norm1_w": gain(6, (DEPTH, D_MODEL)),
        "w_in": nrm(7, (DEPTH, D_MODEL, D_IN_PROJ), D_MODEL ** -0.5),
        "gdn_conv_w": nrm(8, (DEPTH, CONV_K, 2 * GDN_QK + GDN_WIDTH), CONV_K ** -0.5),
        "gdn_conv_b": nrm(9, (DEPTH, 2 * GDN_QK + GDN_WIDTH), 0.01),
        "gdn_a_log": a_log(10, (DEPTH, N_DIR, GDN_HEADS)),
        "gdn_dt_bias": dt_bias(11, (DEPTH, N_DIR, GDN_HEADS)),
        "gdn_norm_w": gain(12, (DEPTH, GDN_DV)),
        "ssm_conv_w": nrm(13, (DEPTH, CONV_K, SSM_XBC), CONV_K ** -0.5),
        "ssm_conv_b": nrm(14, (DEPTH, SSM_XBC), 0.01),
        "ssm_a_log": a_log(15, (DEPTH, N_DIR, SSM_HEADS)),
        "ssm_dt_bias": dt_bias(16, (DEPTH, N_DIR, SSM_HEADS)),
        "ssm_d": gain(17, (DEPTH, SSM_HEADS)),
        "ssm_norm_w": gain(18, (DEPTH, SSM_INNER)),
        "w_br_gdn": nrm(19, (DEPTH, GDN_WIDTH, D_MODEL), GDN_WIDTH ** -0.5),
        "w_br_ssm": nrm(20, (DEPTH, SSM_INNER, D_MODEL), SSM_INNER ** -0.5),
        "w_out": nrm(21, (DEPTH, D_MODEL, D_MODEL), D_MODEL ** -0.5),
        "norm2_w": gain(22, (DEPTH, D_MODEL)),
        "w_ffn_in": nrm(23, (DEPTH, D_MODEL, 2 * D_FF), D_MODEL ** -0.5),
        "w_ffn_out": nrm(24, (DEPTH, D_FF, D_MODEL), D_FF ** -0.5),
        "norm_f_w": gain(25, (D_MODEL,)),
    }


def _fwd_reference(x, c, ctx, c_ctx, ada_w, ada_b, norm1_w, w_in, gdn_conv_w, gdn_conv_b, gdn_a_log,
              gdn_dt_bias, gdn_norm_w, ssm_conv_w, ssm_conv_b, ssm_a_log, ssm_dt_bias, ssm_d,
              ssm_norm_w, w_br_gdn, w_br_ssm, w_out, norm2_w, w_ffn_in, w_ffn_out, norm_f_w):
    bsz = x.shape[0]
    silu_c = jax.nn.silu(c)[:, None, :]
    silu_cc = jax.nn.silu(c_ctx)
    h_lat, h_ctx = x, ctx
    for i in range(DEPTH):
        sh1, sc1, g1, sh2, sc2, g2 = jnp.split(silu_c @ ada_w[i] + ada_b[i], 6, axis=-1)
        csh1, csc1, cg1, csh2, csc2, cg2 = jnp.split(silu_cc @ ada_w[i] + ada_b[i], 6, axis=-1)
        prm = (w_in[i], gdn_conv_w[i], gdn_conv_b[i], gdn_a_log[i], gdn_dt_bias[i],
               ssm_conv_w[i], ssm_conv_b[i], ssm_dt_bias[i])
        a_lat = _rmsnorm(h_lat, norm1_w[i]) * (1 + sc1) + sh1
        a_ctx = _rmsnorm(h_ctx, norm1_w[i]) * (1 + csc1) + csh1
        (cq, ck, cv, cgd, cbeta, cz_gdn, cxs, cbm, ccm, cdt, cz_ssm, cgate) = _mixer_inputs(a_ctx, False, *prm)
        (lq, lk, lv, lgd, lbeta, lz_gdn, lxs, lbm, lcm, ldt, lz_ssm, lgate) = _mixer_inputs(a_lat, True, *prm)
        s0 = jnp.zeros((bsz, GDN_HEADS, GDN_DK, GDN_DV), jnp.float32)
        h0 = jnp.zeros((bsz, SSM_HEADS, SSM_HEAD_DIM, SSM_STATE), jnp.float32)
        co_gdn, s_f, s_b = _gdn_bidir(cq, ck, cv, cgd, cbeta, s0, s0)
        lo_gdn, _, _ = _gdn_bidir(lq, lk, lv, lgd, lbeta, s_f, s_b)
        cy_ssm, hf, hb = _ssd_bidir(cxs, cdt, ssm_a_log[i], cbm, ccm, h0, h0)
        ly_ssm, _, _ = _ssd_bidir(lxs, ldt, ssm_a_log[i], lbm, lcm, hf, hb)
        out_prm = (gdn_norm_w[i], ssm_d[i], ssm_norm_w[i], w_br_gdn[i], w_br_ssm[i], w_out[i])
        h_lat = h_lat + g1 * _mixer_output(lo_gdn, lz_gdn, ly_ssm, lxs, lz_ssm, lgate, *out_prm)
        f_lat = _rmsnorm(h_lat, norm2_w[i]) * (1 + sc2) + sh2
        h_lat = h_lat + g2 * _swiglu(f_lat, w_ffn_in[i], w_ffn_out[i])
        if i < DEPTH - 1:
            h_ctx = h_ctx + cg1 * _mixer_output(co_gdn, cz_gdn, cy_ssm, cxs, cz_ssm, cgate, *out_prm)
            f_ctx = _rmsnorm(h_ctx, norm2_w[i]) * (1 + csc2) + csh2
            h_ctx = h_ctx + cg2 * _swiglu(f_ctx, w_ffn_in[i], w_ffn_out[i])
    return _rmsnorm(h_lat, norm_f_w)


import jax as _jax
import jax.numpy as _jnp

TWIN_FORMAT = 'train_step'
FWD_PARAMS = ['x', 'c', 'ctx', 'c_ctx', 'ada_w', 'ada_b', 'norm1_w', 'w_in', 'gdn_conv_w', 'gdn_conv_b', 'gdn_a_log', 'gdn_dt_bias', 'gdn_norm_w', 'ssm_conv_w', 'ssm_conv_b', 'ssm_a_log', 'ssm_dt_bias', 'ssm_d', 'ssm_norm_w', 'w_br_gdn', 'w_br_ssm', 'w_out', 'norm2_w', 'w_ffn_in', 'w_ffn_out', 'norm_f_w']
TWIN_WEIGHTS = ['c_ctx', 'ada_w', 'ada_b', 'norm1_w', 'w_in', 'gdn_conv_w', 'gdn_conv_b', 'gdn_a_log', 'gdn_dt_bias', 'gdn_norm_w', 'ssm_conv_w', 'ssm_conv_b', 'ssm_a_log', 'ssm_dt_bias', 'ssm_d', 'ssm_norm_w', 'w_br_gdn', 'w_br_ssm', 'w_out', 'norm2_w', 'w_ffn_in', 'w_ffn_out', 'norm_f_w']
TWIN_DIFF_INPUT = 'x'
TWIN_INPUTS = ['x', 'c', 'ctx', 'c_ctx', 'ada_w', 'ada_b', 'norm1_w', 'w_in', 'gdn_conv_w', 'gdn_conv_b', 'gdn_a_log', 'gdn_dt_bias', 'gdn_norm_w', 'ssm_conv_w', 'ssm_conv_b', 'ssm_a_log', 'ssm_dt_bias', 'ssm_d', 'ssm_norm_w', 'w_br_gdn', 'w_br_ssm', 'w_out', 'norm2_w', 'w_ffn_in', 'w_ffn_out', 'norm_f_w', 'loss_target', 'm_c_ctx', 'm_ada_w', 'm_ada_b', 'm_norm1_w', 'm_w_in', 'm_gdn_conv_w', 'm_gdn_conv_b', 'm_gdn_a_log', 'm_gdn_dt_bias', 'm_gdn_norm_w', 'm_ssm_conv_w', 'm_ssm_conv_b', 'm_ssm_a_log', 'm_ssm_dt_bias', 'm_ssm_d', 'm_ssm_norm_w', 'm_w_br_gdn', 'm_w_br_ssm', 'm_w_out', 'm_norm2_w', 'm_w_ffn_in', 'm_w_ffn_out', 'm_norm_f_w', 'v_c_ctx', 'v_ada_w', 'v_ada_b', 'v_norm1_w', 'v_w_in', 'v_gdn_conv_w', 'v_gdn_conv_b', 'v_gdn_a_log', 'v_gdn_dt_bias', 'v_gdn_norm_w', 'v_ssm_conv_w', 'v_ssm_conv_b', 'v_ssm_a_log', 'v_ssm_dt_bias', 'v_ssm_d', 'v_ssm_norm_w', 'v_w_br_gdn', 'v_w_br_ssm', 'v_w_out', 'v_norm2_w', 'v_w_ffn_in', 'v_w_ffn_out', 'v_norm_f_w']
TWIN_OUTPUTS = ['loss', 'grad_x', 'grad_c_ctx', 'grad_ada_w', 'grad_ada_b', 'grad_norm1_w', 'grad_w_in', 'grad_gdn_conv_w', 'grad_gdn_conv_b', 'grad_gdn_a_log', 'grad_gdn_dt_bias', 'grad_gdn_norm_w', 'grad_ssm_conv_w', 'grad_ssm_conv_b', 'grad_ssm_a_log', 'grad_ssm_dt_bias', 'grad_ssm_d', 'grad_ssm_norm_w', 'grad_w_br_gdn', 'grad_w_br_ssm', 'grad_w_out', 'grad_norm2_w', 'grad_w_ffn_in', 'grad_w_ffn_out', 'grad_norm_f_w', 'delta_c_ctx', 'delta_ada_w', 'delta_ada_b', 'delta_norm1_w', 'delta_w_in', 'delta_gdn_conv_w', 'delta_gdn_conv_b', 'delta_gdn_a_log', 'delta_gdn_dt_bias', 'delta_gdn_norm_w', 'delta_ssm_conv_w', 'delta_ssm_conv_b', 'delta_ssm_a_log', 'delta_ssm_dt_bias', 'delta_ssm_d', 'delta_ssm_norm_w', 'delta_w_br_gdn', 'delta_w_br_ssm', 'delta_w_out', 'delta_norm2_w', 'delta_w_ffn_in', 'delta_w_ffn_out', 'delta_norm_f_w', 'new_m_c_ctx', 'new_m_ada_w', 'new_m_ada_b', 'new_m_norm1_w', 'new_m_w_in', 'new_m_gdn_conv_w', 'new_m_gdn_conv_b', 'new_m_gdn_a_log', 'new_m_gdn_dt_bias', 'new_m_gdn_norm_w', 'new_m_ssm_conv_w', 'new_m_ssm_conv_b', 'new_m_ssm_a_log', 'new_m_ssm_dt_bias', 'new_m_ssm_d', 'new_m_ssm_norm_w', 'new_m_w_br_gdn', 'new_m_w_br_ssm', 'new_m_w_out', 'new_m_norm2_w', 'new_m_w_ffn_in', 'new_m_w_ffn_out', 'new_m_norm_f_w', 'new_v_c_ctx', 'new_v_ada_w', 'new_v_ada_b', 'new_v_norm1_w', 'new_v_w_in', 'new_v_gdn_conv_w', 'new_v_gdn_conv_b', 'new_v_gdn_a_log', 'new_v_gdn_dt_bias', 'new_v_gdn_norm_w', 'new_v_ssm_conv_w', 'new_v_ssm_conv_b', 'new_v_ssm_a_log', 'new_v_ssm_dt_bias', 'new_v_ssm_d', 'new_v_ssm_norm_w', 'new_v_w_br_gdn', 'new_v_w_br_ssm', 'new_v_w_out', 'new_v_norm2_w', 'new_v_w_ffn_in', 'new_v_w_ffn_out', 'new_v_norm_f_w']
TWIN_LEAF_KINDS = {'loss': 'loss', 'grad_x': 'grad_x', 'grad_c_ctx': 'grad_w', 'grad_ada_w': 'grad_w', 'grad_ada_b': 'grad_w', 'grad_norm1_w': 'grad_w', 'grad_w_in': 'grad_w', 'grad_gdn_conv_w': 'grad_w', 'grad_gdn_conv_b': 'grad_w', 'grad_gdn_a_log': 'grad_w', 'grad_gdn_dt_bias': 'grad_w', 'grad_gdn_norm_w': 'grad_w', 'grad_ssm_conv_w': 'grad_w', 'grad_ssm_conv_b': 'grad_w', 'grad_ssm_a_log': 'grad_w', 'grad_ssm_dt_bias': 'grad_w', 'grad_ssm_d': 'grad_w', 'grad_ssm_norm_w': 'grad_w', 'grad_w_br_gdn': 'grad_w', 'grad_w_br_ssm': 'grad_w', 'grad_w_out': 'grad_w', 'grad_norm2_w': 'grad_w', 'grad_w_ffn_in': 'grad_w', 'grad_w_ffn_out': 'grad_w', 'grad_norm_f_w': 'grad_w', 'delta_c_ctx': 'delta_w', 'delta_ada_w': 'delta_w', 'delta_ada_b': 'delta_w', 'delta_norm1_w': 'delta_w', 'delta_w_in': 'delta_w', 'delta_gdn_conv_w': 'delta_w', 'delta_gdn_conv_b': 'delta_w', 'delta_gdn_a_log': 'delta_w', 'delta_gdn_dt_bias': 'delta_w', 'delta_gdn_norm_w': 'delta_w', 'delta_ssm_conv_w': 'delta_w', 'delta_ssm_conv_b': 'delta_w', 'delta_ssm_a_log': 'delta_w', 'delta_ssm_dt_bias': 'delta_w', 'delta_ssm_d': 'delta_w', 'delta_ssm_norm_w': 'delta_w', 'delta_w_br_gdn': 'delta_w', 'delta_w_br_ssm': 'delta_w', 'delta_w_out': 'delta_w', 'delta_norm2_w': 'delta_w', 'delta_w_ffn_in': 'delta_w', 'delta_w_ffn_out': 'delta_w', 'delta_norm_f_w': 'delta_w', 'new_m_c_ctx': 'new_m', 'new_m_ada_w': 'new_m', 'new_m_ada_b': 'new_m', 'new_m_norm1_w': 'new_m', 'new_m_w_in': 'new_m', 'new_m_gdn_conv_w': 'new_m', 'new_m_gdn_conv_b': 'new_m', 'new_m_gdn_a_log': 'new_m', 'new_m_gdn_dt_bias': 'new_m', 'new_m_gdn_norm_w': 'new_m', 'new_m_ssm_conv_w': 'new_m', 'new_m_ssm_conv_b': 'new_m', 'new_m_ssm_a_log': 'new_m', 'new_m_ssm_dt_bias': 'new_m', 'new_m_ssm_d': 'new_m', 'new_m_ssm_norm_w': 'new_m', 'new_m_w_br_gdn': 'new_m', 'new_m_w_br_ssm': 'new_m', 'new_m_w_out': 'new_m', 'new_m_norm2_w': 'new_m', 'new_m_w_ffn_in': 'new_m', 'new_m_w_ffn_out': 'new_m', 'new_m_norm_f_w': 'new_m', 'new_v_c_ctx': 'new_v', 'new_v_ada_w': 'new_v', 'new_v_ada_b': 'new_v', 'new_v_norm1_w': 'new_v', 'new_v_w_in': 'new_v', 'new_v_gdn_conv_w': 'new_v', 'new_v_gdn_conv_b': 'new_v', 'new_v_gdn_a_log': 'new_v', 'new_v_gdn_dt_bias': 'new_v', 'new_v_gdn_norm_w': 'new_v', 'new_v_ssm_conv_w': 'new_v', 'new_v_ssm_conv_b': 'new_v', 'new_v_ssm_a_log': 'new_v', 'new_v_ssm_dt_bias': 'new_v', 'new_v_ssm_d': 'new_v', 'new_v_ssm_norm_w': 'new_v', 'new_v_w_br_gdn': 'new_v', 'new_v_w_br_ssm': 'new_v', 'new_v_w_out': 'new_v', 'new_v_norm2_w': 'new_v', 'new_v_w_ffn_in': 'new_v', 'new_v_w_ffn_out': 'new_v', 'new_v_norm_f_w': 'new_v'}


def _forward(args):
    return _fwd_reference(*[args[k] for k in FWD_PARAMS])


def _output_shape():
    def fwd():
        inp = _fwd_setup_inputs(0)
        return _fwd_reference(*[inp[k] for k in FWD_PARAMS])
    out = _jax.eval_shape(fwd)
    return out.shape, out.dtype

N_MICROBATCH = 1
ADAM_LR = 0.001
ADAM_B1 = 0.9
ADAM_B2 = 0.999
ADAM_EPS = 1e-08
ADAM_WD = 0.01
ADAM_STEP = 10
PER_EXAMPLE_BATCH_AXIS = {'x': 0, 'c': 0, 'ctx': 0, 'loss_target': 0}
SHARED_INPUTS = []
_WEIGHT_DTYPES = {'c_ctx': _jnp.float32, 'ada_w': _jnp.float32, 'ada_b': _jnp.float32, 'norm1_w': _jnp.float32, 'w_in': _jnp.float32, 'gdn_conv_w': _jnp.float32, 'gdn_conv_b': _jnp.float32, 'gdn_a_log': _jnp.float32, 'gdn_dt_bias': _jnp.float32, 'gdn_norm_w': _jnp.float32, 'ssm_conv_w': _jnp.float32, 'ssm_conv_b': _jnp.float32, 'ssm_a_log': _jnp.float32, 'ssm_dt_bias': _jnp.float32, 'ssm_d': _jnp.float32, 'ssm_norm_w': _jnp.float32, 'w_br_gdn': _jnp.float32, 'w_br_ssm': _jnp.float32, 'w_out': _jnp.float32, 'norm2_w': _jnp.float32, 'w_ffn_in': _jnp.float32, 'w_ffn_out': _jnp.float32, 'norm_f_w': _jnp.float32}
MOMENT_SCALE = {'c_ctx': 3.655752e-03, 'ada_w': 7.311950e-02, 'ada_b': 1.259739e-01, 'norm1_w': 8.070512e-02, 'w_in': 2.366992e-02, 'gdn_conv_w': 1.995819e-02, 'gdn_conv_b': 2.410419e-02, 'gdn_a_log': 9.246645e-02, 'gdn_dt_bias': 9.483095e-02, 'gdn_norm_w': 7.885084e-02, 'ssm_conv_w': 2.580070e-02, 'ssm_conv_b': 3.565806e-02, 'ssm_a_log': 8.901245e-02, 'ssm_dt_bias': 5.156600e-02, 'ssm_d': 9.387893e-02, 'ssm_norm_w': 3.205029e-02, 'w_br_gdn': 2.710301e-02, 'w_br_ssm': 4.140831e-02, 'w_out': 4.965824e-02, 'norm2_w': 7.261827e-02, 'w_ffn_in': 3.197977e-02, 'w_ffn_out': 5.244106e-02, 'norm_f_w': 6.401548e+01}


def _to_microbatches(a, axis):
    t = _jnp.moveaxis(a, axis, 0)
    t = t.reshape((N_MICROBATCH, t.shape[0] // N_MICROBATCH) + t.shape[1:])
    return _jnp.moveaxis(t, 1, axis + 1)


def setup_inputs(seed: int = 0) -> dict:
    inp = _fwd_setup_inputs(seed)
    key = _jax.random.fold_in(_jax.random.key(seed), 7919)
    shape, _ = _output_shape()
    out = dict(inp)
    out["loss_target"] = _jax.random.normal(_jax.random.fold_in(key, 0), shape, _jnp.float32)
    for i, name in enumerate(TWIN_WEIGHTS):
        w = inp[name].astype(_jnp.float32)
        if MOMENT_SCALE is None:
            s = _jnp.sqrt(_jnp.mean(_jnp.square(w)) + 1e-30)
        else:
            s = MOMENT_SCALE[name]
        km, kv = _jax.random.split(_jax.random.fold_in(key, i + 1))
        out[name] = w
        out["m_" + name] = s * _jax.random.normal(km, w.shape, _jnp.float32)
        out["v_" + name] = (s * s) * _jax.random.uniform(kv, w.shape, _jnp.float32, 0.5, 1.5)
    if N_MICROBATCH > 1:
        for name, axis in PER_EXAMPLE_BATCH_AXIS.items():
            out[name] = _to_microbatches(out[name], axis)
    return {'x': out['x'], 'c': out['c'], 'ctx': out['ctx'], 'c_ctx': out['c_ctx'], 'ada_w': out['ada_w'], 'ada_b': out['ada_b'], 'norm1_w': out['norm1_w'], 'w_in': out['w_in'], 'gdn_conv_w': out['gdn_conv_w'], 'gdn_conv_b': out['gdn_conv_b'], 'gdn_a_log': out['gdn_a_log'], 'gdn_dt_bias': out['gdn_dt_bias'], 'gdn_norm_w': out['gdn_norm_w'], 'ssm_conv_w': out['ssm_conv_w'], 'ssm_conv_b': out['ssm_conv_b'], 'ssm_a_log': out['ssm_a_log'], 'ssm_dt_bias': out['ssm_dt_bias'], 'ssm_d': out['ssm_d'], 'ssm_norm_w': out['ssm_norm_w'], 'w_br_gdn': out['w_br_gdn'], 'w_br_ssm': out['w_br_ssm'], 'w_out': out['w_out'], 'norm2_w': out['norm2_w'], 'w_ffn_in': out['w_ffn_in'], 'w_ffn_out': out['w_ffn_out'], 'norm_f_w': out['norm_f_w'], 'loss_target': out['loss_target'], 'm_c_ctx': out['m_c_ctx'], 'm_ada_w': out['m_ada_w'], 'm_ada_b': out['m_ada_b'], 'm_norm1_w': out['m_norm1_w'], 'm_w_in': out['m_w_in'], 'm_gdn_conv_w': out['m_gdn_conv_w'], 'm_gdn_conv_b': out['m_gdn_conv_b'], 'm_gdn_a_log': out['m_gdn_a_log'], 'm_gdn_dt_bias': out['m_gdn_dt_bias'], 'm_gdn_norm_w': out['m_gdn_norm_w'], 'm_ssm_conv_w': out['m_ssm_conv_w'], 'm_ssm_conv_b': out['m_ssm_conv_b'], 'm_ssm_a_log': out['m_ssm_a_log'], 'm_ssm_dt_bias': out['m_ssm_dt_bias'], 'm_ssm_d': out['m_ssm_d'], 'm_ssm_norm_w': out['m_ssm_norm_w'], 'm_w_br_gdn': out['m_w_br_gdn'], 'm_w_br_ssm': out['m_w_br_ssm'], 'm_w_out': out['m_w_out'], 'm_norm2_w': out['m_norm2_w'], 'm_w_ffn_in': out['m_w_ffn_in'], 'm_w_ffn_out': out['m_w_ffn_out'], 'm_norm_f_w': out['m_norm_f_w'], 'v_c_ctx': out['v_c_ctx'], 'v_ada_w': out['v_ada_w'], 'v_ada_b': out['v_ada_b'], 'v_norm1_w': out['v_norm1_w'], 'v_w_in': out['v_w_in'], 'v_gdn_conv_w': out['v_gdn_conv_w'], 'v_gdn_conv_b': out['v_gdn_conv_b'], 'v_gdn_a_log': out['v_gdn_a_log'], 'v_gdn_dt_bias': out['v_gdn_dt_bias'], 'v_gdn_norm_w': out['v_gdn_norm_w'], 'v_ssm_conv_w': out['v_ssm_conv_w'], 'v_ssm_conv_b': out['v_ssm_conv_b'], 'v_ssm_a_log': out['v_ssm_a_log'], 'v_ssm_dt_bias': out['v_ssm_dt_bias'], 'v_ssm_d': out['v_ssm_d'], 'v_ssm_norm_w': out['v_ssm_norm_w'], 'v_w_br_gdn': out['v_w_br_gdn'], 'v_w_br_ssm': out['v_w_br_ssm'], 'v_w_out': out['v_w_out'], 'v_norm2_w': out['v_norm2_w'], 'v_w_ffn_in': out['v_w_ffn_in'], 'v_w_ffn_out': out['v_w_ffn_out'], 'v_norm_f_w': out['v_norm_f_w']}


def _loss(weights, diff, rest, loss_target):
    with _jax.named_scope("forward"):
        args = {**rest, TWIN_DIFF_INPUT: diff, **{k: w.astype(_WEIGHT_DTYPES[k]) for k, w in weights.items()}}
        y = _forward(args)
    with _jax.named_scope("loss_head"):
        err = _jnp.square(y.astype(_jnp.float32) - loss_target)
        return 0.5 * _jnp.sum(_jnp.mean(err, axis=-1)) if err.ndim else 0.5 * err


def _adamw(w, g, m, v):
    m = ADAM_B1 * m + (1.0 - ADAM_B1) * g
    v = ADAM_B2 * v + (1.0 - ADAM_B2) * _jnp.square(g)
    m_hat = m / (1.0 - ADAM_B1 ** ADAM_STEP)
    v_hat = v / (1.0 - ADAM_B2 ** ADAM_STEP)
    delta = -ADAM_LR * (m_hat / (_jnp.sqrt(v_hat) + ADAM_EPS) + ADAM_WD * w)
    return delta, m, v


def reference(x, c, ctx, c_ctx, ada_w, ada_b, norm1_w, w_in, gdn_conv_w, gdn_conv_b, gdn_a_log, gdn_dt_bias, gdn_norm_w, ssm_conv_w, ssm_conv_b, ssm_a_log, ssm_dt_bias, ssm_d, ssm_norm_w, w_br_gdn, w_br_ssm, w_out, norm2_w, w_ffn_in, w_ffn_out, norm_f_w, loss_target, m_c_ctx, m_ada_w, m_ada_b, m_norm1_w, m_w_in, m_gdn_conv_w, m_gdn_conv_b, m_gdn_a_log, m_gdn_dt_bias, m_gdn_norm_w, m_ssm_conv_w, m_ssm_conv_b, m_ssm_a_log, m_ssm_dt_bias, m_ssm_d, m_ssm_norm_w, m_w_br_gdn, m_w_br_ssm, m_w_out, m_norm2_w, m_w_ffn_in, m_w_ffn_out, m_norm_f_w, v_c_ctx, v_ada_w, v_ada_b, v_norm1_w, v_w_in, v_gdn_conv_w, v_gdn_conv_b, v_gdn_a_log, v_gdn_dt_bias, v_gdn_norm_w, v_ssm_conv_w, v_ssm_conv_b, v_ssm_a_log, v_ssm_dt_bias, v_ssm_d, v_ssm_norm_w, v_w_br_gdn, v_w_br_ssm, v_w_out, v_norm2_w, v_w_ffn_in, v_w_ffn_out, v_norm_f_w):
    given = dict(x=x, c=c, ctx=ctx, c_ctx=c_ctx, ada_w=ada_w, ada_b=ada_b, norm1_w=norm1_w, w_in=w_in, gdn_conv_w=gdn_conv_w, gdn_conv_b=gdn_conv_b, gdn_a_log=gdn_a_log, gdn_dt_bias=gdn_dt_bias, gdn_norm_w=gdn_norm_w, ssm_conv_w=ssm_conv_w, ssm_conv_b=ssm_conv_b, ssm_a_log=ssm_a_log, ssm_dt_bias=ssm_dt_bias, ssm_d=ssm_d, ssm_norm_w=ssm_norm_w, w_br_gdn=w_br_gdn, w_br_ssm=w_br_ssm, w_out=w_out, norm2_w=norm2_w, w_ffn_in=w_ffn_in, w_ffn_out=w_ffn_out, norm_f_w=norm_f_w, loss_target=loss_target, m_c_ctx=m_c_ctx, m_ada_w=m_ada_w, m_ada_b=m_ada_b, m_norm1_w=m_norm1_w, m_w_in=m_w_in, m_gdn_conv_w=m_gdn_conv_w, m_gdn_conv_b=m_gdn_conv_b, m_gdn_a_log=m_gdn_a_log, m_gdn_dt_bias=m_gdn_dt_bias, m_gdn_norm_w=m_gdn_norm_w, m_ssm_conv_w=m_ssm_conv_w, m_ssm_conv_b=m_ssm_conv_b, m_ssm_a_log=m_ssm_a_log, m_ssm_dt_bias=m_ssm_dt_bias, m_ssm_d=m_ssm_d, m_ssm_norm_w=m_ssm_norm_w, m_w_br_gdn=m_w_br_gdn, m_w_br_ssm=m_w_br_ssm, m_w_out=m_w_out, m_norm2_w=m_norm2_w, m_w_ffn_in=m_w_ffn_in, m_w_ffn_out=m_w_ffn_out, m_norm_f_w=m_norm_f_w, v_c_ctx=v_c_ctx, v_ada_w=v_ada_w, v_ada_b=v_ada_b, v_norm1_w=v_norm1_w, v_w_in=v_w_in, v_gdn_conv_w=v_gdn_conv_w, v_gdn_conv_b=v_gdn_conv_b, v_gdn_a_log=v_gdn_a_log, v_gdn_dt_bias=v_gdn_dt_bias, v_gdn_norm_w=v_gdn_norm_w, v_ssm_conv_w=v_ssm_conv_w, v_ssm_conv_b=v_ssm_conv_b, v_ssm_a_log=v_ssm_a_log, v_ssm_dt_bias=v_ssm_dt_bias, v_ssm_d=v_ssm_d, v_ssm_norm_w=v_ssm_norm_w, v_w_br_gdn=v_w_br_gdn, v_w_br_ssm=v_w_br_ssm, v_w_out=v_w_out, v_norm2_w=v_norm2_w, v_w_ffn_in=v_w_ffn_in, v_w_ffn_out=v_w_ffn_out, v_norm_f_w=v_norm_f_w)
    weights = {n: given[n] for n in TWIN_WEIGHTS}
    shared = {n: given[n] for n in SHARED_INPUTS}
    per_example = {n: given[n] for n in ['x', 'c', 'ctx']}
    grad_fn = _jax.value_and_grad(_loss, argnums=(0, 1))

    def one_microbatch(ex, loss_target):
        ex = dict(ex)
        diff = ex.pop(TWIN_DIFF_INPUT)
        return grad_fn(weights, diff, {**shared, **ex}, loss_target)

    if N_MICROBATCH == 1:
        loss, (grad_w, grad_x) = one_microbatch(per_example, given["loss_target"])
    else:
        def body(carry, xs):
            loss_sum, grad_sum = carry
            l_k, (gw_k, gx_k) = one_microbatch(xs[0], xs[1])
            with _jax.named_scope("update"):
                return (loss_sum + l_k, _jax.tree.map(_jnp.add, grad_sum, gw_k)), gx_k

        init = (_jnp.zeros((), _jnp.float32), _jax.tree.map(_jnp.zeros_like, weights))
        (loss, grad_w), grad_x = _jax.lax.scan(body, init, (per_example, given["loss_target"]))
    with _jax.named_scope("update"):
        delta_w, new_m, new_v = {}, {}, {}
        for n in TWIN_WEIGHTS:
            delta_w[n], new_m[n], new_v[n] = _adamw(weights[n], grad_w[n], given["m_" + n], given["v_" + n])
    return (loss, grad_x, *[grad_w[n] for n in TWIN_WEIGHTS], *[delta_w[n] for n in TWIN_WEIGHTS],
            *[new_m[n] for n in TWIN_WEIGHTS], *[new_v[n] for n in TWIN_WEIGHTS])
```

```python
import functools

import jax
import jax.numpy as jnp
from jax import lax
from jax.experimental import pallas as pl
from jax.experimental.pallas import tpu as pltpu

F32 = jnp.float32
BF16 = jnp.bfloat16
HI = lax.Precision.HIGHEST
MESH = pl.DeviceIdType.MESH

D = 1024
CH = 64
TM = 256
EPS = 1e-6
NEG = -1e30
G_HEADS = 8
DK = 128
S_HEADS = 32
S_P = 64
S_GROUPS = 4
S_N = 128
S_INNER = 2048
XBC = 3072
D_FF = 2816
N_DEV = 8
PACK_W = 1024
VMEM_LIMIT = 56 * 1024 * 1024

ADAM_LR = 0.001
ADAM_B1 = 0.9
ADAM_B2 = 0.999
ADAM_EPS = 1e-08
ADAM_WD = 0.01
ADAM_STEP = 10

C_QKV, C_XBC, C_ZS, C_GATE, C_ZG, C_AB, C_DT, C_END = 0, 3072, 6144, 8192, 10240, 11264, 11392, 11520
O_QKV, O_ZG, O_AB, O_ZS, O_XBC, O_DT, O_GATE, O_END = 0, 3072, 4096, 4128, 6176, 9248, 9312, 11360


def _dot(a, b, prec=None):
    return jnp.dot(a, b, precision=prec, preferred_element_type=F32)


def _dot_nt(a, b, prec=None):
    return lax.dot_general(a, b, (((1,), (1,)), ((), ())), precision=prec, preferred_element_type=F32)


def _dot_tn(a, b, prec=None):
    return lax.dot_general(a, b, (((0,), (0,)), ((), ())), precision=prec, preferred_element_type=F32)


def _iota(shape, dim):
    return lax.broadcasted_iota(jnp.int32, shape, dim)


def _rms(x):
    return x * lax.rsqrt(jnp.mean(x * x, axis=-1, keepdims=True) + EPS)


def _l2n(x):
    return x * lax.rsqrt(jnp.sum(x * x, axis=-1, keepdims=True) + EPS)


def _silu(x):
    return x * jax.nn.sigmoid(x)


def _softplus(x):
    return jnp.maximum(x, 0.0) + jnp.log1p(jnp.exp(-jnp.abs(x)))


def _roll_rows(x, s):
    return pltpu.roll(x, s, 0)


def _up_raw(x, keep_up):
    return jnp.where(keep_up > 0.0, _roll_rows(x, 1), 0.0)


def _dn_raw(x, keep_dn):
    return jnp.where(keep_dn > 0.0, _roll_rows(x, x.shape[0] - 1), 0.0)


@jax.custom_vjp
def _shift_up(x, keep_up, keep_dn):
    return _up_raw(x, keep_up)


def _shift_up_fwd(x, keep_up, keep_dn):
    return _up_raw(x, keep_up), (keep_up, keep_dn)


def _shift_up_bwd(res, g):
    keep_up, keep_dn = res
    return _dn_raw(g, keep_dn), jnp.zeros_like(keep_up), jnp.zeros_like(keep_dn)


_shift_up.defvjp(_shift_up_fwd, _shift_up_bwd)


@jax.custom_vjp
def _shift_dn(x, keep_up, keep_dn):
    return _dn_raw(x, keep_dn)


def _shift_dn_fwd(x, keep_up, keep_dn):
    return _dn_raw(x, keep_dn), (keep_up, keep_dn)


def _shift_dn_bwd(res, g):
    keep_up, keep_dn = res
    return _up_raw(g, keep_up), jnp.zeros_like(keep_up), jnp.zeros_like(keep_dn)


_shift_dn.defvjp(_shift_dn_fwd, _shift_dn_bwd)


def _conv_keep(is_ctx, n):
    r = _iota((n, 1), 0)
    pos = jnp.where(is_ctx, r, r & (CH - 1))
    end = jnp.where(is_ctx, n - 1, CH - 1)
    return jnp.where(pos == 0, 0.0, 1.0).astype(F32), jnp.where(pos == end, 0.0, 1.0).astype(F32)


def _conv_silu(u, w3, b, keep_up, keep_dn):
    conv = b + _shift_up(u, keep_up, keep_dn) * w3[0] + u * w3[1] + _shift_dn(u, keep_up, keep_dn) * w3[2]
    return _silu(conv)


def _chunk_tri(n, rev):
    i = _iota((n, n), 0)
    j = _iota((n, n), 1)
    same = (i // CH) == (j // CH)
    seen = (i <= j) if rev else (i >= j)
    return jnp.where(same & seen, 1.0, 0.0).astype(F32)


def _expand_mat(rows, cols, per, base):
    r = _iota((rows, cols), 0)
    c = _iota((rows, cols), 1)
    return jnp.where(r == base + c // per, 1.0, 0.0).astype(F32)


def f_silu_rows(is_ctx, cvec):
    return (_silu(cvec).astype(BF16),)


def f_pre(is_ctx, x, n1w, sc, sh, csc, csh):
    sc_e = jnp.where(is_ctx, csc, sc)
    sh_e = jnp.where(is_ctx, csh, sh)
    a = _rms(x) * n1w * (1.0 + sc_e) + sh_e
    return (a.astype(BF16),)


def f_pre_thru(is_ctx, x, n1w, sc, sh, csc, csh):
    return f_pre(is_ctx, x, n1w, sc, sh, csc, csh)[0], x


def f_gdnprep(is_ctx, qkv_raw, ab_raw, cw, cb, alog, dtb):
    n = qkv_raw.shape[0]
    keep_up, keep_dn = _conv_keep(is_ctx, n)
    s = _conv_silu(qkv_raw, cw, cb, keep_up, keep_dn)
    qs, ks, vs = [], [], []
    for h in range(G_HEADS):
        qs.append(_l2n(s[:, h * DK:(h + 1) * DK]) * (DK ** -0.5))
        ks.append(_l2n(s[:, D + h * DK:D + (h + 1) * DK]))
    q = jnp.concatenate(qs, axis=1)
    k = jnp.concatenate(ks, axis=1)
    v = s[:, 2 * D:3 * D]
    lane = _iota(ab_raw.shape, 1)
    g = jnp.where(lane < 2 * G_HEADS, -jnp.exp(alog) * _softplus(ab_raw + dtb), 0.0)
    gcum = jnp.where(lane < G_HEADS, _dot(_chunk_tri(n, False), g, HI), _dot(_chunk_tri(n, True), g, HI))
    beta = jax.nn.sigmoid(ab_raw)
    return q, k, v, gcum, beta


def f_ssmprep(is_ctx, xbc_raw, dt_raw, cw, cb, alog, dtb):
    n = xbc_raw.shape[0]
    keep_up, keep_dn = _conv_keep(is_ctx, n)
    s = _conv_silu(xbc_raw, cw, cb, keep_up, keep_dn)
    xs = s[:, :S_INNER]
    bm = s[:, S_INNER:S_INNER + S_GROUPS * S_N]
    cm = s[:, S_INNER + S_GROUPS * S_N:]
    lane = _iota(dt_raw.shape, 1)
    dt = jnp.where(lane < 2 * S_HEADS, _softplus(dt_raw + dtb), 0.0)
    da = dt * (-jnp.exp(alog))
    acum = jnp.where(lane < S_HEADS, _dot(_chunk_tri(n, False), da, HI), _dot(_chunk_tri(n, True), da, HI))
    outs = [xs, bm, cm]
    for d in range(2):
        e = _expand_mat(128, S_INNER, S_P, d * S_HEADS)
        outs.append(_dot(dt, e, HI))
        outs.append(_dot(acum, e, HI))
    return tuple(outs)


def f_post(is_ctx, o_f, o_b, zg, y_f, y_b, xs, zs, gnw, ssd8, snw):
    o = o_f + o_b
    ogs = []
    for h in range(G_HEADS):
        sl = slice(h * DK, (h + 1) * DK)
        ogs.append(_rms(o[:, sl]) * gnw * _silu(zg[:, sl]))
    og = jnp.concatenate(ogs, axis=1)
    row0 = jnp.where(_iota(ssd8.shape, 0) == 0, 1.0, 0.0).astype(F32)
    dexp = jnp.sum(_dot(ssd8 * row0, _expand_mat(128, S_INNER, S_P, 0), HI), axis=0, keepdims=True)
    y = (y_f + y_b + dexp * xs) * _silu(zs)
    gw = S_INNER // S_GROUPS
    ys = jnp.concatenate([_rms(y[:, i * gw:(i + 1) * gw]) * snw[:, i * gw:(i + 1) * gw] for i in range(S_GROUPS)], axis=1)
    return og.astype(BF16), ys.astype(BF16)


def f_merge(is_ctx, gate, pg, ps):
    m = jax.nn.sigmoid(gate[:, :D]) * pg + jax.nn.sigmoid(gate[:, D:]) * ps
    return (m.astype(BF16),)


def f_res1(is_ctx, x, mix, g1, n2w, sc2, sh2):
    h1 = x + g1 * mix
    f = _rms(h1) * n2w * (1.0 + sc2) + sh2
    return h1, f.astype(BF16)


def f_act(is_ctx, u):
    return ((_silu(u[:, :D_FF]) * u[:, D_FF:]).astype(BF16),)


def f_final(is_ctx, h1, ff, tgt, g2, nfw):
    h2 = h1 + g2 * ff
    y = _rms(h2) * nfw
    err = y - tgt
    return (0.5 * jnp.sum(jnp.mean(err * err, axis=-1, keepdims=True), axis=0, keepdims=True),)


def _tri_inverse(a):
    n = a.shape[0]
    eye = jnp.where(_iota((n, n), 0) == _iota((n, n), 1), 1.0, 0.0).astype(F32)
    t = eye - a
    p = _dot(a, a, HI)
    for r in range(5):
        t = t + _dot(t, p, HI)
        if r < 4:
            p = _dot(p, p, HI)
    return t


def gdn_step(s, q, k, v, gt, bt, col_g, col_b, rev):
    c = q.shape[0]
    oh = lambda col, w: jnp.where(_iota((128, w), 0) == col, 1.0, 0.0).astype(F32)
    gc = _dot(gt, oh(col_g, 128), HI)
    gc_s = _dot(gt, oh(col_g, c), HI)
    bc = _dot(bt, oh(col_b, 128), HI)
    sel = jnp.where(_iota((c, 128), 1) == 0, 1.0, 0.0).astype(F32)
    gr = _dot_nt(sel, gc, HI)
    ii = _iota((c, c), 0)
    jj = _iota((c, c), 1)
    incl = (ii <= jj) if rev else (ii >= jj)
    strict = (ii < jj) if rev else (ii > jj)
    decay = jnp.exp(jnp.where(incl, gc_s - gr, NEG))
    kb = k * bc
    a = jnp.where(strict, _dot_nt(kb, k) * decay, 0.0)
    t = _tri_inverse(a)
    eg = jnp.exp(gc)
    w = _dot(t, kb * eg, HI)
    u = _dot(t, v * bc, HI)
    v_new = u - _dot(w, s)
    attn = _dot_nt(q, k) * decay
    o = _dot(q * eg, s) + _dot(attn, v_new)
    last = 0 if rev else c - 1
    gtot = jnp.sum(jnp.where(_iota((c, 128), 0) == last, gc, 0.0), axis=0, keepdims=True)
    s_new = s * jnp.exp(gtot) + _dot_tn(k * jnp.exp(gtot - gc), v_new)
    return s_new, o


def ssd_step(ht, x, dtx, acx, bg, cg, rev):
    c = x.shape[0]
    lane = _iota((c, 128), 1)
    ii = _iota((c, 128), 0)
    jl = lane & (S_P - 1)
    lo = lane < S_P
    seen = (ii <= jl) if rev else (ii >= jl)
    m = jnp.concatenate([jnp.where(lo, acx, 0.0), jnp.where(lo, 0.0, acx)], axis=0)
    sel = jnp.where((lane == 0) | (lane == S_P), 1.0, 0.0).astype(F32)
    acr = _dot_nt(sel, m, HI)
    seg = jnp.exp(jnp.where(seen, acx - acr, NEG))
    cb = _dot_nt(cg, jnp.concatenate([bg, bg], axis=0))
    xdt = x * dtx
    xbd = jnp.concatenate([jnp.where(lo, xdt, 0.0), jnp.where(lo, 0.0, xdt)], axis=0)
    y = _dot(cb * seg, xbd) + _dot(cg, ht) * jnp.exp(acx)
    last = 0 if rev else c - 1
    atot = jnp.sum(jnp.where(ii == last, acx, 0.0), axis=0, keepdims=True)
    ht_new = ht * jnp.exp(atot) + _dot_tn(bg, xdt * jnp.exp(atot - acx))
    return ht_new, y


def f_adamw(w, g, m, v):
    m = ADAM_B1 * m + (1.0 - ADAM_B1) * g
    v = ADAM_B2 * v + (1.0 - ADAM_B2) * jnp.square(g)
    m_hat = m / (1.0 - ADAM_B1 ** ADAM_STEP)
    v_hat = v / (1.0 - ADAM_B2 ** ADAM_STEP)
    delta = -ADAM_LR * (m_hat / (jnp.sqrt(v_hat) + ADAM_EPS) + ADAM_WD * w)
    return delta, m, v


def _cparams(sem):
    return pltpu.CompilerParams(dimension_semantics=sem, vmem_limit_bytes=VMEM_LIMIT)


def _pick(n, target):
    if n <= target:
        return n
    best = None
    for t in range(128, target + 1, 128):
        if n % t == 0:
            best = t
    assert best is not None, (n, target)
    return best


def _row_spec(tm, width, colblk, rowoff):
    return pl.BlockSpec((tm, width), lambda i: (i + rowoff, colblk))


def _par_spec(shape):
    nd = len(shape)
    return pl.BlockSpec(tuple(shape), lambda i: (0,) * nd)


def _rowwise(name, fn, rows, pars, outs, ntiles, base=1, tm=TM):
    nr, npar = len(rows), len(pars)

    def body(*refs):
        is_ctx = (pl.program_id(0) + base) == 0
        res = fn(is_ctx, *[r[...] for r in refs[:nr]], *[p[...] for p in refs[nr:nr + npar]])
        for o_ref, r in zip(refs[nr + npar:], res):
            o_ref[...] = r.astype(o_ref.dtype)

    return pl.pallas_call(
        body, grid=(ntiles,), name=name,
        in_specs=[_row_spec(tm, wd, cb, ro) for (_, wd, cb, ro) in rows] + [_par_spec(p.shape) for p in pars],
        out_specs=[_row_spec(tm, wd, 0, 0) for (wd, _) in outs],
        out_shape=[jax.ShapeDtypeStruct((ntiles * tm, wd), dt) for (wd, dt) in outs],
        compiler_params=_cparams(("arbitrary",)),
    )(*[r[0] for r in rows], *pars)


def _ct_spec(tm, desc):
    _, wd, cb, ro = desc[:4]
    if len(desc) > 4 and desc[4]:
        return pl.BlockSpec((tm, wd), lambda i: (jnp.maximum(i + ro, 0), cb))
    return _row_spec(tm, wd, cb, ro)


def _rowwise_bwd(name, fn, rows, pars, cts, drows, dpars, ntiles, base=1, loss_out=False, tm=TM):
    nr, npar = len(rows), len(pars)
    ct_rows = [d for ct in cts if isinstance(ct, list) for d in ct]
    nct = len(ct_rows)

    def body(*refs):
        i = pl.program_id(0)
        is_ctx = (i + base) == 0
        rows_v = [r[...] for r in refs[:nr]]
        pars_v = [p[...] for p in refs[nr:nr + npar]]
        ct_refs = list(refs[nr + npar:nr + npar + nct])
        out_refs = list(refs[nr + npar + nct:])
        outs, vjp = jax.vjp(lambda rv, pv: fn(is_ctx, *rv, *pv), rows_v, pars_v)

        def ct_value(desc):
            val = ct_refs.pop(0)[...].astype(F32)
            if len(desc) > 4 and desc[4]:
                val = jnp.where(is_ctx, 0.0, val)
            return val

        ct_vals = []
        for o, ct in zip(outs, cts):
            if ct is None:
                ct_vals.append(jnp.zeros_like(o))
            elif isinstance(ct, str):
                ct_vals.append(jnp.ones_like(o))
            else:
                acc = ct_value(ct[0])
                for desc in ct[1:]:
                    acc = acc + ct_value(desc)
                ct_vals.append(acc.astype(o.dtype))
        d_rows, d_pars = vjp(tuple(ct_vals))
        for (ri, _), o_ref in zip(drows, out_refs[:len(drows)]):
            o_ref[...] = d_rows[ri].astype(o_ref.dtype)
        acc_refs = out_refs[len(drows):]
        acc_vals = [d_pars[pi] for pi in dpars]
        if loss_out:
            acc_vals.append(jnp.broadcast_to(outs[0], (8, 128)))

        @pl.when(i == 0)
        def _():
            for o_ref, val in zip(acc_refs, acc_vals):
                o_ref[...] = val

        @pl.when(i > 0)
        def _():
            for o_ref, val in zip(acc_refs, acc_vals):
                o_ref[...] += val

    acc_shapes = [pars[pi].shape for pi in dpars] + ([(8, 128)] if loss_out else [])
    return pl.pallas_call(
        body, grid=(ntiles,), name=name,
        in_specs=[_row_spec(tm, wd, cb, ro) for (_, wd, cb, ro) in rows] + [_par_spec(p.shape) for p in pars]
        + [_ct_spec(tm, d) for d in ct_rows],
        out_specs=[_row_spec(tm, rows[ri][1], 0, 0) for (ri, _) in drows] + [_par_spec(s) for s in acc_shapes],
        out_shape=[jax.ShapeDtypeStruct((ntiles * tm, rows[ri][1]), dt) for (ri, dt) in drows]
        + [jax.ShapeDtypeStruct(tuple(s), F32) for s in acc_shapes],
        compiler_params=_cparams(("arbitrary",)),
    )(*[r[0] for r in rows], *pars, *[r[0] for r in ct_rows])


def _mm(name, a, b, mode, out_dtype, tm=768, tn=1024, tk=1024):
    if mode == "nn":
        (m, kd), (_, n) = a.shape, b.shape
    elif mode == "nt":
        (m, kd), (n, _) = a.shape, b.shape
    else:
        (kd, m), (_, n) = a.shape, b.shape
    tm, tn, tk = _pick(m, tm), _pick(n, tn), _pick(kd, tk)
    nk = kd // tk
    a_spec = {"nn": pl.BlockSpec((tm, tk), lambda i, j, k: (i, k)), "nt": pl.BlockSpec((tm, tk), lambda i, j, k: (i, k)),
              "tn": pl.BlockSpec((tk, tm), lambda i, j, k: (k, i))}[mode]
    b_spec = {"nn": pl.BlockSpec((tk, tn), lambda i, j, k: (k, j)), "nt": pl.BlockSpec((tn, tk), lambda i, j, k: (j, k)),
              "tn": pl.BlockSpec((tk, tn), lambda i, j, k: (k, j))}[mode]
    dot = {"nn": _dot, "nt": _dot_nt, "tn": _dot_tn}[mode]

    def body(a_ref, b_ref, o_ref, acc_ref):
        k = pl.program_id(2)
        part = dot(a_ref[...].astype(BF16), b_ref[...].astype(BF16))

        @pl.when(k == 0)
        def _():
            acc_ref[...] = part

        @pl.when(k > 0)
        def _():
            acc_ref[...] += part

        @pl.when(k == nk - 1)
        def _():
            o_ref[...] = acc_ref[...].astype(o_ref.dtype)

    return pl.pallas_call(
        body, grid=(m // tm, n // tn, nk), name=name,
        in_specs=[a_spec, b_spec],
        out_specs=pl.BlockSpec((tm, tn), lambda i, j, k: (i, j)),
        out_shape=jax.ShapeDtypeStruct((m, n), out_dtype),
        scratch_shapes=[pltpu.VMEM((tm, tn), F32)],
        compiler_params=_cparams(("arbitrary", "arbitrary", "arbitrary")),
    )(a, b)


def _chunk_index(i, nch, nctx, rev):
    if not rev:
        return i
    return jnp.where(i < nctx, nctx - 1 - i, nch + nctx - 1 - i)


def _gdn_scan_fwd(name, q, k, v, gcum, beta, d, nctx):
    t = q.shape[0]
    nch = t // CH
    rev = d == 1
    cix = lambda i: _chunk_index(i, nch, nctx, rev)
    head = pl.BlockSpec((CH, DK), lambda i, h: (cix(i), h))
    wide = pl.BlockSpec((CH, 128), lambda i, h: (cix(i), 0))

    def body(q_ref, k_ref, v_ref, g_ref, b_ref, o_ref, ss_ref, s_scr):
        i, h = pl.program_id(0), pl.program_id(1)

        @pl.when(i == 0)
        def _():
            s_scr[h] = jnp.zeros((DK, DK), F32)

        s = s_scr[h]
        ss_ref[0, 0] = s
        s_new, o = gdn_step(s, q_ref[...], k_ref[...], v_ref[...], g_ref[...], b_ref[...],
                            d * G_HEADS + h, 2 * G_HEADS + d * G_HEADS + h, rev)
        s_scr[h] = s_new
        o_ref[...] = o

    return pl.pallas_call(
        body, grid=(nch, G_HEADS), name=name,
        in_specs=[head, head, head, wide, wide],
        out_specs=[head, pl.BlockSpec((1, 1, DK, DK), lambda i, h: (i, h, 0, 0))],
        out_shape=[jax.ShapeDtypeStruct((t, D), F32), jax.ShapeDtypeStruct((nch, G_HEADS, DK, DK), F32)],
        scratch_shapes=[pltpu.VMEM((G_HEADS, DK, DK), F32)],
        compiler_params=_cparams(("arbitrary", "arbitrary")),
    )(q, k, v, gcum, beta)


def _gdn_scan_bwd(name, q, k, v, gcum, beta, ssave, do, d, nctx):
    t = q.shape[0]
    nch = t // CH
    rev = d == 1
    cix = lambda ib: _chunk_index(nch - 1 - ib, nch, nctx, rev)
    head = pl.BlockSpec((CH, DK), lambda ib, h: (cix(ib), h))
    wide = pl.BlockSpec((CH, 128), lambda ib, h: (cix(ib), 0))
    do_spec = pl.BlockSpec((CH, DK), lambda ib, h: (jnp.maximum(cix(ib), nctx) - nctx, h))

    def body(q_ref, k_ref, v_ref, g_ref, b_ref, ss_ref, do_ref, dq_ref, dk_ref, dv_ref, dg_ref, db_ref, ds_scr):
        ib, h = pl.program_id(0), pl.program_id(1)

        @pl.when(ib == 0)
        def _():
            ds_scr[h] = jnp.zeros((DK, DK), F32)

        col_g = d * G_HEADS + h
        col_b = 2 * G_HEADS + d * G_HEADS + h
        do_v = jnp.where(cix(ib) >= nctx, do_ref[...], 0.0)
        _, vjp = jax.vjp(lambda s, q_, k_, v_, g_, b_: gdn_step(s, q_, k_, v_, g_, b_, col_g, col_b, rev),
                         ss_ref[0, 0], q_ref[...], k_ref[...], v_ref[...], g_ref[...], b_ref[...])
        ds, dq, dk, dv, dg, db = vjp((ds_scr[h], do_v))
        ds_scr[h] = ds
        dq_ref[...] = dq
        dk_ref[...] = dk
        dv_ref[...] = dv

        @pl.when(h == 0)
        def _():
            dg_ref[...] = dg
            db_ref[...] = db

        @pl.when(h > 0)
        def _():
            dg_ref[...] += dg
            db_ref[...] += db

    return pl.pallas_call(
        body, grid=(nch, G_HEADS), name=name,
        in_specs=[head, head, head, wide, wide,
                  pl.BlockSpec((1, 1, DK, DK), lambda ib, h: (nch - 1 - ib, h, 0, 0)), do_spec],
        out_specs=[head, head, head, wide, wide],
        out_shape=[jax.ShapeDtypeStruct((t, D), F32)] * 3 + [jax.ShapeDtypeStruct((t, 128), F32)] * 2,
        scratch_shapes=[pltpu.VMEM((G_HEADS, DK, DK), F32)],
        compiler_params=_cparams(("arbitrary", "arbitrary")),
    )(q, k, v, gcum, beta, ssave, do)


N_PAIRS = S_HEADS // 2
PAIRS_PER_GROUP = N_PAIRS // S_GROUPS


def _ssd_scan_fwd(name, xs, dtx, acx, bm, cm, d, nctx):
    t = xs.shape[0]
    nch = t // CH
    rev = d == 1
    cix = lambda i: _chunk_index(i, nch, nctx, rev)
    pair = pl.BlockSpec((CH, 128), lambda i, p: (cix(i), p))
    grp = pl.BlockSpec((CH, S_N), lambda i, p: (cix(i), p // PAIRS_PER_GROUP))

    def body(x_ref, dt_ref, ac_ref, b_ref, c_ref, y_ref, hs_ref, h_scr):
        i, p = pl.program_id(0), pl.program_id(1)

        @pl.when(i == 0)
        def _():
            h_scr[p] = jnp.zeros((S_N, 128), F32)

        ht = h_scr[p]
        hs_ref[0, 0] = ht
        h_new, y = ssd_step(ht, x_ref[...], dt_ref[...], ac_ref[...], b_ref[...], c_ref[...], rev)
        h_scr[p] = h_new
        y_ref[...] = y

    return pl.pallas_call(
        body, grid=(nch, N_PAIRS), name=name,
        in_specs=[pair, pair, pair, grp, grp],
        out_specs=[pair, pl.BlockSpec((1, 1, S_N, 128), lambda i, p: (i, p, 0, 0))],
        out_shape=[jax.ShapeDtypeStruct((t, S_INNER), F32), jax.ShapeDtypeStruct((nch, N_PAIRS, S_N, 128), F32)],
        scratch_shapes=[pltpu.VMEM((N_PAIRS, S_N, 128), F32)],
        compiler_params=_cparams(("arbitrary", "arbitrary")),
    )(xs, dtx, acx, bm, cm)


def _ssd_scan_bwd(name, xs, dtx, acx, bm, cm, hsave, dy, d, nctx):
    t = xs.shape[0]
    nch = t // CH
    rev = d == 1
    cix = lambda ib: _chunk_index(nch - 1 - ib, nch, nctx, rev)
    pair = pl.BlockSpec((CH, 128), lambda ib, p: (cix(ib), p))
    grp = pl.BlockSpec((CH, S_N), lambda ib, p: (cix(ib), p // PAIRS_PER_GROUP))
    dy_spec = pl.BlockSpec((CH, 128), lambda ib, p: (jnp.maximum(cix(ib), nctx) - nctx, p))

    def body(x_ref, dt_ref, ac_ref, b_ref, c_ref, hs_ref, dy_ref, dx_ref, ddt_ref, dac_ref, db_ref, dc_ref, dh_scr):
        ib, p = pl.program_id(0), pl.program_id(1)

        @pl.when(ib == 0)
        def _():
            dh_scr[p] = jnp.zeros((S_N, 128), F32)

        dy_v = jnp.where(cix(ib) >= nctx, dy_ref[...], 0.0)
        _, vjp = jax.vjp(lambda ht, x_, dt_, ac_, b_, c_: ssd_step(ht, x_, dt_, ac_, b_, c_, rev),
                         hs_ref[0, 0], x_ref[...], dt_ref[...], ac_ref[...], b_ref[...], c_ref[...])
        dh, dx, ddt, dac, db, dc = vjp((dh_scr[p], dy_v))
        dh_scr[p] = dh
        dx_ref[...] = dx
        ddt_ref[...] = ddt
        dac_ref[...] = dac
        first = (p % PAIRS_PER_GROUP) == 0

        @pl.when(first)
        def _():
            db_ref[...] = db
            dc_ref[...] = dc

        @pl.when(jnp.logical_not(first))
        def _():
            db_ref[...] += db
            dc_ref[...] += dc

    return pl.pallas_call(
        body, grid=(nch, N_PAIRS), name=name,
        in_specs=[pair, pair, pair, grp, grp,
                  pl.BlockSpec((1, 1, S_N, 128), lambda ib, p: (nch - 1 - ib, p, 0, 0)), dy_spec],
        out_specs=[pair, pair, pair, grp, grp],
        out_shape=[jax.ShapeDtypeStruct((t, S_INNER), F32)] * 3 + [jax.ShapeDtypeStruct((t, S_GROUPS * S_N), F32)] * 2,
        scratch_shapes=[pltpu.VMEM((N_PAIRS, S_N, 128), F32)],
        compiler_params=_cparams(("arbitrary", "arbitrary")),
    )(xs, dtx, acx, bm, cm, hsave, dy)


def _mesh_pos():
    return lax.axis_index("x"), lax.axis_index("y"), lax.axis_index("c")


def _all_gather(name, xs):
    r, cdim = xs.shape

    def body(x_ref, out_ref, send_sems, recv_sems, local_sem):
        x, y, c = _mesh_pos()
        me, sibling = (x, y, c), (x, y, 1 - c)
        chips = [(1 - x, y), (x, 1 - y), (1 - x, 1 - y)]

        def slot(px, py, pc):
            return out_ref.at[4 * px + 2 * py + pc]

        def copy(k, block, to, src=None):
            return pltpu.make_async_remote_copy(
                src_ref=slot(*block) if src is None else src, dst_ref=slot(*block),
                send_sem=send_sems.at[k], recv_sem=recv_sems.at[k], device_id=to, device_id_type=MESH)

        mine = pltpu.make_async_copy(x_ref, slot(*me), local_sem)
        mine.start()
        first = [copy(0, me, sibling, src=x_ref)]
        first += [copy(1 + j, me, (*chip, c), src=x_ref) for j, chip in enumerate(chips)]
        for cp in first:
            cp.start()
        passed = [copy(4 + j, (*chip, c), sibling) for j, chip in enumerate(chips)]
        for j, chip in enumerate(chips):
            copy(1 + j, (*chip, c), me).wait_recv()
            passed[j].start()
        copy(0, sibling, me).wait_recv()
        for j, chip in enumerate(chips):
            copy(4 + j, (*chip, 1 - c), me).wait_recv()
        for cp in first + passed:
            cp.wait_send()
        mine.wait()

    return pl.pallas_call(
        body, name=name,
        out_shape=jax.ShapeDtypeStruct((N_DEV, r, cdim), xs.dtype),
        in_specs=[pl.BlockSpec(memory_space=pl.ANY)],
        out_specs=pl.BlockSpec(memory_space=pl.ANY),
        scratch_shapes=[pltpu.SemaphoreType.DMA((7,)), pltpu.SemaphoreType.DMA((7,)), pltpu.SemaphoreType.DMA(())],
    )(xs)


def _all_to_all(name, g):
    def body(g_ref, out_ref, send_sems, recv_sems, local_sem):
        x, y, c = _mesh_pos()
        me_i = 4 * x + 2 * y + c
        mine = pltpu.make_async_copy(g_ref.at[me_i], out_ref.at[me_i], local_sem)
        mine.start()
        cps = []
        for k in range(1, N_DEV):
            px = 1 - x if (k >> 2) & 1 else x
            py = 1 - y if (k >> 1) & 1 else y
            pc = 1 - c if k & 1 else c
            cp = pltpu.make_async_remote_copy(
                src_ref=g_ref.at[4 * px + 2 * py + pc], dst_ref=out_ref.at[me_i],
                send_sem=send_sems.at[k - 1], recv_sem=recv_sems.at[k - 1],
                device_id=(px, py, pc), device_id_type=MESH)
            cp.start()
            cps.append(cp)
        for cp in cps:
            cp.wait()
        mine.wait()

    return pl.pallas_call(
        body, name=name,
        out_shape=jax.ShapeDtypeStruct(g.shape, g.dtype),
        in_specs=[pl.BlockSpec(memory_space=pl.ANY)],
        out_specs=pl.BlockSpec(memory_space=pl.ANY),
        scratch_shapes=[pltpu.SemaphoreType.DMA((7,)), pltpu.SemaphoreType.DMA((7,)), pltpu.SemaphoreType.DMA(())],
    )(g)


def _reduce_adam(name, recv, w, m, v, tm):
    rows, width = w.shape

    def body(recv_ref, w_ref, m_ref, v_ref, g_ref, d_ref, m2_ref, v2_ref):
        g = recv_ref[0]
        for s in range(1, N_DEV):
            g = g + recv_ref[s]
        delta, m2, v2 = f_adamw(w_ref[...], g, m_ref[...], v_ref[...])
        g_ref[...] = g
        d_ref[...] = delta
        m2_ref[...] = m2
        v2_ref[...] = v2

    row = pl.BlockSpec((tm, width), lambda i: (i, 0))
    return pl.pallas_call(
        body, grid=(rows // tm,), name=name,
        in_specs=[pl.BlockSpec((N_DEV, tm, width), lambda i: (0, i, 0)), row, row, row],
        out_specs=[row] * 4,
        out_shape=[jax.ShapeDtypeStruct((rows, width), F32)] * 4,
        compiler_params=_cparams(("arbitrary",)),
    )(recv, w, m, v)


BIG = ("w_in", "ada_w", "w_br_gdn", "w_br_ssm", "w_out", "w_ffn_in", "w_ffn_out", "gdn_conv_w", "ssm_conv_w")
BIG_ROWS = (1420, 768, 128, 256, 128, 704, 352, 2, 2)
BIG_SHARD_SHAPES = ((1024, 1420), (1024, 768), (128, 1024), (256, 1024), (128, 1024), (1024, 704), (352, 1024),
                    (3, 384), (3, 384))
COL_SHARDED = ("w_in", "ada_w", "w_ffn_in", "gdn_conv_w", "ssm_conv_w")
PACK_ROWS = sum(BIG_ROWS)
SMALL = ("c_ctx", "ada_b", "norm1_w", "gdn_conv_b", "gdn_a_log", "gdn_dt_bias", "gdn_norm_w", "ssm_conv_b",
         "ssm_a_log", "ssm_dt_bias", "ssm_d", "ssm_norm_w", "norm2_w", "norm_f_w")


def _to_rows(a):
    flat = a.reshape(-1)
    pad = (-flat.shape[0]) % PACK_W
    if pad:
        flat = jnp.pad(flat, (0, pad))
    return flat.reshape(-1, PACK_W)


def _pack(arrays):
    return jnp.concatenate([_to_rows(a) for a in arrays], axis=0)


def _unpack(buf, shapes):
    out, r0 = [], 0
    for shp in shapes:
        n = 1
        for s in shp:
            n *= s
        nr = -(-n // PACK_W)
        out.append(buf[r0:r0 + nr].reshape(-1)[:n].reshape(shp))
        r0 += nr
    return out


def _full_from_gathered(blocks, name, shard_shape):
    n = shard_shape[0] * shard_shape[1]
    sh = blocks.reshape(N_DEV, -1)[:, :n].reshape((N_DEV,) + shard_shape)
    if name in COL_SHARDED:
        return jnp.transpose(sh, (1, 0, 2)).reshape(shard_shape[0], N_DEV * shard_shape[1])
    return sh.reshape(N_DEV * shard_shape[0], shard_shape[1])


def _blocks_from_full(full, name, shard_shape):
    if name in COL_SHARDED:
        sh = jnp.transpose(full.reshape(shard_shape[0], N_DEV, shard_shape[1]), (1, 0, 2))
    else:
        sh = full.reshape((N_DEV,) + shard_shape)
    flat = sh.reshape(N_DEV, -1)
    pad = (-flat.shape[1]) % PACK_W
    if pad:
        flat = jnp.pad(flat, ((0, 0), (0, pad)))
    return flat.reshape(N_DEV, -1, PACK_W)


def _pad_cols(a, n):
    return jnp.pad(a, ((0, 0), (0, n - a.shape[1])))


def _w_cat(w_in):
    return jnp.concatenate([
        w_in[:, O_QKV:O_ZG], w_in[:, O_XBC:O_DT], w_in[:, O_ZS:O_XBC], w_in[:, O_GATE:O_END], w_in[:, O_ZG:O_AB],
        _pad_cols(w_in[:, O_AB:O_ZS], 128), _pad_cols(w_in[:, O_DT:O_GATE], 128)], axis=1)


def _w_uncat(wc):
    return jnp.concatenate([
        wc[:, C_QKV:C_XBC], wc[:, C_ZG:C_AB], wc[:, C_AB:C_AB + (O_ZS - O_AB)], wc[:, C_ZS:C_GATE], wc[:, C_XBC:C_ZS],
        wc[:, C_DT:C_DT + (O_GATE - O_DT)], wc[:, C_GATE:C_ZG]], axis=1)


def _pad_row(vec, n=128):
    vec = vec.reshape(1, -1)
    return _pad_cols(vec, n)


def kernel(x, c, ctx, c_ctx, ada_w, ada_b, norm1_w, w_in, gdn_conv_w, gdn_conv_b, gdn_a_log, gdn_dt_bias, gdn_norm_w, ssm_conv_w, ssm_conv_b, ssm_a_log, ssm_dt_bias, ssm_d, ssm_norm_w, w_br_gdn, w_br_ssm, w_out, norm2_w, w_ffn_in, w_ffn_out, norm_f_w, loss_target, m_c_ctx, m_ada_w, m_ada_b, m_norm1_w, m_w_in, m_gdn_conv_w, m_gdn_conv_b, m_gdn_a_log, m_gdn_dt_bias, m_gdn_norm_w, m_ssm_conv_w, m_ssm_conv_b, m_ssm_a_log, m_ssm_dt_bias, m_ssm_d, m_ssm_norm_w, m_w_br_gdn, m_w_br_ssm, m_w_out, m_norm2_w, m_w_ffn_in, m_w_ffn_out, m_norm_f_w, v_c_ctx, v_ada_w, v_ada_b, v_norm1_w, v_w_in, v_gdn_conv_w, v_gdn_conv_b, v_gdn_a_log, v_gdn_dt_bias, v_gdn_norm_w, v_ssm_conv_w, v_ssm_conv_b, v_ssm_a_log, v_ssm_dt_bias, v_ssm_d, v_ssm_norm_w, v_w_br_gdn, v_w_br_ssm, v_w_out, v_norm2_w, v_w_ffn_in, v_w_ffn_out, v_norm_f_w):
    wts = dict(c_ctx=c_ctx, ada_w=ada_w, ada_b=ada_b, norm1_w=norm1_w, w_in=w_in, gdn_conv_w=gdn_conv_w, gdn_conv_b=gdn_conv_b, gdn_a_log=gdn_a_log, gdn_dt_bias=gdn_dt_bias, gdn_norm_w=gdn_norm_w, ssm_conv_w=ssm_conv_w, ssm_conv_b=ssm_conv_b, ssm_a_log=ssm_a_log, ssm_dt_bias=ssm_dt_bias, ssm_d=ssm_d, ssm_norm_w=ssm_norm_w, w_br_gdn=w_br_gdn, w_br_ssm=w_br_ssm, w_out=w_out, norm2_w=norm2_w, w_ffn_in=w_ffn_in, w_ffn_out=w_ffn_out, norm_f_w=norm_f_w)
    mom1 = dict(c_ctx=m_c_ctx, ada_w=m_ada_w, ada_b=m_ada_b, norm1_w=m_norm1_w, w_in=m_w_in, gdn_conv_w=m_gdn_conv_w, gdn_conv_b=m_gdn_conv_b, gdn_a_log=m_gdn_a_log, gdn_dt_bias=m_gdn_dt_bias, gdn_norm_w=m_gdn_norm_w, ssm_conv_w=m_ssm_conv_w, ssm_conv_b=m_ssm_conv_b, ssm_a_log=m_ssm_a_log, ssm_dt_bias=m_ssm_dt_bias, ssm_d=m_ssm_d, ssm_norm_w=m_ssm_norm_w, w_br_gdn=m_w_br_gdn, w_br_ssm=m_w_br_ssm, w_out=m_w_out, norm2_w=m_norm2_w, w_ffn_in=m_w_ffn_in, w_ffn_out=m_w_ffn_out, norm_f_w=m_norm_f_w)
    mom2 = dict(c_ctx=v_c_ctx, ada_w=v_ada_w, ada_b=v_ada_b, norm1_w=v_norm1_w, w_in=v_w_in, gdn_conv_w=v_gdn_conv_w, gdn_conv_b=v_gdn_conv_b, gdn_a_log=v_gdn_a_log, gdn_dt_bias=v_gdn_dt_bias, gdn_norm_w=v_gdn_norm_w, ssm_conv_w=v_ssm_conv_w, ssm_conv_b=v_ssm_conv_b, ssm_a_log=v_ssm_a_log, ssm_dt_bias=v_ssm_dt_bias, ssm_d=v_ssm_d, ssm_norm_w=v_ssm_norm_w, w_br_gdn=v_w_br_gdn, w_br_ssm=v_w_br_ssm, w_out=v_w_out, norm2_w=v_norm2_w, w_ffn_in=v_w_ffn_in, w_ffn_out=v_w_ffn_out, norm_f_w=v_norm_f_w)
    order = list(wts)

    seq = x.shape[1]
    t = TM + seq
    ntl, nlt, nctx = t // TM, seq // TM, TM // CH

    w_pack = _pack([wts[n] for n in BIG])
    gathered = _all_gather("ag_weights", w_pack.astype(BF16))
    conv_rows = sum(BIG_ROWS[:7])
    gathered_cv = _all_gather("ag_conv", jnp.pad(w_pack[conv_rows:], ((0, 4), (0, 0))))
    full, r0 = {}, 0
    for n, nr, shp in zip(BIG, BIG_ROWS, BIG_SHARD_SHAPES):
        if n.endswith("conv_w"):
            off = 0 if n.startswith("gdn") else 2
            full[n] = _full_from_gathered(gathered_cv[:, off:off + 2], n, shp)
        else:
            full[n] = _full_from_gathered(gathered[:, r0:r0 + nr], n, shp)
        r0 += nr
    w_cat = _w_cat(full["w_in"])
    gcw = full["gdn_conv_w"].reshape(3, 1, XBC)
    scw = full["ssm_conv_w"].reshape(3, 1, XBC)

    n1w, n2w, nfw = norm1_w.reshape(1, D), norm2_w.reshape(1, D), norm_f_w.reshape(1, D)
    gcb, scb = gdn_conv_b.reshape(1, XBC), ssm_conv_b.reshape(1, XBC)
    alog16, dtb16 = _pad_row(gdn_a_log), _pad_row(gdn_dt_bias)
    alog64, dtb64 = _pad_row(ssm_a_log), _pad_row(ssm_dt_bias)
    gnw = gdn_norm_w.reshape(1, DK)
    ssd8 = jnp.tile(_pad_row(ssm_d), (8, 1))
    snw = ssm_norm_w.reshape(1, S_INNER)
    x2 = x[0]
    tgt = loss_target[0]
    xa = jnp.concatenate([ctx[0], x2], axis=0)
    cvec = jnp.concatenate([c, c_ctx.reshape(1, D), jnp.zeros((14, D), F32)], axis=0)

    a16 = _rowwise("silu_c", f_silu_rows, [(cvec, D, 0, 0)], [], [(D, BF16)], 1, tm=16)[0]
    mod = _mm("mm_mod", a16, full["ada_w"], "nn", F32) + ada_b
    sh1, sc1, g1, sh2, sc2, g2 = [mod[0:1, i * D:(i + 1) * D] for i in range(6)]
    csh1, csc1 = mod[1:2, 0:D], mod[1:2, D:2 * D]

    pre_pars = [n1w, sc1, sh1, csc1, csh1]
    a = _rowwise("pre", f_pre, [(xa, D, 0, 0)], pre_pars, [(D, BF16)], ntl, base=0)[0]
    proj = _mm("mm_proj", a, w_cat, "nn", F32)
    gp_rows = [(proj, XBC, C_QKV // XBC, 0), (proj, 128, C_AB // 128, 0)]
    gp_pars = [gcw, gcb, alog16, dtb16]
    q, k, v, gcum, beta = _rowwise("gdnprep", f_gdnprep, gp_rows, gp_pars, [(D, F32)] * 3 + [(128, F32)] * 2, ntl, base=0)
    sp_rows = [(proj, XBC, C_XBC // XBC, 0), (proj, 128, C_DT // 128, 0)]
    sp_pars = [scw, scb, alog64, dtb64]
    xs, bm, cm, dtx0, acx0, dtx1, acx1 = _rowwise(
        "ssmprep", f_ssmprep, sp_rows, sp_pars, [(S_INNER, F32), (512, F32), (512, F32)] + [(S_INNER, F32)] * 4, ntl, base=0)
    o0, ss0 = _gdn_scan_fwd("gdn_fwd0", q, k, v, gcum, beta, 0, nctx)
    o1, ss1 = _gdn_scan_fwd("gdn_fwd1", q, k, v, gcum, beta, 1, nctx)
    y0, hs0 = _ssd_scan_fwd("ssd_fwd0", xs, dtx0, acx0, bm, cm, 0, nctx)
    y1, hs1 = _ssd_scan_fwd("ssd_fwd1", xs, dtx1, acx1, bm, cm, 1, nctx)
    post_rows = [(o0, D, 0, 1), (o1, D, 0, 1), (proj, D, C_ZG // D, 1), (y0, S_INNER, 0, 1), (y1, S_INNER, 0, 1),
                 (xs, S_INNER, 0, 1), (proj, S_INNER, C_ZS // S_INNER, 1)]
    post_pars = [gnw, ssd8, snw]
    og, ys = _rowwise("post", f_post, post_rows, post_pars, [(D, BF16), (S_INNER, BF16)], nlt)
    pg = _mm("mm_pg", og, full["w_br_gdn"], "nn", F32)
    ps = _mm("mm_ps", ys, full["w_br_ssm"], "nn", F32)
    merge_rows = [(proj, S_INNER, C_GATE // S_INNER, 1), (pg, D, 0, 0), (ps, D, 0, 0)]
    merged = _rowwise("merge", f_merge, merge_rows, [], [(D, BF16)], nlt)[0]
    mix = _mm("mm_mix", merged, full["w_out"], "nn", F32)
    res_rows = [(x2, D, 0, 0), (mix, D, 0, 0)]
    res_pars = [g1, n2w, sc2, sh2]
    h1, f = _rowwise("res1", f_res1, res_rows, res_pars, [(D, F32), (D, BF16)], nlt)
    u = _mm("mm_u", f, full["w_ffn_in"], "nn", F32)
    hact = _rowwise("act", f_act, [(u, 2 * D_FF, 0, 0)], [], [(D_FF, BF16)], nlt)[0]
    ff = _mm("mm_ff", hact, full["w_ffn_out"], "nn", F32)

    fin_rows = [(h1, D, 0, 0), (ff, D, 0, 0), (tgt, D, 0, 0)]
    d_h1a, d_ff, d_g2, d_nfw, loss_acc = _rowwise_bwd(
        "final", f_final, fin_rows, [g2, nfw], ["one"], [(0, F32), (1, BF16)], [0, 1], nlt, loss_out=True)
    d_hact = _mm("mm_dhact", d_ff, full["w_ffn_out"], "nt", BF16)
    g_w_ffn_out = _mm("mm_gwffo", hact, d_ff, "tn", F32)
    d_u = _rowwise_bwd("act_bwd", f_act, [(u, 2 * D_FF, 0, 0)], [], [[(d_hact, D_FF, 0, 0)]], [(0, BF16)], [], nlt)[0]
    d_f = _mm("mm_df", d_u, full["w_ffn_in"], "nt", BF16)
    g_w_ffn_in = _mm("mm_gwffi", f, d_u, "tn", F32)
    d_xres, d_mix, d_g1, d_n2w, d_sc2, d_sh2 = _rowwise_bwd(
        "res1_bwd", f_res1, res_rows, res_pars, [[(d_h1a, D, 0, 0)], [(d_f, D, 0, 0)]], [(0, F32), (1, BF16)], [0, 1, 2, 3], nlt)
    d_merged = _mm("mm_dmerged", d_mix, full["w_out"], "nt", BF16)
    g_w_out = _mm("mm_gwout", merged, d_mix, "tn", F32)
    d_gate, d_pg, d_ps = _rowwise_bwd(
        "merge_bwd", f_merge, merge_rows, [], [[(d_merged, D, 0, 0)]], [(0, BF16), (1, BF16), (2, BF16)], [], nlt)
    d_og = _mm("mm_dog", d_pg, full["w_br_gdn"], "nt", BF16)
    g_w_br_gdn = _mm("mm_gwbrg", og, d_pg, "tn", F32)
    d_ys = _mm("mm_dys", d_ps, full["w_br_ssm"], "nt", BF16)
    g_w_br_ssm = _mm("mm_gwbrs", ys, d_ps, "tn", F32)
    d_o, d_zg, d_y, d_xs_post, d_zs, d_gnw, d_ssd8, d_snw = _rowwise_bwd(
        "post_bwd", f_post, post_rows, post_pars, [[(d_og, D, 0, 0)], [(d_ys, S_INNER, 0, 0)]],
        [(0, F32), (2, BF16), (3, F32), (5, F32), (6, BF16)], [0, 1, 2], nlt)
    dq0, dk0, dv0, dg0, db0 = _gdn_scan_bwd("gdn_bwd0", q, k, v, gcum, beta, ss0, d_o, 0, nctx)
    dq1, dk1, dv1, dg1, db1 = _gdn_scan_bwd("gdn_bwd1", q, k, v, gcum, beta, ss1, d_o, 1, nctx)
    dxs0, ddt0, dac0, dbm0, dcm0 = _ssd_scan_bwd("ssd_bwd0", xs, dtx0, acx0, bm, cm, hs0, d_y, 0, nctx)
    dxs1, ddt1, dac1, dbm1, dcm1 = _ssd_scan_bwd("ssd_bwd1", xs, dtx1, acx1, bm, cm, hs1, d_y, 1, nctx)
    row = lambda arr, wd: (arr, wd, 0, 0)
    d_qkv_raw, d_ab, d_gcw, d_gcb, d_alog16, d_dtb16 = _rowwise_bwd(
        "gdnprep_bwd", f_gdnprep, gp_rows, gp_pars,
        [[row(dq0, D), row(dq1, D)], [row(dk0, D), row(dk1, D)], [row(dv0, D), row(dv1, D)],
         [row(dg0, 128), row(dg1, 128)], [row(db0, 128), row(db1, 128)]],
        [(0, BF16), (1, BF16)], [0, 1, 2, 3], ntl, base=0)
    d_xbc_raw, d_dt, d_scw, d_scb, d_alog64, d_dtb64 = _rowwise_bwd(
        "ssmprep_bwd", f_ssmprep, sp_rows, sp_pars,
        [[row(dxs0, S_INNER), row(dxs1, S_INNER), (d_xs_post, S_INNER, 0, -1, True)], [row(dbm0, 512), row(dbm1, 512)],
         [row(dcm0, 512), row(dcm1, 512)], [row(ddt0, S_INNER)], [row(dac0, S_INNER)], [row(ddt1, S_INNER)],
         [row(dac1, S_INNER)]],
        [(0, BF16), (1, BF16)], [0, 1, 2, 3], ntl, base=0)
    ctx_zero = lambda wd: jnp.zeros((TM, wd), BF16)
    d_proj = jnp.concatenate([
        d_qkv_raw, d_xbc_raw, jnp.concatenate([ctx_zero(S_INNER), d_zs], axis=0),
        jnp.concatenate([ctx_zero(S_INNER), d_gate], axis=0), jnp.concatenate([ctx_zero(D), d_zg], axis=0), d_ab, d_dt], axis=1)
    d_a = _mm("mm_da", d_proj, w_cat, "nt", BF16)
    g_w_cat = _mm("mm_gwcat", a, d_proj, "tn", F32)
    d_xa, d_n1w, d_sc1, d_sh1, d_csc1, d_csh1 = _rowwise_bwd(
        "pre_bwd", f_pre_thru, [(xa, D, 0, 0)], pre_pars, [[row(d_a, D)], [(d_xres, D, 0, -1, True)]],
        [(0, F32)], [0, 1, 2, 3, 4], ntl, base=0)
    zero4 = jnp.zeros((1, 4 * D), F32)
    d_mod = jnp.concatenate([
        jnp.concatenate([d_sh1, d_sc1, d_g1, d_sh2, d_sc2, d_g2], axis=1),
        jnp.concatenate([d_csh1, d_csc1, zero4], axis=1), jnp.zeros((14, 6 * D), F32)], axis=0)
    g_ada_w = _mm("mm_gwada", a16, d_mod, "tn", F32)
    d_a16 = _mm("mm_da16", d_mod, full["ada_w"], "nt", F32)
    d_cvec = _rowwise_bwd("silu_c_bwd", f_silu_rows, [(cvec, D, 0, 0)], [], [[row(d_a16, D)]], [(0, F32)], [], 1, tm=16)[0]

    grads_full = dict(w_in=_w_uncat(g_w_cat), ada_w=g_ada_w, w_br_gdn=g_w_br_gdn, w_br_ssm=g_w_br_ssm, w_out=g_w_out,
                      w_ffn_in=g_w_ffn_in, w_ffn_out=g_w_ffn_out, gdn_conv_w=d_gcw.reshape(3, XBC),
                      ssm_conv_w=d_scw.reshape(3, XBC))
    g_pack = jnp.concatenate([_blocks_from_full(grads_full[n], n, shp) for n, shp in zip(BIG, BIG_SHARD_SHAPES)], axis=1)
    recv = _all_to_all("a2a_grads", g_pack)
    m_pack = _pack([mom1[n] for n in BIG])
    v_pack = _pack([mom2[n] for n in BIG])
    big_out = _reduce_adam("adam_big", recv, w_pack, m_pack, v_pack, 80)
    big_shapes = [wts[n].shape for n in BIG]
    big_un = [_unpack(buf, big_shapes) for buf in big_out]

    small_g = dict(c_ctx=d_cvec[1], ada_b=d_mod[0] + d_mod[1], norm1_w=d_n1w, gdn_conv_b=d_gcb,
                   gdn_a_log=d_alog16[0, :2 * G_HEADS], gdn_dt_bias=d_dtb16[0, :2 * G_HEADS], gdn_norm_w=d_gnw,
                   ssm_conv_b=d_scb, ssm_a_log=d_alog64[0, :2 * S_HEADS], ssm_dt_bias=d_dtb64[0, :2 * S_HEADS],
                   ssm_d=d_ssd8[0, :S_HEADS], ssm_norm_w=d_snw, norm2_w=d_n2w, norm_f_w=d_nfw)
    sg_pack = _pack([small_g[n] for n in SMALL])
    recv_s = _all_gather("ag_small_grads", sg_pack)
    small_out = _reduce_adam("adam_small", recv_s, _pack([wts[n] for n in SMALL]), _pack([mom1[n] for n in SMALL]),
                             _pack([mom2[n] for n in SMALL]), sg_pack.shape[0])
    small_shapes = [wts[n].shape for n in SMALL]
    small_un = [_unpack(buf, small_shapes) for buf in small_out]

    res = [{}, {}, {}, {}]
    for kind in range(4):
        for n, val in zip(BIG, big_un[kind]):
            res[kind][n] = val
        for n, val in zip(SMALL, small_un[kind]):
            res[kind][n] = val
    loss = lax.psum(loss_acc[0, 0], ("x", "y", "c"))
    grad_x = d_xa[TM:].reshape(x.shape)
    return (loss, grad_x, *[res[0][n] for n in order], *[res[1][n] for n in order], *[res[2][n] for n in order],
            *[res[3][n] for n in order])
```

```python
import functools

import jax
import jax.numpy as jnp
from jax import lax
from jax.experimental import pallas as pl
from jax.experimental.pallas import tpu as pltpu

F32 = jnp.float32
BF16 = jnp.bfloat16
HI = lax.Precision.HIGHEST
HIGH = lax.Precision.HIGH
MESH = pl.DeviceIdType.MESH

D = 1024
CH = 64
TM = 256
EPS = 1e-6
NEG = -1e30
G_HEADS = 8
DK = 128
S_HEADS = 32
S_P = 64
S_GROUPS = 4
S_N = 128
S_INNER = 2048
XBC = 3072
D_FF = 2816
N_DEV = 8
PACK_W = 1024
VMEM_LIMIT = 56 * 1024 * 1024

ADAM_LR = 0.001
ADAM_B1 = 0.9
ADAM_B2 = 0.999
ADAM_EPS = 1e-08
ADAM_WD = 0.01
ADAM_STEP = 10

C_QKV, C_XBC, C_ZS, C_GATE, C_ZG, C_AB, C_DT, C_END = 0, 3072, 6144, 8192, 10240, 11264, 11392, 11520
O_QKV, O_ZG, O_AB, O_ZS, O_XBC, O_DT, O_GATE, O_END = 0, 3072, 4096, 4128, 6176, 9248, 9312, 11360


def _dot(a, b, prec=None):
    return jnp.dot(a, b, precision=prec, preferred_element_type=F32)


def _dot_nt(a, b, prec=None):
    return lax.dot_general(a, b, (((1,), (1,)), ((), ())), precision=prec, preferred_element_type=F32)


def _dot_tn(a, b, prec=None):
    return lax.dot_general(a, b, (((0,), (0,)), ((), ())), precision=prec, preferred_element_type=F32)


def _iota(shape, dim):
    return lax.broadcasted_iota(jnp.int32, shape, dim)


def _rms(x):
    return x * lax.rsqrt(jnp.mean(x * x, axis=-1, keepdims=True) + EPS)


def _l2n(x):
    return x * lax.rsqrt(jnp.sum(x * x, axis=-1, keepdims=True) + EPS)


def _silu(x):
    return x * jax.nn.sigmoid(x)


def _softplus(x):
    return jnp.maximum(x, 0.0) + jnp.log1p(jnp.exp(-jnp.abs(x)))


def _roll_rows(x, s):
    return pltpu.roll(x, s, 0)


def _up_raw(x, keep_up):
    return jnp.where(keep_up > 0.0, _roll_rows(x, 1), 0.0)


def _dn_raw(x, keep_dn):
    return jnp.where(keep_dn > 0.0, _roll_rows(x, x.shape[0] - 1), 0.0)


@jax.custom_vjp
def _shift_up(x, keep_up, keep_dn):
    return _up_raw(x, keep_up)


def _shift_up_fwd(x, keep_up, keep_dn):
    return _up_raw(x, keep_up), (keep_up, keep_dn)


def _shift_up_bwd(res, g):
    keep_up, keep_dn = res
    return _dn_raw(g, keep_dn), jnp.zeros_like(keep_up), jnp.zeros_like(keep_dn)


_shift_up.defvjp(_shift_up_fwd, _shift_up_bwd)


@jax.custom_vjp
def _shift_dn(x, keep_up, keep_dn):
    return _dn_raw(x, keep_dn)


def _shift_dn_fwd(x, keep_up, keep_dn):
    return _dn_raw(x, keep_dn), (keep_up, keep_dn)


def _shift_dn_bwd(res, g):
    keep_up, keep_dn = res
    return _up_raw(g, keep_up), jnp.zeros_like(keep_up), jnp.zeros_like(keep_dn)


_shift_dn.defvjp(_shift_dn_fwd, _shift_dn_bwd)


def _conv_keep(is_ctx, n):
    r = _iota((n, 1), 0)
    pos = jnp.where(is_ctx, r, r & (CH - 1))
    end = jnp.where(is_ctx, n - 1, CH - 1)
    return jnp.where(pos == 0, 0.0, 1.0).astype(F32), jnp.where(pos == end, 0.0, 1.0).astype(F32)


def _conv_silu(u, w3, b, keep_up, keep_dn):
    conv = b + _shift_up(u, keep_up, keep_dn) * w3[0] + u * w3[1] + _shift_dn(u, keep_up, keep_dn) * w3[2]
    return _silu(conv)


def _chunk_tri(n, rev):
    i = _iota((n, n), 0)
    j = _iota((n, n), 1)
    same = (i // CH) == (j // CH)
    seen = (i <= j) if rev else (i >= j)
    return jnp.where(same & seen, 1.0, 0.0).astype(F32)


def _expand_mat(rows, cols, per, base):
    r = _iota((rows, cols), 0)
    c = _iota((rows, cols), 1)
    return jnp.where(r == base + c // per, 1.0, 0.0).astype(F32)


def f_silu_rows(is_ctx, cvec):
    return (_silu(cvec).astype(BF16),)


def f_pre(is_ctx, x, n1w, sc, sh, csc, csh):
    sc_e = jnp.where(is_ctx, csc, sc)
    sh_e = jnp.where(is_ctx, csh, sh)
    a = _rms(x) * n1w * (1.0 + sc_e) + sh_e
    return (a.astype(BF16),)


def f_pre_thru(is_ctx, x, n1w, sc, sh, csc, csh):
    return f_pre(is_ctx, x, n1w, sc, sh, csc, csh)[0], x


def f_gdnprep(is_ctx, qkv_raw, ab_raw, cw, cb, alog, dtb):
    n = qkv_raw.shape[0]
    keep_up, keep_dn = _conv_keep(is_ctx, n)
    s = _conv_silu(qkv_raw, cw, cb, keep_up, keep_dn)
    qs, ks, vs = [], [], []
    for h in range(G_HEADS):
        qs.append(_l2n(s[:, h * DK:(h + 1) * DK]) * (DK ** -0.5))
        ks.append(_l2n(s[:, D + h * DK:D + (h + 1) * DK]))
    q = jnp.concatenate(qs, axis=1)
    k = jnp.concatenate(ks, axis=1)
    v = s[:, 2 * D:3 * D]
    lane = _iota(ab_raw.shape, 1)
    g = jnp.where(lane < 2 * G_HEADS, -jnp.exp(alog) * _softplus(ab_raw + dtb), 0.0)
    gcum = jnp.where(lane < G_HEADS, _dot(_chunk_tri(n, False), g, HI), _dot(_chunk_tri(n, True), g, HI))
    beta = jax.nn.sigmoid(ab_raw)
    return q, k, v, gcum, beta


def f_ssmprep(is_ctx, xbc_raw, dt_raw, cw, cb, alog, dtb):
    n = xbc_raw.shape[0]
    keep_up, keep_dn = _conv_keep(is_ctx, n)
    s = _conv_silu(xbc_raw, cw, cb, keep_up, keep_dn)
    xs = s[:, :S_INNER]
    bm = s[:, S_INNER:S_INNER + S_GROUPS * S_N]
    cm = s[:, S_INNER + S_GROUPS * S_N:]
    lane = _iota(dt_raw.shape, 1)
    dt = jnp.where(lane < 2 * S_HEADS, _softplus(dt_raw + dtb), 0.0)
    da = dt * (-jnp.exp(alog))
    acum = jnp.where(lane < S_HEADS, _dot(_chunk_tri(n, False), da, HI), _dot(_chunk_tri(n, True), da, HI))
    outs = [xs, bm, cm]
    for d in range(2):
        e = _expand_mat(128, S_INNER, S_P, d * S_HEADS)
        outs.append(_dot(dt, e, HI))
        outs.append(_dot(acum, e, HI))
    return tuple(outs)


def f_post(is_ctx, o_f, o_b, zg, y_f, y_b, xs, zs, gnw, ssd8, snw):
    o = o_f + o_b
    ogs = []
    for h in range(G_HEADS):
        sl = slice(h * DK, (h + 1) * DK)
        ogs.append(_rms(o[:, sl]) * gnw * _silu(zg[:, sl]))
    og = jnp.concatenate(ogs, axis=1)
    row0 = jnp.where(_iota(ssd8.shape, 0) == 0, 1.0, 0.0).astype(F32)
    dexp = jnp.sum(_dot(ssd8 * row0, _expand_mat(128, S_INNER, S_P, 0), HI), axis=0, keepdims=True)
    y = (y_f + y_b + dexp * xs) * _silu(zs)
    gw = S_INNER // S_GROUPS
    ys = jnp.concatenate([_rms(y[:, i * gw:(i + 1) * gw]) * snw[:, i * gw:(i + 1) * gw] for i in range(S_GROUPS)], axis=1)
    return og.astype(BF16), ys.astype(BF16)


def f_merge(is_ctx, gate, pg, ps):
    m = jax.nn.sigmoid(gate[:, :D]) * pg + jax.nn.sigmoid(gate[:, D:]) * ps
    return (m.astype(BF16),)


def f_res1(is_ctx, x, mix, g1, n2w, sc2, sh2):
    h1 = x + g1 * mix
    f = _rms(h1) * n2w * (1.0 + sc2) + sh2
    return h1, f.astype(BF16)


def f_act(is_ctx, u):
    return ((_silu(u[:, :D_FF]) * u[:, D_FF:]).astype(BF16),)


def f_final(is_ctx, h1, ff, tgt, g2, nfw):
    h2 = h1 + g2 * ff
    y = _rms(h2) * nfw
    err = y - tgt
    return (0.5 * jnp.sum(jnp.mean(err * err, axis=-1, keepdims=True), axis=0, keepdims=True),)


def _each(fn, *lists):
    return [fn(*args) for args in zip(*lists)]


def _tri_inverse_all(mats):
    n = mats[0].shape[0]
    eye = jnp.where(_iota((n, n), 0) == _iota((n, n), 1), 1.0, 0.0).astype(F32)
    t = [eye - a for a in mats]
    p = [_dot(a, a, HIGH) for a in mats]
    for r in range(5):
        t = _each(lambda t_, p_: t_ + _dot(t_, p_, HIGH), t, p)
        if r < 4:
            p = [_dot(p_, p_, HIGH) for p_ in p]
    return t


def gdn_chunk(ss, qs, ks, vs, gcs, grs, bcs, rev):
    c = qs[0].shape[0]
    ii = _iota((c, c), 0)
    jj = _iota((c, c), 1)
    incl = (ii <= jj) if rev else (ii >= jj)
    strict = (ii < jj) if rev else (ii > jj)
    last = 0 if rev else c - 1
    is_last = _iota((c, 1), 0) == last
    decay = _each(lambda gc, gr: jnp.exp(jnp.where(incl, gc - gr, NEG)), gcs, grs)
    kb = _each(lambda k, bc: k * bc, ks, bcs)
    a = _each(lambda kb_, k, dc: jnp.where(strict, _dot_nt(kb_, k) * dc, 0.0), kb, ks, decay)
    t = _tri_inverse_all(a)
    eg = [jnp.exp(gc) for gc in gcs]
    rhs = _each(lambda kb_, eg_, v, bc: jnp.concatenate([kb_ * eg_, v * bc], axis=1), kb, eg, vs, bcs)
    wu = _each(lambda t_, r: _dot(t_, r, HIGH), t, rhs)
    lhs = _each(lambda wu_, q, eg_: jnp.concatenate([wu_[:, :DK], q * eg_], axis=0), wu, qs, eg)
    ws = _each(_dot, lhs, ss)
    v_new = _each(lambda wu_, ws_: wu_[:, DK:] - ws_[:c], wu, ws)
    attn = _each(lambda q, k, dc: _dot_nt(q, k) * dc, qs, ks, decay)
    o = _each(lambda ws_, at, vn: ws_[c:] + _dot(at, vn), ws, attn, v_new)
    gtot = [jnp.sum(jnp.where(is_last, gc, 0.0), axis=0, keepdims=True) for gc in gcs]
    s_new = _each(lambda s, k, gc, gt_, vn: s * jnp.exp(gt_) + _dot_tn(k * jnp.exp(gt_ - gc), vn), ss, ks, gcs, gtot, v_new)
    return s_new, o


def ssd_chunk(hts, xs, dts, acs, bgs, cgs, rev):
    c = xs[0].shape[0]
    npair = len(xs)
    grp = [p * len(bgs) // npair for p in range(npair)]
    lane = _iota((c, 128), 1)
    ii = _iota((c, 128), 0)
    jl = lane & (S_P - 1)
    lo = lane < S_P
    seen = (ii <= jl) if rev else (ii >= jl)
    last = 0 if rev else c - 1
    sel = jnp.where((lane == 0) | (lane == S_P), 1.0, 0.0).astype(F32)
    split = lambda z: jnp.concatenate([jnp.where(lo, z, 0.0), jnp.where(lo, 0.0, z)], axis=0)
    cb = _each(lambda bg, cg: _dot_nt(cg, jnp.concatenate([bg, bg], axis=0)), bgs, cgs)
    acr = [_dot_nt(sel, split(ac), HIGH) for ac in acs]
    seg = _each(lambda ac, ar: jnp.exp(jnp.where(seen, ac - ar, NEG)), acs, acr)
    xdt = _each(lambda x, dt: x * dt, xs, dts)
    ydiag = [_dot(cb[grp[p]] * seg[p], split(xdt[p])) for p in range(npair)]
    yoff = [_dot(cgs[grp[p]], hts[p]) * jnp.exp(acs[p]) for p in range(npair)]
    atot = [jnp.sum(jnp.where(ii == last, ac, 0.0), axis=0, keepdims=True) for ac in acs]
    h_new = [hts[p] * jnp.exp(atot[p]) + _dot_tn(bgs[grp[p]], xdt[p] * jnp.exp(atot[p] - acs[p])) for p in range(npair)]
    return h_new, _each(lambda a_, b_: a_ + b_, ydiag, yoff)


def f_adamw(w, g, m, v):
    m = ADAM_B1 * m + (1.0 - ADAM_B1) * g
    v = ADAM_B2 * v + (1.0 - ADAM_B2) * jnp.square(g)
    m_hat = m / (1.0 - ADAM_B1 ** ADAM_STEP)
    v_hat = v / (1.0 - ADAM_B2 ** ADAM_STEP)
    delta = -ADAM_LR * (m_hat / (jnp.sqrt(v_hat) + ADAM_EPS) + ADAM_WD * w)
    return delta, m, v


def _cparams(sem):
    return pltpu.CompilerParams(dimension_semantics=sem, vmem_limit_bytes=VMEM_LIMIT)


def _pick(n, target):
    if n <= target:
        return n
    best = None
    for t in range(128, target + 1, 128):
        if n % t == 0:
            best = t
    assert best is not None, (n, target)
    return best


def _row_spec(tm, width, colblk, rowoff):
    return pl.BlockSpec((tm, width), lambda i: (i + rowoff, colblk))


def _par_spec(shape):
    nd = len(shape)
    return pl.BlockSpec(tuple(shape), lambda i: (0,) * nd)


def _rowwise(name, fn, rows, pars, outs, ntiles, base=1, tm=TM):
    nr, npar = len(rows), len(pars)

    def body(*refs):
        is_ctx = (pl.program_id(0) + base) == 0
        res = fn(is_ctx, *[r[...] for r in refs[:nr]], *[p[...] for p in refs[nr:nr + npar]])
        for o_ref, r in zip(refs[nr + npar:], res):
            o_ref[...] = r.astype(o_ref.dtype)

    return pl.pallas_call(
        body, grid=(ntiles,), name=name,
        in_specs=[_row_spec(tm, wd, cb, ro) for (_, wd, cb, ro) in rows] + [_par_spec(p.shape) for p in pars],
        out_specs=[_row_spec(tm, wd, 0, 0) for (wd, _) in outs],
        out_shape=[jax.ShapeDtypeStruct((ntiles * tm, wd), dt) for (wd, dt) in outs],
        compiler_params=_cparams(("arbitrary",)),
    )(*[r[0] for r in rows], *pars)


def _ct_spec(tm, desc):
    _, wd, cb, ro = desc[:4]
    if len(desc) > 4 and desc[4]:
        return pl.BlockSpec((tm, wd), lambda i: (jnp.maximum(i + ro, 0), cb))
    return _row_spec(tm, wd, cb, ro)


def _rowwise_bwd(name, fn, rows, pars, cts, drows, dpars, ntiles, base=1, loss_out=False, tm=TM):
    nr, npar = len(rows), len(pars)
    ct_rows = [d for ct in cts if isinstance(ct, list) for d in ct]
    nct = len(ct_rows)

    def body(*refs):
        i = pl.program_id(0)
        is_ctx = (i + base) == 0
        rows_v = [r[...] for r in refs[:nr]]
        pars_v = [p[...] for p in refs[nr:nr + npar]]
        ct_refs = list(refs[nr + npar:nr + npar + nct])
        out_refs = list(refs[nr + npar + nct:])
        outs, vjp = jax.vjp(lambda rv, pv: fn(is_ctx, *rv, *pv), rows_v, pars_v)

        def ct_value(desc):
            val = ct_refs.pop(0)[...].astype(F32)
            if len(desc) > 4 and desc[4]:
                val = jnp.where(is_ctx, 0.0, val)
            return val

        ct_vals = []
        for o, ct in zip(outs, cts):
            if ct is None:
                ct_vals.append(jnp.zeros_like(o))
            elif isinstance(ct, str):
                ct_vals.append(jnp.ones_like(o))
            else:
                acc = ct_value(ct[0])
                for desc in ct[1:]:
                    acc = acc + ct_value(desc)
                ct_vals.append(acc.astype(o.dtype))
        d_rows, d_pars = vjp(tuple(ct_vals))
        for (ri, _), o_ref in zip(drows, out_refs[:len(drows)]):
            o_ref[...] = d_rows[ri].astype(o_ref.dtype)
        acc_refs = out_refs[len(drows):]
        acc_vals = [d_pars[pi] for pi in dpars]
        if loss_out:
            acc_vals.append(jnp.broadcast_to(outs[0], (8, 128)))

        @pl.when(i == 0)
        def _():
            for o_ref, val in zip(acc_refs, acc_vals):
                o_ref[...] = val

        @pl.when(i > 0)
        def _():
            for o_ref, val in zip(acc_refs, acc_vals):
                o_ref[...] += val

    acc_shapes = [pars[pi].shape for pi in dpars] + ([(8, 128)] if loss_out else [])
    return pl.pallas_call(
        body, grid=(ntiles,), name=name,
        in_specs=[_row_spec(tm, wd, cb, ro) for (_, wd, cb, ro) in rows] + [_par_spec(p.shape) for p in pars]
        + [_ct_spec(tm, d) for d in ct_rows],
        out_specs=[_row_spec(tm, rows[ri][1], 0, 0) for (ri, _) in drows] + [_par_spec(s) for s in acc_shapes],
        out_shape=[jax.ShapeDtypeStruct((ntiles * tm, rows[ri][1]), dt) for (ri, dt) in drows]
        + [jax.ShapeDtypeStruct(tuple(s), F32) for s in acc_shapes],
        compiler_params=_cparams(("arbitrary",)),
    )(*[r[0] for r in rows], *pars, *[r[0] for r in ct_rows])


def _mm(name, a, b, mode, out_dtype, tm=768, tn=1024, tk=1024):
    if mode == "nn":
        (m, kd), (_, n) = a.shape, b.shape
    elif mode == "nt":
        (m, kd), (n, _) = a.shape, b.shape
    else:
        (kd, m), (_, n) = a.shape, b.shape
    tm, tn, tk = _pick(m, tm), _pick(n, tn), _pick(kd, tk)
    nk = kd // tk
    a_spec = {"nn": pl.BlockSpec((tm, tk), lambda i, j, k: (i, k)), "nt": pl.BlockSpec((tm, tk), lambda i, j, k: (i, k)),
              "tn": pl.BlockSpec((tk, tm), lambda i, j, k: (k, i))}[mode]
    b_spec = {"nn": pl.BlockSpec((tk, tn), lambda i, j, k: (k, j)), "nt": pl.BlockSpec((tn, tk), lambda i, j, k: (j, k)),
              "tn": pl.BlockSpec((tk, tn), lambda i, j, k: (k, j))}[mode]
    dot = {"nn": _dot, "nt": _dot_nt, "tn": _dot_tn}[mode]

    def body(a_ref, b_ref, o_ref, acc_ref):
        k = pl.program_id(2)
        part = dot(a_ref[...].astype(BF16), b_ref[...].astype(BF16))

        @pl.when(k == 0)
        def _():
            acc_ref[...] = part

        @pl.when(k > 0)
        def _():
            acc_ref[...] += part

        @pl.when(k == nk - 1)
        def _():
            o_ref[...] = acc_ref[...].astype(o_ref.dtype)

    return pl.pallas_call(
        body, grid=(m // tm, n // tn, nk), name=name,
        in_specs=[a_spec, b_spec],
        out_specs=pl.BlockSpec((tm, tn), lambda i, j, k: (i, j)),
        out_shape=jax.ShapeDtypeStruct((m, n), out_dtype),
        scratch_shapes=[pltpu.VMEM((tm, tn), F32)],
        compiler_params=_cparams(("arbitrary", "arbitrary", "arbitrary")),
    )(a, b)


def _chunk_index(i, nch, nctx, rev):
    if not rev:
        return i
    return jnp.where(i < nctx, nctx - 1 - i, nch + nctx - 1 - i)


def _gdn_cols(d):
    return [d * G_HEADS + h for h in range(G_HEADS)], [2 * G_HEADS + d * G_HEADS + h for h in range(G_HEADS)]


def _gdn_operands(q_ref, k_ref, v_ref, g_ref, b_ref, d):
    cols_g, cols_b = _gdn_cols(d)
    sls = [slice(h * DK, (h + 1) * DK) for h in range(G_HEADS)]
    gt, bt = g_ref[...], b_ref[...]
    gtt = gt.T
    qs = [q_ref[:, sl] for sl in sls]
    ks = [k_ref[:, sl] for sl in sls]
    vs = [v_ref[:, sl] for sl in sls]
    gcs = [gt[:, cg:cg + 1] for cg in cols_g]
    grs = [gtt[cg:cg + 1, :] for cg in cols_g]
    bcs = [bt[:, cb:cb + 1] for cb in cols_b]
    return sls, qs, ks, vs, gcs, grs, bcs


def _gdn_scan_fwd(name, q, k, v, gcum, beta, d, nctx):
    t = q.shape[0]
    nch = t // CH
    rev = d == 1
    cix = lambda i: _chunk_index(i, nch, nctx, rev)
    full = pl.BlockSpec((CH, D), lambda i: (cix(i), 0))
    wide = pl.BlockSpec((CH, 128), lambda i: (cix(i), 0))

    def body(q_ref, k_ref, v_ref, g_ref, b_ref, o_ref, ss_ref, s_scr):
        @pl.when(pl.program_id(0) == 0)
        def _():
            s_scr[...] = jnp.zeros(s_scr.shape, F32)

        sls, qs, ks, vs, gcs, grs, bcs = _gdn_operands(q_ref, k_ref, v_ref, g_ref, b_ref, d)
        ss = [s_scr[h] for h in range(G_HEADS)]
        s_new, o = gdn_chunk(ss, qs, ks, vs, gcs, grs, bcs, rev)
        for h in range(G_HEADS):
            ss_ref[0, h] = ss[h]
            s_scr[h] = s_new[h]
            o_ref[:, sls[h]] = o[h]

    return pl.pallas_call(
        body, grid=(nch,), name=name,
        in_specs=[full, full, full, wide, wide],
        out_specs=[full, pl.BlockSpec((1, G_HEADS, DK, DK), lambda i: (i, 0, 0, 0))],
        out_shape=[jax.ShapeDtypeStruct((t, D), F32), jax.ShapeDtypeStruct((nch, G_HEADS, DK, DK), F32)],
        scratch_shapes=[pltpu.VMEM((G_HEADS, DK, DK), F32)],
        compiler_params=_cparams(("arbitrary",)),
    )(q, k, v, gcum, beta)


def _gdn_scan_bwd(name, q, k, v, gcum, beta, ssave, do, d, nctx):
    t = q.shape[0]
    nch = t // CH
    rev = d == 1
    cix = lambda ib: _chunk_index(nch - 1 - ib, nch, nctx, rev)
    full = pl.BlockSpec((CH, D), lambda ib: (cix(ib), 0))
    wide = pl.BlockSpec((CH, 128), lambda ib: (cix(ib), 0))
    do_spec = pl.BlockSpec((CH, D), lambda ib: (jnp.maximum(cix(ib), nctx) - nctx, 0))

    def body(q_ref, k_ref, v_ref, g_ref, b_ref, ss_ref, do_ref, dq_ref, dk_ref, dv_ref, dg_ref, db_ref, ds_scr):
        ib = pl.program_id(0)

        @pl.when(ib == 0)
        def _():
            ds_scr[...] = jnp.zeros(ds_scr.shape, F32)

        sls, qs, ks, vs, gcs, grs, bcs = _gdn_operands(q_ref, k_ref, v_ref, g_ref, b_ref, d)
        cols_g, cols_b = _gdn_cols(d)
        is_lat = cix(ib) >= nctx
        ss = [ss_ref[0, h] for h in range(G_HEADS)]
        do_v = [jnp.where(is_lat, do_ref[:, sl], 0.0) for sl in sls]
        ds_in = [ds_scr[h] for h in range(G_HEADS)]
        _, vjp = jax.vjp(lambda *a: gdn_chunk(*a, rev), ss, qs, ks, vs, gcs, grs, bcs)
        ds, dq, dk, dv, dgc, dgr, dbc = vjp((ds_in, do_v))
        lane = _iota((CH, 128), 1)
        sub = _iota((128, CH), 0)
        dg = jnp.zeros((CH, 128), F32)
        dgt = jnp.zeros((128, CH), F32)
        db = jnp.zeros((CH, 128), F32)
        for h in range(G_HEADS):
            ds_scr[h] = ds[h]
            dq_ref[:, sls[h]] = dq[h]
            dk_ref[:, sls[h]] = dk[h]
            dv_ref[:, sls[h]] = dv[h]
            dg = dg + jnp.where(lane == cols_g[h], dgc[h], 0.0)
            dgt = dgt + jnp.where(sub == cols_g[h], dgr[h], 0.0)
            db = db + jnp.where(lane == cols_b[h], dbc[h], 0.0)
        dg_ref[...] = dg + dgt.T
        db_ref[...] = db

    return pl.pallas_call(
        body, grid=(nch,), name=name,
        in_specs=[full, full, full, wide, wide,
                  pl.BlockSpec((1, G_HEADS, DK, DK), lambda ib: (nch - 1 - ib, 0, 0, 0)), do_spec],
        out_specs=[full, full, full, wide, wide],
        out_shape=[jax.ShapeDtypeStruct((t, D), F32)] * 3 + [jax.ShapeDtypeStruct((t, 128), F32)] * 2,
        scratch_shapes=[pltpu.VMEM((G_HEADS, DK, DK), F32)],
        compiler_params=_cparams(("arbitrary",)),
    )(q, k, v, gcum, beta, ssave, do)


N_PAIRS = S_HEADS // 2


def _ssd_operands(x_ref, dt_ref, ac_ref, b_ref, c_ref):
    sls = [slice(p * 128, (p + 1) * 128) for p in range(N_PAIRS)]
    gsl = [slice(g * S_N, (g + 1) * S_N) for g in range(S_GROUPS)]
    return (sls, gsl, [x_ref[:, sl] for sl in sls], [dt_ref[:, sl] for sl in sls], [ac_ref[:, sl] for sl in sls],
            [b_ref[:, gs] for gs in gsl], [c_ref[:, gs] for gs in gsl])


def _ssd_scan_fwd(name, xs, dtx, acx, bm, cm, d, nctx):
    t = xs.shape[0]
    nch = t // CH
    rev = d == 1
    cix = lambda i: _chunk_index(i, nch, nctx, rev)
    inner = pl.BlockSpec((CH, S_INNER), lambda i: (cix(i), 0))
    grp = pl.BlockSpec((CH, S_GROUPS * S_N), lambda i: (cix(i), 0))

    def body(x_ref, dt_ref, ac_ref, b_ref, c_ref, y_ref, hs_ref, h_scr):
        @pl.when(pl.program_id(0) == 0)
        def _():
            h_scr[...] = jnp.zeros(h_scr.shape, F32)

        sls, _, x_l, dt_l, ac_l, b_l, c_l = _ssd_operands(x_ref, dt_ref, ac_ref, b_ref, c_ref)
        hts = [h_scr[p] for p in range(N_PAIRS)]
        h_new, y = ssd_chunk(hts, x_l, dt_l, ac_l, b_l, c_l, rev)
        for p in range(N_PAIRS):
            hs_ref[0, p] = hts[p]
            h_scr[p] = h_new[p]
            y_ref[:, sls[p]] = y[p]

    return pl.pallas_call(
        body, grid=(nch,), name=name,
        in_specs=[inner, inner, inner, grp, grp],
        out_specs=[inner, pl.BlockSpec((1, N_PAIRS, S_N, 128), lambda i: (i, 0, 0, 0))],
        out_shape=[jax.ShapeDtypeStruct((t, S_INNER), F32), jax.ShapeDtypeStruct((nch, N_PAIRS, S_N, 128), F32)],
        scratch_shapes=[pltpu.VMEM((N_PAIRS, S_N, 128), F32)],
        compiler_params=_cparams(("arbitrary",)),
    )(xs, dtx, acx, bm, cm)


def _ssd_scan_bwd(name, xs, dtx, acx, bm, cm, hsave, dy, d, nctx):
    t = xs.shape[0]
    nch = t // CH
    rev = d == 1
    cix = lambda ib: _chunk_index(nch - 1 - ib, nch, nctx, rev)
    inner = pl.BlockSpec((CH, S_INNER), lambda ib: (cix(ib), 0))
    grp = pl.BlockSpec((CH, S_GROUPS * S_N), lambda ib: (cix(ib), 0))
    dy_spec = pl.BlockSpec((CH, S_INNER), lambda ib: (jnp.maximum(cix(ib), nctx) - nctx, 0))

    def body(x_ref, dt_ref, ac_ref, b_ref, c_ref, hs_ref, dy_ref, dx_ref, ddt_ref, dac_ref, db_ref, dc_ref, dh_scr):
        ib = pl.program_id(0)

        @pl.when(ib == 0)
        def _():
            dh_scr[...] = jnp.zeros(dh_scr.shape, F32)

        sls, gsl, x_l, dt_l, ac_l, b_l, c_l = _ssd_operands(x_ref, dt_ref, ac_ref, b_ref, c_ref)
        is_lat = cix(ib) >= nctx
        hts = [hs_ref[0, p] for p in range(N_PAIRS)]
        dy_v = [jnp.where(is_lat, dy_ref[:, sl], 0.0) for sl in sls]
        dh_in = [dh_scr[p] for p in range(N_PAIRS)]
        _, vjp = jax.vjp(lambda *a: ssd_chunk(*a, rev), hts, x_l, dt_l, ac_l, b_l, c_l)
        dh, dx, ddt, dac, db, dc = vjp((dh_in, dy_v))
        for p in range(N_PAIRS):
            dh_scr[p] = dh[p]
            dx_ref[:, sls[p]] = dx[p]
            ddt_ref[:, sls[p]] = ddt[p]
            dac_ref[:, sls[p]] = dac[p]
        for g in range(S_GROUPS):
            db_ref[:, gsl[g]] = db[g]
            dc_ref[:, gsl[g]] = dc[g]

    return pl.pallas_call(
        body, grid=(nch,), name=name,
        in_specs=[inner, inner, inner, grp, grp,
                  pl.BlockSpec((1, N_PAIRS, S_N, 128), lambda ib: (nch - 1 - ib, 0, 0, 0)), dy_spec],
        out_specs=[inner, inner, inner, grp, grp],
        out_shape=[jax.ShapeDtypeStruct((t, S_INNER), F32)] * 3 + [jax.ShapeDtypeStruct((t, S_GROUPS * S_N), F32)] * 2,
        scratch_shapes=[pltpu.VMEM((N_PAIRS, S_N, 128), F32)],
        compiler_params=_cparams(("arbitrary",)),
    )(xs, dtx, acx, bm, cm, hsave, dy)


def _mesh_pos():
    return lax.axis_index("x"), lax.axis_index("y"), lax.axis_index("c")


def _all_gather(name, xs):
    r, cdim = xs.shape

    def body(x_ref, out_ref, send_sems, recv_sems, local_sem):
        x, y, c = _mesh_pos()
        me, sibling = (x, y, c), (x, y, 1 - c)
        chips = [(1 - x, y), (x, 1 - y), (1 - x, 1 - y)]

        def slot(px, py, pc):
            return out_ref.at[4 * px + 2 * py + pc]

        def copy(k, block, to, src=None):
            return pltpu.make_async_remote_copy(
                src_ref=slot(*block) if src is None else src, dst_ref=slot(*block),
                send_sem=send_sems.at[k], recv_sem=recv_sems.at[k], device_id=to, device_id_type=MESH)

        mine = pltpu.make_async_copy(x_ref, slot(*me), local_sem)
        mine.start()
        first = [copy(0, me, sibling, src=x_ref)]
        first += [copy(1 + j, me, (*chip, c), src=x_ref) for j, chip in enumerate(chips)]
        for cp in first:
            cp.start()
        passed = [copy(4 + j, (*chip, c), sibling) for j, chip in enumerate(chips)]
        for j, chip in enumerate(chips):
            copy(1 + j, (*chip, c), me).wait_recv()
            passed[j].start()
        copy(0, sibling, me).wait_recv()
        for j, chip in enumerate(chips):
            copy(4 + j, (*chip, 1 - c), me).wait_recv()
        for cp in first + passed:
            cp.wait_send()
        mine.wait()

    return pl.pallas_call(
        body, name=name,
        out_shape=jax.ShapeDtypeStruct((N_DEV, r, cdim), xs.dtype),
        in_specs=[pl.BlockSpec(memory_space=pl.ANY)],
        out_specs=pl.BlockSpec(memory_space=pl.ANY),
        scratch_shapes=[pltpu.SemaphoreType.DMA((7,)), pltpu.SemaphoreType.DMA((7,)), pltpu.SemaphoreType.DMA(())],
    )(xs)


def _all_to_all(name, g):
    def body(g_ref, out_ref, send_sems, recv_sems, local_sem):
        x, y, c = _mesh_pos()
        me_i = 4 * x + 2 * y + c
        mine = pltpu.make_async_copy(g_ref.at[me_i], out_ref.at[me_i], local_sem)
        mine.start()
        cps = []
        for k in range(1, N_DEV):
            px = 1 - x if (k >> 2) & 1 else x
            py = 1 - y if (k >> 1) & 1 else y
            pc = 1 - c if k & 1 else c
            cp = pltpu.make_async_remote_copy(
                src_ref=g_ref.at[4 * px + 2 * py + pc], dst_ref=out_ref.at[me_i],
                send_sem=send_sems.at[k - 1], recv_sem=recv_sems.at[k - 1],
                device_id=(px, py, pc), device_id_type=MESH)
            cp.start()
            cps.append(cp)
        for cp in cps:
            cp.wait()
        mine.wait()

    return pl.pallas_call(
        body, name=name,
        out_shape=jax.ShapeDtypeStruct(g.shape, g.dtype),
        in_specs=[pl.BlockSpec(memory_space=pl.ANY)],
        out_specs=pl.BlockSpec(memory_space=pl.ANY),
        scratch_shapes=[pltpu.SemaphoreType.DMA((7,)), pltpu.SemaphoreType.DMA((7,)), pltpu.SemaphoreType.DMA(())],
    )(g)


def _reduce_adam(name, recv, w, m, v, tm):
    rows, width = w.shape

    def body(recv_ref, w_ref, m_ref, v_ref, g_ref, d_ref, m2_ref, v2_ref):
        g = recv_ref[0]
        for s in range(1, N_DEV):
            g = g + recv_ref[s]
        delta, m2, v2 = f_adamw(w_ref[...], g, m_ref[...], v_ref[...])
        g_ref[...] = g
        d_ref[...] = delta
        m2_ref[...] = m2
        v2_ref[...] = v2

    row = pl.BlockSpec((tm, width), lambda i: (i, 0))
    return pl.pallas_call(
        body, grid=(rows // tm,), name=name,
        in_specs=[pl.BlockSpec((N_DEV, tm, width), lambda i: (0, i, 0)), row, row, row],
        out_specs=[row] * 4,
        out_shape=[jax.ShapeDtypeStruct((rows, width), F32)] * 4,
        compiler_params=_cparams(("arbitrary",)),
    )(recv, w, m, v)


BIG = ("w_in", "ada_w", "w_br_gdn", "w_br_ssm", "w_out", "w_ffn_in", "w_ffn_out", "gdn_conv_w", "ssm_conv_w")
BIG_ROWS = (1420, 768, 128, 256, 128, 704, 352, 2, 2)
BIG_SHARD_SHAPES = ((1024, 1420), (1024, 768), (128, 1024), (256, 1024), (128, 1024), (1024, 704), (352, 1024),
                    (3, 384), (3, 384))
COL_SHARDED = ("w_in", "ada_w", "w_ffn_in", "gdn_conv_w", "ssm_conv_w")
PACK_ROWS = sum(BIG_ROWS)
SMALL = ("c_ctx", "ada_b", "norm1_w", "gdn_conv_b", "gdn_a_log", "gdn_dt_bias", "gdn_norm_w", "ssm_conv_b",
         "ssm_a_log", "ssm_dt_bias", "ssm_d", "ssm_norm_w", "norm2_w", "norm_f_w")


def _to_rows(a):
    flat = a.reshape(-1)
    pad = (-flat.shape[0]) % PACK_W
    if pad:
        flat = jnp.pad(flat, (0, pad))
    return flat.reshape(-1, PACK_W)


def _pack(arrays):
    return jnp.concatenate([_to_rows(a) for a in arrays], axis=0)


def _unpack(buf, shapes):
    out, r0 = [], 0
    for shp in shapes:
        n = 1
        for s in shp:
            n *= s
        nr = -(-n // PACK_W)
        out.append(buf[r0:r0 + nr].reshape(-1)[:n].reshape(shp))
        r0 += nr
    return out


def _full_from_gathered(blocks, name, shard_shape):
    n = shard_shape[0] * shard_shape[1]
    sh = blocks.reshape(N_DEV, -1)[:, :n].reshape((N_DEV,) + shard_shape)
    if name in COL_SHARDED:
        return jnp.transpose(sh, (1, 0, 2)).reshape(shard_shape[0], N_DEV * shard_shape[1])
    return sh.reshape(N_DEV * shard_shape[0], shard_shape[1])


def _blocks_from_full(full, name, shard_shape):
    if name in COL_SHARDED:
        sh = jnp.transpose(full.reshape(shard_shape[0], N_DEV, shard_shape[1]), (1, 0, 2))
    else:
        sh = full.reshape((N_DEV,) + shard_shape)
    flat = sh.reshape(N_DEV, -1)
    pad = (-flat.shape[1]) % PACK_W
    if pad:
        flat = jnp.pad(flat, ((0, 0), (0, pad)))
    return flat.reshape(N_DEV, -1, PACK_W)


def _pad_cols(a, n):
    return jnp.pad(a, ((0, 0), (0, n - a.shape[1])))


def _w_cat(w_in):
    return jnp.concatenate([
        w_in[:, O_QKV:O_ZG], w_in[:, O_XBC:O_DT], w_in[:, O_ZS:O_XBC], w_in[:, O_GATE:O_END], w_in[:, O_ZG:O_AB],
        _pad_cols(w_in[:, O_AB:O_ZS], 128), _pad_cols(w_in[:, O_DT:O_GATE], 128)], axis=1)


def _w_uncat(wc):
    return jnp.concatenate([
        wc[:, C_QKV:C_XBC], wc[:, C_ZG:C_AB], wc[:, C_AB:C_AB + (O_ZS - O_AB)], wc[:, C_ZS:C_GATE], wc[:, C_XBC:C_ZS],
        wc[:, C_DT:C_DT + (O_GATE - O_DT)], wc[:, C_GATE:C_ZG]], axis=1)


def _pad_row(vec, n=128):
    vec = vec.reshape(1, -1)
    return _pad_cols(vec, n)


def kernel(x, c, ctx, c_ctx, ada_w, ada_b, norm1_w, w_in, gdn_conv_w, gdn_conv_b, gdn_a_log, gdn_dt_bias, gdn_norm_w, ssm_conv_w, ssm_conv_b, ssm_a_log, ssm_dt_bias, ssm_d, ssm_norm_w, w_br_gdn, w_br_ssm, w_out, norm2_w, w_ffn_in, w_ffn_out, norm_f_w, loss_target, m_c_ctx, m_ada_w, m_ada_b, m_norm1_w, m_w_in, m_gdn_conv_w, m_gdn_conv_b, m_gdn_a_log, m_gdn_dt_bias, m_gdn_norm_w, m_ssm_conv_w, m_ssm_conv_b, m_ssm_a_log, m_ssm_dt_bias, m_ssm_d, m_ssm_norm_w, m_w_br_gdn, m_w_br_ssm, m_w_out, m_norm2_w, m_w_ffn_in, m_w_ffn_out, m_norm_f_w, v_c_ctx, v_ada_w, v_ada_b, v_norm1_w, v_w_in, v_gdn_conv_w, v_gdn_conv_b, v_gdn_a_log, v_gdn_dt_bias, v_gdn_norm_w, v_ssm_conv_w, v_ssm_conv_b, v_ssm_a_log, v_ssm_dt_bias, v_ssm_d, v_ssm_norm_w, v_w_br_gdn, v_w_br_ssm, v_w_out, v_norm2_w, v_w_ffn_in, v_w_ffn_out, v_norm_f_w):
    wts = dict(c_ctx=c_ctx, ada_w=ada_w, ada_b=ada_b, norm1_w=norm1_w, w_in=w_in, gdn_conv_w=gdn_conv_w, gdn_conv_b=gdn_conv_b, gdn_a_log=gdn_a_log, gdn_dt_bias=gdn_dt_bias, gdn_norm_w=gdn_norm_w, ssm_conv_w=ssm_conv_w, ssm_conv_b=ssm_conv_b, ssm_a_log=ssm_a_log, ssm_dt_bias=ssm_dt_bias, ssm_d=ssm_d, ssm_norm_w=ssm_norm_w, w_br_gdn=w_br_gdn, w_br_ssm=w_br_ssm, w_out=w_out, norm2_w=norm2_w, w_ffn_in=w_ffn_in, w_ffn_out=w_ffn_out, norm_f_w=norm_f_w)
    mom1 = dict(c_ctx=m_c_ctx, ada_w=m_ada_w, ada_b=m_ada_b, norm1_w=m_norm1_w, w_in=m_w_in, gdn_conv_w=m_gdn_conv_w, gdn_conv_b=m_gdn_conv_b, gdn_a_log=m_gdn_a_log, gdn_dt_bias=m_gdn_dt_bias, gdn_norm_w=m_gdn_norm_w, ssm_conv_w=m_ssm_conv_w, ssm_conv_b=m_ssm_conv_b, ssm_a_log=m_ssm_a_log, ssm_dt_bias=m_ssm_dt_bias, ssm_d=m_ssm_d, ssm_norm_w=m_ssm_norm_w, w_br_gdn=m_w_br_gdn, w_br_ssm=m_w_br_ssm, w_out=m_w_out, norm2_w=m_norm2_w, w_ffn_in=m_w_ffn_in, w_ffn_out=m_w_ffn_out, norm_f_w=m_norm_f_w)
    mom2 = dict(c_ctx=v_c_ctx, ada_w=v_ada_w, ada_b=v_ada_b, norm1_w=v_norm1_w, w_in=v_w_in, gdn_conv_w=v_gdn_conv_w, gdn_conv_b=v_gdn_conv_b, gdn_a_log=v_gdn_a_log, gdn_dt_bias=v_gdn_dt_bias, gdn_norm_w=v_gdn_norm_w, ssm_conv_w=v_ssm_conv_w, ssm_conv_b=v_ssm_conv_b, ssm_a_log=v_ssm_a_log, ssm_dt_bias=v_ssm_dt_bias, ssm_d=v_ssm_d, ssm_norm_w=v_ssm_norm_w, w_br_gdn=v_w_br_gdn, w_br_ssm=v_w_br_ssm, w_out=v_w_out, norm2_w=v_norm2_w, w_ffn_in=v_w_ffn_in, w_ffn_out=v_w_ffn_out, norm_f_w=v_norm_f_w)
    order = list(wts)

    seq = x.shape[1]
    t = TM + seq
    ntl, nlt, nctx = t // TM, seq // TM, TM // CH

    w_pack = _pack([wts[n] for n in BIG])
    gathered = _all_gather("ag_weights", w_pack.astype(BF16))
    conv_rows = sum(BIG_ROWS[:7])
    gathered_cv = _all_gather("ag_conv", jnp.pad(w_pack[conv_rows:], ((0, 4), (0, 0))))
    full, r0 = {}, 0
    for n, nr, shp in zip(BIG, BIG_ROWS, BIG_SHARD_SHAPES):
        if n.endswith("conv_w"):
            off = 0 if n.startswith("gdn") else 2
            full[n] = _full_from_gathered(gathered_cv[:, off:off + 2], n, shp)
        else:
            full[n] = _full_from_gathered(gathered[:, r0:r0 + nr], n, shp)
        r0 += nr
    w_cat = _w_cat(full["w_in"])
    gcw = full["gdn_conv_w"].reshape(3, 1, XBC)
    scw = full["ssm_conv_w"].reshape(3, 1, XBC)

    n1w, n2w, nfw = norm1_w.reshape(1, D), norm2_w.reshape(1, D), norm_f_w.reshape(1, D)
    gcb, scb = gdn_conv_b.reshape(1, XBC), ssm_conv_b.reshape(1, XBC)
    alog16, dtb16 = _pad_row(gdn_a_log), _pad_row(gdn_dt_bias)
    alog64, dtb64 = _pad_row(ssm_a_log), _pad_row(ssm_dt_bias)
    gnw = gdn_norm_w.reshape(1, DK)
    ssd8 = jnp.tile(_pad_row(ssm_d), (8, 1))
    snw = ssm_norm_w.reshape(1, S_INNER)
    x2 = x[0]
    tgt = loss_target[0]
    xa = jnp.concatenate([ctx[0], x2], axis=0)
    cvec = jnp.concatenate([c, c_ctx.reshape(1, D), jnp.zeros((14, D), F32)], axis=0)

    a16 = _rowwise("silu_c", f_silu_rows, [(cvec, D, 0, 0)], [], [(D, BF16)], 1, tm=16)[0]
    mod = _mm("mm_mod", a16, full["ada_w"], "nn", F32) + ada_b
    sh1, sc1, g1, sh2, sc2, g2 = [mod[0:1, i * D:(i + 1) * D] for i in range(6)]
    csh1, csc1 = mod[1:2, 0:D], mod[1:2, D:2 * D]

    pre_pars = [n1w, sc1, sh1, csc1, csh1]
    a = _rowwise("pre", f_pre, [(xa, D, 0, 0)], pre_pars, [(D, BF16)], ntl, base=0)[0]
    proj = _mm("mm_proj", a, w_cat, "nn", F32)
    gp_rows = [(proj, XBC, C_QKV // XBC, 0), (proj, 128, C_AB // 128, 0)]
    gp_pars = [gcw, gcb, alog16, dtb16]
    q, k, v, gcum, beta = _rowwise("gdnprep", f_gdnprep, gp_rows, gp_pars, [(D, F32)] * 3 + [(128, F32)] * 2, ntl, base=0)
    sp_rows = [(proj, XBC, C_XBC // XBC, 0), (proj, 128, C_DT // 128, 0)]
    sp_pars = [scw, scb, alog64, dtb64]
    xs, bm, cm, dtx0, acx0, dtx1, acx1 = _rowwise(
        "ssmprep", f_ssmprep, sp_rows, sp_pars, [(S_INNER, F32), (512, F32), (512, F32)] + [(S_INNER, F32)] * 4, ntl, base=0)
    o0, ss0 = _gdn_scan_fwd("gdn_fwd0", q, k, v, gcum, beta, 0, nctx)
    o1, ss1 = _gdn_scan_fwd("gdn_fwd1", q, k, v, gcum, beta, 1, nctx)
    y0, hs0 = _ssd_scan_fwd("ssd_fwd0", xs, dtx0, acx0, bm, cm, 0, nctx)
    y1, hs1 = _ssd_scan_fwd("ssd_fwd1", xs, dtx1, acx1, bm, cm, 1, nctx)
    post_rows = [(o0, D, 0, 1), (o1, D, 0, 1), (proj, D, C_ZG // D, 1), (y0, S_INNER, 0, 1), (y1, S_INNER, 0, 1),
                 (xs, S_INNER, 0, 1), (proj, S_INNER, C_ZS // S_INNER, 1)]
    post_pars = [gnw, ssd8, snw]
    og, ys = _rowwise("post", f_post, post_rows, post_pars, [(D, BF16), (S_INNER, BF16)], nlt)
    pg = _mm("mm_pg", og, full["w_br_gdn"], "nn", F32)
    ps = _mm("mm_ps", ys, full["w_br_ssm"], "nn", F32)
    merge_rows = [(proj, S_INNER, C_GATE // S_INNER, 1), (pg, D, 0, 0), (ps, D, 0, 0)]
    merged = _rowwise("merge", f_merge, merge_rows, [], [(D, BF16)], nlt)[0]
    mix = _mm("mm_mix", merged, full["w_out"], "nn", F32)
    res_rows = [(x2, D, 0, 0), (mix, D, 0, 0)]
    res_pars = [g1, n2w, sc2, sh2]
    h1, f = _rowwise("res1", f_res1, res_rows, res_pars, [(D, F32), (D, BF16)], nlt)
    u = _mm("mm_u", f, full["w_ffn_in"], "nn", F32)
    hact = _rowwise("act", f_act, [(u, 2 * D_FF, 0, 0)], [], [(D_FF, BF16)], nlt)[0]
    ff = _mm("mm_ff", hact, full["w_ffn_out"], "nn", F32)

    fin_rows = [(h1, D, 0, 0), (ff, D, 0, 0), (tgt, D, 0, 0)]
    d_h1a, d_ff, d_g2, d_nfw, loss_acc = _rowwise_bwd(
        "final", f_final, fin_rows, [g2, nfw], ["one"], [(0, F32), (1, BF16)], [0, 1], nlt, loss_out=True)
    d_hact = _mm("mm_dhact", d_ff, full["w_ffn_out"], "nt", BF16)
    g_w_ffn_out = _mm("mm_gwffo", hact, d_ff, "tn", F32)
    d_u = _rowwise_bwd("act_bwd", f_act, [(u, 2 * D_FF, 0, 0)], [], [[(d_hact, D_FF, 0, 0)]], [(0, BF16)], [], nlt)[0]
    d_f = _mm("mm_df", d_u, full["w_ffn_in"], "nt", BF16)
    g_w_ffn_in = _mm("mm_gwffi", f, d_u, "tn", F32)
    d_xres, d_mix, d_g1, d_n2w, d_sc2, d_sh2 = _rowwise_bwd(
        "res1_bwd", f_res1, res_rows, res_pars, [[(d_h1a, D, 0, 0)], [(d_f, D, 0, 0)]], [(0, F32), (1, BF16)], [0, 1, 2, 3], nlt)
    d_merged = _mm("mm_dmerged", d_mix, full["w_out"], "nt", BF16)
    g_w_out = _mm("mm_gwout", merged, d_mix, "tn", F32)
    d_gate, d_pg, d_ps = _rowwise_bwd(
        "merge_bwd", f_merge, merge_rows, [], [[(d_merged, D, 0, 0)]], [(0, BF16), (1, BF16), (2, BF16)], [], nlt)
    d_og = _mm("mm_dog", d_pg, full["w_br_gdn"], "nt", BF16)
    g_w_br_gdn = _mm("mm_gwbrg", og, d_pg, "tn", F32)
    d_ys = _mm("mm_dys", d_ps, full["w_br_ssm"], "nt", BF16)
    g_w_br_ssm = _mm("mm_gwbrs", ys, d_ps, "tn", F32)
    d_o, d_zg, d_y, d_xs_post, d_zs, d_gnw, d_ssd8, d_snw = _rowwise_bwd(
        "post_bwd", f_post, post_rows, post_pars, [[(d_og, D, 0, 0)], [(d_ys, S_INNER, 0, 0)]],
        [(0, F32), (2, BF16), (3, F32), (5, F32), (6, BF16)], [0, 1, 2], nlt)
    dq0, dk0, dv0, dg0, db0 = _gdn_scan_bwd("gdn_bwd0", q, k, v, gcum, beta, ss0, d_o, 0, nctx)
    dq1, dk1, dv1, dg1, db1 = _gdn_scan_bwd("gdn_bwd1", q, k, v, gcum, beta, ss1, d_o, 1, nctx)
    dxs0, ddt0, dac0, dbm0, dcm0 = _ssd_scan_bwd("ssd_bwd0", xs, dtx0, acx0, bm, cm, hs0, d_y, 0, nctx)
    dxs1, ddt1, dac1, dbm1, dcm1 = _ssd_scan_bwd("ssd_bwd1", xs, dtx1, acx1, bm, cm, hs1, d_y, 1, nctx)
    row = lambda arr, wd: (arr, wd, 0, 0)
    d_qkv_raw, d_ab, d_gcw, d_gcb, d_alog16, d_dtb16 = _rowwise_bwd(
        "gdnprep_bwd", f_gdnprep, gp_rows, gp_pars,
        [[row(dq0, D), row(dq1, D)], [row(dk0, D), row(dk1, D)], [row(dv0, D), row(dv1, D)],
         [row(dg0, 128), row(dg1, 128)], [row(db0, 128), row(db1, 128)]],
        [(0, BF16), (1, BF16)], [0, 1, 2, 3], ntl, base=0)
    d_xbc_raw, d_dt, d_scw, d_scb, d_alog64, d_dtb64 = _rowwise_bwd(
        "ssmprep_bwd", f_ssmprep, sp_rows, sp_pars,
        [[row(dxs0, S_INNER), row(dxs1, S_INNER), (d_xs_post, S_INNER, 0, -1, True)], [row(dbm0, 512), row(dbm1, 512)],
         [row(dcm0, 512), row(dcm1, 512)], [row(ddt0, S_INNER)], [row(dac0, S_INNER)], [row(ddt1, S_INNER)],
         [row(dac1, S_INNER)]],
        [(0, BF16), (1, BF16)], [0, 1, 2, 3], ntl, base=0)
    ctx_zero = lambda wd: jnp.zeros((TM, wd), BF16)
    d_proj = jnp.concatenate([
        d_qkv_raw, d_xbc_raw, jnp.concatenate([ctx_zero(S_INNER), d_zs], axis=0),
        jnp.concatenate([ctx_zero(S_INNER), d_gate], axis=0), jnp.concatenate([ctx_zero(D), d_zg], axis=0), d_ab, d_dt], axis=1)
    d_a = _mm("mm_da", d_proj, w_cat, "nt", BF16)
    g_w_cat = _mm("mm_gwcat", a, d_proj, "tn", F32)
    d_xa, d_n1w, d_sc1, d_sh1, d_csc1, d_csh1 = _rowwise_bwd(
        "pre_bwd", f_pre_thru, [(xa, D, 0, 0)], pre_pars, [[row(d_a, D)], [(d_xres, D, 0, -1, True)]],
        [(0, F32)], [0, 1, 2, 3, 4], ntl, base=0)
    zero4 = jnp.zeros((1, 4 * D), F32)
    d_mod = jnp.concatenate([
        jnp.concatenate([d_sh1, d_sc1, d_g1, d_sh2, d_sc2, d_g2], axis=1),
        jnp.concatenate([d_csh1, d_csc1, zero4], axis=1), jnp.zeros((14, 6 * D), F32)], axis=0)
    g_ada_w = _mm("mm_gwada", a16, d_mod, "tn", F32)
    d_a16 = _mm("mm_da16", d_mod, full["ada_w"], "nt", F32)
    d_cvec = _rowwise_bwd("silu_c_bwd", f_silu_rows, [(cvec, D, 0, 0)], [], [[row(d_a16, D)]], [(0, F32)], [], 1, tm=16)[0]

    grads_full = dict(w_in=_w_uncat(g_w_cat), ada_w=g_ada_w, w_br_gdn=g_w_br_gdn, w_br_ssm=g_w_br_ssm, w_out=g_w_out,
                      w_ffn_in=g_w_ffn_in, w_ffn_out=g_w_ffn_out, gdn_conv_w=d_gcw.reshape(3, XBC),
                      ssm_conv_w=d_scw.reshape(3, XBC))
    g_pack = jnp.concatenate([_blocks_from_full(grads_full[n], n, shp) for n, shp in zip(BIG, BIG_SHARD_SHAPES)], axis=1)
    recv = _all_to_all("a2a_grads", g_pack)
    m_pack = _pack([mom1[n] for n in BIG])
    v_pack = _pack([mom2[n] for n in BIG])
    big_out = _reduce_adam("adam_big", recv, w_pack, m_pack, v_pack, 80)
    big_shapes = [wts[n].shape for n in BIG]
    big_un = [_unpack(buf, big_shapes) for buf in big_out]

    small_g = dict(c_ctx=d_cvec[1], ada_b=d_mod[0] + d_mod[1], norm1_w=d_n1w, gdn_conv_b=d_gcb,
                   gdn_a_log=d_alog16[0, :2 * G_HEADS], gdn_dt_bias=d_dtb16[0, :2 * G_HEADS], gdn_norm_w=d_gnw,
                   ssm_conv_b=d_scb, ssm_a_log=d_alog64[0, :2 * S_HEADS], ssm_dt_bias=d_dtb64[0, :2 * S_HEADS],
                   ssm_d=d_ssd8[0, :S_HEADS], ssm_norm_w=d_snw, norm2_w=d_n2w, norm_f_w=d_nfw)
    sg_pack = _pack([small_g[n] for n in SMALL])
    recv_s = _all_gather("ag_small_grads", sg_pack)
    small_out = _reduce_adam("adam_small", recv_s, _pack([wts[n] for n in SMALL]), _pack([mom1[n] for n in SMALL]),
                             _pack([mom2[n] for n in SMALL]), sg_pack.shape[0])
    small_shapes = [wts[n].shape for n in SMALL]
    small_un = [_unpack(buf, small_shapes) for buf in small_out]

    res = [{}, {}, {}, {}]
    for kind in range(4):
        for n, val in zip(BIG, big_un[kind]):
            res[kind][n] = val
        for n, val in zip(SMALL, small_un[kind]):
            res[kind][n] = val
    loss = lax.psum(loss_acc[0, 0], ("x", "y", "c"))
    grad_x = d_xa[TM:].reshape(x.shape)
    return (loss, grad_x, *[res[0][n] for n in order], *[res[1][n] for n in order], *[res[2][n] for n in order],
            *[res[3][n] for n in order])
```

```python
import functools

import jax
import jax.numpy as jnp
from jax import lax
from jax.experimental import pallas as pl
from jax.experimental.pallas import tpu as pltpu

F32 = jnp.float32
BF16 = jnp.bfloat16
HI = lax.Precision.HIGHEST
HIGH = lax.Precision.HIGH
MESH = pl.DeviceIdType.MESH

D = 1024
CH = 64
TM = 256
EPS = 1e-6
NEG = -1e30
G_HEADS = 8
DK = 128
S_HEADS = 32
S_P = 64
S_GROUPS = 4
S_N = 128
S_INNER = 2048
XBC = 3072
D_FF = 2816
N_DEV = 8
PACK_W = 1024
VMEM_LIMIT = 56 * 1024 * 1024

ADAM_LR = 0.001
ADAM_B1 = 0.9
ADAM_B2 = 0.999
ADAM_EPS = 1e-08
ADAM_WD = 0.01
ADAM_STEP = 10

C_QKV, C_XBC, C_ZS, C_GATE, C_ZG, C_AB, C_DT, C_END = 0, 3072, 6144, 8192, 10240, 11264, 11392, 11520
O_QKV, O_ZG, O_AB, O_ZS, O_XBC, O_DT, O_GATE, O_END = 0, 3072, 4096, 4128, 6176, 9248, 9312, 11360


def _dot(a, b, prec=None):
    return jnp.dot(a, b, precision=prec, preferred_element_type=F32)


def _dot_nt(a, b, prec=None):
    return lax.dot_general(a, b, (((1,), (1,)), ((), ())), precision=prec, preferred_element_type=F32)


def _dot_tn(a, b, prec=None):
    return lax.dot_general(a, b, (((0,), (0,)), ((), ())), precision=prec, preferred_element_type=F32)


def _iota(shape, dim):
    return lax.broadcasted_iota(jnp.int32, shape, dim)


def _rms(x):
    return x * lax.rsqrt(jnp.mean(x * x, axis=-1, keepdims=True) + EPS)


def _l2n(x):
    return x * lax.rsqrt(jnp.sum(x * x, axis=-1, keepdims=True) + EPS)


def _silu(x):
    return x * jax.nn.sigmoid(x)


def _softplus(x):
    return jnp.maximum(x, 0.0) + jnp.log1p(jnp.exp(-jnp.abs(x)))


def _roll_rows(x, s):
    return pltpu.roll(x, s, 0)


def _up_raw(x, keep_up):
    return jnp.where(keep_up > 0.0, _roll_rows(x, 1), 0.0)


def _dn_raw(x, keep_dn):
    return jnp.where(keep_dn > 0.0, _roll_rows(x, x.shape[0] - 1), 0.0)


@jax.custom_vjp
def _shift_up(x, keep_up, keep_dn):
    return _up_raw(x, keep_up)


def _shift_up_fwd(x, keep_up, keep_dn):
    return _up_raw(x, keep_up), (keep_up, keep_dn)


def _shift_up_bwd(res, g):
    keep_up, keep_dn = res
    return _dn_raw(g, keep_dn), jnp.zeros_like(keep_up), jnp.zeros_like(keep_dn)


_shift_up.defvjp(_shift_up_fwd, _shift_up_bwd)


@jax.custom_vjp
def _shift_dn(x, keep_up, keep_dn):
    return _dn_raw(x, keep_dn)


def _shift_dn_fwd(x, keep_up, keep_dn):
    return _dn_raw(x, keep_dn), (keep_up, keep_dn)


def _shift_dn_bwd(res, g):
    keep_up, keep_dn = res
    return _up_raw(g, keep_up), jnp.zeros_like(keep_up), jnp.zeros_like(keep_dn)


_shift_dn.defvjp(_shift_dn_fwd, _shift_dn_bwd)


def _conv_keep(is_ctx, n):
    r = _iota((n, 1), 0)
    pos = jnp.where(is_ctx, r, r & (CH - 1))
    end = jnp.where(is_ctx, n - 1, CH - 1)
    return jnp.where(pos == 0, 0.0, 1.0).astype(F32), jnp.where(pos == end, 0.0, 1.0).astype(F32)


def _conv_silu(u, w3, b, keep_up, keep_dn):
    conv = b + _shift_up(u, keep_up, keep_dn) * w3[0] + u * w3[1] + _shift_dn(u, keep_up, keep_dn) * w3[2]
    return _silu(conv)


def _chunk_tri(n, rev):
    i = _iota((n, n), 0)
    j = _iota((n, n), 1)
    same = (i // CH) == (j // CH)
    seen = (i <= j) if rev else (i >= j)
    return jnp.where(same & seen, 1.0, 0.0).astype(F32)


def _expand_mat(rows, cols, per, base):
    r = _iota((rows, cols), 0)
    c = _iota((rows, cols), 1)
    return jnp.where(r == base + c // per, 1.0, 0.0).astype(F32)


def f_silu_rows(is_ctx, cvec):
    return (_silu(cvec).astype(BF16),)


def f_pre(is_ctx, x, n1w, sc, sh, csc, csh):
    sc_e = jnp.where(is_ctx, csc, sc)
    sh_e = jnp.where(is_ctx, csh, sh)
    a = _rms(x) * n1w * (1.0 + sc_e) + sh_e
    return (a.astype(BF16),)


def f_pre_thru(is_ctx, x, n1w, sc, sh, csc, csh):
    return f_pre(is_ctx, x, n1w, sc, sh, csc, csh)[0], x


def f_gdnprep(is_ctx, qkv_raw, ab_raw, cw, cb, alog, dtb):
    n = qkv_raw.shape[0]
    keep_up, keep_dn = _conv_keep(is_ctx, n)
    s = _conv_silu(qkv_raw, cw, cb, keep_up, keep_dn)
    qs, ks, vs = [], [], []
    for h in range(G_HEADS):
        qs.append(_l2n(s[:, h * DK:(h + 1) * DK]) * (DK ** -0.5))
        ks.append(_l2n(s[:, D + h * DK:D + (h + 1) * DK]))
    q = jnp.concatenate(qs, axis=1)
    k = jnp.concatenate(ks, axis=1)
    v = s[:, 2 * D:3 * D]
    lane = _iota(ab_raw.shape, 1)
    g = jnp.where(lane < 2 * G_HEADS, -jnp.exp(alog) * _softplus(ab_raw + dtb), 0.0)
    gcum = jnp.where(lane < G_HEADS, _dot(_chunk_tri(n, False), g, HI), _dot(_chunk_tri(n, True), g, HI))
    beta = jax.nn.sigmoid(ab_raw)
    return q, k, v, gcum, beta


def f_ssmprep(is_ctx, xbc_raw, dt_raw, cw, cb, alog, dtb):
    n = xbc_raw.shape[0]
    keep_up, keep_dn = _conv_keep(is_ctx, n)
    s = _conv_silu(xbc_raw, cw, cb, keep_up, keep_dn)
    xs = s[:, :S_INNER]
    bm = s[:, S_INNER:S_INNER + S_GROUPS * S_N]
    cm = s[:, S_INNER + S_GROUPS * S_N:]
    lane = _iota(dt_raw.shape, 1)
    dt = jnp.where(lane < 2 * S_HEADS, _softplus(dt_raw + dtb), 0.0)
    da = dt * (-jnp.exp(alog))
    acum = jnp.where(lane < S_HEADS, _dot(_chunk_tri(n, False), da, HI), _dot(_chunk_tri(n, True), da, HI))
    return xs, bm, cm, dt, acum


def f_post(is_ctx, o_f, o_b, zg, y_f, y_b, xs, zs, gnw, ssd8, snw):
    o = o_f + o_b
    ogs = []
    for h in range(G_HEADS):
        sl = slice(h * DK, (h + 1) * DK)
        ogs.append(_rms(o[:, sl]) * gnw * _silu(zg[:, sl]))
    og = jnp.concatenate(ogs, axis=1)
    row0 = jnp.where(_iota(ssd8.shape, 0) == 0, 1.0, 0.0).astype(F32)
    dexp = jnp.sum(_dot(ssd8 * row0, _expand_mat(128, S_INNER, S_P, 0), HI), axis=0, keepdims=True)
    y = (y_f + y_b + dexp * xs) * _silu(zs)
    gw = S_INNER // S_GROUPS
    ys = jnp.concatenate([_rms(y[:, i * gw:(i + 1) * gw]) * snw[:, i * gw:(i + 1) * gw] for i in range(S_GROUPS)], axis=1)
    return og.astype(BF16), ys.astype(BF16)


def f_merge(is_ctx, gate, pg, ps):
    m = jax.nn.sigmoid(gate[:, :D]) * pg + jax.nn.sigmoid(gate[:, D:]) * ps
    return (m.astype(BF16),)


def f_res1(is_ctx, x, mix, g1, n2w, sc2, sh2):
    h1 = x + g1 * mix
    f = _rms(h1) * n2w * (1.0 + sc2) + sh2
    return h1, f.astype(BF16)


def f_act(is_ctx, u):
    return ((_silu(u[:, :D_FF]) * u[:, D_FF:]).astype(BF16),)


def f_final(is_ctx, h1, ff, tgt, g2, nfw):
    h2 = h1 + g2 * ff
    y = _rms(h2) * nfw
    err = y - tgt
    return (0.5 * jnp.sum(jnp.mean(err * err, axis=-1, keepdims=True), axis=0, keepdims=True),)


def _each(fn, *lists):
    return [fn(*args) for args in zip(*lists)]


def _tri_inverse_all(mats):
    n = mats[0].shape[0]
    eye = jnp.where(_iota((n, n), 0) == _iota((n, n), 1), 1.0, 0.0).astype(F32)
    t = [eye - a for a in mats]
    p = [_dot(a, a, HIGH) for a in mats]
    for r in range(5):
        t = _each(lambda t_, p_: t_ + _dot(t_, p_, HIGH), t, p)
        if r < 4:
            p = [_dot(p_, p_, HIGH) for p_ in p]
    return t


@jax.custom_vjp
def _inverse_given(a, t):
    return t


def _inverse_given_fwd(a, t):
    return t, t


def _inverse_given_bwd(t, g):
    return -_dot_nt(_dot_tn(t, g, HIGH), t, HIGH), jnp.zeros_like(t)


_inverse_given.defvjp(_inverse_given_fwd, _inverse_given_bwd)


def gdn_chunk(ss, qs, ks, vs, gcs, grs, bcs, rev, t_known=None):
    c = qs[0].shape[0]
    ii = _iota((c, c), 0)
    jj = _iota((c, c), 1)
    incl = (ii <= jj) if rev else (ii >= jj)
    strict = (ii < jj) if rev else (ii > jj)
    last = 0 if rev else c - 1
    is_last = _iota((c, 1), 0) == last
    decay = _each(lambda gc, gr: jnp.exp(jnp.where(incl, gc - gr, NEG)), gcs, grs)
    kb = _each(lambda k, bc: k * bc, ks, bcs)
    a = _each(lambda kb_, k, dc: jnp.where(strict, _dot_nt(kb_, k) * dc, 0.0), kb, ks, decay)
    t = _tri_inverse_all(a) if t_known is None else _each(_inverse_given, a, t_known)
    eg = [jnp.exp(gc) for gc in gcs]
    rhs = _each(lambda kb_, eg_, v, bc: jnp.concatenate([kb_ * eg_, v * bc], axis=1), kb, eg, vs, bcs)
    wu = _each(lambda t_, r: _dot(t_, r, HIGH), t, rhs)
    lhs = _each(lambda wu_, q, eg_: jnp.concatenate([wu_[:, :DK], q * eg_], axis=0), wu, qs, eg)
    ws = _each(_dot, lhs, ss)
    v_new = _each(lambda wu_, ws_: wu_[:, DK:] - ws_[:c], wu, ws)
    attn = _each(lambda q, k, dc: _dot_nt(q, k) * dc, qs, ks, decay)
    o = _each(lambda ws_, at, vn: ws_[c:] + _dot(at, vn), ws, attn, v_new)
    gtot = [jnp.sum(jnp.where(is_last, gc, 0.0), axis=0, keepdims=True) for gc in gcs]
    s_new = _each(lambda s, k, gc, gt_, vn: s * jnp.exp(gt_) + _dot_tn(k * jnp.exp(gt_ - gc), vn), ss, ks, gcs, gtot, v_new)
    return s_new, o, t


def ssd_pick(dt0s, dt1s, ac0s, ac1s, ar0s, ar1s):
    lo = _iota((CH, 128), 1) < S_P
    pick = lambda u0, u1: jnp.where(lo, u0, u1)
    return _each(pick, dt0s, dt1s), _each(pick, ac0s, ac1s), _each(pick, ar0s, ar1s)


def ssd_chunk(hts, xs, dts, acs, acr, bgs, cgs, rev):
    c = xs[0].shape[0]
    npair = len(xs)
    grp = [p * len(bgs) // npair for p in range(npair)]
    lane = _iota((c, 128), 1)
    ii = _iota((c, 128), 0)
    jl = lane & (S_P - 1)
    lo = lane < S_P
    seen = (ii <= jl) if rev else (ii >= jl)
    last = 0 if rev else c - 1
    split = lambda z: jnp.concatenate([jnp.where(lo, z, 0.0), jnp.where(lo, 0.0, z)], axis=0)
    cb = _each(lambda bg, cg: _dot_nt(cg, jnp.concatenate([bg, bg], axis=0)), bgs, cgs)
    seg = _each(lambda ac, ar: jnp.exp(jnp.where(seen, ac - ar, NEG)), acs, acr)
    xdt = _each(lambda x, dt: x * dt, xs, dts)
    ydiag = [_dot(cb[grp[p]] * seg[p], split(xdt[p])) for p in range(npair)]
    yoff = [_dot(cgs[grp[p]], hts[p]) * jnp.exp(acs[p]) for p in range(npair)]
    atot = [jnp.sum(jnp.where(ii == last, ac, 0.0), axis=0, keepdims=True) for ac in acs]
    h_new = [hts[p] * jnp.exp(atot[p]) + _dot_tn(bgs[grp[p]], xdt[p] * jnp.exp(atot[p] - acs[p])) for p in range(npair)]
    return h_new, _each(lambda a_, b_: a_ + b_, ydiag, yoff)


def f_adamw(w, g, m, v):
    m = ADAM_B1 * m + (1.0 - ADAM_B1) * g
    v = ADAM_B2 * v + (1.0 - ADAM_B2) * jnp.square(g)
    m_hat = m / (1.0 - ADAM_B1 ** ADAM_STEP)
    v_hat = v / (1.0 - ADAM_B2 ** ADAM_STEP)
    delta = -ADAM_LR * (m_hat / (jnp.sqrt(v_hat) + ADAM_EPS) + ADAM_WD * w)
    return delta, m, v


def _cparams(sem):
    return pltpu.CompilerParams(dimension_semantics=sem, vmem_limit_bytes=VMEM_LIMIT)


def _pick(n, target):
    if n <= target:
        return n
    best = None
    for t in range(128, target + 1, 128):
        if n % t == 0:
            best = t
    assert best is not None, (n, target)
    return best


def _row_spec(tm, width, colblk, rowoff):
    return pl.BlockSpec((tm, width), lambda i: (i + rowoff, colblk))


def _par_spec(shape):
    nd = len(shape)
    return pl.BlockSpec(tuple(shape), lambda i: (0,) * nd)


def _rowwise(name, fn, rows, pars, outs, ntiles, base=1, tm=TM):
    nr, npar = len(rows), len(pars)

    def body(*refs):
        is_ctx = (pl.program_id(0) + base) == 0
        res = fn(is_ctx, *[r[...] for r in refs[:nr]], *[p[...] for p in refs[nr:nr + npar]])
        for o_ref, r in zip(refs[nr + npar:], res):
            o_ref[...] = r.astype(o_ref.dtype)

    return pl.pallas_call(
        body, grid=(ntiles,), name=name,
        in_specs=[_row_spec(tm, wd, cb, ro) for (_, wd, cb, ro) in rows] + [_par_spec(p.shape) for p in pars],
        out_specs=[_row_spec(tm, wd, 0, 0) for (wd, _) in outs],
        out_shape=[jax.ShapeDtypeStruct((ntiles * tm, wd), dt) for (wd, dt) in outs],
        compiler_params=_cparams(("arbitrary",)),
    )(*[r[0] for r in rows], *pars)


def _ct_spec(tm, desc):
    _, wd, cb, ro = desc[:4]
    if len(desc) > 4 and desc[4]:
        return pl.BlockSpec((tm, wd), lambda i: (jnp.maximum(i + ro, 0), cb))
    return _row_spec(tm, wd, cb, ro)


def _rowwise_bwd(name, fn, rows, pars, cts, drows, dpars, ntiles, base=1, loss_out=False, tm=TM):
    nr, npar = len(rows), len(pars)
    ct_rows = [d for ct in cts if isinstance(ct, list) for d in ct]
    nct = len(ct_rows)

    def body(*refs):
        i = pl.program_id(0)
        is_ctx = (i + base) == 0
        rows_v = [r[...] for r in refs[:nr]]
        pars_v = [p[...] for p in refs[nr:nr + npar]]
        ct_refs = list(refs[nr + npar:nr + npar + nct])
        out_refs = list(refs[nr + npar + nct:])
        outs, vjp = jax.vjp(lambda rv, pv: fn(is_ctx, *rv, *pv), rows_v, pars_v)

        def ct_value(desc):
            val = ct_refs.pop(0)[...].astype(F32)
            if len(desc) > 4 and desc[4]:
                val = jnp.where(is_ctx, 0.0, val)
            return val

        ct_vals = []
        for o, ct in zip(outs, cts):
            if ct is None:
                ct_vals.append(jnp.zeros_like(o))
            elif isinstance(ct, str):
                ct_vals.append(jnp.ones_like(o))
            else:
                acc = ct_value(ct[0])
                for desc in ct[1:]:
                    acc = acc + ct_value(desc)
                ct_vals.append(acc.astype(o.dtype))
        d_rows, d_pars = vjp(tuple(ct_vals))
        for (ri, _), o_ref in zip(drows, out_refs[:len(drows)]):
            o_ref[...] = d_rows[ri].astype(o_ref.dtype)
        acc_refs = out_refs[len(drows):]
        acc_vals = [d_pars[pi] for pi in dpars]
        if loss_out:
            acc_vals.append(jnp.broadcast_to(outs[0], (8, 128)))

        @pl.when(i == 0)
        def _():
            for o_ref, val in zip(acc_refs, acc_vals):
                o_ref[...] = val

        @pl.when(i > 0)
        def _():
            for o_ref, val in zip(acc_refs, acc_vals):
                o_ref[...] += val

    acc_shapes = [pars[pi].shape for pi in dpars] + ([(8, 128)] if loss_out else [])
    return pl.pallas_call(
        body, grid=(ntiles,), name=name,
        in_specs=[_row_spec(tm, wd, cb, ro) for (_, wd, cb, ro) in rows] + [_par_spec(p.shape) for p in pars]
        + [_ct_spec(tm, d) for d in ct_rows],
        out_specs=[_row_spec(tm, rows[ri][1], 0, 0) for (ri, _) in drows] + [_par_spec(s) for s in acc_shapes],
        out_shape=[jax.ShapeDtypeStruct((ntiles * tm, rows[ri][1]), dt) for (ri, dt) in drows]
        + [jax.ShapeDtypeStruct(tuple(s), F32) for s in acc_shapes],
        compiler_params=_cparams(("arbitrary",)),
    )(*[r[0] for r in rows], *pars, *[r[0] for r in ct_rows])


def _mm(name, a, b, mode, out_dtype, tm=768, tn=1024, tk=1024):
    if mode == "nn":
        (m, kd), (_, n) = a.shape, b.shape
    elif mode == "nt":
        (m, kd), (n, _) = a.shape, b.shape
    else:
        (kd, m), (_, n) = a.shape, b.shape
    tm, tn, tk = _pick(m, tm), _pick(n, tn), _pick(kd, tk)
    nk = kd // tk
    a_spec = {"nn": pl.BlockSpec((tm, tk), lambda i, j, k: (i, k)), "nt": pl.BlockSpec((tm, tk), lambda i, j, k: (i, k)),
              "tn": pl.BlockSpec((tk, tm), lambda i, j, k: (k, i))}[mode]
    b_spec = {"nn": pl.BlockSpec((tk, tn), lambda i, j, k: (k, j)), "nt": pl.BlockSpec((tn, tk), lambda i, j, k: (j, k)),
              "tn": pl.BlockSpec((tk, tn), lambda i, j, k: (k, j))}[mode]
    dot = {"nn": _dot, "nt": _dot_nt, "tn": _dot_tn}[mode]

    def body(a_ref, b_ref, o_ref, acc_ref):
        k = pl.program_id(2)
        part = dot(a_ref[...].astype(BF16), b_ref[...].astype(BF16))

        @pl.when(k == 0)
        def _():
            acc_ref[...] = part

        @pl.when(k > 0)
        def _():
            acc_ref[...] += part

        @pl.when(k == nk - 1)
        def _():
            o_ref[...] = acc_ref[...].astype(o_ref.dtype)

    return pl.pallas_call(
        body, grid=(m // tm, n // tn, nk), name=name,
        in_specs=[a_spec, b_spec],
        out_specs=pl.BlockSpec((tm, tn), lambda i, j, k: (i, j)),
        out_shape=jax.ShapeDtypeStruct((m, n), out_dtype),
        scratch_shapes=[pltpu.VMEM((tm, tn), F32)],
        compiler_params=_cparams(("arbitrary", "arbitrary", "arbitrary")),
    )(a, b)


def _chunk_index(i, nch, nctx, rev):
    if not rev:
        return i
    return jnp.where(i < nctx, nctx - 1 - i, nch + nctx - 1 - i)


def _gdn_cols(d):
    return [d * G_HEADS + h for h in range(G_HEADS)], [2 * G_HEADS + d * G_HEADS + h for h in range(G_HEADS)]


def _gdn_operands(q_ref, k_ref, v_ref, g_ref, b_ref, d):
    cols_g, cols_b = _gdn_cols(d)
    sls = [slice(h * DK, (h + 1) * DK) for h in range(G_HEADS)]
    gt, bt = g_ref[...], b_ref[...]
    gtt = gt.T
    qs = [q_ref[:, sl] for sl in sls]
    ks = [k_ref[:, sl] for sl in sls]
    vs = [v_ref[:, sl] for sl in sls]
    gcs = [gt[:, cg:cg + 1] for cg in cols_g]
    grs = [gtt[cg:cg + 1, :] for cg in cols_g]
    bcs = [bt[:, cb:cb + 1] for cb in cols_b]
    return sls, qs, ks, vs, gcs, grs, bcs


def _gdn_scan_fwd(name, q, k, v, gcum, beta, d, nctx):
    t = q.shape[0]
    nch = t // CH
    rev = d == 1
    cix = lambda i: _chunk_index(i, nch, nctx, rev)
    full = pl.BlockSpec((CH, D), lambda i: (cix(i), 0))
    wide = pl.BlockSpec((CH, 128), lambda i: (cix(i), 0))

    def body(q_ref, k_ref, v_ref, g_ref, b_ref, o_ref, ss_ref, ts_ref, s_scr):
        @pl.when(pl.program_id(0) == 0)
        def _():
            s_scr[...] = jnp.zeros(s_scr.shape, F32)

        sls, qs, ks, vs, gcs, grs, bcs = _gdn_operands(q_ref, k_ref, v_ref, g_ref, b_ref, d)
        ss = [s_scr[h] for h in range(G_HEADS)]
        s_new, o, t = gdn_chunk(ss, qs, ks, vs, gcs, grs, bcs, rev)
        for h in range(G_HEADS):
            ss_ref[0, h] = ss[h]
            ts_ref[0, h] = t[h]
            s_scr[h] = s_new[h]
            o_ref[:, sls[h]] = o[h]

    return pl.pallas_call(
        body, grid=(nch,), name=name,
        in_specs=[full, full, full, wide, wide],
        out_specs=[full, pl.BlockSpec((1, G_HEADS, DK, DK), lambda i: (i, 0, 0, 0)),
                   pl.BlockSpec((1, G_HEADS, CH, CH), lambda i: (i, 0, 0, 0))],
        out_shape=[jax.ShapeDtypeStruct((t, D), F32), jax.ShapeDtypeStruct((nch, G_HEADS, DK, DK), F32),
                   jax.ShapeDtypeStruct((nch, G_HEADS, CH, CH), F32)],
        scratch_shapes=[pltpu.VMEM((G_HEADS, DK, DK), F32)],
        compiler_params=_cparams(("arbitrary",)),
    )(q, k, v, gcum, beta)


def _gdn_scan_bwd(name, q, k, v, gcum, beta, ssave, tsave, do, d, nctx):
    t = q.shape[0]
    nch = t // CH
    rev = d == 1
    cix = lambda ib: _chunk_index(nch - 1 - ib, nch, nctx, rev)
    full = pl.BlockSpec((CH, D), lambda ib: (cix(ib), 0))
    wide = pl.BlockSpec((CH, 128), lambda ib: (cix(ib), 0))
    do_spec = pl.BlockSpec((CH, D), lambda ib: (jnp.maximum(cix(ib), nctx) - nctx, 0))

    def body(q_ref, k_ref, v_ref, g_ref, b_ref, ss_ref, ts_ref, do_ref, dq_ref, dk_ref, dv_ref, dg_ref, db_ref, ds_scr):
        ib = pl.program_id(0)

        @pl.when(ib == 0)
        def _():
            ds_scr[...] = jnp.zeros(ds_scr.shape, F32)

        sls, qs, ks, vs, gcs, grs, bcs = _gdn_operands(q_ref, k_ref, v_ref, g_ref, b_ref, d)
        t_known = [ts_ref[0, h] for h in range(G_HEADS)]
        cols_g, cols_b = _gdn_cols(d)
        is_lat = cix(ib) >= nctx
        ss = [ss_ref[0, h] for h in range(G_HEADS)]
        do_v = [jnp.where(is_lat, do_ref[:, sl], 0.0) for sl in sls]
        ds_in = [ds_scr[h] for h in range(G_HEADS)]
        _, vjp = jax.vjp(lambda *a: gdn_chunk(*a, rev, t_known)[:2], ss, qs, ks, vs, gcs, grs, bcs)
        ds, dq, dk, dv, dgc, dgr, dbc = vjp((ds_in, do_v))
        lane = _iota((CH, 128), 1)
        sub = _iota((128, CH), 0)
        dg = jnp.zeros((CH, 128), F32)
        dgt = jnp.zeros((128, CH), F32)
        db = jnp.zeros((CH, 128), F32)
        for h in range(G_HEADS):
            ds_scr[h] = ds[h]
            dq_ref[:, sls[h]] = dq[h]
            dk_ref[:, sls[h]] = dk[h]
            dv_ref[:, sls[h]] = dv[h]
            dg = dg + jnp.where(lane == cols_g[h], dgc[h], 0.0)
            dgt = dgt + jnp.where(sub == cols_g[h], dgr[h], 0.0)
            db = db + jnp.where(lane == cols_b[h], dbc[h], 0.0)
        dg_ref[...] = dg + dgt.T
        db_ref[...] = db

    return pl.pallas_call(
        body, grid=(nch,), name=name,
        in_specs=[full, full, full, wide, wide,
                  pl.BlockSpec((1, G_HEADS, DK, DK), lambda ib: (nch - 1 - ib, 0, 0, 0)),
                  pl.BlockSpec((1, G_HEADS, CH, CH), lambda ib: (nch - 1 - ib, 0, 0, 0)), do_spec],
        out_specs=[full, full, full, wide, wide],
        out_shape=[jax.ShapeDtypeStruct((t, D), F32)] * 3 + [jax.ShapeDtypeStruct((t, 128), F32)] * 2,
        scratch_shapes=[pltpu.VMEM((G_HEADS, DK, DK), F32)],
        compiler_params=_cparams(("arbitrary",)),
    )(q, k, v, gcum, beta, ssave, tsave, do)


N_PAIRS = S_HEADS // 2


def _ssd_operands(x_ref, dt_ref, ac_ref, b_ref, c_ref, d):
    sls = [slice(p * 128, (p + 1) * 128) for p in range(N_PAIRS)]
    gsl = [slice(g * S_N, (g + 1) * S_N) for g in range(S_GROUPS)]
    cols = [d * S_HEADS + h for h in range(S_HEADS)]
    dtc, acc = dt_ref[...], ac_ref[...]
    act = jnp.concatenate([acc, acc], axis=0).T
    col = lambda z, cc: z[:, cc:cc + 1]
    dts, acs, acr = ssd_pick(
        [col(dtc, cols[2 * p]) for p in range(N_PAIRS)], [col(dtc, cols[2 * p + 1]) for p in range(N_PAIRS)],
        [col(acc, cols[2 * p]) for p in range(N_PAIRS)], [col(acc, cols[2 * p + 1]) for p in range(N_PAIRS)],
        [act[cols[2 * p]:cols[2 * p] + 1, :] for p in range(N_PAIRS)],
        [act[cols[2 * p + 1]:cols[2 * p + 1] + 1, :] for p in range(N_PAIRS)])
    ops = ([x_ref[:, sl] for sl in sls], dts, acs, acr, [b_ref[:, gs] for gs in gsl], [c_ref[:, gs] for gs in gsl])
    return sls, gsl, cols, ops


def _ssd_scan_fwd(name, xs, dtc, acc, bm, cm, d, nctx):
    t = xs.shape[0]
    nch = t // CH
    rev = d == 1
    cix = lambda i: _chunk_index(i, nch, nctx, rev)
    inner = pl.BlockSpec((CH, S_INNER), lambda i: (cix(i), 0))
    wide = pl.BlockSpec((CH, 128), lambda i: (cix(i), 0))
    grp = pl.BlockSpec((CH, S_GROUPS * S_N), lambda i: (cix(i), 0))

    def body(x_ref, dt_ref, ac_ref, b_ref, c_ref, y_ref, hs_ref, h_scr):
        @pl.when(pl.program_id(0) == 0)
        def _():
            h_scr[...] = jnp.zeros(h_scr.shape, F32)

        sls, _, _, ops = _ssd_operands(x_ref, dt_ref, ac_ref, b_ref, c_ref, d)
        hts = [h_scr[p] for p in range(N_PAIRS)]
        h_new, y = ssd_chunk(hts, *ops, rev)
        for p in range(N_PAIRS):
            hs_ref[0, p] = hts[p]
            h_scr[p] = h_new[p]
            y_ref[:, sls[p]] = y[p]

    return pl.pallas_call(
        body, grid=(nch,), name=name,
        in_specs=[inner, wide, wide, grp, grp],
        out_specs=[inner, pl.BlockSpec((1, N_PAIRS, S_N, 128), lambda i: (i, 0, 0, 0))],
        out_shape=[jax.ShapeDtypeStruct((t, S_INNER), F32), jax.ShapeDtypeStruct((nch, N_PAIRS, S_N, 128), F32)],
        scratch_shapes=[pltpu.VMEM((N_PAIRS, S_N, 128), F32)],
        compiler_params=_cparams(("arbitrary",)),
    )(xs, dtc, acc, bm, cm)


def _ssd_scan_bwd(name, xs, dtc, acc, bm, cm, hsave, dy, d, nctx):
    t = xs.shape[0]
    nch = t // CH
    rev = d == 1
    cix = lambda ib: _chunk_index(nch - 1 - ib, nch, nctx, rev)
    inner = pl.BlockSpec((CH, S_INNER), lambda ib: (cix(ib), 0))
    wide = pl.BlockSpec((CH, 128), lambda ib: (cix(ib), 0))
    grp = pl.BlockSpec((CH, S_GROUPS * S_N), lambda ib: (cix(ib), 0))
    dy_spec = pl.BlockSpec((CH, S_INNER), lambda ib: (jnp.maximum(cix(ib), nctx) - nctx, 0))

    def body(x_ref, dt_ref, ac_ref, b_ref, c_ref, hs_ref, dy_ref, dx_ref, ddt_ref, dac_ref, db_ref, dc_ref, dh_scr):
        ib = pl.program_id(0)

        @pl.when(ib == 0)
        def _():
            dh_scr[...] = jnp.zeros(dh_scr.shape, F32)

        sls, gsl, cols, ops = _ssd_operands(x_ref, dt_ref, ac_ref, b_ref, c_ref, d)
        is_lat = cix(ib) >= nctx
        hts = [hs_ref[0, p] for p in range(N_PAIRS)]
        dy_v = [jnp.where(is_lat, dy_ref[:, sl], 0.0) for sl in sls]
        dh_in = [dh_scr[p] for p in range(N_PAIRS)]
        _, vjp = jax.vjp(lambda *a: ssd_chunk(*a, rev), hts, *ops)
        dh, dx, ddts, dacs, dacr, db, dc = vjp((dh_in, dy_v))
        for p in range(N_PAIRS):
            dh_scr[p] = dh[p]
            dx_ref[:, sls[p]] = dx[p]
        r = _iota((S_INNER, 128), 0)
        e_t = jnp.where(_iota((S_INNER, 128), 1) == d * S_HEADS + r // S_P, 1.0, 0.0).astype(F32)
        ddt_ref[...] = _dot(jnp.concatenate(ddts, axis=1), e_t, HIGH)
        dac_cols = _dot(jnp.concatenate(dacs, axis=1), e_t, HIGH)
        sub = _iota((128, 128), 0)
        lane = _iota((128, 128), 1)
        m = jnp.zeros((128, 128), F32)
        for p in range(N_PAIRS):
            m = m + jnp.where(sub == p, jnp.sum(dacr[p], axis=0, keepdims=True), 0.0)
        mt = m.T
        s0 = jnp.where(lane == d * S_HEADS + 2 * sub, 1.0, 0.0).astype(F32)
        s1 = jnp.where(lane == d * S_HEADS + 2 * sub + 1, 1.0, 0.0).astype(F32)
        dac_ref[...] = dac_cols + _dot(mt[:CH], s0, HIGH) + _dot(mt[CH:], s1, HIGH)
        for g in range(S_GROUPS):
            db_ref[:, gsl[g]] = db[g]
            dc_ref[:, gsl[g]] = dc[g]

    return pl.pallas_call(
        body, grid=(nch,), name=name,
        in_specs=[inner, wide, wide, grp, grp,
                  pl.BlockSpec((1, N_PAIRS, S_N, 128), lambda ib: (nch - 1 - ib, 0, 0, 0)), dy_spec],
        out_specs=[inner, wide, wide, grp, grp],
        out_shape=[jax.ShapeDtypeStruct((t, S_INNER), F32)] + [jax.ShapeDtypeStruct((t, 128), F32)] * 2
        + [jax.ShapeDtypeStruct((t, S_GROUPS * S_N), F32)] * 2,
        scratch_shapes=[pltpu.VMEM((N_PAIRS, S_N, 128), F32)],
        compiler_params=_cparams(("arbitrary",)),
    )(xs, dtc, acc, bm, cm, hsave, dy)


def _mesh_pos():
    return lax.axis_index("x"), lax.axis_index("y"), lax.axis_index("c")


def _all_gather(name, xs):
    r, cdim = xs.shape

    def body(x_ref, out_ref, send_sems, recv_sems, local_sem):
        x, y, c = _mesh_pos()
        me, sibling = (x, y, c), (x, y, 1 - c)
        chips = [(1 - x, y), (x, 1 - y), (1 - x, 1 - y)]

        def slot(px, py, pc):
            return out_ref.at[4 * px + 2 * py + pc]

        def copy(k, block, to, src=None):
            return pltpu.make_async_remote_copy(
                src_ref=slot(*block) if src is None else src, dst_ref=slot(*block),
                send_sem=send_sems.at[k], recv_sem=recv_sems.at[k], device_id=to, device_id_type=MESH)

        mine = pltpu.make_async_copy(x_ref, slot(*me), local_sem)
        mine.start()
        first = [copy(0, me, sibling, src=x_ref)]
        first += [copy(1 + j, me, (*chip, c), src=x_ref) for j, chip in enumerate(chips)]
        for cp in first:
            cp.start()
        passed = [copy(4 + j, (*chip, c), sibling) for j, chip in enumerate(chips)]
        for j, chip in enumerate(chips):
            copy(1 + j, (*chip, c), me).wait_recv()
            passed[j].start()
        copy(0, sibling, me).wait_recv()
        for j, chip in enumerate(chips):
            copy(4 + j, (*chip, 1 - c), me).wait_recv()
        for cp in first + passed:
            cp.wait_send()
        mine.wait()

    return pl.pallas_call(
        body, name=name,
        out_shape=jax.ShapeDtypeStruct((N_DEV, r, cdim), xs.dtype),
        in_specs=[pl.BlockSpec(memory_space=pl.ANY)],
        out_specs=pl.BlockSpec(memory_space=pl.ANY),
        scratch_shapes=[pltpu.SemaphoreType.DMA((7,)), pltpu.SemaphoreType.DMA((7,)), pltpu.SemaphoreType.DMA(())],
    )(xs)


def _all_to_all(name, g):
    def body(g_ref, out_ref, send_sems, recv_sems, local_sem):
        x, y, c = _mesh_pos()
        me_i = 4 * x + 2 * y + c
        mine = pltpu.make_async_copy(g_ref.at[me_i], out_ref.at[me_i], local_sem)
        mine.start()
        cps = []
        for k in range(1, N_DEV):
            px = 1 - x if (k >> 2) & 1 else x
            py = 1 - y if (k >> 1) & 1 else y
            pc = 1 - c if k & 1 else c
            cp = pltpu.make_async_remote_copy(
                src_ref=g_ref.at[4 * px + 2 * py + pc], dst_ref=out_ref.at[me_i],
                send_sem=send_sems.at[k - 1], recv_sem=recv_sems.at[k - 1],
                device_id=(px, py, pc), device_id_type=MESH)
            cp.start()
            cps.append(cp)
        for cp in cps:
            cp.wait()
        mine.wait()

    return pl.pallas_call(
        body, name=name,
        out_shape=jax.ShapeDtypeStruct(g.shape, g.dtype),
        in_specs=[pl.BlockSpec(memory_space=pl.ANY)],
        out_specs=pl.BlockSpec(memory_space=pl.ANY),
        scratch_shapes=[pltpu.SemaphoreType.DMA((7,)), pltpu.SemaphoreType.DMA((7,)), pltpu.SemaphoreType.DMA(())],
    )(g)


def _reduce_adam(name, recv, w, m, v, tm):
    rows, width = w.shape

    def body(recv_ref, w_ref, m_ref, v_ref, g_ref, d_ref, m2_ref, v2_ref):
        g = recv_ref[0].astype(F32)
        for s in range(1, N_DEV):
            g = g + recv_ref[s].astype(F32)
        delta, m2, v2 = f_adamw(w_ref[...], g, m_ref[...], v_ref[...])
        g_ref[...] = g
        d_ref[...] = delta
        m2_ref[...] = m2
        v2_ref[...] = v2

    row = pl.BlockSpec((tm, width), lambda i: (i, 0))
    return pl.pallas_call(
        body, grid=(rows // tm,), name=name,
        in_specs=[pl.BlockSpec((N_DEV, tm, width), lambda i: (0, i, 0)), row, row, row],
        out_specs=[row] * 4,
        out_shape=[jax.ShapeDtypeStruct((rows, width), F32)] * 4,
        compiler_params=_cparams(("arbitrary",)),
    )(recv, w, m, v)


BIG = ("w_in", "ada_w", "w_br_gdn", "w_br_ssm", "w_out", "w_ffn_in", "w_ffn_out", "gdn_conv_w", "ssm_conv_w")
BIG_ROWS = (1420, 768, 128, 256, 128, 704, 352, 2, 2)
BIG_SHARD_SHAPES = ((1024, 1420), (1024, 768), (128, 1024), (256, 1024), (128, 1024), (1024, 704), (352, 1024),
                    (3, 384), (3, 384))
COL_SHARDED = ("w_in", "ada_w", "w_ffn_in", "gdn_conv_w", "ssm_conv_w")
PACK_ROWS = sum(BIG_ROWS)
SMALL = ("c_ctx", "ada_b", "norm1_w", "gdn_conv_b", "gdn_a_log", "gdn_dt_bias", "gdn_norm_w", "ssm_conv_b",
         "ssm_a_log", "ssm_dt_bias", "ssm_d", "ssm_norm_w", "norm2_w", "norm_f_w")


def _to_rows(a):
    flat = a.reshape(-1)
    pad = (-flat.shape[0]) % PACK_W
    if pad:
        flat = jnp.pad(flat, (0, pad))
    return flat.reshape(-1, PACK_W)


def _pack(arrays):
    return jnp.concatenate([_to_rows(a) for a in arrays], axis=0)


def _unpack(buf, shapes):
    out, r0 = [], 0
    for shp in shapes:
        n = 1
        for s in shp:
            n *= s
        nr = -(-n // PACK_W)
        out.append(buf[r0:r0 + nr].reshape(-1)[:n].reshape(shp))
        r0 += nr
    return out


def _full_from_gathered(blocks, name, shard_shape):
    n = shard_shape[0] * shard_shape[1]
    sh = blocks.reshape(N_DEV, -1)[:, :n].reshape((N_DEV,) + shard_shape)
    if name in COL_SHARDED:
        return jnp.transpose(sh, (1, 0, 2)).reshape(shard_shape[0], N_DEV * shard_shape[1])
    return sh.reshape(N_DEV * shard_shape[0], shard_shape[1])


def _blocks_from_full(full, name, shard_shape):
    if name in COL_SHARDED:
        sh = jnp.transpose(full.reshape(shard_shape[0], N_DEV, shard_shape[1]), (1, 0, 2))
    else:
        sh = full.reshape((N_DEV,) + shard_shape)
    flat = sh.reshape(N_DEV, -1)
    pad = (-flat.shape[1]) % PACK_W
    if pad:
        flat = jnp.pad(flat, ((0, 0), (0, pad)))
    return flat.reshape(N_DEV, -1, PACK_W)


def _pad_cols(a, n):
    return jnp.pad(a, ((0, 0), (0, n - a.shape[1])))


def _w_cat(w_in):
    return jnp.concatenate([
        w_in[:, O_QKV:O_ZG], w_in[:, O_XBC:O_DT], w_in[:, O_ZS:O_XBC], w_in[:, O_GATE:O_END], w_in[:, O_ZG:O_AB],
        _pad_cols(w_in[:, O_AB:O_ZS], 128), _pad_cols(w_in[:, O_DT:O_GATE], 128)], axis=1)


def _w_uncat(wc):
    return jnp.concatenate([
        wc[:, C_QKV:C_XBC], wc[:, C_ZG:C_AB], wc[:, C_AB:C_AB + (O_ZS - O_AB)], wc[:, C_ZS:C_GATE], wc[:, C_XBC:C_ZS],
        wc[:, C_DT:C_DT + (O_GATE - O_DT)], wc[:, C_GATE:C_ZG]], axis=1)


def _pad_row(vec, n=128):
    vec = vec.reshape(1, -1)
    return _pad_cols(vec, n)


def kernel(x, c, ctx, c_ctx, ada_w, ada_b, norm1_w, w_in, gdn_conv_w, gdn_conv_b, gdn_a_log, gdn_dt_bias, gdn_norm_w, ssm_conv_w, ssm_conv_b, ssm_a_log, ssm_dt_bias, ssm_d, ssm_norm_w, w_br_gdn, w_br_ssm, w_out, norm2_w, w_ffn_in, w_ffn_out, norm_f_w, loss_target, m_c_ctx, m_ada_w, m_ada_b, m_norm1_w, m_w_in, m_gdn_conv_w, m_gdn_conv_b, m_gdn_a_log, m_gdn_dt_bias, m_gdn_norm_w, m_ssm_conv_w, m_ssm_conv_b, m_ssm_a_log, m_ssm_dt_bias, m_ssm_d, m_ssm_norm_w, m_w_br_gdn, m_w_br_ssm, m_w_out, m_norm2_w, m_w_ffn_in, m_w_ffn_out, m_norm_f_w, v_c_ctx, v_ada_w, v_ada_b, v_norm1_w, v_w_in, v_gdn_conv_w, v_gdn_conv_b, v_gdn_a_log, v_gdn_dt_bias, v_gdn_norm_w, v_ssm_conv_w, v_ssm_conv_b, v_ssm_a_log, v_ssm_dt_bias, v_ssm_d, v_ssm_norm_w, v_w_br_gdn, v_w_br_ssm, v_w_out, v_norm2_w, v_w_ffn_in, v_w_ffn_out, v_norm_f_w):
    wts = dict(c_ctx=c_ctx, ada_w=ada_w, ada_b=ada_b, norm1_w=norm1_w, w_in=w_in, gdn_conv_w=gdn_conv_w, gdn_conv_b=gdn_conv_b, gdn_a_log=gdn_a_log, gdn_dt_bias=gdn_dt_bias, gdn_norm_w=gdn_norm_w, ssm_conv_w=ssm_conv_w, ssm_conv_b=ssm_conv_b, ssm_a_log=ssm_a_log, ssm_dt_bias=ssm_dt_bias, ssm_d=ssm_d, ssm_norm_w=ssm_norm_w, w_br_gdn=w_br_gdn, w_br_ssm=w_br_ssm, w_out=w_out, norm2_w=norm2_w, w_ffn_in=w_ffn_in, w_ffn_out=w_ffn_out, norm_f_w=norm_f_w)
    mom1 = dict(c_ctx=m_c_ctx, ada_w=m_ada_w, ada_b=m_ada_b, norm1_w=m_norm1_w, w_in=m_w_in, gdn_conv_w=m_gdn_conv_w, gdn_conv_b=m_gdn_conv_b, gdn_a_log=m_gdn_a_log, gdn_dt_bias=m_gdn_dt_bias, gdn_norm_w=m_gdn_norm_w, ssm_conv_w=m_ssm_conv_w, ssm_conv_b=m_ssm_conv_b, ssm_a_log=m_ssm_a_log, ssm_dt_bias=m_ssm_dt_bias, ssm_d=m_ssm_d, ssm_norm_w=m_ssm_norm_w, w_br_gdn=m_w_br_gdn, w_br_ssm=m_w_br_ssm, w_out=m_w_out, norm2_w=m_norm2_w, w_ffn_in=m_w_ffn_in, w_ffn_out=m_w_ffn_out, norm_f_w=m_norm_f_w)
    mom2 = dict(c_ctx=v_c_ctx, ada_w=v_ada_w, ada_b=v_ada_b, norm1_w=v_norm1_w, w_in=v_w_in, gdn_conv_w=v_gdn_conv_w, gdn_conv_b=v_gdn_conv_b, gdn_a_log=v_gdn_a_log, gdn_dt_bias=v_gdn_dt_bias, gdn_norm_w=v_gdn_norm_w, ssm_conv_w=v_ssm_conv_w, ssm_conv_b=v_ssm_conv_b, ssm_a_log=v_ssm_a_log, ssm_dt_bias=v_ssm_dt_bias, ssm_d=v_ssm_d, ssm_norm_w=v_ssm_norm_w, w_br_gdn=v_w_br_gdn, w_br_ssm=v_w_br_ssm, w_out=v_w_out, norm2_w=v_norm2_w, w_ffn_in=v_w_ffn_in, w_ffn_out=v_w_ffn_out, norm_f_w=v_norm_f_w)
    order = list(wts)

    seq = x.shape[1]
    t = TM + seq
    ntl, nlt, nctx = t // TM, seq // TM, TM // CH

    w_pack = _pack([wts[n] for n in BIG])
    gathered = _all_gather("ag_weights", w_pack.astype(BF16))
    conv_rows = sum(BIG_ROWS[:7])
    gathered_cv = _all_gather("ag_conv", jnp.pad(w_pack[conv_rows:], ((0, 4), (0, 0))))
    full, r0 = {}, 0
    for n, nr, shp in zip(BIG, BIG_ROWS, BIG_SHARD_SHAPES):
        if n.endswith("conv_w"):
            off = 0 if n.startswith("gdn") else 2
            full[n] = _full_from_gathered(gathered_cv[:, off:off + 2], n, shp)
        else:
            full[n] = _full_from_gathered(gathered[:, r0:r0 + nr], n, shp)
        r0 += nr
    w_cat = _w_cat(full["w_in"])
    gcw = full["gdn_conv_w"].reshape(3, 1, XBC)
    scw = full["ssm_conv_w"].reshape(3, 1, XBC)

    n1w, n2w, nfw = norm1_w.reshape(1, D), norm2_w.reshape(1, D), norm_f_w.reshape(1, D)
    gcb, scb = gdn_conv_b.reshape(1, XBC), ssm_conv_b.reshape(1, XBC)
    alog16, dtb16 = _pad_row(gdn_a_log), _pad_row(gdn_dt_bias)
    alog64, dtb64 = _pad_row(ssm_a_log), _pad_row(ssm_dt_bias)
    gnw = gdn_norm_w.reshape(1, DK)
    ssd8 = jnp.tile(_pad_row(ssm_d), (8, 1))
    snw = ssm_norm_w.reshape(1, S_INNER)
    x2 = x[0]
    tgt = loss_target[0]
    xa = jnp.concatenate([ctx[0], x2], axis=0)
    cvec = jnp.concatenate([c, c_ctx.reshape(1, D), jnp.zeros((14, D), F32)], axis=0)

    a16 = _rowwise("silu_c", f_silu_rows, [(cvec, D, 0, 0)], [], [(D, BF16)], 1, tm=16)[0]
    mod = _mm("mm_mod", a16, full["ada_w"], "nn", F32) + ada_b
    sh1, sc1, g1, sh2, sc2, g2 = [mod[0:1, i * D:(i + 1) * D] for i in range(6)]
    csh1, csc1 = mod[1:2, 0:D], mod[1:2, D:2 * D]

    pre_pars = [n1w, sc1, sh1, csc1, csh1]
    a = _rowwise("pre", f_pre, [(xa, D, 0, 0)], pre_pars, [(D, BF16)], ntl, base=0)[0]
    proj = _mm("mm_proj", a, w_cat, "nn", F32)
    gp_rows = [(proj, XBC, C_QKV // XBC, 0), (proj, 128, C_AB // 128, 0)]
    gp_pars = [gcw, gcb, alog16, dtb16]
    q, k, v, gcum, beta = _rowwise("gdnprep", f_gdnprep, gp_rows, gp_pars, [(D, F32)] * 3 + [(128, F32)] * 2, ntl, base=0)
    sp_rows = [(proj, XBC, C_XBC // XBC, 0), (proj, 128, C_DT // 128, 0)]
    sp_pars = [scw, scb, alog64, dtb64]
    xs, bm, cm, dtc, acc = _rowwise(
        "ssmprep", f_ssmprep, sp_rows, sp_pars, [(S_INNER, F32), (512, F32), (512, F32), (128, F32), (128, F32)], ntl, base=0)
    o0, ss0, ts0 = _gdn_scan_fwd("gdn_fwd0", q, k, v, gcum, beta, 0, nctx)
    o1, ss1, ts1 = _gdn_scan_fwd("gdn_fwd1", q, k, v, gcum, beta, 1, nctx)
    y0, hs0 = _ssd_scan_fwd("ssd_fwd0", xs, dtc, acc, bm, cm, 0, nctx)
    y1, hs1 = _ssd_scan_fwd("ssd_fwd1", xs, dtc, acc, bm, cm, 1, nctx)
    post_rows = [(o0, D, 0, 1), (o1, D, 0, 1), (proj, D, C_ZG // D, 1), (y0, S_INNER, 0, 1), (y1, S_INNER, 0, 1),
                 (xs, S_INNER, 0, 1), (proj, S_INNER, C_ZS // S_INNER, 1)]
    post_pars = [gnw, ssd8, snw]
    og, ys = _rowwise("post", f_post, post_rows, post_pars, [(D, BF16), (S_INNER, BF16)], nlt)
    pg = _mm("mm_pg", og, full["w_br_gdn"], "nn", F32)
    ps = _mm("mm_ps", ys, full["w_br_ssm"], "nn", F32)
    merge_rows = [(proj, S_INNER, C_GATE // S_INNER, 1), (pg, D, 0, 0), (ps, D, 0, 0)]
    merged = _rowwise("merge", f_merge, merge_rows, [], [(D, BF16)], nlt)[0]
    mix = _mm("mm_mix", merged, full["w_out"], "nn", F32)
    res_rows = [(x2, D, 0, 0), (mix, D, 0, 0)]
    res_pars = [g1, n2w, sc2, sh2]
    h1, f = _rowwise("res1", f_res1, res_rows, res_pars, [(D, F32), (D, BF16)], nlt)
    u = _mm("mm_u", f, full["w_ffn_in"], "nn", F32)
    hact = _rowwise("act", f_act, [(u, 2 * D_FF, 0, 0)], [], [(D_FF, BF16)], nlt)[0]
    ff = _mm("mm_ff", hact, full["w_ffn_out"], "nn", F32)

    fin_rows = [(h1, D, 0, 0), (ff, D, 0, 0), (tgt, D, 0, 0)]
    d_h1a, d_ff, d_g2, d_nfw, loss_acc = _rowwise_bwd(
        "final", f_final, fin_rows, [g2, nfw], ["one"], [(0, F32), (1, BF16)], [0, 1], nlt, loss_out=True)
    d_hact = _mm("mm_dhact", d_ff, full["w_ffn_out"], "nt", BF16)
    g_w_ffn_out = _mm("mm_gwffo", hact, d_ff, "tn", F32)
    d_u = _rowwise_bwd("act_bwd", f_act, [(u, 2 * D_FF, 0, 0)], [], [[(d_hact, D_FF, 0, 0)]], [(0, BF16)], [], nlt)[0]
    d_f = _mm("mm_df", d_u, full["w_ffn_in"], "nt", BF16)
    g_w_ffn_in = _mm("mm_gwffi", f, d_u, "tn", F32)
    d_xres, d_mix, d_g1, d_n2w, d_sc2, d_sh2 = _rowwise_bwd(
        "res1_bwd", f_res1, res_rows, res_pars, [[(d_h1a, D, 0, 0)], [(d_f, D, 0, 0)]], [(0, F32), (1, BF16)], [0, 1, 2, 3], nlt)
    d_merged = _mm("mm_dmerged", d_mix, full["w_out"], "nt", BF16)
    g_w_out = _mm("mm_gwout", merged, d_mix, "tn", F32)
    d_gate, d_pg, d_ps = _rowwise_bwd(
        "merge_bwd", f_merge, merge_rows, [], [[(d_merged, D, 0, 0)]], [(0, BF16), (1, BF16), (2, BF16)], [], nlt)
    d_og = _mm("mm_dog", d_pg, full["w_br_gdn"], "nt", BF16)
    g_w_br_gdn = _mm("mm_gwbrg", og, d_pg, "tn", F32)
    d_ys = _mm("mm_dys", d_ps, full["w_br_ssm"], "nt", BF16)
    g_w_br_ssm = _mm("mm_gwbrs", ys, d_ps, "tn", F32)
    d_o, d_zg, d_y, d_xs_post, d_zs, d_gnw, d_ssd8, d_snw = _rowwise_bwd(
        "post_bwd", f_post, post_rows, post_pars, [[(d_og, D, 0, 0)], [(d_ys, S_INNER, 0, 0)]],
        [(0, F32), (2, BF16), (3, F32), (5, F32), (6, BF16)], [0, 1, 2], nlt)
    dq0, dk0, dv0, dg0, db0 = _gdn_scan_bwd("gdn_bwd0", q, k, v, gcum, beta, ss0, ts0, d_o, 0, nctx)
    dq1, dk1, dv1, dg1, db1 = _gdn_scan_bwd("gdn_bwd1", q, k, v, gcum, beta, ss1, ts1, d_o, 1, nctx)
    dxs0, ddt0, dac0, dbm0, dcm0 = _ssd_scan_bwd("ssd_bwd0", xs, dtc, acc, bm, cm, hs0, d_y, 0, nctx)
    dxs1, ddt1, dac1, dbm1, dcm1 = _ssd_scan_bwd("ssd_bwd1", xs, dtc, acc, bm, cm, hs1, d_y, 1, nctx)
    row = lambda arr, wd: (arr, wd, 0, 0)
    d_qkv_raw, d_ab, d_gcw, d_gcb, d_alog16, d_dtb16 = _rowwise_bwd(
        "gdnprep_bwd", f_gdnprep, gp_rows, gp_pars,
        [[row(dq0, D), row(dq1, D)], [row(dk0, D), row(dk1, D)], [row(dv0, D), row(dv1, D)],
         [row(dg0, 128), row(dg1, 128)], [row(db0, 128), row(db1, 128)]],
        [(0, BF16), (1, BF16)], [0, 1, 2, 3], ntl, base=0)
    d_xbc_raw, d_dt, d_scw, d_scb, d_alog64, d_dtb64 = _rowwise_bwd(
        "ssmprep_bwd", f_ssmprep, sp_rows, sp_pars,
        [[row(dxs0, S_INNER), row(dxs1, S_INNER), (d_xs_post, S_INNER, 0, -1, True)], [row(dbm0, 512), row(dbm1, 512)],
         [row(dcm0, 512), row(dcm1, 512)], [row(ddt0, 128), row(ddt1, 128)], [row(dac0, 128), row(dac1, 128)]],
        [(0, BF16), (1, BF16)], [0, 1, 2, 3], ntl, base=0)
    ctx_zero = lambda wd: jnp.zeros((TM, wd), BF16)
    d_proj = jnp.concatenate([
        d_qkv_raw, d_xbc_raw, jnp.concatenate([ctx_zero(S_INNER), d_zs], axis=0),
        jnp.concatenate([ctx_zero(S_INNER), d_gate], axis=0), jnp.concatenate([ctx_zero(D), d_zg], axis=0), d_ab, d_dt], axis=1)
    d_a = _mm("mm_da", d_proj, w_cat, "nt", BF16)
    g_w_cat = _mm("mm_gwcat", a, d_proj, "tn", F32)
    d_xa, d_n1w, d_sc1, d_sh1, d_csc1, d_csh1 = _rowwise_bwd(
        "pre_bwd", f_pre_thru, [(xa, D, 0, 0)], pre_pars, [[row(d_a, D)], [(d_xres, D, 0, -1, True)]],
        [(0, F32)], [0, 1, 2, 3, 4], ntl, base=0)
    zero4 = jnp.zeros((1, 4 * D), F32)
    d_mod = jnp.concatenate([
        jnp.concatenate([d_sh1, d_sc1, d_g1, d_sh2, d_sc2, d_g2], axis=1),
        jnp.concatenate([d_csh1, d_csc1, zero4], axis=1), jnp.zeros((14, 6 * D), F32)], axis=0)
    g_ada_w = _mm("mm_gwada", a16, d_mod, "tn", F32)
    d_a16 = _mm("mm_da16", d_mod, full["ada_w"], "nt", F32)
    d_cvec = _rowwise_bwd("silu_c_bwd", f_silu_rows, [(cvec, D, 0, 0)], [], [[row(d_a16, D)]], [(0, F32)], [], 1, tm=16)[0]

    grads_full = dict(w_in=_w_uncat(g_w_cat), ada_w=g_ada_w, w_br_gdn=g_w_br_gdn, w_br_ssm=g_w_br_ssm, w_out=g_w_out,
                      w_ffn_in=g_w_ffn_in, w_ffn_out=g_w_ffn_out, gdn_conv_w=d_gcw.reshape(3, XBC),
                      ssm_conv_w=d_scw.reshape(3, XBC))
    g_pack = jnp.concatenate([_blocks_from_full(grads_full[n], n, shp) for n, shp in zip(BIG, BIG_SHARD_SHAPES)], axis=1)
    recv = _all_to_all("a2a_grads", g_pack.astype(BF16))
    m_pack = _pack([mom1[n] for n in BIG])
    v_pack = _pack([mom2[n] for n in BIG])
    big_out = _reduce_adam("adam_big", recv, w_pack, m_pack, v_pack, 80)
    big_shapes = [wts[n].shape for n in BIG]
    big_un = [_unpack(buf, big_shapes) for buf in big_out]

    small_g = dict(c_ctx=d_cvec[1], ada_b=d_mod[0] + d_mod[1], norm1_w=d_n1w, gdn_conv_b=d_gcb,
                   gdn_a_log=d_alog16[0, :2 * G_HEADS], gdn_dt_bias=d_dtb16[0, :2 * G_HEADS], gdn_norm_w=d_gnw,
                   ssm_conv_b=d_scb, ssm_a_log=d_alog64[0, :2 * S_HEADS], ssm_dt_bias=d_dtb64[0, :2 * S_HEADS],
                   ssm_d=d_ssd8[0, :S_HEADS], ssm_norm_w=d_snw, norm2_w=d_n2w, norm_f_w=d_nfw)
    sg_pack = _pack([small_g[n] for n in SMALL])
    recv_s = _all_gather("ag_small_grads", sg_pack)
    small_out = _reduce_adam("adam_small", recv_s, _pack([wts[n] for n in SMALL]), _pack([mom1[n] for n in SMALL]),
                             _pack([mom2[n] for n in SMALL]), sg_pack.shape[0])
    small_shapes = [wts[n].shape for n in SMALL]
    small_un = [_unpack(buf, small_shapes) for buf in small_out]

    res = [{}, {}, {}, {}]
    for kind in range(4):
        for n, val in zip(BIG, big_un[kind]):
            res[kind][n] = val
        for n, val in zip(SMALL, small_un[kind]):
            res[kind][n] = val
    loss = lax.psum(loss_acc[0, 0], ("x", "y", "c"))
    grad_x = d_xa[TM:].reshape(x.shape)
    return (loss, grad_x, *[res[0][n] for n in order], *[res[1][n] for n in order], *[res[2][n] for n in order],
            *[res[3][n] for n in order])
```

```python
import functools

import jax
import jax.numpy as jnp
from jax import lax
from jax.experimental import pallas as pl
from jax.experimental.pallas import tpu as pltpu

F32 = jnp.float32
BF16 = jnp.bfloat16
HI = lax.Precision.HIGHEST
HIGH = lax.Precision.HIGH
MESH = pl.DeviceIdType.MESH

D = 1024
CH = 64
TM = 256
EPS = 1e-6
NEG = -1e30
G_HEADS = 8
DK = 128
S_HEADS = 32
S_P = 64
S_GROUPS = 4
S_N = 128
S_INNER = 2048
XBC = 3072
D_FF = 2816
N_DEV = 8
PACK_W = 1024
VMEM_LIMIT = 56 * 1024 * 1024

ADAM_LR = 0.001
ADAM_B1 = 0.9
ADAM_B2 = 0.999
ADAM_EPS = 1e-08
ADAM_WD = 0.01
ADAM_STEP = 10

C_QKV, C_XBC, C_ZS, C_GATE, C_ZG, C_AB, C_DT, C_END = 0, 3072, 6144, 8192, 10240, 11264, 11392, 11520
O_QKV, O_ZG, O_AB, O_ZS, O_XBC, O_DT, O_GATE, O_END = 0, 3072, 4096, 4128, 6176, 9248, 9312, 11360


def _dot(a, b, prec=None):
    return jnp.dot(a, b, precision=prec, preferred_element_type=F32)


def _dot_nt(a, b, prec=None):
    return lax.dot_general(a, b, (((1,), (1,)), ((), ())), precision=prec, preferred_element_type=F32)


def _dot_tn(a, b, prec=None):
    return lax.dot_general(a, b, (((0,), (0,)), ((), ())), precision=prec, preferred_element_type=F32)


def _iota(shape, dim):
    return lax.broadcasted_iota(jnp.int32, shape, dim)


def _rms(x):
    return x * lax.rsqrt(jnp.mean(x * x, axis=-1, keepdims=True) + EPS)


def _l2n(x):
    return x * lax.rsqrt(jnp.sum(x * x, axis=-1, keepdims=True) + EPS)


def _silu(x):
    return x * jax.nn.sigmoid(x)


def _softplus(x):
    return jnp.maximum(x, 0.0) + jnp.log1p(jnp.exp(-jnp.abs(x)))


def _roll_rows(x, s):
    return pltpu.roll(x, s, 0)


def _up_raw(x, keep_up):
    return jnp.where(keep_up > 0.0, _roll_rows(x, 1), 0.0)


def _dn_raw(x, keep_dn):
    return jnp.where(keep_dn > 0.0, _roll_rows(x, x.shape[0] - 1), 0.0)


@jax.custom_vjp
def _shift_up(x, keep_up, keep_dn):
    return _up_raw(x, keep_up)


def _shift_up_fwd(x, keep_up, keep_dn):
    return _up_raw(x, keep_up), (keep_up, keep_dn)


def _shift_up_bwd(res, g):
    keep_up, keep_dn = res
    return _dn_raw(g, keep_dn), jnp.zeros_like(keep_up), jnp.zeros_like(keep_dn)


_shift_up.defvjp(_shift_up_fwd, _shift_up_bwd)


@jax.custom_vjp
def _shift_dn(x, keep_up, keep_dn):
    return _dn_raw(x, keep_dn)


def _shift_dn_fwd(x, keep_up, keep_dn):
    return _dn_raw(x, keep_dn), (keep_up, keep_dn)


def _shift_dn_bwd(res, g):
    keep_up, keep_dn = res
    return _up_raw(g, keep_up), jnp.zeros_like(keep_up), jnp.zeros_like(keep_dn)


_shift_dn.defvjp(_shift_dn_fwd, _shift_dn_bwd)


def _conv_keep(is_ctx, n):
    r = _iota((n, 1), 0)
    pos = jnp.where(is_ctx, r, r & (CH - 1))
    end = jnp.where(is_ctx, n - 1, CH - 1)
    return jnp.where(pos == 0, 0.0, 1.0).astype(F32), jnp.where(pos == end, 0.0, 1.0).astype(F32)


def _conv_silu(u, w3, b, keep_up, keep_dn):
    conv = b + _shift_up(u, keep_up, keep_dn) * w3[0] + u * w3[1] + _shift_dn(u, keep_up, keep_dn) * w3[2]
    return _silu(conv)


def _chunk_tri(n, rev):
    i = _iota((n, n), 0)
    j = _iota((n, n), 1)
    same = (i // CH) == (j // CH)
    seen = (i <= j) if rev else (i >= j)
    return jnp.where(same & seen, 1.0, 0.0).astype(F32)


def _expand_mat(rows, cols, per, base):
    r = _iota((rows, cols), 0)
    c = _iota((rows, cols), 1)
    return jnp.where(r == base + c // per, 1.0, 0.0).astype(F32)


def f_silu_rows(is_ctx, cvec):
    return (_silu(cvec).astype(BF16),)


def f_pre(is_ctx, x, n1w, sc, sh, csc, csh):
    sc_e = jnp.where(is_ctx, csc, sc)
    sh_e = jnp.where(is_ctx, csh, sh)
    a = _rms(x) * n1w * (1.0 + sc_e) + sh_e
    return (a.astype(BF16),)


def f_pre_thru(is_ctx, x, n1w, sc, sh, csc, csh):
    return f_pre(is_ctx, x, n1w, sc, sh, csc, csh)[0], x


def f_gdnprep(is_ctx, qkv_raw, ab_raw, cw, cb, alog, dtb):
    n = qkv_raw.shape[0]
    keep_up, keep_dn = _conv_keep(is_ctx, n)
    s = _conv_silu(qkv_raw, cw, cb, keep_up, keep_dn)
    qs, ks, vs = [], [], []
    for h in range(G_HEADS):
        qs.append(_l2n(s[:, h * DK:(h + 1) * DK]) * (DK ** -0.5))
        ks.append(_l2n(s[:, D + h * DK:D + (h + 1) * DK]))
    q = jnp.concatenate(qs, axis=1)
    k = jnp.concatenate(ks, axis=1)
    v = s[:, 2 * D:3 * D]
    lane = _iota(ab_raw.shape, 1)
    g = jnp.where(lane < 2 * G_HEADS, -jnp.exp(alog) * _softplus(ab_raw + dtb), 0.0)
    gcum = jnp.where(lane < G_HEADS, _dot(_chunk_tri(n, False), g, HI), _dot(_chunk_tri(n, True), g, HI))
    beta = jax.nn.sigmoid(ab_raw)
    return q, k, v, gcum, beta


def f_ssmprep(is_ctx, xbc_raw, dt_raw, cw, cb, alog, dtb):
    n = xbc_raw.shape[0]
    keep_up, keep_dn = _conv_keep(is_ctx, n)
    s = _conv_silu(xbc_raw, cw, cb, keep_up, keep_dn)
    xs = s[:, :S_INNER]
    bm = s[:, S_INNER:S_INNER + S_GROUPS * S_N]
    cm = s[:, S_INNER + S_GROUPS * S_N:]
    lane = _iota(dt_raw.shape, 1)
    dt = jnp.where(lane < 2 * S_HEADS, _softplus(dt_raw + dtb), 0.0)
    da = dt * (-jnp.exp(alog))
    acum = jnp.where(lane < S_HEADS, _dot(_chunk_tri(n, False), da, HI), _dot(_chunk_tri(n, True), da, HI))
    return xs, bm, cm, dt, acum


def f_post(is_ctx, o_f, o_b, zg, y_f, y_b, xs, zs, gnw, ssd8, snw):
    o = o_f + o_b
    ogs = []
    for h in range(G_HEADS):
        sl = slice(h * DK, (h + 1) * DK)
        ogs.append(_rms(o[:, sl]) * gnw * _silu(zg[:, sl]))
    og = jnp.concatenate(ogs, axis=1)
    row0 = jnp.where(_iota(ssd8.shape, 0) == 0, 1.0, 0.0).astype(F32)
    dexp = jnp.sum(_dot(ssd8 * row0, _expand_mat(128, S_INNER, S_P, 0), HI), axis=0, keepdims=True)
    y = (y_f + y_b + dexp * xs) * _silu(zs)
    gw = S_INNER // S_GROUPS
    ys = jnp.concatenate([_rms(y[:, i * gw:(i + 1) * gw]) * snw[:, i * gw:(i + 1) * gw] for i in range(S_GROUPS)], axis=1)
    return og.astype(BF16), ys.astype(BF16)


def f_merge(is_ctx, gate, pg, ps):
    m = jax.nn.sigmoid(gate[:, :D]) * pg + jax.nn.sigmoid(gate[:, D:]) * ps
    return (m.astype(BF16),)


def f_res1(is_ctx, x, mix, g1, n2w, sc2, sh2):
    h1 = x + g1 * mix
    f = _rms(h1) * n2w * (1.0 + sc2) + sh2
    return h1, f.astype(BF16)


def f_act(is_ctx, u):
    return ((_silu(u[:, :D_FF]) * u[:, D_FF:]).astype(BF16),)


def f_final(is_ctx, h1, ff, tgt, g2, nfw):
    h2 = h1 + g2 * ff
    y = _rms(h2) * nfw
    err = y - tgt
    return (0.5 * jnp.sum(jnp.mean(err * err, axis=-1, keepdims=True), axis=0, keepdims=True),)


def _each(fn, *lists):
    return [fn(*args) for args in zip(*lists)]


def _tri_inverse_all(mats):
    n = mats[0].shape[0]
    eye = jnp.where(_iota((n, n), 0) == _iota((n, n), 1), 1.0, 0.0).astype(F32)
    t = [eye - a for a in mats]
    p = [_dot(a, a, HIGH) for a in mats]
    for r in range(5):
        t = _each(lambda t_, p_: t_ + _dot(t_, p_, HIGH), t, p)
        if r < 4:
            p = [_dot(p_, p_, HIGH) for p_ in p]
    return t


@jax.custom_vjp
def _inverse_given(a, t):
    return t


def _inverse_given_fwd(a, t):
    return t, t


def _inverse_given_bwd(t, g):
    return -_dot_nt(_dot_tn(t, g, HIGH), t, HIGH), jnp.zeros_like(t)


_inverse_given.defvjp(_inverse_given_fwd, _inverse_given_bwd)


def gdn_chunk(ss, qs, ks, vs, gcs, grs, bcs, rev, t_known=None):
    c = qs[0].shape[0]
    ii = _iota((c, c), 0)
    jj = _iota((c, c), 1)
    incl = (ii <= jj) if rev else (ii >= jj)
    strict = (ii < jj) if rev else (ii > jj)
    last = 0 if rev else c - 1
    is_last = _iota((c, 1), 0) == last
    decay = _each(lambda gc, gr: jnp.exp(jnp.where(incl, gc - gr, NEG)), gcs, grs)
    kb = _each(lambda k, bc: k * bc, ks, bcs)
    a = _each(lambda kb_, k, dc: jnp.where(strict, _dot_nt(kb_, k) * dc, 0.0), kb, ks, decay)
    t = _tri_inverse_all(a) if t_known is None else _each(_inverse_given, a, t_known)
    eg = [jnp.exp(gc) for gc in gcs]
    rhs = _each(lambda kb_, eg_, v, bc: jnp.concatenate([kb_ * eg_, v * bc], axis=1), kb, eg, vs, bcs)
    wu = _each(lambda t_, r: _dot(t_, r, HIGH), t, rhs)
    lhs = _each(lambda wu_, q, eg_: jnp.concatenate([wu_[:, :DK], q * eg_], axis=0), wu, qs, eg)
    ws = _each(_dot, lhs, ss)
    v_new = _each(lambda wu_, ws_: wu_[:, DK:] - ws_[:c], wu, ws)
    attn = _each(lambda q, k, dc: _dot_nt(q, k) * dc, qs, ks, decay)
    o = _each(lambda ws_, at, vn: ws_[c:] + _dot(at, vn), ws, attn, v_new)
    gtot = [jnp.sum(jnp.where(is_last, gc, 0.0), axis=0, keepdims=True) for gc in gcs]
    s_new = _each(lambda s, k, gc, gt_, vn: s * jnp.exp(gt_) + _dot_tn(k * jnp.exp(gt_ - gc), vn), ss, ks, gcs, gtot, v_new)
    return s_new, o, t


def ssd_pick(dt0s, dt1s, ac0s, ac1s, ar0s, ar1s):
    lo = _iota((CH, 128), 1) < S_P
    pick = lambda u0, u1: jnp.where(lo, u0, u1)
    return _each(pick, dt0s, dt1s), _each(pick, ac0s, ac1s), _each(pick, ar0s, ar1s)


def ssd_chunk(hts, xs, dts, acs, acr, bgs, cgs, rev):
    c = xs[0].shape[0]
    npair = len(xs)
    grp = [p * len(bgs) // npair for p in range(npair)]
    lane = _iota((c, 128), 1)
    ii = _iota((c, 128), 0)
    jl = lane & (S_P - 1)
    lo = lane < S_P
    seen = (ii <= jl) if rev else (ii >= jl)
    last = 0 if rev else c - 1
    split = lambda z: jnp.concatenate([jnp.where(lo, z, 0.0), jnp.where(lo, 0.0, z)], axis=0)
    cb = _each(lambda bg, cg: _dot_nt(cg, jnp.concatenate([bg, bg], axis=0)), bgs, cgs)
    seg = _each(lambda ac, ar: jnp.exp(jnp.where(seen, ac - ar, NEG)), acs, acr)
    xdt = _each(lambda x, dt: x * dt, xs, dts)
    ydiag = [_dot(cb[grp[p]] * seg[p], split(xdt[p])) for p in range(npair)]
    yoff = [_dot(cgs[grp[p]], hts[p]) * jnp.exp(acs[p]) for p in range(npair)]
    atot = [jnp.sum(jnp.where(ii == last, ac, 0.0), axis=0, keepdims=True) for ac in acs]
    h_new = [hts[p] * jnp.exp(atot[p]) + _dot_tn(bgs[grp[p]], xdt[p] * jnp.exp(atot[p] - acs[p])) for p in range(npair)]
    return h_new, _each(lambda a_, b_: a_ + b_, ydiag, yoff)


def f_adamw(w, g, m, v):
    m = ADAM_B1 * m + (1.0 - ADAM_B1) * g
    v = ADAM_B2 * v + (1.0 - ADAM_B2) * jnp.square(g)
    m_hat = m / (1.0 - ADAM_B1 ** ADAM_STEP)
    v_hat = v / (1.0 - ADAM_B2 ** ADAM_STEP)
    delta = -ADAM_LR * (m_hat / (jnp.sqrt(v_hat) + ADAM_EPS) + ADAM_WD * w)
    return delta, m, v


def _cparams(sem):
    return pltpu.CompilerParams(dimension_semantics=sem, vmem_limit_bytes=VMEM_LIMIT)


def _pick(n, target):
    if n <= target:
        return n
    best = None
    for t in range(128, target + 1, 128):
        if n % t == 0:
            best = t
    assert best is not None, (n, target)
    return best


def _row_spec(tm, width, colblk, rowoff):
    return pl.BlockSpec((tm, width), lambda i: (i + rowoff, colblk))


def _par_spec(shape):
    nd = len(shape)
    return pl.BlockSpec(tuple(shape), lambda i: (0,) * nd)


def _rowwise(name, fn, rows, pars, outs, ntiles, base=1, tm=TM):
    nr, npar = len(rows), len(pars)

    def body(*refs):
        is_ctx = (pl.program_id(0) + base) == 0
        res = fn(is_ctx, *[r[...] for r in refs[:nr]], *[p[...] for p in refs[nr:nr + npar]])
        for o_ref, r in zip(refs[nr + npar:], res):
            o_ref[...] = r.astype(o_ref.dtype)

    return pl.pallas_call(
        body, grid=(ntiles,), name=name,
        in_specs=[_row_spec(tm, wd, cb, ro) for (_, wd, cb, ro) in rows] + [_par_spec(p.shape) for p in pars],
        out_specs=[_row_spec(tm, wd, 0, 0) for (wd, _) in outs],
        out_shape=[jax.ShapeDtypeStruct((ntiles * tm, wd), dt) for (wd, dt) in outs],
        compiler_params=_cparams(("arbitrary",)),
    )(*[r[0] for r in rows], *pars)


def _ct_spec(tm, desc):
    _, wd, cb, ro = desc[:4]
    if len(desc) > 4 and desc[4]:
        return pl.BlockSpec((tm, wd), lambda i: (jnp.maximum(i + ro, 0), cb))
    return _row_spec(tm, wd, cb, ro)


def _rowwise_bwd(name, fn, rows, pars, cts, drows, dpars, ntiles, base=1, loss_out=False, tm=TM):
    nr, npar = len(rows), len(pars)
    ct_rows = [d for ct in cts if isinstance(ct, list) for d in ct]
    nct = len(ct_rows)

    def body(*refs):
        i = pl.program_id(0)
        is_ctx = (i + base) == 0
        rows_v = [r[...] for r in refs[:nr]]
        pars_v = [p[...] for p in refs[nr:nr + npar]]
        ct_refs = list(refs[nr + npar:nr + npar + nct])
        out_refs = list(refs[nr + npar + nct:])
        outs, vjp = jax.vjp(lambda rv, pv: fn(is_ctx, *rv, *pv), rows_v, pars_v)

        def ct_value(desc):
            val = ct_refs.pop(0)[...].astype(F32)
            if len(desc) > 4 and desc[4]:
                val = jnp.where(is_ctx, 0.0, val)
            return val

        ct_vals = []
        for o, ct in zip(outs, cts):
            if ct is None:
                ct_vals.append(jnp.zeros_like(o))
            elif isinstance(ct, str):
                ct_vals.append(jnp.ones_like(o))
            else:
                acc = ct_value(ct[0])
                for desc in ct[1:]:
                    acc = acc + ct_value(desc)
                ct_vals.append(acc.astype(o.dtype))
        d_rows, d_pars = vjp(tuple(ct_vals))
        for (ri, _), o_ref in zip(drows, out_refs[:len(drows)]):
            o_ref[...] = d_rows[ri].astype(o_ref.dtype)
        acc_refs = out_refs[len(drows):]
        acc_vals = [d_pars[pi] for pi in dpars]
        if loss_out:
            acc_vals.append(jnp.broadcast_to(outs[0], (8, 128)))

        @pl.when(i == 0)
        def _():
            for o_ref, val in zip(acc_refs, acc_vals):
                o_ref[...] = val

        @pl.when(i > 0)
        def _():
            for o_ref, val in zip(acc_refs, acc_vals):
                o_ref[...] += val

    acc_shapes = [pars[pi].shape for pi in dpars] + ([(8, 128)] if loss_out else [])
    return pl.pallas_call(
        body, grid=(ntiles,), name=name,
        in_specs=[_row_spec(tm, wd, cb, ro) for (_, wd, cb, ro) in rows] + [_par_spec(p.shape) for p in pars]
        + [_ct_spec(tm, d) for d in ct_rows],
        out_specs=[_row_spec(tm, rows[ri][1], 0, 0) for (ri, _) in drows] + [_par_spec(s) for s in acc_shapes],
        out_shape=[jax.ShapeDtypeStruct((ntiles * tm, rows[ri][1]), dt) for (ri, dt) in drows]
        + [jax.ShapeDtypeStruct(tuple(s), F32) for s in acc_shapes],
        compiler_params=_cparams(("arbitrary",)),
    )(*[r[0] for r in rows], *pars, *[r[0] for r in ct_rows])


def _mm(name, a, b, mode, out_dtype, tm=768, tn=1024, tk=1024):
    if mode == "nn":
        (m, kd), (_, n) = a.shape, b.shape
    elif mode == "nt":
        (m, kd), (n, _) = a.shape, b.shape
    else:
        (kd, m), (_, n) = a.shape, b.shape
    tm, tn, tk = _pick(m, tm), _pick(n, tn), _pick(kd, tk)
    nk = kd // tk
    a_spec = {"nn": pl.BlockSpec((tm, tk), lambda i, j, k: (i, k)), "nt": pl.BlockSpec((tm, tk), lambda i, j, k: (i, k)),
              "tn": pl.BlockSpec((tk, tm), lambda i, j, k: (k, i))}[mode]
    b_spec = {"nn": pl.BlockSpec((tk, tn), lambda i, j, k: (k, j)), "nt": pl.BlockSpec((tn, tk), lambda i, j, k: (j, k)),
              "tn": pl.BlockSpec((tk, tn), lambda i, j, k: (k, j))}[mode]
    dot = {"nn": _dot, "nt": _dot_nt, "tn": _dot_tn}[mode]

    def body(a_ref, b_ref, o_ref, acc_ref):
        k = pl.program_id(2)
        part = dot(a_ref[...].astype(BF16), b_ref[...].astype(BF16))

        @pl.when(k == 0)
        def _():
            acc_ref[...] = part

        @pl.when(k > 0)
        def _():
            acc_ref[...] += part

        @pl.when(k == nk - 1)
        def _():
            o_ref[...] = acc_ref[...].astype(o_ref.dtype)

    return pl.pallas_call(
        body, grid=(m // tm, n // tn, nk), name=name,
        in_specs=[a_spec, b_spec],
        out_specs=pl.BlockSpec((tm, tn), lambda i, j, k: (i, j)),
        out_shape=jax.ShapeDtypeStruct((m, n), out_dtype),
        scratch_shapes=[pltpu.VMEM((tm, tn), F32)],
        compiler_params=_cparams(("arbitrary", "arbitrary", "arbitrary")),
    )(a, b)


def _chunk_index(i, nch, nctx, rev):
    if not rev:
        return i
    return jnp.where(i < nctx, nctx - 1 - i, nch + nctx - 1 - i)


def _gdn_cols(d):
    return [d * G_HEADS + h for h in range(G_HEADS)], [2 * G_HEADS + d * G_HEADS + h for h in range(G_HEADS)]


def _gdn_operands(q_ref, k_ref, v_ref, g_ref, b_ref, d):
    cols_g, cols_b = _gdn_cols(d)
    sls = [slice(h * DK, (h + 1) * DK) for h in range(G_HEADS)]
    gt, bt = g_ref[...], b_ref[...]
    gtt = gt.T
    qs = [q_ref[:, sl] for sl in sls]
    ks = [k_ref[:, sl] for sl in sls]
    vs = [v_ref[:, sl] for sl in sls]
    gcs = [gt[:, cg:cg + 1] for cg in cols_g]
    grs = [gtt[cg:cg + 1, :] for cg in cols_g]
    bcs = [bt[:, cb:cb + 1] for cb in cols_b]
    return sls, qs, ks, vs, gcs, grs, bcs


def _gdn_scan_fwd(name, q, k, v, gcum, beta, d, nctx):
    t = q.shape[0]
    nch = t // CH
    rev = d == 1
    cix = lambda i: _chunk_index(i, nch, nctx, rev)
    full = pl.BlockSpec((CH, D), lambda i: (cix(i), 0))
    wide = pl.BlockSpec((CH, 128), lambda i: (cix(i), 0))

    def body(q_ref, k_ref, v_ref, g_ref, b_ref, o_ref, ss_ref, ts_ref, s_scr):
        @pl.when(pl.program_id(0) == 0)
        def _():
            s_scr[...] = jnp.zeros(s_scr.shape, F32)

        sls, qs, ks, vs, gcs, grs, bcs = _gdn_operands(q_ref, k_ref, v_ref, g_ref, b_ref, d)
        ss = [s_scr[h] for h in range(G_HEADS)]
        s_new, o, t = gdn_chunk(ss, qs, ks, vs, gcs, grs, bcs, rev)
        for h in range(G_HEADS):
            ss_ref[0, h] = ss[h]
            ts_ref[0, h] = t[h]
            s_scr[h] = s_new[h]
            o_ref[:, sls[h]] = o[h]

    return pl.pallas_call(
        body, grid=(nch,), name=name,
        in_specs=[full, full, full, wide, wide],
        out_specs=[full, pl.BlockSpec((1, G_HEADS, DK, DK), lambda i: (i, 0, 0, 0)),
                   pl.BlockSpec((1, G_HEADS, CH, CH), lambda i: (i, 0, 0, 0))],
        out_shape=[jax.ShapeDtypeStruct((t, D), F32), jax.ShapeDtypeStruct((nch, G_HEADS, DK, DK), F32),
                   jax.ShapeDtypeStruct((nch, G_HEADS, CH, CH), F32)],
        scratch_shapes=[pltpu.VMEM((G_HEADS, DK, DK), F32)],
        compiler_params=_cparams(("arbitrary",)),
    )(q, k, v, gcum, beta)


def _gdn_scan_bwd(name, q, k, v, gcum, beta, ssave, tsave, do, d, nctx):
    t = q.shape[0]
    nch = t // CH
    rev = d == 1
    cix = lambda ib: _chunk_index(nch - 1 - ib, nch, nctx, rev)
    full = pl.BlockSpec((CH, D), lambda ib: (cix(ib), 0))
    wide = pl.BlockSpec((CH, 128), lambda ib: (cix(ib), 0))
    do_spec = pl.BlockSpec((CH, D), lambda ib: (jnp.maximum(cix(ib), nctx) - nctx, 0))

    def body(q_ref, k_ref, v_ref, g_ref, b_ref, ss_ref, ts_ref, do_ref, dq_ref, dk_ref, dv_ref, dg_ref, db_ref, ds_scr):
        ib = pl.program_id(0)

        @pl.when(ib == 0)
        def _():
            ds_scr[...] = jnp.zeros(ds_scr.shape, F32)

        sls, qs, ks, vs, gcs, grs, bcs = _gdn_operands(q_ref, k_ref, v_ref, g_ref, b_ref, d)
        t_known = [ts_ref[0, h] for h in range(G_HEADS)]
        cols_g, cols_b = _gdn_cols(d)
        is_lat = cix(ib) >= nctx
        ss = [ss_ref[0, h] for h in range(G_HEADS)]
        do_v = [jnp.where(is_lat, do_ref[:, sl], 0.0) for sl in sls]
        ds_in = [ds_scr[h] for h in range(G_HEADS)]
        _, vjp = jax.vjp(lambda *a: gdn_chunk(*a, rev, t_known)[:2], ss, qs, ks, vs, gcs, grs, bcs)
        ds, dq, dk, dv, dgc, dgr, dbc = vjp((ds_in, do_v))
        lane = _iota((CH, 128), 1)
        sub = _iota((128, CH), 0)
        dg = jnp.zeros((CH, 128), F32)
        dgt = jnp.zeros((128, CH), F32)
        db = jnp.zeros((CH, 128), F32)
        for h in range(G_HEADS):
            ds_scr[h] = ds[h]
            dq_ref[:, sls[h]] = dq[h]
            dk_ref[:, sls[h]] = dk[h]
            dv_ref[:, sls[h]] = dv[h]
            dg = dg + jnp.where(lane == cols_g[h], dgc[h], 0.0)
            dgt = dgt + jnp.where(sub == cols_g[h], dgr[h], 0.0)
            db = db + jnp.where(lane == cols_b[h], dbc[h], 0.0)
        dg_ref[...] = dg + dgt.T
        db_ref[...] = db

    return pl.pallas_call(
        body, grid=(nch,), name=name,
        in_specs=[full, full, full, wide, wide,
                  pl.BlockSpec((1, G_HEADS, DK, DK), lambda ib: (nch - 1 - ib, 0, 0, 0)),
                  pl.BlockSpec((1, G_HEADS, CH, CH), lambda ib: (nch - 1 - ib, 0, 0, 0)), do_spec],
        out_specs=[full, full, full, wide, wide],
        out_shape=[jax.ShapeDtypeStruct((t, D), F32)] * 3 + [jax.ShapeDtypeStruct((t, 128), F32)] * 2,
        scratch_shapes=[pltpu.VMEM((G_HEADS, DK, DK), F32)],
        compiler_params=_cparams(("arbitrary",)),
    )(q, k, v, gcum, beta, ssave, tsave, do)


N_PAIRS = S_HEADS // 2


def _ssd_operands(x_ref, dt_ref, ac_ref, b_ref, c_ref, d):
    sls = [slice(p * 128, (p + 1) * 128) for p in range(N_PAIRS)]
    gsl = [slice(g * S_N, (g + 1) * S_N) for g in range(S_GROUPS)]
    cols = [d * S_HEADS + h for h in range(S_HEADS)]
    dtc, acc = dt_ref[...], ac_ref[...]
    act = jnp.concatenate([acc, acc], axis=0).T
    col = lambda z, cc: z[:, cc:cc + 1]
    dts, acs, acr = ssd_pick(
        [col(dtc, cols[2 * p]) for p in range(N_PAIRS)], [col(dtc, cols[2 * p + 1]) for p in range(N_PAIRS)],
        [col(acc, cols[2 * p]) for p in range(N_PAIRS)], [col(acc, cols[2 * p + 1]) for p in range(N_PAIRS)],
        [act[cols[2 * p]:cols[2 * p] + 1, :] for p in range(N_PAIRS)],
        [act[cols[2 * p + 1]:cols[2 * p + 1] + 1, :] for p in range(N_PAIRS)])
    ops = ([x_ref[:, sl] for sl in sls], dts, acs, acr, [b_ref[:, gs] for gs in gsl], [c_ref[:, gs] for gs in gsl])
    return sls, gsl, cols, ops


def _ssd_scan_fwd(name, xs, dtc, acc, bm, cm, d, nctx):
    t = xs.shape[0]
    nch = t // CH
    rev = d == 1
    cix = lambda i: _chunk_index(i, nch, nctx, rev)
    inner = pl.BlockSpec((CH, S_INNER), lambda i: (cix(i), 0))
    wide = pl.BlockSpec((CH, 128), lambda i: (cix(i), 0))
    grp = pl.BlockSpec((CH, S_GROUPS * S_N), lambda i: (cix(i), 0))

    def body(x_ref, dt_ref, ac_ref, b_ref, c_ref, y_ref, hs_ref, h_scr):
        @pl.when(pl.program_id(0) == 0)
        def _():
            h_scr[...] = jnp.zeros(h_scr.shape, F32)

        sls, _, _, ops = _ssd_operands(x_ref, dt_ref, ac_ref, b_ref, c_ref, d)
        hts = [h_scr[p] for p in range(N_PAIRS)]
        h_new, y = ssd_chunk(hts, *ops, rev)
        for p in range(N_PAIRS):
            hs_ref[0, p] = hts[p]
            h_scr[p] = h_new[p]
            y_ref[:, sls[p]] = y[p]

    return pl.pallas_call(
        body, grid=(nch,), name=name,
        in_specs=[inner, wide, wide, grp, grp],
        out_specs=[inner, pl.BlockSpec((1, N_PAIRS, S_N, 128), lambda i: (i, 0, 0, 0))],
        out_shape=[jax.ShapeDtypeStruct((t, S_INNER), F32), jax.ShapeDtypeStruct((nch, N_PAIRS, S_N, 128), F32)],
        scratch_shapes=[pltpu.VMEM((N_PAIRS, S_N, 128), F32)],
        compiler_params=_cparams(("arbitrary",)),
    )(xs, dtc, acc, bm, cm)


def _ssd_scan_bwd(name, xs, dtc, acc, bm, cm, hsave, dy, d, nctx):
    t = xs.shape[0]
    nch = t // CH
    rev = d == 1
    cix = lambda ib: _chunk_index(nch - 1 - ib, nch, nctx, rev)
    inner = pl.BlockSpec((CH, S_INNER), lambda ib: (cix(ib), 0))
    wide = pl.BlockSpec((CH, 128), lambda ib: (cix(ib), 0))
    grp = pl.BlockSpec((CH, S_GROUPS * S_N), lambda ib: (cix(ib), 0))
    dy_spec = pl.BlockSpec((CH, S_INNER), lambda ib: (jnp.maximum(cix(ib), nctx) - nctx, 0))

    def body(x_ref, dt_ref, ac_ref, b_ref, c_ref, hs_ref, dy_ref, dx_ref, ddt_ref, dac_ref, db_ref, dc_ref, dh_scr):
        ib = pl.program_id(0)

        @pl.when(ib == 0)
        def _():
            dh_scr[...] = jnp.zeros(dh_scr.shape, F32)

        sls, gsl, cols, ops = _ssd_operands(x_ref, dt_ref, ac_ref, b_ref, c_ref, d)
        is_lat = cix(ib) >= nctx
        hts = [hs_ref[0, p] for p in range(N_PAIRS)]
        dy_v = [jnp.where(is_lat, dy_ref[:, sl], 0.0) for sl in sls]
        dh_in = [dh_scr[p] for p in range(N_PAIRS)]
        _, vjp = jax.vjp(lambda *a: ssd_chunk(*a, rev), hts, *ops)
        dh, dx, ddts, dacs, dacr, db, dc = vjp((dh_in, dy_v))
        for p in range(N_PAIRS):
            dh_scr[p] = dh[p]
            dx_ref[:, sls[p]] = dx[p]
        r = _iota((S_INNER, 128), 0)
        e_t = jnp.where(_iota((S_INNER, 128), 1) == d * S_HEADS + r // S_P, 1.0, 0.0).astype(F32)
        ddt_ref[...] = _dot(jnp.concatenate(ddts, axis=1), e_t, HIGH)
        dac_cols = _dot(jnp.concatenate(dacs, axis=1), e_t, HIGH)
        sub = _iota((128, 128), 0)
        lane = _iota((128, 128), 1)
        m = jnp.zeros((128, 128), F32)
        for p in range(N_PAIRS):
            m = m + jnp.where(sub == p, jnp.sum(dacr[p], axis=0, keepdims=True), 0.0)
        mt = m.T
        s0 = jnp.where(lane == d * S_HEADS + 2 * sub, 1.0, 0.0).astype(F32)
        s1 = jnp.where(lane == d * S_HEADS + 2 * sub + 1, 1.0, 0.0).astype(F32)
        dac_ref[...] = dac_cols + _dot(mt[:CH], s0, HIGH) + _dot(mt[CH:], s1, HIGH)
        for g in range(S_GROUPS):
            db_ref[:, gsl[g]] = db[g]
            dc_ref[:, gsl[g]] = dc[g]

    return pl.pallas_call(
        body, grid=(nch,), name=name,
        in_specs=[inner, wide, wide, grp, grp,
                  pl.BlockSpec((1, N_PAIRS, S_N, 128), lambda ib: (nch - 1 - ib, 0, 0, 0)), dy_spec],
        out_specs=[inner, wide, wide, grp, grp],
        out_shape=[jax.ShapeDtypeStruct((t, S_INNER), F32)] + [jax.ShapeDtypeStruct((t, 128), F32)] * 2
        + [jax.ShapeDtypeStruct((t, S_GROUPS * S_N), F32)] * 2,
        scratch_shapes=[pltpu.VMEM((N_PAIRS, S_N, 128), F32)],
        compiler_params=_cparams(("arbitrary",)),
    )(xs, dtc, acc, bm, cm, hsave, dy)


def _mesh_pos():
    return lax.axis_index("x"), lax.axis_index("y"), lax.axis_index("c")


def _hbm_specs(n):
    return [pl.BlockSpec(memory_space=pl.ANY)] * n


def _all_gather(name, shards):
    nw = len(shards)

    def body(*refs):
        x_refs, out_refs = refs[:nw], refs[nw:2 * nw]
        send_sems, recv_sems, local_sems = refs[2 * nw:]
        x, y, c = _mesh_pos()
        me, sibling = (x, y, c), (x, y, 1 - c)
        chips = [(1 - x, y), (x, 1 - y), (1 - x, 1 - y)]

        def slot(w, px, py, pc):
            return out_refs[w].at[4 * px + 2 * py + pc]

        def copy(w, k, block, to, src=None):
            return pltpu.make_async_remote_copy(
                src_ref=slot(w, *block) if src is None else src, dst_ref=slot(w, *block),
                send_sem=send_sems.at[w, k], recv_sem=recv_sems.at[w, k], device_id=to, device_id_type=MESH)

        mine = [pltpu.make_async_copy(x_refs[w], slot(w, *me), local_sems.at[w]) for w in range(nw)]
        for cp in mine:
            cp.start()
        first = []
        for w in range(nw):
            first.append(copy(w, 0, me, sibling, src=x_refs[w]))
            first += [copy(w, 1 + j, me, (*chip, c), src=x_refs[w]) for j, chip in enumerate(chips)]
        for cp in first:
            cp.start()
        passed = []
        for j, chip in enumerate(chips):
            for w in range(nw):
                copy(w, 1 + j, (*chip, c), me).wait_recv()
                fwd = copy(w, 4 + j, (*chip, c), sibling)
                fwd.start()
                passed.append(fwd)
        for w in range(nw):
            copy(w, 0, sibling, me).wait_recv()
            for j, chip in enumerate(chips):
                copy(w, 4 + j, (*chip, 1 - c), me).wait_recv()
        for cp in first + passed:
            cp.wait_send()
        for cp in mine:
            cp.wait()

    return pl.pallas_call(
        body, name=name,
        out_shape=[jax.ShapeDtypeStruct((N_DEV,) + xs.shape, xs.dtype) for xs in shards],
        in_specs=_hbm_specs(nw), out_specs=_hbm_specs(nw),
        scratch_shapes=[pltpu.SemaphoreType.DMA((nw, 7)), pltpu.SemaphoreType.DMA((nw, 7)), pltpu.SemaphoreType.DMA((nw,))],
    )(*shards)


def _all_to_all(name, blocks):
    nw = len(blocks)

    def body(*refs):
        g_refs, out_refs = refs[:nw], refs[nw:2 * nw]
        send_sems, recv_sems, local_sems = refs[2 * nw:]
        x, y, c = _mesh_pos()
        me_i = 4 * x + 2 * y + c
        mine = [pltpu.make_async_copy(g_refs[w].at[me_i], out_refs[w].at[me_i], local_sems.at[w]) for w in range(nw)]
        for cp in mine:
            cp.start()
        cps = []
        for k in range(1, N_DEV):
            px = 1 - x if (k >> 2) & 1 else x
            py = 1 - y if (k >> 1) & 1 else y
            pc = 1 - c if k & 1 else c
            for w in range(nw):
                cp = pltpu.make_async_remote_copy(
                    src_ref=g_refs[w].at[4 * px + 2 * py + pc], dst_ref=out_refs[w].at[me_i],
                    send_sem=send_sems.at[w, k - 1], recv_sem=recv_sems.at[w, k - 1],
                    device_id=(px, py, pc), device_id_type=MESH)
                cp.start()
                cps.append(cp)
        for cp in cps:
            cp.wait()
        for cp in mine:
            cp.wait()

    return pl.pallas_call(
        body, name=name,
        out_shape=[jax.ShapeDtypeStruct(g.shape, g.dtype) for g in blocks],
        in_specs=_hbm_specs(nw), out_specs=_hbm_specs(nw),
        scratch_shapes=[pltpu.SemaphoreType.DMA((nw, 7)), pltpu.SemaphoreType.DMA((nw, 7)), pltpu.SemaphoreType.DMA((nw,))],
    )(*blocks)


def _reduce_adam(name, recv, w, m, v, tm):
    rows, width = w.shape

    def body(recv_ref, w_ref, m_ref, v_ref, g_ref, d_ref, m2_ref, v2_ref):
        g = recv_ref[0].astype(F32)
        for s in range(1, N_DEV):
            g = g + recv_ref[s].astype(F32)
        delta, m2, v2 = f_adamw(w_ref[...], g, m_ref[...], v_ref[...])
        g_ref[...] = g
        d_ref[...] = delta
        m2_ref[...] = m2
        v2_ref[...] = v2

    row = pl.BlockSpec((tm, width), lambda i: (i, 0))
    return pl.pallas_call(
        body, grid=(rows // tm,), name=name,
        in_specs=[pl.BlockSpec((N_DEV, tm, width), lambda i: (0, i, 0)), row, row, row],
        out_specs=[row] * 4,
        out_shape=[jax.ShapeDtypeStruct((rows, width), F32)] * 4,
        compiler_params=_cparams(("arbitrary",)),
    )(recv, w, m, v)


BIG = ("w_in", "ada_w", "w_br_gdn", "w_br_ssm", "w_out", "w_ffn_in", "w_ffn_out")
BIG_COL_SHARDED = ("w_in", "ada_w", "w_ffn_in")
BIG_ADAM_ROWS = dict(w_in=128, ada_w=256, w_br_gdn=128, w_br_ssm=256, w_out=128, w_ffn_in=256, w_ffn_out=352)
CONV = ("gdn_conv_w", "ssm_conv_w")
SMALL = ("c_ctx", "ada_b", "norm1_w", "gdn_conv_b", "gdn_a_log", "gdn_dt_bias", "gdn_norm_w", "ssm_conv_b",
         "ssm_a_log", "ssm_dt_bias", "ssm_d", "ssm_norm_w", "norm2_w", "norm_f_w")
CONV_SHARD = XBC // N_DEV


def _to_rows(a):
    flat = a.reshape(-1)
    pad = (-flat.shape[0]) % PACK_W
    if pad:
        flat = jnp.pad(flat, (0, pad))
    return flat.reshape(-1, PACK_W)


def _pack(arrays, rows=None):
    buf = jnp.concatenate([_to_rows(a) for a in arrays], axis=0)
    if rows is not None and rows > buf.shape[0]:
        buf = jnp.pad(buf, ((0, rows - buf.shape[0]), (0, 0)))
    return buf


def _unpack(buf, shapes):
    out, r0 = [], 0
    for shp in shapes:
        n = 1
        for s in shp:
            n *= s
        nr = -(-n // PACK_W)
        out.append(buf[r0:r0 + nr].reshape(-1)[:n].reshape(shp))
        r0 += nr
    return out


def _full_from_blocks(blocks, col_sharded):
    _, r, c = blocks.shape
    if col_sharded:
        return jnp.transpose(blocks, (1, 0, 2)).reshape(r, N_DEV * c)
    return blocks.reshape(N_DEV * r, c)


def _blocks_from_full(full, col_sharded):
    if col_sharded:
        r, c = full.shape[0], full.shape[1] // N_DEV
        return jnp.transpose(full.reshape(r, N_DEV, c), (1, 0, 2))
    return full.reshape(N_DEV, full.shape[0] // N_DEV, full.shape[1])


def _pad_cols(a, n):
    return jnp.pad(a, ((0, 0), (0, n - a.shape[1])))


def _w_cat(w_in):
    return jnp.concatenate([
        w_in[:, O_QKV:O_ZG], w_in[:, O_XBC:O_DT], w_in[:, O_ZS:O_XBC], w_in[:, O_GATE:O_END], w_in[:, O_ZG:O_AB],
        _pad_cols(w_in[:, O_AB:O_ZS], 128), _pad_cols(w_in[:, O_DT:O_GATE], 128)], axis=1)


def _w_uncat(wc):
    return jnp.concatenate([
        wc[:, C_QKV:C_XBC], wc[:, C_ZG:C_AB], wc[:, C_AB:C_AB + (O_ZS - O_AB)], wc[:, C_ZS:C_GATE], wc[:, C_XBC:C_ZS],
        wc[:, C_DT:C_DT + (O_GATE - O_DT)], wc[:, C_GATE:C_ZG]], axis=1)


def _pad_row(vec, n=128):
    vec = vec.reshape(1, -1)
    return _pad_cols(vec, n)


def kernel(x, c, ctx, c_ctx, ada_w, ada_b, norm1_w, w_in, gdn_conv_w, gdn_conv_b, gdn_a_log, gdn_dt_bias, gdn_norm_w, ssm_conv_w, ssm_conv_b, ssm_a_log, ssm_dt_bias, ssm_d, ssm_norm_w, w_br_gdn, w_br_ssm, w_out, norm2_w, w_ffn_in, w_ffn_out, norm_f_w, loss_target, m_c_ctx, m_ada_w, m_ada_b, m_norm1_w, m_w_in, m_gdn_conv_w, m_gdn_conv_b, m_gdn_a_log, m_gdn_dt_bias, m_gdn_norm_w, m_ssm_conv_w, m_ssm_conv_b, m_ssm_a_log, m_ssm_dt_bias, m_ssm_d, m_ssm_norm_w, m_w_br_gdn, m_w_br_ssm, m_w_out, m_norm2_w, m_w_ffn_in, m_w_ffn_out, m_norm_f_w, v_c_ctx, v_ada_w, v_ada_b, v_norm1_w, v_w_in, v_gdn_conv_w, v_gdn_conv_b, v_gdn_a_log, v_gdn_dt_bias, v_gdn_norm_w, v_ssm_conv_w, v_ssm_conv_b, v_ssm_a_log, v_ssm_dt_bias, v_ssm_d, v_ssm_norm_w, v_w_br_gdn, v_w_br_ssm, v_w_out, v_norm2_w, v_w_ffn_in, v_w_ffn_out, v_norm_f_w):
    wts = dict(c_ctx=c_ctx, ada_w=ada_w, ada_b=ada_b, norm1_w=norm1_w, w_in=w_in, gdn_conv_w=gdn_conv_w, gdn_conv_b=gdn_conv_b, gdn_a_log=gdn_a_log, gdn_dt_bias=gdn_dt_bias, gdn_norm_w=gdn_norm_w, ssm_conv_w=ssm_conv_w, ssm_conv_b=ssm_conv_b, ssm_a_log=ssm_a_log, ssm_dt_bias=ssm_dt_bias, ssm_d=ssm_d, ssm_norm_w=ssm_norm_w, w_br_gdn=w_br_gdn, w_br_ssm=w_br_ssm, w_out=w_out, norm2_w=norm2_w, w_ffn_in=w_ffn_in, w_ffn_out=w_ffn_out, norm_f_w=norm_f_w)
    mom1 = dict(c_ctx=m_c_ctx, ada_w=m_ada_w, ada_b=m_ada_b, norm1_w=m_norm1_w, w_in=m_w_in, gdn_conv_w=m_gdn_conv_w, gdn_conv_b=m_gdn_conv_b, gdn_a_log=m_gdn_a_log, gdn_dt_bias=m_gdn_dt_bias, gdn_norm_w=m_gdn_norm_w, ssm_conv_w=m_ssm_conv_w, ssm_conv_b=m_ssm_conv_b, ssm_a_log=m_ssm_a_log, ssm_dt_bias=m_ssm_dt_bias, ssm_d=m_ssm_d, ssm_norm_w=m_ssm_norm_w, w_br_gdn=m_w_br_gdn, w_br_ssm=m_w_br_ssm, w_out=m_w_out, norm2_w=m_norm2_w, w_ffn_in=m_w_ffn_in, w_ffn_out=m_w_ffn_out, norm_f_w=m_norm_f_w)
    mom2 = dict(c_ctx=v_c_ctx, ada_w=v_ada_w, ada_b=v_ada_b, norm1_w=v_norm1_w, w_in=v_w_in, gdn_conv_w=v_gdn_conv_w, gdn_conv_b=v_gdn_conv_b, gdn_a_log=v_gdn_a_log, gdn_dt_bias=v_gdn_dt_bias, gdn_norm_w=v_gdn_norm_w, ssm_conv_w=v_ssm_conv_w, ssm_conv_b=v_ssm_conv_b, ssm_a_log=v_ssm_a_log, ssm_dt_bias=v_ssm_dt_bias, ssm_d=v_ssm_d, ssm_norm_w=v_ssm_norm_w, w_br_gdn=v_w_br_gdn, w_br_ssm=v_w_br_ssm, w_out=v_w_out, norm2_w=v_norm2_w, w_ffn_in=v_w_ffn_in, w_ffn_out=v_w_ffn_out, norm_f_w=v_norm_f_w)
    order = list(wts)

    seq = x.shape[1]
    t = TM + seq
    ntl, nlt, nctx = t // TM, seq // TM, TM // CH

    me_i = 4 * lax.axis_index("x") + 2 * lax.axis_index("y") + lax.axis_index("c")
    gathered = _all_gather("ag_weights", [wts[n][0].astype(BF16) for n in BIG])
    full = {n: _full_from_blocks(blk, n in BIG_COL_SHARDED) for n, blk in zip(BIG, gathered)}
    conv_sh = _pack([wts[n] for n in CONV], rows=8)
    conv_g = _all_gather("ag_conv", [conv_sh])[0].reshape(N_DEV, -1)
    ncv = 3 * CONV_SHARD
    for i, n in enumerate(CONV):
        off = -(-ncv // PACK_W) * PACK_W * i
        full[n] = jnp.transpose(conv_g[:, off:off + ncv].reshape(N_DEV, 3, CONV_SHARD), (1, 0, 2)).reshape(3, XBC)
    w_cat = _w_cat(full["w_in"])
    gcw = full["gdn_conv_w"].reshape(3, 1, XBC)
    scw = full["ssm_conv_w"].reshape(3, 1, XBC)

    n1w, n2w, nfw = norm1_w.reshape(1, D), norm2_w.reshape(1, D), norm_f_w.reshape(1, D)
    gcb, scb = gdn_conv_b.reshape(1, XBC), ssm_conv_b.reshape(1, XBC)
    alog16, dtb16 = _pad_row(gdn_a_log), _pad_row(gdn_dt_bias)
    alog64, dtb64 = _pad_row(ssm_a_log), _pad_row(ssm_dt_bias)
    gnw = gdn_norm_w.reshape(1, DK)
    ssd8 = jnp.tile(_pad_row(ssm_d), (8, 1))
    snw = ssm_norm_w.reshape(1, S_INNER)
    x2 = x[0]
    tgt = loss_target[0]
    xa = jnp.concatenate([ctx[0], x2], axis=0)
    cvec = jnp.concatenate([c, c_ctx.reshape(1, D), jnp.zeros((14, D), F32)], axis=0)

    a16 = _rowwise("silu_c", f_silu_rows, [(cvec, D, 0, 0)], [], [(D, BF16)], 1, tm=16)[0]
    mod = _mm("mm_mod", a16, full["ada_w"], "nn", F32) + ada_b
    sh1, sc1, g1, sh2, sc2, g2 = [mod[0:1, i * D:(i + 1) * D] for i in range(6)]
    csh1, csc1 = mod[1:2, 0:D], mod[1:2, D:2 * D]

    pre_pars = [n1w, sc1, sh1, csc1, csh1]
    a = _rowwise("pre", f_pre, [(xa, D, 0, 0)], pre_pars, [(D, BF16)], ntl, base=0)[0]
    proj = _mm("mm_proj", a, w_cat, "nn", F32)
    gp_rows = [(proj, XBC, C_QKV // XBC, 0), (proj, 128, C_AB // 128, 0)]
    gp_pars = [gcw, gcb, alog16, dtb16]
    q, k, v, gcum, beta = _rowwise("gdnprep", f_gdnprep, gp_rows, gp_pars, [(D, F32)] * 3 + [(128, F32)] * 2, ntl, base=0)
    sp_rows = [(proj, XBC, C_XBC // XBC, 0), (proj, 128, C_DT // 128, 0)]
    sp_pars = [scw, scb, alog64, dtb64]
    xs, bm, cm, dtc, acc = _rowwise(
        "ssmprep", f_ssmprep, sp_rows, sp_pars, [(S_INNER, F32), (512, F32), (512, F32), (128, F32), (128, F32)], ntl, base=0)
    o0, ss0, ts0 = _gdn_scan_fwd("gdn_fwd0", q, k, v, gcum, beta, 0, nctx)
    o1, ss1, ts1 = _gdn_scan_fwd("gdn_fwd1", q, k, v, gcum, beta, 1, nctx)
    y0, hs0 = _ssd_scan_fwd("ssd_fwd0", xs, dtc, acc, bm, cm, 0, nctx)
    y1, hs1 = _ssd_scan_fwd("ssd_fwd1", xs, dtc, acc, bm, cm, 1, nctx)
    post_rows = [(o0, D, 0, 1), (o1, D, 0, 1), (proj, D, C_ZG // D, 1), (y0, S_INNER, 0, 1), (y1, S_INNER, 0, 1),
                 (xs, S_INNER, 0, 1), (proj, S_INNER, C_ZS // S_INNER, 1)]
    post_pars = [gnw, ssd8, snw]
    og, ys = _rowwise("post", f_post, post_rows, post_pars, [(D, BF16), (S_INNER, BF16)], nlt)
    pg = _mm("mm_pg", og, full["w_br_gdn"], "nn", F32)
    ps = _mm("mm_ps", ys, full["w_br_ssm"], "nn", F32)
    merge_rows = [(proj, S_INNER, C_GATE // S_INNER, 1), (pg, D, 0, 0), (ps, D, 0, 0)]
    merged = _rowwise("merge", f_merge, merge_rows, [], [(D, BF16)], nlt)[0]
    mix = _mm("mm_mix", merged, full["w_out"], "nn", F32)
    res_rows = [(x2, D, 0, 0), (mix, D, 0, 0)]
    res_pars = [g1, n2w, sc2, sh2]
    h1, f = _rowwise("res1", f_res1, res_rows, res_pars, [(D, F32), (D, BF16)], nlt)
    u = _mm("mm_u", f, full["w_ffn_in"], "nn", F32)
    hact = _rowwise("act", f_act, [(u, 2 * D_FF, 0, 0)], [], [(D_FF, BF16)], nlt)[0]
    ff = _mm("mm_ff", hact, full["w_ffn_out"], "nn", F32)

    fin_rows = [(h1, D, 0, 0), (ff, D, 0, 0), (tgt, D, 0, 0)]
    d_h1a, d_ff, d_g2, d_nfw, loss_acc = _rowwise_bwd(
        "final", f_final, fin_rows, [g2, nfw], ["one"], [(0, F32), (1, BF16)], [0, 1], nlt, loss_out=True)
    d_hact = _mm("mm_dhact", d_ff, full["w_ffn_out"], "nt", BF16)
    g_w_ffn_out = _mm("mm_gwffo", hact, d_ff, "tn", BF16)
    d_u = _rowwise_bwd("act_bwd", f_act, [(u, 2 * D_FF, 0, 0)], [], [[(d_hact, D_FF, 0, 0)]], [(0, BF16)], [], nlt)[0]
    d_f = _mm("mm_df", d_u, full["w_ffn_in"], "nt", BF16)
    g_w_ffn_in = _mm("mm_gwffi", f, d_u, "tn", BF16)
    d_xres, d_mix, d_g1, d_n2w, d_sc2, d_sh2 = _rowwise_bwd(
        "res1_bwd", f_res1, res_rows, res_pars, [[(d_h1a, D, 0, 0)], [(d_f, D, 0, 0)]], [(0, F32), (1, BF16)], [0, 1, 2, 3], nlt)
    d_merged = _mm("mm_dmerged", d_mix, full["w_out"], "nt", BF16)
    g_w_out = _mm("mm_gwout", merged, d_mix, "tn", BF16)
    d_gate, d_pg, d_ps = _rowwise_bwd(
        "merge_bwd", f_merge, merge_rows, [], [[(d_merged, D, 0, 0)]], [(0, BF16), (1, BF16), (2, BF16)], [], nlt)
    d_og = _mm("mm_dog", d_pg, full["w_br_gdn"], "nt", BF16)
    g_w_br_gdn = _mm("mm_gwbrg", og, d_pg, "tn", BF16)
    d_ys = _mm("mm_dys", d_ps, full["w_br_ssm"], "nt", BF16)
    g_w_br_ssm = _mm("mm_gwbrs", ys, d_ps, "tn", BF16)
    d_o, d_zg, d_y, d_xs_post, d_zs, d_gnw, d_ssd8, d_snw = _rowwise_bwd(
        "post_bwd", f_post, post_rows, post_pars, [[(d_og, D, 0, 0)], [(d_ys, S_INNER, 0, 0)]],
        [(0, F32), (2, BF16), (3, F32), (5, F32), (6, BF16)], [0, 1, 2], nlt)
    dq0, dk0, dv0, dg0, db0 = _gdn_scan_bwd("gdn_bwd0", q, k, v, gcum, beta, ss0, ts0, d_o, 0, nctx)
    dq1, dk1, dv1, dg1, db1 = _gdn_scan_bwd("gdn_bwd1", q, k, v, gcum, beta, ss1, ts1, d_o, 1, nctx)
    dxs0, ddt0, dac0, dbm0, dcm0 = _ssd_scan_bwd("ssd_bwd0", xs, dtc, acc, bm, cm, hs0, d_y, 0, nctx)
    dxs1, ddt1, dac1, dbm1, dcm1 = _ssd_scan_bwd("ssd_bwd1", xs, dtc, acc, bm, cm, hs1, d_y, 1, nctx)
    row = lambda arr, wd: (arr, wd, 0, 0)
    d_qkv_raw, d_ab, d_gcw, d_gcb, d_alog16, d_dtb16 = _rowwise_bwd(
        "gdnprep_bwd", f_gdnprep, gp_rows, gp_pars,
        [[row(dq0, D), row(dq1, D)], [row(dk0, D), row(dk1, D)], [row(dv0, D), row(dv1, D)],
         [row(dg0, 128), row(dg1, 128)], [row(db0, 128), row(db1, 128)]],
        [(0, BF16), (1, BF16)], [0, 1, 2, 3], ntl, base=0)
    d_xbc_raw, d_dt, d_scw, d_scb, d_alog64, d_dtb64 = _rowwise_bwd(
        "ssmprep_bwd", f_ssmprep, sp_rows, sp_pars,
        [[row(dxs0, S_INNER), row(dxs1, S_INNER), (d_xs_post, S_INNER, 0, -1, True)], [row(dbm0, 512), row(dbm1, 512)],
         [row(dcm0, 512), row(dcm1, 512)], [row(ddt0, 128), row(ddt1, 128)], [row(dac0, 128), row(dac1, 128)]],
        [(0, BF16), (1, BF16)], [0, 1, 2, 3], ntl, base=0)
    ctx_zero = lambda wd: jnp.zeros((TM, wd), BF16)
    d_proj = jnp.concatenate([
        d_qkv_raw, d_xbc_raw, jnp.concatenate([ctx_zero(S_INNER), d_zs], axis=0),
        jnp.concatenate([ctx_zero(S_INNER), d_gate], axis=0), jnp.concatenate([ctx_zero(D), d_zg], axis=0), d_ab, d_dt], axis=1)
    d_a = _mm("mm_da", d_proj, w_cat, "nt", BF16)
    g_w_cat = _mm("mm_gwcat", a, d_proj, "tn", BF16)
    d_xa, d_n1w, d_sc1, d_sh1, d_csc1, d_csh1 = _rowwise_bwd(
        "pre_bwd", f_pre_thru, [(xa, D, 0, 0)], pre_pars, [[row(d_a, D)], [(d_xres, D, 0, -1, True)]],
        [(0, F32)], [0, 1, 2, 3, 4], ntl, base=0)
    zero4 = jnp.zeros((1, 4 * D), F32)
    d_mod = jnp.concatenate([
        jnp.concatenate([d_sh1, d_sc1, d_g1, d_sh2, d_sc2, d_g2], axis=1),
        jnp.concatenate([d_csh1, d_csc1, zero4], axis=1), jnp.zeros((14, 6 * D), F32)], axis=0)
    g_ada_w = _mm("mm_gwada", a16, d_mod, "tn", BF16)
    d_a16 = _mm("mm_da16", d_mod, full["ada_w"], "nt", F32)
    d_cvec = _rowwise_bwd("silu_c_bwd", f_silu_rows, [(cvec, D, 0, 0)], [], [[row(d_a16, D)]], [(0, F32)], [], 1, tm=16)[0]

    grads_full = dict(w_in=_w_uncat(g_w_cat), ada_w=g_ada_w, w_br_gdn=g_w_br_gdn, w_br_ssm=g_w_br_ssm, w_out=g_w_out,
                      w_ffn_in=g_w_ffn_in, w_ffn_out=g_w_ffn_out)
    recv = _all_to_all("a2a_grads", [_blocks_from_full(grads_full[n], n in BIG_COL_SHARDED) for n in BIG])
    big_un = {n: _reduce_adam("adam_" + n, rcv, wts[n][0], mom1[n][0], mom2[n][0], BIG_ADAM_ROWS[n])
              for n, rcv in zip(BIG, recv)}

    small_g = dict(c_ctx=d_cvec[1], ada_b=d_mod[0] + d_mod[1], norm1_w=d_n1w, gdn_conv_b=d_gcb,
                   gdn_a_log=d_alog16[0, :2 * G_HEADS], gdn_dt_bias=d_dtb16[0, :2 * G_HEADS], gdn_norm_w=d_gnw,
                   ssm_conv_b=d_scb, ssm_a_log=d_alog64[0, :2 * S_HEADS], ssm_dt_bias=d_dtb64[0, :2 * S_HEADS],
                   ssm_d=d_ssd8[0, :S_HEADS], ssm_norm_w=d_snw, norm2_w=d_n2w, norm_f_w=d_nfw,
                   gdn_conv_w=d_gcw.reshape(3, XBC), ssm_conv_w=d_scw.reshape(3, XBC))
    small_names = SMALL + CONV
    sg_pack = _pack([small_g[n] for n in small_names], rows=48)
    recv_s = _all_gather("ag_small_grads", [sg_pack])[0]

    def placed(src, n):
        if n not in CONV:
            return src[n]
        return lax.dynamic_update_slice(jnp.zeros((3, XBC), F32), src[n][0], (0, me_i * CONV_SHARD))

    small_out = _reduce_adam("adam_small", recv_s, *[_pack([placed(src, n) for n in small_names], rows=48)
                                                     for src in (wts, mom1, mom2)], 48)
    small_shapes = [wts[n].shape if n in SMALL else (3, XBC) for n in small_names]
    small_un = [_unpack(buf, small_shapes) for buf in small_out]

    res = [{}, {}, {}, {}]
    for kind in range(4):
        for n in BIG:
            res[kind][n] = big_un[n][kind].reshape(wts[n].shape)
        for n, val in zip(small_names, small_un[kind]):
            if n in CONV:
                val = lax.dynamic_slice(val, (0, me_i * CONV_SHARD), (3, CONV_SHARD)).reshape(wts[n].shape)
            res[kind][n] = val
    loss = lax.psum(loss_acc[0, 0], ("x", "y", "c"))
    grad_x = d_xa[TM:].reshape(x.shape)
    return (loss, grad_x, *[res[0][n] for n in order], *[res[1][n] for n in order], *[res[2][n] for n in order],
            *[res[3][n] for n in order])
```

```python
import functools

import jax
import jax.numpy as jnp
from jax import lax
from jax.experimental import pallas as pl
from jax.experimental.pallas import tpu as pltpu

F32 = jnp.float32
BF16 = jnp.bfloat16
HI = lax.Precision.HIGHEST
HIGH = lax.Precision.HIGH
MESH = pl.DeviceIdType.MESH

D = 1024
CH = 64
TM = 256
EPS = 1e-6
NEG = -1e30
G_HEADS = 8
DK = 128
S_HEADS = 32
S_P = 64
S_GROUPS = 4
S_N = 128
S_INNER = 2048
XBC = 3072
D_FF = 2816
N_DEV = 8
PACK_W = 1024
VMEM_LIMIT = 56 * 1024 * 1024

ADAM_LR = 0.001
ADAM_B1 = 0.9
ADAM_B2 = 0.999
ADAM_EPS = 1e-08
ADAM_WD = 0.01
ADAM_STEP = 10

C_QKV, C_XBC, C_ZS, C_GATE, C_ZG, C_AB, C_DT, C_END = 0, 3072, 6144, 8192, 10240, 11264, 11392, 11520
O_QKV, O_ZG, O_AB, O_ZS, O_XBC, O_DT, O_GATE, O_END = 0, 3072, 4096, 4128, 6176, 9248, 9312, 11360


def _dot(a, b, prec=None):
    return jnp.dot(a, b, precision=prec, preferred_element_type=F32)


def _dot_nt(a, b, prec=None):
    return lax.dot_general(a, b, (((1,), (1,)), ((), ())), precision=prec, preferred_element_type=F32)


def _dot_tn(a, b, prec=None):
    return lax.dot_general(a, b, (((0,), (0,)), ((), ())), precision=prec, preferred_element_type=F32)


def _iota(shape, dim):
    return lax.broadcasted_iota(jnp.int32, shape, dim)


def _rms(x):
    return x * lax.rsqrt(jnp.mean(x * x, axis=-1, keepdims=True) + EPS)


def _l2n(x):
    return x * lax.rsqrt(jnp.sum(x * x, axis=-1, keepdims=True) + EPS)


def _silu(x):
    return x * jax.nn.sigmoid(x)


def _softplus(x):
    return jnp.maximum(x, 0.0) + jnp.log1p(jnp.exp(-jnp.abs(x)))


def _roll_rows(x, s):
    return pltpu.roll(x, s, 0)


def _up_raw(x, keep_up):
    return jnp.where(keep_up > 0.0, _roll_rows(x, 1), 0.0)


def _dn_raw(x, keep_dn):
    return jnp.where(keep_dn > 0.0, _roll_rows(x, x.shape[0] - 1), 0.0)


@jax.custom_vjp
def _shift_up(x, keep_up, keep_dn):
    return _up_raw(x, keep_up)


def _shift_up_fwd(x, keep_up, keep_dn):
    return _up_raw(x, keep_up), (keep_up, keep_dn)


def _shift_up_bwd(res, g):
    keep_up, keep_dn = res
    return _dn_raw(g, keep_dn), jnp.zeros_like(keep_up), jnp.zeros_like(keep_dn)


_shift_up.defvjp(_shift_up_fwd, _shift_up_bwd)


@jax.custom_vjp
def _shift_dn(x, keep_up, keep_dn):
    return _dn_raw(x, keep_dn)


def _shift_dn_fwd(x, keep_up, keep_dn):
    return _dn_raw(x, keep_dn), (keep_up, keep_dn)


def _shift_dn_bwd(res, g):
    keep_up, keep_dn = res
    return _up_raw(g, keep_up), jnp.zeros_like(keep_up), jnp.zeros_like(keep_dn)


_shift_dn.defvjp(_shift_dn_fwd, _shift_dn_bwd)


def _conv_keep(is_ctx, n):
    r = _iota((n, 1), 0)
    pos = jnp.where(is_ctx, r, r & (CH - 1))
    end = jnp.where(is_ctx, n - 1, CH - 1)
    return jnp.where(pos == 0, 0.0, 1.0).astype(F32), jnp.where(pos == end, 0.0, 1.0).astype(F32)


def _conv_silu(u, w3, b, keep_up, keep_dn):
    conv = b + _shift_up(u, keep_up, keep_dn) * w3[0] + u * w3[1] + _shift_dn(u, keep_up, keep_dn) * w3[2]
    return _silu(conv)


def _chunk_tri(n, rev):
    i = _iota((n, n), 0)
    j = _iota((n, n), 1)
    same = (i // CH) == (j // CH)
    seen = (i <= j) if rev else (i >= j)
    return jnp.where(same & seen, 1.0, 0.0).astype(F32)


def _expand_mat(rows, cols, per, base):
    r = _iota((rows, cols), 0)
    c = _iota((rows, cols), 1)
    return jnp.where(r == base + c // per, 1.0, 0.0).astype(F32)


def f_silu_rows(is_ctx, cvec):
    return (_silu(cvec).astype(BF16),)


def f_pre(is_ctx, x, n1w, sc, sh, csc, csh):
    sc_e = jnp.where(is_ctx, csc, sc)
    sh_e = jnp.where(is_ctx, csh, sh)
    a = _rms(x) * n1w * (1.0 + sc_e) + sh_e
    return (a.astype(BF16),)


def f_pre_thru(is_ctx, x, n1w, sc, sh, csc, csh):
    return f_pre(is_ctx, x, n1w, sc, sh, csc, csh)[0], x


def f_gdnprep(is_ctx, qkv_raw, ab_raw, cw, cb, alog, dtb):
    n = qkv_raw.shape[0]
    keep_up, keep_dn = _conv_keep(is_ctx, n)
    s = _conv_silu(qkv_raw, cw, cb, keep_up, keep_dn)
    qs, ks, vs = [], [], []
    for h in range(G_HEADS):
        qs.append(_l2n(s[:, h * DK:(h + 1) * DK]) * (DK ** -0.5))
        ks.append(_l2n(s[:, D + h * DK:D + (h + 1) * DK]))
    q = jnp.concatenate(qs, axis=1)
    k = jnp.concatenate(ks, axis=1)
    v = s[:, 2 * D:3 * D]
    lane = _iota(ab_raw.shape, 1)
    g = jnp.where(lane < 2 * G_HEADS, -jnp.exp(alog) * _softplus(ab_raw + dtb), 0.0)
    gcum = jnp.where(lane < G_HEADS, _dot(_chunk_tri(n, False), g, HI), _dot(_chunk_tri(n, True), g, HI))
    beta = jax.nn.sigmoid(ab_raw)
    return q, k, v, gcum, beta


def f_ssmprep(is_ctx, xbc_raw, dt_raw, cw, cb, alog, dtb):
    n = xbc_raw.shape[0]
    keep_up, keep_dn = _conv_keep(is_ctx, n)
    s = _conv_silu(xbc_raw, cw, cb, keep_up, keep_dn)
    xs = s[:, :S_INNER]
    bm = s[:, S_INNER:S_INNER + S_GROUPS * S_N]
    cm = s[:, S_INNER + S_GROUPS * S_N:]
    lane = _iota(dt_raw.shape, 1)
    dt = jnp.where(lane < 2 * S_HEADS, _softplus(dt_raw + dtb), 0.0)
    da = dt * (-jnp.exp(alog))
    acum = jnp.where(lane < S_HEADS, _dot(_chunk_tri(n, False), da, HI), _dot(_chunk_tri(n, True), da, HI))
    return xs, bm, cm, dt, acum


def f_post(is_ctx, o_f, o_b, zg, y_f, y_b, xs, zs, gnw, ssd8, snw):
    o = o_f + o_b
    ogs = []
    for h in range(G_HEADS):
        sl = slice(h * DK, (h + 1) * DK)
        ogs.append(_rms(o[:, sl]) * gnw * _silu(zg[:, sl]))
    og = jnp.concatenate(ogs, axis=1)
    row0 = jnp.where(_iota(ssd8.shape, 0) == 0, 1.0, 0.0).astype(F32)
    dexp = jnp.sum(_dot(ssd8 * row0, _expand_mat(128, S_INNER, S_P, 0), HI), axis=0, keepdims=True)
    y = (y_f + y_b + dexp * xs) * _silu(zs)
    gw = S_INNER // S_GROUPS
    ys = jnp.concatenate([_rms(y[:, i * gw:(i + 1) * gw]) * snw[:, i * gw:(i + 1) * gw] for i in range(S_GROUPS)], axis=1)
    return og.astype(BF16), ys.astype(BF16)


def f_merge(is_ctx, gate, pg, ps):
    m = jax.nn.sigmoid(gate[:, :D]) * pg + jax.nn.sigmoid(gate[:, D:]) * ps
    return (m.astype(BF16),)


def f_res1(is_ctx, x, mix, g1, n2w, sc2, sh2):
    h1 = x + g1 * mix
    f = _rms(h1) * n2w * (1.0 + sc2) + sh2
    return h1, f.astype(BF16)


def f_act(is_ctx, u):
    return ((_silu(u[:, :D_FF]) * u[:, D_FF:]).astype(BF16),)


def f_final(is_ctx, h1, ff, tgt, g2, nfw):
    h2 = h1 + g2 * ff
    y = _rms(h2) * nfw
    err = y - tgt
    return (0.5 * jnp.sum(jnp.mean(err * err, axis=-1, keepdims=True), axis=0, keepdims=True),)


def _each(fn, *lists):
    return [fn(*args) for args in zip(*lists)]


def _tri_inverse_all(mats):
    n = mats[0].shape[0]
    eye = jnp.where(_iota((n, n), 0) == _iota((n, n), 1), 1.0, 0.0).astype(F32)
    t = [eye - a for a in mats]
    p = [_dot(a, a, HIGH) for a in mats]
    for r in range(5):
        t = _each(lambda t_, p_: t_ + _dot(t_, p_, HIGH), t, p)
        if r < 4:
            p = [_dot(p_, p_, HIGH) for p_ in p]
    return t


@jax.custom_vjp
def _inverse_given(a, t):
    return t


def _inverse_given_fwd(a, t):
    return t, t


def _inverse_given_bwd(t, g):
    return -_dot_nt(_dot_tn(t, g, HIGH), t, HIGH), jnp.zeros_like(t)


_inverse_given.defvjp(_inverse_given_fwd, _inverse_given_bwd)


def gdn_chunk(ss, qs, ks, vs, gcs, grs, bcs, rev, t_known=None):
    c = qs[0].shape[0]
    ii = _iota((c, c), 0)
    jj = _iota((c, c), 1)
    incl = (ii <= jj) if rev else (ii >= jj)
    strict = (ii < jj) if rev else (ii > jj)
    last = 0 if rev else c - 1
    is_last = _iota((c, 1), 0) == last
    decay = _each(lambda gc, gr: jnp.exp(jnp.where(incl, gc - gr, NEG)), gcs, grs)
    kb = _each(lambda k, bc: k * bc, ks, bcs)
    a = _each(lambda kb_, k, dc: jnp.where(strict, _dot_nt(kb_, k) * dc, 0.0), kb, ks, decay)
    t = _tri_inverse_all(a) if t_known is None else _each(_inverse_given, a, t_known)
    eg = [jnp.exp(gc) for gc in gcs]
    rhs = _each(lambda kb_, eg_, v, bc: jnp.concatenate([kb_ * eg_, v * bc], axis=1), kb, eg, vs, bcs)
    wu = _each(lambda t_, r: _dot(t_, r, HIGH), t, rhs)
    lhs = _each(lambda wu_, q, eg_: jnp.concatenate([wu_[:, :DK], q * eg_], axis=0), wu, qs, eg)
    ws = _each(_dot, lhs, ss)
    v_new = _each(lambda wu_, ws_: wu_[:, DK:] - ws_[:c], wu, ws)
    attn = _each(lambda q, k, dc: _dot_nt(q, k) * dc, qs, ks, decay)
    o = _each(lambda ws_, at, vn: ws_[c:] + _dot(at, vn), ws, attn, v_new)
    gtot = [jnp.sum(jnp.where(is_last, gc, 0.0), axis=0, keepdims=True) for gc in gcs]
    s_new = _each(lambda s, k, gc, gt_, vn: s * jnp.exp(gt_) + _dot_tn(k * jnp.exp(gt_ - gc), vn), ss, ks, gcs, gtot, v_new)
    return s_new, o, t


def ssd_pick(dt0s, dt1s, ac0s, ac1s, ar0s, ar1s):
    lo = _iota((CH, 128), 1) < S_P
    pick = lambda u0, u1: jnp.where(lo, u0, u1)
    return _each(pick, dt0s, dt1s), _each(pick, ac0s, ac1s), _each(pick, ar0s, ar1s)


def ssd_chunk(hts, xs, dts, acs, acr, bgs, cgs, rev):
    c = xs[0].shape[0]
    npair = len(xs)
    grp = [p * len(bgs) // npair for p in range(npair)]
    lane = _iota((c, 128), 1)
    ii = _iota((c, 128), 0)
    jl = lane & (S_P - 1)
    lo = lane < S_P
    seen = (ii <= jl) if rev else (ii >= jl)
    last = 0 if rev else c - 1
    split = lambda z: jnp.concatenate([jnp.where(lo, z, 0.0), jnp.where(lo, 0.0, z)], axis=0)
    cb = _each(lambda bg, cg: _dot_nt(cg, jnp.concatenate([bg, bg], axis=0)), bgs, cgs)
    seg = _each(lambda ac, ar: jnp.exp(jnp.where(seen, ac - ar, NEG)), acs, acr)
    xdt = _each(lambda x, dt: x * dt, xs, dts)
    ydiag = [_dot(cb[grp[p]] * seg[p], split(xdt[p])) for p in range(npair)]
    yoff = [_dot(cgs[grp[p]], hts[p]) * jnp.exp(acs[p]) for p in range(npair)]
    atot = [jnp.sum(jnp.where(ii == last, ac, 0.0), axis=0, keepdims=True) for ac in acs]
    h_new = [hts[p] * jnp.exp(atot[p]) + _dot_tn(bgs[grp[p]], xdt[p] * jnp.exp(atot[p] - acs[p])) for p in range(npair)]
    return h_new, _each(lambda a_, b_: a_ + b_, ydiag, yoff)


def f_adamw(w, g, m, v):
    m = ADAM_B1 * m + (1.0 - ADAM_B1) * g
    v = ADAM_B2 * v + (1.0 - ADAM_B2) * jnp.square(g)
    m_hat = m / (1.0 - ADAM_B1 ** ADAM_STEP)
    v_hat = v / (1.0 - ADAM_B2 ** ADAM_STEP)
    delta = -ADAM_LR * (m_hat / (jnp.sqrt(v_hat) + ADAM_EPS) + ADAM_WD * w)
    return delta, m, v


def _cparams(sem):
    return pltpu.CompilerParams(dimension_semantics=sem, vmem_limit_bytes=VMEM_LIMIT)


def _pick(n, target):
    if n <= target:
        return n
    best = None
    for t in range(128, target + 1, 128):
        if n % t == 0:
            best = t
    assert best is not None, (n, target)
    return best


def _row_spec(tm, width, colblk, rowoff):
    return pl.BlockSpec((tm, width), lambda i: (i + rowoff, colblk))


def _par_spec(shape):
    nd = len(shape)
    return pl.BlockSpec(tuple(shape), lambda i: (0,) * nd)


def _rowwise(name, fn, rows, pars, outs, ntiles, base=1, tm=TM):
    nr, npar = len(rows), len(pars)

    def body(*refs):
        is_ctx = (pl.program_id(0) + base) == 0
        res = fn(is_ctx, *[r[...] for r in refs[:nr]], *[p[...] for p in refs[nr:nr + npar]])
        for o_ref, r in zip(refs[nr + npar:], res):
            o_ref[...] = r.astype(o_ref.dtype)

    return pl.pallas_call(
        body, grid=(ntiles,), name=name,
        in_specs=[_row_spec(tm, wd, cb, ro) for (_, wd, cb, ro) in rows] + [_par_spec(p.shape) for p in pars],
        out_specs=[_row_spec(tm, wd, 0, 0) for (wd, _) in outs],
        out_shape=[jax.ShapeDtypeStruct((ntiles * tm, wd), dt) for (wd, dt) in outs],
        compiler_params=_cparams(("arbitrary",)),
    )(*[r[0] for r in rows], *pars)


def _ct_spec(tm, desc):
    _, wd, cb, ro = desc[:4]
    if len(desc) > 4 and desc[4]:
        return pl.BlockSpec((tm, wd), lambda i: (jnp.maximum(i + ro, 0), cb))
    return _row_spec(tm, wd, cb, ro)


def _rowwise_bwd(name, fn, rows, pars, cts, drows, dpars, ntiles, base=1, loss_out=False, tm=TM):
    nr, npar = len(rows), len(pars)
    ct_rows = [d for ct in cts if isinstance(ct, list) for d in ct]
    nct = len(ct_rows)

    def body(*refs):
        i = pl.program_id(0)
        is_ctx = (i + base) == 0
        rows_v = [r[...] for r in refs[:nr]]
        pars_v = [p[...] for p in refs[nr:nr + npar]]
        ct_refs = list(refs[nr + npar:nr + npar + nct])
        out_refs = list(refs[nr + npar + nct:])
        outs, vjp = jax.vjp(lambda rv, pv: fn(is_ctx, *rv, *pv), rows_v, pars_v)

        def ct_value(desc):
            val = ct_refs.pop(0)[...].astype(F32)
            if len(desc) > 4 and desc[4]:
                val = jnp.where(is_ctx, 0.0, val)
            return val

        ct_vals = []
        for o, ct in zip(outs, cts):
            if ct is None:
                ct_vals.append(jnp.zeros_like(o))
            elif isinstance(ct, str):
                ct_vals.append(jnp.ones_like(o))
            else:
                acc = ct_value(ct[0])
                for desc in ct[1:]:
                    acc = acc + ct_value(desc)
                ct_vals.append(acc.astype(o.dtype))
        d_rows, d_pars = vjp(tuple(ct_vals))
        for (ri, _), o_ref in zip(drows, out_refs[:len(drows)]):
            o_ref[...] = d_rows[ri].astype(o_ref.dtype)
        acc_refs = out_refs[len(drows):]
        acc_vals = [d_pars[pi] for pi in dpars]
        if loss_out:
            acc_vals.append(jnp.broadcast_to(outs[0], (8, 128)))

        @pl.when(i == 0)
        def _():
            for o_ref, val in zip(acc_refs, acc_vals):
                o_ref[...] = val

        @pl.when(i > 0)
        def _():
            for o_ref, val in zip(acc_refs, acc_vals):
                o_ref[...] += val

    acc_shapes = [pars[pi].shape for pi in dpars] + ([(8, 128)] if loss_out else [])
    return pl.pallas_call(
        body, grid=(ntiles,), name=name,
        in_specs=[_row_spec(tm, wd, cb, ro) for (_, wd, cb, ro) in rows] + [_par_spec(p.shape) for p in pars]
        + [_ct_spec(tm, d) for d in ct_rows],
        out_specs=[_row_spec(tm, rows[ri][1], 0, 0) for (ri, _) in drows] + [_par_spec(s) for s in acc_shapes],
        out_shape=[jax.ShapeDtypeStruct((ntiles * tm, rows[ri][1]), dt) for (ri, dt) in drows]
        + [jax.ShapeDtypeStruct(tuple(s), F32) for s in acc_shapes],
        compiler_params=_cparams(("arbitrary",)),
    )(*[r[0] for r in rows], *pars, *[r[0] for r in ct_rows])


def _mm(name, a, b, mode, out_dtype, tm=1024, tn=1024, tk=1024):
    if mode == "nn":
        (m, kd), (_, n) = a.shape, b.shape
    elif mode == "nt":
        (m, kd), (n, _) = a.shape, b.shape
    else:
        (kd, m), (_, n) = a.shape, b.shape
    tm, tn, tk = _pick(m, tm), _pick(n, tn), _pick(kd, tk)
    nk = kd // tk
    a_spec = {"nn": pl.BlockSpec((tm, tk), lambda i, j, k: (i, k)), "nt": pl.BlockSpec((tm, tk), lambda i, j, k: (i, k)),
              "tn": pl.BlockSpec((tk, tm), lambda i, j, k: (k, i))}[mode]
    b_spec = {"nn": pl.BlockSpec((tk, tn), lambda i, j, k: (k, j)), "nt": pl.BlockSpec((tn, tk), lambda i, j, k: (j, k)),
              "tn": pl.BlockSpec((tk, tn), lambda i, j, k: (k, j))}[mode]
    dot = {"nn": _dot, "nt": _dot_nt, "tn": _dot_tn}[mode]

    if nk == 1:
        def body(a_ref, b_ref, o_ref):
            o_ref[...] = dot(a_ref[...].astype(BF16), b_ref[...].astype(BF16)).astype(o_ref.dtype)
    else:
        def body(a_ref, b_ref, o_ref, acc_ref):
            k = pl.program_id(2)
            part = dot(a_ref[...].astype(BF16), b_ref[...].astype(BF16))

            @pl.when(k == 0)
            def _():
                acc_ref[...] = part

            @pl.when((k > 0) & (k < nk - 1))
            def _():
                acc_ref[...] += part

            @pl.when(k == nk - 1)
            def _():
                o_ref[...] = (acc_ref[...] + part).astype(o_ref.dtype)

    return pl.pallas_call(
        body, grid=(m // tm, n // tn, nk), name=name,
        in_specs=[a_spec, b_spec],
        out_specs=pl.BlockSpec((tm, tn), lambda i, j, k: (i, j)),
        out_shape=jax.ShapeDtypeStruct((m, n), out_dtype),
        scratch_shapes=[] if nk == 1 else [pltpu.VMEM((tm, tn), F32)],
        compiler_params=_cparams(("arbitrary", "arbitrary", "arbitrary")),
    )(a, b)


def _chunk_index(i, nch, nctx, rev):
    if not rev:
        return i
    return jnp.where(i < nctx, nctx - 1 - i, nch + nctx - 1 - i)


def _gdn_cols(d):
    return [d * G_HEADS + h for h in range(G_HEADS)], [2 * G_HEADS + d * G_HEADS + h for h in range(G_HEADS)]


def _gdn_operands(q_ref, k_ref, v_ref, g_ref, b_ref, d):
    cols_g, cols_b = _gdn_cols(d)
    sls = [slice(h * DK, (h + 1) * DK) for h in range(G_HEADS)]
    gt, bt = g_ref[...], b_ref[...]
    gtt = gt.T
    qs = [q_ref[:, sl] for sl in sls]
    ks = [k_ref[:, sl] for sl in sls]
    vs = [v_ref[:, sl] for sl in sls]
    gcs = [gt[:, cg:cg + 1] for cg in cols_g]
    grs = [gtt[cg:cg + 1, :] for cg in cols_g]
    bcs = [bt[:, cb:cb + 1] for cb in cols_b]
    return sls, qs, ks, vs, gcs, grs, bcs


def _gdn_scan_fwd(name, q, k, v, gcum, beta, d, nctx):
    t = q.shape[0]
    nch = t // CH
    rev = d == 1
    cix = lambda i: _chunk_index(i, nch, nctx, rev)
    full = pl.BlockSpec((CH, D), lambda i: (cix(i), 0))
    wide = pl.BlockSpec((CH, 128), lambda i: (cix(i), 0))

    def body(q_ref, k_ref, v_ref, g_ref, b_ref, o_ref, ss_ref, ts_ref, s_scr):
        @pl.when(pl.program_id(0) == 0)
        def _():
            s_scr[...] = jnp.zeros(s_scr.shape, F32)

        sls, qs, ks, vs, gcs, grs, bcs = _gdn_operands(q_ref, k_ref, v_ref, g_ref, b_ref, d)
        ss = [s_scr[h] for h in range(G_HEADS)]
        s_new, o, t = gdn_chunk(ss, qs, ks, vs, gcs, grs, bcs, rev)
        for h in range(G_HEADS):
            ss_ref[0, h] = ss[h]
            ts_ref[0, h] = t[h]
            s_scr[h] = s_new[h]
            o_ref[:, sls[h]] = o[h]

    return pl.pallas_call(
        body, grid=(nch,), name=name,
        in_specs=[full, full, full, wide, wide],
        out_specs=[full, pl.BlockSpec((1, G_HEADS, DK, DK), lambda i: (i, 0, 0, 0)),
                   pl.BlockSpec((1, G_HEADS, CH, CH), lambda i: (i, 0, 0, 0))],
        out_shape=[jax.ShapeDtypeStruct((t, D), F32), jax.ShapeDtypeStruct((nch, G_HEADS, DK, DK), F32),
                   jax.ShapeDtypeStruct((nch, G_HEADS, CH, CH), F32)],
        scratch_shapes=[pltpu.VMEM((G_HEADS, DK, DK), F32)],
        compiler_params=_cparams(("arbitrary",)),
    )(q, k, v, gcum, beta)


def _gdn_scan_bwd(name, q, k, v, gcum, beta, ssave, tsave, do, d, nctx):
    t = q.shape[0]
    nch = t // CH
    rev = d == 1
    cix = lambda ib: _chunk_index(nch - 1 - ib, nch, nctx, rev)
    full = pl.BlockSpec((CH, D), lambda ib: (cix(ib), 0))
    wide = pl.BlockSpec((CH, 128), lambda ib: (cix(ib), 0))
    do_spec = pl.BlockSpec((CH, D), lambda ib: (jnp.maximum(cix(ib), nctx) - nctx, 0))

    def body(q_ref, k_ref, v_ref, g_ref, b_ref, ss_ref, ts_ref, do_ref, dq_ref, dk_ref, dv_ref, dg_ref, db_ref, ds_scr):
        ib = pl.program_id(0)

        @pl.when(ib == 0)
        def _():
            ds_scr[...] = jnp.zeros(ds_scr.shape, F32)

        sls, qs, ks, vs, gcs, grs, bcs = _gdn_operands(q_ref, k_ref, v_ref, g_ref, b_ref, d)
        t_known = [ts_ref[0, h] for h in range(G_HEADS)]
        cols_g, cols_b = _gdn_cols(d)
        is_lat = cix(ib) >= nctx
        ss = [ss_ref[0, h] for h in range(G_HEADS)]
        do_v = [jnp.where(is_lat, do_ref[:, sl], 0.0) for sl in sls]
        ds_in = [ds_scr[h] for h in range(G_HEADS)]
        _, vjp = jax.vjp(lambda *a: gdn_chunk(*a, rev, t_known)[:2], ss, qs, ks, vs, gcs, grs, bcs)
        ds, dq, dk, dv, dgc, dgr, dbc = vjp((ds_in, do_v))
        lane = _iota((CH, 128), 1)
        sub = _iota((128, CH), 0)
        dg = jnp.zeros((CH, 128), F32)
        dgt = jnp.zeros((128, CH), F32)
        db = jnp.zeros((CH, 128), F32)
        for h in range(G_HEADS):
            ds_scr[h] = ds[h]
            dq_ref[:, sls[h]] = dq[h]
            dk_ref[:, sls[h]] = dk[h]
            dv_ref[:, sls[h]] = dv[h]
            dg = dg + jnp.where(lane == cols_g[h], dgc[h], 0.0)
            dgt = dgt + jnp.where(sub == cols_g[h], dgr[h], 0.0)
            db = db + jnp.where(lane == cols_b[h], dbc[h], 0.0)
        dg_ref[...] = dg + dgt.T
        db_ref[...] = db

    return pl.pallas_call(
        body, grid=(nch,), name=name,
        in_specs=[full, full, full, wide, wide,
                  pl.BlockSpec((1, G_HEADS, DK, DK), lambda ib: (nch - 1 - ib, 0, 0, 0)),
                  pl.BlockSpec((1, G_HEADS, CH, CH), lambda ib: (nch - 1 - ib, 0, 0, 0)), do_spec],
        out_specs=[full, full, full, wide, wide],
        out_shape=[jax.ShapeDtypeStruct((t, D), F32)] * 3 + [jax.ShapeDtypeStruct((t, 128), F32)] * 2,
        scratch_shapes=[pltpu.VMEM((G_HEADS, DK, DK), F32)],
        compiler_params=_cparams(("arbitrary",)),
    )(q, k, v, gcum, beta, ssave, tsave, do)


N_PAIRS = S_HEADS // 2


def _ssd_operands(x_ref, dt_ref, ac_ref, b_ref, c_ref, d):
    sls = [slice(p * 128, (p + 1) * 128) for p in range(N_PAIRS)]
    gsl = [slice(g * S_N, (g + 1) * S_N) for g in range(S_GROUPS)]
    cols = [d * S_HEADS + h for h in range(S_HEADS)]
    dtc, acc = dt_ref[...], ac_ref[...]
    act = jnp.concatenate([acc, acc], axis=0).T
    col = lambda z, cc: z[:, cc:cc + 1]
    dts, acs, acr = ssd_pick(
        [col(dtc, cols[2 * p]) for p in range(N_PAIRS)], [col(dtc, cols[2 * p + 1]) for p in range(N_PAIRS)],
        [col(acc, cols[2 * p]) for p in range(N_PAIRS)], [col(acc, cols[2 * p + 1]) for p in range(N_PAIRS)],
        [act[cols[2 * p]:cols[2 * p] + 1, :] for p in range(N_PAIRS)],
        [act[cols[2 * p + 1]:cols[2 * p + 1] + 1, :] for p in range(N_PAIRS)])
    ops = ([x_ref[:, sl] for sl in sls], dts, acs, acr, [b_ref[:, gs] for gs in gsl], [c_ref[:, gs] for gs in gsl])
    return sls, gsl, cols, ops


def _ssd_scan_fwd(name, xs, dtc, acc, bm, cm, d, nctx):
    t = xs.shape[0]
    nch = t // CH
    rev = d == 1
    cix = lambda i: _chunk_index(i, nch, nctx, rev)
    inner = pl.BlockSpec((CH, S_INNER), lambda i: (cix(i), 0))
    wide = pl.BlockSpec((CH, 128), lambda i: (cix(i), 0))
    grp = pl.BlockSpec((CH, S_GROUPS * S_N), lambda i: (cix(i), 0))

    def body(x_ref, dt_ref, ac_ref, b_ref, c_ref, y_ref, hs_ref, h_scr):
        @pl.when(pl.program_id(0) == 0)
        def _():
            h_scr[...] = jnp.zeros(h_scr.shape, F32)

        sls, _, _, ops = _ssd_operands(x_ref, dt_ref, ac_ref, b_ref, c_ref, d)
        hts = [h_scr[p] for p in range(N_PAIRS)]
        h_new, y = ssd_chunk(hts, *ops, rev)
        for p in range(N_PAIRS):
            hs_ref[0, p] = hts[p]
            h_scr[p] = h_new[p]
            y_ref[:, sls[p]] = y[p]

    return pl.pallas_call(
        body, grid=(nch,), name=name,
        in_specs=[inner, wide, wide, grp, grp],
        out_specs=[inner, pl.BlockSpec((1, N_PAIRS, S_N, 128), lambda i: (i, 0, 0, 0))],
        out_shape=[jax.ShapeDtypeStruct((t, S_INNER), F32), jax.ShapeDtypeStruct((nch, N_PAIRS, S_N, 128), F32)],
        scratch_shapes=[pltpu.VMEM((N_PAIRS, S_N, 128), F32)],
        compiler_params=_cparams(("arbitrary",)),
    )(xs, dtc, acc, bm, cm)


def _ssd_scan_bwd(name, xs, dtc, acc, bm, cm, hsave, dy, d, nctx):
    t = xs.shape[0]
    nch = t // CH
    rev = d == 1
    cix = lambda ib: _chunk_index(nch - 1 - ib, nch, nctx, rev)
    inner = pl.BlockSpec((CH, S_INNER), lambda ib: (cix(ib), 0))
    wide = pl.BlockSpec((CH, 128), lambda ib: (cix(ib), 0))
    grp = pl.BlockSpec((CH, S_GROUPS * S_N), lambda ib: (cix(ib), 0))
    dy_spec = pl.BlockSpec((CH, S_INNER), lambda ib: (jnp.maximum(cix(ib), nctx) - nctx, 0))

    def body(x_ref, dt_ref, ac_ref, b_ref, c_ref, hs_ref, dy_ref, dx_ref, ddt_ref, dac_ref, db_ref, dc_ref, dh_scr):
        ib = pl.program_id(0)

        @pl.when(ib == 0)
        def _():
            dh_scr[...] = jnp.zeros(dh_scr.shape, F32)

        sls, gsl, cols, ops = _ssd_operands(x_ref, dt_ref, ac_ref, b_ref, c_ref, d)
        is_lat = cix(ib) >= nctx
        hts = [hs_ref[0, p] for p in range(N_PAIRS)]
        dy_v = [jnp.where(is_lat, dy_ref[:, sl], 0.0) for sl in sls]
        dh_in = [dh_scr[p] for p in range(N_PAIRS)]
        _, vjp = jax.vjp(lambda *a: ssd_chunk(*a, rev), hts, *ops)
        dh, dx, ddts, dacs, dacr, db, dc = vjp((dh_in, dy_v))
        for p in range(N_PAIRS):
            dh_scr[p] = dh[p]
            dx_ref[:, sls[p]] = dx[p]
        r = _iota((S_INNER, 128), 0)
        e_t = jnp.where(_iota((S_INNER, 128), 1) == d * S_HEADS + r // S_P, 1.0, 0.0).astype(F32)
        ddt_ref[...] = _dot(jnp.concatenate(ddts, axis=1), e_t, HIGH)
        dac_cols = _dot(jnp.concatenate(dacs, axis=1), e_t, HIGH)
        sub = _iota((128, 128), 0)
        lane = _iota((128, 128), 1)
        m = jnp.zeros((128, 128), F32)
        for p in range(N_PAIRS):
            m = m + jnp.where(sub == p, jnp.sum(dacr[p], axis=0, keepdims=True), 0.0)
        mt = m.T
        s0 = jnp.where(lane == d * S_HEADS + 2 * sub, 1.0, 0.0).astype(F32)
        s1 = jnp.where(lane == d * S_HEADS + 2 * sub + 1, 1.0, 0.0).astype(F32)
        dac_ref[...] = dac_cols + _dot(mt[:CH], s0, HIGH) + _dot(mt[CH:], s1, HIGH)
        for g in range(S_GROUPS):
            db_ref[:, gsl[g]] = db[g]
            dc_ref[:, gsl[g]] = dc[g]

    return pl.pallas_call(
        body, grid=(nch,), name=name,
        in_specs=[inner, wide, wide, grp, grp,
                  pl.BlockSpec((1, N_PAIRS, S_N, 128), lambda ib: (nch - 1 - ib, 0, 0, 0)), dy_spec],
        out_specs=[inner, wide, wide, grp, grp],
        out_shape=[jax.ShapeDtypeStruct((t, S_INNER), F32)] + [jax.ShapeDtypeStruct((t, 128), F32)] * 2
        + [jax.ShapeDtypeStruct((t, S_GROUPS * S_N), F32)] * 2,
        scratch_shapes=[pltpu.VMEM((N_PAIRS, S_N, 128), F32)],
        compiler_params=_cparams(("arbitrary",)),
    )(xs, dtc, acc, bm, cm, hsave, dy)


def _mesh_pos():
    return lax.axis_index("x"), lax.axis_index("y"), lax.axis_index("c")


def _hbm_specs(n):
    return [pl.BlockSpec(memory_space=pl.ANY)] * n


def _all_gather(name, shards):
    nw = len(shards)

    def body(*refs):
        x_refs, out_refs = refs[:nw], refs[nw:2 * nw]
        send_sems, recv_sems, local_sems = refs[2 * nw:]
        x, y, c = _mesh_pos()
        me, sibling = (x, y, c), (x, y, 1 - c)
        chips = [(1 - x, y), (x, 1 - y), (1 - x, 1 - y)]

        def slot(w, px, py, pc):
            return out_refs[w].at[4 * px + 2 * py + pc]

        def copy(w, k, block, to, src=None):
            return pltpu.make_async_remote_copy(
                src_ref=slot(w, *block) if src is None else src, dst_ref=slot(w, *block),
                send_sem=send_sems.at[w, k], recv_sem=recv_sems.at[w, k], device_id=to, device_id_type=MESH)

        mine = [pltpu.make_async_copy(x_refs[w], slot(w, *me), local_sems.at[w]) for w in range(nw)]
        for cp in mine:
            cp.start()
        first = []
        for w in range(nw):
            first.append(copy(w, 0, me, sibling, src=x_refs[w]))
            first += [copy(w, 1 + j, me, (*chip, c), src=x_refs[w]) for j, chip in enumerate(chips)]
        for cp in first:
            cp.start()
        passed = []
        for j, chip in enumerate(chips):
            for w in range(nw):
                copy(w, 1 + j, (*chip, c), me).wait_recv()
                fwd = copy(w, 4 + j, (*chip, c), sibling)
                fwd.start()
                passed.append(fwd)
        for w in range(nw):
            copy(w, 0, sibling, me).wait_recv()
            for j, chip in enumerate(chips):
                copy(w, 4 + j, (*chip, 1 - c), me).wait_recv()
        for cp in first + passed:
            cp.wait_send()
        for cp in mine:
            cp.wait()

    return pl.pallas_call(
        body, name=name,
        out_shape=[jax.ShapeDtypeStruct((N_DEV,) + xs.shape, xs.dtype) for xs in shards],
        in_specs=_hbm_specs(nw), out_specs=_hbm_specs(nw),
        scratch_shapes=[pltpu.SemaphoreType.DMA((nw, 7)), pltpu.SemaphoreType.DMA((nw, 7)), pltpu.SemaphoreType.DMA((nw,))],
    )(*shards)


def _all_to_all(name, blocks):
    nw = len(blocks)

    def body(*refs):
        g_refs, out_refs = refs[:nw], refs[nw:2 * nw]
        send_sems, recv_sems, local_sems = refs[2 * nw:]
        x, y, c = _mesh_pos()
        me_i = 4 * x + 2 * y + c
        mine = [pltpu.make_async_copy(g_refs[w].at[me_i], out_refs[w].at[me_i], local_sems.at[w]) for w in range(nw)]
        for cp in mine:
            cp.start()
        cps = []
        for k in range(1, N_DEV):
            px = 1 - x if (k >> 2) & 1 else x
            py = 1 - y if (k >> 1) & 1 else y
            pc = 1 - c if k & 1 else c
            for w in range(nw):
                cp = pltpu.make_async_remote_copy(
                    src_ref=g_refs[w].at[4 * px + 2 * py + pc], dst_ref=out_refs[w].at[me_i],
                    send_sem=send_sems.at[w, k - 1], recv_sem=recv_sems.at[w, k - 1],
                    device_id=(px, py, pc), device_id_type=MESH)
                cp.start()
                cps.append(cp)
        for cp in cps:
            cp.wait()
        for cp in mine:
            cp.wait()

    return pl.pallas_call(
        body, name=name,
        out_shape=[jax.ShapeDtypeStruct(g.shape, g.dtype) for g in blocks],
        in_specs=_hbm_specs(nw), out_specs=_hbm_specs(nw),
        scratch_shapes=[pltpu.SemaphoreType.DMA((nw, 7)), pltpu.SemaphoreType.DMA((nw, 7)), pltpu.SemaphoreType.DMA((nw,))],
    )(*blocks)


def _reduce_adam(name, recv, w, m, v, tm):
    rows, width = w.shape

    def body(recv_ref, w_ref, m_ref, v_ref, g_ref, d_ref, m2_ref, v2_ref):
        g = recv_ref[0].astype(F32)
        for s in range(1, N_DEV):
            g = g + recv_ref[s].astype(F32)
        delta, m2, v2 = f_adamw(w_ref[...], g, m_ref[...], v_ref[...])
        g_ref[...] = g
        d_ref[...] = delta
        m2_ref[...] = m2
        v2_ref[...] = v2

    row = pl.BlockSpec((tm, width), lambda i: (i, 0))
    return pl.pallas_call(
        body, grid=(rows // tm,), name=name,
        in_specs=[pl.BlockSpec((N_DEV, tm, width), lambda i: (0, i, 0)), row, row, row],
        out_specs=[row] * 4,
        out_shape=[jax.ShapeDtypeStruct((rows, width), F32)] * 4,
        compiler_params=_cparams(("arbitrary",)),
    )(recv, w, m, v)


BIG = ("w_in", "ada_w", "w_br_gdn", "w_br_ssm", "w_out", "w_ffn_in", "w_ffn_out")
BIG_COL_SHARDED = ("w_in", "ada_w", "w_ffn_in")
BIG_ADAM_ROWS = dict(w_in=128, ada_w=256, w_br_gdn=128, w_br_ssm=256, w_out=128, w_ffn_in=256, w_ffn_out=352)
CONV = ("gdn_conv_w", "ssm_conv_w")
SMALL = ("c_ctx", "ada_b", "norm1_w", "gdn_conv_b", "gdn_a_log", "gdn_dt_bias", "gdn_norm_w", "ssm_conv_b",
         "ssm_a_log", "ssm_dt_bias", "ssm_d", "ssm_norm_w", "norm2_w", "norm_f_w")
CONV_SHARD = XBC // N_DEV


def _to_rows(a):
    flat = a.reshape(-1)
    pad = (-flat.shape[0]) % PACK_W
    if pad:
        flat = jnp.pad(flat, (0, pad))
    return flat.reshape(-1, PACK_W)


def _pack(arrays, rows=None):
    buf = jnp.concatenate([_to_rows(a) for a in arrays], axis=0)
    if rows is not None and rows > buf.shape[0]:
        buf = jnp.pad(buf, ((0, rows - buf.shape[0]), (0, 0)))
    return buf


def _unpack(buf, shapes):
    out, r0 = [], 0
    for shp in shapes:
        n = 1
        for s in shp:
            n *= s
        nr = -(-n // PACK_W)
        out.append(buf[r0:r0 + nr].reshape(-1)[:n].reshape(shp))
        r0 += nr
    return out


def _full_from_blocks(blocks, col_sharded):
    _, r, c = blocks.shape
    if col_sharded:
        return jnp.transpose(blocks, (1, 0, 2)).reshape(r, N_DEV * c)
    return blocks.reshape(N_DEV * r, c)


def _blocks_from_full(full, col_sharded):
    if col_sharded:
        r, c = full.shape[0], full.shape[1] // N_DEV
        return jnp.transpose(full.reshape(r, N_DEV, c), (1, 0, 2))
    return full.reshape(N_DEV, full.shape[0] // N_DEV, full.shape[1])


def _pad_cols(a, n):
    return jnp.pad(a, ((0, 0), (0, n - a.shape[1])))


def _w_cat(w_in):
    return jnp.concatenate([
        w_in[:, O_QKV:O_ZG], w_in[:, O_XBC:O_DT], w_in[:, O_ZS:O_XBC], w_in[:, O_GATE:O_END], w_in[:, O_ZG:O_AB],
        _pad_cols(w_in[:, O_AB:O_ZS], 128), _pad_cols(w_in[:, O_DT:O_GATE], 128)], axis=1)


def _w_uncat(wc):
    return jnp.concatenate([
        wc[:, C_QKV:C_XBC], wc[:, C_ZG:C_AB], wc[:, C_AB:C_AB + (O_ZS - O_AB)], wc[:, C_ZS:C_GATE], wc[:, C_XBC:C_ZS],
        wc[:, C_DT:C_DT + (O_GATE - O_DT)], wc[:, C_GATE:C_ZG]], axis=1)


def _pad_row(vec, n=128):
    vec = vec.reshape(1, -1)
    return _pad_cols(vec, n)


def kernel(x, c, ctx, c_ctx, ada_w, ada_b, norm1_w, w_in, gdn_conv_w, gdn_conv_b, gdn_a_log, gdn_dt_bias, gdn_norm_w, ssm_conv_w, ssm_conv_b, ssm_a_log, ssm_dt_bias, ssm_d, ssm_norm_w, w_br_gdn, w_br_ssm, w_out, norm2_w, w_ffn_in, w_ffn_out, norm_f_w, loss_target, m_c_ctx, m_ada_w, m_ada_b, m_norm1_w, m_w_in, m_gdn_conv_w, m_gdn_conv_b, m_gdn_a_log, m_gdn_dt_bias, m_gdn_norm_w, m_ssm_conv_w, m_ssm_conv_b, m_ssm_a_log, m_ssm_dt_bias, m_ssm_d, m_ssm_norm_w, m_w_br_gdn, m_w_br_ssm, m_w_out, m_norm2_w, m_w_ffn_in, m_w_ffn_out, m_norm_f_w, v_c_ctx, v_ada_w, v_ada_b, v_norm1_w, v_w_in, v_gdn_conv_w, v_gdn_conv_b, v_gdn_a_log, v_gdn_dt_bias, v_gdn_norm_w, v_ssm_conv_w, v_ssm_conv_b, v_ssm_a_log, v_ssm_dt_bias, v_ssm_d, v_ssm_norm_w, v_w_br_gdn, v_w_br_ssm, v_w_out, v_norm2_w, v_w_ffn_in, v_w_ffn_out, v_norm_f_w):
    wts = dict(c_ctx=c_ctx, ada_w=ada_w, ada_b=ada_b, norm1_w=norm1_w, w_in=w_in, gdn_conv_w=gdn_conv_w, gdn_conv_b=gdn_conv_b, gdn_a_log=gdn_a_log, gdn_dt_bias=gdn_dt_bias, gdn_norm_w=gdn_norm_w, ssm_conv_w=ssm_conv_w, ssm_conv_b=ssm_conv_b, ssm_a_log=ssm_a_log, ssm_dt_bias=ssm_dt_bias, ssm_d=ssm_d, ssm_norm_w=ssm_norm_w, w_br_gdn=w_br_gdn, w_br_ssm=w_br_ssm, w_out=w_out, norm2_w=norm2_w, w_ffn_in=w_ffn_in, w_ffn_out=w_ffn_out, norm_f_w=norm_f_w)
    mom1 = dict(c_ctx=m_c_ctx, ada_w=m_ada_w, ada_b=m_ada_b, norm1_w=m_norm1_w, w_in=m_w_in, gdn_conv_w=m_gdn_conv_w, gdn_conv_b=m_gdn_conv_b, gdn_a_log=m_gdn_a_log, gdn_dt_bias=m_gdn_dt_bias, gdn_norm_w=m_gdn_norm_w, ssm_conv_w=m_ssm_conv_w, ssm_conv_b=m_ssm_conv_b, ssm_a_log=m_ssm_a_log, ssm_dt_bias=m_ssm_dt_bias, ssm_d=m_ssm_d, ssm_norm_w=m_ssm_norm_w, w_br_gdn=m_w_br_gdn, w_br_ssm=m_w_br_ssm, w_out=m_w_out, norm2_w=m_norm2_w, w_ffn_in=m_w_ffn_in, w_ffn_out=m_w_ffn_out, norm_f_w=m_norm_f_w)
    mom2 = dict(c_ctx=v_c_ctx, ada_w=v_ada_w, ada_b=v_ada_b, norm1_w=v_norm1_w, w_in=v_w_in, gdn_conv_w=v_gdn_conv_w, gdn_conv_b=v_gdn_conv_b, gdn_a_log=v_gdn_a_log, gdn_dt_bias=v_gdn_dt_bias, gdn_norm_w=v_gdn_norm_w, ssm_conv_w=v_ssm_conv_w, ssm_conv_b=v_ssm_conv_b, ssm_a_log=v_ssm_a_log, ssm_dt_bias=v_ssm_dt_bias, ssm_d=v_ssm_d, ssm_norm_w=v_ssm_norm_w, w_br_gdn=v_w_br_gdn, w_br_ssm=v_w_br_ssm, w_out=v_w_out, norm2_w=v_norm2_w, w_ffn_in=v_w_ffn_in, w_ffn_out=v_w_ffn_out, norm_f_w=v_norm_f_w)
    order = list(wts)

    seq = x.shape[1]
    t = TM + seq
    ntl, nlt, nctx = t // TM, seq // TM, TM // CH

    me_i = 4 * lax.axis_index("x") + 2 * lax.axis_index("y") + lax.axis_index("c")
    gathered = _all_gather("ag_weights", [wts[n][0].astype(BF16) for n in BIG])
    full = {n: _full_from_blocks(blk, n in BIG_COL_SHARDED) for n, blk in zip(BIG, gathered)}
    conv_sh = _pack([wts[n] for n in CONV], rows=8)
    conv_g = _all_gather("ag_conv", [conv_sh])[0].reshape(N_DEV, -1)
    ncv = 3 * CONV_SHARD
    for i, n in enumerate(CONV):
        off = -(-ncv // PACK_W) * PACK_W * i
        full[n] = jnp.transpose(conv_g[:, off:off + ncv].reshape(N_DEV, 3, CONV_SHARD), (1, 0, 2)).reshape(3, XBC)
    w_cat = _w_cat(full["w_in"])
    gcw = full["gdn_conv_w"].reshape(3, 1, XBC)
    scw = full["ssm_conv_w"].reshape(3, 1, XBC)

    n1w, n2w, nfw = norm1_w.reshape(1, D), norm2_w.reshape(1, D), norm_f_w.reshape(1, D)
    gcb, scb = gdn_conv_b.reshape(1, XBC), ssm_conv_b.reshape(1, XBC)
    alog16, dtb16 = _pad_row(gdn_a_log), _pad_row(gdn_dt_bias)
    alog64, dtb64 = _pad_row(ssm_a_log), _pad_row(ssm_dt_bias)
    gnw = gdn_norm_w.reshape(1, DK)
    ssd8 = jnp.tile(_pad_row(ssm_d), (8, 1))
    snw = ssm_norm_w.reshape(1, S_INNER)
    x2 = x[0]
    tgt = loss_target[0]
    xa = jnp.concatenate([ctx[0], x2], axis=0)
    cvec = jnp.concatenate([c, c_ctx.reshape(1, D), jnp.zeros((14, D), F32)], axis=0)

    a16 = _rowwise("silu_c", f_silu_rows, [(cvec, D, 0, 0)], [], [(D, BF16)], 1, tm=16)[0]
    mod = _mm("mm_mod", a16, full["ada_w"], "nn", F32) + ada_b
    sh1, sc1, g1, sh2, sc2, g2 = [mod[0:1, i * D:(i + 1) * D] for i in range(6)]
    csh1, csc1 = mod[1:2, 0:D], mod[1:2, D:2 * D]

    pre_pars = [n1w, sc1, sh1, csc1, csh1]
    a = _rowwise("pre", f_pre, [(xa, D, 0, 0)], pre_pars, [(D, BF16)], ntl, base=0)[0]
    proj = _mm("mm_proj", a, w_cat, "nn", F32, tm=1408, tn=1280)
    gp_rows = [(proj, XBC, C_QKV // XBC, 0), (proj, 128, C_AB // 128, 0)]
    gp_pars = [gcw, gcb, alog16, dtb16]
    q, k, v, gcum, beta = _rowwise("gdnprep", f_gdnprep, gp_rows, gp_pars, [(D, F32)] * 3 + [(128, F32)] * 2, ntl, base=0)
    sp_rows = [(proj, XBC, C_XBC // XBC, 0), (proj, 128, C_DT // 128, 0)]
    sp_pars = [scw, scb, alog64, dtb64]
    xs, bm, cm, dtc, acc = _rowwise(
        "ssmprep", f_ssmprep, sp_rows, sp_pars, [(S_INNER, F32), (512, F32), (512, F32), (128, F32), (128, F32)], ntl, base=0)
    o0, ss0, ts0 = _gdn_scan_fwd("gdn_fwd0", q, k, v, gcum, beta, 0, nctx)
    o1, ss1, ts1 = _gdn_scan_fwd("gdn_fwd1", q, k, v, gcum, beta, 1, nctx)
    y0, hs0 = _ssd_scan_fwd("ssd_fwd0", xs, dtc, acc, bm, cm, 0, nctx)
    y1, hs1 = _ssd_scan_fwd("ssd_fwd1", xs, dtc, acc, bm, cm, 1, nctx)
    post_rows = [(o0, D, 0, 1), (o1, D, 0, 1), (proj, D, C_ZG // D, 1), (y0, S_INNER, 0, 1), (y1, S_INNER, 0, 1),
                 (xs, S_INNER, 0, 1), (proj, S_INNER, C_ZS // S_INNER, 1)]
    post_pars = [gnw, ssd8, snw]
    og, ys = _rowwise("post", f_post, post_rows, post_pars, [(D, BF16), (S_INNER, BF16)], nlt)
    pg = _mm("mm_pg", og, full["w_br_gdn"], "nn", F32)
    ps = _mm("mm_ps", ys, full["w_br_ssm"], "nn", F32, tk=2048)
    merge_rows = [(proj, S_INNER, C_GATE // S_INNER, 1), (pg, D, 0, 0), (ps, D, 0, 0)]
    merged = _rowwise("merge", f_merge, merge_rows, [], [(D, BF16)], nlt)[0]
    mix = _mm("mm_mix", merged, full["w_out"], "nn", F32)
    res_rows = [(x2, D, 0, 0), (mix, D, 0, 0)]
    res_pars = [g1, n2w, sc2, sh2]
    h1, f = _rowwise("res1", f_res1, res_rows, res_pars, [(D, F32), (D, BF16)], nlt)
    u = _mm("mm_u", f, full["w_ffn_in"], "nn", F32, tn=1408)
    hact = _rowwise("act", f_act, [(u, 2 * D_FF, 0, 0)], [], [(D_FF, BF16)], nlt)[0]
    ff = _mm("mm_ff", hact, full["w_ffn_out"], "nn", F32, tk=2816)

    fin_rows = [(h1, D, 0, 0), (ff, D, 0, 0), (tgt, D, 0, 0)]
    d_h1a, d_ff, d_g2, d_nfw, loss_acc = _rowwise_bwd(
        "final", f_final, fin_rows, [g2, nfw], ["one"], [(0, F32), (1, BF16)], [0, 1], nlt, loss_out=True)
    d_hact = _mm("mm_dhact", d_ff, full["w_ffn_out"], "nt", BF16, tn=1408)
    g_w_ffn_out = _mm("mm_gwffo", hact, d_ff, "tn", BF16, tm=1408, tk=2048)
    d_u = _rowwise_bwd("act_bwd", f_act, [(u, 2 * D_FF, 0, 0)], [], [[(d_hact, D_FF, 0, 0)]], [(0, BF16)], [], nlt)[0]
    d_f = _mm("mm_df", d_u, full["w_ffn_in"], "nt", BF16, tk=2816)
    g_w_ffn_in = _mm("mm_gwffi", f, d_u, "tn", BF16, tn=1408, tk=2048)
    d_xres, d_mix, d_g1, d_n2w, d_sc2, d_sh2 = _rowwise_bwd(
        "res1_bwd", f_res1, res_rows, res_pars, [[(d_h1a, D, 0, 0)], [(d_f, D, 0, 0)]], [(0, F32), (1, BF16)], [0, 1, 2, 3], nlt)
    d_merged = _mm("mm_dmerged", d_mix, full["w_out"], "nt", BF16)
    g_w_out = _mm("mm_gwout", merged, d_mix, "tn", BF16, tk=2048)
    d_gate, d_pg, d_ps = _rowwise_bwd(
        "merge_bwd", f_merge, merge_rows, [], [[(d_merged, D, 0, 0)]], [(0, BF16), (1, BF16), (2, BF16)], [], nlt)
    d_og = _mm("mm_dog", d_pg, full["w_br_gdn"], "nt", BF16)
    g_w_br_gdn = _mm("mm_gwbrg", og, d_pg, "tn", BF16, tk=2048)
    d_ys = _mm("mm_dys", d_ps, full["w_br_ssm"], "nt", BF16, tn=2048)
    g_w_br_ssm = _mm("mm_gwbrs", ys, d_ps, "tn", BF16, tk=2048)
    d_o, d_zg, d_y, d_xs_post, d_zs, d_gnw, d_ssd8, d_snw = _rowwise_bwd(
        "post_bwd", f_post, post_rows, post_pars, [[(d_og, D, 0, 0)], [(d_ys, S_INNER, 0, 0)]],
        [(0, F32), (2, BF16), (3, F32), (5, F32), (6, BF16)], [0, 1, 2], nlt)
    dq0, dk0, dv0, dg0, db0 = _gdn_scan_bwd("gdn_bwd0", q, k, v, gcum, beta, ss0, ts0, d_o, 0, nctx)
    dq1, dk1, dv1, dg1, db1 = _gdn_scan_bwd("gdn_bwd1", q, k, v, gcum, beta, ss1, ts1, d_o, 1, nctx)
    dxs0, ddt0, dac0, dbm0, dcm0 = _ssd_scan_bwd("ssd_bwd0", xs, dtc, acc, bm, cm, hs0, d_y, 0, nctx)
    dxs1, ddt1, dac1, dbm1, dcm1 = _ssd_scan_bwd("ssd_bwd1", xs, dtc, acc, bm, cm, hs1, d_y, 1, nctx)
    row = lambda arr, wd: (arr, wd, 0, 0)
    d_qkv_raw, d_ab, d_gcw, d_gcb, d_alog16, d_dtb16 = _rowwise_bwd(
        "gdnprep_bwd", f_gdnprep, gp_rows, gp_pars,
        [[row(dq0, D), row(dq1, D)], [row(dk0, D), row(dk1, D)], [row(dv0, D), row(dv1, D)],
         [row(dg0, 128), row(dg1, 128)], [row(db0, 128), row(db1, 128)]],
        [(0, BF16), (1, BF16)], [0, 1, 2, 3], ntl, base=0)
    d_xbc_raw, d_dt, d_scw, d_scb, d_alog64, d_dtb64 = _rowwise_bwd(
        "ssmprep_bwd", f_ssmprep, sp_rows, sp_pars,
        [[row(dxs0, S_INNER), row(dxs1, S_INNER), (d_xs_post, S_INNER, 0, -1, True)], [row(dbm0, 512), row(dbm1, 512)],
         [row(dcm0, 512), row(dcm1, 512)], [row(ddt0, 128), row(ddt1, 128)], [row(dac0, 128), row(dac1, 128)]],
        [(0, BF16), (1, BF16)], [0, 1, 2, 3], ntl, base=0)
    ctx_zero = lambda wd: jnp.zeros((TM, wd), BF16)
    d_proj = jnp.concatenate([
        d_qkv_raw, d_xbc_raw, jnp.concatenate([ctx_zero(S_INNER), d_zs], axis=0),
        jnp.concatenate([ctx_zero(S_INNER), d_gate], axis=0), jnp.concatenate([ctx_zero(D), d_zg], axis=0), d_ab, d_dt], axis=1)
    d_a = _mm("mm_da", d_proj, w_cat, "nt", BF16, tk=3840)
    g_w_cat = _mm("mm_gwcat", a, d_proj, "tn", BF16, tn=768, tk=2816)
    d_xa, d_n1w, d_sc1, d_sh1, d_csc1, d_csh1 = _rowwise_bwd(
        "pre_bwd", f_pre_thru, [(xa, D, 0, 0)], pre_pars, [[row(d_a, D)], [(d_xres, D, 0, -1, True)]],
        [(0, F32)], [0, 1, 2, 3, 4], ntl, base=0)
    zero4 = jnp.zeros((1, 4 * D), F32)
    d_mod = jnp.concatenate([
        jnp.concatenate([d_sh1, d_sc1, d_g1, d_sh2, d_sc2, d_g2], axis=1),
        jnp.concatenate([d_csh1, d_csc1, zero4], axis=1), jnp.zeros((14, 6 * D), F32)], axis=0)
    g_ada_w = _mm("mm_gwada", a16, d_mod, "tn", BF16)
    d_a16 = _mm("mm_da16", d_mod, full["ada_w"], "nt", F32)
    d_cvec = _rowwise_bwd("silu_c_bwd", f_silu_rows, [(cvec, D, 0, 0)], [], [[row(d_a16, D)]], [(0, F32)], [], 1, tm=16)[0]

    grads_full = dict(w_in=_w_uncat(g_w_cat), ada_w=g_ada_w, w_br_gdn=g_w_br_gdn, w_br_ssm=g_w_br_ssm, w_out=g_w_out,
                      w_ffn_in=g_w_ffn_in, w_ffn_out=g_w_ffn_out)
    recv = _all_to_all("a2a_grads", [_blocks_from_full(grads_full[n], n in BIG_COL_SHARDED) for n in BIG])
    big_un = {n: _reduce_adam("adam_" + n, rcv, wts[n][0], mom1[n][0], mom2[n][0], BIG_ADAM_ROWS[n])
              for n, rcv in zip(BIG, recv)}

    small_g = dict(c_ctx=d_cvec[1], ada_b=d_mod[0] + d_mod[1], norm1_w=d_n1w, gdn_conv_b=d_gcb,
                   gdn_a_log=d_alog16[0, :2 * G_HEADS], gdn_dt_bias=d_dtb16[0, :2 * G_HEADS], gdn_norm_w=d_gnw,
                   ssm_conv_b=d_scb, ssm_a_log=d_alog64[0, :2 * S_HEADS], ssm_dt_bias=d_dtb64[0, :2 * S_HEADS],
                   ssm_d=d_ssd8[0, :S_HEADS], ssm_norm_w=d_snw, norm2_w=d_n2w, norm_f_w=d_nfw,
                   gdn_conv_w=d_gcw.reshape(3, XBC), ssm_conv_w=d_scw.reshape(3, XBC))
    small_names = SMALL + CONV
    sg_pack = _pack([small_g[n] for n in small_names], rows=48)
    recv_s = _all_gather("ag_small_grads", [sg_pack])[0]

    def placed(src, n):
        if n not in CONV:
            return src[n]
        return lax.dynamic_update_slice(jnp.zeros((3, XBC), F32), src[n][0], (0, me_i * CONV_SHARD))

    small_out = _reduce_adam("adam_small", recv_s, *[_pack([placed(src, n) for n in small_names], rows=48)
                                                     for src in (wts, mom1, mom2)], 48)
    small_shapes = [wts[n].shape if n in SMALL else (3, XBC) for n in small_names]
    small_un = [_unpack(buf, small_shapes) for buf in small_out]

    res = [{}, {}, {}, {}]
    for kind in range(4):
        for n in BIG:
            res[kind][n] = big_un[n][kind].reshape(wts[n].shape)
        for n, val in zip(small_names, small_un[kind]):
            if n in CONV:
                val = lax.dynamic_slice(val, (0, me_i * CONV_SHARD), (3, CONV_SHARD)).reshape(wts[n].shape)
            res[kind][n] = val
    loss = lax.psum(loss_acc[0, 0], ("x", "y", "c"))
    grad_x = d_xa[TM:].reshape(x.shape)
    return (loss, grad_x, *[res[0][n] for n in order], *[res[1][n] for n in order], *[res[2][n] for n in order],
            *[res[3][n] for n in order])
```

```python
import functools

import jax
import jax.numpy as jnp
from jax import lax
from jax.experimental import pallas as pl
from jax.experimental.pallas import tpu as pltpu

F32 = jnp.float32
BF16 = jnp.bfloat16
HI = lax.Precision.HIGHEST
HIGH = lax.Precision.HIGH
MESH = pl.DeviceIdType.MESH

D = 1024
CH = 64
TM = 256
EPS = 1e-6
NEG = -1e30
G_HEADS = 8
DK = 128
S_HEADS = 32
S_P = 64
S_GROUPS = 4
S_N = 128
S_INNER = 2048
XBC = 3072
D_FF = 2816
N_DEV = 8
PACK_W = 1024
VMEM_LIMIT = 56 * 1024 * 1024

ADAM_LR = 0.001
ADAM_B1 = 0.9
ADAM_B2 = 0.999
ADAM_EPS = 1e-08
ADAM_WD = 0.01
ADAM_STEP = 10

C_QKV, C_XBC, C_ZS, C_GATE, C_ZG, C_AB, C_DT, C_END = 0, 3072, 6144, 8192, 10240, 11264, 11392, 11520
O_QKV, O_ZG, O_AB, O_ZS, O_XBC, O_DT, O_GATE, O_END = 0, 3072, 4096, 4128, 6176, 9248, 9312, 11360


def _dot(a, b, prec=None):
    return jnp.dot(a, b, precision=prec, preferred_element_type=F32)


def _dot_nt(a, b, prec=None):
    return lax.dot_general(a, b, (((1,), (1,)), ((), ())), precision=prec, preferred_element_type=F32)


def _dot_tn(a, b, prec=None):
    return lax.dot_general(a, b, (((0,), (0,)), ((), ())), precision=prec, preferred_element_type=F32)


def _iota(shape, dim):
    return lax.broadcasted_iota(jnp.int32, shape, dim)


def _rms(x):
    return x * lax.rsqrt(jnp.mean(x * x, axis=-1, keepdims=True) + EPS)


def _l2n(x):
    return x * lax.rsqrt(jnp.sum(x * x, axis=-1, keepdims=True) + EPS)


def _silu(x):
    return x * jax.nn.sigmoid(x)


def _softplus(x):
    return jnp.maximum(x, 0.0) + jnp.log1p(jnp.exp(-jnp.abs(x)))


def _roll_rows(x, s):
    return pltpu.roll(x, s, 0)


def _up_raw(x, keep_up):
    return jnp.where(keep_up > 0.0, _roll_rows(x, 1), 0.0)


def _dn_raw(x, keep_dn):
    return jnp.where(keep_dn > 0.0, _roll_rows(x, x.shape[0] - 1), 0.0)


@jax.custom_vjp
def _shift_up(x, keep_up, keep_dn):
    return _up_raw(x, keep_up)


def _shift_up_fwd(x, keep_up, keep_dn):
    return _up_raw(x, keep_up), (keep_up, keep_dn)


def _shift_up_bwd(res, g):
    keep_up, keep_dn = res
    return _dn_raw(g, keep_dn), jnp.zeros_like(keep_up), jnp.zeros_like(keep_dn)


_shift_up.defvjp(_shift_up_fwd, _shift_up_bwd)


@jax.custom_vjp
def _shift_dn(x, keep_up, keep_dn):
    return _dn_raw(x, keep_dn)


def _shift_dn_fwd(x, keep_up, keep_dn):
    return _dn_raw(x, keep_dn), (keep_up, keep_dn)


def _shift_dn_bwd(res, g):
    keep_up, keep_dn = res
    return _up_raw(g, keep_up), jnp.zeros_like(keep_up), jnp.zeros_like(keep_dn)


_shift_dn.defvjp(_shift_dn_fwd, _shift_dn_bwd)


def _conv_keep(is_ctx, n):
    r = _iota((n, 1), 0)
    pos = jnp.where(is_ctx, r, r & (CH - 1))
    end = jnp.where(is_ctx, n - 1, CH - 1)
    return jnp.where(pos == 0, 0.0, 1.0).astype(F32), jnp.where(pos == end, 0.0, 1.0).astype(F32)


def _conv_silu(u, w3, b, keep_up, keep_dn):
    conv = b + _shift_up(u, keep_up, keep_dn) * w3[0] + u * w3[1] + _shift_dn(u, keep_up, keep_dn) * w3[2]
    return _silu(conv)


def _chunk_tri(n, rev):
    i = _iota((n, n), 0)
    j = _iota((n, n), 1)
    same = (i // CH) == (j // CH)
    seen = (i <= j) if rev else (i >= j)
    return jnp.where(same & seen, 1.0, 0.0).astype(F32)


def _expand_mat(rows, cols, per, base):
    r = _iota((rows, cols), 0)
    c = _iota((rows, cols), 1)
    return jnp.where(r == base + c // per, 1.0, 0.0).astype(F32)


def f_silu_rows(is_ctx, cvec):
    return (_silu(cvec).astype(BF16),)


def f_pre(is_ctx, x, n1w, sc, sh, csc, csh):
    sc_e = jnp.where(is_ctx, csc, sc)
    sh_e = jnp.where(is_ctx, csh, sh)
    a = _rms(x) * n1w * (1.0 + sc_e) + sh_e
    return (a.astype(BF16),)


def f_pre_thru(is_ctx, x, n1w, sc, sh, csc, csh):
    return f_pre(is_ctx, x, n1w, sc, sh, csc, csh)[0], x


def f_gdnprep(is_ctx, qkv_raw, ab_raw, cw, cb, alog, dtb):
    n = qkv_raw.shape[0]
    keep_up, keep_dn = _conv_keep(is_ctx, n)
    s = _conv_silu(qkv_raw, cw, cb, keep_up, keep_dn)
    qs, ks, vs = [], [], []
    for h in range(G_HEADS):
        qs.append(_l2n(s[:, h * DK:(h + 1) * DK]) * (DK ** -0.5))
        ks.append(_l2n(s[:, D + h * DK:D + (h + 1) * DK]))
    q = jnp.concatenate(qs, axis=1)
    k = jnp.concatenate(ks, axis=1)
    v = s[:, 2 * D:3 * D]
    lane = _iota(ab_raw.shape, 1)
    g = jnp.where(lane < 2 * G_HEADS, -jnp.exp(alog) * _softplus(ab_raw + dtb), 0.0)
    gcum = jnp.where(lane < G_HEADS, _dot(_chunk_tri(n, False), g, HI), _dot(_chunk_tri(n, True), g, HI))
    beta = jax.nn.sigmoid(ab_raw)
    return q, k, v, gcum, beta


def f_ssmprep(is_ctx, xbc_raw, dt_raw, cw, cb, alog, dtb):
    n = xbc_raw.shape[0]
    keep_up, keep_dn = _conv_keep(is_ctx, n)
    s = _conv_silu(xbc_raw, cw, cb, keep_up, keep_dn)
    xs = s[:, :S_INNER]
    bm = s[:, S_INNER:S_INNER + S_GROUPS * S_N]
    cm = s[:, S_INNER + S_GROUPS * S_N:]
    lane = _iota(dt_raw.shape, 1)
    dt = jnp.where(lane < 2 * S_HEADS, _softplus(dt_raw + dtb), 0.0)
    da = dt * (-jnp.exp(alog))
    acum = jnp.where(lane < S_HEADS, _dot(_chunk_tri(n, False), da, HI), _dot(_chunk_tri(n, True), da, HI))
    return xs, bm, cm, dt, acum


def f_post(is_ctx, o_f, o_b, zg, y_f, y_b, xs, zs, gnw, ssd8, snw):
    o = o_f + o_b
    ogs = []
    for h in range(G_HEADS):
        sl = slice(h * DK, (h + 1) * DK)
        ogs.append(_rms(o[:, sl]) * gnw * _silu(zg[:, sl]))
    og = jnp.concatenate(ogs, axis=1)
    row0 = jnp.where(_iota(ssd8.shape, 0) == 0, 1.0, 0.0).astype(F32)
    dexp = jnp.sum(_dot(ssd8 * row0, _expand_mat(128, S_INNER, S_P, 0), HI), axis=0, keepdims=True)
    y = (y_f + y_b + dexp * xs) * _silu(zs)
    gw = S_INNER // S_GROUPS
    ys = jnp.concatenate([_rms(y[:, i * gw:(i + 1) * gw]) * snw[:, i * gw:(i + 1) * gw] for i in range(S_GROUPS)], axis=1)
    return og.astype(BF16), ys.astype(BF16)


def f_merge(is_ctx, gate, pg, ps):
    m = jax.nn.sigmoid(gate[:, :D]) * pg + jax.nn.sigmoid(gate[:, D:]) * ps
    return (m.astype(BF16),)


def f_res1(is_ctx, x, mix, g1, n2w, sc2, sh2):
    h1 = x + g1 * mix
    f = _rms(h1) * n2w * (1.0 + sc2) + sh2
    return h1, f.astype(BF16)


def f_act(is_ctx, u):
    return ((_silu(u[:, :D_FF]) * u[:, D_FF:]).astype(BF16),)


def f_final(is_ctx, h1, ff, tgt, g2, nfw):
    h2 = h1 + g2 * ff
    y = _rms(h2) * nfw
    err = y - tgt
    return (0.5 * jnp.sum(jnp.mean(err * err, axis=-1, keepdims=True), axis=0, keepdims=True),)


def _each(fn, *lists):
    return [fn(*args) for args in zip(*lists)]


def _tri_inverse_all(mats):
    n = mats[0].shape[0]
    eye = jnp.where(_iota((n, n), 0) == _iota((n, n), 1), 1.0, 0.0).astype(F32)
    t = [eye - a for a in mats]
    p = [_dot(a, a, HIGH) for a in mats]
    for r in range(5):
        t = _each(lambda t_, p_: t_ + _dot(t_, p_, HIGH), t, p)
        if r < 4:
            p = [_dot(p_, p_, HIGH) for p_ in p]
    return t


@jax.custom_vjp
def _inverse_given(a, t):
    return t


def _inverse_given_fwd(a, t):
    return t, t


def _inverse_given_bwd(t, g):
    return -_dot_nt(_dot_tn(t, g, HIGH), t, HIGH), jnp.zeros_like(t)


_inverse_given.defvjp(_inverse_given_fwd, _inverse_given_bwd)


def gdn_chunk(ss, qs, ks, vs, gcs, grs, bcs, rev, t_known=None):
    c = qs[0].shape[0]
    ii = _iota((c, c), 0)
    jj = _iota((c, c), 1)
    incl = (ii <= jj) if rev else (ii >= jj)
    strict = (ii < jj) if rev else (ii > jj)
    last = 0 if rev else c - 1
    is_last = _iota((c, 1), 0) == last
    decay = _each(lambda gc, gr: jnp.exp(jnp.where(incl, gc - gr, NEG)), gcs, grs)
    kb = _each(lambda k, bc: k * bc, ks, bcs)
    a = _each(lambda kb_, k, dc: jnp.where(strict, _dot_nt(kb_, k) * dc, 0.0), kb, ks, decay)
    t = _tri_inverse_all(a) if t_known is None else _each(_inverse_given, a, t_known)
    eg = [jnp.exp(gc) for gc in gcs]
    rhs = _each(lambda kb_, eg_, v, bc: jnp.concatenate([kb_ * eg_, v * bc], axis=1), kb, eg, vs, bcs)
    wu = _each(lambda t_, r: _dot(t_, r, HIGH), t, rhs)
    lhs = _each(lambda wu_, q, eg_: jnp.concatenate([wu_[:, :DK], q * eg_], axis=0), wu, qs, eg)
    ws = _each(_dot, lhs, ss)
    v_new = _each(lambda wu_, ws_: wu_[:, DK:] - ws_[:c], wu, ws)
    attn = _each(lambda q, k, dc: _dot_nt(q, k) * dc, qs, ks, decay)
    o = _each(lambda ws_, at, vn: ws_[c:] + _dot(at, vn), ws, attn, v_new)
    gtot = [jnp.sum(jnp.where(is_last, gc, 0.0), axis=0, keepdims=True) for gc in gcs]
    s_new = _each(lambda s, k, gc, gt_, vn: s * jnp.exp(gt_) + _dot_tn(k * jnp.exp(gt_ - gc), vn), ss, ks, gcs, gtot, v_new)
    return s_new, o, t


def ssd_pick(dt0s, dt1s, ac0s, ac1s, ar0s, ar1s):
    lo = _iota((CH, 128), 1) < S_P
    pick = lambda u0, u1: jnp.where(lo, u0, u1)
    return _each(pick, dt0s, dt1s), _each(pick, ac0s, ac1s), _each(pick, ar0s, ar1s)


def ssd_chunk(hts, xs, dts, acs, acr, bgs, cgs, rev):
    c = xs[0].shape[0]
    npair = len(xs)
    grp = [p * len(bgs) // npair for p in range(npair)]
    lane = _iota((c, 128), 1)
    ii = _iota((c, 128), 0)
    jl = lane & (S_P - 1)
    lo = lane < S_P
    seen = (ii <= jl) if rev else (ii >= jl)
    last = 0 if rev else c - 1
    split = lambda z: jnp.concatenate([jnp.where(lo, z, 0.0), jnp.where(lo, 0.0, z)], axis=0)
    cb = _each(lambda bg, cg: _dot_nt(cg, jnp.concatenate([bg, bg], axis=0)), bgs, cgs)
    seg = _each(lambda ac, ar: jnp.exp(jnp.where(seen, ac - ar, NEG)), acs, acr)
    xdt = _each(lambda x, dt: x * dt, xs, dts)
    ydiag = [_dot(cb[grp[p]] * seg[p], split(xdt[p])) for p in range(npair)]
    yoff = [_dot(cgs[grp[p]], hts[p]) * jnp.exp(acs[p]) for p in range(npair)]
    atot = [jnp.sum(jnp.where(ii == last, ac, 0.0), axis=0, keepdims=True) for ac in acs]
    h_new = [hts[p] * jnp.exp(atot[p]) + _dot_tn(bgs[grp[p]], xdt[p] * jnp.exp(atot[p] - acs[p])) for p in range(npair)]
    return h_new, _each(lambda a_, b_: a_ + b_, ydiag, yoff)


def f_adamw(w, g, m, v):
    m = ADAM_B1 * m + (1.0 - ADAM_B1) * g
    v = ADAM_B2 * v + (1.0 - ADAM_B2) * jnp.square(g)
    m_hat = m / (1.0 - ADAM_B1 ** ADAM_STEP)
    v_hat = v / (1.0 - ADAM_B2 ** ADAM_STEP)
    delta = -ADAM_LR * (m_hat / (jnp.sqrt(v_hat) + ADAM_EPS) + ADAM_WD * w)
    return delta, m, v


def _cparams(sem):
    return pltpu.CompilerParams(dimension_semantics=sem, vmem_limit_bytes=VMEM_LIMIT)


def _pick(n, target):
    if n <= target:
        return n
    best = None
    for t in range(128, target + 1, 128):
        if n % t == 0:
            best = t
    assert best is not None, (n, target)
    return best


def _row_spec(tm, width, colblk, rowoff):
    return pl.BlockSpec((tm, width), lambda i: (i + rowoff, colblk))


def _par_spec(shape):
    nd = len(shape)
    return pl.BlockSpec(tuple(shape), lambda i: (0,) * nd)


def _rowwise(name, fn, rows, pars, outs, ntiles, base=1, tm=TM):
    nr, npar = len(rows), len(pars)

    def body(*refs):
        is_ctx = (pl.program_id(0) + base) == 0
        res = fn(is_ctx, *[r[...] for r in refs[:nr]], *[p[...] for p in refs[nr:nr + npar]])
        for o_ref, r in zip(refs[nr + npar:], res):
            o_ref[...] = r.astype(o_ref.dtype)

    return pl.pallas_call(
        body, grid=(ntiles,), name=name,
        in_specs=[_row_spec(tm, wd, cb, ro) for (_, wd, cb, ro) in rows] + [_par_spec(p.shape) for p in pars],
        out_specs=[_row_spec(tm, wd, 0, 0) for (wd, _) in outs],
        out_shape=[jax.ShapeDtypeStruct((ntiles * tm, wd), dt) for (wd, dt) in outs],
        compiler_params=_cparams(("arbitrary",)),
    )(*[r[0] for r in rows], *pars)


def _ct_spec(tm, desc):
    _, wd, cb, ro = desc[:4]
    if len(desc) > 4 and desc[4]:
        return pl.BlockSpec((tm, wd), lambda i: (jnp.maximum(i + ro, 0), cb))
    return _row_spec(tm, wd, cb, ro)


def _rowwise_bwd(name, fn, rows, pars, cts, drows, dpars, ntiles, base=1, loss_out=False, tm=TM):
    nr, npar = len(rows), len(pars)
    ct_rows = [d for ct in cts if isinstance(ct, list) for d in ct]
    nct = len(ct_rows)

    def body(*refs):
        i = pl.program_id(0)
        is_ctx = (i + base) == 0
        rows_v = [r[...] for r in refs[:nr]]
        pars_v = [p[...] for p in refs[nr:nr + npar]]
        ct_refs = list(refs[nr + npar:nr + npar + nct])
        out_refs = list(refs[nr + npar + nct:])
        outs, vjp = jax.vjp(lambda rv, pv: fn(is_ctx, *rv, *pv), rows_v, pars_v)

        def ct_value(desc):
            val = ct_refs.pop(0)[...].astype(F32)
            if len(desc) > 4 and desc[4]:
                val = jnp.where(is_ctx, 0.0, val)
            return val

        ct_vals = []
        for o, ct in zip(outs, cts):
            if ct is None:
                ct_vals.append(jnp.zeros_like(o))
            elif isinstance(ct, str):
                ct_vals.append(jnp.ones_like(o))
            else:
                acc = ct_value(ct[0])
                for desc in ct[1:]:
                    acc = acc + ct_value(desc)
                ct_vals.append(acc.astype(o.dtype))
        d_rows, d_pars = vjp(tuple(ct_vals))
        for (ri, _), o_ref in zip(drows, out_refs[:len(drows)]):
            o_ref[...] = d_rows[ri].astype(o_ref.dtype)
        acc_refs = out_refs[len(drows):]
        acc_vals = [d_pars[pi] for pi in dpars]
        if loss_out:
            acc_vals.append(jnp.broadcast_to(outs[0], (8, 128)))

        @pl.when(i == 0)
        def _():
            for o_ref, val in zip(acc_refs, acc_vals):
                o_ref[...] = val

        @pl.when(i > 0)
        def _():
            for o_ref, val in zip(acc_refs, acc_vals):
                o_ref[...] += val

    acc_shapes = [pars[pi].shape for pi in dpars] + ([(8, 128)] if loss_out else [])
    return pl.pallas_call(
        body, grid=(ntiles,), name=name,
        in_specs=[_row_spec(tm, wd, cb, ro) for (_, wd, cb, ro) in rows] + [_par_spec(p.shape) for p in pars]
        + [_ct_spec(tm, d) for d in ct_rows],
        out_specs=[_row_spec(tm, rows[ri][1], 0, 0) for (ri, _) in drows] + [_par_spec(s) for s in acc_shapes],
        out_shape=[jax.ShapeDtypeStruct((ntiles * tm, rows[ri][1]), dt) for (ri, dt) in drows]
        + [jax.ShapeDtypeStruct(tuple(s), F32) for s in acc_shapes],
        compiler_params=_cparams(("arbitrary",)),
    )(*[r[0] for r in rows], *pars, *[r[0] for r in ct_rows])


def _mm(name, a, b, mode, out_dtype, tm=1024, tn=1024, tk=1024, ride=None):
    if mode == "nn":
        (m, kd), (_, n) = a.shape, b.shape
    elif mode == "nt":
        (m, kd), (n, _) = a.shape, b.shape
    else:
        (kd, m), (_, n) = a.shape, b.shape
    tm, tn, tk = _pick(m, tm), _pick(n, tn), _pick(kd, tk)
    nk = kd // tk
    a_spec = {"nn": pl.BlockSpec((tm, tk), lambda i, j, k: (i, k)), "nt": pl.BlockSpec((tm, tk), lambda i, j, k: (i, k)),
              "tn": pl.BlockSpec((tk, tm), lambda i, j, k: (k, i))}[mode]
    b_spec = {"nn": pl.BlockSpec((tk, tn), lambda i, j, k: (k, j)), "nt": pl.BlockSpec((tn, tk), lambda i, j, k: (j, k)),
              "tn": pl.BlockSpec((tk, tn), lambda i, j, k: (k, j))}[mode]
    dot = {"nn": _dot, "nt": _dot_nt, "tn": _dot_tn}[mode]

    if nk == 1:
        def body(a_ref, b_ref, o_ref):
            o_ref[...] = dot(a_ref[...].astype(BF16), b_ref[...].astype(BF16)).astype(o_ref.dtype)
    else:
        def body(a_ref, b_ref, o_ref, acc_ref):
            k = pl.program_id(2)
            part = dot(a_ref[...].astype(BF16), b_ref[...].astype(BF16))

            @pl.when(k == 0)
            def _():
                acc_ref[...] = part

            @pl.when((k > 0) & (k < nk - 1))
            def _():
                acc_ref[...] += part

            @pl.when(k == nk - 1)
            def _():
                o_ref[...] = (acc_ref[...] + part).astype(o_ref.dtype)

    grid = (m // tm, n // tn, nk)
    at = lambda pos: functools.reduce(jnp.logical_and, [pl.program_id(ax) == pos(g) for ax, g in enumerate(grid)])
    r_in, r_out, r_shapes, r_scr, r_ops = _ride_args(ride)
    res = pl.pallas_call(
        _ride(body, ride, 2, 1, lambda: at(lambda g: 0), lambda: at(lambda g: g - 1)), grid=grid, name=name,
        in_specs=[a_spec, b_spec] + r_in,
        out_specs=[pl.BlockSpec((tm, tn), lambda i, j, k: (i, j))] + r_out,
        out_shape=[jax.ShapeDtypeStruct((m, n), out_dtype)] + r_shapes,
        scratch_shapes=([] if nk == 1 else [pltpu.VMEM((tm, tn), F32)]) + r_scr,
        compiler_params=_cparams(("arbitrary", "arbitrary", "arbitrary")),
    )(a, b, *r_ops)
    return res[0] if ride is None else (res[0], res[1:])


def _chunk_index(i, nch, nctx, rev):
    if not rev:
        return i
    return jnp.where(i < nctx, nctx - 1 - i, nch + nctx - 1 - i)


def _gdn_cols(d):
    return [d * G_HEADS + h for h in range(G_HEADS)], [2 * G_HEADS + d * G_HEADS + h for h in range(G_HEADS)]


def _gdn_operands(q_ref, k_ref, v_ref, g_ref, b_ref, d):
    cols_g, cols_b = _gdn_cols(d)
    sls = [slice(h * DK, (h + 1) * DK) for h in range(G_HEADS)]
    gt, bt = g_ref[...], b_ref[...]
    gtt = gt.T
    qs = [q_ref[:, sl] for sl in sls]
    ks = [k_ref[:, sl] for sl in sls]
    vs = [v_ref[:, sl] for sl in sls]
    gcs = [gt[:, cg:cg + 1] for cg in cols_g]
    grs = [gtt[cg:cg + 1, :] for cg in cols_g]
    bcs = [bt[:, cb:cb + 1] for cb in cols_b]
    return sls, qs, ks, vs, gcs, grs, bcs


def _gdn_scan_fwd(name, q, k, v, gcum, beta, d, nctx, ride=None):
    t = q.shape[0]
    nch = t // CH
    rev = d == 1
    cix = lambda i: _chunk_index(i, nch, nctx, rev)
    full = pl.BlockSpec((CH, D), lambda i: (cix(i), 0))
    wide = pl.BlockSpec((CH, 128), lambda i: (cix(i), 0))

    def body(q_ref, k_ref, v_ref, g_ref, b_ref, o_ref, ss_ref, ts_ref, s_scr):
        @pl.when(pl.program_id(0) == 0)
        def _():
            s_scr[...] = jnp.zeros(s_scr.shape, F32)

        sls, qs, ks, vs, gcs, grs, bcs = _gdn_operands(q_ref, k_ref, v_ref, g_ref, b_ref, d)
        ss = [s_scr[h] for h in range(G_HEADS)]
        s_new, o, t = gdn_chunk(ss, qs, ks, vs, gcs, grs, bcs, rev)
        for h in range(G_HEADS):
            ss_ref[0, h] = ss[h]
            ts_ref[0, h] = t[h]
            s_scr[h] = s_new[h]
            o_ref[:, sls[h]] = o[h]

    r_in, r_out, r_shapes, r_scr, r_ops = _ride_args(ride)
    res = pl.pallas_call(
        _ride(body, ride, 5, 3, lambda: pl.program_id(0) == 0, lambda: pl.program_id(0) == nch - 1), grid=(nch,), name=name,
        in_specs=[full, full, full, wide, wide] + r_in,
        out_specs=[full, pl.BlockSpec((1, G_HEADS, DK, DK), lambda i: (i, 0, 0, 0)),
                   pl.BlockSpec((1, G_HEADS, CH, CH), lambda i: (i, 0, 0, 0))] + r_out,
        out_shape=[jax.ShapeDtypeStruct((t, D), F32), jax.ShapeDtypeStruct((nch, G_HEADS, DK, DK), F32),
                   jax.ShapeDtypeStruct((nch, G_HEADS, CH, CH), F32)] + r_shapes,
        scratch_shapes=[pltpu.VMEM((G_HEADS, DK, DK), F32)] + r_scr,
        compiler_params=_cparams(("arbitrary",)),
    )(q, k, v, gcum, beta, *r_ops)
    return res if ride is None else (*res[:3], res[3:])


def _gdn_scan_bwd(name, q, k, v, gcum, beta, ssave, tsave, do, d, nctx, ride=None):
    t = q.shape[0]
    nch = t // CH
    rev = d == 1
    cix = lambda ib: _chunk_index(nch - 1 - ib, nch, nctx, rev)
    full = pl.BlockSpec((CH, D), lambda ib: (cix(ib), 0))
    wide = pl.BlockSpec((CH, 128), lambda ib: (cix(ib), 0))
    do_spec = pl.BlockSpec((CH, D), lambda ib: (jnp.maximum(cix(ib), nctx) - nctx, 0))

    def body(q_ref, k_ref, v_ref, g_ref, b_ref, ss_ref, ts_ref, do_ref, dq_ref, dk_ref, dv_ref, dg_ref, db_ref, ds_scr):
        ib = pl.program_id(0)

        @pl.when(ib == 0)
        def _():
            ds_scr[...] = jnp.zeros(ds_scr.shape, F32)

        sls, qs, ks, vs, gcs, grs, bcs = _gdn_operands(q_ref, k_ref, v_ref, g_ref, b_ref, d)
        t_known = [ts_ref[0, h] for h in range(G_HEADS)]
        cols_g, cols_b = _gdn_cols(d)
        is_lat = cix(ib) >= nctx
        ss = [ss_ref[0, h] for h in range(G_HEADS)]
        do_v = [jnp.where(is_lat, do_ref[:, sl], 0.0) for sl in sls]
        ds_in = [ds_scr[h] for h in range(G_HEADS)]
        _, vjp = jax.vjp(lambda *a: gdn_chunk(*a, rev, t_known)[:2], ss, qs, ks, vs, gcs, grs, bcs)
        ds, dq, dk, dv, dgc, dgr, dbc = vjp((ds_in, do_v))
        lane = _iota((CH, 128), 1)
        sub = _iota((128, CH), 0)
        dg = jnp.zeros((CH, 128), F32)
        dgt = jnp.zeros((128, CH), F32)
        db = jnp.zeros((CH, 128), F32)
        for h in range(G_HEADS):
            ds_scr[h] = ds[h]
            dq_ref[:, sls[h]] = dq[h]
            dk_ref[:, sls[h]] = dk[h]
            dv_ref[:, sls[h]] = dv[h]
            dg = dg + jnp.where(lane == cols_g[h], dgc[h], 0.0)
            dgt = dgt + jnp.where(sub == cols_g[h], dgr[h], 0.0)
            db = db + jnp.where(lane == cols_b[h], dbc[h], 0.0)
        dg_ref[...] = dg + dgt.T
        db_ref[...] = db

    r_in, r_out, r_shapes, r_scr, r_ops = _ride_args(ride)
    res = pl.pallas_call(
        _ride(body, ride, 8, 5, lambda: pl.program_id(0) == 0, lambda: pl.program_id(0) == nch - 1), grid=(nch,), name=name,
        in_specs=[full, full, full, wide, wide,
                  pl.BlockSpec((1, G_HEADS, DK, DK), lambda ib: (nch - 1 - ib, 0, 0, 0)),
                  pl.BlockSpec((1, G_HEADS, CH, CH), lambda ib: (nch - 1 - ib, 0, 0, 0)), do_spec] + r_in,
        out_specs=[full, full, full, wide, wide] + r_out,
        out_shape=[jax.ShapeDtypeStruct((t, D), F32)] * 3 + [jax.ShapeDtypeStruct((t, 128), F32)] * 2 + r_shapes,
        scratch_shapes=[pltpu.VMEM((G_HEADS, DK, DK), F32)] + r_scr,
        compiler_params=_cparams(("arbitrary",)),
    )(q, k, v, gcum, beta, ssave, tsave, do, *r_ops)
    return res if ride is None else (*res[:5], res[5:])


N_PAIRS = S_HEADS // 2


def _ssd_operands(x_ref, dt_ref, ac_ref, b_ref, c_ref, d):
    sls = [slice(p * 128, (p + 1) * 128) for p in range(N_PAIRS)]
    gsl = [slice(g * S_N, (g + 1) * S_N) for g in range(S_GROUPS)]
    cols = [d * S_HEADS + h for h in range(S_HEADS)]
    dtc, acc = dt_ref[...], ac_ref[...]
    act = jnp.concatenate([acc, acc], axis=0).T
    col = lambda z, cc: z[:, cc:cc + 1]
    dts, acs, acr = ssd_pick(
        [col(dtc, cols[2 * p]) for p in range(N_PAIRS)], [col(dtc, cols[2 * p + 1]) for p in range(N_PAIRS)],
        [col(acc, cols[2 * p]) for p in range(N_PAIRS)], [col(acc, cols[2 * p + 1]) for p in range(N_PAIRS)],
        [act[cols[2 * p]:cols[2 * p] + 1, :] for p in range(N_PAIRS)],
        [act[cols[2 * p + 1]:cols[2 * p + 1] + 1, :] for p in range(N_PAIRS)])
    ops = ([x_ref[:, sl] for sl in sls], dts, acs, acr, [b_ref[:, gs] for gs in gsl], [c_ref[:, gs] for gs in gsl])
    return sls, gsl, cols, ops


def _ssd_scan_fwd(name, xs, dtc, acc, bm, cm, d, nctx):
    t = xs.shape[0]
    nch = t // CH
    rev = d == 1
    cix = lambda i: _chunk_index(i, nch, nctx, rev)
    inner = pl.BlockSpec((CH, S_INNER), lambda i: (cix(i), 0))
    wide = pl.BlockSpec((CH, 128), lambda i: (cix(i), 0))
    grp = pl.BlockSpec((CH, S_GROUPS * S_N), lambda i: (cix(i), 0))

    def body(x_ref, dt_ref, ac_ref, b_ref, c_ref, y_ref, hs_ref, h_scr):
        @pl.when(pl.program_id(0) == 0)
        def _():
            h_scr[...] = jnp.zeros(h_scr.shape, F32)

        sls, _, _, ops = _ssd_operands(x_ref, dt_ref, ac_ref, b_ref, c_ref, d)
        hts = [h_scr[p] for p in range(N_PAIRS)]
        h_new, y = ssd_chunk(hts, *ops, rev)
        for p in range(N_PAIRS):
            hs_ref[0, p] = hts[p]
            h_scr[p] = h_new[p]
            y_ref[:, sls[p]] = y[p]

    return pl.pallas_call(
        body, grid=(nch,), name=name,
        in_specs=[inner, wide, wide, grp, grp],
        out_specs=[inner, pl.BlockSpec((1, N_PAIRS, S_N, 128), lambda i: (i, 0, 0, 0))],
        out_shape=[jax.ShapeDtypeStruct((t, S_INNER), F32), jax.ShapeDtypeStruct((nch, N_PAIRS, S_N, 128), F32)],
        scratch_shapes=[pltpu.VMEM((N_PAIRS, S_N, 128), F32)],
        compiler_params=_cparams(("arbitrary",)),
    )(xs, dtc, acc, bm, cm)


def _ssd_scan_bwd(name, xs, dtc, acc, bm, cm, hsave, dy, d, nctx):
    t = xs.shape[0]
    nch = t // CH
    rev = d == 1
    cix = lambda ib: _chunk_index(nch - 1 - ib, nch, nctx, rev)
    inner = pl.BlockSpec((CH, S_INNER), lambda ib: (cix(ib), 0))
    wide = pl.BlockSpec((CH, 128), lambda ib: (cix(ib), 0))
    grp = pl.BlockSpec((CH, S_GROUPS * S_N), lambda ib: (cix(ib), 0))
    dy_spec = pl.BlockSpec((CH, S_INNER), lambda ib: (jnp.maximum(cix(ib), nctx) - nctx, 0))

    def body(x_ref, dt_ref, ac_ref, b_ref, c_ref, hs_ref, dy_ref, dx_ref, ddt_ref, dac_ref, db_ref, dc_ref, dh_scr):
        ib = pl.program_id(0)

        @pl.when(ib == 0)
        def _():
            dh_scr[...] = jnp.zeros(dh_scr.shape, F32)

        sls, gsl, cols, ops = _ssd_operands(x_ref, dt_ref, ac_ref, b_ref, c_ref, d)
        is_lat = cix(ib) >= nctx
        hts = [hs_ref[0, p] for p in range(N_PAIRS)]
        dy_v = [jnp.where(is_lat, dy_ref[:, sl], 0.0) for sl in sls]
        dh_in = [dh_scr[p] for p in range(N_PAIRS)]
        _, vjp = jax.vjp(lambda *a: ssd_chunk(*a, rev), hts, *ops)
        dh, dx, ddts, dacs, dacr, db, dc = vjp((dh_in, dy_v))
        for p in range(N_PAIRS):
            dh_scr[p] = dh[p]
            dx_ref[:, sls[p]] = dx[p]
        r = _iota((S_INNER, 128), 0)
        e_t = jnp.where(_iota((S_INNER, 128), 1) == d * S_HEADS + r // S_P, 1.0, 0.0).astype(F32)
        ddt_ref[...] = _dot(jnp.concatenate(ddts, axis=1), e_t, HIGH)
        dac_cols = _dot(jnp.concatenate(dacs, axis=1), e_t, HIGH)
        sub = _iota((128, 128), 0)
        lane = _iota((128, 128), 1)
        m = jnp.zeros((128, 128), F32)
        for p in range(N_PAIRS):
            m = m + jnp.where(sub == p, jnp.sum(dacr[p], axis=0, keepdims=True), 0.0)
        mt = m.T
        s0 = jnp.where(lane == d * S_HEADS + 2 * sub, 1.0, 0.0).astype(F32)
        s1 = jnp.where(lane == d * S_HEADS + 2 * sub + 1, 1.0, 0.0).astype(F32)
        dac_ref[...] = dac_cols + _dot(mt[:CH], s0, HIGH) + _dot(mt[CH:], s1, HIGH)
        for g in range(S_GROUPS):
            db_ref[:, gsl[g]] = db[g]
            dc_ref[:, gsl[g]] = dc[g]

    return pl.pallas_call(
        body, grid=(nch,), name=name,
        in_specs=[inner, wide, wide, grp, grp,
                  pl.BlockSpec((1, N_PAIRS, S_N, 128), lambda ib: (nch - 1 - ib, 0, 0, 0)), dy_spec],
        out_specs=[inner, wide, wide, grp, grp],
        out_shape=[jax.ShapeDtypeStruct((t, S_INNER), F32)] + [jax.ShapeDtypeStruct((t, 128), F32)] * 2
        + [jax.ShapeDtypeStruct((t, S_GROUPS * S_N), F32)] * 2,
        scratch_shapes=[pltpu.VMEM((N_PAIRS, S_N, 128), F32)],
        compiler_params=_cparams(("arbitrary",)),
    )(xs, dtc, acc, bm, cm, hsave, dy)


def _mesh_pos():
    return lax.axis_index("x"), lax.axis_index("y"), lax.axis_index("c")


def _hbm_specs(n):
    return [pl.BlockSpec(memory_space=pl.ANY)] * n


def _sem_shapes(nw):
    return [pltpu.SemaphoreType.DMA((nw, 7)), pltpu.SemaphoreType.DMA((nw, 7)), pltpu.SemaphoreType.DMA((nw,))]


class _AllGather:
    def __init__(self, shards):
        self.arrays = list(shards)
        self.out_shapes = [jax.ShapeDtypeStruct((N_DEV,) + xs.shape, xs.dtype) for xs in shards]

    def _parts(self, x_refs, out_refs, sems):
        send_sems, recv_sems, local_sems = sems
        nw = len(x_refs)
        x, y, c = _mesh_pos()
        me, sibling = (x, y, c), (x, y, 1 - c)
        chips = [(1 - x, y), (x, 1 - y), (1 - x, 1 - y)]

        def slot(w, px, py, pc):
            return out_refs[w].at[4 * px + 2 * py + pc]

        def copy(w, k, block, to, src=None):
            return pltpu.make_async_remote_copy(
                src_ref=slot(w, *block) if src is None else src, dst_ref=slot(w, *block),
                send_sem=send_sems.at[w, k], recv_sem=recv_sems.at[w, k], device_id=to, device_id_type=MESH)

        mine = [pltpu.make_async_copy(x_refs[w], slot(w, *me), local_sems.at[w]) for w in range(nw)]
        first = []
        for w in range(nw):
            first.append(copy(w, 0, me, sibling, src=x_refs[w]))
            first += [copy(w, 1 + j, me, (*chip, c), src=x_refs[w]) for j, chip in enumerate(chips)]
        return nw, me, sibling, chips, c, copy, mine, first

    def start(self, x_refs, out_refs, sems):
        _, _, _, _, _, _, mine, first = self._parts(x_refs, out_refs, sems)
        for cp in mine + first:
            cp.start()

    def finish(self, x_refs, out_refs, sems):
        nw, me, sibling, chips, c, copy, mine, first = self._parts(x_refs, out_refs, sems)
        passed = []
        for j, chip in enumerate(chips):
            for w in range(nw):
                copy(w, 1 + j, (*chip, c), me).wait_recv()
                fwd = copy(w, 4 + j, (*chip, c), sibling)
                fwd.start()
                passed.append(fwd)
        for w in range(nw):
            copy(w, 0, sibling, me).wait_recv()
            for j, chip in enumerate(chips):
                copy(w, 4 + j, (*chip, 1 - c), me).wait_recv()
        for cp in first + passed:
            cp.wait_send()
        for cp in mine:
            cp.wait()


class _AllToAll:
    def __init__(self, blocks):
        self.arrays = list(blocks)
        self.out_shapes = [jax.ShapeDtypeStruct(g.shape, g.dtype) for g in blocks]

    def _parts(self, g_refs, out_refs, sems):
        send_sems, recv_sems, local_sems = sems
        nw = len(g_refs)
        x, y, c = _mesh_pos()
        me_i = 4 * x + 2 * y + c
        mine = [pltpu.make_async_copy(g_refs[w].at[me_i], out_refs[w].at[me_i], local_sems.at[w]) for w in range(nw)]
        cps = []
        for k in range(1, N_DEV):
            px = 1 - x if (k >> 2) & 1 else x
            py = 1 - y if (k >> 1) & 1 else y
            pc = 1 - c if k & 1 else c
            for w in range(nw):
                cps.append(pltpu.make_async_remote_copy(
                    src_ref=g_refs[w].at[4 * px + 2 * py + pc], dst_ref=out_refs[w].at[me_i],
                    send_sem=send_sems.at[w, k - 1], recv_sem=recv_sems.at[w, k - 1],
                    device_id=(px, py, pc), device_id_type=MESH))
        return mine, cps

    def start(self, g_refs, out_refs, sems):
        mine, cps = self._parts(g_refs, out_refs, sems)
        for cp in mine + cps:
            cp.start()

    def finish(self, g_refs, out_refs, sems):
        mine, cps = self._parts(g_refs, out_refs, sems)
        for cp in cps + mine:
            cp.wait()


def _exchange(name, ex):
    nw = len(ex.arrays)

    def body(*refs):
        ins, outs, sems = refs[:nw], refs[nw:2 * nw], refs[2 * nw:]
        ex.start(ins, outs, sems)
        ex.finish(ins, outs, sems)

    return pl.pallas_call(body, name=name, out_shape=ex.out_shapes, in_specs=_hbm_specs(nw), out_specs=_hbm_specs(nw),
                          scratch_shapes=_sem_shapes(nw))(*ex.arrays)


def _ride(body, ex, n_in, n_out, is_first, is_last):
    if ex is None:
        return body
    nw = len(ex.arrays)

    def riding(*refs):
        ins, ex_in = refs[:n_in], refs[n_in:n_in + nw]
        outs = refs[n_in + nw:n_in + nw + n_out]
        ex_out = refs[n_in + nw + n_out:n_in + 2 * nw + n_out]
        rest = refs[n_in + 2 * nw + n_out:]
        scratch, sems = rest[:len(rest) - 3], rest[len(rest) - 3:]

        @pl.when(is_first())
        def _():
            ex.start(ex_in, ex_out, sems)

        body(*ins, *outs, *scratch)

        @pl.when(is_last())
        def _():
            ex.finish(ex_in, ex_out, sems)

    return riding


def _ride_args(ex):
    if ex is None:
        return [], [], [], [], []
    nw = len(ex.arrays)
    return _hbm_specs(nw), _hbm_specs(nw), list(ex.out_shapes), _sem_shapes(nw), list(ex.arrays)


def _reduce_adam(name, recv, w, m, v, tm):
    rows, width = w.shape
    nslot = recv.shape[0]

    def body(recv_ref, w_ref, m_ref, v_ref, g_ref, d_ref, m2_ref, v2_ref):
        g = recv_ref[0].astype(F32)
        for s in range(1, nslot):
            g = g + recv_ref[s].astype(F32)
        delta, m2, v2 = f_adamw(w_ref[...], g, m_ref[...], v_ref[...])
        g_ref[...] = g
        d_ref[...] = delta
        m2_ref[...] = m2
        v2_ref[...] = v2

    row = pl.BlockSpec((tm, width), lambda i: (i, 0))
    return pl.pallas_call(
        body, grid=(rows // tm,), name=name,
        in_specs=[pl.BlockSpec((nslot, tm, width), lambda i: (0, i, 0)), row, row, row],
        out_specs=[row] * 4,
        out_shape=[jax.ShapeDtypeStruct((rows, width), F32)] * 4,
        compiler_params=_cparams(("arbitrary",)),
    )(recv, w, m, v)


BIG = ("w_in", "ada_w", "w_br_gdn", "w_br_ssm", "w_out", "w_ffn_in", "w_ffn_out")
BIG_FIRST = ("ada_w", "w_in")
BIG_LATE = ("w_br_gdn", "w_br_ssm", "w_out", "w_ffn_in", "w_ffn_out")
BIG_COL_SHARDED = ("w_in", "ada_w", "w_ffn_in")
BIG_ADAM_ROWS = dict(w_in=128, ada_w=256, w_br_gdn=128, w_br_ssm=256, w_out=128, w_ffn_in=256, w_ffn_out=352)
CONV = ("gdn_conv_w", "ssm_conv_w")
SMALL = ("c_ctx", "ada_b", "norm1_w", "gdn_conv_b", "gdn_a_log", "gdn_dt_bias", "gdn_norm_w", "ssm_conv_b",
         "ssm_a_log", "ssm_dt_bias", "ssm_d", "ssm_norm_w", "norm2_w", "norm_f_w")
CONV_SHARD = XBC // N_DEV


def _to_rows(a):
    flat = a.reshape(-1)
    pad = (-flat.shape[0]) % PACK_W
    if pad:
        flat = jnp.pad(flat, (0, pad))
    return flat.reshape(-1, PACK_W)


def _pack(arrays, rows=None):
    buf = jnp.concatenate([_to_rows(a) for a in arrays], axis=0)
    if rows is not None and rows > buf.shape[0]:
        buf = jnp.pad(buf, ((0, rows - buf.shape[0]), (0, 0)))
    return buf


def _unpack(buf, shapes):
    out, r0 = [], 0
    for shp in shapes:
        n = 1
        for s in shp:
            n *= s
        nr = -(-n // PACK_W)
        out.append(buf[r0:r0 + nr].reshape(-1)[:n].reshape(shp))
        r0 += nr
    return out


def _full_from_blocks(blocks, col_sharded):
    _, r, c = blocks.shape
    if col_sharded:
        return jnp.transpose(blocks, (1, 0, 2)).reshape(r, N_DEV * c)
    return blocks.reshape(N_DEV * r, c)


def _blocks_from_full(full, col_sharded):
    if col_sharded:
        r, c = full.shape[0], full.shape[1] // N_DEV
        return jnp.transpose(full.reshape(r, N_DEV, c), (1, 0, 2))
    return full.reshape(N_DEV, full.shape[0] // N_DEV, full.shape[1])


def _pad_cols(a, n):
    return jnp.pad(a, ((0, 0), (0, n - a.shape[1])))


def _w_cat(w_in):
    return jnp.concatenate([
        w_in[:, O_QKV:O_ZG], w_in[:, O_XBC:O_DT], w_in[:, O_ZS:O_XBC], w_in[:, O_GATE:O_END], w_in[:, O_ZG:O_AB],
        _pad_cols(w_in[:, O_AB:O_ZS], 128), _pad_cols(w_in[:, O_DT:O_GATE], 128)], axis=1)


def _w_uncat(wc):
    return jnp.concatenate([
        wc[:, C_QKV:C_XBC], wc[:, C_ZG:C_AB], wc[:, C_AB:C_AB + (O_ZS - O_AB)], wc[:, C_ZS:C_GATE], wc[:, C_XBC:C_ZS],
        wc[:, C_DT:C_DT + (O_GATE - O_DT)], wc[:, C_GATE:C_ZG]], axis=1)


def _pad_row(vec, n=128):
    vec = vec.reshape(1, -1)
    return _pad_cols(vec, n)


def kernel(x, c, ctx, c_ctx, ada_w, ada_b, norm1_w, w_in, gdn_conv_w, gdn_conv_b, gdn_a_log, gdn_dt_bias, gdn_norm_w, ssm_conv_w, ssm_conv_b, ssm_a_log, ssm_dt_bias, ssm_d, ssm_norm_w, w_br_gdn, w_br_ssm, w_out, norm2_w, w_ffn_in, w_ffn_out, norm_f_w, loss_target, m_c_ctx, m_ada_w, m_ada_b, m_norm1_w, m_w_in, m_gdn_conv_w, m_gdn_conv_b, m_gdn_a_log, m_gdn_dt_bias, m_gdn_norm_w, m_ssm_conv_w, m_ssm_conv_b, m_ssm_a_log, m_ssm_dt_bias, m_ssm_d, m_ssm_norm_w, m_w_br_gdn, m_w_br_ssm, m_w_out, m_norm2_w, m_w_ffn_in, m_w_ffn_out, m_norm_f_w, v_c_ctx, v_ada_w, v_ada_b, v_norm1_w, v_w_in, v_gdn_conv_w, v_gdn_conv_b, v_gdn_a_log, v_gdn_dt_bias, v_gdn_norm_w, v_ssm_conv_w, v_ssm_conv_b, v_ssm_a_log, v_ssm_dt_bias, v_ssm_d, v_ssm_norm_w, v_w_br_gdn, v_w_br_ssm, v_w_out, v_norm2_w, v_w_ffn_in, v_w_ffn_out, v_norm_f_w):
    wts = dict(c_ctx=c_ctx, ada_w=ada_w, ada_b=ada_b, norm1_w=norm1_w, w_in=w_in, gdn_conv_w=gdn_conv_w, gdn_conv_b=gdn_conv_b, gdn_a_log=gdn_a_log, gdn_dt_bias=gdn_dt_bias, gdn_norm_w=gdn_norm_w, ssm_conv_w=ssm_conv_w, ssm_conv_b=ssm_conv_b, ssm_a_log=ssm_a_log, ssm_dt_bias=ssm_dt_bias, ssm_d=ssm_d, ssm_norm_w=ssm_norm_w, w_br_gdn=w_br_gdn, w_br_ssm=w_br_ssm, w_out=w_out, norm2_w=norm2_w, w_ffn_in=w_ffn_in, w_ffn_out=w_ffn_out, norm_f_w=norm_f_w)
    mom1 = dict(c_ctx=m_c_ctx, ada_w=m_ada_w, ada_b=m_ada_b, norm1_w=m_norm1_w, w_in=m_w_in, gdn_conv_w=m_gdn_conv_w, gdn_conv_b=m_gdn_conv_b, gdn_a_log=m_gdn_a_log, gdn_dt_bias=m_gdn_dt_bias, gdn_norm_w=m_gdn_norm_w, ssm_conv_w=m_ssm_conv_w, ssm_conv_b=m_ssm_conv_b, ssm_a_log=m_ssm_a_log, ssm_dt_bias=m_ssm_dt_bias, ssm_d=m_ssm_d, ssm_norm_w=m_ssm_norm_w, w_br_gdn=m_w_br_gdn, w_br_ssm=m_w_br_ssm, w_out=m_w_out, norm2_w=m_norm2_w, w_ffn_in=m_w_ffn_in, w_ffn_out=m_w_ffn_out, norm_f_w=m_norm_f_w)
    mom2 = dict(c_ctx=v_c_ctx, ada_w=v_ada_w, ada_b=v_ada_b, norm1_w=v_norm1_w, w_in=v_w_in, gdn_conv_w=v_gdn_conv_w, gdn_conv_b=v_gdn_conv_b, gdn_a_log=v_gdn_a_log, gdn_dt_bias=v_gdn_dt_bias, gdn_norm_w=v_gdn_norm_w, ssm_conv_w=v_ssm_conv_w, ssm_conv_b=v_ssm_conv_b, ssm_a_log=v_ssm_a_log, ssm_dt_bias=v_ssm_dt_bias, ssm_d=v_ssm_d, ssm_norm_w=v_ssm_norm_w, w_br_gdn=v_w_br_gdn, w_br_ssm=v_w_br_ssm, w_out=v_w_out, norm2_w=v_norm2_w, w_ffn_in=v_w_ffn_in, w_ffn_out=v_w_ffn_out, norm_f_w=v_norm_f_w)
    order = list(wts)

    seq = x.shape[1]
    t = TM + seq
    ntl, nlt, nctx = t // TM, seq // TM, TM // CH

    me_i = 4 * lax.axis_index("x") + 2 * lax.axis_index("y") + lax.axis_index("c")
    gathered = _exchange("ag_weights", _AllGather([wts[n][0].astype(BF16) for n in BIG_FIRST]))
    full = {n: _full_from_blocks(blk, n in BIG_COL_SHARDED) for n, blk in zip(BIG_FIRST, gathered)}
    late_gather = _AllGather([wts[n][0].astype(BF16) for n in BIG_LATE])
    conv_sh = _pack([wts[n] for n in CONV], rows=8)
    conv_g = _exchange("ag_conv", _AllGather([conv_sh]))[0].reshape(N_DEV, -1)
    ncv = 3 * CONV_SHARD
    for i, n in enumerate(CONV):
        off = -(-ncv // PACK_W) * PACK_W * i
        full[n] = jnp.transpose(conv_g[:, off:off + ncv].reshape(N_DEV, 3, CONV_SHARD), (1, 0, 2)).reshape(3, XBC)
    w_cat = _w_cat(full["w_in"])
    gcw = full["gdn_conv_w"].reshape(3, 1, XBC)
    scw = full["ssm_conv_w"].reshape(3, 1, XBC)

    n1w, n2w, nfw = norm1_w.reshape(1, D), norm2_w.reshape(1, D), norm_f_w.reshape(1, D)
    gcb, scb = gdn_conv_b.reshape(1, XBC), ssm_conv_b.reshape(1, XBC)
    alog16, dtb16 = _pad_row(gdn_a_log), _pad_row(gdn_dt_bias)
    alog64, dtb64 = _pad_row(ssm_a_log), _pad_row(ssm_dt_bias)
    gnw = gdn_norm_w.reshape(1, DK)
    ssd8 = jnp.tile(_pad_row(ssm_d), (8, 1))
    snw = ssm_norm_w.reshape(1, S_INNER)
    x2 = x[0]
    tgt = loss_target[0]
    xa = jnp.concatenate([ctx[0], x2], axis=0)
    cvec = jnp.concatenate([c, c_ctx.reshape(1, D), jnp.zeros((14, D), F32)], axis=0)

    a16 = _rowwise("silu_c", f_silu_rows, [(cvec, D, 0, 0)], [], [(D, BF16)], 1, tm=16)[0]
    mod = _mm("mm_mod", a16, full["ada_w"], "nn", F32) + ada_b
    sh1, sc1, g1, sh2, sc2, g2 = [mod[0:1, i * D:(i + 1) * D] for i in range(6)]
    csh1, csc1 = mod[1:2, 0:D], mod[1:2, D:2 * D]

    pre_pars = [n1w, sc1, sh1, csc1, csh1]
    a = _rowwise("pre", f_pre, [(xa, D, 0, 0)], pre_pars, [(D, BF16)], ntl, base=0)[0]
    proj = _mm("mm_proj", a, w_cat, "nn", F32, tm=1408, tn=1280)
    gp_rows = [(proj, XBC, C_QKV // XBC, 0), (proj, 128, C_AB // 128, 0)]
    gp_pars = [gcw, gcb, alog16, dtb16]
    q, k, v, gcum, beta = _rowwise("gdnprep", f_gdnprep, gp_rows, gp_pars, [(D, F32)] * 3 + [(128, F32)] * 2, ntl, base=0)
    sp_rows = [(proj, XBC, C_XBC // XBC, 0), (proj, 128, C_DT // 128, 0)]
    sp_pars = [scw, scb, alog64, dtb64]
    xs, bm, cm, dtc, acc = _rowwise(
        "ssmprep", f_ssmprep, sp_rows, sp_pars, [(S_INNER, F32), (512, F32), (512, F32), (128, F32), (128, F32)], ntl, base=0)
    o0, ss0, ts0, gathered = _gdn_scan_fwd("gdn_fwd0", q, k, v, gcum, beta, 0, nctx, ride=late_gather)
    full.update({n: _full_from_blocks(blk, n in BIG_COL_SHARDED) for n, blk in zip(BIG_LATE, gathered)})
    o1, ss1, ts1 = _gdn_scan_fwd("gdn_fwd1", q, k, v, gcum, beta, 1, nctx)
    y0, hs0 = _ssd_scan_fwd("ssd_fwd0", xs, dtc, acc, bm, cm, 0, nctx)
    y1, hs1 = _ssd_scan_fwd("ssd_fwd1", xs, dtc, acc, bm, cm, 1, nctx)
    post_rows = [(o0, D, 0, 1), (o1, D, 0, 1), (proj, D, C_ZG // D, 1), (y0, S_INNER, 0, 1), (y1, S_INNER, 0, 1),
                 (xs, S_INNER, 0, 1), (proj, S_INNER, C_ZS // S_INNER, 1)]
    post_pars = [gnw, ssd8, snw]
    og, ys = _rowwise("post", f_post, post_rows, post_pars, [(D, BF16), (S_INNER, BF16)], nlt)
    pg = _mm("mm_pg", og, full["w_br_gdn"], "nn", F32)
    ps = _mm("mm_ps", ys, full["w_br_ssm"], "nn", F32, tk=2048)
    merge_rows = [(proj, S_INNER, C_GATE // S_INNER, 1), (pg, D, 0, 0), (ps, D, 0, 0)]
    merged = _rowwise("merge", f_merge, merge_rows, [], [(D, BF16)], nlt)[0]
    mix = _mm("mm_mix", merged, full["w_out"], "nn", F32)
    res_rows = [(x2, D, 0, 0), (mix, D, 0, 0)]
    res_pars = [g1, n2w, sc2, sh2]
    h1, f = _rowwise("res1", f_res1, res_rows, res_pars, [(D, F32), (D, BF16)], nlt)
    u = _mm("mm_u", f, full["w_ffn_in"], "nn", F32, tn=1408)
    hact = _rowwise("act", f_act, [(u, 2 * D_FF, 0, 0)], [], [(D_FF, BF16)], nlt)[0]
    ff = _mm("mm_ff", hact, full["w_ffn_out"], "nn", F32, tk=2816)

    fin_rows = [(h1, D, 0, 0), (ff, D, 0, 0), (tgt, D, 0, 0)]
    d_h1a, d_ff, d_g2, d_nfw, loss_acc = _rowwise_bwd(
        "final", f_final, fin_rows, [g2, nfw], ["one"], [(0, F32), (1, BF16)], [0, 1], nlt, loss_out=True)
    d_hact = _mm("mm_dhact", d_ff, full["w_ffn_out"], "nt", BF16, tn=1408)
    g_w_ffn_out = _mm("mm_gwffo", hact, d_ff, "tn", BF16, tm=1408, tk=2048)
    d_u = _rowwise_bwd("act_bwd", f_act, [(u, 2 * D_FF, 0, 0)], [], [[(d_hact, D_FF, 0, 0)]], [(0, BF16)], [], nlt)[0]
    d_f = _mm("mm_df", d_u, full["w_ffn_in"], "nt", BF16, tk=2816)
    g_w_ffn_in = _mm("mm_gwffi", f, d_u, "tn", BF16, tn=1408, tk=2048)
    d_xres, d_mix, d_g1, d_n2w, d_sc2, d_sh2 = _rowwise_bwd(
        "res1_bwd", f_res1, res_rows, res_pars, [[(d_h1a, D, 0, 0)], [(d_f, D, 0, 0)]], [(0, F32), (1, BF16)], [0, 1, 2, 3], nlt)
    d_merged = _mm("mm_dmerged", d_mix, full["w_out"], "nt", BF16)
    g_w_out = _mm("mm_gwout", merged, d_mix, "tn", BF16, tk=2048)
    d_gate, d_pg, d_ps = _rowwise_bwd(
        "merge_bwd", f_merge, merge_rows, [], [[(d_merged, D, 0, 0)]], [(0, BF16), (1, BF16), (2, BF16)], [], nlt)
    d_og = _mm("mm_dog", d_pg, full["w_br_gdn"], "nt", BF16)
    g_w_br_gdn = _mm("mm_gwbrg", og, d_pg, "tn", BF16, tk=2048)
    d_ys = _mm("mm_dys", d_ps, full["w_br_ssm"], "nt", BF16, tn=2048)
    g_w_br_ssm = _mm("mm_gwbrs", ys, d_ps, "tn", BF16, tk=2048)
    d_o, d_zg, d_y, d_xs_post, d_zs, d_gnw, d_ssd8, d_snw = _rowwise_bwd(
        "post_bwd", f_post, post_rows, post_pars, [[(d_og, D, 0, 0)], [(d_ys, S_INNER, 0, 0)]],
        [(0, F32), (2, BF16), (3, F32), (5, F32), (6, BF16)], [0, 1, 2], nlt)
    late_grads = dict(w_br_gdn=g_w_br_gdn, w_br_ssm=g_w_br_ssm, w_out=g_w_out, w_ffn_in=g_w_ffn_in, w_ffn_out=g_w_ffn_out)
    late_a2a = _AllToAll([_blocks_from_full(late_grads[n], n in BIG_COL_SHARDED) for n in BIG_LATE])
    dq0, dk0, dv0, dg0, db0, recv_late = _gdn_scan_bwd("gdn_bwd0", q, k, v, gcum, beta, ss0, ts0, d_o, 0, nctx, ride=late_a2a)
    dq1, dk1, dv1, dg1, db1 = _gdn_scan_bwd("gdn_bwd1", q, k, v, gcum, beta, ss1, ts1, d_o, 1, nctx)
    dxs0, ddt0, dac0, dbm0, dcm0 = _ssd_scan_bwd("ssd_bwd0", xs, dtc, acc, bm, cm, hs0, d_y, 0, nctx)
    dxs1, ddt1, dac1, dbm1, dcm1 = _ssd_scan_bwd("ssd_bwd1", xs, dtc, acc, bm, cm, hs1, d_y, 1, nctx)
    row = lambda arr, wd: (arr, wd, 0, 0)
    d_qkv_raw, d_ab, d_gcw, d_gcb, d_alog16, d_dtb16 = _rowwise_bwd(
        "gdnprep_bwd", f_gdnprep, gp_rows, gp_pars,
        [[row(dq0, D), row(dq1, D)], [row(dk0, D), row(dk1, D)], [row(dv0, D), row(dv1, D)],
         [row(dg0, 128), row(dg1, 128)], [row(db0, 128), row(db1, 128)]],
        [(0, BF16), (1, BF16)], [0, 1, 2, 3], ntl, base=0)
    d_xbc_raw, d_dt, d_scw, d_scb, d_alog64, d_dtb64 = _rowwise_bwd(
        "ssmprep_bwd", f_ssmprep, sp_rows, sp_pars,
        [[row(dxs0, S_INNER), row(dxs1, S_INNER), (d_xs_post, S_INNER, 0, -1, True)], [row(dbm0, 512), row(dbm1, 512)],
         [row(dcm0, 512), row(dcm1, 512)], [row(ddt0, 128), row(ddt1, 128)], [row(dac0, 128), row(dac1, 128)]],
        [(0, BF16), (1, BF16)], [0, 1, 2, 3], ntl, base=0)
    ctx_zero = lambda wd: jnp.zeros((TM, wd), BF16)
    d_proj = jnp.concatenate([
        d_qkv_raw, d_xbc_raw, jnp.concatenate([ctx_zero(S_INNER), d_zs], axis=0),
        jnp.concatenate([ctx_zero(S_INNER), d_gate], axis=0), jnp.concatenate([ctx_zero(D), d_zg], axis=0), d_ab, d_dt], axis=1)
    g_w_cat = _mm("mm_gwcat", a, d_proj, "tn", BF16, tn=768, tk=2816)
    w_in_a2a = _AllToAll([_blocks_from_full(_w_uncat(g_w_cat), True)])
    d_a, recv_w_in = _mm("mm_da", d_proj, w_cat, "nt", BF16, tk=3840, ride=w_in_a2a)
    d_xa, d_n1w, d_sc1, d_sh1, d_csc1, d_csh1 = _rowwise_bwd(
        "pre_bwd", f_pre_thru, [(xa, D, 0, 0)], pre_pars, [[row(d_a, D)], [(d_xres, D, 0, -1, True)]],
        [(0, F32)], [0, 1, 2, 3, 4], ntl, base=0)
    zero4 = jnp.zeros((1, 4 * D), F32)
    d_mod = jnp.concatenate([
        jnp.concatenate([d_sh1, d_sc1, d_g1, d_sh2, d_sc2, d_g2], axis=1),
        jnp.concatenate([d_csh1, d_csc1, zero4], axis=1), jnp.zeros((14, 6 * D), F32)], axis=0)
    d_a16 = _mm("mm_da16", d_mod, full["ada_w"], "nt", F32)
    d_cvec = _rowwise_bwd("silu_c_bwd", f_silu_rows, [(cvec, D, 0, 0)], [], [[row(d_a16, D)]], [(0, F32)], [], 1, tm=16)[0]

    recv = dict(zip(BIG_LATE, recv_late), w_in=recv_w_in[0])
    big_un = {n: _reduce_adam("adam_" + n, recv[n], wts[n][0], mom1[n][0], mom2[n][0], BIG_ADAM_ROWS[n])
              for n in BIG if n != "ada_w"}

    small_g = dict(c_ctx=d_cvec[1], ada_b=d_mod[0] + d_mod[1], norm1_w=d_n1w, gdn_conv_b=d_gcb,
                   gdn_a_log=d_alog16[0, :2 * G_HEADS], gdn_dt_bias=d_dtb16[0, :2 * G_HEADS], gdn_norm_w=d_gnw,
                   ssm_conv_b=d_scb, ssm_a_log=d_alog64[0, :2 * S_HEADS], ssm_dt_bias=d_dtb64[0, :2 * S_HEADS],
                   ssm_d=d_ssd8[0, :S_HEADS], ssm_norm_w=d_snw, norm2_w=d_n2w, norm_f_w=d_nfw,
                   gdn_conv_w=d_gcw.reshape(3, XBC), ssm_conv_w=d_scw.reshape(3, XBC))
    small_names = SMALL + CONV
    factors = [a16[0].astype(F32), d_mod[0], d_mod[1]]
    sg_pack = _pack([small_g[n] for n in small_names] + factors, rows=56)
    recv_all = _exchange("ag_small_grads", _AllGather([sg_pack]))[0]
    recv_s = recv_all[:, :48]
    fac = recv_all[:, 42:55].reshape(N_DEV, -1)
    my_cols = lambda z: lax.dynamic_slice(z, (0, me_i * (6 * D // N_DEV)), (N_DEV, 6 * D // N_DEV))
    lhs = jnp.concatenate([fac[:, :D], jnp.broadcast_to(a16[1:2].astype(F32), (N_DEV, D))], axis=0)
    rhs = jnp.concatenate([my_cols(fac[:, D:7 * D]), my_cols(fac[:, 7 * D:])], axis=0)
    g_ada_w = _mm("mm_gwada", lhs, rhs, "tn", F32)
    big_un["ada_w"] = _reduce_adam("adam_ada_w", g_ada_w[None], wts["ada_w"][0], mom1["ada_w"][0], mom2["ada_w"][0],
                                   BIG_ADAM_ROWS["ada_w"])

    def placed(src, n):
        if n not in CONV:
            return src[n]
        return lax.dynamic_update_slice(jnp.zeros((3, XBC), F32), src[n][0], (0, me_i * CONV_SHARD))

    small_out = _reduce_adam("adam_small", recv_s, *[_pack([placed(src, n) for n in small_names], rows=48)
                                                     for src in (wts, mom1, mom2)], 48)
    small_shapes = [wts[n].shape if n in SMALL else (3, XBC) for n in small_names]
    small_un = [_unpack(buf, small_shapes) for buf in small_out]

    res = [{}, {}, {}, {}]
    for kind in range(4):
        for n in BIG:
            res[kind][n] = big_un[n][kind].reshape(wts[n].shape)
        for n, val in zip(small_names, small_un[kind]):
            if n in CONV:
                val = lax.dynamic_slice(val, (0, me_i * CONV_SHARD), (3, CONV_SHARD)).reshape(wts[n].shape)
            res[kind][n] = val
    loss = lax.psum(loss_acc[0, 0], ("x", "y", "c"))
    grad_x = d_xa[TM:].reshape(x.shape)
    return (loss, grad_x, *[res[0][n] for n in order], *[res[1][n] for n in order], *[res[2][n] for n in order],
            *[res[3][n] for n in order])
```

```python
import functools

import jax
import jax.numpy as jnp
from jax import lax
from jax.experimental import pallas as pl
from jax.experimental.pallas import tpu as pltpu

F32 = jnp.float32
BF16 = jnp.bfloat16
HI = lax.Precision.HIGHEST
HIGH = lax.Precision.HIGH
MESH = pl.DeviceIdType.MESH

D = 1024
CH = 64
TM = 256
EPS = 1e-6
NEG = -1e30
G_HEADS = 8
DK = 128
S_HEADS = 32
S_P = 64
S_GROUPS = 4
S_N = 128
S_INNER = 2048
XBC = 3072
D_FF = 2816
N_DEV = 8
PACK_W = 1024
VMEM_LIMIT = 56 * 1024 * 1024

ADAM_LR = 0.001
ADAM_B1 = 0.9
ADAM_B2 = 0.999
ADAM_EPS = 1e-08
ADAM_WD = 0.01
ADAM_STEP = 10

C_QKV, C_XBC, C_ZS, C_GATE, C_ZG, C_AB, C_DT, C_END = 0, 3072, 6144, 8192, 10240, 11264, 11392, 11520
O_QKV, O_ZG, O_AB, O_ZS, O_XBC, O_DT, O_GATE, O_END = 0, 3072, 4096, 4128, 6176, 9248, 9312, 11360


def _dot(a, b, prec=None):
    return jnp.dot(a, b, precision=prec, preferred_element_type=F32)


def _dot_nt(a, b, prec=None):
    return lax.dot_general(a, b, (((1,), (1,)), ((), ())), precision=prec, preferred_element_type=F32)


def _dot_tn(a, b, prec=None):
    return lax.dot_general(a, b, (((0,), (0,)), ((), ())), precision=prec, preferred_element_type=F32)


def _iota(shape, dim):
    return lax.broadcasted_iota(jnp.int32, shape, dim)


def _rms(x):
    return x * lax.rsqrt(jnp.mean(x * x, axis=-1, keepdims=True) + EPS)


def _l2n(x):
    return x * lax.rsqrt(jnp.sum(x * x, axis=-1, keepdims=True) + EPS)


def _silu(x):
    return x * jax.nn.sigmoid(x)


def _softplus(x):
    return jnp.maximum(x, 0.0) + jnp.log1p(jnp.exp(-jnp.abs(x)))


def _roll_rows(x, s):
    return pltpu.roll(x, s, 0)


def _up_raw(x, keep_up):
    return jnp.where(keep_up > 0.0, _roll_rows(x, 1), 0.0)


def _dn_raw(x, keep_dn):
    return jnp.where(keep_dn > 0.0, _roll_rows(x, x.shape[0] - 1), 0.0)


@jax.custom_vjp
def _shift_up(x, keep_up, keep_dn):
    return _up_raw(x, keep_up)


def _shift_up_fwd(x, keep_up, keep_dn):
    return _up_raw(x, keep_up), (keep_up, keep_dn)


def _shift_up_bwd(res, g):
    keep_up, keep_dn = res
    return _dn_raw(g, keep_dn), jnp.zeros_like(keep_up), jnp.zeros_like(keep_dn)


_shift_up.defvjp(_shift_up_fwd, _shift_up_bwd)


@jax.custom_vjp
def _shift_dn(x, keep_up, keep_dn):
    return _dn_raw(x, keep_dn)


def _shift_dn_fwd(x, keep_up, keep_dn):
    return _dn_raw(x, keep_dn), (keep_up, keep_dn)


def _shift_dn_bwd(res, g):
    keep_up, keep_dn = res
    return _up_raw(g, keep_up), jnp.zeros_like(keep_up), jnp.zeros_like(keep_dn)


_shift_dn.defvjp(_shift_dn_fwd, _shift_dn_bwd)


def _conv_keep(is_ctx, n):
    r = _iota((n, 1), 0)
    pos = jnp.where(is_ctx, r, r & (CH - 1))
    end = jnp.where(is_ctx, n - 1, CH - 1)
    return jnp.where(pos == 0, 0.0, 1.0).astype(F32), jnp.where(pos == end, 0.0, 1.0).astype(F32)


def _conv_silu(u, w3, b, keep_up, keep_dn):
    conv = b + _shift_up(u, keep_up, keep_dn) * w3[0] + u * w3[1] + _shift_dn(u, keep_up, keep_dn) * w3[2]
    return _silu(conv)


def _chunk_tri(n, rev):
    i = _iota((n, n), 0)
    j = _iota((n, n), 1)
    same = (i // CH) == (j // CH)
    seen = (i <= j) if rev else (i >= j)
    return jnp.where(same & seen, 1.0, 0.0).astype(F32)


def _expand_mat(rows, cols, per, base):
    r = _iota((rows, cols), 0)
    c = _iota((rows, cols), 1)
    return jnp.where(r == base + c // per, 1.0, 0.0).astype(F32)


def f_silu_rows(is_ctx, cvec):
    return (_silu(cvec).astype(BF16),)


def f_pre(is_ctx, x, n1w, sc, sh, csc, csh):
    sc_e = jnp.where(is_ctx, csc, sc)
    sh_e = jnp.where(is_ctx, csh, sh)
    a = _rms(x) * n1w * (1.0 + sc_e) + sh_e
    return (a.astype(BF16),)


def f_pre_thru(is_ctx, x, n1w, sc, sh, csc, csh):
    return f_pre(is_ctx, x, n1w, sc, sh, csc, csh)[0], x


def f_gdnprep(is_ctx, qkv_raw, ab_raw, cw, cb, alog, dtb):
    n = qkv_raw.shape[0]
    keep_up, keep_dn = _conv_keep(is_ctx, n)
    s = _conv_silu(qkv_raw.astype(F32), cw, cb, keep_up, keep_dn)
    qs, ks, vs = [], [], []
    for h in range(G_HEADS):
        qs.append(_l2n(s[:, h * DK:(h + 1) * DK]) * (DK ** -0.5))
        ks.append(_l2n(s[:, D + h * DK:D + (h + 1) * DK]))
    q = jnp.concatenate(qs, axis=1)
    k = jnp.concatenate(ks, axis=1)
    v = s[:, 2 * D:3 * D]
    lane = _iota(ab_raw.shape, 1)
    g = jnp.where(lane < 2 * G_HEADS, -jnp.exp(alog) * _softplus(ab_raw + dtb), 0.0)
    gcum = jnp.where(lane < G_HEADS, _dot(_chunk_tri(n, False), g, HI), _dot(_chunk_tri(n, True), g, HI))
    beta = jax.nn.sigmoid(ab_raw)
    return q, k, v, gcum, beta


def f_ssmprep(is_ctx, xbc_raw, dt_raw, cw, cb, alog, dtb):
    n = xbc_raw.shape[0]
    keep_up, keep_dn = _conv_keep(is_ctx, n)
    s = _conv_silu(xbc_raw.astype(F32), cw, cb, keep_up, keep_dn)
    xs = s[:, :S_INNER]
    bm = s[:, S_INNER:S_INNER + S_GROUPS * S_N]
    cm = s[:, S_INNER + S_GROUPS * S_N:]
    lane = _iota(dt_raw.shape, 1)
    dt = jnp.where(lane < 2 * S_HEADS, _softplus(dt_raw + dtb), 0.0)
    da = dt * (-jnp.exp(alog))
    acum = jnp.where(lane < S_HEADS, _dot(_chunk_tri(n, False), da, HI), _dot(_chunk_tri(n, True), da, HI))
    return xs, bm, cm, dt, acum


def f_post(is_ctx, o_f, o_b, zg, y_f, y_b, xs, zs, gnw, ssd8, snw):
    zg, zs = zg.astype(F32), zs.astype(F32)
    o = o_f + o_b
    ogs = []
    for h in range(G_HEADS):
        sl = slice(h * DK, (h + 1) * DK)
        ogs.append(_rms(o[:, sl]) * gnw * _silu(zg[:, sl]))
    og = jnp.concatenate(ogs, axis=1)
    row0 = jnp.where(_iota(ssd8.shape, 0) == 0, 1.0, 0.0).astype(F32)
    dexp = jnp.sum(_dot(ssd8 * row0, _expand_mat(128, S_INNER, S_P, 0), HI), axis=0, keepdims=True)
    y = (y_f + y_b + dexp * xs) * _silu(zs)
    gw = S_INNER // S_GROUPS
    ys = jnp.concatenate([_rms(y[:, i * gw:(i + 1) * gw]) * snw[:, i * gw:(i + 1) * gw] for i in range(S_GROUPS)], axis=1)
    return og.astype(BF16), ys.astype(BF16)


def f_merge(is_ctx, gate, pg, ps):
    gate = gate.astype(F32)
    m = jax.nn.sigmoid(gate[:, :D]) * pg + jax.nn.sigmoid(gate[:, D:]) * ps
    return (m.astype(BF16),)


def f_res1(is_ctx, x, mix, g1, n2w, sc2, sh2):
    h1 = x + g1 * mix
    f = _rms(h1) * n2w * (1.0 + sc2) + sh2
    return h1, f.astype(BF16)


def f_act(is_ctx, u):
    u = u.astype(F32)
    return ((_silu(u[:, :D_FF]) * u[:, D_FF:]).astype(BF16),)


def f_final(is_ctx, h1, ff, tgt, g2, nfw):
    h2 = h1 + g2 * ff
    y = _rms(h2) * nfw
    err = y - tgt
    return (0.5 * jnp.sum(jnp.mean(err * err, axis=-1, keepdims=True), axis=0, keepdims=True),)


def _each(fn, *lists):
    return [fn(*args) for args in zip(*lists)]


def _tri_inverse_all(mats):
    n = mats[0].shape[0]
    eye = jnp.where(_iota((n, n), 0) == _iota((n, n), 1), 1.0, 0.0).astype(F32)
    t = [eye - a for a in mats]
    p = [_dot(a, a, HIGH) for a in mats]
    for r in range(5):
        t = _each(lambda t_, p_: t_ + _dot(t_, p_, HIGH), t, p)
        if r < 4:
            p = [_dot(p_, p_, HIGH) for p_ in p]
    return t


@jax.custom_vjp
def _inverse_given(a, t):
    return t


def _inverse_given_fwd(a, t):
    return t, t


def _inverse_given_bwd(t, g):
    return -_dot_nt(_dot_tn(t, g, HIGH), t, HIGH), jnp.zeros_like(t)


_inverse_given.defvjp(_inverse_given_fwd, _inverse_given_bwd)


def gdn_chunk(ss, qs, ks, vs, gcs, grs, bcs, rev, t_known=None):
    c = qs[0].shape[0]
    ii = _iota((c, c), 0)
    jj = _iota((c, c), 1)
    incl = (ii <= jj) if rev else (ii >= jj)
    strict = (ii < jj) if rev else (ii > jj)
    last = 0 if rev else c - 1
    is_last = _iota((c, 1), 0) == last
    decay = _each(lambda gc, gr: jnp.exp(jnp.where(incl, gc - gr, NEG)), gcs, grs)
    kb = _each(lambda k, bc: k * bc, ks, bcs)
    a = _each(lambda kb_, k, dc: jnp.where(strict, _dot_nt(kb_, k) * dc, 0.0), kb, ks, decay)
    t = _tri_inverse_all(a) if t_known is None else _each(_inverse_given, a, t_known)
    eg = [jnp.exp(gc) for gc in gcs]
    rhs = _each(lambda kb_, eg_, v, bc: jnp.concatenate([kb_ * eg_, v * bc], axis=1), kb, eg, vs, bcs)
    wu = _each(lambda t_, r: _dot(t_, r, HIGH), t, rhs)
    lhs = _each(lambda wu_, q, eg_: jnp.concatenate([wu_[:, :DK], q * eg_], axis=0), wu, qs, eg)
    ws = _each(_dot, lhs, ss)
    v_new = _each(lambda wu_, ws_: wu_[:, DK:] - ws_[:c], wu, ws)
    attn = _each(lambda q, k, dc: _dot_nt(q, k) * dc, qs, ks, decay)
    o = _each(lambda ws_, at, vn: ws_[c:] + _dot(at, vn), ws, attn, v_new)
    gtot = [jnp.sum(jnp.where(is_last, gc, 0.0), axis=0, keepdims=True) for gc in gcs]
    s_new = _each(lambda s, k, gc, gt_, vn: s * jnp.exp(gt_) + _dot_tn(k * jnp.exp(gt_ - gc), vn), ss, ks, gcs, gtot, v_new)
    return s_new, o, t


def ssd_pick(dt0s, dt1s, ac0s, ac1s, ar0s, ar1s):
    lo = _iota((CH, 128), 1) < S_P
    pick = lambda u0, u1: jnp.where(lo, u0, u1)
    return _each(pick, dt0s, dt1s), _each(pick, ac0s, ac1s), _each(pick, ar0s, ar1s)


def ssd_chunk(hts, xs, dts, acs, acr, bgs, cgs, rev):
    c = xs[0].shape[0]
    npair = len(xs)
    grp = [p * len(bgs) // npair for p in range(npair)]
    lane = _iota((c, 128), 1)
    ii = _iota((c, 128), 0)
    jl = lane & (S_P - 1)
    lo = lane < S_P
    seen = (ii <= jl) if rev else (ii >= jl)
    last = 0 if rev else c - 1
    split = lambda z: jnp.concatenate([jnp.where(lo, z, 0.0), jnp.where(lo, 0.0, z)], axis=0)
    cb = _each(lambda bg, cg: _dot_nt(cg, jnp.concatenate([bg, bg], axis=0)), bgs, cgs)
    seg = _each(lambda ac, ar: jnp.exp(jnp.where(seen, ac - ar, NEG)), acs, acr)
    xdt = _each(lambda x, dt: x * dt, xs, dts)
    ydiag = [_dot(cb[grp[p]] * seg[p], split(xdt[p])) for p in range(npair)]
    yoff = [_dot(cgs[grp[p]], hts[p]) * jnp.exp(acs[p]) for p in range(npair)]
    atot = [jnp.sum(jnp.where(ii == last, ac, 0.0), axis=0, keepdims=True) for ac in acs]
    h_new = [hts[p] * jnp.exp(atot[p]) + _dot_tn(bgs[grp[p]], xdt[p] * jnp.exp(atot[p] - acs[p])) for p in range(npair)]
    return h_new, _each(lambda a_, b_: a_ + b_, ydiag, yoff)


def f_adamw(w, g, m, v):
    m = ADAM_B1 * m + (1.0 - ADAM_B1) * g
    v = ADAM_B2 * v + (1.0 - ADAM_B2) * jnp.square(g)
    m_hat = m / (1.0 - ADAM_B1 ** ADAM_STEP)
    v_hat = v / (1.0 - ADAM_B2 ** ADAM_STEP)
    delta = -ADAM_LR * (m_hat / (jnp.sqrt(v_hat) + ADAM_EPS) + ADAM_WD * w)
    return delta, m, v


def _cparams(sem):
    return pltpu.CompilerParams(dimension_semantics=sem, vmem_limit_bytes=VMEM_LIMIT)


def _pick(n, target):
    if n <= target:
        return n
    best = None
    for t in range(128, target + 1, 128):
        if n % t == 0:
            best = t
    assert best is not None, (n, target)
    return best


def _row_spec(tm, width, colblk, rowoff):
    return pl.BlockSpec((tm, width), lambda i: (i + rowoff, colblk))


def _par_spec(shape):
    nd = len(shape)
    return pl.BlockSpec(tuple(shape), lambda i: (0,) * nd)


def _rowwise(name, fn, rows, pars, outs, ntiles, base=1, tm=TM):
    nr, npar = len(rows), len(pars)

    def body(*refs):
        is_ctx = (pl.program_id(0) + base) == 0
        res = fn(is_ctx, *[r[...] for r in refs[:nr]], *[p[...] for p in refs[nr:nr + npar]])
        for o_ref, r in zip(refs[nr + npar:], res):
            o_ref[...] = r.astype(o_ref.dtype)

    return pl.pallas_call(
        body, grid=(ntiles,), name=name,
        in_specs=[_row_spec(tm, wd, cb, ro) for (_, wd, cb, ro) in rows] + [_par_spec(p.shape) for p in pars],
        out_specs=[_row_spec(tm, wd, 0, 0) for (wd, _) in outs],
        out_shape=[jax.ShapeDtypeStruct((ntiles * tm, wd), dt) for (wd, dt) in outs],
        compiler_params=_cparams(("arbitrary",)),
    )(*[r[0] for r in rows], *pars)


def _ct_spec(tm, desc):
    _, wd, cb, ro = desc[:4]
    if len(desc) > 4 and desc[4]:
        return pl.BlockSpec((tm, wd), lambda i: (jnp.maximum(i + ro, 0), cb))
    return _row_spec(tm, wd, cb, ro)


def _rowwise_bwd(name, fn, rows, pars, cts, drows, dpars, ntiles, base=1, loss_out=False, tm=TM):
    nr, npar = len(rows), len(pars)
    ct_rows = [d for ct in cts if isinstance(ct, list) for d in ct]
    nct = len(ct_rows)

    def body(*refs):
        i = pl.program_id(0)
        is_ctx = (i + base) == 0
        rows_v = [r[...] for r in refs[:nr]]
        pars_v = [p[...] for p in refs[nr:nr + npar]]
        ct_refs = list(refs[nr + npar:nr + npar + nct])
        out_refs = list(refs[nr + npar + nct:])
        outs, vjp = jax.vjp(lambda rv, pv: fn(is_ctx, *rv, *pv), rows_v, pars_v)

        def ct_value(desc):
            val = ct_refs.pop(0)[...].astype(F32)
            if len(desc) > 4 and desc[4]:
                val = jnp.where(is_ctx, 0.0, val)
            return val

        ct_vals = []
        for o, ct in zip(outs, cts):
            if ct is None:
                ct_vals.append(jnp.zeros_like(o))
            elif isinstance(ct, str):
                ct_vals.append(jnp.ones_like(o))
            else:
                acc = ct_value(ct[0])
                for desc in ct[1:]:
                    acc = acc + ct_value(desc)
                ct_vals.append(acc.astype(o.dtype))
        d_rows, d_pars = vjp(tuple(ct_vals))
        for (ri, _), o_ref in zip(drows, out_refs[:len(drows)]):
            o_ref[...] = d_rows[ri].astype(o_ref.dtype)
        acc_refs = out_refs[len(drows):]
        acc_vals = [d_pars[pi] for pi in dpars]
        if loss_out:
            acc_vals.append(jnp.broadcast_to(outs[0], (8, 128)))

        @pl.when(i == 0)
        def _():
            for o_ref, val in zip(acc_refs, acc_vals):
                o_ref[...] = val

        @pl.when(i > 0)
        def _():
            for o_ref, val in zip(acc_refs, acc_vals):
                o_ref[...] += val

    acc_shapes = [pars[pi].shape for pi in dpars] + ([(8, 128)] if loss_out else [])
    return pl.pallas_call(
        body, grid=(ntiles,), name=name,
        in_specs=[_row_spec(tm, wd, cb, ro) for (_, wd, cb, ro) in rows] + [_par_spec(p.shape) for p in pars]
        + [_ct_spec(tm, d) for d in ct_rows],
        out_specs=[_row_spec(tm, rows[ri][1], 0, 0) for (ri, _) in drows] + [_par_spec(s) for s in acc_shapes],
        out_shape=[jax.ShapeDtypeStruct((ntiles * tm, rows[ri][1]), dt) for (ri, dt) in drows]
        + [jax.ShapeDtypeStruct(tuple(s), F32) for s in acc_shapes],
        compiler_params=_cparams(("arbitrary",)),
    )(*[r[0] for r in rows], *pars, *[r[0] for r in ct_rows])


def _mm(name, a, b, mode, out_dtype, tm=1024, tn=1024, tk=1024, ride=None):
    if mode == "nn":
        (m, kd), (_, n) = a.shape, b.shape
    elif mode == "nt":
        (m, kd), (n, _) = a.shape, b.shape
    else:
        (kd, m), (_, n) = a.shape, b.shape
    tm, tn, tk = _pick(m, tm), _pick(n, tn), _pick(kd, tk)
    nk = kd // tk
    a_spec = {"nn": pl.BlockSpec((tm, tk), lambda i, j, k: (i, k)), "nt": pl.BlockSpec((tm, tk), lambda i, j, k: (i, k)),
              "tn": pl.BlockSpec((tk, tm), lambda i, j, k: (k, i))}[mode]
    b_spec = {"nn": pl.BlockSpec((tk, tn), lambda i, j, k: (k, j)), "nt": pl.BlockSpec((tn, tk), lambda i, j, k: (j, k)),
              "tn": pl.BlockSpec((tk, tn), lambda i, j, k: (k, j))}[mode]
    dot = {"nn": _dot, "nt": _dot_nt, "tn": _dot_tn}[mode]

    if nk == 1:
        def body(a_ref, b_ref, o_ref):
            o_ref[...] = dot(a_ref[...].astype(BF16), b_ref[...].astype(BF16)).astype(o_ref.dtype)
    else:
        def body(a_ref, b_ref, o_ref, acc_ref):
            k = pl.program_id(2)
            part = dot(a_ref[...].astype(BF16), b_ref[...].astype(BF16))

            @pl.when(k == 0)
            def _():
                acc_ref[...] = part

            @pl.when((k > 0) & (k < nk - 1))
            def _():
                acc_ref[...] += part

            @pl.when(k == nk - 1)
            def _():
                o_ref[...] = (acc_ref[...] + part).astype(o_ref.dtype)

    grid = (m // tm, n // tn, nk)
    at = lambda pos: functools.reduce(jnp.logical_and, [pl.program_id(ax) == pos(g) for ax, g in enumerate(grid)])
    r_in, r_out, r_shapes, r_scr, r_ops = _ride_args(ride)
    res = pl.pallas_call(
        _ride(body, ride, 2, 1, lambda: at(lambda g: 0), lambda: at(lambda g: g - 1)), grid=grid, name=name,
        in_specs=[a_spec, b_spec] + r_in,
        out_specs=[pl.BlockSpec((tm, tn), lambda i, j, k: (i, j))] + r_out,
        out_shape=[jax.ShapeDtypeStruct((m, n), out_dtype)] + r_shapes,
        scratch_shapes=([] if nk == 1 else [pltpu.VMEM((tm, tn), F32)]) + r_scr,
        compiler_params=_cparams(("arbitrary", "arbitrary", "arbitrary")),
    )(a, b, *r_ops)
    return res[0] if ride is None else (res[0], res[1:])


def _chunk_index(i, nch, nctx, rev):
    if not rev:
        return i
    return jnp.where(i < nctx, nctx - 1 - i, nch + nctx - 1 - i)


def _gdn_cols(d):
    return [d * G_HEADS + h for h in range(G_HEADS)], [2 * G_HEADS + d * G_HEADS + h for h in range(G_HEADS)]


def _gdn_operands(q_ref, k_ref, v_ref, g_ref, b_ref, d):
    cols_g, cols_b = _gdn_cols(d)
    sls = [slice(h * DK, (h + 1) * DK) for h in range(G_HEADS)]
    gt, bt = g_ref[...], b_ref[...]
    gtt = gt.T
    qs = [q_ref[:, sl] for sl in sls]
    ks = [k_ref[:, sl] for sl in sls]
    vs = [v_ref[:, sl] for sl in sls]
    gcs = [gt[:, cg:cg + 1] for cg in cols_g]
    grs = [gtt[cg:cg + 1, :] for cg in cols_g]
    bcs = [bt[:, cb:cb + 1] for cb in cols_b]
    return sls, qs, ks, vs, gcs, grs, bcs


def _gdn_scan_fwd(name, q, k, v, gcum, beta, d, nctx, ride=None):
    t = q.shape[0]
    nch = t // CH
    rev = d == 1
    cix = lambda i: _chunk_index(i, nch, nctx, rev)
    full = pl.BlockSpec((CH, D), lambda i: (cix(i), 0))
    wide = pl.BlockSpec((CH, 128), lambda i: (cix(i), 0))

    def body(q_ref, k_ref, v_ref, g_ref, b_ref, o_ref, ss_ref, ts_ref, s_scr):
        @pl.when(pl.program_id(0) == 0)
        def _():
            s_scr[...] = jnp.zeros(s_scr.shape, F32)

        sls, qs, ks, vs, gcs, grs, bcs = _gdn_operands(q_ref, k_ref, v_ref, g_ref, b_ref, d)
        ss = [s_scr[h] for h in range(G_HEADS)]
        s_new, o, t = gdn_chunk(ss, qs, ks, vs, gcs, grs, bcs, rev)
        for h in range(G_HEADS):
            ss_ref[0, h] = ss[h]
            ts_ref[0, h] = t[h]
            s_scr[h] = s_new[h]
            o_ref[:, sls[h]] = o[h]

    r_in, r_out, r_shapes, r_scr, r_ops = _ride_args(ride)
    res = pl.pallas_call(
        _ride(body, ride, 5, 3, lambda: pl.program_id(0) == 0, lambda: pl.program_id(0) == nch - 1), grid=(nch,), name=name,
        in_specs=[full, full, full, wide, wide] + r_in,
        out_specs=[full, pl.BlockSpec((1, G_HEADS, DK, DK), lambda i: (i, 0, 0, 0)),
                   pl.BlockSpec((1, G_HEADS, CH, CH), lambda i: (i, 0, 0, 0))] + r_out,
        out_shape=[jax.ShapeDtypeStruct((t, D), F32), jax.ShapeDtypeStruct((nch, G_HEADS, DK, DK), F32),
                   jax.ShapeDtypeStruct((nch, G_HEADS, CH, CH), F32)] + r_shapes,
        scratch_shapes=[pltpu.VMEM((G_HEADS, DK, DK), F32)] + r_scr,
        compiler_params=_cparams(("arbitrary",)),
    )(q, k, v, gcum, beta, *r_ops)
    return res if ride is None else (*res[:3], res[3:])


def _gdn_scan_bwd(name, q, k, v, gcum, beta, ssave, tsave, do, d, nctx, ride=None):
    t = q.shape[0]
    nch = t // CH
    rev = d == 1
    cix = lambda ib: _chunk_index(nch - 1 - ib, nch, nctx, rev)
    full = pl.BlockSpec((CH, D), lambda ib: (cix(ib), 0))
    wide = pl.BlockSpec((CH, 128), lambda ib: (cix(ib), 0))
    do_spec = pl.BlockSpec((CH, D), lambda ib: (jnp.maximum(cix(ib), nctx) - nctx, 0))

    def body(q_ref, k_ref, v_ref, g_ref, b_ref, ss_ref, ts_ref, do_ref, dq_ref, dk_ref, dv_ref, dg_ref, db_ref, ds_scr):
        ib = pl.program_id(0)

        @pl.when(ib == 0)
        def _():
            ds_scr[...] = jnp.zeros(ds_scr.shape, F32)

        sls, qs, ks, vs, gcs, grs, bcs = _gdn_operands(q_ref, k_ref, v_ref, g_ref, b_ref, d)
        t_known = [ts_ref[0, h] for h in range(G_HEADS)]
        cols_g, cols_b = _gdn_cols(d)
        is_lat = cix(ib) >= nctx
        ss = [ss_ref[0, h] for h in range(G_HEADS)]
        do_v = [jnp.where(is_lat, do_ref[:, sl], 0.0) for sl in sls]
        ds_in = [ds_scr[h] for h in range(G_HEADS)]
        _, vjp = jax.vjp(lambda *a: gdn_chunk(*a, rev, t_known)[:2], ss, qs, ks, vs, gcs, grs, bcs)
        ds, dq, dk, dv, dgc, dgr, dbc = vjp((ds_in, do_v))
        lane = _iota((CH, 128), 1)
        sub = _iota((128, CH), 0)
        dg = jnp.zeros((CH, 128), F32)
        dgt = jnp.zeros((128, CH), F32)
        db = jnp.zeros((CH, 128), F32)
        for h in range(G_HEADS):
            ds_scr[h] = ds[h]
            dq_ref[:, sls[h]] = dq[h]
            dk_ref[:, sls[h]] = dk[h]
            dv_ref[:, sls[h]] = dv[h]
            dg = dg + jnp.where(lane == cols_g[h], dgc[h], 0.0)
            dgt = dgt + jnp.where(sub == cols_g[h], dgr[h], 0.0)
            db = db + jnp.where(lane == cols_b[h], dbc[h], 0.0)
        dg_ref[...] = dg + dgt.T
        db_ref[...] = db

    r_in, r_out, r_shapes, r_scr, r_ops = _ride_args(ride)
    res = pl.pallas_call(
        _ride(body, ride, 8, 5, lambda: pl.program_id(0) == 0, lambda: pl.program_id(0) == nch - 1), grid=(nch,), name=name,
        in_specs=[full, full, full, wide, wide,
                  pl.BlockSpec((1, G_HEADS, DK, DK), lambda ib: (nch - 1 - ib, 0, 0, 0)),
                  pl.BlockSpec((1, G_HEADS, CH, CH), lambda ib: (nch - 1 - ib, 0, 0, 0)), do_spec] + r_in,
        out_specs=[full, full, full, wide, wide] + r_out,
        out_shape=[jax.ShapeDtypeStruct((t, D), F32)] * 3 + [jax.ShapeDtypeStruct((t, 128), F32)] * 2 + r_shapes,
        scratch_shapes=[pltpu.VMEM((G_HEADS, DK, DK), F32)] + r_scr,
        compiler_params=_cparams(("arbitrary",)),
    )(q, k, v, gcum, beta, ssave, tsave, do, *r_ops)
    return res if ride is None else (*res[:5], res[5:])


N_PAIRS = S_HEADS // 2


def _ssd_operands(x_ref, dt_ref, ac_ref, b_ref, c_ref, d):
    sls = [slice(p * 128, (p + 1) * 128) for p in range(N_PAIRS)]
    gsl = [slice(g * S_N, (g + 1) * S_N) for g in range(S_GROUPS)]
    cols = [d * S_HEADS + h for h in range(S_HEADS)]
    dtc, acc = dt_ref[...], ac_ref[...]
    act = jnp.concatenate([acc, acc], axis=0).T
    col = lambda z, cc: z[:, cc:cc + 1]
    dts, acs, acr = ssd_pick(
        [col(dtc, cols[2 * p]) for p in range(N_PAIRS)], [col(dtc, cols[2 * p + 1]) for p in range(N_PAIRS)],
        [col(acc, cols[2 * p]) for p in range(N_PAIRS)], [col(acc, cols[2 * p + 1]) for p in range(N_PAIRS)],
        [act[cols[2 * p]:cols[2 * p] + 1, :] for p in range(N_PAIRS)],
        [act[cols[2 * p + 1]:cols[2 * p + 1] + 1, :] for p in range(N_PAIRS)])
    ops = ([x_ref[:, sl] for sl in sls], dts, acs, acr, [b_ref[:, gs] for gs in gsl], [c_ref[:, gs] for gs in gsl])
    return sls, gsl, cols, ops


def _ssd_scan_fwd(name, xs, dtc, acc, bm, cm, d, nctx):
    t = xs.shape[0]
    nch = t // CH
    rev = d == 1
    cix = lambda i: _chunk_index(i, nch, nctx, rev)
    inner = pl.BlockSpec((CH, S_INNER), lambda i: (cix(i), 0))
    wide = pl.BlockSpec((CH, 128), lambda i: (cix(i), 0))
    grp = pl.BlockSpec((CH, S_GROUPS * S_N), lambda i: (cix(i), 0))

    def body(x_ref, dt_ref, ac_ref, b_ref, c_ref, y_ref, hs_ref, h_scr):
        @pl.when(pl.program_id(0) == 0)
        def _():
            h_scr[...] = jnp.zeros(h_scr.shape, F32)

        sls, _, _, ops = _ssd_operands(x_ref, dt_ref, ac_ref, b_ref, c_ref, d)
        hts = [h_scr[p] for p in range(N_PAIRS)]
        h_new, y = ssd_chunk(hts, *ops, rev)
        for p in range(N_PAIRS):
            hs_ref[0, p] = hts[p]
            h_scr[p] = h_new[p]
            y_ref[:, sls[p]] = y[p]

    return pl.pallas_call(
        body, grid=(nch,), name=name,
        in_specs=[inner, wide, wide, grp, grp],
        out_specs=[inner, pl.BlockSpec((1, N_PAIRS, S_N, 128), lambda i: (i, 0, 0, 0))],
        out_shape=[jax.ShapeDtypeStruct((t, S_INNER), F32), jax.ShapeDtypeStruct((nch, N_PAIRS, S_N, 128), F32)],
        scratch_shapes=[pltpu.VMEM((N_PAIRS, S_N, 128), F32)],
        compiler_params=_cparams(("arbitrary",)),
    )(xs, dtc, acc, bm, cm)


def _ssd_scan_bwd(name, xs, dtc, acc, bm, cm, hsave, dy, d, nctx):
    t = xs.shape[0]
    nch = t // CH
    rev = d == 1
    cix = lambda ib: _chunk_index(nch - 1 - ib, nch, nctx, rev)
    inner = pl.BlockSpec((CH, S_INNER), lambda ib: (cix(ib), 0))
    wide = pl.BlockSpec((CH, 128), lambda ib: (cix(ib), 0))
    grp = pl.BlockSpec((CH, S_GROUPS * S_N), lambda ib: (cix(ib), 0))
    dy_spec = pl.BlockSpec((CH, S_INNER), lambda ib: (jnp.maximum(cix(ib), nctx) - nctx, 0))

    def body(x_ref, dt_ref, ac_ref, b_ref, c_ref, hs_ref, dy_ref, dx_ref, ddt_ref, dac_ref, db_ref, dc_ref, dh_scr):
        ib = pl.program_id(0)

        @pl.when(ib == 0)
        def _():
            dh_scr[...] = jnp.zeros(dh_scr.shape, F32)

        sls, gsl, cols, ops = _ssd_operands(x_ref, dt_ref, ac_ref, b_ref, c_ref, d)
        is_lat = cix(ib) >= nctx
        hts = [hs_ref[0, p] for p in range(N_PAIRS)]
        dy_v = [jnp.where(is_lat, dy_ref[:, sl], 0.0) for sl in sls]
        dh_in = [dh_scr[p] for p in range(N_PAIRS)]
        _, vjp = jax.vjp(lambda *a: ssd_chunk(*a, rev), hts, *ops)
        dh, dx, ddts, dacs, dacr, db, dc = vjp((dh_in, dy_v))
        for p in range(N_PAIRS):
            dh_scr[p] = dh[p]
            dx_ref[:, sls[p]] = dx[p]
        r = _iota((S_INNER, 128), 0)
        e_t = jnp.where(_iota((S_INNER, 128), 1) == d * S_HEADS + r // S_P, 1.0, 0.0).astype(F32)
        ddt_ref[...] = _dot(jnp.concatenate(ddts, axis=1), e_t, HIGH)
        dac_cols = _dot(jnp.concatenate(dacs, axis=1), e_t, HIGH)
        sub = _iota((128, 128), 0)
        lane = _iota((128, 128), 1)
        m = jnp.zeros((128, 128), F32)
        for p in range(N_PAIRS):
            m = m + jnp.where(sub == p, jnp.sum(dacr[p], axis=0, keepdims=True), 0.0)
        mt = m.T
        s0 = jnp.where(lane == d * S_HEADS + 2 * sub, 1.0, 0.0).astype(F32)
        s1 = jnp.where(lane == d * S_HEADS + 2 * sub + 1, 1.0, 0.0).astype(F32)
        dac_ref[...] = dac_cols + _dot(mt[:CH], s0, HIGH) + _dot(mt[CH:], s1, HIGH)
        for g in range(S_GROUPS):
            db_ref[:, gsl[g]] = db[g]
            dc_ref[:, gsl[g]] = dc[g]

    return pl.pallas_call(
        body, grid=(nch,), name=name,
        in_specs=[inner, wide, wide, grp, grp,
                  pl.BlockSpec((1, N_PAIRS, S_N, 128), lambda ib: (nch - 1 - ib, 0, 0, 0)), dy_spec],
        out_specs=[inner, wide, wide, grp, grp],
        out_shape=[jax.ShapeDtypeStruct((t, S_INNER), F32)] + [jax.ShapeDtypeStruct((t, 128), F32)] * 2
        + [jax.ShapeDtypeStruct((t, S_GROUPS * S_N), F32)] * 2,
        scratch_shapes=[pltpu.VMEM((N_PAIRS, S_N, 128), F32)],
        compiler_params=_cparams(("arbitrary",)),
    )(xs, dtc, acc, bm, cm, hsave, dy)


def _mesh_pos():
    return lax.axis_index("x"), lax.axis_index("y"), lax.axis_index("c")


def _hbm_specs(n):
    return [pl.BlockSpec(memory_space=pl.ANY)] * n


def _sem_shapes(nw):
    return [pltpu.SemaphoreType.DMA((nw, 7)), pltpu.SemaphoreType.DMA((nw, 7)), pltpu.SemaphoreType.DMA((nw,))]


class _AllGather:
    def __init__(self, shards):
        self.arrays = list(shards)
        self.out_shapes = [jax.ShapeDtypeStruct((N_DEV,) + xs.shape, xs.dtype) for xs in shards]

    def _parts(self, x_refs, out_refs, sems):
        send_sems, recv_sems, local_sems = sems
        nw = len(x_refs)
        x, y, c = _mesh_pos()
        me, sibling = (x, y, c), (x, y, 1 - c)
        chips = [(1 - x, y), (x, 1 - y), (1 - x, 1 - y)]

        def slot(w, px, py, pc):
            return out_refs[w].at[4 * px + 2 * py + pc]

        def copy(w, k, block, to, src=None):
            return pltpu.make_async_remote_copy(
                src_ref=slot(w, *block) if src is None else src, dst_ref=slot(w, *block),
                send_sem=send_sems.at[w, k], recv_sem=recv_sems.at[w, k], device_id=to, device_id_type=MESH)

        mine = [pltpu.make_async_copy(x_refs[w], slot(w, *me), local_sems.at[w]) for w in range(nw)]
        first = []
        for w in range(nw):
            first.append(copy(w, 0, me, sibling, src=x_refs[w]))
            first += [copy(w, 1 + j, me, (*chip, c), src=x_refs[w]) for j, chip in enumerate(chips)]
        return nw, me, sibling, chips, c, copy, mine, first

    def start(self, x_refs, out_refs, sems):
        _, _, _, _, _, _, mine, first = self._parts(x_refs, out_refs, sems)
        for cp in mine + first:
            cp.start()

    def finish(self, x_refs, out_refs, sems):
        nw, me, sibling, chips, c, copy, mine, first = self._parts(x_refs, out_refs, sems)
        passed = []
        for j, chip in enumerate(chips):
            for w in range(nw):
                copy(w, 1 + j, (*chip, c), me).wait_recv()
                fwd = copy(w, 4 + j, (*chip, c), sibling)
                fwd.start()
                passed.append(fwd)
        for w in range(nw):
            copy(w, 0, sibling, me).wait_recv()
            for j, chip in enumerate(chips):
                copy(w, 4 + j, (*chip, 1 - c), me).wait_recv()
        for cp in first + passed:
            cp.wait_send()
        for cp in mine:
            cp.wait()


class _AllToAll:
    def __init__(self, blocks):
        self.arrays = list(blocks)
        self.out_shapes = [jax.ShapeDtypeStruct(g.shape, g.dtype) for g in blocks]

    def _parts(self, g_refs, out_refs, sems):
        send_sems, recv_sems, local_sems = sems
        nw = len(g_refs)
        x, y, c = _mesh_pos()
        me_i = 4 * x + 2 * y + c
        mine = [pltpu.make_async_copy(g_refs[w].at[me_i], out_refs[w].at[me_i], local_sems.at[w]) for w in range(nw)]
        cps = []
        for k in range(1, N_DEV):
            px = 1 - x if (k >> 2) & 1 else x
            py = 1 - y if (k >> 1) & 1 else y
            pc = 1 - c if k & 1 else c
            for w in range(nw):
                cps.append(pltpu.make_async_remote_copy(
                    src_ref=g_refs[w].at[4 * px + 2 * py + pc], dst_ref=out_refs[w].at[me_i],
                    send_sem=send_sems.at[w, k - 1], recv_sem=recv_sems.at[w, k - 1],
                    device_id=(px, py, pc), device_id_type=MESH))
        return mine, cps

    def start(self, g_refs, out_refs, sems):
        mine, cps = self._parts(g_refs, out_refs, sems)
        for cp in mine + cps:
            cp.start()

    def finish(self, g_refs, out_refs, sems):
        mine, cps = self._parts(g_refs, out_refs, sems)
        for cp in cps + mine:
            cp.wait()


def _exchange(name, ex):
    nw = len(ex.arrays)

    def body(*refs):
        ins, outs, sems = refs[:nw], refs[nw:2 * nw], refs[2 * nw:]
        ex.start(ins, outs, sems)
        ex.finish(ins, outs, sems)

    return pl.pallas_call(body, name=name, out_shape=ex.out_shapes, in_specs=_hbm_specs(nw), out_specs=_hbm_specs(nw),
                          scratch_shapes=_sem_shapes(nw))(*ex.arrays)


def _ride(body, ex, n_in, n_out, is_first, is_last):
    if ex is None:
        return body
    nw = len(ex.arrays)

    def riding(*refs):
        ins, ex_in = refs[:n_in], refs[n_in:n_in + nw]
        outs = refs[n_in + nw:n_in + nw + n_out]
        ex_out = refs[n_in + nw + n_out:n_in + 2 * nw + n_out]
        rest = refs[n_in + 2 * nw + n_out:]
        scratch, sems = rest[:len(rest) - 3], rest[len(rest) - 3:]

        @pl.when(is_first())
        def _():
            ex.start(ex_in, ex_out, sems)

        body(*ins, *outs, *scratch)

        @pl.when(is_last())
        def _():
            ex.finish(ex_in, ex_out, sems)

    return riding


def _ride_args(ex):
    if ex is None:
        return [], [], [], [], []
    nw = len(ex.arrays)
    return _hbm_specs(nw), _hbm_specs(nw), list(ex.out_shapes), _sem_shapes(nw), list(ex.arrays)


def _reduce_adam(name, recv, w, m, v, tm):
    rows, width = w.shape
    nslot = recv.shape[0]

    def body(recv_ref, w_ref, m_ref, v_ref, g_ref, d_ref, m2_ref, v2_ref):
        g = recv_ref[0].astype(F32)
        for s in range(1, nslot):
            g = g + recv_ref[s].astype(F32)
        delta, m2, v2 = f_adamw(w_ref[...], g, m_ref[...], v_ref[...])
        g_ref[...] = g
        d_ref[...] = delta
        m2_ref[...] = m2
        v2_ref[...] = v2

    row = pl.BlockSpec((tm, width), lambda i: (i, 0))
    return pl.pallas_call(
        body, grid=(rows // tm,), name=name,
        in_specs=[pl.BlockSpec((nslot, tm, width), lambda i: (0, i, 0)), row, row, row],
        out_specs=[row] * 4,
        out_shape=[jax.ShapeDtypeStruct((rows, width), F32)] * 4,
        compiler_params=_cparams(("arbitrary",)),
    )(recv, w, m, v)


BIG = ("w_in", "ada_w", "w_br_gdn", "w_br_ssm", "w_out", "w_ffn_in", "w_ffn_out")
BIG_FIRST = ("ada_w", "w_in")
BIG_LATE = ("w_br_gdn", "w_br_ssm", "w_out", "w_ffn_in", "w_ffn_out")
BIG_COL_SHARDED = ("w_in", "ada_w", "w_ffn_in")
BIG_ADAM_ROWS = dict(w_in=128, ada_w=256, w_br_gdn=128, w_br_ssm=256, w_out=128, w_ffn_in=256, w_ffn_out=352)
CONV = ("gdn_conv_w", "ssm_conv_w")
SMALL = ("c_ctx", "ada_b", "norm1_w", "gdn_conv_b", "gdn_a_log", "gdn_dt_bias", "gdn_norm_w", "ssm_conv_b",
         "ssm_a_log", "ssm_dt_bias", "ssm_d", "ssm_norm_w", "norm2_w", "norm_f_w")
CONV_SHARD = XBC // N_DEV


def _to_rows(a):
    flat = a.reshape(-1)
    pad = (-flat.shape[0]) % PACK_W
    if pad:
        flat = jnp.pad(flat, (0, pad))
    return flat.reshape(-1, PACK_W)


def _pack(arrays, rows=None):
    buf = jnp.concatenate([_to_rows(a) for a in arrays], axis=0)
    if rows is not None and rows > buf.shape[0]:
        buf = jnp.pad(buf, ((0, rows - buf.shape[0]), (0, 0)))
    return buf


def _unpack(buf, shapes):
    out, r0 = [], 0
    for shp in shapes:
        n = 1
        for s in shp:
            n *= s
        nr = -(-n // PACK_W)
        out.append(buf[r0:r0 + nr].reshape(-1)[:n].reshape(shp))
        r0 += nr
    return out


def _full_from_blocks(blocks, col_sharded):
    _, r, c = blocks.shape
    if col_sharded:
        return jnp.transpose(blocks, (1, 0, 2)).reshape(r, N_DEV * c)
    return blocks.reshape(N_DEV * r, c)


def _blocks_from_full(full, col_sharded):
    if col_sharded:
        r, c = full.shape[0], full.shape[1] // N_DEV
        return jnp.transpose(full.reshape(r, N_DEV, c), (1, 0, 2))
    return full.reshape(N_DEV, full.shape[0] // N_DEV, full.shape[1])


def _pad_cols(a, n):
    return jnp.pad(a, ((0, 0), (0, n - a.shape[1])))


def _w_cat(w_in):
    return jnp.concatenate([
        w_in[:, O_QKV:O_ZG], w_in[:, O_XBC:O_DT], w_in[:, O_ZS:O_XBC], w_in[:, O_GATE:O_END], w_in[:, O_ZG:O_AB],
        _pad_cols(w_in[:, O_AB:O_ZS], 128), _pad_cols(w_in[:, O_DT:O_GATE], 128)], axis=1)


def _w_uncat(wc):
    return jnp.concatenate([
        wc[:, C_QKV:C_XBC], wc[:, C_ZG:C_AB], wc[:, C_AB:C_AB + (O_ZS - O_AB)], wc[:, C_ZS:C_GATE], wc[:, C_XBC:C_ZS],
        wc[:, C_DT:C_DT + (O_GATE - O_DT)], wc[:, C_GATE:C_ZG]], axis=1)


def _pad_row(vec, n=128):
    vec = vec.reshape(1, -1)
    return _pad_cols(vec, n)


def kernel(x, c, ctx, c_ctx, ada_w, ada_b, norm1_w, w_in, gdn_conv_w, gdn_conv_b, gdn_a_log, gdn_dt_bias, gdn_norm_w, ssm_conv_w, ssm_conv_b, ssm_a_log, ssm_dt_bias, ssm_d, ssm_norm_w, w_br_gdn, w_br_ssm, w_out, norm2_w, w_ffn_in, w_ffn_out, norm_f_w, loss_target, m_c_ctx, m_ada_w, m_ada_b, m_norm1_w, m_w_in, m_gdn_conv_w, m_gdn_conv_b, m_gdn_a_log, m_gdn_dt_bias, m_gdn_norm_w, m_ssm_conv_w, m_ssm_conv_b, m_ssm_a_log, m_ssm_dt_bias, m_ssm_d, m_ssm_norm_w, m_w_br_gdn, m_w_br_ssm, m_w_out, m_norm2_w, m_w_ffn_in, m_w_ffn_out, m_norm_f_w, v_c_ctx, v_ada_w, v_ada_b, v_norm1_w, v_w_in, v_gdn_conv_w, v_gdn_conv_b, v_gdn_a_log, v_gdn_dt_bias, v_gdn_norm_w, v_ssm_conv_w, v_ssm_conv_b, v_ssm_a_log, v_ssm_dt_bias, v_ssm_d, v_ssm_norm_w, v_w_br_gdn, v_w_br_ssm, v_w_out, v_norm2_w, v_w_ffn_in, v_w_ffn_out, v_norm_f_w):
    wts = dict(c_ctx=c_ctx, ada_w=ada_w, ada_b=ada_b, norm1_w=norm1_w, w_in=w_in, gdn_conv_w=gdn_conv_w, gdn_conv_b=gdn_conv_b, gdn_a_log=gdn_a_log, gdn_dt_bias=gdn_dt_bias, gdn_norm_w=gdn_norm_w, ssm_conv_w=ssm_conv_w, ssm_conv_b=ssm_conv_b, ssm_a_log=ssm_a_log, ssm_dt_bias=ssm_dt_bias, ssm_d=ssm_d, ssm_norm_w=ssm_norm_w, w_br_gdn=w_br_gdn, w_br_ssm=w_br_ssm, w_out=w_out, norm2_w=norm2_w, w_ffn_in=w_ffn_in, w_ffn_out=w_ffn_out, norm_f_w=norm_f_w)
    mom1 = dict(c_ctx=m_c_ctx, ada_w=m_ada_w, ada_b=m_ada_b, norm1_w=m_norm1_w, w_in=m_w_in, gdn_conv_w=m_gdn_conv_w, gdn_conv_b=m_gdn_conv_b, gdn_a_log=m_gdn_a_log, gdn_dt_bias=m_gdn_dt_bias, gdn_norm_w=m_gdn_norm_w, ssm_conv_w=m_ssm_conv_w, ssm_conv_b=m_ssm_conv_b, ssm_a_log=m_ssm_a_log, ssm_dt_bias=m_ssm_dt_bias, ssm_d=m_ssm_d, ssm_norm_w=m_ssm_norm_w, w_br_gdn=m_w_br_gdn, w_br_ssm=m_w_br_ssm, w_out=m_w_out, norm2_w=m_norm2_w, w_ffn_in=m_w_ffn_in, w_ffn_out=m_w_ffn_out, norm_f_w=m_norm_f_w)
    mom2 = dict(c_ctx=v_c_ctx, ada_w=v_ada_w, ada_b=v_ada_b, norm1_w=v_norm1_w, w_in=v_w_in, gdn_conv_w=v_gdn_conv_w, gdn_conv_b=v_gdn_conv_b, gdn_a_log=v_gdn_a_log, gdn_dt_bias=v_gdn_dt_bias, gdn_norm_w=v_gdn_norm_w, ssm_conv_w=v_ssm_conv_w, ssm_conv_b=v_ssm_conv_b, ssm_a_log=v_ssm_a_log, ssm_dt_bias=v_ssm_dt_bias, ssm_d=v_ssm_d, ssm_norm_w=v_ssm_norm_w, w_br_gdn=v_w_br_gdn, w_br_ssm=v_w_br_ssm, w_out=v_w_out, norm2_w=v_norm2_w, w_ffn_in=v_w_ffn_in, w_ffn_out=v_w_ffn_out, norm_f_w=v_norm_f_w)
    order = list(wts)

    seq = x.shape[1]
    t = TM + seq
    ntl, nlt, nctx = t // TM, seq // TM, TM // CH

    me_i = 4 * lax.axis_index("x") + 2 * lax.axis_index("y") + lax.axis_index("c")
    gathered = _exchange("ag_weights", _AllGather([wts[n][0].astype(BF16) for n in BIG_FIRST]))
    full = {n: _full_from_blocks(blk, n in BIG_COL_SHARDED) for n, blk in zip(BIG_FIRST, gathered)}
    late_gather = _AllGather([wts[n][0].astype(BF16) for n in BIG_LATE])
    conv_sh = _pack([wts[n] for n in CONV], rows=8)
    conv_g = _exchange("ag_conv", _AllGather([conv_sh]))[0].reshape(N_DEV, -1)
    ncv = 3 * CONV_SHARD
    for i, n in enumerate(CONV):
        off = -(-ncv // PACK_W) * PACK_W * i
        full[n] = jnp.transpose(conv_g[:, off:off + ncv].reshape(N_DEV, 3, CONV_SHARD), (1, 0, 2)).reshape(3, XBC)
    w_cat = _w_cat(full["w_in"])
    gcw = full["gdn_conv_w"].reshape(3, 1, XBC)
    scw = full["ssm_conv_w"].reshape(3, 1, XBC)

    n1w, n2w, nfw = norm1_w.reshape(1, D), norm2_w.reshape(1, D), norm_f_w.reshape(1, D)
    gcb, scb = gdn_conv_b.reshape(1, XBC), ssm_conv_b.reshape(1, XBC)
    alog16, dtb16 = _pad_row(gdn_a_log), _pad_row(gdn_dt_bias)
    alog64, dtb64 = _pad_row(ssm_a_log), _pad_row(ssm_dt_bias)
    gnw = gdn_norm_w.reshape(1, DK)
    ssd8 = jnp.tile(_pad_row(ssm_d), (8, 1))
    snw = ssm_norm_w.reshape(1, S_INNER)
    x2 = x[0]
    tgt = loss_target[0]
    xa = jnp.concatenate([ctx[0], x2], axis=0)
    cvec = jnp.concatenate([c, c_ctx.reshape(1, D), jnp.zeros((14, D), F32)], axis=0)

    a16 = _rowwise("silu_c", f_silu_rows, [(cvec, D, 0, 0)], [], [(D, BF16)], 1, tm=16)[0]
    mod = _mm("mm_mod", a16, full["ada_w"], "nn", F32) + ada_b
    sh1, sc1, g1, sh2, sc2, g2 = [mod[0:1, i * D:(i + 1) * D] for i in range(6)]
    csh1, csc1 = mod[1:2, 0:D], mod[1:2, D:2 * D]

    pre_pars = [n1w, sc1, sh1, csc1, csh1]
    a = _rowwise("pre", f_pre, [(xa, D, 0, 0)], pre_pars, [(D, BF16)], ntl, base=0)[0]
    proj = _mm("mm_proj", a, w_cat[:, :C_AB], "nn", BF16, tm=1408, tn=1408)
    proj_s = _mm("mm_proj_s", a, w_cat[:, C_AB:], "nn", F32, tm=1408)
    gp_rows = [(proj, XBC, C_QKV // XBC, 0), (proj_s, 128, 0, 0)]
    gp_pars = [gcw, gcb, alog16, dtb16]
    q, k, v, gcum, beta = _rowwise("gdnprep", f_gdnprep, gp_rows, gp_pars, [(D, F32)] * 3 + [(128, F32)] * 2, ntl, base=0)
    sp_rows = [(proj, XBC, C_XBC // XBC, 0), (proj_s, 128, 1, 0)]
    sp_pars = [scw, scb, alog64, dtb64]
    xs, bm, cm, dtc, acc = _rowwise(
        "ssmprep", f_ssmprep, sp_rows, sp_pars, [(S_INNER, F32), (512, F32), (512, F32), (128, F32), (128, F32)], ntl, base=0)
    o0, ss0, ts0, gathered = _gdn_scan_fwd("gdn_fwd0", q, k, v, gcum, beta, 0, nctx, ride=late_gather)
    full.update({n: _full_from_blocks(blk, n in BIG_COL_SHARDED) for n, blk in zip(BIG_LATE, gathered)})
    o1, ss1, ts1 = _gdn_scan_fwd("gdn_fwd1", q, k, v, gcum, beta, 1, nctx)
    y0, hs0 = _ssd_scan_fwd("ssd_fwd0", xs, dtc, acc, bm, cm, 0, nctx)
    y1, hs1 = _ssd_scan_fwd("ssd_fwd1", xs, dtc, acc, bm, cm, 1, nctx)
    post_rows = [(o0, D, 0, 1), (o1, D, 0, 1), (proj, D, C_ZG // D, 1), (y0, S_INNER, 0, 1), (y1, S_INNER, 0, 1),
                 (xs, S_INNER, 0, 1), (proj, S_INNER, C_ZS // S_INNER, 1)]
    post_pars = [gnw, ssd8, snw]
    og, ys = _rowwise("post", f_post, post_rows, post_pars, [(D, BF16), (S_INNER, BF16)], nlt)
    pg = _mm("mm_pg", og, full["w_br_gdn"], "nn", F32)
    ps = _mm("mm_ps", ys, full["w_br_ssm"], "nn", F32, tk=2048)
    merge_rows = [(proj, S_INNER, C_GATE // S_INNER, 1), (pg, D, 0, 0), (ps, D, 0, 0)]
    merged = _rowwise("merge", f_merge, merge_rows, [], [(D, BF16)], nlt)[0]
    mix = _mm("mm_mix", merged, full["w_out"], "nn", F32)
    res_rows = [(x2, D, 0, 0), (mix, D, 0, 0)]
    res_pars = [g1, n2w, sc2, sh2]
    h1, f = _rowwise("res1", f_res1, res_rows, res_pars, [(D, F32), (D, BF16)], nlt)
    u = _mm("mm_u", f, full["w_ffn_in"], "nn", BF16, tn=1408)
    hact = _rowwise("act", f_act, [(u, 2 * D_FF, 0, 0)], [], [(D_FF, BF16)], nlt)[0]
    ff = _mm("mm_ff", hact, full["w_ffn_out"], "nn", F32, tk=2816)

    fin_rows = [(h1, D, 0, 0), (ff, D, 0, 0), (tgt, D, 0, 0)]
    d_h1a, d_ff, d_g2, d_nfw, loss_acc = _rowwise_bwd(
        "final", f_final, fin_rows, [g2, nfw], ["one"], [(0, F32), (1, BF16)], [0, 1], nlt, loss_out=True)
    d_hact = _mm("mm_dhact", d_ff, full["w_ffn_out"], "nt", BF16, tn=1408)
    g_w_ffn_out = _mm("mm_gwffo", hact, d_ff, "tn", BF16, tm=1408, tk=2048)
    d_u = _rowwise_bwd("act_bwd", f_act, [(u, 2 * D_FF, 0, 0)], [], [[(d_hact, D_FF, 0, 0)]], [(0, BF16)], [], nlt)[0]
    d_f = _mm("mm_df", d_u, full["w_ffn_in"], "nt", BF16, tk=2816)
    g_w_ffn_in = _mm("mm_gwffi", f, d_u, "tn", BF16, tn=1408, tk=2048)
    d_xres, d_mix, d_g1, d_n2w, d_sc2, d_sh2 = _rowwise_bwd(
        "res1_bwd", f_res1, res_rows, res_pars, [[(d_h1a, D, 0, 0)], [(d_f, D, 0, 0)]], [(0, F32), (1, BF16)], [0, 1, 2, 3], nlt)
    d_merged = _mm("mm_dmerged", d_mix, full["w_out"], "nt", BF16)
    g_w_out = _mm("mm_gwout", merged, d_mix, "tn", BF16, tk=2048)
    d_gate, d_pg, d_ps = _rowwise_bwd(
        "merge_bwd", f_merge, merge_rows, [], [[(d_merged, D, 0, 0)]], [(0, BF16), (1, BF16), (2, BF16)], [], nlt)
    d_og = _mm("mm_dog", d_pg, full["w_br_gdn"], "nt", BF16)
    g_w_br_gdn = _mm("mm_gwbrg", og, d_pg, "tn", BF16, tk=2048)
    d_ys = _mm("mm_dys", d_ps, full["w_br_ssm"], "nt", BF16, tn=2048)
    g_w_br_ssm = _mm("mm_gwbrs", ys, d_ps, "tn", BF16, tk=2048)
    d_o, d_zg, d_y, d_xs_post, d_zs, d_gnw, d_ssd8, d_snw = _rowwise_bwd(
        "post_bwd", f_post, post_rows, post_pars, [[(d_og, D, 0, 0)], [(d_ys, S_INNER, 0, 0)]],
        [(0, F32), (2, BF16), (3, F32), (5, F32), (6, BF16)], [0, 1, 2], nlt)
    late_grads = dict(w_br_gdn=g_w_br_gdn, w_br_ssm=g_w_br_ssm, w_out=g_w_out, w_ffn_in=g_w_ffn_in, w_ffn_out=g_w_ffn_out)
    late_a2a = _AllToAll([_blocks_from_full(late_grads[n], n in BIG_COL_SHARDED) for n in BIG_LATE])
    dq0, dk0, dv0, dg0, db0, recv_late = _gdn_scan_bwd("gdn_bwd0", q, k, v, gcum, beta, ss0, ts0, d_o, 0, nctx, ride=late_a2a)
    dq1, dk1, dv1, dg1, db1 = _gdn_scan_bwd("gdn_bwd1", q, k, v, gcum, beta, ss1, ts1, d_o, 1, nctx)
    dxs0, ddt0, dac0, dbm0, dcm0 = _ssd_scan_bwd("ssd_bwd0", xs, dtc, acc, bm, cm, hs0, d_y, 0, nctx)
    dxs1, ddt1, dac1, dbm1, dcm1 = _ssd_scan_bwd("ssd_bwd1", xs, dtc, acc, bm, cm, hs1, d_y, 1, nctx)
    row = lambda arr, wd: (arr, wd, 0, 0)
    d_qkv_raw, d_ab, d_gcw, d_gcb, d_alog16, d_dtb16 = _rowwise_bwd(
        "gdnprep_bwd", f_gdnprep, gp_rows, gp_pars,
        [[row(dq0, D), row(dq1, D)], [row(dk0, D), row(dk1, D)], [row(dv0, D), row(dv1, D)],
         [row(dg0, 128), row(dg1, 128)], [row(db0, 128), row(db1, 128)]],
        [(0, BF16), (1, BF16)], [0, 1, 2, 3], ntl, base=0)
    d_xbc_raw, d_dt, d_scw, d_scb, d_alog64, d_dtb64 = _rowwise_bwd(
        "ssmprep_bwd", f_ssmprep, sp_rows, sp_pars,
        [[row(dxs0, S_INNER), row(dxs1, S_INNER), (d_xs_post, S_INNER, 0, -1, True)], [row(dbm0, 512), row(dbm1, 512)],
         [row(dcm0, 512), row(dcm1, 512)], [row(ddt0, 128), row(ddt1, 128)], [row(dac0, 128), row(dac1, 128)]],
        [(0, BF16), (1, BF16)], [0, 1, 2, 3], ntl, base=0)
    ctx_zero = lambda wd: jnp.zeros((TM, wd), BF16)
    d_proj = jnp.concatenate([
        d_qkv_raw, d_xbc_raw, jnp.concatenate([ctx_zero(S_INNER), d_zs], axis=0),
        jnp.concatenate([ctx_zero(S_INNER), d_gate], axis=0), jnp.concatenate([ctx_zero(D), d_zg], axis=0), d_ab, d_dt], axis=1)
    g_w_cat = _mm("mm_gwcat", a, d_proj, "tn", BF16, tn=768, tk=2816)
    w_in_a2a = _AllToAll([_blocks_from_full(_w_uncat(g_w_cat), True)])
    d_a, recv_w_in = _mm("mm_da", d_proj, w_cat, "nt", BF16, tk=3840, ride=w_in_a2a)
    d_xa, d_n1w, d_sc1, d_sh1, d_csc1, d_csh1 = _rowwise_bwd(
        "pre_bwd", f_pre_thru, [(xa, D, 0, 0)], pre_pars, [[row(d_a, D)], [(d_xres, D, 0, -1, True)]],
        [(0, F32)], [0, 1, 2, 3, 4], ntl, base=0)
    zero4 = jnp.zeros((1, 4 * D), F32)
    d_mod = jnp.concatenate([
        jnp.concatenate([d_sh1, d_sc1, d_g1, d_sh2, d_sc2, d_g2], axis=1),
        jnp.concatenate([d_csh1, d_csc1, zero4], axis=1), jnp.zeros((14, 6 * D), F32)], axis=0)
    d_a16 = _mm("mm_da16", d_mod, full["ada_w"], "nt", F32)
    d_cvec = _rowwise_bwd("silu_c_bwd", f_silu_rows, [(cvec, D, 0, 0)], [], [[row(d_a16, D)]], [(0, F32)], [], 1, tm=16)[0]

    recv = dict(zip(BIG_LATE, recv_late), w_in=recv_w_in[0])
    big_un = {n: _reduce_adam("adam_" + n, recv[n], wts[n][0], mom1[n][0], mom2[n][0], BIG_ADAM_ROWS[n])
              for n in BIG if n != "ada_w"}

    small_g = dict(c_ctx=d_cvec[1], ada_b=d_mod[0] + d_mod[1], norm1_w=d_n1w, gdn_conv_b=d_gcb,
                   gdn_a_log=d_alog16[0, :2 * G_HEADS], gdn_dt_bias=d_dtb16[0, :2 * G_HEADS], gdn_norm_w=d_gnw,
                   ssm_conv_b=d_scb, ssm_a_log=d_alog64[0, :2 * S_HEADS], ssm_dt_bias=d_dtb64[0, :2 * S_HEADS],
                   ssm_d=d_ssd8[0, :S_HEADS], ssm_norm_w=d_snw, norm2_w=d_n2w, norm_f_w=d_nfw,
                   gdn_conv_w=d_gcw.reshape(3, XBC), ssm_conv_w=d_scw.reshape(3, XBC))
    small_names = SMALL + CONV
    factors = [a16[0].astype(F32), d_mod[0], d_mod[1]]
    sg_pack = _pack([small_g[n] for n in small_names] + factors, rows=56)
    recv_all = _exchange("ag_small_grads", _AllGather([sg_pack]))[0]
    recv_s = recv_all[:, :48]
    fac = recv_all[:, 42:55].reshape(N_DEV, -1)
    my_cols = lambda z: lax.dynamic_slice(z, (0, me_i * (6 * D // N_DEV)), (N_DEV, 6 * D // N_DEV))
    lhs = jnp.concatenate([fac[:, :D], jnp.broadcast_to(a16[1:2].astype(F32), (N_DEV, D))], axis=0)
    rhs = jnp.concatenate([my_cols(fac[:, D:7 * D]), my_cols(fac[:, 7 * D:])], axis=0)
    g_ada_w = _mm("mm_gwada", lhs, rhs, "tn", F32)
    big_un["ada_w"] = _reduce_adam("adam_ada_w", g_ada_w[None], wts["ada_w"][0], mom1["ada_w"][0], mom2["ada_w"][0],
                                   BIG_ADAM_ROWS["ada_w"])

    def placed(src, n):
        if n not in CONV:
            return src[n]
        return lax.dynamic_update_slice(jnp.zeros((3, XBC), F32), src[n][0], (0, me_i * CONV_SHARD))

    small_out = _reduce_adam("adam_small", recv_s, *[_pack([placed(src, n) for n in small_names], rows=48)
                                                     for src in (wts, mom1, mom2)], 48)
    small_shapes = [wts[n].shape if n in SMALL else (3, XBC) for n in small_names]
    small_un = [_unpack(buf, small_shapes) for buf in small_out]

    res = [{}, {}, {}, {}]
    for kind in range(4):
        for n in BIG:
            res[kind][n] = big_un[n][kind].reshape(wts[n].shape)
        for n, val in zip(small_names, small_un[kind]):
            if n in CONV:
                val = lax.dynamic_slice(val, (0, me_i * CONV_SHARD), (3, CONV_SHARD)).reshape(wts[n].shape)
            res[kind][n] = val
    loss = lax.psum(loss_acc[0, 0], ("x", "y", "c"))
    grad_x = d_xa[TM:].reshape(x.shape)
    return (loss, grad_x, *[res[0][n] for n in order], *[res[1][n] for n in order], *[res[2][n] for n in order],
            *[res[3][n] for n in order])
```

```python
import functools

import jax
import jax.numpy as jnp
from jax import lax
from jax.experimental import pallas as pl
from jax.experimental.pallas import tpu as pltpu

F32 = jnp.float32
BF16 = jnp.bfloat16
HI = lax.Precision.HIGHEST
HIGH = lax.Precision.HIGH
MESH = pl.DeviceIdType.MESH

D = 1024
CH = 64
TM = 256
EPS = 1e-6
NEG = -1e30
G_HEADS = 8
DK = 128
S_HEADS = 32
S_P = 64
S_GROUPS = 4
S_N = 128
S_INNER = 2048
XBC = 3072
D_FF = 2816
N_DEV = 8
PACK_W = 1024
VMEM_LIMIT = 56 * 1024 * 1024

ADAM_LR = 0.001
ADAM_B1 = 0.9
ADAM_B2 = 0.999
ADAM_EPS = 1e-08
ADAM_WD = 0.01
ADAM_STEP = 10

C_QKV, C_XBC, C_ZS, C_GATE, C_ZG, C_AB, C_DT, C_END = 0, 3072, 6144, 8192, 10240, 11264, 11392, 11520
O_QKV, O_ZG, O_AB, O_ZS, O_XBC, O_DT, O_GATE, O_END = 0, 3072, 4096, 4128, 6176, 9248, 9312, 11360


def _dot(a, b, prec=None):
    return jnp.dot(a, b, precision=prec, preferred_element_type=F32)


def _dot_nt(a, b, prec=None):
    return lax.dot_general(a, b, (((1,), (1,)), ((), ())), precision=prec, preferred_element_type=F32)


def _dot_tn(a, b, prec=None):
    return lax.dot_general(a, b, (((0,), (0,)), ((), ())), precision=prec, preferred_element_type=F32)


def _iota(shape, dim):
    return lax.broadcasted_iota(jnp.int32, shape, dim)


def _rms(x):
    return x * lax.rsqrt(jnp.mean(x * x, axis=-1, keepdims=True) + EPS)


def _l2n(x):
    return x * lax.rsqrt(jnp.sum(x * x, axis=-1, keepdims=True) + EPS)


def _silu(x):
    return x * jax.nn.sigmoid(x)


def _softplus(x):
    return jnp.maximum(x, 0.0) + jnp.log1p(jnp.exp(-jnp.abs(x)))


def _roll_rows(x, s):
    return pltpu.roll(x, s, 0)


def _up_raw(x, keep_up):
    return jnp.where(keep_up > 0.0, _roll_rows(x, 1), 0.0)


def _dn_raw(x, keep_dn):
    return jnp.where(keep_dn > 0.0, _roll_rows(x, x.shape[0] - 1), 0.0)


@jax.custom_vjp
def _shift_up(x, keep_up, keep_dn):
    return _up_raw(x, keep_up)


def _shift_up_fwd(x, keep_up, keep_dn):
    return _up_raw(x, keep_up), (keep_up, keep_dn)


def _shift_up_bwd(res, g):
    keep_up, keep_dn = res
    return _dn_raw(g, keep_dn), jnp.zeros_like(keep_up), jnp.zeros_like(keep_dn)


_shift_up.defvjp(_shift_up_fwd, _shift_up_bwd)


@jax.custom_vjp
def _shift_dn(x, keep_up, keep_dn):
    return _dn_raw(x, keep_dn)


def _shift_dn_fwd(x, keep_up, keep_dn):
    return _dn_raw(x, keep_dn), (keep_up, keep_dn)


def _shift_dn_bwd(res, g):
    keep_up, keep_dn = res
    return _up_raw(g, keep_up), jnp.zeros_like(keep_up), jnp.zeros_like(keep_dn)


_shift_dn.defvjp(_shift_dn_fwd, _shift_dn_bwd)


def _conv_keep(is_ctx, n):
    r = _iota((n, 1), 0)
    pos = jnp.where(is_ctx, r, r & (CH - 1))
    end = jnp.where(is_ctx, n - 1, CH - 1)
    return jnp.where(pos == 0, 0.0, 1.0).astype(F32), jnp.where(pos == end, 0.0, 1.0).astype(F32)


def _conv_silu(u, w3, b, keep_up, keep_dn):
    conv = b + _shift_up(u, keep_up, keep_dn) * w3[0] + u * w3[1] + _shift_dn(u, keep_up, keep_dn) * w3[2]
    return _silu(conv)


def _chunk_tri(n, rev):
    i = _iota((n, n), 0)
    j = _iota((n, n), 1)
    same = (i // CH) == (j // CH)
    seen = (i <= j) if rev else (i >= j)
    return jnp.where(same & seen, 1.0, 0.0).astype(F32)


def _expand_mat(rows, cols, per, base):
    r = _iota((rows, cols), 0)
    c = _iota((rows, cols), 1)
    return jnp.where(r == base + c // per, 1.0, 0.0).astype(F32)


def f_silu_rows(is_ctx, cvec):
    return (_silu(cvec).astype(BF16),)


def f_pre(is_ctx, x, n1w, sc, sh, csc, csh):
    sc_e = jnp.where(is_ctx, csc, sc)
    sh_e = jnp.where(is_ctx, csh, sh)
    a = _rms(x) * n1w * (1.0 + sc_e) + sh_e
    return (a.astype(BF16),)


def f_pre_thru(is_ctx, x, n1w, sc, sh, csc, csh):
    return f_pre(is_ctx, x, n1w, sc, sh, csc, csh)[0], x


def f_gdnprep(is_ctx, qkv_raw, ab_raw, cw, cb, alog, dtb):
    n = qkv_raw.shape[0]
    keep_up, keep_dn = _conv_keep(is_ctx, n)
    s = _conv_silu(qkv_raw, cw, cb, keep_up, keep_dn)
    qs, ks, vs = [], [], []
    for h in range(G_HEADS):
        qs.append(_l2n(s[:, h * DK:(h + 1) * DK]) * (DK ** -0.5))
        ks.append(_l2n(s[:, D + h * DK:D + (h + 1) * DK]))
    q = jnp.concatenate(qs, axis=1)
    k = jnp.concatenate(ks, axis=1)
    v = s[:, 2 * D:3 * D]
    lane = _iota(ab_raw.shape, 1)
    g = jnp.where(lane < 2 * G_HEADS, -jnp.exp(alog) * _softplus(ab_raw + dtb), 0.0)
    gcum = jnp.where(lane < G_HEADS, _dot(_chunk_tri(n, False), g, HI), _dot(_chunk_tri(n, True), g, HI))
    beta = jax.nn.sigmoid(ab_raw)
    return q, k, v, gcum, beta


def f_ssmprep(is_ctx, xbc_raw, dt_raw, cw, cb, alog, dtb):
    n = xbc_raw.shape[0]
    keep_up, keep_dn = _conv_keep(is_ctx, n)
    s = _conv_silu(xbc_raw, cw, cb, keep_up, keep_dn)
    xs = s[:, :S_INNER]
    bm = s[:, S_INNER:S_INNER + S_GROUPS * S_N]
    cm = s[:, S_INNER + S_GROUPS * S_N:]
    lane = _iota(dt_raw.shape, 1)
    dt = jnp.where(lane < 2 * S_HEADS, _softplus(dt_raw + dtb), 0.0)
    da = dt * (-jnp.exp(alog))
    acum = jnp.where(lane < S_HEADS, _dot(_chunk_tri(n, False), da, HI), _dot(_chunk_tri(n, True), da, HI))
    return xs, bm, cm, dt, acum


def f_post(is_ctx, o_f, o_b, zg, y_f, y_b, xs, zs, gnw, ssd8, snw):
    o = o_f + o_b
    ogs = []
    for h in range(G_HEADS):
        sl = slice(h * DK, (h + 1) * DK)
        ogs.append(_rms(o[:, sl]) * gnw * _silu(zg[:, sl]))
    og = jnp.concatenate(ogs, axis=1)
    row0 = jnp.where(_iota(ssd8.shape, 0) == 0, 1.0, 0.0).astype(F32)
    dexp = jnp.sum(_dot(ssd8 * row0, _expand_mat(128, S_INNER, S_P, 0), HI), axis=0, keepdims=True)
    y = (y_f + y_b + dexp * xs) * _silu(zs)
    gw = S_INNER // S_GROUPS
    ys = jnp.concatenate([_rms(y[:, i * gw:(i + 1) * gw]) * snw[:, i * gw:(i + 1) * gw] for i in range(S_GROUPS)], axis=1)
    return og.astype(BF16), ys.astype(BF16)


def f_merge(is_ctx, gate, pg, ps):
    m = jax.nn.sigmoid(gate[:, :D]) * pg + jax.nn.sigmoid(gate[:, D:]) * ps
    return (m.astype(BF16),)


def f_res1(is_ctx, x, mix, g1, n2w, sc2, sh2):
    h1 = x + g1 * mix
    f = _rms(h1) * n2w * (1.0 + sc2) + sh2
    return h1, f.astype(BF16)


def f_act(is_ctx, u):
    return ((_silu(u[:, :D_FF]) * u[:, D_FF:]).astype(BF16),)


def f_final(is_ctx, h1, ff, tgt, g2, nfw):
    h2 = h1 + g2 * ff
    y = _rms(h2) * nfw
    err = y - tgt
    return (0.5 * jnp.sum(jnp.mean(err * err, axis=-1, keepdims=True), axis=0, keepdims=True),)


def _each(fn, *lists):
    return [fn(*args) for args in zip(*lists)]


def _tri_inverse_all(mats):
    n = mats[0].shape[0]
    eye = jnp.where(_iota((n, n), 0) == _iota((n, n), 1), 1.0, 0.0).astype(F32)
    t = [eye - a for a in mats]
    p = [_dot(a, a, HIGH) for a in mats]
    for r in range(5):
        t = _each(lambda t_, p_: t_ + _dot(t_, p_, HIGH), t, p)
        if r < 4:
            p = [_dot(p_, p_, HIGH) for p_ in p]
    return t


@jax.custom_vjp
def _inverse_given(a, t):
    return t


def _inverse_given_fwd(a, t):
    return t, t


def _inverse_given_bwd(t, g):
    return -_dot_nt(_dot_tn(t, g, HIGH), t, HIGH), jnp.zeros_like(t)


_inverse_given.defvjp(_inverse_given_fwd, _inverse_given_bwd)


def gdn_local(qs, ks, vs, gcs, grs, bcs, rev, t_known=None):
    c = qs[0].shape[0]
    ii = _iota((c, c), 0)
    jj = _iota((c, c), 1)
    incl = (ii <= jj) if rev else (ii >= jj)
    strict = (ii < jj) if rev else (ii > jj)
    decay = _each(lambda gc, gr: jnp.exp(jnp.where(incl, gc - gr, NEG)), gcs, grs)
    kb = _each(lambda k, bc: k * bc, ks, bcs)
    a = _each(lambda kb_, k, dc: jnp.where(strict, _dot_nt(kb_, k) * dc, 0.0), kb, ks, decay)
    t = _tri_inverse_all(a) if t_known is None else _each(_inverse_given, a, t_known)
    eg = [jnp.exp(gc) for gc in gcs]
    rhs = _each(lambda kb_, eg_, v, bc: jnp.concatenate([kb_ * eg_, v * bc], axis=1), kb, eg, vs, bcs)
    wu = _each(lambda t_, r: _dot(t_, r, HIGH), t, rhs)
    lhs = _each(lambda wu_, q, eg_: jnp.concatenate([wu_[:, :DK], q * eg_], axis=0), wu, qs, eg)
    attn = _each(lambda q, k, dc: _dot_nt(q, k) * dc, qs, ks, decay)
    return wu, attn, lhs, t


def gdn_state(ss, wu, attn, lhs, ks, gcs, rev):
    c = ks[0].shape[0]
    is_last = _iota((c, 1), 0) == (0 if rev else c - 1)
    ws = _each(_dot, lhs, ss)
    v_new = _each(lambda wu_, ws_: wu_[:, DK:] - ws_[:c], wu, ws)
    o = _each(lambda ws_, at, vn: ws_[c:] + _dot(at, vn), ws, attn, v_new)
    gtot = [jnp.sum(jnp.where(is_last, gc, 0.0), axis=0, keepdims=True) for gc in gcs]
    s_new = _each(lambda s, k, gc, gt_, vn: s * jnp.exp(gt_) + _dot_tn(k * jnp.exp(gt_ - gc), vn), ss, ks, gcs, gtot, v_new)
    return s_new, o


def gdn_chunk(ss, qs, ks, vs, gcs, grs, bcs, rev, t_known=None):
    wu, attn, lhs, t = gdn_local(qs, ks, vs, gcs, grs, bcs, rev, t_known)
    s_new, o = gdn_state(ss, wu, attn, lhs, ks, gcs, rev)
    return s_new, o, t


def ssd_pick(dt0s, dt1s, ac0s, ac1s, ar0s, ar1s):
    lo = _iota((CH, 128), 1) < S_P
    pick = lambda u0, u1: jnp.where(lo, u0, u1)
    return _each(pick, dt0s, dt1s), _each(pick, ac0s, ac1s), _each(pick, ar0s, ar1s)


def ssd_chunk(hts, xs, dts, acs, acr, bgs, cgs, rev):
    c = xs[0].shape[0]
    npair = len(xs)
    grp = [p * len(bgs) // npair for p in range(npair)]
    lane = _iota((c, 128), 1)
    ii = _iota((c, 128), 0)
    jl = lane & (S_P - 1)
    lo = lane < S_P
    seen = (ii <= jl) if rev else (ii >= jl)
    last = 0 if rev else c - 1
    split = lambda z: jnp.concatenate([jnp.where(lo, z, 0.0), jnp.where(lo, 0.0, z)], axis=0)
    cb = _each(lambda bg, cg: _dot_nt(cg, jnp.concatenate([bg, bg], axis=0)), bgs, cgs)
    seg = _each(lambda ac, ar: jnp.exp(jnp.where(seen, ac - ar, NEG)), acs, acr)
    xdt = _each(lambda x, dt: x * dt, xs, dts)
    ydiag = [_dot(cb[grp[p]] * seg[p], split(xdt[p])) for p in range(npair)]
    yoff = [_dot(cgs[grp[p]], hts[p]) * jnp.exp(acs[p]) for p in range(npair)]
    atot = [jnp.sum(jnp.where(ii == last, ac, 0.0), axis=0, keepdims=True) for ac in acs]
    h_new = [hts[p] * jnp.exp(atot[p]) + _dot_tn(bgs[grp[p]], xdt[p] * jnp.exp(atot[p] - acs[p])) for p in range(npair)]
    return h_new, _each(lambda a_, b_: a_ + b_, ydiag, yoff)


def f_adamw(w, g, m, v):
    m = ADAM_B1 * m + (1.0 - ADAM_B1) * g
    v = ADAM_B2 * v + (1.0 - ADAM_B2) * jnp.square(g)
    m_hat = m / (1.0 - ADAM_B1 ** ADAM_STEP)
    v_hat = v / (1.0 - ADAM_B2 ** ADAM_STEP)
    delta = -ADAM_LR * (m_hat / (jnp.sqrt(v_hat) + ADAM_EPS) + ADAM_WD * w)
    return delta, m, v


def _cparams(sem):
    return pltpu.CompilerParams(dimension_semantics=sem, vmem_limit_bytes=VMEM_LIMIT)


def _pick(n, target):
    if n <= target:
        return n
    best = None
    for t in range(128, target + 1, 128):
        if n % t == 0:
            best = t
    assert best is not None, (n, target)
    return best


def _row_spec(tm, width, colblk, rowoff):
    return pl.BlockSpec((tm, width), lambda i: (i + rowoff, colblk))


def _par_spec(shape):
    nd = len(shape)
    return pl.BlockSpec(tuple(shape), lambda i: (0,) * nd)


def _rowwise(name, fn, rows, pars, outs, ntiles, base=1, tm=TM):
    nr, npar = len(rows), len(pars)

    def body(*refs):
        is_ctx = (pl.program_id(0) + base) == 0
        res = fn(is_ctx, *[r[...] for r in refs[:nr]], *[p[...] for p in refs[nr:nr + npar]])
        for o_ref, r in zip(refs[nr + npar:], res):
            o_ref[...] = r.astype(o_ref.dtype)

    return pl.pallas_call(
        body, grid=(ntiles,), name=name,
        in_specs=[_row_spec(tm, wd, cb, ro) for (_, wd, cb, ro) in rows] + [_par_spec(p.shape) for p in pars],
        out_specs=[_row_spec(tm, wd, 0, 0) for (wd, _) in outs],
        out_shape=[jax.ShapeDtypeStruct((ntiles * tm, wd), dt) for (wd, dt) in outs],
        compiler_params=_cparams(("arbitrary",)),
    )(*[r[0] for r in rows], *pars)


def _ct_spec(tm, desc):
    _, wd, cb, ro = desc[:4]
    if len(desc) > 4 and desc[4]:
        return pl.BlockSpec((tm, wd), lambda i: (jnp.maximum(i + ro, 0), cb))
    return _row_spec(tm, wd, cb, ro)


def _rowwise_bwd(name, fn, rows, pars, cts, drows, dpars, ntiles, base=1, loss_out=False, tm=TM):
    nr, npar = len(rows), len(pars)
    ct_rows = [d for ct in cts if isinstance(ct, list) for d in ct]
    nct = len(ct_rows)

    def body(*refs):
        i = pl.program_id(0)
        is_ctx = (i + base) == 0
        rows_v = [r[...] for r in refs[:nr]]
        pars_v = [p[...] for p in refs[nr:nr + npar]]
        ct_refs = list(refs[nr + npar:nr + npar + nct])
        out_refs = list(refs[nr + npar + nct:])
        outs, vjp = jax.vjp(lambda rv, pv: fn(is_ctx, *rv, *pv), rows_v, pars_v)

        def ct_value(desc):
            val = ct_refs.pop(0)[...].astype(F32)
            if len(desc) > 4 and desc[4]:
                val = jnp.where(is_ctx, 0.0, val)
            return val

        ct_vals = []
        for o, ct in zip(outs, cts):
            if ct is None:
                ct_vals.append(jnp.zeros_like(o))
            elif isinstance(ct, str):
                ct_vals.append(jnp.ones_like(o))
            else:
                acc = ct_value(ct[0])
                for desc in ct[1:]:
                    acc = acc + ct_value(desc)
                ct_vals.append(acc.astype(o.dtype))
        d_rows, d_pars = vjp(tuple(ct_vals))
        for (ri, _), o_ref in zip(drows, out_refs[:len(drows)]):
            o_ref[...] = d_rows[ri].astype(o_ref.dtype)
        acc_refs = out_refs[len(drows):]
        acc_vals = [d_pars[pi] for pi in dpars]
        if loss_out:
            acc_vals.append(jnp.broadcast_to(outs[0], (8, 128)))

        @pl.when(i == 0)
        def _():
            for o_ref, val in zip(acc_refs, acc_vals):
                o_ref[...] = val

        @pl.when(i > 0)
        def _():
            for o_ref, val in zip(acc_refs, acc_vals):
                o_ref[...] += val

    acc_shapes = [pars[pi].shape for pi in dpars] + ([(8, 128)] if loss_out else [])
    return pl.pallas_call(
        body, grid=(ntiles,), name=name,
        in_specs=[_row_spec(tm, wd, cb, ro) for (_, wd, cb, ro) in rows] + [_par_spec(p.shape) for p in pars]
        + [_ct_spec(tm, d) for d in ct_rows],
        out_specs=[_row_spec(tm, rows[ri][1], 0, 0) for (ri, _) in drows] + [_par_spec(s) for s in acc_shapes],
        out_shape=[jax.ShapeDtypeStruct((ntiles * tm, rows[ri][1]), dt) for (ri, dt) in drows]
        + [jax.ShapeDtypeStruct(tuple(s), F32) for s in acc_shapes],
        compiler_params=_cparams(("arbitrary",)),
    )(*[r[0] for r in rows], *pars, *[r[0] for r in ct_rows])


def _mm(name, a, b, mode, out_dtype, tm=1024, tn=1024, tk=1024, ride=None):
    if mode == "nn":
        (m, kd), (_, n) = a.shape, b.shape
    elif mode == "nt":
        (m, kd), (n, _) = a.shape, b.shape
    else:
        (kd, m), (_, n) = a.shape, b.shape
    tm, tn, tk = _pick(m, tm), _pick(n, tn), _pick(kd, tk)
    nk = kd // tk
    a_spec = {"nn": pl.BlockSpec((tm, tk), lambda i, j, k: (i, k)), "nt": pl.BlockSpec((tm, tk), lambda i, j, k: (i, k)),
              "tn": pl.BlockSpec((tk, tm), lambda i, j, k: (k, i))}[mode]
    b_spec = {"nn": pl.BlockSpec((tk, tn), lambda i, j, k: (k, j)), "nt": pl.BlockSpec((tn, tk), lambda i, j, k: (j, k)),
              "tn": pl.BlockSpec((tk, tn), lambda i, j, k: (k, j))}[mode]
    dot = {"nn": _dot, "nt": _dot_nt, "tn": _dot_tn}[mode]

    if nk == 1:
        def body(a_ref, b_ref, o_ref):
            o_ref[...] = dot(a_ref[...].astype(BF16), b_ref[...].astype(BF16)).astype(o_ref.dtype)
    else:
        def body(a_ref, b_ref, o_ref, acc_ref):
            k = pl.program_id(2)
            part = dot(a_ref[...].astype(BF16), b_ref[...].astype(BF16))

            @pl.when(k == 0)
            def _():
                acc_ref[...] = part

            @pl.when((k > 0) & (k < nk - 1))
            def _():
                acc_ref[...] += part

            @pl.when(k == nk - 1)
            def _():
                o_ref[...] = (acc_ref[...] + part).astype(o_ref.dtype)

    grid = (m // tm, n // tn, nk)
    at = lambda pos: functools.reduce(jnp.logical_and, [pl.program_id(ax) == pos(g) for ax, g in enumerate(grid)])
    r_in, r_out, r_shapes, r_scr, r_ops = _ride_args(ride)
    res = pl.pallas_call(
        _ride(body, ride, 2, 1, lambda: at(lambda g: 0), lambda: at(lambda g: g - 1)), grid=grid, name=name,
        in_specs=[a_spec, b_spec] + r_in,
        out_specs=[pl.BlockSpec((tm, tn), lambda i, j, k: (i, j))] + r_out,
        out_shape=[jax.ShapeDtypeStruct((m, n), out_dtype)] + r_shapes,
        scratch_shapes=([] if nk == 1 else [pltpu.VMEM((tm, tn), F32)]) + r_scr,
        compiler_params=_cparams(("arbitrary", "arbitrary", "arbitrary")),
    )(a, b, *r_ops)
    return res[0] if ride is None else (res[0], res[1:])


def _chunk_index(i, nch, nctx, rev):
    if not rev:
        return i
    return jnp.where(i < nctx, nctx - 1 - i, nch + nctx - 1 - i)


def _gdn_cols(d):
    return [d * G_HEADS + h for h in range(G_HEADS)], [2 * G_HEADS + d * G_HEADS + h for h in range(G_HEADS)]


def _gdn_operands(q_ref, k_ref, v_ref, g_ref, b_ref, d, rows=slice(None)):
    cols_g, cols_b = _gdn_cols(d)
    sls = [slice(h * DK, (h + 1) * DK) for h in range(G_HEADS)]
    gt, bt = g_ref[rows, :], b_ref[rows, :]
    gtt = gt.T
    qs = [q_ref[rows, sl] for sl in sls]
    ks = [k_ref[rows, sl] for sl in sls]
    vs = [v_ref[rows, sl] for sl in sls]
    gcs = [gt[:, cg:cg + 1] for cg in cols_g]
    grs = [gtt[cg:cg + 1, :] for cg in cols_g]
    bcs = [bt[:, cb:cb + 1] for cb in cols_b]
    return sls, qs, ks, vs, gcs, grs, bcs


GDN_FWD_CHUNKS = 2


def _gdn_scan_fwd(name, q, k, v, gcum, beta, d, nctx, ride=None):
    t = q.shape[0]
    nch = t // CH
    per = GDN_FWD_CHUNKS
    nst = nch // per
    rev = d == 1
    bix = lambda i: _chunk_index(i, nst, nctx // per, rev)
    full = pl.BlockSpec((per * CH, D), lambda i: (bix(i), 0))
    wide = pl.BlockSpec((per * CH, 128), lambda i: (bix(i), 0))
    order = list(range(per - 1, -1, -1)) if rev else list(range(per))

    def body(q_ref, k_ref, v_ref, g_ref, b_ref, o_ref, ss_ref, ts_ref, s_scr):
        @pl.when(pl.program_id(0) == 0)
        def _():
            s_scr[...] = jnp.zeros(s_scr.shape, F32)

        ops = [_gdn_operands(q_ref, k_ref, v_ref, g_ref, b_ref, d, slice(c * CH, (c + 1) * CH)) for c in order]
        sls = ops[0][0]
        cat = [sum((o[i] for o in ops), []) for i in range(1, 7)]
        wu, attn, lhs, tinv = gdn_local(*cat, rev)
        ss = [s_scr[h] for h in range(G_HEADS)]
        for n, c in enumerate(order):
            sl = slice(n * G_HEADS, (n + 1) * G_HEADS)
            s_new, o = gdn_state(ss, wu[sl], attn[sl], lhs[sl], cat[1][sl], cat[3][sl], rev)
            for h in range(G_HEADS):
                ss_ref[n, h] = ss[h]
                ts_ref[n, h] = tinv[n * G_HEADS + h]
                o_ref[c * CH:(c + 1) * CH, sls[h]] = o[h]
            ss = s_new
        for h in range(G_HEADS):
            s_scr[h] = ss[h]

    r_in, r_out, r_shapes, r_scr, r_ops = _ride_args(ride)
    res = pl.pallas_call(
        _ride(body, ride, 5, 3, lambda: pl.program_id(0) == 0, lambda: pl.program_id(0) == nst - 1), grid=(nst,), name=name,
        in_specs=[full, full, full, wide, wide] + r_in,
        out_specs=[full, pl.BlockSpec((per, G_HEADS, DK, DK), lambda i: (i, 0, 0, 0)),
                   pl.BlockSpec((per, G_HEADS, CH, CH), lambda i: (i, 0, 0, 0))] + r_out,
        out_shape=[jax.ShapeDtypeStruct((t, D), F32), jax.ShapeDtypeStruct((nch, G_HEADS, DK, DK), F32),
                   jax.ShapeDtypeStruct((nch, G_HEADS, CH, CH), F32)] + r_shapes,
        scratch_shapes=[pltpu.VMEM((G_HEADS, DK, DK), F32)] + r_scr,
        compiler_params=_cparams(("arbitrary",)),
    )(q, k, v, gcum, beta, *r_ops)
    return res if ride is None else (*res[:3], res[3:])


def _gdn_scan_bwd(name, q, k, v, gcum, beta, ssave, tsave, do, d, nctx, ride=None):
    t = q.shape[0]
    nch = t // CH
    rev = d == 1
    cix = lambda ib: _chunk_index(nch - 1 - ib, nch, nctx, rev)
    full = pl.BlockSpec((CH, D), lambda ib: (cix(ib), 0))
    wide = pl.BlockSpec((CH, 128), lambda ib: (cix(ib), 0))
    do_spec = pl.BlockSpec((CH, D), lambda ib: (jnp.maximum(cix(ib), nctx) - nctx, 0))

    def body(q_ref, k_ref, v_ref, g_ref, b_ref, ss_ref, ts_ref, do_ref, dq_ref, dk_ref, dv_ref, dg_ref, db_ref, ds_scr):
        ib = pl.program_id(0)

        @pl.when(ib == 0)
        def _():
            ds_scr[...] = jnp.zeros(ds_scr.shape, F32)

        sls, qs, ks, vs, gcs, grs, bcs = _gdn_operands(q_ref, k_ref, v_ref, g_ref, b_ref, d)
        t_known = [ts_ref[0, h] for h in range(G_HEADS)]
        cols_g, cols_b = _gdn_cols(d)
        is_lat = cix(ib) >= nctx
        ss = [ss_ref[0, h] for h in range(G_HEADS)]
        do_v = [jnp.where(is_lat, do_ref[:, sl], 0.0) for sl in sls]
        ds_in = [ds_scr[h] for h in range(G_HEADS)]
        _, vjp = jax.vjp(lambda *a: gdn_chunk(*a, rev, t_known)[:2], ss, qs, ks, vs, gcs, grs, bcs)
        ds, dq, dk, dv, dgc, dgr, dbc = vjp((ds_in, do_v))
        lane = _iota((CH, 128), 1)
        sub = _iota((128, CH), 0)
        dg = jnp.zeros((CH, 128), F32)
        dgt = jnp.zeros((128, CH), F32)
        db = jnp.zeros((CH, 128), F32)
        for h in range(G_HEADS):
            ds_scr[h] = ds[h]
            dq_ref[:, sls[h]] = dq[h]
            dk_ref[:, sls[h]] = dk[h]
            dv_ref[:, sls[h]] = dv[h]
            dg = dg + jnp.where(lane == cols_g[h], dgc[h], 0.0)
            dgt = dgt + jnp.where(sub == cols_g[h], dgr[h], 0.0)
            db = db + jnp.where(lane == cols_b[h], dbc[h], 0.0)
        dg_ref[...] = dg + dgt.T
        db_ref[...] = db

    r_in, r_out, r_shapes, r_scr, r_ops = _ride_args(ride)
    res = pl.pallas_call(
        _ride(body, ride, 8, 5, lambda: pl.program_id(0) == 0, lambda: pl.program_id(0) == nch - 1), grid=(nch,), name=name,
        in_specs=[full, full, full, wide, wide,
                  pl.BlockSpec((1, G_HEADS, DK, DK), lambda ib: (nch - 1 - ib, 0, 0, 0)),
                  pl.BlockSpec((1, G_HEADS, CH, CH), lambda ib: (nch - 1 - ib, 0, 0, 0)), do_spec] + r_in,
        out_specs=[full, full, full, wide, wide] + r_out,
        out_shape=[jax.ShapeDtypeStruct((t, D), F32)] * 3 + [jax.ShapeDtypeStruct((t, 128), F32)] * 2 + r_shapes,
        scratch_shapes=[pltpu.VMEM((G_HEADS, DK, DK), F32)] + r_scr,
        compiler_params=_cparams(("arbitrary",)),
    )(q, k, v, gcum, beta, ssave, tsave, do, *r_ops)
    return res if ride is None else (*res[:5], res[5:])


N_PAIRS = S_HEADS // 2


def _ssd_operands(x_ref, dt_ref, ac_ref, b_ref, c_ref, d):
    sls = [slice(p * 128, (p + 1) * 128) for p in range(N_PAIRS)]
    gsl = [slice(g * S_N, (g + 1) * S_N) for g in range(S_GROUPS)]
    cols = [d * S_HEADS + h for h in range(S_HEADS)]
    dtc, acc = dt_ref[...], ac_ref[...]
    act = jnp.concatenate([acc, acc], axis=0).T
    col = lambda z, cc: z[:, cc:cc + 1]
    dts, acs, acr = ssd_pick(
        [col(dtc, cols[2 * p]) for p in range(N_PAIRS)], [col(dtc, cols[2 * p + 1]) for p in range(N_PAIRS)],
        [col(acc, cols[2 * p]) for p in range(N_PAIRS)], [col(acc, cols[2 * p + 1]) for p in range(N_PAIRS)],
        [act[cols[2 * p]:cols[2 * p] + 1, :] for p in range(N_PAIRS)],
        [act[cols[2 * p + 1]:cols[2 * p + 1] + 1, :] for p in range(N_PAIRS)])
    ops = ([x_ref[:, sl] for sl in sls], dts, acs, acr, [b_ref[:, gs] for gs in gsl], [c_ref[:, gs] for gs in gsl])
    return sls, gsl, cols, ops


def _ssd_scan_fwd(name, xs, dtc, acc, bm, cm, d, nctx):
    t = xs.shape[0]
    nch = t // CH
    rev = d == 1
    cix = lambda i: _chunk_index(i, nch, nctx, rev)
    inner = pl.BlockSpec((CH, S_INNER), lambda i: (cix(i), 0))
    wide = pl.BlockSpec((CH, 128), lambda i: (cix(i), 0))
    grp = pl.BlockSpec((CH, S_GROUPS * S_N), lambda i: (cix(i), 0))

    def body(x_ref, dt_ref, ac_ref, b_ref, c_ref, y_ref, hs_ref, h_scr):
        @pl.when(pl.program_id(0) == 0)
        def _():
            h_scr[...] = jnp.zeros(h_scr.shape, F32)

        sls, _, _, ops = _ssd_operands(x_ref, dt_ref, ac_ref, b_ref, c_ref, d)
        hts = [h_scr[p] for p in range(N_PAIRS)]
        h_new, y = ssd_chunk(hts, *ops, rev)
        for p in range(N_PAIRS):
            hs_ref[0, p] = hts[p]
            h_scr[p] = h_new[p]
            y_ref[:, sls[p]] = y[p]

    return pl.pallas_call(
        body, grid=(nch,), name=name,
        in_specs=[inner, wide, wide, grp, grp],
        out_specs=[inner, pl.BlockSpec((1, N_PAIRS, S_N, 128), lambda i: (i, 0, 0, 0))],
        out_shape=[jax.ShapeDtypeStruct((t, S_INNER), F32), jax.ShapeDtypeStruct((nch, N_PAIRS, S_N, 128), F32)],
        scratch_shapes=[pltpu.VMEM((N_PAIRS, S_N, 128), F32)],
        compiler_params=_cparams(("arbitrary",)),
    )(xs, dtc, acc, bm, cm)


def _ssd_scan_bwd(name, xs, dtc, acc, bm, cm, hsave, dy, d, nctx):
    t = xs.shape[0]
    nch = t // CH
    rev = d == 1
    cix = lambda ib: _chunk_index(nch - 1 - ib, nch, nctx, rev)
    inner = pl.BlockSpec((CH, S_INNER), lambda ib: (cix(ib), 0))
    wide = pl.BlockSpec((CH, 128), lambda ib: (cix(ib), 0))
    grp = pl.BlockSpec((CH, S_GROUPS * S_N), lambda ib: (cix(ib), 0))
    dy_spec = pl.BlockSpec((CH, S_INNER), lambda ib: (jnp.maximum(cix(ib), nctx) - nctx, 0))

    def body(x_ref, dt_ref, ac_ref, b_ref, c_ref, hs_ref, dy_ref, dx_ref, ddt_ref, dac_ref, db_ref, dc_ref, dh_scr):
        ib = pl.program_id(0)

        @pl.when(ib == 0)
        def _():
            dh_scr[...] = jnp.zeros(dh_scr.shape, F32)

        sls, gsl, cols, ops = _ssd_operands(x_ref, dt_ref, ac_ref, b_ref, c_ref, d)
        is_lat = cix(ib) >= nctx
        hts = [hs_ref[0, p] for p in range(N_PAIRS)]
        dy_v = [jnp.where(is_lat, dy_ref[:, sl], 0.0) for sl in sls]
        dh_in = [dh_scr[p] for p in range(N_PAIRS)]
        _, vjp = jax.vjp(lambda *a: ssd_chunk(*a, rev), hts, *ops)
        dh, dx, ddts, dacs, dacr, db, dc = vjp((dh_in, dy_v))
        for p in range(N_PAIRS):
            dh_scr[p] = dh[p]
            dx_ref[:, sls[p]] = dx[p]
        r = _iota((S_INNER, 128), 0)
        e_t = jnp.where(_iota((S_INNER, 128), 1) == d * S_HEADS + r // S_P, 1.0, 0.0).astype(F32)
        ddt_ref[...] = _dot(jnp.concatenate(ddts, axis=1), e_t, HIGH)
        dac_cols = _dot(jnp.concatenate(dacs, axis=1), e_t, HIGH)
        sub = _iota((128, 128), 0)
        lane = _iota((128, 128), 1)
        m = jnp.zeros((128, 128), F32)
        for p in range(N_PAIRS):
            m = m + jnp.where(sub == p, jnp.sum(dacr[p], axis=0, keepdims=True), 0.0)
        mt = m.T
        s0 = jnp.where(lane == d * S_HEADS + 2 * sub, 1.0, 0.0).astype(F32)
        s1 = jnp.where(lane == d * S_HEADS + 2 * sub + 1, 1.0, 0.0).astype(F32)
        dac_ref[...] = dac_cols + _dot(mt[:CH], s0, HIGH) + _dot(mt[CH:], s1, HIGH)
        for g in range(S_GROUPS):
            db_ref[:, gsl[g]] = db[g]
            dc_ref[:, gsl[g]] = dc[g]

    return pl.pallas_call(
        body, grid=(nch,), name=name,
        in_specs=[inner, wide, wide, grp, grp,
                  pl.BlockSpec((1, N_PAIRS, S_N, 128), lambda ib: (nch - 1 - ib, 0, 0, 0)), dy_spec],
        out_specs=[inner, wide, wide, grp, grp],
        out_shape=[jax.ShapeDtypeStruct((t, S_INNER), F32)] + [jax.ShapeDtypeStruct((t, 128), F32)] * 2
        + [jax.ShapeDtypeStruct((t, S_GROUPS * S_N), F32)] * 2,
        scratch_shapes=[pltpu.VMEM((N_PAIRS, S_N, 128), F32)],
        compiler_params=_cparams(("arbitrary",)),
    )(xs, dtc, acc, bm, cm, hsave, dy)


def _mesh_pos():
    return lax.axis_index("x"), lax.axis_index("y"), lax.axis_index("c")


def _hbm_specs(n):
    return [pl.BlockSpec(memory_space=pl.ANY)] * n


def _sem_shapes(nw):
    return [pltpu.SemaphoreType.DMA((nw, 7)), pltpu.SemaphoreType.DMA((nw, 7)), pltpu.SemaphoreType.DMA((nw,))]


class _AllGather:
    def __init__(self, shards):
        self.arrays = list(shards)
        self.out_shapes = [jax.ShapeDtypeStruct((N_DEV,) + xs.shape, xs.dtype) for xs in shards]

    def _parts(self, x_refs, out_refs, sems):
        send_sems, recv_sems, local_sems = sems
        nw = len(x_refs)
        x, y, c = _mesh_pos()
        me, sibling = (x, y, c), (x, y, 1 - c)
        chips = [(1 - x, y), (x, 1 - y), (1 - x, 1 - y)]

        def slot(w, px, py, pc):
            return out_refs[w].at[4 * px + 2 * py + pc]

        def copy(w, k, block, to, src=None):
            return pltpu.make_async_remote_copy(
                src_ref=slot(w, *block) if src is None else src, dst_ref=slot(w, *block),
                send_sem=send_sems.at[w, k], recv_sem=recv_sems.at[w, k], device_id=to, device_id_type=MESH)

        mine = [pltpu.make_async_copy(x_refs[w], slot(w, *me), local_sems.at[w]) for w in range(nw)]
        first = []
        for w in range(nw):
            first.append(copy(w, 0, me, sibling, src=x_refs[w]))
            first += [copy(w, 1 + j, me, (*chip, c), src=x_refs[w]) for j, chip in enumerate(chips)]
        return nw, me, sibling, chips, c, copy, mine, first

    def start(self, x_refs, out_refs, sems):
        _, _, _, _, _, _, mine, first = self._parts(x_refs, out_refs, sems)
        for cp in mine + first:
            cp.start()

    def finish(self, x_refs, out_refs, sems):
        nw, me, sibling, chips, c, copy, mine, first = self._parts(x_refs, out_refs, sems)
        passed = []
        for j, chip in enumerate(chips):
            for w in range(nw):
                copy(w, 1 + j, (*chip, c), me).wait_recv()
                fwd = copy(w, 4 + j, (*chip, c), sibling)
                fwd.start()
                passed.append(fwd)
        for w in range(nw):
            copy(w, 0, sibling, me).wait_recv()
            for j, chip in enumerate(chips):
                copy(w, 4 + j, (*chip, 1 - c), me).wait_recv()
        for cp in first + passed:
            cp.wait_send()
        for cp in mine:
            cp.wait()


class _AllToAll:
    def __init__(self, blocks):
        self.arrays = list(blocks)
        self.out_shapes = [jax.ShapeDtypeStruct(g.shape, g.dtype) for g in blocks]

    def _parts(self, g_refs, out_refs, sems):
        send_sems, recv_sems, local_sems = sems
        nw = len(g_refs)
        x, y, c = _mesh_pos()
        me_i = 4 * x + 2 * y + c
        mine = [pltpu.make_async_copy(g_refs[w].at[me_i], out_refs[w].at[me_i], local_sems.at[w]) for w in range(nw)]
        cps = []
        for k in range(1, N_DEV):
            px = 1 - x if (k >> 2) & 1 else x
            py = 1 - y if (k >> 1) & 1 else y
            pc = 1 - c if k & 1 else c
            for w in range(nw):
                cps.append(pltpu.make_async_remote_copy(
                    src_ref=g_refs[w].at[4 * px + 2 * py + pc], dst_ref=out_refs[w].at[me_i],
                    send_sem=send_sems.at[w, k - 1], recv_sem=recv_sems.at[w, k - 1],
                    device_id=(px, py, pc), device_id_type=MESH))
        return mine, cps

    def start(self, g_refs, out_refs, sems):
        mine, cps = self._parts(g_refs, out_refs, sems)
        for cp in mine + cps:
            cp.start()

    def finish(self, g_refs, out_refs, sems):
        mine, cps = self._parts(g_refs, out_refs, sems)
        for cp in cps + mine:
            cp.wait()


def _exchange(name, ex):
    nw = len(ex.arrays)

    def body(*refs):
        ins, outs, sems = refs[:nw], refs[nw:2 * nw], refs[2 * nw:]
        ex.start(ins, outs, sems)
        ex.finish(ins, outs, sems)

    return pl.pallas_call(body, name=name, out_shape=ex.out_shapes, in_specs=_hbm_specs(nw), out_specs=_hbm_specs(nw),
                          scratch_shapes=_sem_shapes(nw))(*ex.arrays)


def _ride(body, ex, n_in, n_out, is_first, is_last):
    if ex is None:
        return body
    nw = len(ex.arrays)

    def riding(*refs):
        ins, ex_in = refs[:n_in], refs[n_in:n_in + nw]
        outs = refs[n_in + nw:n_in + nw + n_out]
        ex_out = refs[n_in + nw + n_out:n_in + 2 * nw + n_out]
        rest = refs[n_in + 2 * nw + n_out:]
        scratch, sems = rest[:len(rest) - 3], rest[len(rest) - 3:]

        @pl.when(is_first())
        def _():
            ex.start(ex_in, ex_out, sems)

        body(*ins, *outs, *scratch)

        @pl.when(is_last())
        def _():
            ex.finish(ex_in, ex_out, sems)

    return riding


def _ride_args(ex):
    if ex is None:
        return [], [], [], [], []
    nw = len(ex.arrays)
    return _hbm_specs(nw), _hbm_specs(nw), list(ex.out_shapes), _sem_shapes(nw), list(ex.arrays)


def _reduce_adam(name, recv, w, m, v, tm):
    rows, width = w.shape
    nslot = recv.shape[0]

    def body(recv_ref, w_ref, m_ref, v_ref, g_ref, d_ref, m2_ref, v2_ref):
        g = recv_ref[0].astype(F32)
        for s in range(1, nslot):
            g = g + recv_ref[s].astype(F32)
        delta, m2, v2 = f_adamw(w_ref[...], g, m_ref[...], v_ref[...])
        g_ref[...] = g
        d_ref[...] = delta
        m2_ref[...] = m2
        v2_ref[...] = v2

    row = pl.BlockSpec((tm, width), lambda i: (i, 0))
    return pl.pallas_call(
        body, grid=(rows // tm,), name=name,
        in_specs=[pl.BlockSpec((nslot, tm, width), lambda i: (0, i, 0)), row, row, row],
        out_specs=[row] * 4,
        out_shape=[jax.ShapeDtypeStruct((rows, width), F32)] * 4,
        compiler_params=_cparams(("arbitrary",)),
    )(recv, w, m, v)


BIG = ("w_in", "ada_w", "w_br_gdn", "w_br_ssm", "w_out", "w_ffn_in", "w_ffn_out")
BIG_FIRST = ("ada_w", "w_in")
BIG_LATE = ("w_br_gdn", "w_br_ssm", "w_out", "w_ffn_in", "w_ffn_out")
BIG_COL_SHARDED = ("w_in", "ada_w", "w_ffn_in")
BIG_ADAM_ROWS = dict(w_in=128, ada_w=256, w_br_gdn=128, w_br_ssm=256, w_out=128, w_ffn_in=256, w_ffn_out=352)
CONV = ("gdn_conv_w", "ssm_conv_w")
SMALL = ("c_ctx", "ada_b", "norm1_w", "gdn_conv_b", "gdn_a_log", "gdn_dt_bias", "gdn_norm_w", "ssm_conv_b",
         "ssm_a_log", "ssm_dt_bias", "ssm_d", "ssm_norm_w", "norm2_w", "norm_f_w")
CONV_SHARD = XBC // N_DEV


def _to_rows(a):
    flat = a.reshape(-1)
    pad = (-flat.shape[0]) % PACK_W
    if pad:
        flat = jnp.pad(flat, (0, pad))
    return flat.reshape(-1, PACK_W)


def _pack(arrays, rows=None):
    buf = jnp.concatenate([_to_rows(a) for a in arrays], axis=0)
    if rows is not None and rows > buf.shape[0]:
        buf = jnp.pad(buf, ((0, rows - buf.shape[0]), (0, 0)))
    return buf


def _unpack(buf, shapes):
    out, r0 = [], 0
    for shp in shapes:
        n = 1
        for s in shp:
            n *= s
        nr = -(-n // PACK_W)
        out.append(buf[r0:r0 + nr].reshape(-1)[:n].reshape(shp))
        r0 += nr
    return out


def _full_from_blocks(blocks, col_sharded):
    _, r, c = blocks.shape
    if col_sharded:
        return jnp.transpose(blocks, (1, 0, 2)).reshape(r, N_DEV * c)
    return blocks.reshape(N_DEV * r, c)


def _blocks_from_full(full, col_sharded):
    if col_sharded:
        r, c = full.shape[0], full.shape[1] // N_DEV
        return jnp.transpose(full.reshape(r, N_DEV, c), (1, 0, 2))
    return full.reshape(N_DEV, full.shape[0] // N_DEV, full.shape[1])


def _pad_cols(a, n):
    return jnp.pad(a, ((0, 0), (0, n - a.shape[1])))


def _w_cat(w_in):
    return jnp.concatenate([
        w_in[:, O_QKV:O_ZG], w_in[:, O_XBC:O_DT], w_in[:, O_ZS:O_XBC], w_in[:, O_GATE:O_END], w_in[:, O_ZG:O_AB],
        _pad_cols(w_in[:, O_AB:O_ZS], 128), _pad_cols(w_in[:, O_DT:O_GATE], 128)], axis=1)


def _w_uncat(wc):
    return jnp.concatenate([
        wc[:, C_QKV:C_XBC], wc[:, C_ZG:C_AB], wc[:, C_AB:C_AB + (O_ZS - O_AB)], wc[:, C_ZS:C_GATE], wc[:, C_XBC:C_ZS],
        wc[:, C_DT:C_DT + (O_GATE - O_DT)], wc[:, C_GATE:C_ZG]], axis=1)


def _pad_row(vec, n=128):
    vec = vec.reshape(1, -1)
    return _pad_cols(vec, n)


def kernel(x, c, ctx, c_ctx, ada_w, ada_b, norm1_w, w_in, gdn_conv_w, gdn_conv_b, gdn_a_log, gdn_dt_bias, gdn_norm_w, ssm_conv_w, ssm_conv_b, ssm_a_log, ssm_dt_bias, ssm_d, ssm_norm_w, w_br_gdn, w_br_ssm, w_out, norm2_w, w_ffn_in, w_ffn_out, norm_f_w, loss_target, m_c_ctx, m_ada_w, m_ada_b, m_norm1_w, m_w_in, m_gdn_conv_w, m_gdn_conv_b, m_gdn_a_log, m_gdn_dt_bias, m_gdn_norm_w, m_ssm_conv_w, m_ssm_conv_b, m_ssm_a_log, m_ssm_dt_bias, m_ssm_d, m_ssm_norm_w, m_w_br_gdn, m_w_br_ssm, m_w_out, m_norm2_w, m_w_ffn_in, m_w_ffn_out, m_norm_f_w, v_c_ctx, v_ada_w, v_ada_b, v_norm1_w, v_w_in, v_gdn_conv_w, v_gdn_conv_b, v_gdn_a_log, v_gdn_dt_bias, v_gdn_norm_w, v_ssm_conv_w, v_ssm_conv_b, v_ssm_a_log, v_ssm_dt_bias, v_ssm_d, v_ssm_norm_w, v_w_br_gdn, v_w_br_ssm, v_w_out, v_norm2_w, v_w_ffn_in, v_w_ffn_out, v_norm_f_w):
    wts = dict(c_ctx=c_ctx, ada_w=ada_w, ada_b=ada_b, norm1_w=norm1_w, w_in=w_in, gdn_conv_w=gdn_conv_w, gdn_conv_b=gdn_conv_b, gdn_a_log=gdn_a_log, gdn_dt_bias=gdn_dt_bias, gdn_norm_w=gdn_norm_w, ssm_conv_w=ssm_conv_w, ssm_conv_b=ssm_conv_b, ssm_a_log=ssm_a_log, ssm_dt_bias=ssm_dt_bias, ssm_d=ssm_d, ssm_norm_w=ssm_norm_w, w_br_gdn=w_br_gdn, w_br_ssm=w_br_ssm, w_out=w_out, norm2_w=norm2_w, w_ffn_in=w_ffn_in, w_ffn_out=w_ffn_out, norm_f_w=norm_f_w)
    mom1 = dict(c_ctx=m_c_ctx, ada_w=m_ada_w, ada_b=m_ada_b, norm1_w=m_norm1_w, w_in=m_w_in, gdn_conv_w=m_gdn_conv_w, gdn_conv_b=m_gdn_conv_b, gdn_a_log=m_gdn_a_log, gdn_dt_bias=m_gdn_dt_bias, gdn_norm_w=m_gdn_norm_w, ssm_conv_w=m_ssm_conv_w, ssm_conv_b=m_ssm_conv_b, ssm_a_log=m_ssm_a_log, ssm_dt_bias=m_ssm_dt_bias, ssm_d=m_ssm_d, ssm_norm_w=m_ssm_norm_w, w_br_gdn=m_w_br_gdn, w_br_ssm=m_w_br_ssm, w_out=m_w_out, norm2_w=m_norm2_w, w_ffn_in=m_w_ffn_in, w_ffn_out=m_w_ffn_out, norm_f_w=m_norm_f_w)
    mom2 = dict(c_ctx=v_c_ctx, ada_w=v_ada_w, ada_b=v_ada_b, norm1_w=v_norm1_w, w_in=v_w_in, gdn_conv_w=v_gdn_conv_w, gdn_conv_b=v_gdn_conv_b, gdn_a_log=v_gdn_a_log, gdn_dt_bias=v_gdn_dt_bias, gdn_norm_w=v_gdn_norm_w, ssm_conv_w=v_ssm_conv_w, ssm_conv_b=v_ssm_conv_b, ssm_a_log=v_ssm_a_log, ssm_dt_bias=v_ssm_dt_bias, ssm_d=v_ssm_d, ssm_norm_w=v_ssm_norm_w, w_br_gdn=v_w_br_gdn, w_br_ssm=v_w_br_ssm, w_out=v_w_out, norm2_w=v_norm2_w, w_ffn_in=v_w_ffn_in, w_ffn_out=v_w_ffn_out, norm_f_w=v_norm_f_w)
    order = list(wts)

    seq = x.shape[1]
    t = TM + seq
    ntl, nlt, nctx = t // TM, seq // TM, TM // CH

    me_i = 4 * lax.axis_index("x") + 2 * lax.axis_index("y") + lax.axis_index("c")
    gathered = _exchange("ag_weights", _AllGather([wts[n][0].astype(BF16) for n in BIG_FIRST]))
    full = {n: _full_from_blocks(blk, n in BIG_COL_SHARDED) for n, blk in zip(BIG_FIRST, gathered)}
    late_gather = _AllGather([wts[n][0].astype(BF16) for n in BIG_LATE])
    conv_sh = _pack([wts[n] for n in CONV], rows=8)
    conv_g = _exchange("ag_conv", _AllGather([conv_sh]))[0].reshape(N_DEV, -1)
    ncv = 3 * CONV_SHARD
    for i, n in enumerate(CONV):
        off = -(-ncv // PACK_W) * PACK_W * i
        full[n] = jnp.transpose(conv_g[:, off:off + ncv].reshape(N_DEV, 3, CONV_SHARD), (1, 0, 2)).reshape(3, XBC)
    w_cat = _w_cat(full["w_in"])
    gcw = full["gdn_conv_w"].reshape(3, 1, XBC)
    scw = full["ssm_conv_w"].reshape(3, 1, XBC)

    n1w, n2w, nfw = norm1_w.reshape(1, D), norm2_w.reshape(1, D), norm_f_w.reshape(1, D)
    gcb, scb = gdn_conv_b.reshape(1, XBC), ssm_conv_b.reshape(1, XBC)
    alog16, dtb16 = _pad_row(gdn_a_log), _pad_row(gdn_dt_bias)
    alog64, dtb64 = _pad_row(ssm_a_log), _pad_row(ssm_dt_bias)
    gnw = gdn_norm_w.reshape(1, DK)
    ssd8 = jnp.tile(_pad_row(ssm_d), (8, 1))
    snw = ssm_norm_w.reshape(1, S_INNER)
    x2 = x[0]
    tgt = loss_target[0]
    xa = jnp.concatenate([ctx[0], x2], axis=0)
    cvec = jnp.concatenate([c, c_ctx.reshape(1, D), jnp.zeros((14, D), F32)], axis=0)

    a16 = _rowwise("silu_c", f_silu_rows, [(cvec, D, 0, 0)], [], [(D, BF16)], 1, tm=16)[0]
    mod = _mm("mm_mod", a16, full["ada_w"], "nn", F32) + ada_b
    sh1, sc1, g1, sh2, sc2, g2 = [mod[0:1, i * D:(i + 1) * D] for i in range(6)]
    csh1, csc1 = mod[1:2, 0:D], mod[1:2, D:2 * D]

    pre_pars = [n1w, sc1, sh1, csc1, csh1]
    a = _rowwise("pre", f_pre, [(xa, D, 0, 0)], pre_pars, [(D, BF16)], ntl, base=0)[0]
    proj = _mm("mm_proj", a, w_cat, "nn", F32, tm=1408, tn=1280)
    gp_rows = [(proj, XBC, C_QKV // XBC, 0), (proj, 128, C_AB // 128, 0)]
    gp_pars = [gcw, gcb, alog16, dtb16]
    q, k, v, gcum, beta = _rowwise("gdnprep", f_gdnprep, gp_rows, gp_pars, [(D, F32)] * 3 + [(128, F32)] * 2, ntl, base=0)
    sp_rows = [(proj, XBC, C_XBC // XBC, 0), (proj, 128, C_DT // 128, 0)]
    sp_pars = [scw, scb, alog64, dtb64]
    xs, bm, cm, dtc, acc = _rowwise(
        "ssmprep", f_ssmprep, sp_rows, sp_pars, [(S_INNER, F32), (512, F32), (512, F32), (128, F32), (128, F32)], ntl, base=0)
    o0, ss0, ts0, gathered = _gdn_scan_fwd("gdn_fwd0", q, k, v, gcum, beta, 0, nctx, ride=late_gather)
    full.update({n: _full_from_blocks(blk, n in BIG_COL_SHARDED) for n, blk in zip(BIG_LATE, gathered)})
    o1, ss1, ts1 = _gdn_scan_fwd("gdn_fwd1", q, k, v, gcum, beta, 1, nctx)
    y0, hs0 = _ssd_scan_fwd("ssd_fwd0", xs, dtc, acc, bm, cm, 0, nctx)
    y1, hs1 = _ssd_scan_fwd("ssd_fwd1", xs, dtc, acc, bm, cm, 1, nctx)
    post_rows = [(o0, D, 0, 1), (o1, D, 0, 1), (proj, D, C_ZG // D, 1), (y0, S_INNER, 0, 1), (y1, S_INNER, 0, 1),
                 (xs, S_INNER, 0, 1), (proj, S_INNER, C_ZS // S_INNER, 1)]
    post_pars = [gnw, ssd8, snw]
    og, ys = _rowwise("post", f_post, post_rows, post_pars, [(D, BF16), (S_INNER, BF16)], nlt)
    pg = _mm("mm_pg", og, full["w_br_gdn"], "nn", F32)
    ps = _mm("mm_ps", ys, full["w_br_ssm"], "nn", F32, tk=2048)
    merge_rows = [(proj, S_INNER, C_GATE // S_INNER, 1), (pg, D, 0, 0), (ps, D, 0, 0)]
    merged = _rowwise("merge", f_merge, merge_rows, [], [(D, BF16)], nlt)[0]
    mix = _mm("mm_mix", merged, full["w_out"], "nn", F32)
    res_rows = [(x2, D, 0, 0), (mix, D, 0, 0)]
    res_pars = [g1, n2w, sc2, sh2]
    h1, f = _rowwise("res1", f_res1, res_rows, res_pars, [(D, F32), (D, BF16)], nlt)
    u = _mm("mm_u", f, full["w_ffn_in"], "nn", F32, tn=1408)
    hact = _rowwise("act", f_act, [(u, 2 * D_FF, 0, 0)], [], [(D_FF, BF16)], nlt)[0]
    ff = _mm("mm_ff", hact, full["w_ffn_out"], "nn", F32, tk=2816)

    fin_rows = [(h1, D, 0, 0), (ff, D, 0, 0), (tgt, D, 0, 0)]
    d_h1a, d_ff, d_g2, d_nfw, loss_acc = _rowwise_bwd(
        "final", f_final, fin_rows, [g2, nfw], ["one"], [(0, F32), (1, BF16)], [0, 1], nlt, loss_out=True)
    d_hact = _mm("mm_dhact", d_ff, full["w_ffn_out"], "nt", BF16, tn=1408)
    g_w_ffn_out = _mm("mm_gwffo", hact, d_ff, "tn", BF16, tm=1408, tk=2048)
    d_u = _rowwise_bwd("act_bwd", f_act, [(u, 2 * D_FF, 0, 0)], [], [[(d_hact, D_FF, 0, 0)]], [(0, BF16)], [], nlt)[0]
    d_f = _mm("mm_df", d_u, full["w_ffn_in"], "nt", BF16, tk=2816)
    g_w_ffn_in = _mm("mm_gwffi", f, d_u, "tn", BF16, tn=1408, tk=2048)
    d_xres, d_mix, d_g1, d_n2w, d_sc2, d_sh2 = _rowwise_bwd(
        "res1_bwd", f_res1, res_rows, res_pars, [[(d_h1a, D, 0, 0)], [(d_f, D, 0, 0)]], [(0, F32), (1, BF16)], [0, 1, 2, 3], nlt)
    d_merged = _mm("mm_dmerged", d_mix, full["w_out"], "nt", BF16)
    g_w_out = _mm("mm_gwout", merged, d_mix, "tn", BF16, tk=2048)
    d_gate, d_pg, d_ps = _rowwise_bwd(
        "merge_bwd", f_merge, merge_rows, [], [[(d_merged, D, 0, 0)]], [(0, BF16), (1, BF16), (2, BF16)], [], nlt)
    d_og = _mm("mm_dog", d_pg, full["w_br_gdn"], "nt", BF16)
    g_w_br_gdn = _mm("mm_gwbrg", og, d_pg, "tn", BF16, tk=2048)
    d_ys = _mm("mm_dys", d_ps, full["w_br_ssm"], "nt", BF16, tn=2048)
    g_w_br_ssm = _mm("mm_gwbrs", ys, d_ps, "tn", BF16, tk=2048)
    d_o, d_zg, d_y, d_xs_post, d_zs, d_gnw, d_ssd8, d_snw = _rowwise_bwd(
        "post_bwd", f_post, post_rows, post_pars, [[(d_og, D, 0, 0)], [(d_ys, S_INNER, 0, 0)]],
        [(0, F32), (2, BF16), (3, F32), (5, F32), (6, BF16)], [0, 1, 2], nlt)
    late_grads = dict(w_br_gdn=g_w_br_gdn, w_br_ssm=g_w_br_ssm, w_out=g_w_out, w_ffn_in=g_w_ffn_in, w_ffn_out=g_w_ffn_out)
    late_a2a = _AllToAll([_blocks_from_full(late_grads[n], n in BIG_COL_SHARDED) for n in BIG_LATE])
    dq0, dk0, dv0, dg0, db0, recv_late = _gdn_scan_bwd("gdn_bwd0", q, k, v, gcum, beta, ss0, ts0, d_o, 0, nctx, ride=late_a2a)
    dq1, dk1, dv1, dg1, db1 = _gdn_scan_bwd("gdn_bwd1", q, k, v, gcum, beta, ss1, ts1, d_o, 1, nctx)
    dxs0, ddt0, dac0, dbm0, dcm0 = _ssd_scan_bwd("ssd_bwd0", xs, dtc, acc, bm, cm, hs0, d_y, 0, nctx)
    dxs1, ddt1, dac1, dbm1, dcm1 = _ssd_scan_bwd("ssd_bwd1", xs, dtc, acc, bm, cm, hs1, d_y, 1, nctx)
    row = lambda arr, wd: (arr, wd, 0, 0)
    d_qkv_raw, d_ab, d_gcw, d_gcb, d_alog16, d_dtb16 = _rowwise_bwd(
        "gdnprep_bwd", f_gdnprep, gp_rows, gp_pars,
        [[row(dq0, D), row(dq1, D)], [row(dk0, D), row(dk1, D)], [row(dv0, D), row(dv1, D)],
         [row(dg0, 128), row(dg1, 128)], [row(db0, 128), row(db1, 128)]],
        [(0, BF16), (1, BF16)], [0, 1, 2, 3], ntl, base=0)
    d_xbc_raw, d_dt, d_scw, d_scb, d_alog64, d_dtb64 = _rowwise_bwd(
        "ssmprep_bwd", f_ssmprep, sp_rows, sp_pars,
        [[row(dxs0, S_INNER), row(dxs1, S_INNER), (d_xs_post, S_INNER, 0, -1, True)], [row(dbm0, 512), row(dbm1, 512)],
         [row(dcm0, 512), row(dcm1, 512)], [row(ddt0, 128), row(ddt1, 128)], [row(dac0, 128), row(dac1, 128)]],
        [(0, BF16), (1, BF16)], [0, 1, 2, 3], ntl, base=0)
    ctx_zero = lambda wd: jnp.zeros((TM, wd), BF16)
    d_proj = jnp.concatenate([
        d_qkv_raw, d_xbc_raw, jnp.concatenate([ctx_zero(S_INNER), d_zs], axis=0),
        jnp.concatenate([ctx_zero(S_INNER), d_gate], axis=0), jnp.concatenate([ctx_zero(D), d_zg], axis=0), d_ab, d_dt], axis=1)
    g_w_cat = _mm("mm_gwcat", a, d_proj, "tn", BF16, tn=768, tk=2816)
    w_in_a2a = _AllToAll([_blocks_from_full(_w_uncat(g_w_cat), True)])
    d_a, recv_w_in = _mm("mm_da", d_proj, w_cat, "nt", BF16, tk=3840, ride=w_in_a2a)
    d_xa, d_n1w, d_sc1, d_sh1, d_csc1, d_csh1 = _rowwise_bwd(
        "pre_bwd", f_pre_thru, [(xa, D, 0, 0)], pre_pars, [[row(d_a, D)], [(d_xres, D, 0, -1, True)]],
        [(0, F32)], [0, 1, 2, 3, 4], ntl, base=0)
    zero4 = jnp.zeros((1, 4 * D), F32)
    d_mod = jnp.concatenate([
        jnp.concatenate([d_sh1, d_sc1, d_g1, d_sh2, d_sc2, d_g2], axis=1),
        jnp.concatenate([d_csh1, d_csc1, zero4], axis=1), jnp.zeros((14, 6 * D), F32)], axis=0)
    d_a16 = _mm("mm_da16", d_mod, full["ada_w"], "nt", F32)
    d_cvec = _rowwise_bwd("silu_c_bwd", f_silu_rows, [(cvec, D, 0, 0)], [], [[row(d_a16, D)]], [(0, F32)], [], 1, tm=16)[0]

    recv = dict(zip(BIG_LATE, recv_late), w_in=recv_w_in[0])
    big_un = {n: _reduce_adam("adam_" + n, recv[n], wts[n][0], mom1[n][0], mom2[n][0], BIG_ADAM_ROWS[n])
              for n in BIG if n != "ada_w"}

    small_g = dict(c_ctx=d_cvec[1], ada_b=d_mod[0] + d_mod[1], norm1_w=d_n1w, gdn_conv_b=d_gcb,
                   gdn_a_log=d_alog16[0, :2 * G_HEADS], gdn_dt_bias=d_dtb16[0, :2 * G_HEADS], gdn_norm_w=d_gnw,
                   ssm_conv_b=d_scb, ssm_a_log=d_alog64[0, :2 * S_HEADS], ssm_dt_bias=d_dtb64[0, :2 * S_HEADS],
                   ssm_d=d_ssd8[0, :S_HEADS], ssm_norm_w=d_snw, norm2_w=d_n2w, norm_f_w=d_nfw,
                   gdn_conv_w=d_gcw.reshape(3, XBC), ssm_conv_w=d_scw.reshape(3, XBC))
    small_names = SMALL + CONV
    factors = [a16[0].astype(F32), d_mod[0], d_mod[1]]
    sg_pack = _pack([small_g[n] for n in small_names] + factors, rows=56)
    recv_all = _exchange("ag_small_grads", _AllGather([sg_pack]))[0]
    recv_s = recv_all[:, :48]
    fac = recv_all[:, 42:55].reshape(N_DEV, -1)
    my_cols = lambda z: lax.dynamic_slice(z, (0, me_i * (6 * D // N_DEV)), (N_DEV, 6 * D // N_DEV))
    lhs = jnp.concatenate([fac[:, :D], jnp.broadcast_to(a16[1:2].astype(F32), (N_DEV, D))], axis=0)
    rhs = jnp.concatenate([my_cols(fac[:, D:7 * D]), my_cols(fac[:, 7 * D:])], axis=0)
    g_ada_w = _mm("mm_gwada", lhs, rhs, "tn", F32)
    big_un["ada_w"] = _reduce_adam("adam_ada_w", g_ada_w[None], wts["ada_w"][0], mom1["ada_w"][0], mom2["ada_w"][0],
                                   BIG_ADAM_ROWS["ada_w"])

    def placed(src, n):
        if n not in CONV:
            return src[n]
        return lax.dynamic_update_slice(jnp.zeros((3, XBC), F32), src[n][0], (0, me_i * CONV_SHARD))

    small_out = _reduce_adam("adam_small", recv_s, *[_pack([placed(src, n) for n in small_names], rows=48)
                                                     for src in (wts, mom1, mom2)], 48)
    small_shapes = [wts[n].shape if n in SMALL else (3, XBC) for n in small_names]
    small_un = [_unpack(buf, small_shapes) for buf in small_out]

    res = [{}, {}, {}, {}]
    for kind in range(4):
        for n in BIG:
            res[kind][n] = big_un[n][kind].reshape(wts[n].shape)
        for n, val in zip(small_names, small_un[kind]):
            if n in CONV:
                val = lax.dynamic_slice(val, (0, me_i * CONV_SHARD), (3, CONV_SHARD)).reshape(wts[n].shape)
            res[kind][n] = val
    loss = lax.psum(loss_acc[0, 0], ("x", "y", "c"))
    grad_x = d_xa[TM:].reshape(x.shape)
    return (loss, grad_x, *[res[0][n] for n in order], *[res[1][n] for n in order], *[res[2][n] for n in order],
            *[res[3][n] for n in order])
```

```python
import functools

import jax
import jax.numpy as jnp
from jax import lax
from jax.experimental import pallas as pl
from jax.experimental.pallas import tpu as pltpu

F32 = jnp.float32
BF16 = jnp.bfloat16
HI = lax.Precision.HIGHEST
HIGH = lax.Precision.HIGH
MESH = pl.DeviceIdType.MESH

D = 1024
CH = 64
TM = 256
EPS = 1e-6
NEG = -1e30
G_HEADS = 8
DK = 128
S_HEADS = 32
S_P = 64
S_GROUPS = 4
S_N = 128
S_INNER = 2048
XBC = 3072
D_FF = 2816
N_DEV = 8
PACK_W = 1024
VMEM_LIMIT = 56 * 1024 * 1024

ADAM_LR = 0.001
ADAM_B1 = 0.9
ADAM_B2 = 0.999
ADAM_EPS = 1e-08
ADAM_WD = 0.01
ADAM_STEP = 10

C_QKV, C_XBC, C_ZS, C_GATE, C_ZG, C_AB, C_DT, C_END = 0, 3072, 6144, 8192, 10240, 11264, 11392, 11520
O_QKV, O_ZG, O_AB, O_ZS, O_XBC, O_DT, O_GATE, O_END = 0, 3072, 4096, 4128, 6176, 9248, 9312, 11360


def _dot(a, b, prec=None):
    return jnp.dot(a, b, precision=prec, preferred_element_type=F32)


def _dot_nt(a, b, prec=None):
    return lax.dot_general(a, b, (((1,), (1,)), ((), ())), precision=prec, preferred_element_type=F32)


def _dot_tn(a, b, prec=None):
    return lax.dot_general(a, b, (((0,), (0,)), ((), ())), precision=prec, preferred_element_type=F32)


def _iota(shape, dim):
    return lax.broadcasted_iota(jnp.int32, shape, dim)


def _rms(x):
    return x * lax.rsqrt(jnp.mean(x * x, axis=-1, keepdims=True) + EPS)


def _l2n(x):
    return x * lax.rsqrt(jnp.sum(x * x, axis=-1, keepdims=True) + EPS)


def _silu(x):
    return x * jax.nn.sigmoid(x)


def _softplus(x):
    return jnp.maximum(x, 0.0) + jnp.log1p(jnp.exp(-jnp.abs(x)))


def _roll_rows(x, s):
    return pltpu.roll(x, s, 0)


def _up_raw(x, keep_up):
    return jnp.where(keep_up > 0.0, _roll_rows(x, 1), 0.0)


def _dn_raw(x, keep_dn):
    return jnp.where(keep_dn > 0.0, _roll_rows(x, x.shape[0] - 1), 0.0)


@jax.custom_vjp
def _shift_up(x, keep_up, keep_dn):
    return _up_raw(x, keep_up)


def _shift_up_fwd(x, keep_up, keep_dn):
    return _up_raw(x, keep_up), (keep_up, keep_dn)


def _shift_up_bwd(res, g):
    keep_up, keep_dn = res
    return _dn_raw(g, keep_dn), jnp.zeros_like(keep_up), jnp.zeros_like(keep_dn)


_shift_up.defvjp(_shift_up_fwd, _shift_up_bwd)


@jax.custom_vjp
def _shift_dn(x, keep_up, keep_dn):
    return _dn_raw(x, keep_dn)


def _shift_dn_fwd(x, keep_up, keep_dn):
    return _dn_raw(x, keep_dn), (keep_up, keep_dn)


def _shift_dn_bwd(res, g):
    keep_up, keep_dn = res
    return _up_raw(g, keep_up), jnp.zeros_like(keep_up), jnp.zeros_like(keep_dn)


_shift_dn.defvjp(_shift_dn_fwd, _shift_dn_bwd)


def _conv_keep(is_ctx, n):
    r = _iota((n, 1), 0)
    pos = jnp.where(is_ctx, r, r & (CH - 1))
    end = jnp.where(is_ctx, n - 1, CH - 1)
    return jnp.where(pos == 0, 0.0, 1.0).astype(F32), jnp.where(pos == end, 0.0, 1.0).astype(F32)


def _conv_silu(u, w3, b, keep_up, keep_dn):
    conv = b + _shift_up(u, keep_up, keep_dn) * w3[0] + u * w3[1] + _shift_dn(u, keep_up, keep_dn) * w3[2]
    return _silu(conv)


def _chunk_tri(n, rev):
    i = _iota((n, n), 0)
    j = _iota((n, n), 1)
    same = (i // CH) == (j // CH)
    seen = (i <= j) if rev else (i >= j)
    return jnp.where(same & seen, 1.0, 0.0).astype(F32)


def _expand_mat(rows, cols, per, base):
    r = _iota((rows, cols), 0)
    c = _iota((rows, cols), 1)
    return jnp.where(r == base + c // per, 1.0, 0.0).astype(F32)


def f_silu_rows(is_ctx, cvec):
    return (_silu(cvec).astype(BF16),)


def f_pre(is_ctx, x, n1w, sc, sh, csc, csh):
    sc_e = jnp.where(is_ctx, csc, sc)
    sh_e = jnp.where(is_ctx, csh, sh)
    a = _rms(x) * n1w * (1.0 + sc_e) + sh_e
    return (a.astype(BF16),)


def f_pre_thru(is_ctx, x, n1w, sc, sh, csc, csh):
    return f_pre(is_ctx, x, n1w, sc, sh, csc, csh)[0], x


def f_gdnprep(is_ctx, qkv_raw, ab_raw, cw, cb, alog, dtb):
    n = qkv_raw.shape[0]
    keep_up, keep_dn = _conv_keep(is_ctx, n)
    s = _conv_silu(qkv_raw, cw, cb, keep_up, keep_dn)
    qs, ks, vs = [], [], []
    for h in range(G_HEADS):
        qs.append(_l2n(s[:, h * DK:(h + 1) * DK]) * (DK ** -0.5))
        ks.append(_l2n(s[:, D + h * DK:D + (h + 1) * DK]))
    q = jnp.concatenate(qs, axis=1)
    k = jnp.concatenate(ks, axis=1)
    v = s[:, 2 * D:3 * D]
    lane = _iota(ab_raw.shape, 1)
    g = jnp.where(lane < 2 * G_HEADS, -jnp.exp(alog) * _softplus(ab_raw + dtb), 0.0)
    gcum = jnp.where(lane < G_HEADS, _dot(_chunk_tri(n, False), g, HI), _dot(_chunk_tri(n, True), g, HI))
    beta = jax.nn.sigmoid(ab_raw)
    return q, k, v, gcum, beta


def f_ssmprep(is_ctx, xbc_raw, dt_raw, cw, cb, alog, dtb):
    n = xbc_raw.shape[0]
    keep_up, keep_dn = _conv_keep(is_ctx, n)
    s = _conv_silu(xbc_raw, cw, cb, keep_up, keep_dn)
    xs = s[:, :S_INNER]
    bm = s[:, S_INNER:S_INNER + S_GROUPS * S_N]
    cm = s[:, S_INNER + S_GROUPS * S_N:]
    lane = _iota(dt_raw.shape, 1)
    dt = jnp.where(lane < 2 * S_HEADS, _softplus(dt_raw + dtb), 0.0)
    da = dt * (-jnp.exp(alog))
    acum = jnp.where(lane < S_HEADS, _dot(_chunk_tri(n, False), da, HI), _dot(_chunk_tri(n, True), da, HI))
    return xs, bm, cm, dt, acum


def f_post(is_ctx, o_f, o_b, zg, y_f, y_b, xs, zs, gnw, ssd8, snw):
    o = o_f + o_b
    ogs = []
    for h in range(G_HEADS):
        sl = slice(h * DK, (h + 1) * DK)
        ogs.append(_rms(o[:, sl]) * gnw * _silu(zg[:, sl]))
    og = jnp.concatenate(ogs, axis=1)
    row0 = jnp.where(_iota(ssd8.shape, 0) == 0, 1.0, 0.0).astype(F32)
    dexp = jnp.sum(_dot(ssd8 * row0, _expand_mat(128, S_INNER, S_P, 0), HI), axis=0, keepdims=True)
    y = (y_f + y_b + dexp * xs) * _silu(zs)
    gw = S_INNER // S_GROUPS
    ys = jnp.concatenate([_rms(y[:, i * gw:(i + 1) * gw]) * snw[:, i * gw:(i + 1) * gw] for i in range(S_GROUPS)], axis=1)
    return og.astype(BF16), ys.astype(BF16)


def f_merge(is_ctx, gate, pg, ps):
    m = jax.nn.sigmoid(gate[:, :D]) * pg + jax.nn.sigmoid(gate[:, D:]) * ps
    return (m.astype(BF16),)


def f_res1(is_ctx, x, mix, g1, n2w, sc2, sh2):
    h1 = x + g1 * mix
    f = _rms(h1) * n2w * (1.0 + sc2) + sh2
    return h1, f.astype(BF16)


def f_act(is_ctx, u):
    return ((_silu(u[:, :D_FF]) * u[:, D_FF:]).astype(BF16),)


def f_final(is_ctx, h1, ff, tgt, g2, nfw):
    h2 = h1 + g2 * ff
    y = _rms(h2) * nfw
    err = y - tgt
    return (0.5 * jnp.sum(jnp.mean(err * err, axis=-1, keepdims=True), axis=0, keepdims=True),)


def _each(fn, *lists):
    return [fn(*args) for args in zip(*lists)]


def _tri_inverse_all(mats):
    n = mats[0].shape[0]
    eye = jnp.where(_iota((n, n), 0) == _iota((n, n), 1), 1.0, 0.0).astype(F32)
    t = [eye - a for a in mats]
    p = [_dot(a, a, HIGH) for a in mats]
    for r in range(5):
        t = _each(lambda t_, p_: t_ + _dot(t_, p_, HIGH), t, p)
        if r < 4:
            p = [_dot(p_, p_, HIGH) for p_ in p]
    return t


@jax.custom_vjp
def _inverse_given(a, t):
    return t


def _inverse_given_fwd(a, t):
    return t, t


def _inverse_given_bwd(t, g):
    return -_dot_nt(_dot_tn(t, g, HIGH), t, HIGH), jnp.zeros_like(t)


_inverse_given.defvjp(_inverse_given_fwd, _inverse_given_bwd)


def gdn_local(qs, ks, vs, gcs, grs, bcs, rev, t_known=None):
    c = qs[0].shape[0]
    ii = _iota((c, c), 0)
    jj = _iota((c, c), 1)
    incl = (ii <= jj) if rev else (ii >= jj)
    strict = (ii < jj) if rev else (ii > jj)
    decay = _each(lambda gc, gr: jnp.exp(jnp.where(incl, gc - gr, NEG)), gcs, grs)
    kb = _each(lambda k, bc: k * bc, ks, bcs)
    a = _each(lambda kb_, k, dc: jnp.where(strict, _dot_nt(kb_, k) * dc, 0.0), kb, ks, decay)
    t = _tri_inverse_all(a) if t_known is None else _each(_inverse_given, a, t_known)
    eg = [jnp.exp(gc) for gc in gcs]
    rhs = _each(lambda kb_, eg_, v, bc: jnp.concatenate([kb_ * eg_, v * bc], axis=1), kb, eg, vs, bcs)
    wu = _each(lambda t_, r: _dot(t_, r, HIGH), t, rhs)
    lhs = _each(lambda wu_, q, eg_: jnp.concatenate([wu_[:, :DK], q * eg_], axis=0), wu, qs, eg)
    attn = _each(lambda q, k, dc: _dot_nt(q, k) * dc, qs, ks, decay)
    return wu, attn, lhs, t


def gdn_state(ss, wu, attn, lhs, ks, gcs, rev):
    c = ks[0].shape[0]
    is_last = _iota((c, 1), 0) == (0 if rev else c - 1)
    ws = _each(_dot, lhs, ss)
    v_new = _each(lambda wu_, ws_: wu_[:, DK:] - ws_[:c], wu, ws)
    o = _each(lambda ws_, at, vn: ws_[c:] + _dot(at, vn), ws, attn, v_new)
    gtot = [jnp.sum(jnp.where(is_last, gc, 0.0), axis=0, keepdims=True) for gc in gcs]
    s_new = _each(lambda s, k, gc, gt_, vn: s * jnp.exp(gt_) + _dot_tn(k * jnp.exp(gt_ - gc), vn), ss, ks, gcs, gtot, v_new)
    return s_new, o


def gdn_chunk(ss, qs, ks, vs, gcs, grs, bcs, rev, t_known=None):
    wu, attn, lhs, t = gdn_local(qs, ks, vs, gcs, grs, bcs, rev, t_known)
    s_new, o = gdn_state(ss, wu, attn, lhs, ks, gcs, rev)
    return s_new, o, t


def ssd_pick(dt0s, dt1s, ac0s, ac1s, ar0s, ar1s):
    lo = _iota((CH, 128), 1) < S_P
    pick = lambda u0, u1: jnp.where(lo, u0, u1)
    return _each(pick, dt0s, dt1s), _each(pick, ac0s, ac1s), _each(pick, ar0s, ar1s)


def ssd_local(xs, dts, acs, acr, bgs, cgs, rev):
    c = xs[0].shape[0]
    npair = len(xs)
    grp = [p * len(bgs) // npair for p in range(npair)]
    lane = _iota((c, 128), 1)
    ii = _iota((c, 128), 0)
    jl = lane & (S_P - 1)
    lo = lane < S_P
    seen = (ii <= jl) if rev else (ii >= jl)
    last = 0 if rev else c - 1
    split = lambda z: jnp.concatenate([jnp.where(lo, z, 0.0), jnp.where(lo, 0.0, z)], axis=0)
    cb = _each(lambda bg, cg: _dot_nt(cg, jnp.concatenate([bg, bg], axis=0)), bgs, cgs)
    seg = _each(lambda ac, ar: jnp.exp(jnp.where(seen, ac - ar, NEG)), acs, acr)
    xdt = _each(lambda x, dt: x * dt, xs, dts)
    ydiag = [_dot(cb[grp[p]] * seg[p], split(xdt[p])) for p in range(npair)]
    eac = [jnp.exp(ac) for ac in acs]
    atot = [jnp.sum(jnp.where(ii == last, ac, 0.0), axis=0, keepdims=True) for ac in acs]
    upd = [_dot_tn(bgs[grp[p]], xdt[p] * jnp.exp(atot[p] - acs[p])) for p in range(npair)]
    return ydiag, eac, [jnp.exp(at) for at in atot], upd


def ssd_state(hts, ydiag, eac, etot, upd, cgs):
    npair = len(hts)
    grp = [p * len(cgs) // npair for p in range(npair)]
    y = [ydiag[p] + _dot(cgs[grp[p]], hts[p]) * eac[p] for p in range(npair)]
    h_new = [hts[p] * etot[p] + upd[p] for p in range(npair)]
    return h_new, y


def ssd_chunk(hts, xs, dts, acs, acr, bgs, cgs, rev):
    ydiag, eac, etot, upd = ssd_local(xs, dts, acs, acr, bgs, cgs, rev)
    return ssd_state(hts, ydiag, eac, etot, upd, cgs)


def f_adamw(w, g, m, v):
    m = ADAM_B1 * m + (1.0 - ADAM_B1) * g
    v = ADAM_B2 * v + (1.0 - ADAM_B2) * jnp.square(g)
    m_hat = m / (1.0 - ADAM_B1 ** ADAM_STEP)
    v_hat = v / (1.0 - ADAM_B2 ** ADAM_STEP)
    delta = -ADAM_LR * (m_hat / (jnp.sqrt(v_hat) + ADAM_EPS) + ADAM_WD * w)
    return delta, m, v


def _cparams(sem):
    return pltpu.CompilerParams(dimension_semantics=sem, vmem_limit_bytes=VMEM_LIMIT)


def _pick(n, target):
    if n <= target:
        return n
    best = None
    for t in range(128, target + 1, 128):
        if n % t == 0:
            best = t
    assert best is not None, (n, target)
    return best


def _row_spec(tm, width, colblk, rowoff):
    return pl.BlockSpec((tm, width), lambda i: (i + rowoff, colblk))


def _par_spec(shape):
    nd = len(shape)
    return pl.BlockSpec(tuple(shape), lambda i: (0,) * nd)


def _rowwise(name, fn, rows, pars, outs, ntiles, base=1, tm=TM):
    nr, npar = len(rows), len(pars)

    def body(*refs):
        is_ctx = (pl.program_id(0) + base) == 0
        res = fn(is_ctx, *[r[...] for r in refs[:nr]], *[p[...] for p in refs[nr:nr + npar]])
        for o_ref, r in zip(refs[nr + npar:], res):
            o_ref[...] = r.astype(o_ref.dtype)

    return pl.pallas_call(
        body, grid=(ntiles,), name=name,
        in_specs=[_row_spec(tm, wd, cb, ro) for (_, wd, cb, ro) in rows] + [_par_spec(p.shape) for p in pars],
        out_specs=[_row_spec(tm, wd, 0, 0) for (wd, _) in outs],
        out_shape=[jax.ShapeDtypeStruct((ntiles * tm, wd), dt) for (wd, dt) in outs],
        compiler_params=_cparams(("arbitrary",)),
    )(*[r[0] for r in rows], *pars)


def _ct_spec(tm, desc):
    _, wd, cb, ro = desc[:4]
    if len(desc) > 4 and desc[4]:
        return pl.BlockSpec((tm, wd), lambda i: (jnp.maximum(i + ro, 0), cb))
    return _row_spec(tm, wd, cb, ro)


def _rowwise_bwd(name, fn, rows, pars, cts, drows, dpars, ntiles, base=1, loss_out=False, tm=TM):
    nr, npar = len(rows), len(pars)
    ct_rows = [d for ct in cts if isinstance(ct, list) for d in ct]
    nct = len(ct_rows)

    def body(*refs):
        i = pl.program_id(0)
        is_ctx = (i + base) == 0
        rows_v = [r[...] for r in refs[:nr]]
        pars_v = [p[...] for p in refs[nr:nr + npar]]
        ct_refs = list(refs[nr + npar:nr + npar + nct])
        out_refs = list(refs[nr + npar + nct:])
        outs, vjp = jax.vjp(lambda rv, pv: fn(is_ctx, *rv, *pv), rows_v, pars_v)

        def ct_value(desc):
            val = ct_refs.pop(0)[...].astype(F32)
            if len(desc) > 4 and desc[4]:
                val = jnp.where(is_ctx, 0.0, val)
            return val

        ct_vals = []
        for o, ct in zip(outs, cts):
            if ct is None:
                ct_vals.append(jnp.zeros_like(o))
            elif isinstance(ct, str):
                ct_vals.append(jnp.ones_like(o))
            else:
                acc = ct_value(ct[0])
                for desc in ct[1:]:
                    acc = acc + ct_value(desc)
                ct_vals.append(acc.astype(o.dtype))
        d_rows, d_pars = vjp(tuple(ct_vals))
        for (ri, _), o_ref in zip(drows, out_refs[:len(drows)]):
            o_ref[...] = d_rows[ri].astype(o_ref.dtype)
        acc_refs = out_refs[len(drows):]
        acc_vals = [d_pars[pi] for pi in dpars]
        if loss_out:
            acc_vals.append(jnp.broadcast_to(outs[0], (8, 128)))

        @pl.when(i == 0)
        def _():
            for o_ref, val in zip(acc_refs, acc_vals):
                o_ref[...] = val

        @pl.when(i > 0)
        def _():
            for o_ref, val in zip(acc_refs, acc_vals):
                o_ref[...] += val

    acc_shapes = [pars[pi].shape for pi in dpars] + ([(8, 128)] if loss_out else [])
    return pl.pallas_call(
        body, grid=(ntiles,), name=name,
        in_specs=[_row_spec(tm, wd, cb, ro) for (_, wd, cb, ro) in rows] + [_par_spec(p.shape) for p in pars]
        + [_ct_spec(tm, d) for d in ct_rows],
        out_specs=[_row_spec(tm, rows[ri][1], 0, 0) for (ri, _) in drows] + [_par_spec(s) for s in acc_shapes],
        out_shape=[jax.ShapeDtypeStruct((ntiles * tm, rows[ri][1]), dt) for (ri, dt) in drows]
        + [jax.ShapeDtypeStruct(tuple(s), F32) for s in acc_shapes],
        compiler_params=_cparams(("arbitrary",)),
    )(*[r[0] for r in rows], *pars, *[r[0] for r in ct_rows])


def _mm(name, a, b, mode, out_dtype, tm=1024, tn=1024, tk=1024, ride=None):
    if mode == "nn":
        (m, kd), (_, n) = a.shape, b.shape
    elif mode == "nt":
        (m, kd), (n, _) = a.shape, b.shape
    else:
        (kd, m), (_, n) = a.shape, b.shape
    tm, tn, tk = _pick(m, tm), _pick(n, tn), _pick(kd, tk)
    nk = kd // tk
    a_spec = {"nn": pl.BlockSpec((tm, tk), lambda i, j, k: (i, k)), "nt": pl.BlockSpec((tm, tk), lambda i, j, k: (i, k)),
              "tn": pl.BlockSpec((tk, tm), lambda i, j, k: (k, i))}[mode]
    b_spec = {"nn": pl.BlockSpec((tk, tn), lambda i, j, k: (k, j)), "nt": pl.BlockSpec((tn, tk), lambda i, j, k: (j, k)),
              "tn": pl.BlockSpec((tk, tn), lambda i, j, k: (k, j))}[mode]
    dot = {"nn": _dot, "nt": _dot_nt, "tn": _dot_tn}[mode]

    if nk == 1:
        def body(a_ref, b_ref, o_ref):
            o_ref[...] = dot(a_ref[...].astype(BF16), b_ref[...].astype(BF16)).astype(o_ref.dtype)
    else:
        def body(a_ref, b_ref, o_ref, acc_ref):
            k = pl.program_id(2)
            part = dot(a_ref[...].astype(BF16), b_ref[...].astype(BF16))

            @pl.when(k == 0)
            def _():
                acc_ref[...] = part

            @pl.when((k > 0) & (k < nk - 1))
            def _():
                acc_ref[...] += part

            @pl.when(k == nk - 1)
            def _():
                o_ref[...] = (acc_ref[...] + part).astype(o_ref.dtype)

    grid = (m // tm, n // tn, nk)
    at = lambda pos: functools.reduce(jnp.logical_and, [pl.program_id(ax) == pos(g) for ax, g in enumerate(grid)])
    r_in, r_out, r_shapes, r_scr, r_ops = _ride_args(ride)
    res = pl.pallas_call(
        _ride(body, ride, 2, 1, lambda: at(lambda g: 0), lambda: at(lambda g: g - 1)), grid=grid, name=name,
        in_specs=[a_spec, b_spec] + r_in,
        out_specs=[pl.BlockSpec((tm, tn), lambda i, j, k: (i, j))] + r_out,
        out_shape=[jax.ShapeDtypeStruct((m, n), out_dtype)] + r_shapes,
        scratch_shapes=([] if nk == 1 else [pltpu.VMEM((tm, tn), F32)]) + r_scr,
        compiler_params=_cparams(("arbitrary", "arbitrary", "arbitrary")),
    )(a, b, *r_ops)
    return res[0] if ride is None else (res[0], res[1:])


def _chunk_index(i, nch, nctx, rev):
    if not rev:
        return i
    return jnp.where(i < nctx, nctx - 1 - i, nch + nctx - 1 - i)


def _gdn_cols(d):
    return [d * G_HEADS + h for h in range(G_HEADS)], [2 * G_HEADS + d * G_HEADS + h for h in range(G_HEADS)]


def _gdn_operands(q_ref, k_ref, v_ref, g_ref, b_ref, d, rows=slice(None)):
    cols_g, cols_b = _gdn_cols(d)
    sls = [slice(h * DK, (h + 1) * DK) for h in range(G_HEADS)]
    gt, bt = g_ref[rows, :], b_ref[rows, :]
    gtt = gt.T
    qs = [q_ref[rows, sl] for sl in sls]
    ks = [k_ref[rows, sl] for sl in sls]
    vs = [v_ref[rows, sl] for sl in sls]
    gcs = [gt[:, cg:cg + 1] for cg in cols_g]
    grs = [gtt[cg:cg + 1, :] for cg in cols_g]
    bcs = [bt[:, cb:cb + 1] for cb in cols_b]
    return sls, qs, ks, vs, gcs, grs, bcs


GDN_FWD_CHUNKS = 4


def _gdn_scan_fwd(name, q, k, v, gcum, beta, d, nctx, ride=None):
    t = q.shape[0]
    nch = t // CH
    per = GDN_FWD_CHUNKS
    nst = nch // per
    rev = d == 1
    bix = lambda i: _chunk_index(i, nst, nctx // per, rev)
    full = pl.BlockSpec((per * CH, D), lambda i: (bix(i), 0))
    wide = pl.BlockSpec((per * CH, 128), lambda i: (bix(i), 0))
    order = list(range(per - 1, -1, -1)) if rev else list(range(per))

    def body(q_ref, k_ref, v_ref, g_ref, b_ref, o_ref, ss_ref, ts_ref, s_scr):
        @pl.when(pl.program_id(0) == 0)
        def _():
            s_scr[...] = jnp.zeros(s_scr.shape, F32)

        ops = [_gdn_operands(q_ref, k_ref, v_ref, g_ref, b_ref, d, slice(c * CH, (c + 1) * CH)) for c in order]
        sls = ops[0][0]
        cat = [sum((o[i] for o in ops), []) for i in range(1, 7)]
        wu, attn, lhs, tinv = gdn_local(*cat, rev)
        ss = [s_scr[h] for h in range(G_HEADS)]
        for n, c in enumerate(order):
            sl = slice(n * G_HEADS, (n + 1) * G_HEADS)
            s_new, o = gdn_state(ss, wu[sl], attn[sl], lhs[sl], cat[1][sl], cat[3][sl], rev)
            for h in range(G_HEADS):
                ss_ref[n, h] = ss[h]
                ts_ref[n, h] = tinv[n * G_HEADS + h]
                o_ref[c * CH:(c + 1) * CH, sls[h]] = o[h]
            ss = s_new
        for h in range(G_HEADS):
            s_scr[h] = ss[h]

    r_in, r_out, r_shapes, r_scr, r_ops = _ride_args(ride)
    res = pl.pallas_call(
        _ride(body, ride, 5, 3, lambda: pl.program_id(0) == 0, lambda: pl.program_id(0) == nst - 1), grid=(nst,), name=name,
        in_specs=[full, full, full, wide, wide] + r_in,
        out_specs=[full, pl.BlockSpec((per, G_HEADS, DK, DK), lambda i: (i, 0, 0, 0)),
                   pl.BlockSpec((per, G_HEADS, CH, CH), lambda i: (i, 0, 0, 0))] + r_out,
        out_shape=[jax.ShapeDtypeStruct((t, D), F32), jax.ShapeDtypeStruct((nch, G_HEADS, DK, DK), F32),
                   jax.ShapeDtypeStruct((nch, G_HEADS, CH, CH), F32)] + r_shapes,
        scratch_shapes=[pltpu.VMEM((G_HEADS, DK, DK), F32)] + r_scr,
        compiler_params=_cparams(("arbitrary",)),
    )(q, k, v, gcum, beta, *r_ops)
    return res if ride is None else (*res[:3], res[3:])


def _gdn_scan_bwd(name, q, k, v, gcum, beta, ssave, tsave, do, d, nctx, ride=None):
    t = q.shape[0]
    nch = t // CH
    rev = d == 1
    cix = lambda ib: _chunk_index(nch - 1 - ib, nch, nctx, rev)
    full = pl.BlockSpec((CH, D), lambda ib: (cix(ib), 0))
    wide = pl.BlockSpec((CH, 128), lambda ib: (cix(ib), 0))
    do_spec = pl.BlockSpec((CH, D), lambda ib: (jnp.maximum(cix(ib), nctx) - nctx, 0))

    def body(q_ref, k_ref, v_ref, g_ref, b_ref, ss_ref, ts_ref, do_ref, dq_ref, dk_ref, dv_ref, dg_ref, db_ref, ds_scr):
        ib = pl.program_id(0)

        @pl.when(ib == 0)
        def _():
            ds_scr[...] = jnp.zeros(ds_scr.shape, F32)

        sls, qs, ks, vs, gcs, grs, bcs = _gdn_operands(q_ref, k_ref, v_ref, g_ref, b_ref, d)
        t_known = [ts_ref[0, h] for h in range(G_HEADS)]
        cols_g, cols_b = _gdn_cols(d)
        is_lat = cix(ib) >= nctx
        ss = [ss_ref[0, h] for h in range(G_HEADS)]
        do_v = [jnp.where(is_lat, do_ref[:, sl], 0.0) for sl in sls]
        ds_in = [ds_scr[h] for h in range(G_HEADS)]
        _, vjp = jax.vjp(lambda *a: gdn_chunk(*a, rev, t_known)[:2], ss, qs, ks, vs, gcs, grs, bcs)
        ds, dq, dk, dv, dgc, dgr, dbc = vjp((ds_in, do_v))
        lane = _iota((CH, 128), 1)
        sub = _iota((128, CH), 0)
        dg = jnp.zeros((CH, 128), F32)
        dgt = jnp.zeros((128, CH), F32)
        db = jnp.zeros((CH, 128), F32)
        for h in range(G_HEADS):
            ds_scr[h] = ds[h]
            dq_ref[:, sls[h]] = dq[h]
            dk_ref[:, sls[h]] = dk[h]
            dv_ref[:, sls[h]] = dv[h]
            dg = dg + jnp.where(lane == cols_g[h], dgc[h], 0.0)
            dgt = dgt + jnp.where(sub == cols_g[h], dgr[h], 0.0)
            db = db + jnp.where(lane == cols_b[h], dbc[h], 0.0)
        dg_ref[...] = dg + dgt.T
        db_ref[...] = db

    r_in, r_out, r_shapes, r_scr, r_ops = _ride_args(ride)
    res = pl.pallas_call(
        _ride(body, ride, 8, 5, lambda: pl.program_id(0) == 0, lambda: pl.program_id(0) == nch - 1), grid=(nch,), name=name,
        in_specs=[full, full, full, wide, wide,
                  pl.BlockSpec((1, G_HEADS, DK, DK), lambda ib: (nch - 1 - ib, 0, 0, 0)),
                  pl.BlockSpec((1, G_HEADS, CH, CH), lambda ib: (nch - 1 - ib, 0, 0, 0)), do_spec] + r_in,
        out_specs=[full, full, full, wide, wide] + r_out,
        out_shape=[jax.ShapeDtypeStruct((t, D), F32)] * 3 + [jax.ShapeDtypeStruct((t, 128), F32)] * 2 + r_shapes,
        scratch_shapes=[pltpu.VMEM((G_HEADS, DK, DK), F32)] + r_scr,
        compiler_params=_cparams(("arbitrary",)),
    )(q, k, v, gcum, beta, ssave, tsave, do, *r_ops)
    return res if ride is None else (*res[:5], res[5:])


N_PAIRS = S_HEADS // 2


def _ssd_operands(x_ref, dt_ref, ac_ref, b_ref, c_ref, d, rows=slice(None)):
    sls = [slice(p * 128, (p + 1) * 128) for p in range(N_PAIRS)]
    gsl = [slice(g * S_N, (g + 1) * S_N) for g in range(S_GROUPS)]
    cols = [d * S_HEADS + h for h in range(S_HEADS)]
    dtc, acc = dt_ref[rows, :], ac_ref[rows, :]
    act = jnp.concatenate([acc, acc], axis=0).T
    col = lambda z, cc: z[:, cc:cc + 1]
    dts, acs, acr = ssd_pick(
        [col(dtc, cols[2 * p]) for p in range(N_PAIRS)], [col(dtc, cols[2 * p + 1]) for p in range(N_PAIRS)],
        [col(acc, cols[2 * p]) for p in range(N_PAIRS)], [col(acc, cols[2 * p + 1]) for p in range(N_PAIRS)],
        [act[cols[2 * p]:cols[2 * p] + 1, :] for p in range(N_PAIRS)],
        [act[cols[2 * p + 1]:cols[2 * p + 1] + 1, :] for p in range(N_PAIRS)])
    ops = ([x_ref[rows, sl] for sl in sls], dts, acs, acr, [b_ref[rows, gs] for gs in gsl], [c_ref[rows, gs] for gs in gsl])
    return sls, gsl, cols, ops


SSD_FWD_CHUNKS = 4


def _ssd_scan_fwd(name, xs, dtc, acc, bm, cm, d, nctx):
    t = xs.shape[0]
    nch = t // CH
    per = SSD_FWD_CHUNKS
    nst = nch // per
    rev = d == 1
    bix = lambda i: _chunk_index(i, nst, nctx // per, rev)
    inner = pl.BlockSpec((per * CH, S_INNER), lambda i: (bix(i), 0))
    wide = pl.BlockSpec((per * CH, 128), lambda i: (bix(i), 0))
    grp = pl.BlockSpec((per * CH, S_GROUPS * S_N), lambda i: (bix(i), 0))
    order = list(range(per - 1, -1, -1)) if rev else list(range(per))

    def body(x_ref, dt_ref, ac_ref, b_ref, c_ref, y_ref, hs_ref, h_scr):
        @pl.when(pl.program_id(0) == 0)
        def _():
            h_scr[...] = jnp.zeros(h_scr.shape, F32)

        loc, cgs, sls = [], [], None
        for c in order:
            sls, _, _, ops = _ssd_operands(x_ref, dt_ref, ac_ref, b_ref, c_ref, d, slice(c * CH, (c + 1) * CH))
            loc.append(ssd_local(*ops, rev))
            cgs.append(ops[5])
        hts = [h_scr[p] for p in range(N_PAIRS)]
        for n, c in enumerate(order):
            h_new, y = ssd_state(hts, *loc[n], cgs[n])
            for p in range(N_PAIRS):
                hs_ref[n, p] = hts[p]
                y_ref[c * CH:(c + 1) * CH, sls[p]] = y[p]
            hts = h_new
        for p in range(N_PAIRS):
            h_scr[p] = hts[p]

    return pl.pallas_call(
        body, grid=(nst,), name=name,
        in_specs=[inner, wide, wide, grp, grp],
        out_specs=[inner, pl.BlockSpec((per, N_PAIRS, S_N, 128), lambda i: (i, 0, 0, 0))],
        out_shape=[jax.ShapeDtypeStruct((t, S_INNER), F32), jax.ShapeDtypeStruct((nch, N_PAIRS, S_N, 128), F32)],
        scratch_shapes=[pltpu.VMEM((N_PAIRS, S_N, 128), F32)],
        compiler_params=_cparams(("arbitrary",)),
    )(xs, dtc, acc, bm, cm)


def _ssd_scan_bwd(name, xs, dtc, acc, bm, cm, hsave, dy, d, nctx):
    t = xs.shape[0]
    nch = t // CH
    rev = d == 1
    cix = lambda ib: _chunk_index(nch - 1 - ib, nch, nctx, rev)
    inner = pl.BlockSpec((CH, S_INNER), lambda ib: (cix(ib), 0))
    wide = pl.BlockSpec((CH, 128), lambda ib: (cix(ib), 0))
    grp = pl.BlockSpec((CH, S_GROUPS * S_N), lambda ib: (cix(ib), 0))
    dy_spec = pl.BlockSpec((CH, S_INNER), lambda ib: (jnp.maximum(cix(ib), nctx) - nctx, 0))

    def body(x_ref, dt_ref, ac_ref, b_ref, c_ref, hs_ref, dy_ref, dx_ref, ddt_ref, dac_ref, db_ref, dc_ref, dh_scr):
        ib = pl.program_id(0)

        @pl.when(ib == 0)
        def _():
            dh_scr[...] = jnp.zeros(dh_scr.shape, F32)

        sls, gsl, cols, ops = _ssd_operands(x_ref, dt_ref, ac_ref, b_ref, c_ref, d)
        is_lat = cix(ib) >= nctx
        hts = [hs_ref[0, p] for p in range(N_PAIRS)]
        dy_v = [jnp.where(is_lat, dy_ref[:, sl], 0.0) for sl in sls]
        dh_in = [dh_scr[p] for p in range(N_PAIRS)]
        _, vjp = jax.vjp(lambda *a: ssd_chunk(*a, rev), hts, *ops)
        dh, dx, ddts, dacs, dacr, db, dc = vjp((dh_in, dy_v))
        for p in range(N_PAIRS):
            dh_scr[p] = dh[p]
            dx_ref[:, sls[p]] = dx[p]
        r = _iota((S_INNER, 128), 0)
        e_t = jnp.where(_iota((S_INNER, 128), 1) == d * S_HEADS + r // S_P, 1.0, 0.0).astype(F32)
        ddt_ref[...] = _dot(jnp.concatenate(ddts, axis=1), e_t, HIGH)
        dac_cols = _dot(jnp.concatenate(dacs, axis=1), e_t, HIGH)
        sub = _iota((128, 128), 0)
        lane = _iota((128, 128), 1)
        m = jnp.zeros((128, 128), F32)
        for p in range(N_PAIRS):
            m = m + jnp.where(sub == p, jnp.sum(dacr[p], axis=0, keepdims=True), 0.0)
        mt = m.T
        s0 = jnp.where(lane == d * S_HEADS + 2 * sub, 1.0, 0.0).astype(F32)
        s1 = jnp.where(lane == d * S_HEADS + 2 * sub + 1, 1.0, 0.0).astype(F32)
        dac_ref[...] = dac_cols + _dot(mt[:CH], s0, HIGH) + _dot(mt[CH:], s1, HIGH)
        for g in range(S_GROUPS):
            db_ref[:, gsl[g]] = db[g]
            dc_ref[:, gsl[g]] = dc[g]

    return pl.pallas_call(
        body, grid=(nch,), name=name,
        in_specs=[inner, wide, wide, grp, grp,
                  pl.BlockSpec((1, N_PAIRS, S_N, 128), lambda ib: (nch - 1 - ib, 0, 0, 0)), dy_spec],
        out_specs=[inner, wide, wide, grp, grp],
        out_shape=[jax.ShapeDtypeStruct((t, S_INNER), F32)] + [jax.ShapeDtypeStruct((t, 128), F32)] * 2
        + [jax.ShapeDtypeStruct((t, S_GROUPS * S_N), F32)] * 2,
        scratch_shapes=[pltpu.VMEM((N_PAIRS, S_N, 128), F32)],
        compiler_params=_cparams(("arbitrary",)),
    )(xs, dtc, acc, bm, cm, hsave, dy)


def _mesh_pos():
    return lax.axis_index("x"), lax.axis_index("y"), lax.axis_index("c")


def _hbm_specs(n):
    return [pl.BlockSpec(memory_space=pl.ANY)] * n


def _sem_shapes(nw):
    return [pltpu.SemaphoreType.DMA((nw, 7)), pltpu.SemaphoreType.DMA((nw, 7)), pltpu.SemaphoreType.DMA((nw,))]


class _AllGather:
    def __init__(self, shards):
        self.arrays = list(shards)
        self.out_shapes = [jax.ShapeDtypeStruct((N_DEV,) + xs.shape, xs.dtype) for xs in shards]

    def _parts(self, x_refs, out_refs, sems):
        send_sems, recv_sems, local_sems = sems
        nw = len(x_refs)
        x, y, c = _mesh_pos()
        me, sibling = (x, y, c), (x, y, 1 - c)
        chips = [(1 - x, y), (x, 1 - y), (1 - x, 1 - y)]

        def slot(w, px, py, pc):
            return out_refs[w].at[4 * px + 2 * py + pc]

        def copy(w, k, block, to, src=None):
            return pltpu.make_async_remote_copy(
                src_ref=slot(w, *block) if src is None else src, dst_ref=slot(w, *block),
                send_sem=send_sems.at[w, k], recv_sem=recv_sems.at[w, k], device_id=to, device_id_type=MESH)

        mine = [pltpu.make_async_copy(x_refs[w], slot(w, *me), local_sems.at[w]) for w in range(nw)]
        first = []
        for w in range(nw):
            first.append(copy(w, 0, me, sibling, src=x_refs[w]))
            first += [copy(w, 1 + j, me, (*chip, c), src=x_refs[w]) for j, chip in enumerate(chips)]
        return nw, me, sibling, chips, c, copy, mine, first

    def start(self, x_refs, out_refs, sems):
        _, _, _, _, _, _, mine, first = self._parts(x_refs, out_refs, sems)
        for cp in mine + first:
            cp.start()

    def finish(self, x_refs, out_refs, sems):
        nw, me, sibling, chips, c, copy, mine, first = self._parts(x_refs, out_refs, sems)
        passed = []
        for j, chip in enumerate(chips):
            for w in range(nw):
                copy(w, 1 + j, (*chip, c), me).wait_recv()
                fwd = copy(w, 4 + j, (*chip, c), sibling)
                fwd.start()
                passed.append(fwd)
        for w in range(nw):
            copy(w, 0, sibling, me).wait_recv()
            for j, chip in enumerate(chips):
                copy(w, 4 + j, (*chip, 1 - c), me).wait_recv()
        for cp in first + passed:
            cp.wait_send()
        for cp in mine:
            cp.wait()


class _AllToAll:
    def __init__(self, blocks):
        self.arrays = list(blocks)
        self.out_shapes = [jax.ShapeDtypeStruct(g.shape, g.dtype) for g in blocks]

    def _parts(self, g_refs, out_refs, sems):
        send_sems, recv_sems, local_sems = sems
        nw = len(g_refs)
        x, y, c = _mesh_pos()
        me_i = 4 * x + 2 * y + c
        mine = [pltpu.make_async_copy(g_refs[w].at[me_i], out_refs[w].at[me_i], local_sems.at[w]) for w in range(nw)]
        cps = []
        for k in range(1, N_DEV):
            px = 1 - x if (k >> 2) & 1 else x
            py = 1 - y if (k >> 1) & 1 else y
            pc = 1 - c if k & 1 else c
            for w in range(nw):
                cps.append(pltpu.make_async_remote_copy(
                    src_ref=g_refs[w].at[4 * px + 2 * py + pc], dst_ref=out_refs[w].at[me_i],
                    send_sem=send_sems.at[w, k - 1], recv_sem=recv_sems.at[w, k - 1],
                    device_id=(px, py, pc), device_id_type=MESH))
        return mine, cps

    def start(self, g_refs, out_refs, sems):
        mine, cps = self._parts(g_refs, out_refs, sems)
        for cp in mine + cps:
            cp.start()

    def finish(self, g_refs, out_refs, sems):
        mine, cps = self._parts(g_refs, out_refs, sems)
        for cp in cps + mine:
            cp.wait()


def _exchange(name, ex):
    nw = len(ex.arrays)

    def body(*refs):
        ins, outs, sems = refs[:nw], refs[nw:2 * nw], refs[2 * nw:]
        ex.start(ins, outs, sems)
        ex.finish(ins, outs, sems)

    return pl.pallas_call(body, name=name, out_shape=ex.out_shapes, in_specs=_hbm_specs(nw), out_specs=_hbm_specs(nw),
                          scratch_shapes=_sem_shapes(nw))(*ex.arrays)


def _ride(body, ex, n_in, n_out, is_first, is_last):
    if ex is None:
        return body
    nw = len(ex.arrays)

    def riding(*refs):
        ins, ex_in = refs[:n_in], refs[n_in:n_in + nw]
        outs = refs[n_in + nw:n_in + nw + n_out]
        ex_out = refs[n_in + nw + n_out:n_in + 2 * nw + n_out]
        rest = refs[n_in + 2 * nw + n_out:]
        scratch, sems = rest[:len(rest) - 3], rest[len(rest) - 3:]

        @pl.when(is_first())
        def _():
            ex.start(ex_in, ex_out, sems)

        body(*ins, *outs, *scratch)

        @pl.when(is_last())
        def _():
            ex.finish(ex_in, ex_out, sems)

    return riding


def _ride_args(ex):
    if ex is None:
        return [], [], [], [], []
    nw = len(ex.arrays)
    return _hbm_specs(nw), _hbm_specs(nw), list(ex.out_shapes), _sem_shapes(nw), list(ex.arrays)


def _reduce_adam(name, recv, w, m, v, tm):
    rows, width = w.shape
    nslot = recv.shape[0]

    def body(recv_ref, w_ref, m_ref, v_ref, g_ref, d_ref, m2_ref, v2_ref):
        g = recv_ref[0].astype(F32)
        for s in range(1, nslot):
            g = g + recv_ref[s].astype(F32)
        delta, m2, v2 = f_adamw(w_ref[...], g, m_ref[...], v_ref[...])
        g_ref[...] = g
        d_ref[...] = delta
        m2_ref[...] = m2
        v2_ref[...] = v2

    row = pl.BlockSpec((tm, width), lambda i: (i, 0))
    return pl.pallas_call(
        body, grid=(rows // tm,), name=name,
        in_specs=[pl.BlockSpec((nslot, tm, width), lambda i: (0, i, 0)), row, row, row],
        out_specs=[row] * 4,
        out_shape=[jax.ShapeDtypeStruct((rows, width), F32)] * 4,
        compiler_params=_cparams(("arbitrary",)),
    )(recv, w, m, v)


BIG = ("w_in", "ada_w", "w_br_gdn", "w_br_ssm", "w_out", "w_ffn_in", "w_ffn_out")
BIG_FIRST = ("ada_w", "w_in")
BIG_LATE = ("w_br_gdn", "w_br_ssm", "w_out", "w_ffn_in", "w_ffn_out")
BIG_COL_SHARDED = ("w_in", "ada_w", "w_ffn_in")
BIG_ADAM_ROWS = dict(w_in=128, ada_w=256, w_br_gdn=128, w_br_ssm=256, w_out=128, w_ffn_in=256, w_ffn_out=352)
CONV = ("gdn_conv_w", "ssm_conv_w")
SMALL = ("c_ctx", "ada_b", "norm1_w", "gdn_conv_b", "gdn_a_log", "gdn_dt_bias", "gdn_norm_w", "ssm_conv_b",
         "ssm_a_log", "ssm_dt_bias", "ssm_d", "ssm_norm_w", "norm2_w", "norm_f_w")
CONV_SHARD = XBC // N_DEV


def _to_rows(a):
    flat = a.reshape(-1)
    pad = (-flat.shape[0]) % PACK_W
    if pad:
        flat = jnp.pad(flat, (0, pad))
    return flat.reshape(-1, PACK_W)


def _pack(arrays, rows=None):
    buf = jnp.concatenate([_to_rows(a) for a in arrays], axis=0)
    if rows is not None and rows > buf.shape[0]:
        buf = jnp.pad(buf, ((0, rows - buf.shape[0]), (0, 0)))
    return buf


def _unpack(buf, shapes):
    out, r0 = [], 0
    for shp in shapes:
        n = 1
        for s in shp:
            n *= s
        nr = -(-n // PACK_W)
        out.append(buf[r0:r0 + nr].reshape(-1)[:n].reshape(shp))
        r0 += nr
    return out


def _full_from_blocks(blocks, col_sharded):
    _, r, c = blocks.shape
    if col_sharded:
        return jnp.transpose(blocks, (1, 0, 2)).reshape(r, N_DEV * c)
    return blocks.reshape(N_DEV * r, c)


def _blocks_from_full(full, col_sharded):
    if col_sharded:
        r, c = full.shape[0], full.shape[1] // N_DEV
        return jnp.transpose(full.reshape(r, N_DEV, c), (1, 0, 2))
    return full.reshape(N_DEV, full.shape[0] // N_DEV, full.shape[1])


def _pad_cols(a, n):
    return jnp.pad(a, ((0, 0), (0, n - a.shape[1])))


def _w_cat(w_in):
    return jnp.concatenate([
        w_in[:, O_QKV:O_ZG], w_in[:, O_XBC:O_DT], w_in[:, O_ZS:O_XBC], w_in[:, O_GATE:O_END], w_in[:, O_ZG:O_AB],
        _pad_cols(w_in[:, O_AB:O_ZS], 128), _pad_cols(w_in[:, O_DT:O_GATE], 128)], axis=1)


def _w_uncat(wc):
    return jnp.concatenate([
        wc[:, C_QKV:C_XBC], wc[:, C_ZG:C_AB], wc[:, C_AB:C_AB + (O_ZS - O_AB)], wc[:, C_ZS:C_GATE], wc[:, C_XBC:C_ZS],
        wc[:, C_DT:C_DT + (O_GATE - O_DT)], wc[:, C_GATE:C_ZG]], axis=1)


def _pad_row(vec, n=128):
    vec = vec.reshape(1, -1)
    return _pad_cols(vec, n)


def kernel(x, c, ctx, c_ctx, ada_w, ada_b, norm1_w, w_in, gdn_conv_w, gdn_conv_b, gdn_a_log, gdn_dt_bias, gdn_norm_w, ssm_conv_w, ssm_conv_b, ssm_a_log, ssm_dt_bias, ssm_d, ssm_norm_w, w_br_gdn, w_br_ssm, w_out, norm2_w, w_ffn_in, w_ffn_out, norm_f_w, loss_target, m_c_ctx, m_ada_w, m_ada_b, m_norm1_w, m_w_in, m_gdn_conv_w, m_gdn_conv_b, m_gdn_a_log, m_gdn_dt_bias, m_gdn_norm_w, m_ssm_conv_w, m_ssm_conv_b, m_ssm_a_log, m_ssm_dt_bias, m_ssm_d, m_ssm_norm_w, m_w_br_gdn, m_w_br_ssm, m_w_out, m_norm2_w, m_w_ffn_in, m_w_ffn_out, m_norm_f_w, v_c_ctx, v_ada_w, v_ada_b, v_norm1_w, v_w_in, v_gdn_conv_w, v_gdn_conv_b, v_gdn_a_log, v_gdn_dt_bias, v_gdn_norm_w, v_ssm_conv_w, v_ssm_conv_b, v_ssm_a_log, v_ssm_dt_bias, v_ssm_d, v_ssm_norm_w, v_w_br_gdn, v_w_br_ssm, v_w_out, v_norm2_w, v_w_ffn_in, v_w_ffn_out, v_norm_f_w):
    wts = dict(c_ctx=c_ctx, ada_w=ada_w, ada_b=ada_b, norm1_w=norm1_w, w_in=w_in, gdn_conv_w=gdn_conv_w, gdn_conv_b=gdn_conv_b, gdn_a_log=gdn_a_log, gdn_dt_bias=gdn_dt_bias, gdn_norm_w=gdn_norm_w, ssm_conv_w=ssm_conv_w, ssm_conv_b=ssm_conv_b, ssm_a_log=ssm_a_log, ssm_dt_bias=ssm_dt_bias, ssm_d=ssm_d, ssm_norm_w=ssm_norm_w, w_br_gdn=w_br_gdn, w_br_ssm=w_br_ssm, w_out=w_out, norm2_w=norm2_w, w_ffn_in=w_ffn_in, w_ffn_out=w_ffn_out, norm_f_w=norm_f_w)
    mom1 = dict(c_ctx=m_c_ctx, ada_w=m_ada_w, ada_b=m_ada_b, norm1_w=m_norm1_w, w_in=m_w_in, gdn_conv_w=m_gdn_conv_w, gdn_conv_b=m_gdn_conv_b, gdn_a_log=m_gdn_a_log, gdn_dt_bias=m_gdn_dt_bias, gdn_norm_w=m_gdn_norm_w, ssm_conv_w=m_ssm_conv_w, ssm_conv_b=m_ssm_conv_b, ssm_a_log=m_ssm_a_log, ssm_dt_bias=m_ssm_dt_bias, ssm_d=m_ssm_d, ssm_norm_w=m_ssm_norm_w, w_br_gdn=m_w_br_gdn, w_br_ssm=m_w_br_ssm, w_out=m_w_out, norm2_w=m_norm2_w, w_ffn_in=m_w_ffn_in, w_ffn_out=m_w_ffn_out, norm_f_w=m_norm_f_w)
    mom2 = dict(c_ctx=v_c_ctx, ada_w=v_ada_w, ada_b=v_ada_b, norm1_w=v_norm1_w, w_in=v_w_in, gdn_conv_w=v_gdn_conv_w, gdn_conv_b=v_gdn_conv_b, gdn_a_log=v_gdn_a_log, gdn_dt_bias=v_gdn_dt_bias, gdn_norm_w=v_gdn_norm_w, ssm_conv_w=v_ssm_conv_w, ssm_conv_b=v_ssm_conv_b, ssm_a_log=v_ssm_a_log, ssm_dt_bias=v_ssm_dt_bias, ssm_d=v_ssm_d, ssm_norm_w=v_ssm_norm_w, w_br_gdn=v_w_br_gdn, w_br_ssm=v_w_br_ssm, w_out=v_w_out, norm2_w=v_norm2_w, w_ffn_in=v_w_ffn_in, w_ffn_out=v_w_ffn_out, norm_f_w=v_norm_f_w)
    order = list(wts)

    seq = x.shape[1]
    t = TM + seq
    ntl, nlt, nctx = t // TM, seq // TM, TM // CH

    me_i = 4 * lax.axis_index("x") + 2 * lax.axis_index("y") + lax.axis_index("c")
    gathered = _exchange("ag_weights", _AllGather([wts[n][0].astype(BF16) for n in BIG_FIRST]))
    full = {n: _full_from_blocks(blk, n in BIG_COL_SHARDED) for n, blk in zip(BIG_FIRST, gathered)}
    late_gather = _AllGather([wts[n][0].astype(BF16) for n in BIG_LATE])
    conv_sh = _pack([wts[n] for n in CONV], rows=8)
    conv_g = _exchange("ag_conv", _AllGather([conv_sh]))[0].reshape(N_DEV, -1)
    ncv = 3 * CONV_SHARD
    for i, n in enumerate(CONV):
        off = -(-ncv // PACK_W) * PACK_W * i
        full[n] = jnp.transpose(conv_g[:, off:off + ncv].reshape(N_DEV, 3, CONV_SHARD), (1, 0, 2)).reshape(3, XBC)
    w_cat = _w_cat(full["w_in"])
    gcw = full["gdn_conv_w"].reshape(3, 1, XBC)
    scw = full["ssm_conv_w"].reshape(3, 1, XBC)

    n1w, n2w, nfw = norm1_w.reshape(1, D), norm2_w.reshape(1, D), norm_f_w.reshape(1, D)
    gcb, scb = gdn_conv_b.reshape(1, XBC), ssm_conv_b.reshape(1, XBC)
    alog16, dtb16 = _pad_row(gdn_a_log), _pad_row(gdn_dt_bias)
    alog64, dtb64 = _pad_row(ssm_a_log), _pad_row(ssm_dt_bias)
    gnw = gdn_norm_w.reshape(1, DK)
    ssd8 = jnp.tile(_pad_row(ssm_d), (8, 1))
    snw = ssm_norm_w.reshape(1, S_INNER)
    x2 = x[0]
    tgt = loss_target[0]
    xa = jnp.concatenate([ctx[0], x2], axis=0)
    cvec = jnp.concatenate([c, c_ctx.reshape(1, D), jnp.zeros((14, D), F32)], axis=0)

    a16 = _rowwise("silu_c", f_silu_rows, [(cvec, D, 0, 0)], [], [(D, BF16)], 1, tm=16)[0]
    mod = _mm("mm_mod", a16, full["ada_w"], "nn", F32) + ada_b
    sh1, sc1, g1, sh2, sc2, g2 = [mod[0:1, i * D:(i + 1) * D] for i in range(6)]
    csh1, csc1 = mod[1:2, 0:D], mod[1:2, D:2 * D]

    pre_pars = [n1w, sc1, sh1, csc1, csh1]
    a = _rowwise("pre", f_pre, [(xa, D, 0, 0)], pre_pars, [(D, BF16)], ntl, base=0)[0]
    proj = _mm("mm_proj", a, w_cat, "nn", F32, tm=1408, tn=1280)
    gp_rows = [(proj, XBC, C_QKV // XBC, 0), (proj, 128, C_AB // 128, 0)]
    gp_pars = [gcw, gcb, alog16, dtb16]
    q, k, v, gcum, beta = _rowwise("gdnprep", f_gdnprep, gp_rows, gp_pars, [(D, F32)] * 3 + [(128, F32)] * 2, ntl, base=0)
    sp_rows = [(proj, XBC, C_XBC // XBC, 0), (proj, 128, C_DT // 128, 0)]
    sp_pars = [scw, scb, alog64, dtb64]
    xs, bm, cm, dtc, acc = _rowwise(
        "ssmprep", f_ssmprep, sp_rows, sp_pars, [(S_INNER, F32), (512, F32), (512, F32), (128, F32), (128, F32)], ntl, base=0)
    o0, ss0, ts0, gathered = _gdn_scan_fwd("gdn_fwd0", q, k, v, gcum, beta, 0, nctx, ride=late_gather)
    full.update({n: _full_from_blocks(blk, n in BIG_COL_SHARDED) for n, blk in zip(BIG_LATE, gathered)})
    o1, ss1, ts1 = _gdn_scan_fwd("gdn_fwd1", q, k, v, gcum, beta, 1, nctx)
    y0, hs0 = _ssd_scan_fwd("ssd_fwd0", xs, dtc, acc, bm, cm, 0, nctx)
    y1, hs1 = _ssd_scan_fwd("ssd_fwd1", xs, dtc, acc, bm, cm, 1, nctx)
    post_rows = [(o0, D, 0, 1), (o1, D, 0, 1), (proj, D, C_ZG // D, 1), (y0, S_INNER, 0, 1), (y1, S_INNER, 0, 1),
                 (xs, S_INNER, 0, 1), (proj, S_INNER, C_ZS // S_INNER, 1)]
    post_pars = [gnw, ssd8, snw]
    og, ys = _rowwise("post", f_post, post_rows, post_pars, [(D, BF16), (S_INNER, BF16)], nlt)
    pg = _mm("mm_pg", og, full["w_br_gdn"], "nn", F32)
    ps = _mm("mm_ps", ys, full["w_br_ssm"], "nn", F32, tk=2048)
    merge_rows = [(proj, S_INNER, C_GATE // S_INNER, 1), (pg, D, 0, 0), (ps, D, 0, 0)]
    merged = _rowwise("merge", f_merge, merge_rows, [], [(D, BF16)], nlt)[0]
    mix = _mm("mm_mix", merged, full["w_out"], "nn", F32)
    res_rows = [(x2, D, 0, 0), (mix, D, 0, 0)]
    res_pars = [g1, n2w, sc2, sh2]
    h1, f = _rowwise("res1", f_res1, res_rows, res_pars, [(D, F32), (D, BF16)], nlt)
    u = _mm("mm_u", f, full["w_ffn_in"], "nn", F32, tn=1408)
    hact = _rowwise("act", f_act, [(u, 2 * D_FF, 0, 0)], [], [(D_FF, BF16)], nlt)[0]
    ff = _mm("mm_ff", hact, full["w_ffn_out"], "nn", F32, tk=2816)

    fin_rows = [(h1, D, 0, 0), (ff, D, 0, 0), (tgt, D, 0, 0)]
    d_h1a, d_ff, d_g2, d_nfw, loss_acc = _rowwise_bwd(
        "final", f_final, fin_rows, [g2, nfw], ["one"], [(0, F32), (1, BF16)], [0, 1], nlt, loss_out=True)
    d_hact = _mm("mm_dhact", d_ff, full["w_ffn_out"], "nt", BF16, tn=1408)
    g_w_ffn_out = _mm("mm_gwffo", hact, d_ff, "tn", BF16, tm=1408, tk=2048)
    d_u = _rowwise_bwd("act_bwd", f_act, [(u, 2 * D_FF, 0, 0)], [], [[(d_hact, D_FF, 0, 0)]], [(0, BF16)], [], nlt)[0]
    d_f = _mm("mm_df", d_u, full["w_ffn_in"], "nt", BF16, tk=2816)
    g_w_ffn_in = _mm("mm_gwffi", f, d_u, "tn", BF16, tn=1408, tk=2048)
    d_xres, d_mix, d_g1, d_n2w, d_sc2, d_sh2 = _rowwise_bwd(
        "res1_bwd", f_res1, res_rows, res_pars, [[(d_h1a, D, 0, 0)], [(d_f, D, 0, 0)]], [(0, F32), (1, BF16)], [0, 1, 2, 3], nlt)
    d_merged = _mm("mm_dmerged", d_mix, full["w_out"], "nt", BF16)
    g_w_out = _mm("mm_gwout", merged, d_mix, "tn", BF16, tk=2048)
    d_gate, d_pg, d_ps = _rowwise_bwd(
        "merge_bwd", f_merge, merge_rows, [], [[(d_merged, D, 0, 0)]], [(0, BF16), (1, BF16), (2, BF16)], [], nlt)
    d_og = _mm("mm_dog", d_pg, full["w_br_gdn"], "nt", BF16)
    g_w_br_gdn = _mm("mm_gwbrg", og, d_pg, "tn", BF16, tk=2048)
    d_ys = _mm("mm_dys", d_ps, full["w_br_ssm"], "nt", BF16, tn=2048)
    g_w_br_ssm = _mm("mm_gwbrs", ys, d_ps, "tn", BF16, tk=2048)
    d_o, d_zg, d_y, d_xs_post, d_zs, d_gnw, d_ssd8, d_snw = _rowwise_bwd(
        "post_bwd", f_post, post_rows, post_pars, [[(d_og, D, 0, 0)], [(d_ys, S_INNER, 0, 0)]],
        [(0, F32), (2, BF16), (3, F32), (5, F32), (6, BF16)], [0, 1, 2], nlt)
    late_grads = dict(w_br_gdn=g_w_br_gdn, w_br_ssm=g_w_br_ssm, w_out=g_w_out, w_ffn_in=g_w_ffn_in, w_ffn_out=g_w_ffn_out)
    late_a2a = _AllToAll([_blocks_from_full(late_grads[n], n in BIG_COL_SHARDED) for n in BIG_LATE])
    dq0, dk0, dv0, dg0, db0, recv_late = _gdn_scan_bwd("gdn_bwd0", q, k, v, gcum, beta, ss0, ts0, d_o, 0, nctx, ride=late_a2a)
    dq1, dk1, dv1, dg1, db1 = _gdn_scan_bwd("gdn_bwd1", q, k, v, gcum, beta, ss1, ts1, d_o, 1, nctx)
    dxs0, ddt0, dac0, dbm0, dcm0 = _ssd_scan_bwd("ssd_bwd0", xs, dtc, acc, bm, cm, hs0, d_y, 0, nctx)
    dxs1, ddt1, dac1, dbm1, dcm1 = _ssd_scan_bwd("ssd_bwd1", xs, dtc, acc, bm, cm, hs1, d_y, 1, nctx)
    row = lambda arr, wd: (arr, wd, 0, 0)
    d_qkv_raw, d_ab, d_gcw, d_gcb, d_alog16, d_dtb16 = _rowwise_bwd(
        "gdnprep_bwd", f_gdnprep, gp_rows, gp_pars,
        [[row(dq0, D), row(dq1, D)], [row(dk0, D), row(dk1, D)], [row(dv0, D), row(dv1, D)],
         [row(dg0, 128), row(dg1, 128)], [row(db0, 128), row(db1, 128)]],
        [(0, BF16), (1, BF16)], [0, 1, 2, 3], ntl, base=0)
    d_xbc_raw, d_dt, d_scw, d_scb, d_alog64, d_dtb64 = _rowwise_bwd(
        "ssmprep_bwd", f_ssmprep, sp_rows, sp_pars,
        [[row(dxs0, S_INNER), row(dxs1, S_INNER), (d_xs_post, S_INNER, 0, -1, True)], [row(dbm0, 512), row(dbm1, 512)],
         [row(dcm0, 512), row(dcm1, 512)], [row(ddt0, 128), row(ddt1, 128)], [row(dac0, 128), row(dac1, 128)]],
        [(0, BF16), (1, BF16)], [0, 1, 2, 3], ntl, base=0)
    ctx_zero = lambda wd: jnp.zeros((TM, wd), BF16)
    d_proj = jnp.concatenate([
        d_qkv_raw, d_xbc_raw, jnp.concatenate([ctx_zero(S_INNER), d_zs], axis=0),
        jnp.concatenate([ctx_zero(S_INNER), d_gate], axis=0), jnp.concatenate([ctx_zero(D), d_zg], axis=0), d_ab, d_dt], axis=1)
    g_w_cat = _mm("mm_gwcat", a, d_proj, "tn", BF16, tn=768, tk=2816)
    w_in_a2a = _AllToAll([_blocks_from_full(_w_uncat(g_w_cat), True)])
    d_a, recv_w_in = _mm("mm_da", d_proj, w_cat, "nt", BF16, tk=3840, ride=w_in_a2a)
    d_xa, d_n1w, d_sc1, d_sh1, d_csc1, d_csh1 = _rowwise_bwd(
        "pre_bwd", f_pre_thru, [(xa, D, 0, 0)], pre_pars, [[row(d_a, D)], [(d_xres, D, 0, -1, True)]],
        [(0, F32)], [0, 1, 2, 3, 4], ntl, base=0)
    zero4 = jnp.zeros((1, 4 * D), F32)
    d_mod = jnp.concatenate([
        jnp.concatenate([d_sh1, d_sc1, d_g1, d_sh2, d_sc2, d_g2], axis=1),
        jnp.concatenate([d_csh1, d_csc1, zero4], axis=1), jnp.zeros((14, 6 * D), F32)], axis=0)
    d_a16 = _mm("mm_da16", d_mod, full["ada_w"], "nt", F32)
    d_cvec = _rowwise_bwd("silu_c_bwd", f_silu_rows, [(cvec, D, 0, 0)], [], [[row(d_a16, D)]], [(0, F32)], [], 1, tm=16)[0]

    recv = dict(zip(BIG_LATE, recv_late), w_in=recv_w_in[0])
    big_un = {n: _reduce_adam("adam_" + n, recv[n], wts[n][0], mom1[n][0], mom2[n][0], BIG_ADAM_ROWS[n])
              for n in BIG if n != "ada_w"}

    small_g = dict(c_ctx=d_cvec[1], ada_b=d_mod[0] + d_mod[1], norm1_w=d_n1w, gdn_conv_b=d_gcb,
                   gdn_a_log=d_alog16[0, :2 * G_HEADS], gdn_dt_bias=d_dtb16[0, :2 * G_HEADS], gdn_norm_w=d_gnw,
                   ssm_conv_b=d_scb, ssm_a_log=d_alog64[0, :2 * S_HEADS], ssm_dt_bias=d_dtb64[0, :2 * S_HEADS],
                   ssm_d=d_ssd8[0, :S_HEADS], ssm_norm_w=d_snw, norm2_w=d_n2w, norm_f_w=d_nfw,
                   gdn_conv_w=d_gcw.reshape(3, XBC), ssm_conv_w=d_scw.reshape(3, XBC))
    small_names = SMALL + CONV
    factors = [a16[0].astype(F32), d_mod[0], d_mod[1]]
    sg_pack = _pack([small_g[n] for n in small_names] + factors, rows=56)
    recv_all = _exchange("ag_small_grads", _AllGather([sg_pack]))[0]
    recv_s = recv_all[:, :48]
    fac = recv_all[:, 42:55].reshape(N_DEV, -1)
    my_cols = lambda z: lax.dynamic_slice(z, (0, me_i * (6 * D // N_DEV)), (N_DEV, 6 * D // N_DEV))
    lhs = jnp.concatenate([fac[:, :D], jnp.broadcast_to(a16[1:2].astype(F32), (N_DEV, D))], axis=0)
    rhs = jnp.concatenate([my_cols(fac[:, D:7 * D]), my_cols(fac[:, 7 * D:])], axis=0)
    g_ada_w = _mm("mm_gwada", lhs, rhs, "tn", F32)
    big_un["ada_w"] = _reduce_adam("adam_ada_w", g_ada_w[None], wts["ada_w"][0], mom1["ada_w"][0], mom2["ada_w"][0],
                                   BIG_ADAM_ROWS["ada_w"])

    def placed(src, n):
        if n not in CONV:
            return src[n]
        return lax.dynamic_update_slice(jnp.zeros((3, XBC), F32), src[n][0], (0, me_i * CONV_SHARD))

    small_out = _reduce_adam("adam_small", recv_s, *[_pack([placed(src, n) for n in small_names], rows=48)
                                                     for src in (wts, mom1, mom2)], 48)
    small_shapes = [wts[n].shape if n in SMALL else (3, XBC) for n in small_names]
    small_un = [_unpack(buf, small_shapes) for buf in small_out]

    res = [{}, {}, {}, {}]
    for kind in range(4):
        for n in BIG:
            res[kind][n] = big_un[n][kind].reshape(wts[n].shape)
        for n, val in zip(small_names, small_un[kind]):
            if n in CONV:
                val = lax.dynamic_slice(val, (0, me_i * CONV_SHARD), (3, CONV_SHARD)).reshape(wts[n].shape)
            res[kind][n] = val
    loss = lax.psum(loss_acc[0, 0], ("x", "y", "c"))
    grad_x = d_xa[TM:].reshape(x.shape)
    return (loss, grad_x, *[res[0][n] for n in order], *[res[1][n] for n in order], *[res[2][n] for n in order],
            *[res[3][n] for n in order])
```

```python
import functools

import jax
import jax.numpy as jnp
from jax import lax
from jax.experimental import pallas as pl
from jax.experimental.pallas import tpu as pltpu

F32 = jnp.float32
BF16 = jnp.bfloat16
HI = lax.Precision.HIGHEST
HIGH = lax.Precision.HIGH
MESH = pl.DeviceIdType.MESH

D = 1024
CH = 64
TM = 256
EPS = 1e-6
NEG = -1e30
G_HEADS = 8
DK = 128
S_HEADS = 32
S_P = 64
S_GROUPS = 4
S_N = 128
S_INNER = 2048
XBC = 3072
D_FF = 2816
N_DEV = 8
PACK_W = 1024
VMEM_LIMIT = 56 * 1024 * 1024

ADAM_LR = 0.001
ADAM_B1 = 0.9
ADAM_B2 = 0.999
ADAM_EPS = 1e-08
ADAM_WD = 0.01
ADAM_STEP = 10

C_QKV, C_XBC, C_ZS, C_GATE, C_ZG, C_AB, C_DT, C_END = 0, 3072, 6144, 8192, 10240, 11264, 11392, 11520
O_QKV, O_ZG, O_AB, O_ZS, O_XBC, O_DT, O_GATE, O_END = 0, 3072, 4096, 4128, 6176, 9248, 9312, 11360


def _dot(a, b, prec=None):
    return jnp.dot(a, b, precision=prec, preferred_element_type=F32)


def _dot_nt(a, b, prec=None):
    return lax.dot_general(a, b, (((1,), (1,)), ((), ())), precision=prec, preferred_element_type=F32)


def _dot_tn(a, b, prec=None):
    return lax.dot_general(a, b, (((0,), (0,)), ((), ())), precision=prec, preferred_element_type=F32)


def _iota(shape, dim):
    return lax.broadcasted_iota(jnp.int32, shape, dim)


def _rms(x):
    return x * lax.rsqrt(jnp.mean(x * x, axis=-1, keepdims=True) + EPS)


def _l2n(x):
    return x * lax.rsqrt(jnp.sum(x * x, axis=-1, keepdims=True) + EPS)


def _silu(x):
    return x * jax.nn.sigmoid(x)


def _softplus(x):
    return jnp.maximum(x, 0.0) + jnp.log1p(jnp.exp(-jnp.abs(x)))


def _roll_rows(x, s):
    return pltpu.roll(x, s, 0)


def _up_raw(x, keep_up):
    return jnp.where(keep_up > 0.0, _roll_rows(x, 1), 0.0)


def _dn_raw(x, keep_dn):
    return jnp.where(keep_dn > 0.0, _roll_rows(x, x.shape[0] - 1), 0.0)


@jax.custom_vjp
def _shift_up(x, keep_up, keep_dn):
    return _up_raw(x, keep_up)


def _shift_up_fwd(x, keep_up, keep_dn):
    return _up_raw(x, keep_up), (keep_up, keep_dn)


def _shift_up_bwd(res, g):
    keep_up, keep_dn = res
    return _dn_raw(g, keep_dn), jnp.zeros_like(keep_up), jnp.zeros_like(keep_dn)


_shift_up.defvjp(_shift_up_fwd, _shift_up_bwd)


@jax.custom_vjp
def _shift_dn(x, keep_up, keep_dn):
    return _dn_raw(x, keep_dn)


def _shift_dn_fwd(x, keep_up, keep_dn):
    return _dn_raw(x, keep_dn), (keep_up, keep_dn)


def _shift_dn_bwd(res, g):
    keep_up, keep_dn = res
    return _up_raw(g, keep_up), jnp.zeros_like(keep_up), jnp.zeros_like(keep_dn)


_shift_dn.defvjp(_shift_dn_fwd, _shift_dn_bwd)


def _conv_keep(is_ctx, n):
    r = _iota((n, 1), 0)
    pos = jnp.where(is_ctx, r, r & (CH - 1))
    end = jnp.where(is_ctx, n - 1, CH - 1)
    return jnp.where(pos == 0, 0.0, 1.0).astype(F32), jnp.where(pos == end, 0.0, 1.0).astype(F32)


def _conv_silu(u, w3, b, keep_up, keep_dn):
    conv = b + _shift_up(u, keep_up, keep_dn) * w3[0] + u * w3[1] + _shift_dn(u, keep_up, keep_dn) * w3[2]
    return _silu(conv)


def _chunk_tri(n, rev):
    i = _iota((n, n), 0)
    j = _iota((n, n), 1)
    same = (i // CH) == (j // CH)
    seen = (i <= j) if rev else (i >= j)
    return jnp.where(same & seen, 1.0, 0.0).astype(F32)


def _expand_mat(rows, cols, per, base):
    r = _iota((rows, cols), 0)
    c = _iota((rows, cols), 1)
    return jnp.where(r == base + c // per, 1.0, 0.0).astype(F32)


def f_silu_rows(is_ctx, cvec):
    return (_silu(cvec).astype(BF16),)


def f_pre(is_ctx, x, n1w, sc, sh, csc, csh):
    sc_e = jnp.where(is_ctx, csc, sc)
    sh_e = jnp.where(is_ctx, csh, sh)
    a = _rms(x) * n1w * (1.0 + sc_e) + sh_e
    return (a.astype(BF16),)


def f_pre_thru(is_ctx, x, n1w, sc, sh, csc, csh):
    return f_pre(is_ctx, x, n1w, sc, sh, csc, csh)[0], x


def f_gdnprep(is_ctx, qkv_raw, ab_raw, cw, cb, alog, dtb):
    n = qkv_raw.shape[0]
    keep_up, keep_dn = _conv_keep(is_ctx, n)
    s = _conv_silu(qkv_raw, cw, cb, keep_up, keep_dn)
    qs, ks, vs = [], [], []
    for h in range(G_HEADS):
        qs.append(_l2n(s[:, h * DK:(h + 1) * DK]) * (DK ** -0.5))
        ks.append(_l2n(s[:, D + h * DK:D + (h + 1) * DK]))
    q = jnp.concatenate(qs, axis=1)
    k = jnp.concatenate(ks, axis=1)
    v = s[:, 2 * D:3 * D]
    lane = _iota(ab_raw.shape, 1)
    g = jnp.where(lane < 2 * G_HEADS, -jnp.exp(alog) * _softplus(ab_raw + dtb), 0.0)
    gcum = jnp.where(lane < G_HEADS, _dot(_chunk_tri(n, False), g, HI), _dot(_chunk_tri(n, True), g, HI))
    beta = jax.nn.sigmoid(ab_raw)
    return q, k, v, gcum, beta


def f_ssmprep(is_ctx, xbc_raw, dt_raw, cw, cb, alog, dtb):
    n = xbc_raw.shape[0]
    keep_up, keep_dn = _conv_keep(is_ctx, n)
    s = _conv_silu(xbc_raw, cw, cb, keep_up, keep_dn)
    xs = s[:, :S_INNER]
    bm = s[:, S_INNER:S_INNER + S_GROUPS * S_N]
    cm = s[:, S_INNER + S_GROUPS * S_N:]
    lane = _iota(dt_raw.shape, 1)
    dt = jnp.where(lane < 2 * S_HEADS, _softplus(dt_raw + dtb), 0.0)
    da = dt * (-jnp.exp(alog))
    acum = jnp.where(lane < S_HEADS, _dot(_chunk_tri(n, False), da, HI), _dot(_chunk_tri(n, True), da, HI))
    return xs, bm, cm, dt, acum


def f_post(is_ctx, o, zg, y_scan, xs, zs, gnw, ssd8, snw):
    ogs = []
    for h in range(G_HEADS):
        sl = slice(h * DK, (h + 1) * DK)
        ogs.append(_rms(o[:, sl]) * gnw * _silu(zg[:, sl]))
    og = jnp.concatenate(ogs, axis=1)
    row0 = jnp.where(_iota(ssd8.shape, 0) == 0, 1.0, 0.0).astype(F32)
    dexp = jnp.sum(_dot(ssd8 * row0, _expand_mat(128, S_INNER, S_P, 0), HI), axis=0, keepdims=True)
    y = (y_scan + dexp * xs) * _silu(zs)
    gw = S_INNER // S_GROUPS
    ys = jnp.concatenate([_rms(y[:, i * gw:(i + 1) * gw]) * snw[:, i * gw:(i + 1) * gw] for i in range(S_GROUPS)], axis=1)
    return og.astype(BF16), ys.astype(BF16)


def f_merge(is_ctx, gate, pg, ps):
    m = jax.nn.sigmoid(gate[:, :D]) * pg + jax.nn.sigmoid(gate[:, D:]) * ps
    return (m.astype(BF16),)


def f_res1(is_ctx, x, mix, g1, n2w, sc2, sh2):
    h1 = x + g1 * mix
    f = _rms(h1) * n2w * (1.0 + sc2) + sh2
    return h1, f.astype(BF16)


def f_act(is_ctx, u):
    return ((_silu(u[:, :D_FF]) * u[:, D_FF:]).astype(BF16),)


def f_final(is_ctx, h1, ff, tgt, g2, nfw):
    h2 = h1 + g2 * ff
    y = _rms(h2) * nfw
    err = y - tgt
    return (0.5 * jnp.sum(jnp.mean(err * err, axis=-1, keepdims=True), axis=0, keepdims=True),)


def _each(fn, *lists):
    return [fn(*args) for args in zip(*lists)]


def _tri_inverse_all(mats):
    n = mats[0].shape[0]
    eye = jnp.where(_iota((n, n), 0) == _iota((n, n), 1), 1.0, 0.0).astype(F32)
    t = [eye - a for a in mats]
    p = [_dot(a, a, HIGH) for a in mats]
    for r in range(5):
        t = _each(lambda t_, p_: t_ + _dot(t_, p_, HIGH), t, p)
        if r < 4:
            p = [_dot(p_, p_, HIGH) for p_ in p]
    return t


@jax.custom_vjp
def _inverse_given(a, t):
    return t


def _inverse_given_fwd(a, t):
    return t, t


def _inverse_given_bwd(t, g):
    return -_dot_nt(_dot_tn(t, g, HIGH), t, HIGH), jnp.zeros_like(t)


_inverse_given.defvjp(_inverse_given_fwd, _inverse_given_bwd)


def gdn_local(qs, ks, vs, gcs, grs, bcs, rev, t_known=None):
    c = qs[0].shape[0]
    ii = _iota((c, c), 0)
    jj = _iota((c, c), 1)
    incl = (ii <= jj) if rev else (ii >= jj)
    strict = (ii < jj) if rev else (ii > jj)
    decay = _each(lambda gc, gr: jnp.exp(jnp.where(incl, gc - gr, NEG)), gcs, grs)
    kb = _each(lambda k, bc: k * bc, ks, bcs)
    a = _each(lambda kb_, k, dc: jnp.where(strict, _dot_nt(kb_, k) * dc, 0.0), kb, ks, decay)
    t = _tri_inverse_all(a) if t_known is None else _each(_inverse_given, a, t_known)
    eg = [jnp.exp(gc) for gc in gcs]
    rhs = _each(lambda kb_, eg_, v, bc: jnp.concatenate([kb_ * eg_, v * bc], axis=1), kb, eg, vs, bcs)
    wu = _each(lambda t_, r: _dot(t_, r, HIGH), t, rhs)
    lhs = _each(lambda wu_, q, eg_: jnp.concatenate([wu_[:, :DK], q * eg_], axis=0), wu, qs, eg)
    attn = _each(lambda q, k, dc: _dot_nt(q, k) * dc, qs, ks, decay)
    return wu, attn, lhs, t


def gdn_state(ss, wu, attn, lhs, ks, gcs, rev):
    c = ks[0].shape[0]
    is_last = _iota((c, 1), 0) == (0 if rev else c - 1)
    ws = _each(_dot, lhs, ss)
    v_new = _each(lambda wu_, ws_: wu_[:, DK:] - ws_[:c], wu, ws)
    o = _each(lambda ws_, at, vn: ws_[c:] + _dot(at, vn), ws, attn, v_new)
    gtot = [jnp.sum(jnp.where(is_last, gc, 0.0), axis=0, keepdims=True) for gc in gcs]
    s_new = _each(lambda s, k, gc, gt_, vn: s * jnp.exp(gt_) + _dot_tn(k * jnp.exp(gt_ - gc), vn), ss, ks, gcs, gtot, v_new)
    return s_new, o


def gdn_chunk(ss, qs, ks, vs, gcs, grs, bcs, rev, t_known=None):
    wu, attn, lhs, t = gdn_local(qs, ks, vs, gcs, grs, bcs, rev, t_known)
    s_new, o = gdn_state(ss, wu, attn, lhs, ks, gcs, rev)
    return s_new, o, t


def ssd_pick(dt0s, dt1s, ac0s, ac1s, ar0s, ar1s):
    lo = _iota((CH, 128), 1) < S_P
    pick = lambda u0, u1: jnp.where(lo, u0, u1)
    return _each(pick, dt0s, dt1s), _each(pick, ac0s, ac1s), _each(pick, ar0s, ar1s)


def ssd_local(xs, dts, acs, acr, bgs, cgs, rev):
    c = xs[0].shape[0]
    npair = len(xs)
    grp = [p * len(bgs) // npair for p in range(npair)]
    lane = _iota((c, 128), 1)
    ii = _iota((c, 128), 0)
    jl = lane & (S_P - 1)
    lo = lane < S_P
    seen = (ii <= jl) if rev else (ii >= jl)
    last = 0 if rev else c - 1
    split = lambda z: jnp.concatenate([jnp.where(lo, z, 0.0), jnp.where(lo, 0.0, z)], axis=0)
    cb = _each(lambda bg, cg: _dot_nt(cg, jnp.concatenate([bg, bg], axis=0)), bgs, cgs)
    seg = _each(lambda ac, ar: jnp.exp(jnp.where(seen, ac - ar, NEG)), acs, acr)
    xdt = _each(lambda x, dt: x * dt, xs, dts)
    ydiag = [_dot(cb[grp[p]] * seg[p], split(xdt[p])) for p in range(npair)]
    eac = [jnp.exp(ac) for ac in acs]
    atot = [jnp.sum(jnp.where(ii == last, ac, 0.0), axis=0, keepdims=True) for ac in acs]
    upd = [_dot_tn(bgs[grp[p]], xdt[p] * jnp.exp(atot[p] - acs[p])) for p in range(npair)]
    return ydiag, eac, [jnp.exp(at) for at in atot], upd


def ssd_state(hts, ydiag, eac, etot, upd, cgs):
    npair = len(hts)
    grp = [p * len(cgs) // npair for p in range(npair)]
    y = [ydiag[p] + _dot(cgs[grp[p]], hts[p]) * eac[p] for p in range(npair)]
    h_new = [hts[p] * etot[p] + upd[p] for p in range(npair)]
    return h_new, y


def ssd_chunk(hts, xs, dts, acs, acr, bgs, cgs, rev):
    ydiag, eac, etot, upd = ssd_local(xs, dts, acs, acr, bgs, cgs, rev)
    return ssd_state(hts, ydiag, eac, etot, upd, cgs)


def f_adamw(w, g, m, v):
    m = ADAM_B1 * m + (1.0 - ADAM_B1) * g
    v = ADAM_B2 * v + (1.0 - ADAM_B2) * jnp.square(g)
    m_hat = m / (1.0 - ADAM_B1 ** ADAM_STEP)
    v_hat = v / (1.0 - ADAM_B2 ** ADAM_STEP)
    delta = -ADAM_LR * (m_hat / (jnp.sqrt(v_hat) + ADAM_EPS) + ADAM_WD * w)
    return delta, m, v


def _cparams(sem):
    return pltpu.CompilerParams(dimension_semantics=sem, vmem_limit_bytes=VMEM_LIMIT)


def _pick(n, target):
    if n <= target:
        return n
    best = None
    for t in range(128, target + 1, 128):
        if n % t == 0:
            best = t
    assert best is not None, (n, target)
    return best


def _row_spec(tm, width, colblk, rowoff):
    return pl.BlockSpec((tm, width), lambda i: (i + rowoff, colblk))


def _par_spec(shape):
    nd = len(shape)
    return pl.BlockSpec(tuple(shape), lambda i: (0,) * nd)


def _rowwise(name, fn, rows, pars, outs, ntiles, base=1, tm=TM):
    nr, npar = len(rows), len(pars)

    def body(*refs):
        is_ctx = (pl.program_id(0) + base) == 0
        res = fn(is_ctx, *[r[...] for r in refs[:nr]], *[p[...] for p in refs[nr:nr + npar]])
        for o_ref, r in zip(refs[nr + npar:], res):
            o_ref[...] = r.astype(o_ref.dtype)

    return pl.pallas_call(
        body, grid=(ntiles,), name=name,
        in_specs=[_row_spec(tm, wd, cb, ro) for (_, wd, cb, ro) in rows] + [_par_spec(p.shape) for p in pars],
        out_specs=[_row_spec(tm, wd, 0, 0) for (wd, _) in outs],
        out_shape=[jax.ShapeDtypeStruct((ntiles * tm, wd), dt) for (wd, dt) in outs],
        compiler_params=_cparams(("arbitrary",)),
    )(*[r[0] for r in rows], *pars)


def _ct_spec(tm, desc):
    _, wd, cb, ro = desc[:4]
    if len(desc) > 4 and desc[4]:
        return pl.BlockSpec((tm, wd), lambda i: (jnp.maximum(i + ro, 0), cb))
    return _row_spec(tm, wd, cb, ro)


def _rowwise_bwd(name, fn, rows, pars, cts, drows, dpars, ntiles, base=1, loss_out=False, tm=TM):
    nr, npar = len(rows), len(pars)
    ct_rows = [d for ct in cts if isinstance(ct, list) for d in ct]
    nct = len(ct_rows)

    def body(*refs):
        i = pl.program_id(0)
        is_ctx = (i + base) == 0
        rows_v = [r[...] for r in refs[:nr]]
        pars_v = [p[...] for p in refs[nr:nr + npar]]
        ct_refs = list(refs[nr + npar:nr + npar + nct])
        out_refs = list(refs[nr + npar + nct:])
        outs, vjp = jax.vjp(lambda rv, pv: fn(is_ctx, *rv, *pv), rows_v, pars_v)

        def ct_value(desc):
            val = ct_refs.pop(0)[...].astype(F32)
            if len(desc) > 4 and desc[4]:
                val = jnp.where(is_ctx, 0.0, val)
            return val

        ct_vals = []
        for o, ct in zip(outs, cts):
            if ct is None:
                ct_vals.append(jnp.zeros_like(o))
            elif isinstance(ct, str):
                ct_vals.append(jnp.ones_like(o))
            else:
                acc = ct_value(ct[0])
                for desc in ct[1:]:
                    acc = acc + ct_value(desc)
                ct_vals.append(acc.astype(o.dtype))
        d_rows, d_pars = vjp(tuple(ct_vals))
        for (ri, _), o_ref in zip(drows, out_refs[:len(drows)]):
            o_ref[...] = d_rows[ri].astype(o_ref.dtype)
        acc_refs = out_refs[len(drows):]
        acc_vals = [d_pars[pi] for pi in dpars]
        if loss_out:
            acc_vals.append(jnp.broadcast_to(outs[0], (8, 128)))

        @pl.when(i == 0)
        def _():
            for o_ref, val in zip(acc_refs, acc_vals):
                o_ref[...] = val

        @pl.when(i > 0)
        def _():
            for o_ref, val in zip(acc_refs, acc_vals):
                o_ref[...] += val

    acc_shapes = [pars[pi].shape for pi in dpars] + ([(8, 128)] if loss_out else [])
    return pl.pallas_call(
        body, grid=(ntiles,), name=name,
        in_specs=[_row_spec(tm, wd, cb, ro) for (_, wd, cb, ro) in rows] + [_par_spec(p.shape) for p in pars]
        + [_ct_spec(tm, d) for d in ct_rows],
        out_specs=[_row_spec(tm, rows[ri][1], 0, 0) for (ri, _) in drows] + [_par_spec(s) for s in acc_shapes],
        out_shape=[jax.ShapeDtypeStruct((ntiles * tm, rows[ri][1]), dt) for (ri, dt) in drows]
        + [jax.ShapeDtypeStruct(tuple(s), F32) for s in acc_shapes],
        compiler_params=_cparams(("arbitrary",)),
    )(*[r[0] for r in rows], *pars, *[r[0] for r in ct_rows])


def _mm(name, a, b, mode, out_dtype, tm=1024, tn=1024, tk=1024, ride=None):
    if mode == "nn":
        (m, kd), (_, n) = a.shape, b.shape
    elif mode == "nt":
        (m, kd), (n, _) = a.shape, b.shape
    else:
        (kd, m), (_, n) = a.shape, b.shape
    tm, tn, tk = _pick(m, tm), _pick(n, tn), _pick(kd, tk)
    nk = kd // tk
    a_spec = {"nn": pl.BlockSpec((tm, tk), lambda i, j, k: (i, k)), "nt": pl.BlockSpec((tm, tk), lambda i, j, k: (i, k)),
              "tn": pl.BlockSpec((tk, tm), lambda i, j, k: (k, i))}[mode]
    b_spec = {"nn": pl.BlockSpec((tk, tn), lambda i, j, k: (k, j)), "nt": pl.BlockSpec((tn, tk), lambda i, j, k: (j, k)),
              "tn": pl.BlockSpec((tk, tn), lambda i, j, k: (k, j))}[mode]
    dot = {"nn": _dot, "nt": _dot_nt, "tn": _dot_tn}[mode]

    if nk == 1:
        def body(a_ref, b_ref, o_ref):
            o_ref[...] = dot(a_ref[...].astype(BF16), b_ref[...].astype(BF16)).astype(o_ref.dtype)
    else:
        def body(a_ref, b_ref, o_ref, acc_ref):
            k = pl.program_id(2)
            part = dot(a_ref[...].astype(BF16), b_ref[...].astype(BF16))

            @pl.when(k == 0)
            def _():
                acc_ref[...] = part

            @pl.when((k > 0) & (k < nk - 1))
            def _():
                acc_ref[...] += part

            @pl.when(k == nk - 1)
            def _():
                o_ref[...] = (acc_ref[...] + part).astype(o_ref.dtype)

    grid = (m // tm, n // tn, nk)
    at = lambda pos: functools.reduce(jnp.logical_and, [pl.program_id(ax) == pos(g) for ax, g in enumerate(grid)])
    r_in, r_out, r_shapes, r_scr, r_ops = _ride_args(ride)
    res = pl.pallas_call(
        _ride(body, ride, 2, 1, lambda: at(lambda g: 0), lambda: at(lambda g: g - 1)), grid=grid, name=name,
        in_specs=[a_spec, b_spec] + r_in,
        out_specs=[pl.BlockSpec((tm, tn), lambda i, j, k: (i, j))] + r_out,
        out_shape=[jax.ShapeDtypeStruct((m, n), out_dtype)] + r_shapes,
        scratch_shapes=([] if nk == 1 else [pltpu.VMEM((tm, tn), F32)]) + r_scr,
        compiler_params=_cparams(("arbitrary", "arbitrary", "arbitrary")),
    )(a, b, *r_ops)
    return res[0] if ride is None else (res[0], res[1:])


def _chunk_index(i, nch, nctx, rev):
    if not rev:
        return i
    return jnp.where(i < nctx, nctx - 1 - i, nch + nctx - 1 - i)


def _gdn_cols(d):
    return [d * G_HEADS + h for h in range(G_HEADS)], [2 * G_HEADS + d * G_HEADS + h for h in range(G_HEADS)]


def _gdn_operands(q_ref, k_ref, v_ref, g_ref, b_ref, d, rows=slice(None)):
    cols_g, cols_b = _gdn_cols(d)
    sls = [slice(h * DK, (h + 1) * DK) for h in range(G_HEADS)]
    gt, bt = g_ref[rows, :], b_ref[rows, :]
    gtt = gt.T
    qs = [q_ref[rows, sl] for sl in sls]
    ks = [k_ref[rows, sl] for sl in sls]
    vs = [v_ref[rows, sl] for sl in sls]
    gcs = [gt[:, cg:cg + 1] for cg in cols_g]
    grs = [gtt[cg:cg + 1, :] for cg in cols_g]
    bcs = [bt[:, cb:cb + 1] for cb in cols_b]
    return sls, qs, ks, vs, gcs, grs, bcs


GDN_FWD_CHUNKS = 4


def _gdn_scan_fwd(name, q, k, v, gcum, beta, d, nctx, ride=None, add=None):
    t = q.shape[0]
    nch = t // CH
    per = GDN_FWD_CHUNKS
    nst = nch // per
    rev = d == 1
    bix = lambda i: _chunk_index(i, nst, nctx // per, rev)
    full = pl.BlockSpec((per * CH, D), lambda i: (bix(i), 0))
    wide = pl.BlockSpec((per * CH, 128), lambda i: (bix(i), 0))
    order = list(range(per - 1, -1, -1)) if rev else list(range(per))

    def body(q_ref, k_ref, v_ref, g_ref, b_ref, *rest):
        add_ref = rest[0] if add is not None else None
        o_ref, ss_ref, ts_ref, s_scr = rest[-4:]

        @pl.when(pl.program_id(0) == 0)
        def _():
            s_scr[...] = jnp.zeros(s_scr.shape, F32)

        ops = [_gdn_operands(q_ref, k_ref, v_ref, g_ref, b_ref, d, slice(c * CH, (c + 1) * CH)) for c in order]
        sls = ops[0][0]
        cat = [sum((o[i] for o in ops), []) for i in range(1, 7)]
        wu, attn, lhs, tinv = gdn_local(*cat, rev)
        ss = [s_scr[h] for h in range(G_HEADS)]
        for n, c in enumerate(order):
            sl = slice(n * G_HEADS, (n + 1) * G_HEADS)
            s_new, o = gdn_state(ss, wu[sl], attn[sl], lhs[sl], cat[1][sl], cat[3][sl], rev)
            for h in range(G_HEADS):
                ss_ref[n, h] = ss[h]
                ts_ref[n, h] = tinv[n * G_HEADS + h]
                rows = slice(c * CH, (c + 1) * CH)
                o_ref[rows, sls[h]] = o[h] if add is None else o[h] + add_ref[rows, sls[h]]
            ss = s_new
        for h in range(G_HEADS):
            s_scr[h] = ss[h]

    r_in, r_out, r_shapes, r_scr, r_ops = _ride_args(ride)
    extra = [] if add is None else [add]
    res = pl.pallas_call(
        _ride(body, ride, 5 + len(extra), 3, lambda: pl.program_id(0) == 0, lambda: pl.program_id(0) == nst - 1), grid=(nst,), name=name,
        in_specs=[full, full, full, wide, wide] + [full] * len(extra) + r_in,
        out_specs=[full, pl.BlockSpec((per, G_HEADS, DK, DK), lambda i: (i, 0, 0, 0)),
                   pl.BlockSpec((per, G_HEADS, CH, CH), lambda i: (i, 0, 0, 0))] + r_out,
        out_shape=[jax.ShapeDtypeStruct((t, D), F32), jax.ShapeDtypeStruct((nch, G_HEADS, DK, DK), F32),
                   jax.ShapeDtypeStruct((nch, G_HEADS, CH, CH), F32)] + r_shapes,
        scratch_shapes=[pltpu.VMEM((G_HEADS, DK, DK), F32)] + r_scr,
        compiler_params=_cparams(("arbitrary",)),
    )(q, k, v, gcum, beta, *extra, *r_ops)
    return res if ride is None else (*res[:3], res[3:])


def _gdn_scan_bwd(name, q, k, v, gcum, beta, ssave, tsave, do, d, nctx, ride=None):
    t = q.shape[0]
    nch = t // CH
    rev = d == 1
    cix = lambda ib: _chunk_index(nch - 1 - ib, nch, nctx, rev)
    full = pl.BlockSpec((CH, D), lambda ib: (cix(ib), 0))
    wide = pl.BlockSpec((CH, 128), lambda ib: (cix(ib), 0))
    do_spec = pl.BlockSpec((CH, D), lambda ib: (jnp.maximum(cix(ib), nctx) - nctx, 0))

    def body(q_ref, k_ref, v_ref, g_ref, b_ref, ss_ref, ts_ref, do_ref, dq_ref, dk_ref, dv_ref, dg_ref, db_ref, ds_scr):
        ib = pl.program_id(0)

        @pl.when(ib == 0)
        def _():
            ds_scr[...] = jnp.zeros(ds_scr.shape, F32)

        sls, qs, ks, vs, gcs, grs, bcs = _gdn_operands(q_ref, k_ref, v_ref, g_ref, b_ref, d)
        t_known = [ts_ref[0, h] for h in range(G_HEADS)]
        cols_g, cols_b = _gdn_cols(d)
        is_lat = cix(ib) >= nctx
        ss = [ss_ref[0, h] for h in range(G_HEADS)]
        do_v = [jnp.where(is_lat, do_ref[:, sl], 0.0) for sl in sls]
        ds_in = [ds_scr[h] for h in range(G_HEADS)]
        _, vjp = jax.vjp(lambda *a: gdn_chunk(*a, rev, t_known)[:2], ss, qs, ks, vs, gcs, grs, bcs)
        ds, dq, dk, dv, dgc, dgr, dbc = vjp((ds_in, do_v))
        lane = _iota((CH, 128), 1)
        sub = _iota((128, CH), 0)
        dg = jnp.zeros((CH, 128), F32)
        dgt = jnp.zeros((128, CH), F32)
        db = jnp.zeros((CH, 128), F32)
        for h in range(G_HEADS):
            ds_scr[h] = ds[h]
            dq_ref[:, sls[h]] = dq[h]
            dk_ref[:, sls[h]] = dk[h]
            dv_ref[:, sls[h]] = dv[h]
            dg = dg + jnp.where(lane == cols_g[h], dgc[h], 0.0)
            dgt = dgt + jnp.where(sub == cols_g[h], dgr[h], 0.0)
            db = db + jnp.where(lane == cols_b[h], dbc[h], 0.0)
        dg_ref[...] = dg + dgt.T
        db_ref[...] = db

    r_in, r_out, r_shapes, r_scr, r_ops = _ride_args(ride)
    res = pl.pallas_call(
        _ride(body, ride, 8, 5, lambda: pl.program_id(0) == 0, lambda: pl.program_id(0) == nch - 1), grid=(nch,), name=name,
        in_specs=[full, full, full, wide, wide,
                  pl.BlockSpec((1, G_HEADS, DK, DK), lambda ib: (nch - 1 - ib, 0, 0, 0)),
                  pl.BlockSpec((1, G_HEADS, CH, CH), lambda ib: (nch - 1 - ib, 0, 0, 0)), do_spec] + r_in,
        out_specs=[full, full, full, wide, wide] + r_out,
        out_shape=[jax.ShapeDtypeStruct((t, D), F32)] * 3 + [jax.ShapeDtypeStruct((t, 128), F32)] * 2 + r_shapes,
        scratch_shapes=[pltpu.VMEM((G_HEADS, DK, DK), F32)] + r_scr,
        compiler_params=_cparams(("arbitrary",)),
    )(q, k, v, gcum, beta, ssave, tsave, do, *r_ops)
    return res if ride is None else (*res[:5], res[5:])


N_PAIRS = S_HEADS // 2


def _ssd_operands(x_ref, dt_ref, ac_ref, b_ref, c_ref, d, rows=slice(None)):
    sls = [slice(p * 128, (p + 1) * 128) for p in range(N_PAIRS)]
    gsl = [slice(g * S_N, (g + 1) * S_N) for g in range(S_GROUPS)]
    cols = [d * S_HEADS + h for h in range(S_HEADS)]
    dtc, acc = dt_ref[rows, :], ac_ref[rows, :]
    act = jnp.concatenate([acc, acc], axis=0).T
    col = lambda z, cc: z[:, cc:cc + 1]
    dts, acs, acr = ssd_pick(
        [col(dtc, cols[2 * p]) for p in range(N_PAIRS)], [col(dtc, cols[2 * p + 1]) for p in range(N_PAIRS)],
        [col(acc, cols[2 * p]) for p in range(N_PAIRS)], [col(acc, cols[2 * p + 1]) for p in range(N_PAIRS)],
        [act[cols[2 * p]:cols[2 * p] + 1, :] for p in range(N_PAIRS)],
        [act[cols[2 * p + 1]:cols[2 * p + 1] + 1, :] for p in range(N_PAIRS)])
    ops = ([x_ref[rows, sl] for sl in sls], dts, acs, acr, [b_ref[rows, gs] for gs in gsl], [c_ref[rows, gs] for gs in gsl])
    return sls, gsl, cols, ops


SSD_FWD_CHUNKS = 4


def _ssd_scan_fwd(name, xs, dtc, acc, bm, cm, d, nctx, add=None):
    t = xs.shape[0]
    nch = t // CH
    per = SSD_FWD_CHUNKS
    nst = nch // per
    rev = d == 1
    bix = lambda i: _chunk_index(i, nst, nctx // per, rev)
    inner = pl.BlockSpec((per * CH, S_INNER), lambda i: (bix(i), 0))
    wide = pl.BlockSpec((per * CH, 128), lambda i: (bix(i), 0))
    grp = pl.BlockSpec((per * CH, S_GROUPS * S_N), lambda i: (bix(i), 0))
    order = list(range(per - 1, -1, -1)) if rev else list(range(per))

    def body(x_ref, dt_ref, ac_ref, b_ref, c_ref, *rest):
        add_ref = rest[0] if add is not None else None
        y_ref, hs_ref, h_scr = rest[-3:]

        @pl.when(pl.program_id(0) == 0)
        def _():
            h_scr[...] = jnp.zeros(h_scr.shape, F32)

        loc, cgs, sls = [], [], None
        for c in order:
            sls, _, _, ops = _ssd_operands(x_ref, dt_ref, ac_ref, b_ref, c_ref, d, slice(c * CH, (c + 1) * CH))
            loc.append(ssd_local(*ops, rev))
            cgs.append(ops[5])
        hts = [h_scr[p] for p in range(N_PAIRS)]
        for n, c in enumerate(order):
            h_new, y = ssd_state(hts, *loc[n], cgs[n])
            for p in range(N_PAIRS):
                hs_ref[n, p] = hts[p]
                rows = slice(c * CH, (c + 1) * CH)
                y_ref[rows, sls[p]] = y[p] if add is None else y[p] + add_ref[rows, sls[p]]
            hts = h_new
        for p in range(N_PAIRS):
            h_scr[p] = hts[p]

    return pl.pallas_call(
        body, grid=(nst,), name=name,
        in_specs=[inner, wide, wide, grp, grp] + ([] if add is None else [inner]),
        out_specs=[inner, pl.BlockSpec((per, N_PAIRS, S_N, 128), lambda i: (i, 0, 0, 0))],
        out_shape=[jax.ShapeDtypeStruct((t, S_INNER), F32), jax.ShapeDtypeStruct((nch, N_PAIRS, S_N, 128), F32)],
        scratch_shapes=[pltpu.VMEM((N_PAIRS, S_N, 128), F32)],
        compiler_params=_cparams(("arbitrary",)),
    )(xs, dtc, acc, bm, cm, *([] if add is None else [add]))


def _ssd_scan_bwd(name, xs, dtc, acc, bm, cm, hsave, dy, d, nctx):
    t = xs.shape[0]
    nch = t // CH
    rev = d == 1
    cix = lambda ib: _chunk_index(nch - 1 - ib, nch, nctx, rev)
    inner = pl.BlockSpec((CH, S_INNER), lambda ib: (cix(ib), 0))
    wide = pl.BlockSpec((CH, 128), lambda ib: (cix(ib), 0))
    grp = pl.BlockSpec((CH, S_GROUPS * S_N), lambda ib: (cix(ib), 0))
    dy_spec = pl.BlockSpec((CH, S_INNER), lambda ib: (jnp.maximum(cix(ib), nctx) - nctx, 0))

    def body(x_ref, dt_ref, ac_ref, b_ref, c_ref, hs_ref, dy_ref, dx_ref, ddt_ref, dac_ref, db_ref, dc_ref, dh_scr):
        ib = pl.program_id(0)

        @pl.when(ib == 0)
        def _():
            dh_scr[...] = jnp.zeros(dh_scr.shape, F32)

        sls, gsl, cols, ops = _ssd_operands(x_ref, dt_ref, ac_ref, b_ref, c_ref, d)
        is_lat = cix(ib) >= nctx
        hts = [hs_ref[0, p] for p in range(N_PAIRS)]
        dy_v = [jnp.where(is_lat, dy_ref[:, sl], 0.0) for sl in sls]
        dh_in = [dh_scr[p] for p in range(N_PAIRS)]
        _, vjp = jax.vjp(lambda *a: ssd_chunk(*a, rev), hts, *ops)
        dh, dx, ddts, dacs, dacr, db, dc = vjp((dh_in, dy_v))
        for p in range(N_PAIRS):
            dh_scr[p] = dh[p]
            dx_ref[:, sls[p]] = dx[p]
        r = _iota((S_INNER, 128), 0)
        e_t = jnp.where(_iota((S_INNER, 128), 1) == d * S_HEADS + r // S_P, 1.0, 0.0).astype(F32)
        ddt_ref[...] = _dot(jnp.concatenate(ddts, axis=1), e_t, HIGH)
        dac_cols = _dot(jnp.concatenate(dacs, axis=1), e_t, HIGH)
        sub = _iota((128, 128), 0)
        lane = _iota((128, 128), 1)
        m = jnp.zeros((128, 128), F32)
        for p in range(N_PAIRS):
            m = m + jnp.where(sub == p, jnp.sum(dacr[p], axis=0, keepdims=True), 0.0)
        mt = m.T
        s0 = jnp.where(lane == d * S_HEADS + 2 * sub, 1.0, 0.0).astype(F32)
        s1 = jnp.where(lane == d * S_HEADS + 2 * sub + 1, 1.0, 0.0).astype(F32)
        dac_ref[...] = dac_cols + _dot(mt[:CH], s0, HIGH) + _dot(mt[CH:], s1, HIGH)
        for g in range(S_GROUPS):
            db_ref[:, gsl[g]] = db[g]
            dc_ref[:, gsl[g]] = dc[g]

    return pl.pallas_call(
        body, grid=(nch,), name=name,
        in_specs=[inner, wide, wide, grp, grp,
                  pl.BlockSpec((1, N_PAIRS, S_N, 128), lambda ib: (nch - 1 - ib, 0, 0, 0)), dy_spec],
        out_specs=[inner, wide, wide, grp, grp],
        out_shape=[jax.ShapeDtypeStruct((t, S_INNER), F32)] + [jax.ShapeDtypeStruct((t, 128), F32)] * 2
        + [jax.ShapeDtypeStruct((t, S_GROUPS * S_N), F32)] * 2,
        scratch_shapes=[pltpu.VMEM((N_PAIRS, S_N, 128), F32)],
        compiler_params=_cparams(("arbitrary",)),
    )(xs, dtc, acc, bm, cm, hsave, dy)


def _mesh_pos():
    return lax.axis_index("x"), lax.axis_index("y"), lax.axis_index("c")


def _hbm_specs(n):
    return [pl.BlockSpec(memory_space=pl.ANY)] * n


def _sem_shapes(nw):
    return [pltpu.SemaphoreType.DMA((nw, 7)), pltpu.SemaphoreType.DMA((nw, 7)), pltpu.SemaphoreType.DMA((nw,))]


class _AllGather:
    def __init__(self, shards):
        self.arrays = list(shards)
        self.out_shapes = [jax.ShapeDtypeStruct((N_DEV,) + xs.shape, xs.dtype) for xs in shards]

    def _parts(self, x_refs, out_refs, sems):
        send_sems, recv_sems, local_sems = sems
        nw = len(x_refs)
        x, y, c = _mesh_pos()
        me, sibling = (x, y, c), (x, y, 1 - c)
        chips = [(1 - x, y), (x, 1 - y), (1 - x, 1 - y)]

        def slot(w, px, py, pc):
            return out_refs[w].at[4 * px + 2 * py + pc]

        def copy(w, k, block, to, src=None):
            return pltpu.make_async_remote_copy(
                src_ref=slot(w, *block) if src is None else src, dst_ref=slot(w, *block),
                send_sem=send_sems.at[w, k], recv_sem=recv_sems.at[w, k], device_id=to, device_id_type=MESH)

        mine = [pltpu.make_async_copy(x_refs[w], slot(w, *me), local_sems.at[w]) for w in range(nw)]
        first = []
        for w in range(nw):
            first.append(copy(w, 0, me, sibling, src=x_refs[w]))
            first += [copy(w, 1 + j, me, (*chip, c), src=x_refs[w]) for j, chip in enumerate(chips)]
        return nw, me, sibling, chips, c, copy, mine, first

    def start(self, x_refs, out_refs, sems):
        _, _, _, _, _, _, mine, first = self._parts(x_refs, out_refs, sems)
        for cp in mine + first:
            cp.start()

    def finish(self, x_refs, out_refs, sems):
        nw, me, sibling, chips, c, copy, mine, first = self._parts(x_refs, out_refs, sems)
        passed = []
        for j, chip in enumerate(chips):
            for w in range(nw):
                copy(w, 1 + j, (*chip, c), me).wait_recv()
                fwd = copy(w, 4 + j, (*chip, c), sibling)
                fwd.start()
                passed.append(fwd)
        for w in range(nw):
            copy(w, 0, sibling, me).wait_recv()
            for j, chip in enumerate(chips):
                copy(w, 4 + j, (*chip, 1 - c), me).wait_recv()
        for cp in first + passed:
            cp.wait_send()
        for cp in mine:
            cp.wait()


class _AllToAll:
    def __init__(self, blocks):
        self.arrays = list(blocks)
        self.out_shapes = [jax.ShapeDtypeStruct(g.shape, g.dtype) for g in blocks]

    def _parts(self, g_refs, out_refs, sems):
        send_sems, recv_sems, local_sems = sems
        nw = len(g_refs)
        x, y, c = _mesh_pos()
        me_i = 4 * x + 2 * y + c
        mine = [pltpu.make_async_copy(g_refs[w].at[me_i], out_refs[w].at[me_i], local_sems.at[w]) for w in range(nw)]
        cps = []
        for k in range(1, N_DEV):
            px = 1 - x if (k >> 2) & 1 else x
            py = 1 - y if (k >> 1) & 1 else y
            pc = 1 - c if k & 1 else c
            for w in range(nw):
                cps.append(pltpu.make_async_remote_copy(
                    src_ref=g_refs[w].at[4 * px + 2 * py + pc], dst_ref=out_refs[w].at[me_i],
                    send_sem=send_sems.at[w, k - 1], recv_sem=recv_sems.at[w, k - 1],
                    device_id=(px, py, pc), device_id_type=MESH))
        return mine, cps

    def start(self, g_refs, out_refs, sems):
        mine, cps = self._parts(g_refs, out_refs, sems)
        for cp in mine + cps:
            cp.start()

    def finish(self, g_refs, out_refs, sems):
        mine, cps = self._parts(g_refs, out_refs, sems)
        for cp in cps + mine:
            cp.wait()


def _exchange(name, ex):
    nw = len(ex.arrays)

    def body(*refs):
        ins, outs, sems = refs[:nw], refs[nw:2 * nw], refs[2 * nw:]
        ex.start(ins, outs, sems)
        ex.finish(ins, outs, sems)

    return pl.pallas_call(body, name=name, out_shape=ex.out_shapes, in_specs=_hbm_specs(nw), out_specs=_hbm_specs(nw),
                          scratch_shapes=_sem_shapes(nw))(*ex.arrays)


def _ride(body, ex, n_in, n_out, is_first, is_last):
    if ex is None:
        return body
    nw = len(ex.arrays)

    def riding(*refs):
        ins, ex_in = refs[:n_in], refs[n_in:n_in + nw]
        outs = refs[n_in + nw:n_in + nw + n_out]
        ex_out = refs[n_in + nw + n_out:n_in + 2 * nw + n_out]
        rest = refs[n_in + 2 * nw + n_out:]
        scratch, sems = rest[:len(rest) - 3], rest[len(rest) - 3:]

        @pl.when(is_first())
        def _():
            ex.start(ex_in, ex_out, sems)

        body(*ins, *outs, *scratch)

        @pl.when(is_last())
        def _():
            ex.finish(ex_in, ex_out, sems)

    return riding


def _ride_args(ex):
    if ex is None:
        return [], [], [], [], []
    nw = len(ex.arrays)
    return _hbm_specs(nw), _hbm_specs(nw), list(ex.out_shapes), _sem_shapes(nw), list(ex.arrays)


def _reduce_adam(name, recv, w, m, v, tm):
    rows, width = w.shape
    nslot = recv.shape[0]

    def body(recv_ref, w_ref, m_ref, v_ref, g_ref, d_ref, m2_ref, v2_ref):
        g = recv_ref[0].astype(F32)
        for s in range(1, nslot):
            g = g + recv_ref[s].astype(F32)
        delta, m2, v2 = f_adamw(w_ref[...], g, m_ref[...], v_ref[...])
        g_ref[...] = g
        d_ref[...] = delta
        m2_ref[...] = m2
        v2_ref[...] = v2

    row = pl.BlockSpec((tm, width), lambda i: (i, 0))
    return pl.pallas_call(
        body, grid=(rows // tm,), name=name,
        in_specs=[pl.BlockSpec((nslot, tm, width), lambda i: (0, i, 0)), row, row, row],
        out_specs=[row] * 4,
        out_shape=[jax.ShapeDtypeStruct((rows, width), F32)] * 4,
        compiler_params=_cparams(("arbitrary",)),
    )(recv, w, m, v)


BIG = ("w_in", "ada_w", "w_br_gdn", "w_br_ssm", "w_out", "w_ffn_in", "w_ffn_out")
BIG_FIRST = ("ada_w", "w_in")
BIG_LATE = ("w_br_gdn", "w_br_ssm", "w_out", "w_ffn_in", "w_ffn_out")
BIG_COL_SHARDED = ("w_in", "ada_w", "w_ffn_in")
BIG_ADAM_ROWS = dict(w_in=128, ada_w=256, w_br_gdn=128, w_br_ssm=256, w_out=128, w_ffn_in=256, w_ffn_out=352)
CONV = ("gdn_conv_w", "ssm_conv_w")
SMALL = ("c_ctx", "ada_b", "norm1_w", "gdn_conv_b", "gdn_a_log", "gdn_dt_bias", "gdn_norm_w", "ssm_conv_b",
         "ssm_a_log", "ssm_dt_bias", "ssm_d", "ssm_norm_w", "norm2_w", "norm_f_w")
CONV_SHARD = XBC // N_DEV


def _to_rows(a):
    flat = a.reshape(-1)
    pad = (-flat.shape[0]) % PACK_W
    if pad:
        flat = jnp.pad(flat, (0, pad))
    return flat.reshape(-1, PACK_W)


def _pack(arrays, rows=None):
    buf = jnp.concatenate([_to_rows(a) for a in arrays], axis=0)
    if rows is not None and rows > buf.shape[0]:
        buf = jnp.pad(buf, ((0, rows - buf.shape[0]), (0, 0)))
    return buf


def _unpack(buf, shapes):
    out, r0 = [], 0
    for shp in shapes:
        n = 1
        for s in shp:
            n *= s
        nr = -(-n // PACK_W)
        out.append(buf[r0:r0 + nr].reshape(-1)[:n].reshape(shp))
        r0 += nr
    return out


def _full_from_blocks(blocks, col_sharded):
    _, r, c = blocks.shape
    if col_sharded:
        return jnp.transpose(blocks, (1, 0, 2)).reshape(r, N_DEV * c)
    return blocks.reshape(N_DEV * r, c)


def _blocks_from_full(full, col_sharded):
    if col_sharded:
        r, c = full.shape[0], full.shape[1] // N_DEV
        return jnp.transpose(full.reshape(r, N_DEV, c), (1, 0, 2))
    return full.reshape(N_DEV, full.shape[0] // N_DEV, full.shape[1])


def _pad_cols(a, n):
    return jnp.pad(a, ((0, 0), (0, n - a.shape[1])))


def _w_cat(w_in):
    return jnp.concatenate([
        w_in[:, O_QKV:O_ZG], w_in[:, O_XBC:O_DT], w_in[:, O_ZS:O_XBC], w_in[:, O_GATE:O_END], w_in[:, O_ZG:O_AB],
        _pad_cols(w_in[:, O_AB:O_ZS], 128), _pad_cols(w_in[:, O_DT:O_GATE], 128)], axis=1)


def _w_uncat(wc):
    return jnp.concatenate([
        wc[:, C_QKV:C_XBC], wc[:, C_ZG:C_AB], wc[:, C_AB:C_AB + (O_ZS - O_AB)], wc[:, C_ZS:C_GATE], wc[:, C_XBC:C_ZS],
        wc[:, C_DT:C_DT + (O_GATE - O_DT)], wc[:, C_GATE:C_ZG]], axis=1)


def _pad_row(vec, n=128):
    vec = vec.reshape(1, -1)
    return _pad_cols(vec, n)


def kernel(x, c, ctx, c_ctx, ada_w, ada_b, norm1_w, w_in, gdn_conv_w, gdn_conv_b, gdn_a_log, gdn_dt_bias, gdn_norm_w, ssm_conv_w, ssm_conv_b, ssm_a_log, ssm_dt_bias, ssm_d, ssm_norm_w, w_br_gdn, w_br_ssm, w_out, norm2_w, w_ffn_in, w_ffn_out, norm_f_w, loss_target, m_c_ctx, m_ada_w, m_ada_b, m_norm1_w, m_w_in, m_gdn_conv_w, m_gdn_conv_b, m_gdn_a_log, m_gdn_dt_bias, m_gdn_norm_w, m_ssm_conv_w, m_ssm_conv_b, m_ssm_a_log, m_ssm_dt_bias, m_ssm_d, m_ssm_norm_w, m_w_br_gdn, m_w_br_ssm, m_w_out, m_norm2_w, m_w_ffn_in, m_w_ffn_out, m_norm_f_w, v_c_ctx, v_ada_w, v_ada_b, v_norm1_w, v_w_in, v_gdn_conv_w, v_gdn_conv_b, v_gdn_a_log, v_gdn_dt_bias, v_gdn_norm_w, v_ssm_conv_w, v_ssm_conv_b, v_ssm_a_log, v_ssm_dt_bias, v_ssm_d, v_ssm_norm_w, v_w_br_gdn, v_w_br_ssm, v_w_out, v_norm2_w, v_w_ffn_in, v_w_ffn_out, v_norm_f_w):
    wts = dict(c_ctx=c_ctx, ada_w=ada_w, ada_b=ada_b, norm1_w=norm1_w, w_in=w_in, gdn_conv_w=gdn_conv_w, gdn_conv_b=gdn_conv_b, gdn_a_log=gdn_a_log, gdn_dt_bias=gdn_dt_bias, gdn_norm_w=gdn_norm_w, ssm_conv_w=ssm_conv_w, ssm_conv_b=ssm_conv_b, ssm_a_log=ssm_a_log, ssm_dt_bias=ssm_dt_bias, ssm_d=ssm_d, ssm_norm_w=ssm_norm_w, w_br_gdn=w_br_gdn, w_br_ssm=w_br_ssm, w_out=w_out, norm2_w=norm2_w, w_ffn_in=w_ffn_in, w_ffn_out=w_ffn_out, norm_f_w=norm_f_w)
    mom1 = dict(c_ctx=m_c_ctx, ada_w=m_ada_w, ada_b=m_ada_b, norm1_w=m_norm1_w, w_in=m_w_in, gdn_conv_w=m_gdn_conv_w, gdn_conv_b=m_gdn_conv_b, gdn_a_log=m_gdn_a_log, gdn_dt_bias=m_gdn_dt_bias, gdn_norm_w=m_gdn_norm_w, ssm_conv_w=m_ssm_conv_w, ssm_conv_b=m_ssm_conv_b, ssm_a_log=m_ssm_a_log, ssm_dt_bias=m_ssm_dt_bias, ssm_d=m_ssm_d, ssm_norm_w=m_ssm_norm_w, w_br_gdn=m_w_br_gdn, w_br_ssm=m_w_br_ssm, w_out=m_w_out, norm2_w=m_norm2_w, w_ffn_in=m_w_ffn_in, w_ffn_out=m_w_ffn_out, norm_f_w=m_norm_f_w)
    mom2 = dict(c_ctx=v_c_ctx, ada_w=v_ada_w, ada_b=v_ada_b, norm1_w=v_norm1_w, w_in=v_w_in, gdn_conv_w=v_gdn_conv_w, gdn_conv_b=v_gdn_conv_b, gdn_a_log=v_gdn_a_log, gdn_dt_bias=v_gdn_dt_bias, gdn_norm_w=v_gdn_norm_w, ssm_conv_w=v_ssm_conv_w, ssm_conv_b=v_ssm_conv_b, ssm_a_log=v_ssm_a_log, ssm_dt_bias=v_ssm_dt_bias, ssm_d=v_ssm_d, ssm_norm_w=v_ssm_norm_w, w_br_gdn=v_w_br_gdn, w_br_ssm=v_w_br_ssm, w_out=v_w_out, norm2_w=v_norm2_w, w_ffn_in=v_w_ffn_in, w_ffn_out=v_w_ffn_out, norm_f_w=v_norm_f_w)
    order = list(wts)

    seq = x.shape[1]
    t = TM + seq
    ntl, nlt, nctx = t // TM, seq // TM, TM // CH

    me_i = 4 * lax.axis_index("x") + 2 * lax.axis_index("y") + lax.axis_index("c")
    gathered = _exchange("ag_weights", _AllGather([wts[n][0].astype(BF16) for n in BIG_FIRST]))
    full = {n: _full_from_blocks(blk, n in BIG_COL_SHARDED) for n, blk in zip(BIG_FIRST, gathered)}
    late_gather = _AllGather([wts[n][0].astype(BF16) for n in BIG_LATE])
    conv_sh = _pack([wts[n] for n in CONV], rows=8)
    conv_g = _exchange("ag_conv", _AllGather([conv_sh]))[0].reshape(N_DEV, -1)
    ncv = 3 * CONV_SHARD
    for i, n in enumerate(CONV):
        off = -(-ncv // PACK_W) * PACK_W * i
        full[n] = jnp.transpose(conv_g[:, off:off + ncv].reshape(N_DEV, 3, CONV_SHARD), (1, 0, 2)).reshape(3, XBC)
    w_cat = _w_cat(full["w_in"])
    gcw = full["gdn_conv_w"].reshape(3, 1, XBC)
    scw = full["ssm_conv_w"].reshape(3, 1, XBC)

    n1w, n2w, nfw = norm1_w.reshape(1, D), norm2_w.reshape(1, D), norm_f_w.reshape(1, D)
    gcb, scb = gdn_conv_b.reshape(1, XBC), ssm_conv_b.reshape(1, XBC)
    alog16, dtb16 = _pad_row(gdn_a_log), _pad_row(gdn_dt_bias)
    alog64, dtb64 = _pad_row(ssm_a_log), _pad_row(ssm_dt_bias)
    gnw = gdn_norm_w.reshape(1, DK)
    ssd8 = jnp.tile(_pad_row(ssm_d), (8, 1))
    snw = ssm_norm_w.reshape(1, S_INNER)
    x2 = x[0]
    tgt = loss_target[0]
    xa = jnp.concatenate([ctx[0], x2], axis=0)
    cvec = jnp.concatenate([c, c_ctx.reshape(1, D), jnp.zeros((14, D), F32)], axis=0)

    a16 = _rowwise("silu_c", f_silu_rows, [(cvec, D, 0, 0)], [], [(D, BF16)], 1, tm=16)[0]
    mod = _mm("mm_mod", a16, full["ada_w"], "nn", F32) + ada_b
    sh1, sc1, g1, sh2, sc2, g2 = [mod[0:1, i * D:(i + 1) * D] for i in range(6)]
    csh1, csc1 = mod[1:2, 0:D], mod[1:2, D:2 * D]

    pre_pars = [n1w, sc1, sh1, csc1, csh1]
    a = _rowwise("pre", f_pre, [(xa, D, 0, 0)], pre_pars, [(D, BF16)], ntl, base=0)[0]
    proj = _mm("mm_proj", a, w_cat, "nn", F32, tm=1408, tn=1280)
    gp_rows = [(proj, XBC, C_QKV // XBC, 0), (proj, 128, C_AB // 128, 0)]
    gp_pars = [gcw, gcb, alog16, dtb16]
    q, k, v, gcum, beta = _rowwise("gdnprep", f_gdnprep, gp_rows, gp_pars, [(D, F32)] * 3 + [(128, F32)] * 2, ntl, base=0)
    sp_rows = [(proj, XBC, C_XBC // XBC, 0), (proj, 128, C_DT // 128, 0)]
    sp_pars = [scw, scb, alog64, dtb64]
    xs, bm, cm, dtc, acc = _rowwise(
        "ssmprep", f_ssmprep, sp_rows, sp_pars, [(S_INNER, F32), (512, F32), (512, F32), (128, F32), (128, F32)], ntl, base=0)
    o0, ss0, ts0, gathered = _gdn_scan_fwd("gdn_fwd0", q, k, v, gcum, beta, 0, nctx, ride=late_gather)
    full.update({n: _full_from_blocks(blk, n in BIG_COL_SHARDED) for n, blk in zip(BIG_LATE, gathered)})
    o_sum, ss1, ts1 = _gdn_scan_fwd("gdn_fwd1", q, k, v, gcum, beta, 1, nctx, add=o0)
    y0, hs0 = _ssd_scan_fwd("ssd_fwd0", xs, dtc, acc, bm, cm, 0, nctx)
    y_sum, hs1 = _ssd_scan_fwd("ssd_fwd1", xs, dtc, acc, bm, cm, 1, nctx, add=y0)
    post_rows = [(o_sum, D, 0, 1), (proj, D, C_ZG // D, 1), (y_sum, S_INNER, 0, 1), (xs, S_INNER, 0, 1),
                 (proj, S_INNER, C_ZS // S_INNER, 1)]
    post_pars = [gnw, ssd8, snw]
    og, ys = _rowwise("post", f_post, post_rows, post_pars, [(D, BF16), (S_INNER, BF16)], nlt)
    pg = _mm("mm_pg", og, full["w_br_gdn"], "nn", F32)
    ps = _mm("mm_ps", ys, full["w_br_ssm"], "nn", F32, tk=2048)
    merge_rows = [(proj, S_INNER, C_GATE // S_INNER, 1), (pg, D, 0, 0), (ps, D, 0, 0)]
    merged = _rowwise("merge", f_merge, merge_rows, [], [(D, BF16)], nlt)[0]
    mix = _mm("mm_mix", merged, full["w_out"], "nn", F32)
    res_rows = [(x2, D, 0, 0), (mix, D, 0, 0)]
    res_pars = [g1, n2w, sc2, sh2]
    h1, f = _rowwise("res1", f_res1, res_rows, res_pars, [(D, F32), (D, BF16)], nlt)
    u = _mm("mm_u", f, full["w_ffn_in"], "nn", F32, tn=1408)
    hact = _rowwise("act", f_act, [(u, 2 * D_FF, 0, 0)], [], [(D_FF, BF16)], nlt)[0]
    ff = _mm("mm_ff", hact, full["w_ffn_out"], "nn", F32, tk=2816)

    fin_rows = [(h1, D, 0, 0), (ff, D, 0, 0), (tgt, D, 0, 0)]
    d_h1a, d_ff, d_g2, d_nfw, loss_acc = _rowwise_bwd(
        "final", f_final, fin_rows, [g2, nfw], ["one"], [(0, F32), (1, BF16)], [0, 1], nlt, loss_out=True)
    d_hact = _mm("mm_dhact", d_ff, full["w_ffn_out"], "nt", BF16, tn=1408)
    g_w_ffn_out = _mm("mm_gwffo", hact, d_ff, "tn", BF16, tm=1408, tk=2048)
    d_u = _rowwise_bwd("act_bwd", f_act, [(u, 2 * D_FF, 0, 0)], [], [[(d_hact, D_FF, 0, 0)]], [(0, BF16)], [], nlt)[0]
    d_f = _mm("mm_df", d_u, full["w_ffn_in"], "nt", BF16, tk=2816)
    g_w_ffn_in = _mm("mm_gwffi", f, d_u, "tn", BF16, tn=1408, tk=2048)
    d_xres, d_mix, d_g1, d_n2w, d_sc2, d_sh2 = _rowwise_bwd(
        "res1_bwd", f_res1, res_rows, res_pars, [[(d_h1a, D, 0, 0)], [(d_f, D, 0, 0)]], [(0, F32), (1, BF16)], [0, 1, 2, 3], nlt)
    d_merged = _mm("mm_dmerged", d_mix, full["w_out"], "nt", BF16)
    g_w_out = _mm("mm_gwout", merged, d_mix, "tn", BF16, tk=2048)
    d_gate, d_pg, d_ps = _rowwise_bwd(
        "merge_bwd", f_merge, merge_rows, [], [[(d_merged, D, 0, 0)]], [(0, BF16), (1, BF16), (2, BF16)], [], nlt)
    d_og = _mm("mm_dog", d_pg, full["w_br_gdn"], "nt", BF16)
    g_w_br_gdn = _mm("mm_gwbrg", og, d_pg, "tn", BF16, tk=2048)
    d_ys = _mm("mm_dys", d_ps, full["w_br_ssm"], "nt", BF16, tn=2048)
    g_w_br_ssm = _mm("mm_gwbrs", ys, d_ps, "tn", BF16, tk=2048)
    d_o, d_zg, d_y, d_xs_post, d_zs, d_gnw, d_ssd8, d_snw = _rowwise_bwd(
        "post_bwd", f_post, post_rows, post_pars, [[(d_og, D, 0, 0)], [(d_ys, S_INNER, 0, 0)]],
        [(0, F32), (1, BF16), (2, F32), (3, F32), (4, BF16)], [0, 1, 2], nlt)
    late_grads = dict(w_br_gdn=g_w_br_gdn, w_br_ssm=g_w_br_ssm, w_out=g_w_out, w_ffn_in=g_w_ffn_in, w_ffn_out=g_w_ffn_out)
    late_a2a = _AllToAll([_blocks_from_full(late_grads[n], n in BIG_COL_SHARDED) for n in BIG_LATE])
    dq0, dk0, dv0, dg0, db0, recv_late = _gdn_scan_bwd("gdn_bwd0", q, k, v, gcum, beta, ss0, ts0, d_o, 0, nctx, ride=late_a2a)
    dq1, dk1, dv1, dg1, db1 = _gdn_scan_bwd("gdn_bwd1", q, k, v, gcum, beta, ss1, ts1, d_o, 1, nctx)
    dxs0, ddt0, dac0, dbm0, dcm0 = _ssd_scan_bwd("ssd_bwd0", xs, dtc, acc, bm, cm, hs0, d_y, 0, nctx)
    dxs1, ddt1, dac1, dbm1, dcm1 = _ssd_scan_bwd("ssd_bwd1", xs, dtc, acc, bm, cm, hs1, d_y, 1, nctx)
    row = lambda arr, wd: (arr, wd, 0, 0)
    d_qkv_raw, d_ab, d_gcw, d_gcb, d_alog16, d_dtb16 = _rowwise_bwd(
        "gdnprep_bwd", f_gdnprep, gp_rows, gp_pars,
        [[row(dq0, D), row(dq1, D)], [row(dk0, D), row(dk1, D)], [row(dv0, D), row(dv1, D)],
         [row(dg0, 128), row(dg1, 128)], [row(db0, 128), row(db1, 128)]],
        [(0, BF16), (1, BF16)], [0, 1, 2, 3], ntl, base=0)
    d_xbc_raw, d_dt, d_scw, d_scb, d_alog64, d_dtb64 = _rowwise_bwd(
        "ssmprep_bwd", f_ssmprep, sp_rows, sp_pars,
        [[row(dxs0, S_INNER), row(dxs1, S_INNER), (d_xs_post, S_INNER, 0, -1, True)], [row(dbm0, 512), row(dbm1, 512)],
         [row(dcm0, 512), row(dcm1, 512)], [row(ddt0, 128), row(ddt1, 128)], [row(dac0, 128), row(dac1, 128)]],
        [(0, BF16), (1, BF16)], [0, 1, 2, 3], ntl, base=0)
    ctx_zero = lambda wd: jnp.zeros((TM, wd), BF16)
    d_proj = jnp.concatenate([
        d_qkv_raw, d_xbc_raw, jnp.concatenate([ctx_zero(S_INNER), d_zs], axis=0),
        jnp.concatenate([ctx_zero(S_INNER), d_gate], axis=0), jnp.concatenate([ctx_zero(D), d_zg], axis=0), d_ab, d_dt], axis=1)
    g_w_cat = _mm("mm_gwcat", a, d_proj, "tn", BF16, tn=768, tk=2816)
    w_in_a2a = _AllToAll([_blocks_from_full(_w_uncat(g_w_cat), True)])
    d_a, recv_w_in = _mm("mm_da", d_proj, w_cat, "nt", BF16, tk=3840, ride=w_in_a2a)
    d_xa, d_n1w, d_sc1, d_sh1, d_csc1, d_csh1 = _rowwise_bwd(
        "pre_bwd", f_pre_thru, [(xa, D, 0, 0)], pre_pars, [[row(d_a, D)], [(d_xres, D, 0, -1, True)]],
        [(0, F32)], [0, 1, 2, 3, 4], ntl, base=0)
    zero4 = jnp.zeros((1, 4 * D), F32)
    d_mod = jnp.concatenate([
        jnp.concatenate([d_sh1, d_sc1, d_g1, d_sh2, d_sc2, d_g2], axis=1),
        jnp.concatenate([d_csh1, d_csc1, zero4], axis=1), jnp.zeros((14, 6 * D), F32)], axis=0)
    d_a16 = _mm("mm_da16", d_mod, full["ada_w"], "nt", F32)
    d_cvec = _rowwise_bwd("silu_c_bwd", f_silu_rows, [(cvec, D, 0, 0)], [], [[row(d_a16, D)]], [(0, F32)], [], 1, tm=16)[0]

    recv = dict(zip(BIG_LATE, recv_late), w_in=recv_w_in[0])
    big_un = {n: _reduce_adam("adam_" + n, recv[n], wts[n][0], mom1[n][0], mom2[n][0], BIG_ADAM_ROWS[n])
              for n in BIG if n != "ada_w"}

    small_g = dict(c_ctx=d_cvec[1], ada_b=d_mod[0] + d_mod[1], norm1_w=d_n1w, gdn_conv_b=d_gcb,
                   gdn_a_log=d_alog16[0, :2 * G_HEADS], gdn_dt_bias=d_dtb16[0, :2 * G_HEADS], gdn_norm_w=d_gnw,
                   ssm_conv_b=d_scb, ssm_a_log=d_alog64[0, :2 * S_HEADS], ssm_dt_bias=d_dtb64[0, :2 * S_HEADS],
                   ssm_d=d_ssd8[0, :S_HEADS], ssm_norm_w=d_snw, norm2_w=d_n2w, norm_f_w=d_nfw,
                   gdn_conv_w=d_gcw.reshape(3, XBC), ssm_conv_w=d_scw.reshape(3, XBC))
    small_names = SMALL + CONV
    factors = [a16[0].astype(F32), d_mod[0], d_mod[1]]
    sg_pack = _pack([small_g[n] for n in small_names] + factors, rows=56)
    recv_all = _exchange("ag_small_grads", _AllGather([sg_pack]))[0]
    recv_s = recv_all[:, :48]
    fac = recv_all[:, 42:55].reshape(N_DEV, -1)
    my_cols = lambda z: lax.dynamic_slice(z, (0, me_i * (6 * D // N_DEV)), (N_DEV, 6 * D // N_DEV))
    lhs = jnp.concatenate([fac[:, :D], jnp.broadcast_to(a16[1:2].astype(F32), (N_DEV, D))], axis=0)
    rhs = jnp.concatenate([my_cols(fac[:, D:7 * D]), my_cols(fac[:, 7 * D:])], axis=0)
    g_ada_w = _mm("mm_gwada", lhs, rhs, "tn", F32)
    big_un["ada_w"] = _reduce_adam("adam_ada_w", g_ada_w[None], wts["ada_w"][0], mom1["ada_w"][0], mom2["ada_w"][0],
                                   BIG_ADAM_ROWS["ada_w"])

    def placed(src, n):
        if n not in CONV:
            return src[n]
        return lax.dynamic_update_slice(jnp.zeros((3, XBC), F32), src[n][0], (0, me_i * CONV_SHARD))

    small_out = _reduce_adam("adam_small", recv_s, *[_pack([placed(src, n) for n in small_names], rows=48)
                                                     for src in (wts, mom1, mom2)], 48)
    small_shapes = [wts[n].shape if n in SMALL else (3, XBC) for n in small_names]
    small_un = [_unpack(buf, small_shapes) for buf in small_out]

    res = [{}, {}, {}, {}]
    for kind in range(4):
        for n in BIG:
            res[kind][n] = big_un[n][kind].reshape(wts[n].shape)
        for n, val in zip(small_names, small_un[kind]):
            if n in CONV:
                val = lax.dynamic_slice(val, (0, me_i * CONV_SHARD), (3, CONV_SHARD)).reshape(wts[n].shape)
            res[kind][n] = val
    loss = lax.psum(loss_acc[0, 0], ("x", "y", "c"))
    grad_x = d_xa[TM:].reshape(x.shape)
    return (loss, grad_x, *[res[0][n] for n in order], *[res[1][n] for n in order], *[res[2][n] for n in order],
            *[res[3][n] for n in order])
```

```python
import functools

import jax
import jax.numpy as jnp
from jax import lax
from jax.experimental import pallas as pl
from jax.experimental.pallas import tpu as pltpu

F32 = jnp.float32
BF16 = jnp.bfloat16
HI = lax.Precision.HIGHEST
HIGH = lax.Precision.HIGH
MESH = pl.DeviceIdType.MESH

D = 1024
CH = 64
TM = 256
EPS = 1e-6
NEG = -1e30
G_HEADS = 8
DK = 128
S_HEADS = 32
S_P = 64
S_GROUPS = 4
S_N = 128
S_INNER = 2048
XBC = 3072
D_FF = 2816
N_DEV = 8
PACK_W = 1024
VMEM_LIMIT = 56 * 1024 * 1024

ADAM_LR = 0.001
ADAM_B1 = 0.9
ADAM_B2 = 0.999
ADAM_EPS = 1e-08
ADAM_WD = 0.01
ADAM_STEP = 10

C_QKV, C_XBC, C_ZS, C_GATE, C_ZG, C_AB, C_DT, C_END = 0, 3072, 6144, 8192, 10240, 11264, 11392, 11520
O_QKV, O_ZG, O_AB, O_ZS, O_XBC, O_DT, O_GATE, O_END = 0, 3072, 4096, 4128, 6176, 9248, 9312, 11360


def _dot(a, b, prec=None):
    return jnp.dot(a, b, precision=prec, preferred_element_type=F32)


def _dot_nt(a, b, prec=None):
    return lax.dot_general(a, b, (((1,), (1,)), ((), ())), precision=prec, preferred_element_type=F32)


def _dot_tn(a, b, prec=None):
    return lax.dot_general(a, b, (((0,), (0,)), ((), ())), precision=prec, preferred_element_type=F32)


def _iota(shape, dim):
    return lax.broadcasted_iota(jnp.int32, shape, dim)


def _rms(x):
    return x * lax.rsqrt(jnp.mean(x * x, axis=-1, keepdims=True) + EPS)


def _l2n(x):
    return x * lax.rsqrt(jnp.sum(x * x, axis=-1, keepdims=True) + EPS)


def _silu(x):
    return x * jax.nn.sigmoid(x)


def _softplus(x):
    return jnp.maximum(x, 0.0) + jnp.log1p(jnp.exp(-jnp.abs(x)))


def _roll_rows(x, s):
    return pltpu.roll(x, s, 0)


def _up_raw(x, keep_up):
    return jnp.where(keep_up > 0.0, _roll_rows(x, 1), 0.0)


def _dn_raw(x, keep_dn):
    return jnp.where(keep_dn > 0.0, _roll_rows(x, x.shape[0] - 1), 0.0)


@jax.custom_vjp
def _shift_up(x, keep_up, keep_dn):
    return _up_raw(x, keep_up)


def _shift_up_fwd(x, keep_up, keep_dn):
    return _up_raw(x, keep_up), (keep_up, keep_dn)


def _shift_up_bwd(res, g):
    keep_up, keep_dn = res
    return _dn_raw(g, keep_dn), jnp.zeros_like(keep_up), jnp.zeros_like(keep_dn)


_shift_up.defvjp(_shift_up_fwd, _shift_up_bwd)


@jax.custom_vjp
def _shift_dn(x, keep_up, keep_dn):
    return _dn_raw(x, keep_dn)


def _shift_dn_fwd(x, keep_up, keep_dn):
    return _dn_raw(x, keep_dn), (keep_up, keep_dn)


def _shift_dn_bwd(res, g):
    keep_up, keep_dn = res
    return _up_raw(g, keep_up), jnp.zeros_like(keep_up), jnp.zeros_like(keep_dn)


_shift_dn.defvjp(_shift_dn_fwd, _shift_dn_bwd)


def _conv_keep(is_ctx, n):
    r = _iota((n, 1), 0)
    pos = jnp.where(is_ctx, r, r & (CH - 1))
    end = jnp.where(is_ctx, n - 1, CH - 1)
    return jnp.where(pos == 0, 0.0, 1.0).astype(F32), jnp.where(pos == end, 0.0, 1.0).astype(F32)


def _conv_silu(u, w3, b, keep_up, keep_dn):
    conv = b + _shift_up(u, keep_up, keep_dn) * w3[0] + u * w3[1] + _shift_dn(u, keep_up, keep_dn) * w3[2]
    return _silu(conv)


def _chunk_tri(n, rev):
    i = _iota((n, n), 0)
    j = _iota((n, n), 1)
    same = (i // CH) == (j // CH)
    seen = (i <= j) if rev else (i >= j)
    return jnp.where(same & seen, 1.0, 0.0).astype(F32)


def _expand_mat(rows, cols, per, base):
    r = _iota((rows, cols), 0)
    c = _iota((rows, cols), 1)
    return jnp.where(r == base + c // per, 1.0, 0.0).astype(F32)


def f_silu_rows(is_ctx, cvec):
    return (_silu(cvec).astype(BF16),)


def f_pre(is_ctx, x, xc, n1w, sc, sh, csc, csh):
    x = jnp.where(is_ctx, xc, x)
    sc_e = jnp.where(is_ctx, csc, sc)
    sh_e = jnp.where(is_ctx, csh, sh)
    a = _rms(x) * n1w * (1.0 + sc_e) + sh_e
    return (a.astype(BF16),)


def f_pre_thru(is_ctx, x, xc, n1w, sc, sh, csc, csh):
    return f_pre(is_ctx, x, xc, n1w, sc, sh, csc, csh)[0], x


def f_gdnprep(is_ctx, qkv_raw, ab_raw, cw, cb, alog, dtb):
    n = qkv_raw.shape[0]
    keep_up, keep_dn = _conv_keep(is_ctx, n)
    s = _conv_silu(qkv_raw, cw, cb, keep_up, keep_dn)
    qs, ks, vs = [], [], []
    for h in range(G_HEADS):
        qs.append(_l2n(s[:, h * DK:(h + 1) * DK]) * (DK ** -0.5))
        ks.append(_l2n(s[:, D + h * DK:D + (h + 1) * DK]))
    q = jnp.concatenate(qs, axis=1)
    k = jnp.concatenate(ks, axis=1)
    v = s[:, 2 * D:3 * D]
    lane = _iota(ab_raw.shape, 1)
    g = jnp.where(lane < 2 * G_HEADS, -jnp.exp(alog) * _softplus(ab_raw + dtb), 0.0)
    gcum = jnp.where(lane < G_HEADS, _dot(_chunk_tri(n, False), g, HI), _dot(_chunk_tri(n, True), g, HI))
    beta = jax.nn.sigmoid(ab_raw)
    return q, k, v, gcum, beta


def f_ssmprep(is_ctx, xbc_raw, dt_raw, cw, cb, alog, dtb):
    n = xbc_raw.shape[0]
    keep_up, keep_dn = _conv_keep(is_ctx, n)
    s = _conv_silu(xbc_raw, cw, cb, keep_up, keep_dn)
    xs = s[:, :S_INNER]
    bm = s[:, S_INNER:S_INNER + S_GROUPS * S_N]
    cm = s[:, S_INNER + S_GROUPS * S_N:]
    lane = _iota(dt_raw.shape, 1)
    dt = jnp.where(lane < 2 * S_HEADS, _softplus(dt_raw + dtb), 0.0)
    da = dt * (-jnp.exp(alog))
    acum = jnp.where(lane < S_HEADS, _dot(_chunk_tri(n, False), da, HI), _dot(_chunk_tri(n, True), da, HI))
    return xs, bm, cm, dt, acum


def f_post(is_ctx, o, zg, y_scan, xs, zs, gnw, ssd8, snw):
    ogs = []
    for h in range(G_HEADS):
        sl = slice(h * DK, (h + 1) * DK)
        ogs.append(_rms(o[:, sl]) * gnw * _silu(zg[:, sl]))
    og = jnp.concatenate(ogs, axis=1)
    row0 = jnp.where(_iota(ssd8.shape, 0) == 0, 1.0, 0.0).astype(F32)
    dexp = jnp.sum(_dot(ssd8 * row0, _expand_mat(128, S_INNER, S_P, 0), HI), axis=0, keepdims=True)
    y = (y_scan + dexp * xs) * _silu(zs)
    gw = S_INNER // S_GROUPS
    ys = jnp.concatenate([_rms(y[:, i * gw:(i + 1) * gw]) * snw[:, i * gw:(i + 1) * gw] for i in range(S_GROUPS)], axis=1)
    return og.astype(BF16), ys.astype(BF16)


def f_merge(is_ctx, gate, pg, ps):
    m = jax.nn.sigmoid(gate[:, :D]) * pg + jax.nn.sigmoid(gate[:, D:]) * ps
    return (m.astype(BF16),)


def f_res1(is_ctx, x, mix, g1, n2w, sc2, sh2):
    h1 = x + g1 * mix
    f = _rms(h1) * n2w * (1.0 + sc2) + sh2
    return h1, f.astype(BF16)


def f_act(is_ctx, u):
    return ((_silu(u[:, :D_FF]) * u[:, D_FF:]).astype(BF16),)


def f_final(is_ctx, h1, ff, tgt, g2, nfw):
    h2 = h1 + g2 * ff
    y = _rms(h2) * nfw
    err = y - tgt
    return (0.5 * jnp.sum(jnp.mean(err * err, axis=-1, keepdims=True), axis=0, keepdims=True),)


def _each(fn, *lists):
    return [fn(*args) for args in zip(*lists)]


def _tri_inverse_all(mats):
    n = mats[0].shape[0]
    eye = jnp.where(_iota((n, n), 0) == _iota((n, n), 1), 1.0, 0.0).astype(F32)
    t = [eye - a for a in mats]
    p = [_dot(a, a, HIGH) for a in mats]
    for r in range(5):
        t = _each(lambda t_, p_: t_ + _dot(t_, p_, HIGH), t, p)
        if r < 4:
            p = [_dot(p_, p_, HIGH) for p_ in p]
    return t


@jax.custom_vjp
def _inverse_given(a, t):
    return t


def _inverse_given_fwd(a, t):
    return t, t


def _inverse_given_bwd(t, g):
    return -_dot_nt(_dot_tn(t, g, HIGH), t, HIGH), jnp.zeros_like(t)


_inverse_given.defvjp(_inverse_given_fwd, _inverse_given_bwd)


def gdn_local(qs, ks, vs, gcs, grs, bcs, rev, t_known=None):
    c = qs[0].shape[0]
    ii = _iota((c, c), 0)
    jj = _iota((c, c), 1)
    incl = (ii <= jj) if rev else (ii >= jj)
    strict = (ii < jj) if rev else (ii > jj)
    decay = _each(lambda gc, gr: jnp.exp(jnp.where(incl, gc - gr, NEG)), gcs, grs)
    kb = _each(lambda k, bc: k * bc, ks, bcs)
    a = _each(lambda kb_, k, dc: jnp.where(strict, _dot_nt(kb_, k) * dc, 0.0), kb, ks, decay)
    t = _tri_inverse_all(a) if t_known is None else _each(_inverse_given, a, t_known)
    eg = [jnp.exp(gc) for gc in gcs]
    rhs = _each(lambda kb_, eg_, v, bc: jnp.concatenate([kb_ * eg_, v * bc], axis=1), kb, eg, vs, bcs)
    wu = _each(lambda t_, r: _dot(t_, r, HIGH), t, rhs)
    lhs = _each(lambda wu_, q, eg_: jnp.concatenate([wu_[:, :DK], q * eg_], axis=0), wu, qs, eg)
    attn = _each(lambda q, k, dc: _dot_nt(q, k) * dc, qs, ks, decay)
    return wu, attn, lhs, t


def gdn_state(ss, wu, attn, lhs, ks, gcs, rev):
    c = ks[0].shape[0]
    is_last = _iota((c, 1), 0) == (0 if rev else c - 1)
    ws = _each(_dot, lhs, ss)
    v_new = _each(lambda wu_, ws_: wu_[:, DK:] - ws_[:c], wu, ws)
    o = _each(lambda ws_, at, vn: ws_[c:] + _dot(at, vn), ws, attn, v_new)
    gtot = [jnp.sum(jnp.where(is_last, gc, 0.0), axis=0, keepdims=True) for gc in gcs]
    s_new = _each(lambda s, k, gc, gt_, vn: s * jnp.exp(gt_) + _dot_tn(k * jnp.exp(gt_ - gc), vn), ss, ks, gcs, gtot, v_new)
    return s_new, o


def gdn_chunk(ss, qs, ks, vs, gcs, grs, bcs, rev, t_known=None):
    wu, attn, lhs, t = gdn_local(qs, ks, vs, gcs, grs, bcs, rev, t_known)
    s_new, o = gdn_state(ss, wu, attn, lhs, ks, gcs, rev)
    return s_new, o, t


def ssd_pick(dt0s, dt1s, ac0s, ac1s, ar0s, ar1s):
    lo = _iota((CH, 128), 1) < S_P
    pick = lambda u0, u1: jnp.where(lo, u0, u1)
    return _each(pick, dt0s, dt1s), _each(pick, ac0s, ac1s), _each(pick, ar0s, ar1s)


def ssd_local(xs, dts, acs, acr, bgs, cgs, rev):
    c = xs[0].shape[0]
    npair = len(xs)
    grp = [p * len(bgs) // npair for p in range(npair)]
    lane = _iota((c, 128), 1)
    ii = _iota((c, 128), 0)
    jl = lane & (S_P - 1)
    lo = lane < S_P
    seen = (ii <= jl) if rev else (ii >= jl)
    last = 0 if rev else c - 1
    split = lambda z: jnp.concatenate([jnp.where(lo, z, 0.0), jnp.where(lo, 0.0, z)], axis=0)
    cb = _each(lambda bg, cg: _dot_nt(cg, jnp.concatenate([bg, bg], axis=0)), bgs, cgs)
    seg = _each(lambda ac, ar: jnp.exp(jnp.where(seen, ac - ar, NEG)), acs, acr)
    xdt = _each(lambda x, dt: x * dt, xs, dts)
    ydiag = [_dot(cb[grp[p]] * seg[p], split(xdt[p])) for p in range(npair)]
    eac = [jnp.exp(ac) for ac in acs]
    atot = [jnp.sum(jnp.where(ii == last, ac, 0.0), axis=0, keepdims=True) for ac in acs]
    upd = [_dot_tn(bgs[grp[p]], xdt[p] * jnp.exp(atot[p] - acs[p])) for p in range(npair)]
    return ydiag, eac, [jnp.exp(at) for at in atot], upd


def ssd_state(hts, ydiag, eac, etot, upd, cgs):
    npair = len(hts)
    grp = [p * len(cgs) // npair for p in range(npair)]
    y = [ydiag[p] + _dot(cgs[grp[p]], hts[p]) * eac[p] for p in range(npair)]
    h_new = [hts[p] * etot[p] + upd[p] for p in range(npair)]
    return h_new, y


def ssd_chunk(hts, xs, dts, acs, acr, bgs, cgs, rev):
    ydiag, eac, etot, upd = ssd_local(xs, dts, acs, acr, bgs, cgs, rev)
    return ssd_state(hts, ydiag, eac, etot, upd, cgs)


def f_adamw(w, g, m, v):
    m = ADAM_B1 * m + (1.0 - ADAM_B1) * g
    v = ADAM_B2 * v + (1.0 - ADAM_B2) * jnp.square(g)
    m_hat = m / (1.0 - ADAM_B1 ** ADAM_STEP)
    v_hat = v / (1.0 - ADAM_B2 ** ADAM_STEP)
    delta = -ADAM_LR * (m_hat / (jnp.sqrt(v_hat) + ADAM_EPS) + ADAM_WD * w)
    return delta, m, v


def _cparams(sem):
    return pltpu.CompilerParams(dimension_semantics=sem, vmem_limit_bytes=VMEM_LIMIT)


def _pick(n, target):
    if n <= target:
        return n
    best = None
    for t in range(128, target + 1, 128):
        if n % t == 0:
            best = t
    assert best is not None, (n, target)
    return best


def _row_spec(tm, width, colblk, rowoff):
    return pl.BlockSpec((tm, width), lambda i: (i + rowoff, colblk))


def _par_spec(shape):
    nd = len(shape)
    return pl.BlockSpec(tuple(shape), lambda i: (0,) * nd)


def _rowwise(name, fn, rows, pars, outs, ntiles, base=1, tm=TM):
    nr, npar = len(rows), len(pars)

    def body(*refs):
        is_ctx = (pl.program_id(0) + base) == 0
        res = fn(is_ctx, *[r[...] for r in refs[:nr]], *[p[...] for p in refs[nr:nr + npar]])
        for o_ref, r in zip(refs[nr + npar:], res):
            o_ref[...] = r.astype(o_ref.dtype)

    return pl.pallas_call(
        body, grid=(ntiles,), name=name,
        in_specs=[_ct_spec(tm, d) for d in rows] + [_par_spec(p.shape) for p in pars],
        out_specs=[_row_spec(tm, wd, 0, 0) for (wd, _) in outs],
        out_shape=[jax.ShapeDtypeStruct((ntiles * tm, wd), dt) for (wd, dt) in outs],
        compiler_params=_cparams(("arbitrary",)),
    )(*[r[0] for r in rows], *pars)


def _ct_spec(tm, desc):
    _, wd, cb, ro = desc[:4]
    flag = desc[4] if len(desc) > 4 else False
    if flag == "first":
        return pl.BlockSpec((tm, wd), lambda i: (0, cb))
    if flag:
        return pl.BlockSpec((tm, wd), lambda i: (jnp.maximum(i + ro, 0), cb))
    return _row_spec(tm, wd, cb, ro)


def _rowwise_bwd(name, fn, rows, pars, cts, drows, dpars, ntiles, base=1, loss_out=False, tm=TM):
    nr, npar = len(rows), len(pars)
    ct_rows = [d for ct in cts if isinstance(ct, list) for d in ct]
    nct = len(ct_rows)

    def body(*refs):
        i = pl.program_id(0)
        is_ctx = (i + base) == 0
        rows_v = [r[...] for r in refs[:nr]]
        pars_v = [p[...] for p in refs[nr:nr + npar]]
        ct_refs = list(refs[nr + npar:nr + npar + nct])
        out_refs = list(refs[nr + npar + nct:])
        outs, vjp = jax.vjp(lambda rv, pv: fn(is_ctx, *rv, *pv), rows_v, pars_v)

        def ct_value(desc):
            val = ct_refs.pop(0)[...].astype(F32)
            if len(desc) > 4 and desc[4]:
                val = jnp.where(is_ctx, 0.0, val)
            return val

        ct_vals = []
        for o, ct in zip(outs, cts):
            if ct is None:
                ct_vals.append(jnp.zeros_like(o))
            elif isinstance(ct, str):
                ct_vals.append(jnp.ones_like(o))
            else:
                acc = ct_value(ct[0])
                for desc in ct[1:]:
                    acc = acc + ct_value(desc)
                ct_vals.append(acc.astype(o.dtype))
        d_rows, d_pars = vjp(tuple(ct_vals))
        for (ri, _), o_ref in zip(drows, out_refs[:len(drows)]):
            o_ref[...] = d_rows[ri].astype(o_ref.dtype)
        acc_refs = out_refs[len(drows):]
        acc_vals = [d_pars[pi] for pi in dpars]
        if loss_out:
            acc_vals.append(jnp.broadcast_to(outs[0], (8, 128)))

        @pl.when(i == 0)
        def _():
            for o_ref, val in zip(acc_refs, acc_vals):
                o_ref[...] = val

        @pl.when(i > 0)
        def _():
            for o_ref, val in zip(acc_refs, acc_vals):
                o_ref[...] += val

    acc_shapes = [pars[pi].shape for pi in dpars] + ([(8, 128)] if loss_out else [])
    lat_row = lambda ri: len(rows[ri]) > 4 and rows[ri][4] is True
    return pl.pallas_call(
        body, grid=(ntiles,), name=name,
        in_specs=[_ct_spec(tm, d) for d in rows] + [_par_spec(p.shape) for p in pars]
        + [_ct_spec(tm, d) for d in ct_rows],
        out_specs=[_ct_spec(tm, (None, rows[ri][1], 0, -1, True) if lat_row(ri) else (None, rows[ri][1], 0, 0))
                   for (ri, _) in drows] + [_par_spec(s) for s in acc_shapes],
        out_shape=[jax.ShapeDtypeStruct(((ntiles - int(lat_row(ri))) * tm, rows[ri][1]), dt) for (ri, dt) in drows]
        + [jax.ShapeDtypeStruct(tuple(s), F32) for s in acc_shapes],
        compiler_params=_cparams(("arbitrary",)),
    )(*[r[0] for r in rows], *pars, *[r[0] for r in ct_rows])


def _mm(name, a, b, mode, out_dtype, tm=1024, tn=1024, tk=1024, ride=None):
    if mode == "nn":
        (m, kd), (_, n) = a.shape, b.shape
    elif mode == "nt":
        (m, kd), (n, _) = a.shape, b.shape
    else:
        (kd, m), (_, n) = a.shape, b.shape
    tm, tn, tk = _pick(m, tm), _pick(n, tn), _pick(kd, tk)
    nk = kd // tk
    a_spec = {"nn": pl.BlockSpec((tm, tk), lambda i, j, k: (i, k)), "nt": pl.BlockSpec((tm, tk), lambda i, j, k: (i, k)),
              "tn": pl.BlockSpec((tk, tm), lambda i, j, k: (k, i))}[mode]
    b_spec = {"nn": pl.BlockSpec((tk, tn), lambda i, j, k: (k, j)), "nt": pl.BlockSpec((tn, tk), lambda i, j, k: (j, k)),
              "tn": pl.BlockSpec((tk, tn), lambda i, j, k: (k, j))}[mode]
    dot = {"nn": _dot, "nt": _dot_nt, "tn": _dot_tn}[mode]

    if nk == 1:
        def body(a_ref, b_ref, o_ref):
            o_ref[...] = dot(a_ref[...].astype(BF16), b_ref[...].astype(BF16)).astype(o_ref.dtype)
    else:
        def body(a_ref, b_ref, o_ref, acc_ref):
            k = pl.program_id(2)
            part = dot(a_ref[...].astype(BF16), b_ref[...].astype(BF16))

            @pl.when(k == 0)
            def _():
                acc_ref[...] = part

            @pl.when((k > 0) & (k < nk - 1))
            def _():
                acc_ref[...] += part

            @pl.when(k == nk - 1)
            def _():
                o_ref[...] = (acc_ref[...] + part).astype(o_ref.dtype)

    grid = (m // tm, n // tn, nk)
    at = lambda pos: functools.reduce(jnp.logical_and, [pl.program_id(ax) == pos(g) for ax, g in enumerate(grid)])
    r_in, r_out, r_shapes, r_scr, r_ops = _ride_args(ride)
    res = pl.pallas_call(
        _ride(body, ride, 2, 1, lambda: at(lambda g: 0), lambda: at(lambda g: g - 1)), grid=grid, name=name,
        in_specs=[a_spec, b_spec] + r_in,
        out_specs=[pl.BlockSpec((tm, tn), lambda i, j, k: (i, j))] + r_out,
        out_shape=[jax.ShapeDtypeStruct((m, n), out_dtype)] + r_shapes,
        scratch_shapes=([] if nk == 1 else [pltpu.VMEM((tm, tn), F32)]) + r_scr,
        compiler_params=_cparams(("arbitrary", "arbitrary", "arbitrary")),
    )(a, b, *r_ops)
    return res[0] if ride is None else (res[0], res[1:])


def _chunk_index(i, nch, nctx, rev):
    if not rev:
        return i
    return jnp.where(i < nctx, nctx - 1 - i, nch + nctx - 1 - i)


def _gdn_cols(d):
    return [d * G_HEADS + h for h in range(G_HEADS)], [2 * G_HEADS + d * G_HEADS + h for h in range(G_HEADS)]


def _gdn_operands(q_ref, k_ref, v_ref, g_ref, b_ref, d, rows=slice(None)):
    cols_g, cols_b = _gdn_cols(d)
    sls = [slice(h * DK, (h + 1) * DK) for h in range(G_HEADS)]
    gt, bt = g_ref[rows, :], b_ref[rows, :]
    gtt = gt.T
    qs = [q_ref[rows, sl] for sl in sls]
    ks = [k_ref[rows, sl] for sl in sls]
    vs = [v_ref[rows, sl] for sl in sls]
    gcs = [gt[:, cg:cg + 1] for cg in cols_g]
    grs = [gtt[cg:cg + 1, :] for cg in cols_g]
    bcs = [bt[:, cb:cb + 1] for cb in cols_b]
    return sls, qs, ks, vs, gcs, grs, bcs


GDN_FWD_CHUNKS = 4


def _gdn_scan_fwd(name, q, k, v, gcum, beta, d, nctx, ride=None, add=None):
    t = q.shape[0]
    nch = t // CH
    per = GDN_FWD_CHUNKS
    nst = nch // per
    rev = d == 1
    bix = lambda i: _chunk_index(i, nst, nctx // per, rev)
    full = pl.BlockSpec((per * CH, D), lambda i: (bix(i), 0))
    wide = pl.BlockSpec((per * CH, 128), lambda i: (bix(i), 0))
    order = list(range(per - 1, -1, -1)) if rev else list(range(per))

    def body(q_ref, k_ref, v_ref, g_ref, b_ref, *rest):
        add_ref = rest[0] if add is not None else None
        o_ref, ss_ref, ts_ref, s_scr = rest[-4:]

        @pl.when(pl.program_id(0) == 0)
        def _():
            s_scr[...] = jnp.zeros(s_scr.shape, F32)

        ops = [_gdn_operands(q_ref, k_ref, v_ref, g_ref, b_ref, d, slice(c * CH, (c + 1) * CH)) for c in order]
        sls = ops[0][0]
        cat = [sum((o[i] for o in ops), []) for i in range(1, 7)]
        wu, attn, lhs, tinv = gdn_local(*cat, rev)
        ss = [s_scr[h] for h in range(G_HEADS)]
        for n, c in enumerate(order):
            sl = slice(n * G_HEADS, (n + 1) * G_HEADS)
            s_new, o = gdn_state(ss, wu[sl], attn[sl], lhs[sl], cat[1][sl], cat[3][sl], rev)
            for h in range(G_HEADS):
                ss_ref[n, h] = ss[h]
                ts_ref[n, h] = tinv[n * G_HEADS + h]
                rows = slice(c * CH, (c + 1) * CH)
                o_ref[rows, sls[h]] = o[h] if add is None else o[h] + add_ref[rows, sls[h]]
            ss = s_new
        for h in range(G_HEADS):
            s_scr[h] = ss[h]

    r_in, r_out, r_shapes, r_scr, r_ops = _ride_args(ride)
    extra = [] if add is None else [add]
    res = pl.pallas_call(
        _ride(body, ride, 5 + len(extra), 3, lambda: pl.program_id(0) == 0, lambda: pl.program_id(0) == nst - 1), grid=(nst,), name=name,
        in_specs=[full, full, full, wide, wide] + [full] * len(extra) + r_in,
        out_specs=[full, pl.BlockSpec((per, G_HEADS, DK, DK), lambda i: (i, 0, 0, 0)),
                   pl.BlockSpec((per, G_HEADS, CH, CH), lambda i: (i, 0, 0, 0))] + r_out,
        out_shape=[jax.ShapeDtypeStruct((t, D), F32), jax.ShapeDtypeStruct((nch, G_HEADS, DK, DK), F32),
                   jax.ShapeDtypeStruct((nch, G_HEADS, CH, CH), F32)] + r_shapes,
        scratch_shapes=[pltpu.VMEM((G_HEADS, DK, DK), F32)] + r_scr,
        compiler_params=_cparams(("arbitrary",)),
    )(q, k, v, gcum, beta, *extra, *r_ops)
    return res if ride is None else (*res[:3], res[3:])


def _gdn_scan_bwd(name, q, k, v, gcum, beta, ssave, tsave, do, d, nctx, ride=None):
    t = q.shape[0]
    nch = t // CH
    rev = d == 1
    cix = lambda ib: _chunk_index(nch - 1 - ib, nch, nctx, rev)
    full = pl.BlockSpec((CH, D), lambda ib: (cix(ib), 0))
    wide = pl.BlockSpec((CH, 128), lambda ib: (cix(ib), 0))
    do_spec = pl.BlockSpec((CH, D), lambda ib: (jnp.maximum(cix(ib), nctx) - nctx, 0))

    def body(q_ref, k_ref, v_ref, g_ref, b_ref, ss_ref, ts_ref, do_ref, dq_ref, dk_ref, dv_ref, dg_ref, db_ref, ds_scr):
        ib = pl.program_id(0)

        @pl.when(ib == 0)
        def _():
            ds_scr[...] = jnp.zeros(ds_scr.shape, F32)

        sls, qs, ks, vs, gcs, grs, bcs = _gdn_operands(q_ref, k_ref, v_ref, g_ref, b_ref, d)
        t_known = [ts_ref[0, h] for h in range(G_HEADS)]
        cols_g, cols_b = _gdn_cols(d)
        is_lat = cix(ib) >= nctx
        ss = [ss_ref[0, h] for h in range(G_HEADS)]
        do_v = [jnp.where(is_lat, do_ref[:, sl], 0.0) for sl in sls]
        ds_in = [ds_scr[h] for h in range(G_HEADS)]
        _, vjp = jax.vjp(lambda *a: gdn_chunk(*a, rev, t_known)[:2], ss, qs, ks, vs, gcs, grs, bcs)
        ds, dq, dk, dv, dgc, dgr, dbc = vjp((ds_in, do_v))
        lane = _iota((CH, 128), 1)
        sub = _iota((128, CH), 0)
        dg = jnp.zeros((CH, 128), F32)
        dgt = jnp.zeros((128, CH), F32)
        db = jnp.zeros((CH, 128), F32)
        for h in range(G_HEADS):
            ds_scr[h] = ds[h]
            dq_ref[:, sls[h]] = dq[h]
            dk_ref[:, sls[h]] = dk[h]
            dv_ref[:, sls[h]] = dv[h]
            dg = dg + jnp.where(lane == cols_g[h], dgc[h], 0.0)
            dgt = dgt + jnp.where(sub == cols_g[h], dgr[h], 0.0)
            db = db + jnp.where(lane == cols_b[h], dbc[h], 0.0)
        dg_ref[...] = dg + dgt.T
        db_ref[...] = db

    r_in, r_out, r_shapes, r_scr, r_ops = _ride_args(ride)
    res = pl.pallas_call(
        _ride(body, ride, 8, 5, lambda: pl.program_id(0) == 0, lambda: pl.program_id(0) == nch - 1), grid=(nch,), name=name,
        in_specs=[full, full, full, wide, wide,
                  pl.BlockSpec((1, G_HEADS, DK, DK), lambda ib: (nch - 1 - ib, 0, 0, 0)),
                  pl.BlockSpec((1, G_HEADS, CH, CH), lambda ib: (nch - 1 - ib, 0, 0, 0)), do_spec] + r_in,
        out_specs=[full, full, full, wide, wide] + r_out,
        out_shape=[jax.ShapeDtypeStruct((t, D), F32)] * 3 + [jax.ShapeDtypeStruct((t, 128), F32)] * 2 + r_shapes,
        scratch_shapes=[pltpu.VMEM((G_HEADS, DK, DK), F32)] + r_scr,
        compiler_params=_cparams(("arbitrary",)),
    )(q, k, v, gcum, beta, ssave, tsave, do, *r_ops)
    return res if ride is None else (*res[:5], res[5:])


N_PAIRS = S_HEADS // 2


def _ssd_operands(x_ref, dt_ref, ac_ref, b_ref, c_ref, d, rows=slice(None)):
    sls = [slice(p * 128, (p + 1) * 128) for p in range(N_PAIRS)]
    gsl = [slice(g * S_N, (g + 1) * S_N) for g in range(S_GROUPS)]
    cols = [d * S_HEADS + h for h in range(S_HEADS)]
    dtc, acc = dt_ref[rows, :], ac_ref[rows, :]
    act = jnp.concatenate([acc, acc], axis=0).T
    col = lambda z, cc: z[:, cc:cc + 1]
    dts, acs, acr = ssd_pick(
        [col(dtc, cols[2 * p]) for p in range(N_PAIRS)], [col(dtc, cols[2 * p + 1]) for p in range(N_PAIRS)],
        [col(acc, cols[2 * p]) for p in range(N_PAIRS)], [col(acc, cols[2 * p + 1]) for p in range(N_PAIRS)],
        [act[cols[2 * p]:cols[2 * p] + 1, :] for p in range(N_PAIRS)],
        [act[cols[2 * p + 1]:cols[2 * p + 1] + 1, :] for p in range(N_PAIRS)])
    ops = ([x_ref[rows, sl] for sl in sls], dts, acs, acr, [b_ref[rows, gs] for gs in gsl], [c_ref[rows, gs] for gs in gsl])
    return sls, gsl, cols, ops


SSD_FWD_CHUNKS = 4


def _ssd_scan_fwd(name, xs, dtc, acc, bm, cm, d, nctx, add=None):
    t = xs.shape[0]
    nch = t // CH
    per = SSD_FWD_CHUNKS
    nst = nch // per
    rev = d == 1
    bix = lambda i: _chunk_index(i, nst, nctx // per, rev)
    inner = pl.BlockSpec((per * CH, S_INNER), lambda i: (bix(i), 0))
    wide = pl.BlockSpec((per * CH, 128), lambda i: (bix(i), 0))
    grp = pl.BlockSpec((per * CH, S_GROUPS * S_N), lambda i: (bix(i), 0))
    order = list(range(per - 1, -1, -1)) if rev else list(range(per))

    def body(x_ref, dt_ref, ac_ref, b_ref, c_ref, *rest):
        add_ref = rest[0] if add is not None else None
        y_ref, hs_ref, h_scr = rest[-3:]

        @pl.when(pl.program_id(0) == 0)
        def _():
            h_scr[...] = jnp.zeros(h_scr.shape, F32)

        loc, cgs, sls = [], [], None
        for c in order:
            sls, _, _, ops = _ssd_operands(x_ref, dt_ref, ac_ref, b_ref, c_ref, d, slice(c * CH, (c + 1) * CH))
            loc.append(ssd_local(*ops, rev))
            cgs.append(ops[5])
        hts = [h_scr[p] for p in range(N_PAIRS)]
        for n, c in enumerate(order):
            h_new, y = ssd_state(hts, *loc[n], cgs[n])
            for p in range(N_PAIRS):
                hs_ref[n, p] = hts[p]
                rows = slice(c * CH, (c + 1) * CH)
                y_ref[rows, sls[p]] = y[p] if add is None else y[p] + add_ref[rows, sls[p]]
            hts = h_new
        for p in range(N_PAIRS):
            h_scr[p] = hts[p]

    return pl.pallas_call(
        body, grid=(nst,), name=name,
        in_specs=[inner, wide, wide, grp, grp] + ([] if add is None else [inner]),
        out_specs=[inner, pl.BlockSpec((per, N_PAIRS, S_N, 128), lambda i: (i, 0, 0, 0))],
        out_shape=[jax.ShapeDtypeStruct((t, S_INNER), F32), jax.ShapeDtypeStruct((nch, N_PAIRS, S_N, 128), F32)],
        scratch_shapes=[pltpu.VMEM((N_PAIRS, S_N, 128), F32)],
        compiler_params=_cparams(("arbitrary",)),
    )(xs, dtc, acc, bm, cm, *([] if add is None else [add]))


def _ssd_scan_bwd(name, xs, dtc, acc, bm, cm, hsave, dy, d, nctx):
    t = xs.shape[0]
    nch = t // CH
    rev = d == 1
    cix = lambda ib: _chunk_index(nch - 1 - ib, nch, nctx, rev)
    inner = pl.BlockSpec((CH, S_INNER), lambda ib: (cix(ib), 0))
    wide = pl.BlockSpec((CH, 128), lambda ib: (cix(ib), 0))
    grp = pl.BlockSpec((CH, S_GROUPS * S_N), lambda ib: (cix(ib), 0))
    dy_spec = pl.BlockSpec((CH, S_INNER), lambda ib: (jnp.maximum(cix(ib), nctx) - nctx, 0))

    def body(x_ref, dt_ref, ac_ref, b_ref, c_ref, hs_ref, dy_ref, dx_ref, ddt_ref, dac_ref, db_ref, dc_ref, dh_scr):
        ib = pl.program_id(0)

        @pl.when(ib == 0)
        def _():
            dh_scr[...] = jnp.zeros(dh_scr.shape, F32)

        sls, gsl, cols, ops = _ssd_operands(x_ref, dt_ref, ac_ref, b_ref, c_ref, d)
        is_lat = cix(ib) >= nctx
        hts = [hs_ref[0, p] for p in range(N_PAIRS)]
        dy_v = [jnp.where(is_lat, dy_ref[:, sl], 0.0) for sl in sls]
        dh_in = [dh_scr[p] for p in range(N_PAIRS)]
        _, vjp = jax.vjp(lambda *a: ssd_chunk(*a, rev), hts, *ops)
        dh, dx, ddts, dacs, dacr, db, dc = vjp((dh_in, dy_v))
        for p in range(N_PAIRS):
            dh_scr[p] = dh[p]
            dx_ref[:, sls[p]] = dx[p]
        r = _iota((S_INNER, 128), 0)
        e_t = jnp.where(_iota((S_INNER, 128), 1) == d * S_HEADS + r // S_P, 1.0, 0.0).astype(F32)
        ddt_ref[...] = _dot(jnp.concatenate(ddts, axis=1), e_t, HIGH)
        dac_cols = _dot(jnp.concatenate(dacs, axis=1), e_t, HIGH)
        sub = _iota((128, 128), 0)
        lane = _iota((128, 128), 1)
        m = jnp.zeros((128, 128), F32)
        for p in range(N_PAIRS):
            m = m + jnp.where(sub == p, jnp.sum(dacr[p], axis=0, keepdims=True), 0.0)
        mt = m.T
        s0 = jnp.where(lane == d * S_HEADS + 2 * sub, 1.0, 0.0).astype(F32)
        s1 = jnp.where(lane == d * S_HEADS + 2 * sub + 1, 1.0, 0.0).astype(F32)
        dac_ref[...] = dac_cols + _dot(mt[:CH], s0, HIGH) + _dot(mt[CH:], s1, HIGH)
        for g in range(S_GROUPS):
            db_ref[:, gsl[g]] = db[g]
            dc_ref[:, gsl[g]] = dc[g]

    return pl.pallas_call(
        body, grid=(nch,), name=name,
        in_specs=[inner, wide, wide, grp, grp,
                  pl.BlockSpec((1, N_PAIRS, S_N, 128), lambda ib: (nch - 1 - ib, 0, 0, 0)), dy_spec],
        out_specs=[inner, wide, wide, grp, grp],
        out_shape=[jax.ShapeDtypeStruct((t, S_INNER), F32)] + [jax.ShapeDtypeStruct((t, 128), F32)] * 2
        + [jax.ShapeDtypeStruct((t, S_GROUPS * S_N), F32)] * 2,
        scratch_shapes=[pltpu.VMEM((N_PAIRS, S_N, 128), F32)],
        compiler_params=_cparams(("arbitrary",)),
    )(xs, dtc, acc, bm, cm, hsave, dy)


def _mesh_pos():
    return lax.axis_index("x"), lax.axis_index("y"), lax.axis_index("c")


def _hbm_specs(n):
    return [pl.BlockSpec(memory_space=pl.ANY)] * n


def _sem_shapes(nw):
    return [pltpu.SemaphoreType.DMA((nw, 7)), pltpu.SemaphoreType.DMA((nw, 7)), pltpu.SemaphoreType.DMA((nw,))]


class _AllGather:
    def __init__(self, shards):
        self.arrays = list(shards)
        self.out_shapes = [jax.ShapeDtypeStruct((N_DEV,) + xs.shape, xs.dtype) for xs in shards]

    def _parts(self, x_refs, out_refs, sems):
        send_sems, recv_sems, local_sems = sems
        nw = len(x_refs)
        x, y, c = _mesh_pos()
        me, sibling = (x, y, c), (x, y, 1 - c)
        chips = [(1 - x, y), (x, 1 - y), (1 - x, 1 - y)]

        def slot(w, px, py, pc):
            return out_refs[w].at[4 * px + 2 * py + pc]

        def copy(w, k, block, to, src=None):
            return pltpu.make_async_remote_copy(
                src_ref=slot(w, *block) if src is None else src, dst_ref=slot(w, *block),
                send_sem=send_sems.at[w, k], recv_sem=recv_sems.at[w, k], device_id=to, device_id_type=MESH)

        mine = [pltpu.make_async_copy(x_refs[w], slot(w, *me), local_sems.at[w]) for w in range(nw)]
        first = []
        for w in range(nw):
            first.append(copy(w, 0, me, sibling, src=x_refs[w]))
            first += [copy(w, 1 + j, me, (*chip, c), src=x_refs[w]) for j, chip in enumerate(chips)]
        return nw, me, sibling, chips, c, copy, mine, first

    def start(self, x_refs, out_refs, sems):
        _, _, _, _, _, _, mine, first = self._parts(x_refs, out_refs, sems)
        for cp in mine + first:
            cp.start()

    def finish(self, x_refs, out_refs, sems):
        nw, me, sibling, chips, c, copy, mine, first = self._parts(x_refs, out_refs, sems)
        passed = []
        for j, chip in enumerate(chips):
            for w in range(nw):
                copy(w, 1 + j, (*chip, c), me).wait_recv()
                fwd = copy(w, 4 + j, (*chip, c), sibling)
                fwd.start()
                passed.append(fwd)
        for w in range(nw):
            copy(w, 0, sibling, me).wait_recv()
            for j, chip in enumerate(chips):
                copy(w, 4 + j, (*chip, 1 - c), me).wait_recv()
        for cp in first + passed:
            cp.wait_send()
        for cp in mine:
            cp.wait()


class _AllToAll:
    def __init__(self, blocks):
        self.arrays = list(blocks)
        self.out_shapes = [jax.ShapeDtypeStruct(g.shape, g.dtype) for g in blocks]

    def _parts(self, g_refs, out_refs, sems):
        send_sems, recv_sems, local_sems = sems
        nw = len(g_refs)
        x, y, c = _mesh_pos()
        me_i = 4 * x + 2 * y + c
        mine = [pltpu.make_async_copy(g_refs[w].at[me_i], out_refs[w].at[me_i], local_sems.at[w]) for w in range(nw)]
        cps = []
        for k in range(1, N_DEV):
            px = 1 - x if (k >> 2) & 1 else x
            py = 1 - y if (k >> 1) & 1 else y
            pc = 1 - c if k & 1 else c
            for w in range(nw):
                cps.append(pltpu.make_async_remote_copy(
                    src_ref=g_refs[w].at[4 * px + 2 * py + pc], dst_ref=out_refs[w].at[me_i],
                    send_sem=send_sems.at[w, k - 1], recv_sem=recv_sems.at[w, k - 1],
                    device_id=(px, py, pc), device_id_type=MESH))
        return mine, cps

    def start(self, g_refs, out_refs, sems):
        mine, cps = self._parts(g_refs, out_refs, sems)
        for cp in mine + cps:
            cp.start()

    def finish(self, g_refs, out_refs, sems):
        mine, cps = self._parts(g_refs, out_refs, sems)
        for cp in cps + mine:
            cp.wait()


def _exchange(name, ex):
    nw = len(ex.arrays)

    def body(*refs):
        ins, outs, sems = refs[:nw], refs[nw:2 * nw], refs[2 * nw:]
        ex.start(ins, outs, sems)
        ex.finish(ins, outs, sems)

    return pl.pallas_call(body, name=name, out_shape=ex.out_shapes, in_specs=_hbm_specs(nw), out_specs=_hbm_specs(nw),
                          scratch_shapes=_sem_shapes(nw))(*ex.arrays)


def _ride(body, ex, n_in, n_out, is_first, is_last):
    if ex is None:
        return body
    nw = len(ex.arrays)

    def riding(*refs):
        ins, ex_in = refs[:n_in], refs[n_in:n_in + nw]
        outs = refs[n_in + nw:n_in + nw + n_out]
        ex_out = refs[n_in + nw + n_out:n_in + 2 * nw + n_out]
        rest = refs[n_in + 2 * nw + n_out:]
        scratch, sems = rest[:len(rest) - 3], rest[len(rest) - 3:]

        @pl.when(is_first())
        def _():
            ex.start(ex_in, ex_out, sems)

        body(*ins, *outs, *scratch)

        @pl.when(is_last())
        def _():
            ex.finish(ex_in, ex_out, sems)

    return riding


def _ride_args(ex):
    if ex is None:
        return [], [], [], [], []
    nw = len(ex.arrays)
    return _hbm_specs(nw), _hbm_specs(nw), list(ex.out_shapes), _sem_shapes(nw), list(ex.arrays)


def _reduce_adam(name, recv, w, m, v, tm):
    rows, width = w.shape
    nslot = recv.shape[0]

    def body(recv_ref, w_ref, m_ref, v_ref, g_ref, d_ref, m2_ref, v2_ref):
        g = recv_ref[0].astype(F32)
        for s in range(1, nslot):
            g = g + recv_ref[s].astype(F32)
        delta, m2, v2 = f_adamw(w_ref[...], g, m_ref[...], v_ref[...])
        g_ref[...] = g
        d_ref[...] = delta
        m2_ref[...] = m2
        v2_ref[...] = v2

    row = pl.BlockSpec((tm, width), lambda i: (i, 0))
    return pl.pallas_call(
        body, grid=(rows // tm,), name=name,
        in_specs=[pl.BlockSpec((nslot, tm, width), lambda i: (0, i, 0)), row, row, row],
        out_specs=[row] * 4,
        out_shape=[jax.ShapeDtypeStruct((rows, width), F32)] * 4,
        compiler_params=_cparams(("arbitrary",)),
    )(recv, w, m, v)


BIG = ("w_in", "ada_w", "w_br_gdn", "w_br_ssm", "w_out", "w_ffn_in", "w_ffn_out")
BIG_FIRST = ("ada_w", "w_in")
BIG_LATE = ("w_br_gdn", "w_br_ssm", "w_out", "w_ffn_in", "w_ffn_out")
BIG_COL_SHARDED = ("w_in", "ada_w", "w_ffn_in")
BIG_ADAM_ROWS = dict(w_in=128, ada_w=256, w_br_gdn=128, w_br_ssm=256, w_out=128, w_ffn_in=256, w_ffn_out=352)
CONV = ("gdn_conv_w", "ssm_conv_w")
SMALL = ("c_ctx", "ada_b", "norm1_w", "gdn_conv_b", "gdn_a_log", "gdn_dt_bias", "gdn_norm_w", "ssm_conv_b",
         "ssm_a_log", "ssm_dt_bias", "ssm_d", "ssm_norm_w", "norm2_w", "norm_f_w")
CONV_SHARD = XBC // N_DEV


def _to_rows(a):
    flat = a.reshape(-1)
    pad = (-flat.shape[0]) % PACK_W
    if pad:
        flat = jnp.pad(flat, (0, pad))
    return flat.reshape(-1, PACK_W)


def _pack(arrays, rows=None):
    buf = jnp.concatenate([_to_rows(a) for a in arrays], axis=0)
    if rows is not None and rows > buf.shape[0]:
        buf = jnp.pad(buf, ((0, rows - buf.shape[0]), (0, 0)))
    return buf


def _unpack(buf, shapes):
    out, r0 = [], 0
    for shp in shapes:
        n = 1
        for s in shp:
            n *= s
        nr = -(-n // PACK_W)
        out.append(buf[r0:r0 + nr].reshape(-1)[:n].reshape(shp))
        r0 += nr
    return out


def _full_from_blocks(blocks, col_sharded):
    _, r, c = blocks.shape
    if col_sharded:
        return jnp.transpose(blocks, (1, 0, 2)).reshape(r, N_DEV * c)
    return blocks.reshape(N_DEV * r, c)


def _blocks_from_full(full, col_sharded):
    if col_sharded:
        r, c = full.shape[0], full.shape[1] // N_DEV
        return jnp.transpose(full.reshape(r, N_DEV, c), (1, 0, 2))
    return full.reshape(N_DEV, full.shape[0] // N_DEV, full.shape[1])


def _pad_cols(a, n):
    return jnp.pad(a, ((0, 0), (0, n - a.shape[1])))


def _w_cat(w_in):
    return jnp.concatenate([
        w_in[:, O_QKV:O_ZG], w_in[:, O_XBC:O_DT], w_in[:, O_ZS:O_XBC], w_in[:, O_GATE:O_END], w_in[:, O_ZG:O_AB],
        _pad_cols(w_in[:, O_AB:O_ZS], 128), _pad_cols(w_in[:, O_DT:O_GATE], 128)], axis=1)


def _w_uncat(wc):
    return jnp.concatenate([
        wc[:, C_QKV:C_XBC], wc[:, C_ZG:C_AB], wc[:, C_AB:C_AB + (O_ZS - O_AB)], wc[:, C_ZS:C_GATE], wc[:, C_XBC:C_ZS],
        wc[:, C_DT:C_DT + (O_GATE - O_DT)], wc[:, C_GATE:C_ZG]], axis=1)


def _pad_row(vec, n=128):
    vec = vec.reshape(1, -1)
    return _pad_cols(vec, n)


def kernel(x, c, ctx, c_ctx, ada_w, ada_b, norm1_w, w_in, gdn_conv_w, gdn_conv_b, gdn_a_log, gdn_dt_bias, gdn_norm_w, ssm_conv_w, ssm_conv_b, ssm_a_log, ssm_dt_bias, ssm_d, ssm_norm_w, w_br_gdn, w_br_ssm, w_out, norm2_w, w_ffn_in, w_ffn_out, norm_f_w, loss_target, m_c_ctx, m_ada_w, m_ada_b, m_norm1_w, m_w_in, m_gdn_conv_w, m_gdn_conv_b, m_gdn_a_log, m_gdn_dt_bias, m_gdn_norm_w, m_ssm_conv_w, m_ssm_conv_b, m_ssm_a_log, m_ssm_dt_bias, m_ssm_d, m_ssm_norm_w, m_w_br_gdn, m_w_br_ssm, m_w_out, m_norm2_w, m_w_ffn_in, m_w_ffn_out, m_norm_f_w, v_c_ctx, v_ada_w, v_ada_b, v_norm1_w, v_w_in, v_gdn_conv_w, v_gdn_conv_b, v_gdn_a_log, v_gdn_dt_bias, v_gdn_norm_w, v_ssm_conv_w, v_ssm_conv_b, v_ssm_a_log, v_ssm_dt_bias, v_ssm_d, v_ssm_norm_w, v_w_br_gdn, v_w_br_ssm, v_w_out, v_norm2_w, v_w_ffn_in, v_w_ffn_out, v_norm_f_w):
    wts = dict(c_ctx=c_ctx, ada_w=ada_w, ada_b=ada_b, norm1_w=norm1_w, w_in=w_in, gdn_conv_w=gdn_conv_w, gdn_conv_b=gdn_conv_b, gdn_a_log=gdn_a_log, gdn_dt_bias=gdn_dt_bias, gdn_norm_w=gdn_norm_w, ssm_conv_w=ssm_conv_w, ssm_conv_b=ssm_conv_b, ssm_a_log=ssm_a_log, ssm_dt_bias=ssm_dt_bias, ssm_d=ssm_d, ssm_norm_w=ssm_norm_w, w_br_gdn=w_br_gdn, w_br_ssm=w_br_ssm, w_out=w_out, norm2_w=norm2_w, w_ffn_in=w_ffn_in, w_ffn_out=w_ffn_out, norm_f_w=norm_f_w)
    mom1 = dict(c_ctx=m_c_ctx, ada_w=m_ada_w, ada_b=m_ada_b, norm1_w=m_norm1_w, w_in=m_w_in, gdn_conv_w=m_gdn_conv_w, gdn_conv_b=m_gdn_conv_b, gdn_a_log=m_gdn_a_log, gdn_dt_bias=m_gdn_dt_bias, gdn_norm_w=m_gdn_norm_w, ssm_conv_w=m_ssm_conv_w, ssm_conv_b=m_ssm_conv_b, ssm_a_log=m_ssm_a_log, ssm_dt_bias=m_ssm_dt_bias, ssm_d=m_ssm_d, ssm_norm_w=m_ssm_norm_w, w_br_gdn=m_w_br_gdn, w_br_ssm=m_w_br_ssm, w_out=m_w_out, norm2_w=m_norm2_w, w_ffn_in=m_w_ffn_in, w_ffn_out=m_w_ffn_out, norm_f_w=m_norm_f_w)
    mom2 = dict(c_ctx=v_c_ctx, ada_w=v_ada_w, ada_b=v_ada_b, norm1_w=v_norm1_w, w_in=v_w_in, gdn_conv_w=v_gdn_conv_w, gdn_conv_b=v_gdn_conv_b, gdn_a_log=v_gdn_a_log, gdn_dt_bias=v_gdn_dt_bias, gdn_norm_w=v_gdn_norm_w, ssm_conv_w=v_ssm_conv_w, ssm_conv_b=v_ssm_conv_b, ssm_a_log=v_ssm_a_log, ssm_dt_bias=v_ssm_dt_bias, ssm_d=v_ssm_d, ssm_norm_w=v_ssm_norm_w, w_br_gdn=v_w_br_gdn, w_br_ssm=v_w_br_ssm, w_out=v_w_out, norm2_w=v_norm2_w, w_ffn_in=v_w_ffn_in, w_ffn_out=v_w_ffn_out, norm_f_w=v_norm_f_w)
    order = list(wts)

    seq = x.shape[1]
    t = TM + seq
    ntl, nlt, nctx = t // TM, seq // TM, TM // CH

    me_i = 4 * lax.axis_index("x") + 2 * lax.axis_index("y") + lax.axis_index("c")
    gathered = _exchange("ag_weights", _AllGather([wts[n][0].astype(BF16) for n in BIG_FIRST]))
    full = {n: _full_from_blocks(blk, n in BIG_COL_SHARDED) for n, blk in zip(BIG_FIRST, gathered)}
    late_gather = _AllGather([wts[n][0].astype(BF16) for n in BIG_LATE])
    conv_sh = _pack([wts[n] for n in CONV], rows=8)
    conv_g = _exchange("ag_conv", _AllGather([conv_sh]))[0].reshape(N_DEV, -1)
    ncv = 3 * CONV_SHARD
    for i, n in enumerate(CONV):
        off = -(-ncv // PACK_W) * PACK_W * i
        full[n] = jnp.transpose(conv_g[:, off:off + ncv].reshape(N_DEV, 3, CONV_SHARD), (1, 0, 2)).reshape(3, XBC)
    w_cat = _w_cat(full["w_in"])
    gcw = full["gdn_conv_w"].reshape(3, 1, XBC)
    scw = full["ssm_conv_w"].reshape(3, 1, XBC)

    n1w, n2w, nfw = norm1_w.reshape(1, D), norm2_w.reshape(1, D), norm_f_w.reshape(1, D)
    gcb, scb = gdn_conv_b.reshape(1, XBC), ssm_conv_b.reshape(1, XBC)
    alog16, dtb16 = _pad_row(gdn_a_log), _pad_row(gdn_dt_bias)
    alog64, dtb64 = _pad_row(ssm_a_log), _pad_row(ssm_dt_bias)
    gnw = gdn_norm_w.reshape(1, DK)
    ssd8 = jnp.tile(_pad_row(ssm_d), (8, 1))
    snw = ssm_norm_w.reshape(1, S_INNER)
    x2 = x[0]
    tgt = loss_target[0]
    cvec = jnp.concatenate([c, c_ctx.reshape(1, D), jnp.zeros((14, D), F32)], axis=0)

    a16 = _rowwise("silu_c", f_silu_rows, [(cvec, D, 0, 0)], [], [(D, BF16)], 1, tm=16)[0]
    mod = _mm("mm_mod", a16, full["ada_w"], "nn", F32) + ada_b
    sh1, sc1, g1, sh2, sc2, g2 = [mod[0:1, i * D:(i + 1) * D] for i in range(6)]
    csh1, csc1 = mod[1:2, 0:D], mod[1:2, D:2 * D]

    pre_pars = [n1w, sc1, sh1, csc1, csh1]
    pre_rows = [(x2, D, 0, -1, True), (ctx[0], D, 0, 0, "first")]
    a = _rowwise("pre", f_pre, pre_rows, pre_pars, [(D, BF16)], ntl, base=0)[0]
    proj = _mm("mm_proj", a, w_cat, "nn", F32, tm=1408, tn=1280)
    gp_rows = [(proj, XBC, C_QKV // XBC, 0), (proj, 128, C_AB // 128, 0)]
    gp_pars = [gcw, gcb, alog16, dtb16]
    q, k, v, gcum, beta = _rowwise("gdnprep", f_gdnprep, gp_rows, gp_pars, [(D, F32)] * 3 + [(128, F32)] * 2, ntl, base=0)
    sp_rows = [(proj, XBC, C_XBC // XBC, 0), (proj, 128, C_DT // 128, 0)]
    sp_pars = [scw, scb, alog64, dtb64]
    xs, bm, cm, dtc, acc = _rowwise(
        "ssmprep", f_ssmprep, sp_rows, sp_pars, [(S_INNER, F32), (512, F32), (512, F32), (128, F32), (128, F32)], ntl, base=0)
    o0, ss0, ts0, gathered = _gdn_scan_fwd("gdn_fwd0", q, k, v, gcum, beta, 0, nctx, ride=late_gather)
    full.update({n: _full_from_blocks(blk, n in BIG_COL_SHARDED) for n, blk in zip(BIG_LATE, gathered)})
    o_sum, ss1, ts1 = _gdn_scan_fwd("gdn_fwd1", q, k, v, gcum, beta, 1, nctx, add=o0)
    y0, hs0 = _ssd_scan_fwd("ssd_fwd0", xs, dtc, acc, bm, cm, 0, nctx)
    y_sum, hs1 = _ssd_scan_fwd("ssd_fwd1", xs, dtc, acc, bm, cm, 1, nctx, add=y0)
    post_rows = [(o_sum, D, 0, 1), (proj, D, C_ZG // D, 1), (y_sum, S_INNER, 0, 1), (xs, S_INNER, 0, 1),
                 (proj, S_INNER, C_ZS // S_INNER, 1)]
    post_pars = [gnw, ssd8, snw]
    og, ys = _rowwise("post", f_post, post_rows, post_pars, [(D, BF16), (S_INNER, BF16)], nlt)
    pg = _mm("mm_pg", og, full["w_br_gdn"], "nn", F32)
    ps = _mm("mm_ps", ys, full["w_br_ssm"], "nn", F32, tk=2048)
    merge_rows = [(proj, S_INNER, C_GATE // S_INNER, 1), (pg, D, 0, 0), (ps, D, 0, 0)]
    merged = _rowwise("merge", f_merge, merge_rows, [], [(D, BF16)], nlt)[0]
    mix = _mm("mm_mix", merged, full["w_out"], "nn", F32)
    res_rows = [(x2, D, 0, 0), (mix, D, 0, 0)]
    res_pars = [g1, n2w, sc2, sh2]
    h1, f = _rowwise("res1", f_res1, res_rows, res_pars, [(D, F32), (D, BF16)], nlt)
    u = _mm("mm_u", f, full["w_ffn_in"], "nn", F32, tn=1408)
    hact = _rowwise("act", f_act, [(u, 2 * D_FF, 0, 0)], [], [(D_FF, BF16)], nlt)[0]
    ff = _mm("mm_ff", hact, full["w_ffn_out"], "nn", F32, tk=2816)

    fin_rows = [(h1, D, 0, 0), (ff, D, 0, 0), (tgt, D, 0, 0)]
    d_h1a, d_ff, d_g2, d_nfw, loss_acc = _rowwise_bwd(
        "final", f_final, fin_rows, [g2, nfw], ["one"], [(0, F32), (1, BF16)], [0, 1], nlt, loss_out=True)
    d_hact = _mm("mm_dhact", d_ff, full["w_ffn_out"], "nt", BF16, tn=1408)
    g_w_ffn_out = _mm("mm_gwffo", hact, d_ff, "tn", BF16, tm=1408, tk=2048)
    d_u = _rowwise_bwd("act_bwd", f_act, [(u, 2 * D_FF, 0, 0)], [], [[(d_hact, D_FF, 0, 0)]], [(0, BF16)], [], nlt)[0]
    d_f = _mm("mm_df", d_u, full["w_ffn_in"], "nt", BF16, tk=2816)
    g_w_ffn_in = _mm("mm_gwffi", f, d_u, "tn", BF16, tn=1408, tk=2048)
    d_xres, d_mix, d_g1, d_n2w, d_sc2, d_sh2 = _rowwise_bwd(
        "res1_bwd", f_res1, res_rows, res_pars, [[(d_h1a, D, 0, 0)], [(d_f, D, 0, 0)]], [(0, F32), (1, BF16)], [0, 1, 2, 3], nlt)
    d_merged = _mm("mm_dmerged", d_mix, full["w_out"], "nt", BF16)
    g_w_out = _mm("mm_gwout", merged, d_mix, "tn", BF16, tk=2048)
    d_gate, d_pg, d_ps = _rowwise_bwd(
        "merge_bwd", f_merge, merge_rows, [], [[(d_merged, D, 0, 0)]], [(0, BF16), (1, BF16), (2, BF16)], [], nlt)
    d_og = _mm("mm_dog", d_pg, full["w_br_gdn"], "nt", BF16)
    g_w_br_gdn = _mm("mm_gwbrg", og, d_pg, "tn", BF16, tk=2048)
    d_ys = _mm("mm_dys", d_ps, full["w_br_ssm"], "nt", BF16, tn=2048)
    g_w_br_ssm = _mm("mm_gwbrs", ys, d_ps, "tn", BF16, tk=2048)
    d_o, d_zg, d_y, d_xs_post, d_zs, d_gnw, d_ssd8, d_snw = _rowwise_bwd(
        "post_bwd", f_post, post_rows, post_pars, [[(d_og, D, 0, 0)], [(d_ys, S_INNER, 0, 0)]],
        [(0, F32), (1, BF16), (2, F32), (3, F32), (4, BF16)], [0, 1, 2], nlt)
    late_grads = dict(w_br_gdn=g_w_br_gdn, w_br_ssm=g_w_br_ssm, w_out=g_w_out, w_ffn_in=g_w_ffn_in, w_ffn_out=g_w_ffn_out)
    late_a2a = _AllToAll([_blocks_from_full(late_grads[n], n in BIG_COL_SHARDED) for n in BIG_LATE])
    dq0, dk0, dv0, dg0, db0, recv_late = _gdn_scan_bwd("gdn_bwd0", q, k, v, gcum, beta, ss0, ts0, d_o, 0, nctx, ride=late_a2a)
    dq1, dk1, dv1, dg1, db1 = _gdn_scan_bwd("gdn_bwd1", q, k, v, gcum, beta, ss1, ts1, d_o, 1, nctx)
    dxs0, ddt0, dac0, dbm0, dcm0 = _ssd_scan_bwd("ssd_bwd0", xs, dtc, acc, bm, cm, hs0, d_y, 0, nctx)
    dxs1, ddt1, dac1, dbm1, dcm1 = _ssd_scan_bwd("ssd_bwd1", xs, dtc, acc, bm, cm, hs1, d_y, 1, nctx)
    row = lambda arr, wd: (arr, wd, 0, 0)
    d_qkv_raw, d_ab, d_gcw, d_gcb, d_alog16, d_dtb16 = _rowwise_bwd(
        "gdnprep_bwd", f_gdnprep, gp_rows, gp_pars,
        [[row(dq0, D), row(dq1, D)], [row(dk0, D), row(dk1, D)], [row(dv0, D), row(dv1, D)],
         [row(dg0, 128), row(dg1, 128)], [row(db0, 128), row(db1, 128)]],
        [(0, BF16), (1, BF16)], [0, 1, 2, 3], ntl, base=0)
    d_xbc_raw, d_dt, d_scw, d_scb, d_alog64, d_dtb64 = _rowwise_bwd(
        "ssmprep_bwd", f_ssmprep, sp_rows, sp_pars,
        [[row(dxs0, S_INNER), row(dxs1, S_INNER), (d_xs_post, S_INNER, 0, -1, True)], [row(dbm0, 512), row(dbm1, 512)],
         [row(dcm0, 512), row(dcm1, 512)], [row(ddt0, 128), row(ddt1, 128)], [row(dac0, 128), row(dac1, 128)]],
        [(0, BF16), (1, BF16)], [0, 1, 2, 3], ntl, base=0)
    ctx_zero = lambda wd: jnp.zeros((TM, wd), BF16)
    d_proj = jnp.concatenate([
        d_qkv_raw, d_xbc_raw, jnp.concatenate([ctx_zero(S_INNER), d_zs], axis=0),
        jnp.concatenate([ctx_zero(S_INNER), d_gate], axis=0), jnp.concatenate([ctx_zero(D), d_zg], axis=0), d_ab, d_dt], axis=1)
    g_w_cat = _mm("mm_gwcat", a, d_proj, "tn", BF16, tn=768, tk=2816)
    w_in_a2a = _AllToAll([_blocks_from_full(_w_uncat(g_w_cat), True)])
    d_a, recv_w_in = _mm("mm_da", d_proj, w_cat, "nt", BF16, tk=3840, ride=w_in_a2a)
    d_x, d_n1w, d_sc1, d_sh1, d_csc1, d_csh1 = _rowwise_bwd(
        "pre_bwd", f_pre_thru, pre_rows, pre_pars, [[row(d_a, D)], [(d_xres, D, 0, -1, True)]],
        [(0, F32)], [0, 1, 2, 3, 4], ntl, base=0)
    zero4 = jnp.zeros((1, 4 * D), F32)
    d_mod = jnp.concatenate([
        jnp.concatenate([d_sh1, d_sc1, d_g1, d_sh2, d_sc2, d_g2], axis=1),
        jnp.concatenate([d_csh1, d_csc1, zero4], axis=1), jnp.zeros((14, 6 * D), F32)], axis=0)
    d_a16 = _mm("mm_da16", d_mod, full["ada_w"], "nt", F32)
    d_cvec = _rowwise_bwd("silu_c_bwd", f_silu_rows, [(cvec, D, 0, 0)], [], [[row(d_a16, D)]], [(0, F32)], [], 1, tm=16)[0]

    recv = dict(zip(BIG_LATE, recv_late), w_in=recv_w_in[0])
    big_un = {n: _reduce_adam("adam_" + n, recv[n], wts[n][0], mom1[n][0], mom2[n][0], BIG_ADAM_ROWS[n])
              for n in BIG if n != "ada_w"}

    small_g = dict(c_ctx=d_cvec[1], ada_b=d_mod[0] + d_mod[1], norm1_w=d_n1w, gdn_conv_b=d_gcb,
                   gdn_a_log=d_alog16[0, :2 * G_HEADS], gdn_dt_bias=d_dtb16[0, :2 * G_HEADS], gdn_norm_w=d_gnw,
                   ssm_conv_b=d_scb, ssm_a_log=d_alog64[0, :2 * S_HEADS], ssm_dt_bias=d_dtb64[0, :2 * S_HEADS],
                   ssm_d=d_ssd8[0, :S_HEADS], ssm_norm_w=d_snw, norm2_w=d_n2w, norm_f_w=d_nfw,
                   gdn_conv_w=d_gcw.reshape(3, XBC), ssm_conv_w=d_scw.reshape(3, XBC))
    small_names = SMALL + CONV
    factors = [a16[0].astype(F32), d_mod[0], d_mod[1]]
    round8 = lambda r: -(-r // 8) * 8
    packed_rows = lambda arrs: sum(-(-arr.size // PACK_W) for arr in arrs)
    n_small = packed_rows([small_g[n] for n in small_names])
    n_fac = packed_rows(factors)
    rows_small = round8(n_small)
    sg_pack = _pack([small_g[n] for n in small_names] + factors, rows=round8(n_small + n_fac))
    recv_all = _exchange("ag_small_grads", _AllGather([sg_pack]))[0]
    recv_s = recv_all[:, :rows_small]
    fac = recv_all[:, n_small:n_small + n_fac].reshape(N_DEV, -1)
    my_cols = lambda z: lax.dynamic_slice(z, (0, me_i * (6 * D // N_DEV)), (N_DEV, 6 * D // N_DEV))
    lhs = jnp.concatenate([fac[:, :D], jnp.broadcast_to(a16[1:2].astype(F32), (N_DEV, D))], axis=0)
    rhs = jnp.concatenate([my_cols(fac[:, D:7 * D]), my_cols(fac[:, 7 * D:])], axis=0)
    g_ada_w = _mm("mm_gwada", lhs, rhs, "tn", F32)
    big_un["ada_w"] = _reduce_adam("adam_ada_w", g_ada_w[None], wts["ada_w"][0], mom1["ada_w"][0], mom2["ada_w"][0],
                                   BIG_ADAM_ROWS["ada_w"])

    def placed(src, n):
        if n not in CONV:
            return src[n]
        return lax.dynamic_update_slice(jnp.zeros((3, XBC), F32), src[n][0], (0, me_i * CONV_SHARD))

    small_out = _reduce_adam("adam_small", recv_s, *[_pack([placed(src, n) for n in small_names], rows=rows_small)
                                                     for src in (wts, mom1, mom2)], rows_small)
    small_shapes = [wts[n].shape if n in SMALL else (3, XBC) for n in small_names]
    small_un = [_unpack(buf, small_shapes) for buf in small_out]

    res = [{}, {}, {}, {}]
    for kind in range(4):
        for n in BIG:
            res[kind][n] = big_un[n][kind].reshape(wts[n].shape)
        for n, val in zip(small_names, small_un[kind]):
            if n in CONV:
                val = lax.dynamic_slice(val, (0, me_i * CONV_SHARD), (3, CONV_SHARD)).reshape(wts[n].shape)
            res[kind][n] = val
    loss = lax.psum(loss_acc[0, 0], ("x", "y", "c"))
    grad_x = d_x.reshape(x.shape)
    return (loss, grad_x, *[res[0][n] for n in order], *[res[1][n] for n in order], *[res[2][n] for n in order],
            *[res[3][n] for n in order])
```

```python
import functools

import jax
import jax.numpy as jnp
from jax import lax
from jax.experimental import pallas as pl
from jax.experimental.pallas import tpu as pltpu

F32 = jnp.float32
BF16 = jnp.bfloat16
HI = lax.Precision.HIGHEST
HIGH = lax.Precision.HIGH
MESH = pl.DeviceIdType.MESH

D = 1024
CH = 64
TM = 256
EPS = 1e-6
NEG = -1e30
G_HEADS = 8
DK = 128
S_HEADS = 32
S_P = 64
S_GROUPS = 4
S_N = 128
S_INNER = 2048
XBC = 3072
D_FF = 2816
N_DEV = 8
PACK_W = 1024
VMEM_LIMIT = 56 * 1024 * 1024

ADAM_LR = 0.001
ADAM_B1 = 0.9
ADAM_B2 = 0.999
ADAM_EPS = 1e-08
ADAM_WD = 0.01
ADAM_STEP = 10

C_QKV, C_XBC, C_ZS, C_GATE, C_ZG, C_AB, C_DT, C_END = 0, 3072, 6144, 8192, 10240, 11264, 11392, 11520
O_QKV, O_ZG, O_AB, O_ZS, O_XBC, O_DT, O_GATE, O_END = 0, 3072, 4096, 4128, 6176, 9248, 9312, 11360


def _dot(a, b, prec=None):
    return jnp.dot(a, b, precision=prec, preferred_element_type=F32)


def _dot_nt(a, b, prec=None):
    return lax.dot_general(a, b, (((1,), (1,)), ((), ())), precision=prec, preferred_element_type=F32)


def _dot_tn(a, b, prec=None):
    return lax.dot_general(a, b, (((0,), (0,)), ((), ())), precision=prec, preferred_element_type=F32)


def _iota(shape, dim):
    return lax.broadcasted_iota(jnp.int32, shape, dim)


def _rms(x):
    return x * lax.rsqrt(jnp.mean(x * x, axis=-1, keepdims=True) + EPS)


def _l2n(x):
    return x * lax.rsqrt(jnp.sum(x * x, axis=-1, keepdims=True) + EPS)


def _silu(x):
    return x * jax.nn.sigmoid(x)


def _softplus(x):
    return jnp.maximum(x, 0.0) + jnp.log1p(jnp.exp(-jnp.abs(x)))


def _roll_rows(x, s):
    return pltpu.roll(x, s, 0)


def _up_raw(x, keep_up):
    return jnp.where(keep_up > 0.0, _roll_rows(x, 1), 0.0)


def _dn_raw(x, keep_dn):
    return jnp.where(keep_dn > 0.0, _roll_rows(x, x.shape[0] - 1), 0.0)


@jax.custom_vjp
def _shift_up(x, keep_up, keep_dn):
    return _up_raw(x, keep_up)


def _shift_up_fwd(x, keep_up, keep_dn):
    return _up_raw(x, keep_up), (keep_up, keep_dn)


def _shift_up_bwd(res, g):
    keep_up, keep_dn = res
    return _dn_raw(g, keep_dn), jnp.zeros_like(keep_up), jnp.zeros_like(keep_dn)


_shift_up.defvjp(_shift_up_fwd, _shift_up_bwd)


@jax.custom_vjp
def _shift_dn(x, keep_up, keep_dn):
    return _dn_raw(x, keep_dn)


def _shift_dn_fwd(x, keep_up, keep_dn):
    return _dn_raw(x, keep_dn), (keep_up, keep_dn)


def _shift_dn_bwd(res, g):
    keep_up, keep_dn = res
    return _up_raw(g, keep_up), jnp.zeros_like(keep_up), jnp.zeros_like(keep_dn)


_shift_dn.defvjp(_shift_dn_fwd, _shift_dn_bwd)


def _conv_keep(is_ctx, n):
    r = _iota((n, 1), 0)
    pos = jnp.where(is_ctx, r, r & (CH - 1))
    end = jnp.where(is_ctx, n - 1, CH - 1)
    return jnp.where(pos == 0, 0.0, 1.0).astype(F32), jnp.where(pos == end, 0.0, 1.0).astype(F32)


def _conv_silu(u, w3, b, keep_up, keep_dn):
    conv = b + _shift_up(u, keep_up, keep_dn) * w3[0] + u * w3[1] + _shift_dn(u, keep_up, keep_dn) * w3[2]
    return _silu(conv)


def _chunk_tri(n, rev):
    i = _iota((n, n), 0)
    j = _iota((n, n), 1)
    same = (i // CH) == (j // CH)
    seen = (i <= j) if rev else (i >= j)
    return jnp.where(same & seen, 1.0, 0.0).astype(F32)


def _expand_mat(rows, cols, per, base):
    r = _iota((rows, cols), 0)
    c = _iota((rows, cols), 1)
    return jnp.where(r == base + c // per, 1.0, 0.0).astype(F32)


def f_silu_rows(is_ctx, cvec):
    return (_silu(cvec).astype(BF16),)


def f_pre(is_ctx, x, xc, n1w, sc, sh, csc, csh):
    x = jnp.where(is_ctx, xc, x)
    sc_e = jnp.where(is_ctx, csc, sc)
    sh_e = jnp.where(is_ctx, csh, sh)
    a = _rms(x) * n1w * (1.0 + sc_e) + sh_e
    return (a.astype(BF16),)


def f_pre_thru(is_ctx, x, xc, n1w, sc, sh, csc, csh):
    return f_pre(is_ctx, x, xc, n1w, sc, sh, csc, csh)[0], x


def f_gdnprep(is_ctx, qkv_raw, ab_raw, cw, cb, alog, dtb):
    n = qkv_raw.shape[0]
    keep_up, keep_dn = _conv_keep(is_ctx, n)
    s = _conv_silu(qkv_raw, cw, cb, keep_up, keep_dn)
    qs, ks, vs = [], [], []
    for h in range(G_HEADS):
        qs.append(_l2n(s[:, h * DK:(h + 1) * DK]) * (DK ** -0.5))
        ks.append(_l2n(s[:, D + h * DK:D + (h + 1) * DK]))
    q = jnp.concatenate(qs, axis=1)
    k = jnp.concatenate(ks, axis=1)
    v = s[:, 2 * D:3 * D]
    lane = _iota(ab_raw.shape, 1)
    g = jnp.where(lane < 2 * G_HEADS, -jnp.exp(alog) * _softplus(ab_raw + dtb), 0.0)
    gcum = jnp.where(lane < G_HEADS, _dot(_chunk_tri(n, False), g, HI), _dot(_chunk_tri(n, True), g, HI))
    beta = jax.nn.sigmoid(ab_raw)
    return q, k, v, gcum, beta


def f_ssmprep(is_ctx, xbc_raw, dt_raw, cw, cb, alog, dtb):
    n = xbc_raw.shape[0]
    keep_up, keep_dn = _conv_keep(is_ctx, n)
    s = _conv_silu(xbc_raw, cw, cb, keep_up, keep_dn)
    xs = s[:, :S_INNER]
    bm = s[:, S_INNER:S_INNER + S_GROUPS * S_N]
    cm = s[:, S_INNER + S_GROUPS * S_N:]
    lane = _iota(dt_raw.shape, 1)
    dt = jnp.where(lane < 2 * S_HEADS, _softplus(dt_raw + dtb), 0.0)
    da = dt * (-jnp.exp(alog))
    acum = jnp.where(lane < S_HEADS, _dot(_chunk_tri(n, False), da, HI), _dot(_chunk_tri(n, True), da, HI))
    return xs, bm, cm, dt, acum


def f_post(is_ctx, o, zg, y_scan, xs, zs, gnw, ssd8, snw):
    ogs = []
    for h in range(G_HEADS):
        sl = slice(h * DK, (h + 1) * DK)
        ogs.append(_rms(o[:, sl]) * gnw * _silu(zg[:, sl]))
    og = jnp.concatenate(ogs, axis=1)
    row0 = jnp.where(_iota(ssd8.shape, 0) == 0, 1.0, 0.0).astype(F32)
    dexp = jnp.sum(_dot(ssd8 * row0, _expand_mat(128, S_INNER, S_P, 0), HI), axis=0, keepdims=True)
    y = (y_scan + dexp * xs) * _silu(zs)
    gw = S_INNER // S_GROUPS
    ys = jnp.concatenate([_rms(y[:, i * gw:(i + 1) * gw]) * snw[:, i * gw:(i + 1) * gw] for i in range(S_GROUPS)], axis=1)
    return og.astype(BF16), ys.astype(BF16)


def f_merge(is_ctx, gate, pg, ps):
    m = jax.nn.sigmoid(gate[:, :D]) * pg + jax.nn.sigmoid(gate[:, D:]) * ps
    return (m.astype(BF16),)


def f_res1(is_ctx, x, mix, g1, n2w, sc2, sh2):
    h1 = x + g1 * mix
    f = _rms(h1) * n2w * (1.0 + sc2) + sh2
    return h1, f.astype(BF16)


def f_act(is_ctx, u):
    return ((_silu(u[:, :D_FF]) * u[:, D_FF:]).astype(BF16),)


def f_final(is_ctx, h1, ff, tgt, g2, nfw):
    h2 = h1 + g2 * ff
    y = _rms(h2) * nfw
    err = y - tgt
    return (0.5 * jnp.sum(jnp.mean(err * err, axis=-1, keepdims=True), axis=0, keepdims=True),)


def _each(fn, *lists):
    return [fn(*args) for args in zip(*lists)]


def _tri_inverse_all(mats):
    n = mats[0].shape[0]
    eye = jnp.where(_iota((n, n), 0) == _iota((n, n), 1), 1.0, 0.0).astype(F32)
    t = [eye - a for a in mats]
    p = [_dot(a, a, HIGH) for a in mats]
    for r in range(5):
        t = _each(lambda t_, p_: t_ + _dot(t_, p_, HIGH), t, p)
        if r < 4:
            p = [_dot(p_, p_, HIGH) for p_ in p]
    return t


@jax.custom_vjp
def _inverse_given(a, t):
    return t


def _inverse_given_fwd(a, t):
    return t, t


def _inverse_given_bwd(t, g):
    return -_dot_nt(_dot_tn(t, g, HIGH), t, HIGH), jnp.zeros_like(t)


_inverse_given.defvjp(_inverse_given_fwd, _inverse_given_bwd)


def gdn_local(qs, ks, vs, gcs, grs, bcs, rev, t_known=None):
    c = qs[0].shape[0]
    ii = _iota((c, c), 0)
    jj = _iota((c, c), 1)
    incl = (ii <= jj) if rev else (ii >= jj)
    strict = (ii < jj) if rev else (ii > jj)
    decay = _each(lambda gc, gr: jnp.exp(jnp.where(incl, gc - gr, NEG)), gcs, grs)
    kb = _each(lambda k, bc: k * bc, ks, bcs)
    a = _each(lambda kb_, k, dc: jnp.where(strict, _dot_nt(kb_, k) * dc, 0.0), kb, ks, decay)
    t = _tri_inverse_all(a) if t_known is None else _each(_inverse_given, a, t_known)
    eg = [jnp.exp(gc) for gc in gcs]
    rhs = _each(lambda kb_, eg_, v, bc: jnp.concatenate([kb_ * eg_, v * bc], axis=1), kb, eg, vs, bcs)
    wu = _each(lambda t_, r: _dot(t_, r, HIGH), t, rhs)
    lhs = _each(lambda wu_, q, eg_: jnp.concatenate([wu_[:, :DK], q * eg_], axis=0), wu, qs, eg)
    attn = _each(lambda q, k, dc: _dot_nt(q, k) * dc, qs, ks, decay)
    return wu, attn, lhs, t


def gdn_state(ss, wu, attn, lhs, ks, gcs, rev):
    c = ks[0].shape[0]
    is_last = _iota((c, 1), 0) == (0 if rev else c - 1)
    ws = _each(_dot, lhs, ss)
    v_new = _each(lambda wu_, ws_: wu_[:, DK:] - ws_[:c], wu, ws)
    o = _each(lambda ws_, at, vn: ws_[c:] + _dot(at, vn), ws, attn, v_new)
    gtot = [jnp.sum(jnp.where(is_last, gc, 0.0), axis=0, keepdims=True) for gc in gcs]
    s_new = _each(lambda s, k, gc, gt_, vn: s * jnp.exp(gt_) + _dot_tn(k * jnp.exp(gt_ - gc), vn), ss, ks, gcs, gtot, v_new)
    return s_new, o


def gdn_chunk(ss, qs, ks, vs, gcs, grs, bcs, rev, t_known=None):
    wu, attn, lhs, t = gdn_local(qs, ks, vs, gcs, grs, bcs, rev, t_known)
    s_new, o = gdn_state(ss, wu, attn, lhs, ks, gcs, rev)
    return s_new, o, t


def ssd_pick(dt0s, dt1s, ac0s, ac1s, ar0s, ar1s):
    lo = _iota((CH, 128), 1) < S_P
    pick = lambda u0, u1: jnp.where(lo, u0, u1)
    return _each(pick, dt0s, dt1s), _each(pick, ac0s, ac1s), _each(pick, ar0s, ar1s)


def ssd_local(xs, dts, acs, acr, bgs, cgs, rev):
    c = xs[0].shape[0]
    npair = len(xs)
    grp = [p * len(bgs) // npair for p in range(npair)]
    lane = _iota((c, 128), 1)
    ii = _iota((c, 128), 0)
    jl = lane & (S_P - 1)
    lo = lane < S_P
    seen = (ii <= jl) if rev else (ii >= jl)
    last = 0 if rev else c - 1
    split = lambda z: jnp.concatenate([jnp.where(lo, z, 0.0), jnp.where(lo, 0.0, z)], axis=0)
    cb = _each(lambda bg, cg: _dot_nt(cg, jnp.concatenate([bg, bg], axis=0)), bgs, cgs)
    seg = _each(lambda ac, ar: jnp.exp(jnp.where(seen, ac - ar, NEG)), acs, acr)
    xdt = _each(lambda x, dt: x * dt, xs, dts)
    ydiag = [_dot(cb[grp[p]] * seg[p], split(xdt[p])) for p in range(npair)]
    eac = [jnp.exp(ac) for ac in acs]
    atot = [jnp.sum(jnp.where(ii == last, ac, 0.0), axis=0, keepdims=True) for ac in acs]
    upd = [_dot_tn(bgs[grp[p]], xdt[p] * jnp.exp(atot[p] - acs[p])) for p in range(npair)]
    return ydiag, eac, [jnp.exp(at) for at in atot], upd


def ssd_state(hts, ydiag, eac, etot, upd, cgs):
    npair = len(hts)
    grp = [p * len(cgs) // npair for p in range(npair)]
    y = [ydiag[p] + _dot(cgs[grp[p]], hts[p]) * eac[p] for p in range(npair)]
    h_new = [hts[p] * etot[p] + upd[p] for p in range(npair)]
    return h_new, y


def ssd_chunk(hts, xs, dts, acs, acr, bgs, cgs, rev):
    ydiag, eac, etot, upd = ssd_local(xs, dts, acs, acr, bgs, cgs, rev)
    return ssd_state(hts, ydiag, eac, etot, upd, cgs)


def f_adamw(w, g, m, v):
    m = ADAM_B1 * m + (1.0 - ADAM_B1) * g
    v = ADAM_B2 * v + (1.0 - ADAM_B2) * jnp.square(g)
    m_hat = m / (1.0 - ADAM_B1 ** ADAM_STEP)
    v_hat = v / (1.0 - ADAM_B2 ** ADAM_STEP)
    delta = -ADAM_LR * (m_hat / (jnp.sqrt(v_hat) + ADAM_EPS) + ADAM_WD * w)
    return delta, m, v


def _cparams(sem):
    return pltpu.CompilerParams(dimension_semantics=sem, vmem_limit_bytes=VMEM_LIMIT)


def _pick(n, target):
    if n <= target:
        return n
    best = None
    for t in range(128, target + 1, 128):
        if n % t == 0:
            best = t
    assert best is not None, (n, target)
    return best


def _row_spec(tm, width, colblk, rowoff):
    return pl.BlockSpec((tm, width), lambda i: (i + rowoff, colblk))


def _par_spec(shape):
    nd = len(shape)
    return pl.BlockSpec(tuple(shape), lambda i: (0,) * nd)


def _rowwise(name, fn, rows, pars, outs, ntiles, base=1, tm=TM):
    nr, npar = len(rows), len(pars)

    def body(*refs):
        is_ctx = (pl.program_id(0) + base) == 0
        res = fn(is_ctx, *[r[...] for r in refs[:nr]], *[p[...] for p in refs[nr:nr + npar]])
        for o_ref, r in zip(refs[nr + npar:], res):
            o_ref[...] = r.astype(o_ref.dtype)

    return pl.pallas_call(
        body, grid=(ntiles,), name=name,
        in_specs=[_ct_spec(tm, d) for d in rows] + [_par_spec(p.shape) for p in pars],
        out_specs=[_row_spec(tm, wd, 0, 0) for (wd, _) in outs],
        out_shape=[jax.ShapeDtypeStruct((ntiles * tm, wd), dt) for (wd, dt) in outs],
        compiler_params=_cparams(("arbitrary",)),
    )(*[r[0] for r in rows], *pars)


def _ct_spec(tm, desc):
    _, wd, cb, ro = desc[:4]
    flag = desc[4] if len(desc) > 4 else False
    if flag == "first":
        return pl.BlockSpec((tm, wd), lambda i: (0, cb))
    if flag:
        return pl.BlockSpec((tm, wd), lambda i: (jnp.maximum(i + ro, 0), cb))
    return _row_spec(tm, wd, cb, ro)


def _rowwise_bwd(name, fn, rows, pars, cts, drows, dpars, ntiles, base=1, loss_out=False, tm=TM):
    nr, npar = len(rows), len(pars)
    ct_rows = [d for ct in cts if isinstance(ct, list) for d in ct]
    nct = len(ct_rows)

    def body(*refs):
        i = pl.program_id(0)
        is_ctx = (i + base) == 0
        rows_v = [r[...] for r in refs[:nr]]
        pars_v = [p[...] for p in refs[nr:nr + npar]]
        ct_refs = list(refs[nr + npar:nr + npar + nct])
        out_refs = list(refs[nr + npar + nct:])
        outs, vjp = jax.vjp(lambda rv, pv: fn(is_ctx, *rv, *pv), rows_v, pars_v)

        def ct_value(desc):
            val = ct_refs.pop(0)[...].astype(F32)
            if len(desc) > 4 and desc[4]:
                val = jnp.where(is_ctx, 0.0, val)
            return val

        ct_vals = []
        for o, ct in zip(outs, cts):
            if ct is None:
                ct_vals.append(jnp.zeros_like(o))
            elif isinstance(ct, str):
                ct_vals.append(jnp.ones_like(o))
            else:
                acc = ct_value(ct[0])
                for desc in ct[1:]:
                    acc = acc + ct_value(desc)
                ct_vals.append(acc.astype(o.dtype))
        d_rows, d_pars = vjp(tuple(ct_vals))
        for (ri, _), o_ref in zip(drows, out_refs[:len(drows)]):
            o_ref[...] = d_rows[ri].astype(o_ref.dtype)
        acc_refs = out_refs[len(drows):]
        acc_vals = [d_pars[pi] for pi in dpars]
        if loss_out:
            acc_vals.append(jnp.broadcast_to(outs[0], (8, 128)))

        @pl.when(i == 0)
        def _():
            for o_ref, val in zip(acc_refs, acc_vals):
                o_ref[...] = val

        @pl.when(i > 0)
        def _():
            for o_ref, val in zip(acc_refs, acc_vals):
                o_ref[...] += val

    acc_shapes = [pars[pi].shape for pi in dpars] + ([(8, 128)] if loss_out else [])
    lat_row = lambda ri: len(rows[ri]) > 4 and rows[ri][4] is True
    return pl.pallas_call(
        body, grid=(ntiles,), name=name,
        in_specs=[_ct_spec(tm, d) for d in rows] + [_par_spec(p.shape) for p in pars]
        + [_ct_spec(tm, d) for d in ct_rows],
        out_specs=[_ct_spec(tm, (None, rows[ri][1], 0, -1, True) if lat_row(ri) else (None, rows[ri][1], 0, 0))
                   for (ri, _) in drows] + [_par_spec(s) for s in acc_shapes],
        out_shape=[jax.ShapeDtypeStruct(((ntiles - int(lat_row(ri))) * tm, rows[ri][1]), dt) for (ri, dt) in drows]
        + [jax.ShapeDtypeStruct(tuple(s), F32) for s in acc_shapes],
        compiler_params=_cparams(("arbitrary",)),
    )(*[r[0] for r in rows], *pars, *[r[0] for r in ct_rows])


def _mm(name, a, b, mode, out_dtype, tm=1024, tn=1024, tk=1024, ride=None):
    if mode == "nn":
        (m, kd), (_, n) = a.shape, b.shape
    elif mode == "nt":
        (m, kd), (n, _) = a.shape, b.shape
    else:
        (kd, m), (_, n) = a.shape, b.shape
    tm, tn, tk = _pick(m, tm), _pick(n, tn), _pick(kd, tk)
    nk = kd // tk
    a_spec = {"nn": pl.BlockSpec((tm, tk), lambda i, j, k: (i, k)), "nt": pl.BlockSpec((tm, tk), lambda i, j, k: (i, k)),
              "tn": pl.BlockSpec((tk, tm), lambda i, j, k: (k, i))}[mode]
    b_spec = {"nn": pl.BlockSpec((tk, tn), lambda i, j, k: (k, j)), "nt": pl.BlockSpec((tn, tk), lambda i, j, k: (j, k)),
              "tn": pl.BlockSpec((tk, tn), lambda i, j, k: (k, j))}[mode]
    dot = {"nn": _dot, "nt": _dot_nt, "tn": _dot_tn}[mode]

    if nk == 1:
        def body(a_ref, b_ref, o_ref):
            o_ref[...] = dot(a_ref[...].astype(BF16), b_ref[...].astype(BF16)).astype(o_ref.dtype)
    else:
        def body(a_ref, b_ref, o_ref, acc_ref):
            k = pl.program_id(2)
            part = dot(a_ref[...].astype(BF16), b_ref[...].astype(BF16))

            @pl.when(k == 0)
            def _():
                acc_ref[...] = part

            @pl.when((k > 0) & (k < nk - 1))
            def _():
                acc_ref[...] += part

            @pl.when(k == nk - 1)
            def _():
                o_ref[...] = (acc_ref[...] + part).astype(o_ref.dtype)

    grid = (m // tm, n // tn, nk)
    at = lambda pos: functools.reduce(jnp.logical_and, [pl.program_id(ax) == pos(g) for ax, g in enumerate(grid)])
    r_in, r_out, r_shapes, r_scr, r_ops = _ride_args(ride)
    res = pl.pallas_call(
        _ride(body, ride, 2, 1, lambda: at(lambda g: 0), lambda: at(lambda g: g - 1)), grid=grid, name=name,
        in_specs=[a_spec, b_spec] + r_in,
        out_specs=[pl.BlockSpec((tm, tn), lambda i, j, k: (i, j))] + r_out,
        out_shape=[jax.ShapeDtypeStruct((m, n), out_dtype)] + r_shapes,
        scratch_shapes=([] if nk == 1 else [pltpu.VMEM((tm, tn), F32)]) + r_scr,
        compiler_params=_cparams(("arbitrary", "arbitrary", "arbitrary")),
    )(a, b, *r_ops)
    return res[0] if ride is None else (res[0], res[1:])


def _chunk_index(i, nch, nctx, rev):
    if not rev:
        return i
    return jnp.where(i < nctx, nctx - 1 - i, nch + nctx - 1 - i)


def _gdn_cols(d):
    return [d * G_HEADS + h for h in range(G_HEADS)], [2 * G_HEADS + d * G_HEADS + h for h in range(G_HEADS)]


def _gdn_operands(q_ref, k_ref, v_ref, g_ref, b_ref, d, rows=slice(None)):
    cols_g, cols_b = _gdn_cols(d)
    sls = [slice(h * DK, (h + 1) * DK) for h in range(G_HEADS)]
    gt, bt = g_ref[rows, :], b_ref[rows, :]
    gtt = gt.T
    qs = [q_ref[rows, sl] for sl in sls]
    ks = [k_ref[rows, sl] for sl in sls]
    vs = [v_ref[rows, sl] for sl in sls]
    gcs = [gt[:, cg:cg + 1] for cg in cols_g]
    grs = [gtt[cg:cg + 1, :] for cg in cols_g]
    bcs = [bt[:, cb:cb + 1] for cb in cols_b]
    return sls, qs, ks, vs, gcs, grs, bcs


GDN_FWD_CHUNKS = 4


def _gdn_scan_fwd(name, q, k, v, gcum, beta, d, nctx, ride=None, add=None):
    t = q.shape[0]
    nch = t // CH
    per = GDN_FWD_CHUNKS
    nst = nch // per
    rev = d == 1
    bix = lambda i: _chunk_index(i, nst, nctx // per, rev)
    full = pl.BlockSpec((per * CH, D), lambda i: (bix(i), 0))
    wide = pl.BlockSpec((per * CH, 128), lambda i: (bix(i), 0))
    order = list(range(per - 1, -1, -1)) if rev else list(range(per))

    def body(q_ref, k_ref, v_ref, g_ref, b_ref, *rest):
        add_ref = rest[0] if add is not None else None
        o_ref, ss_ref, ts_ref, s_scr = rest[-4:]

        @pl.when(pl.program_id(0) == 0)
        def _():
            s_scr[...] = jnp.zeros(s_scr.shape, F32)

        ops = [_gdn_operands(q_ref, k_ref, v_ref, g_ref, b_ref, d, slice(c * CH, (c + 1) * CH)) for c in order]
        sls = ops[0][0]
        cat = [sum((o[i] for o in ops), []) for i in range(1, 7)]
        wu, attn, lhs, tinv = gdn_local(*cat, rev)
        ss = [s_scr[h] for h in range(G_HEADS)]
        for n, c in enumerate(order):
            sl = slice(n * G_HEADS, (n + 1) * G_HEADS)
            s_new, o = gdn_state(ss, wu[sl], attn[sl], lhs[sl], cat[1][sl], cat[3][sl], rev)
            for h in range(G_HEADS):
                ss_ref[n, h] = ss[h]
                ts_ref[n, h] = tinv[n * G_HEADS + h]
                rows = slice(c * CH, (c + 1) * CH)
                o_ref[rows, sls[h]] = o[h] if add is None else o[h] + add_ref[rows, sls[h]]
            ss = s_new
        for h in range(G_HEADS):
            s_scr[h] = ss[h]

    r_in, r_out, r_shapes, r_scr, r_ops = _ride_args(ride)
    extra = [] if add is None else [add]
    res = pl.pallas_call(
        _ride(body, ride, 5 + len(extra), 3, lambda: pl.program_id(0) == 0, lambda: pl.program_id(0) == nst - 1), grid=(nst,), name=name,
        in_specs=[full, full, full, wide, wide] + [full] * len(extra) + r_in,
        out_specs=[full, pl.BlockSpec((per, G_HEADS, DK, DK), lambda i: (i, 0, 0, 0)),
                   pl.BlockSpec((per, G_HEADS, CH, CH), lambda i: (i, 0, 0, 0))] + r_out,
        out_shape=[jax.ShapeDtypeStruct((t, D), F32), jax.ShapeDtypeStruct((nch, G_HEADS, DK, DK), F32),
                   jax.ShapeDtypeStruct((nch, G_HEADS, CH, CH), F32)] + r_shapes,
        scratch_shapes=[pltpu.VMEM((G_HEADS, DK, DK), F32)] + r_scr,
        compiler_params=_cparams(("arbitrary",)),
    )(q, k, v, gcum, beta, *extra, *r_ops)
    return res if ride is None else (*res[:3], res[3:])


def _gdn_scan_bwd(name, q, k, v, gcum, beta, ssave, tsave, do, d, nctx, ride=None):
    t = q.shape[0]
    nch = t // CH
    rev = d == 1
    cix = lambda ib: _chunk_index(nch - 1 - ib, nch, nctx, rev)
    full = pl.BlockSpec((CH, D), lambda ib: (cix(ib), 0))
    wide = pl.BlockSpec((CH, 128), lambda ib: (cix(ib), 0))

    def body(q_ref, k_ref, v_ref, g_ref, b_ref, ss_ref, ts_ref, do_ref, dq_ref, dk_ref, dv_ref, dg_ref, db_ref, ds_scr):
        ib = pl.program_id(0)

        @pl.when(ib == 0)
        def _():
            ds_scr[...] = jnp.zeros(ds_scr.shape, F32)

        sls, qs, ks, vs, gcs, grs, bcs = _gdn_operands(q_ref, k_ref, v_ref, g_ref, b_ref, d)
        t_known = [ts_ref[0, h] for h in range(G_HEADS)]
        cols_g, cols_b = _gdn_cols(d)
        ss = [ss_ref[0, h] for h in range(G_HEADS)]
        do_v = [do_ref[:, sl] for sl in sls]
        ds_in = [ds_scr[h] for h in range(G_HEADS)]
        _, vjp = jax.vjp(lambda *a: gdn_chunk(*a, rev, t_known)[:2], ss, qs, ks, vs, gcs, grs, bcs)
        ds, dq, dk, dv, dgc, dgr, dbc = vjp((ds_in, do_v))
        lane = _iota((CH, 128), 1)
        sub = _iota((128, CH), 0)
        dg = jnp.zeros((CH, 128), F32)
        dgt = jnp.zeros((128, CH), F32)
        db = jnp.zeros((CH, 128), F32)
        for h in range(G_HEADS):
            ds_scr[h] = ds[h]
            dq_ref[:, sls[h]] = dq[h]
            dk_ref[:, sls[h]] = dk[h]
            dv_ref[:, sls[h]] = dv[h]
            dg = dg + jnp.where(lane == cols_g[h], dgc[h], 0.0)
            dgt = dgt + jnp.where(sub == cols_g[h], dgr[h], 0.0)
            db = db + jnp.where(lane == cols_b[h], dbc[h], 0.0)
        dg_ref[...] = dg + dgt.T
        db_ref[...] = db

    r_in, r_out, r_shapes, r_scr, r_ops = _ride_args(ride)
    res = pl.pallas_call(
        _ride(body, ride, 8, 5, lambda: pl.program_id(0) == 0, lambda: pl.program_id(0) == nch - 1), grid=(nch,), name=name,
        in_specs=[full, full, full, wide, wide,
                  pl.BlockSpec((1, G_HEADS, DK, DK), lambda ib: (nch - 1 - ib, 0, 0, 0)),
                  pl.BlockSpec((1, G_HEADS, CH, CH), lambda ib: (nch - 1 - ib, 0, 0, 0)), full] + r_in,
        out_specs=[full, full, full, wide, wide] + r_out,
        out_shape=[jax.ShapeDtypeStruct((t, D), F32)] * 3 + [jax.ShapeDtypeStruct((t, 128), F32)] * 2 + r_shapes,
        scratch_shapes=[pltpu.VMEM((G_HEADS, DK, DK), F32)] + r_scr,
        compiler_params=_cparams(("arbitrary",)),
    )(q, k, v, gcum, beta, ssave, tsave, do, *r_ops)
    return res if ride is None else (*res[:5], res[5:])


N_PAIRS = S_HEADS // 2


def _ssd_operands(x_ref, dt_ref, ac_ref, b_ref, c_ref, d, rows=slice(None)):
    sls = [slice(p * 128, (p + 1) * 128) for p in range(N_PAIRS)]
    gsl = [slice(g * S_N, (g + 1) * S_N) for g in range(S_GROUPS)]
    cols = [d * S_HEADS + h for h in range(S_HEADS)]
    dtc, acc = dt_ref[rows, :], ac_ref[rows, :]
    act = jnp.concatenate([acc, acc], axis=0).T
    col = lambda z, cc: z[:, cc:cc + 1]
    dts, acs, acr = ssd_pick(
        [col(dtc, cols[2 * p]) for p in range(N_PAIRS)], [col(dtc, cols[2 * p + 1]) for p in range(N_PAIRS)],
        [col(acc, cols[2 * p]) for p in range(N_PAIRS)], [col(acc, cols[2 * p + 1]) for p in range(N_PAIRS)],
        [act[cols[2 * p]:cols[2 * p] + 1, :] for p in range(N_PAIRS)],
        [act[cols[2 * p + 1]:cols[2 * p + 1] + 1, :] for p in range(N_PAIRS)])
    ops = ([x_ref[rows, sl] for sl in sls], dts, acs, acr, [b_ref[rows, gs] for gs in gsl], [c_ref[rows, gs] for gs in gsl])
    return sls, gsl, cols, ops


SSD_FWD_CHUNKS = 4


def _ssd_scan_fwd(name, xs, dtc, acc, bm, cm, d, nctx, add=None):
    t = xs.shape[0]
    nch = t // CH
    per = SSD_FWD_CHUNKS
    nst = nch // per
    rev = d == 1
    bix = lambda i: _chunk_index(i, nst, nctx // per, rev)
    inner = pl.BlockSpec((per * CH, S_INNER), lambda i: (bix(i), 0))
    wide = pl.BlockSpec((per * CH, 128), lambda i: (bix(i), 0))
    grp = pl.BlockSpec((per * CH, S_GROUPS * S_N), lambda i: (bix(i), 0))
    order = list(range(per - 1, -1, -1)) if rev else list(range(per))

    def body(x_ref, dt_ref, ac_ref, b_ref, c_ref, *rest):
        add_ref = rest[0] if add is not None else None
        y_ref, hs_ref, h_scr = rest[-3:]

        @pl.when(pl.program_id(0) == 0)
        def _():
            h_scr[...] = jnp.zeros(h_scr.shape, F32)

        loc, cgs, sls = [], [], None
        for c in order:
            sls, _, _, ops = _ssd_operands(x_ref, dt_ref, ac_ref, b_ref, c_ref, d, slice(c * CH, (c + 1) * CH))
            loc.append(ssd_local(*ops, rev))
            cgs.append(ops[5])
        hts = [h_scr[p] for p in range(N_PAIRS)]
        for n, c in enumerate(order):
            h_new, y = ssd_state(hts, *loc[n], cgs[n])
            for p in range(N_PAIRS):
                hs_ref[n, p] = hts[p]
                rows = slice(c * CH, (c + 1) * CH)
                y_ref[rows, sls[p]] = y[p] if add is None else y[p] + add_ref[rows, sls[p]]
            hts = h_new
        for p in range(N_PAIRS):
            h_scr[p] = hts[p]

    return pl.pallas_call(
        body, grid=(nst,), name=name,
        in_specs=[inner, wide, wide, grp, grp] + ([] if add is None else [inner]),
        out_specs=[inner, pl.BlockSpec((per, N_PAIRS, S_N, 128), lambda i: (i, 0, 0, 0))],
        out_shape=[jax.ShapeDtypeStruct((t, S_INNER), F32), jax.ShapeDtypeStruct((nch, N_PAIRS, S_N, 128), F32)],
        scratch_shapes=[pltpu.VMEM((N_PAIRS, S_N, 128), F32)],
        compiler_params=_cparams(("arbitrary",)),
    )(xs, dtc, acc, bm, cm, *([] if add is None else [add]))


def _ssd_scan_bwd(name, xs, dtc, acc, bm, cm, hsave, dy, d, nctx):
    t = xs.shape[0]
    nch = t // CH
    rev = d == 1
    cix = lambda ib: _chunk_index(nch - 1 - ib, nch, nctx, rev)
    inner = pl.BlockSpec((CH, S_INNER), lambda ib: (cix(ib), 0))
    wide = pl.BlockSpec((CH, 128), lambda ib: (cix(ib), 0))
    grp = pl.BlockSpec((CH, S_GROUPS * S_N), lambda ib: (cix(ib), 0))

    def body(x_ref, dt_ref, ac_ref, b_ref, c_ref, hs_ref, dy_ref, dx_ref, ddt_ref, dac_ref, db_ref, dc_ref, dh_scr):
        ib = pl.program_id(0)

        @pl.when(ib == 0)
        def _():
            dh_scr[...] = jnp.zeros(dh_scr.shape, F32)

        sls, gsl, cols, ops = _ssd_operands(x_ref, dt_ref, ac_ref, b_ref, c_ref, d)
        hts = [hs_ref[0, p] for p in range(N_PAIRS)]
        dy_v = [dy_ref[:, sl] for sl in sls]
        dh_in = [dh_scr[p] for p in range(N_PAIRS)]
        _, vjp = jax.vjp(lambda *a: ssd_chunk(*a, rev), hts, *ops)
        dh, dx, ddts, dacs, dacr, db, dc = vjp((dh_in, dy_v))
        for p in range(N_PAIRS):
            dh_scr[p] = dh[p]
            dx_ref[:, sls[p]] = dx[p]
        r = _iota((S_INNER, 128), 0)
        e_t = jnp.where(_iota((S_INNER, 128), 1) == d * S_HEADS + r // S_P, 1.0, 0.0).astype(F32)
        ddt_ref[...] = _dot(jnp.concatenate(ddts, axis=1), e_t, HIGH)
        dac_cols = _dot(jnp.concatenate(dacs, axis=1), e_t, HIGH)
        sub = _iota((128, 128), 0)
        lane = _iota((128, 128), 1)
        m = jnp.zeros((128, 128), F32)
        for p in range(N_PAIRS):
            m = m + jnp.where(sub == p, jnp.sum(dacr[p], axis=0, keepdims=True), 0.0)
        mt = m.T
        s0 = jnp.where(lane == d * S_HEADS + 2 * sub, 1.0, 0.0).astype(F32)
        s1 = jnp.where(lane == d * S_HEADS + 2 * sub + 1, 1.0, 0.0).astype(F32)
        dac_ref[...] = dac_cols + _dot(mt[:CH], s0, HIGH) + _dot(mt[CH:], s1, HIGH)
        for g in range(S_GROUPS):
            db_ref[:, gsl[g]] = db[g]
            dc_ref[:, gsl[g]] = dc[g]

    return pl.pallas_call(
        body, grid=(nch,), name=name,
        in_specs=[inner, wide, wide, grp, grp,
                  pl.BlockSpec((1, N_PAIRS, S_N, 128), lambda ib: (nch - 1 - ib, 0, 0, 0)), inner],
        out_specs=[inner, wide, wide, grp, grp],
        out_shape=[jax.ShapeDtypeStruct((t, S_INNER), F32)] + [jax.ShapeDtypeStruct((t, 128), F32)] * 2
        + [jax.ShapeDtypeStruct((t, S_GROUPS * S_N), F32)] * 2,
        scratch_shapes=[pltpu.VMEM((N_PAIRS, S_N, 128), F32)],
        compiler_params=_cparams(("arbitrary",)),
    )(xs, dtc, acc, bm, cm, hsave, dy)


def _mesh_pos():
    return lax.axis_index("x"), lax.axis_index("y"), lax.axis_index("c")


def _hbm_specs(n):
    return [pl.BlockSpec(memory_space=pl.ANY)] * n


def _sem_shapes(nw):
    return [pltpu.SemaphoreType.DMA((nw, 7)), pltpu.SemaphoreType.DMA((nw, 7)), pltpu.SemaphoreType.DMA((nw,))]


class _AllGather:
    def __init__(self, shards):
        self.arrays = list(shards)
        self.out_shapes = [jax.ShapeDtypeStruct((N_DEV,) + xs.shape, xs.dtype) for xs in shards]

    def _parts(self, x_refs, out_refs, sems):
        send_sems, recv_sems, local_sems = sems
        nw = len(x_refs)
        x, y, c = _mesh_pos()
        me, sibling = (x, y, c), (x, y, 1 - c)
        chips = [(1 - x, y), (x, 1 - y), (1 - x, 1 - y)]

        def slot(w, px, py, pc):
            return out_refs[w].at[4 * px + 2 * py + pc]

        def copy(w, k, block, to, src=None):
            return pltpu.make_async_remote_copy(
                src_ref=slot(w, *block) if src is None else src, dst_ref=slot(w, *block),
                send_sem=send_sems.at[w, k], recv_sem=recv_sems.at[w, k], device_id=to, device_id_type=MESH)

        mine = [pltpu.make_async_copy(x_refs[w], slot(w, *me), local_sems.at[w]) for w in range(nw)]
        first = []
        for w in range(nw):
            first.append(copy(w, 0, me, sibling, src=x_refs[w]))
            first += [copy(w, 1 + j, me, (*chip, c), src=x_refs[w]) for j, chip in enumerate(chips)]
        return nw, me, sibling, chips, c, copy, mine, first

    def start(self, x_refs, out_refs, sems):
        _, _, _, _, _, _, mine, first = self._parts(x_refs, out_refs, sems)
        for cp in mine + first:
            cp.start()

    def finish(self, x_refs, out_refs, sems):
        nw, me, sibling, chips, c, copy, mine, first = self._parts(x_refs, out_refs, sems)
        passed = []
        for j, chip in enumerate(chips):
            for w in range(nw):
                copy(w, 1 + j, (*chip, c), me).wait_recv()
                fwd = copy(w, 4 + j, (*chip, c), sibling)
                fwd.start()
                passed.append(fwd)
        for w in range(nw):
            copy(w, 0, sibling, me).wait_recv()
            for j, chip in enumerate(chips):
                copy(w, 4 + j, (*chip, 1 - c), me).wait_recv()
        for cp in first + passed:
            cp.wait_send()
        for cp in mine:
            cp.wait()


class _AllToAll:
    def __init__(self, blocks):
        self.arrays = list(blocks)
        self.out_shapes = [jax.ShapeDtypeStruct(g.shape, g.dtype) for g in blocks]

    def _parts(self, g_refs, out_refs, sems):
        send_sems, recv_sems, local_sems = sems
        nw = len(g_refs)
        x, y, c = _mesh_pos()
        me_i = 4 * x + 2 * y + c
        mine = [pltpu.make_async_copy(g_refs[w].at[me_i], out_refs[w].at[me_i], local_sems.at[w]) for w in range(nw)]
        cps = []
        for k in range(1, N_DEV):
            px = 1 - x if (k >> 2) & 1 else x
            py = 1 - y if (k >> 1) & 1 else y
            pc = 1 - c if k & 1 else c
            for w in range(nw):
                cps.append(pltpu.make_async_remote_copy(
                    src_ref=g_refs[w].at[4 * px + 2 * py + pc], dst_ref=out_refs[w].at[me_i],
                    send_sem=send_sems.at[w, k - 1], recv_sem=recv_sems.at[w, k - 1],
                    device_id=(px, py, pc), device_id_type=MESH))
        return mine, cps

    def start(self, g_refs, out_refs, sems):
        mine, cps = self._parts(g_refs, out_refs, sems)
        for cp in mine + cps:
            cp.start()

    def finish(self, g_refs, out_refs, sems):
        mine, cps = self._parts(g_refs, out_refs, sems)
        for cp in cps + mine:
            cp.wait()


def _exchange(name, ex):
    nw = len(ex.arrays)

    def body(*refs):
        ins, outs, sems = refs[:nw], refs[nw:2 * nw], refs[2 * nw:]
        ex.start(ins, outs, sems)
        ex.finish(ins, outs, sems)

    return pl.pallas_call(body, name=name, out_shape=ex.out_shapes, in_specs=_hbm_specs(nw), out_specs=_hbm_specs(nw),
                          scratch_shapes=_sem_shapes(nw))(*ex.arrays)


def _ride(body, ex, n_in, n_out, is_first, is_last):
    if ex is None:
        return body
    nw = len(ex.arrays)

    def riding(*refs):
        ins, ex_in = refs[:n_in], refs[n_in:n_in + nw]
        outs = refs[n_in + nw:n_in + nw + n_out]
        ex_out = refs[n_in + nw + n_out:n_in + 2 * nw + n_out]
        rest = refs[n_in + 2 * nw + n_out:]
        scratch, sems = rest[:len(rest) - 3], rest[len(rest) - 3:]

        @pl.when(is_first())
        def _():
            ex.start(ex_in, ex_out, sems)

        body(*ins, *outs, *scratch)

        @pl.when(is_last())
        def _():
            ex.finish(ex_in, ex_out, sems)

    return riding


def _ride_args(ex):
    if ex is None:
        return [], [], [], [], []
    nw = len(ex.arrays)
    return _hbm_specs(nw), _hbm_specs(nw), list(ex.out_shapes), _sem_shapes(nw), list(ex.arrays)


def _reduce_adam(name, recv, w, m, v, tm):
    rows, width = w.shape
    nslot = recv.shape[0]

    def body(recv_ref, w_ref, m_ref, v_ref, g_ref, d_ref, m2_ref, v2_ref):
        g = recv_ref[0].astype(F32)
        for s in range(1, nslot):
            g = g + recv_ref[s].astype(F32)
        delta, m2, v2 = f_adamw(w_ref[...], g, m_ref[...], v_ref[...])
        g_ref[...] = g
        d_ref[...] = delta
        m2_ref[...] = m2
        v2_ref[...] = v2

    row = pl.BlockSpec((tm, width), lambda i: (i, 0))
    return pl.pallas_call(
        body, grid=(rows // tm,), name=name,
        in_specs=[pl.BlockSpec((nslot, tm, width), lambda i: (0, i, 0)), row, row, row],
        out_specs=[row] * 4,
        out_shape=[jax.ShapeDtypeStruct((rows, width), F32)] * 4,
        compiler_params=_cparams(("arbitrary",)),
    )(recv, w, m, v)


BIG = ("w_in", "ada_w", "w_br_gdn", "w_br_ssm", "w_out", "w_ffn_in", "w_ffn_out")
BIG_FIRST = ("ada_w", "w_in")
BIG_LATE = ("w_br_gdn", "w_br_ssm", "w_out", "w_ffn_in", "w_ffn_out")
BIG_COL_SHARDED = ("w_in", "ada_w", "w_ffn_in")
BIG_ADAM_ROWS = dict(w_in=128, ada_w=256, w_br_gdn=128, w_br_ssm=256, w_out=128, w_ffn_in=256, w_ffn_out=352)
CONV = ("gdn_conv_w", "ssm_conv_w")
SMALL = ("c_ctx", "ada_b", "norm1_w", "gdn_conv_b", "gdn_a_log", "gdn_dt_bias", "gdn_norm_w", "ssm_conv_b",
         "ssm_a_log", "ssm_dt_bias", "ssm_d", "ssm_norm_w", "norm2_w", "norm_f_w")
CONV_SHARD = XBC // N_DEV


def _to_rows(a):
    flat = a.reshape(-1)
    pad = (-flat.shape[0]) % PACK_W
    if pad:
        flat = jnp.pad(flat, (0, pad))
    return flat.reshape(-1, PACK_W)


def _pack(arrays, rows=None):
    buf = jnp.concatenate([_to_rows(a) for a in arrays], axis=0)
    if rows is not None and rows > buf.shape[0]:
        buf = jnp.pad(buf, ((0, rows - buf.shape[0]), (0, 0)))
    return buf


def _unpack(buf, shapes):
    out, r0 = [], 0
    for shp in shapes:
        n = 1
        for s in shp:
            n *= s
        nr = -(-n // PACK_W)
        out.append(buf[r0:r0 + nr].reshape(-1)[:n].reshape(shp))
        r0 += nr
    return out


def _full_from_blocks(blocks, col_sharded):
    _, r, c = blocks.shape
    if col_sharded:
        return jnp.transpose(blocks, (1, 0, 2)).reshape(r, N_DEV * c)
    return blocks.reshape(N_DEV * r, c)


def _blocks_from_full(full, col_sharded):
    if col_sharded:
        r, c = full.shape[0], full.shape[1] // N_DEV
        return jnp.transpose(full.reshape(r, N_DEV, c), (1, 0, 2))
    return full.reshape(N_DEV, full.shape[0] // N_DEV, full.shape[1])


def _pad_cols(a, n):
    return jnp.pad(a, ((0, 0), (0, n - a.shape[1])))


def _w_cat(w_in):
    return jnp.concatenate([
        w_in[:, O_QKV:O_ZG], w_in[:, O_XBC:O_DT], w_in[:, O_ZS:O_XBC], w_in[:, O_GATE:O_END], w_in[:, O_ZG:O_AB],
        _pad_cols(w_in[:, O_AB:O_ZS], 128), _pad_cols(w_in[:, O_DT:O_GATE], 128)], axis=1)


def _w_uncat(wc):
    return jnp.concatenate([
        wc[:, C_QKV:C_XBC], wc[:, C_ZG:C_AB], wc[:, C_AB:C_AB + (O_ZS - O_AB)], wc[:, C_ZS:C_GATE], wc[:, C_XBC:C_ZS],
        wc[:, C_DT:C_DT + (O_GATE - O_DT)], wc[:, C_GATE:C_ZG]], axis=1)


def _pad_row(vec, n=128):
    vec = vec.reshape(1, -1)
    return _pad_cols(vec, n)


def kernel(x, c, ctx, c_ctx, ada_w, ada_b, norm1_w, w_in, gdn_conv_w, gdn_conv_b, gdn_a_log, gdn_dt_bias, gdn_norm_w, ssm_conv_w, ssm_conv_b, ssm_a_log, ssm_dt_bias, ssm_d, ssm_norm_w, w_br_gdn, w_br_ssm, w_out, norm2_w, w_ffn_in, w_ffn_out, norm_f_w, loss_target, m_c_ctx, m_ada_w, m_ada_b, m_norm1_w, m_w_in, m_gdn_conv_w, m_gdn_conv_b, m_gdn_a_log, m_gdn_dt_bias, m_gdn_norm_w, m_ssm_conv_w, m_ssm_conv_b, m_ssm_a_log, m_ssm_dt_bias, m_ssm_d, m_ssm_norm_w, m_w_br_gdn, m_w_br_ssm, m_w_out, m_norm2_w, m_w_ffn_in, m_w_ffn_out, m_norm_f_w, v_c_ctx, v_ada_w, v_ada_b, v_norm1_w, v_w_in, v_gdn_conv_w, v_gdn_conv_b, v_gdn_a_log, v_gdn_dt_bias, v_gdn_norm_w, v_ssm_conv_w, v_ssm_conv_b, v_ssm_a_log, v_ssm_dt_bias, v_ssm_d, v_ssm_norm_w, v_w_br_gdn, v_w_br_ssm, v_w_out, v_norm2_w, v_w_ffn_in, v_w_ffn_out, v_norm_f_w):
    wts = dict(c_ctx=c_ctx, ada_w=ada_w, ada_b=ada_b, norm1_w=norm1_w, w_in=w_in, gdn_conv_w=gdn_conv_w, gdn_conv_b=gdn_conv_b, gdn_a_log=gdn_a_log, gdn_dt_bias=gdn_dt_bias, gdn_norm_w=gdn_norm_w, ssm_conv_w=ssm_conv_w, ssm_conv_b=ssm_conv_b, ssm_a_log=ssm_a_log, ssm_dt_bias=ssm_dt_bias, ssm_d=ssm_d, ssm_norm_w=ssm_norm_w, w_br_gdn=w_br_gdn, w_br_ssm=w_br_ssm, w_out=w_out, norm2_w=norm2_w, w_ffn_in=w_ffn_in, w_ffn_out=w_ffn_out, norm_f_w=norm_f_w)
    mom1 = dict(c_ctx=m_c_ctx, ada_w=m_ada_w, ada_b=m_ada_b, norm1_w=m_norm1_w, w_in=m_w_in, gdn_conv_w=m_gdn_conv_w, gdn_conv_b=m_gdn_conv_b, gdn_a_log=m_gdn_a_log, gdn_dt_bias=m_gdn_dt_bias, gdn_norm_w=m_gdn_norm_w, ssm_conv_w=m_ssm_conv_w, ssm_conv_b=m_ssm_conv_b, ssm_a_log=m_ssm_a_log, ssm_dt_bias=m_ssm_dt_bias, ssm_d=m_ssm_d, ssm_norm_w=m_ssm_norm_w, w_br_gdn=m_w_br_gdn, w_br_ssm=m_w_br_ssm, w_out=m_w_out, norm2_w=m_norm2_w, w_ffn_in=m_w_ffn_in, w_ffn_out=m_w_ffn_out, norm_f_w=m_norm_f_w)
    mom2 = dict(c_ctx=v_c_ctx, ada_w=v_ada_w, ada_b=v_ada_b, norm1_w=v_norm1_w, w_in=v_w_in, gdn_conv_w=v_gdn_conv_w, gdn_conv_b=v_gdn_conv_b, gdn_a_log=v_gdn_a_log, gdn_dt_bias=v_gdn_dt_bias, gdn_norm_w=v_gdn_norm_w, ssm_conv_w=v_ssm_conv_w, ssm_conv_b=v_ssm_conv_b, ssm_a_log=v_ssm_a_log, ssm_dt_bias=v_ssm_dt_bias, ssm_d=v_ssm_d, ssm_norm_w=v_ssm_norm_w, w_br_gdn=v_w_br_gdn, w_br_ssm=v_w_br_ssm, w_out=v_w_out, norm2_w=v_norm2_w, w_ffn_in=v_w_ffn_in, w_ffn_out=v_w_ffn_out, norm_f_w=v_norm_f_w)
    order = list(wts)

    seq = x.shape[1]
    t = TM + seq
    ntl, nlt, nctx = t // TM, seq // TM, TM // CH

    me_i = 4 * lax.axis_index("x") + 2 * lax.axis_index("y") + lax.axis_index("c")
    gathered = _exchange("ag_weights", _AllGather([wts[n][0].astype(BF16) for n in BIG_FIRST]))
    full = {n: _full_from_blocks(blk, n in BIG_COL_SHARDED) for n, blk in zip(BIG_FIRST, gathered)}
    late_gather = _AllGather([wts[n][0].astype(BF16) for n in BIG_LATE])
    conv_sh = _pack([wts[n] for n in CONV], rows=8)
    conv_g = _exchange("ag_conv", _AllGather([conv_sh]))[0].reshape(N_DEV, -1)
    ncv = 3 * CONV_SHARD
    for i, n in enumerate(CONV):
        off = -(-ncv // PACK_W) * PACK_W * i
        full[n] = jnp.transpose(conv_g[:, off:off + ncv].reshape(N_DEV, 3, CONV_SHARD), (1, 0, 2)).reshape(3, XBC)
    w_cat = _w_cat(full["w_in"])
    gcw = full["gdn_conv_w"].reshape(3, 1, XBC)
    scw = full["ssm_conv_w"].reshape(3, 1, XBC)

    n1w, n2w, nfw = norm1_w.reshape(1, D), norm2_w.reshape(1, D), norm_f_w.reshape(1, D)
    gcb, scb = gdn_conv_b.reshape(1, XBC), ssm_conv_b.reshape(1, XBC)
    alog16, dtb16 = _pad_row(gdn_a_log), _pad_row(gdn_dt_bias)
    alog64, dtb64 = _pad_row(ssm_a_log), _pad_row(ssm_dt_bias)
    gnw = gdn_norm_w.reshape(1, DK)
    ssd8 = jnp.tile(_pad_row(ssm_d), (8, 1))
    snw = ssm_norm_w.reshape(1, S_INNER)
    x2 = x[0]
    tgt = loss_target[0]
    cvec = jnp.concatenate([c, c_ctx.reshape(1, D), jnp.zeros((14, D), F32)], axis=0)

    a16 = _rowwise("silu_c", f_silu_rows, [(cvec, D, 0, 0)], [], [(D, BF16)], 1, tm=16)[0]
    mod = _mm("mm_mod", a16, full["ada_w"], "nn", F32) + ada_b
    sh1, sc1, g1, sh2, sc2, g2 = [mod[0:1, i * D:(i + 1) * D] for i in range(6)]
    csh1, csc1 = mod[1:2, 0:D], mod[1:2, D:2 * D]

    pre_pars = [n1w, sc1, sh1, csc1, csh1]
    pre_rows = [(x2, D, 0, -1, True), (ctx[0], D, 0, 0, "first")]
    a = _rowwise("pre", f_pre, pre_rows, pre_pars, [(D, BF16)], ntl, base=0)[0]
    proj = _mm("mm_proj", a, w_cat, "nn", F32, tm=1408, tn=1280)
    gp_rows = [(proj, XBC, C_QKV // XBC, 0), (proj, 128, C_AB // 128, 0)]
    gp_pars = [gcw, gcb, alog16, dtb16]
    q, k, v, gcum, beta = _rowwise("gdnprep", f_gdnprep, gp_rows, gp_pars, [(D, F32)] * 3 + [(128, F32)] * 2, ntl, base=0)
    sp_rows = [(proj, XBC, C_XBC // XBC, 0), (proj, 128, C_DT // 128, 0)]
    sp_pars = [scw, scb, alog64, dtb64]
    xs, bm, cm, dtc, acc = _rowwise(
        "ssmprep", f_ssmprep, sp_rows, sp_pars, [(S_INNER, F32), (512, F32), (512, F32), (128, F32), (128, F32)], ntl, base=0)
    o0, ss0, ts0, gathered = _gdn_scan_fwd("gdn_fwd0", q, k, v, gcum, beta, 0, nctx, ride=late_gather)
    full.update({n: _full_from_blocks(blk, n in BIG_COL_SHARDED) for n, blk in zip(BIG_LATE, gathered)})
    o_sum, ss1, ts1 = _gdn_scan_fwd("gdn_fwd1", q, k, v, gcum, beta, 1, nctx, add=o0)
    y0, hs0 = _ssd_scan_fwd("ssd_fwd0", xs, dtc, acc, bm, cm, 0, nctx)
    y_sum, hs1 = _ssd_scan_fwd("ssd_fwd1", xs, dtc, acc, bm, cm, 1, nctx, add=y0)
    post_rows = [(o_sum, D, 0, 1), (proj, D, C_ZG // D, 1), (y_sum, S_INNER, 0, 1), (xs, S_INNER, 0, 1),
                 (proj, S_INNER, C_ZS // S_INNER, 1)]
    post_pars = [gnw, ssd8, snw]
    og, ys = _rowwise("post", f_post, post_rows, post_pars, [(D, BF16), (S_INNER, BF16)], nlt)
    pg = _mm("mm_pg", og, full["w_br_gdn"], "nn", F32)
    ps = _mm("mm_ps", ys, full["w_br_ssm"], "nn", F32, tk=2048)
    merge_rows = [(proj, S_INNER, C_GATE // S_INNER, 1), (pg, D, 0, 0), (ps, D, 0, 0)]
    merged = _rowwise("merge", f_merge, merge_rows, [], [(D, BF16)], nlt)[0]
    mix = _mm("mm_mix", merged, full["w_out"], "nn", F32)
    res_rows = [(x2, D, 0, 0), (mix, D, 0, 0)]
    res_pars = [g1, n2w, sc2, sh2]
    h1, f = _rowwise("res1", f_res1, res_rows, res_pars, [(D, F32), (D, BF16)], nlt)
    u = _mm("mm_u", f, full["w_ffn_in"], "nn", F32, tn=1408)
    hact = _rowwise("act", f_act, [(u, 2 * D_FF, 0, 0)], [], [(D_FF, BF16)], nlt)[0]
    ff = _mm("mm_ff", hact, full["w_ffn_out"], "nn", F32, tk=2816)

    fin_rows = [(h1, D, 0, 0), (ff, D, 0, 0), (tgt, D, 0, 0)]
    d_h1a, d_ff, d_g2, d_nfw, loss_acc = _rowwise_bwd(
        "final", f_final, fin_rows, [g2, nfw], ["one"], [(0, F32), (1, BF16)], [0, 1], nlt, loss_out=True)
    d_hact = _mm("mm_dhact", d_ff, full["w_ffn_out"], "nt", BF16, tn=1408)
    g_w_ffn_out = _mm("mm_gwffo", hact, d_ff, "tn", BF16, tm=1408, tk=2048)
    d_u = _rowwise_bwd("act_bwd", f_act, [(u, 2 * D_FF, 0, 0)], [], [[(d_hact, D_FF, 0, 0)]], [(0, BF16)], [], nlt)[0]
    d_f = _mm("mm_df", d_u, full["w_ffn_in"], "nt", BF16, tk=2816)
    g_w_ffn_in = _mm("mm_gwffi", f, d_u, "tn", BF16, tn=1408, tk=2048)
    d_xres, d_mix, d_g1, d_n2w, d_sc2, d_sh2 = _rowwise_bwd(
        "res1_bwd", f_res1, res_rows, res_pars, [[(d_h1a, D, 0, 0)], [(d_f, D, 0, 0)]], [(0, F32), (1, BF16)], [0, 1, 2, 3], nlt)
    d_merged = _mm("mm_dmerged", d_mix, full["w_out"], "nt", BF16)
    g_w_out = _mm("mm_gwout", merged, d_mix, "tn", BF16, tk=2048)
    lat = lambda arr, wd: (arr, wd, 0, -1, True)
    merge_rows_all = [(proj, S_INNER, C_GATE // S_INNER, 0), lat(pg, D), lat(ps, D)]
    d_gate, d_pg, d_ps = _rowwise_bwd(
        "merge_bwd", f_merge, merge_rows_all, [], [[lat(d_merged, D)]], [(0, BF16), (1, BF16), (2, BF16)], [], ntl, base=0)
    d_og = _mm("mm_dog", d_pg, full["w_br_gdn"], "nt", BF16)
    g_w_br_gdn = _mm("mm_gwbrg", og, d_pg, "tn", BF16, tk=2048)
    d_ys = _mm("mm_dys", d_ps, full["w_br_ssm"], "nt", BF16, tn=2048)
    g_w_br_ssm = _mm("mm_gwbrs", ys, d_ps, "tn", BF16, tk=2048)
    post_rows_all = [(arr, wd, cb, 0) for (arr, wd, cb, _) in post_rows]
    d_o, d_zg, d_y, d_xs_post, d_zs, d_gnw, d_ssd8, d_snw = _rowwise_bwd(
        "post_bwd", f_post, post_rows_all, post_pars, [[lat(d_og, D)], [lat(d_ys, S_INNER)]],
        [(0, F32), (1, BF16), (2, F32), (3, F32), (4, BF16)], [0, 1, 2], ntl, base=0)
    late_grads = dict(w_br_gdn=g_w_br_gdn, w_br_ssm=g_w_br_ssm, w_out=g_w_out, w_ffn_in=g_w_ffn_in, w_ffn_out=g_w_ffn_out)
    late_a2a = _AllToAll([_blocks_from_full(late_grads[n], n in BIG_COL_SHARDED) for n in BIG_LATE])
    dq0, dk0, dv0, dg0, db0, recv_late = _gdn_scan_bwd("gdn_bwd0", q, k, v, gcum, beta, ss0, ts0, d_o, 0, nctx, ride=late_a2a)
    dq1, dk1, dv1, dg1, db1 = _gdn_scan_bwd("gdn_bwd1", q, k, v, gcum, beta, ss1, ts1, d_o, 1, nctx)
    dxs0, ddt0, dac0, dbm0, dcm0 = _ssd_scan_bwd("ssd_bwd0", xs, dtc, acc, bm, cm, hs0, d_y, 0, nctx)
    dxs1, ddt1, dac1, dbm1, dcm1 = _ssd_scan_bwd("ssd_bwd1", xs, dtc, acc, bm, cm, hs1, d_y, 1, nctx)
    row = lambda arr, wd: (arr, wd, 0, 0)
    d_qkv_raw, d_ab, d_gcw, d_gcb, d_alog16, d_dtb16 = _rowwise_bwd(
        "gdnprep_bwd", f_gdnprep, gp_rows, gp_pars,
        [[row(dq0, D), row(dq1, D)], [row(dk0, D), row(dk1, D)], [row(dv0, D), row(dv1, D)],
         [row(dg0, 128), row(dg1, 128)], [row(db0, 128), row(db1, 128)]],
        [(0, BF16), (1, BF16)], [0, 1, 2, 3], ntl, base=0)
    d_xbc_raw, d_dt, d_scw, d_scb, d_alog64, d_dtb64 = _rowwise_bwd(
        "ssmprep_bwd", f_ssmprep, sp_rows, sp_pars,
        [[row(dxs0, S_INNER), row(dxs1, S_INNER), row(d_xs_post, S_INNER)], [row(dbm0, 512), row(dbm1, 512)],
         [row(dcm0, 512), row(dcm1, 512)], [row(ddt0, 128), row(ddt1, 128)], [row(dac0, 128), row(dac1, 128)]],
        [(0, BF16), (1, BF16)], [0, 1, 2, 3], ntl, base=0)
    d_proj = jnp.concatenate([d_qkv_raw, d_xbc_raw, d_zs, d_gate, d_zg, d_ab, d_dt], axis=1)
    g_w_cat = _mm("mm_gwcat", a, d_proj, "tn", BF16, tn=768, tk=2816)
    w_in_a2a = _AllToAll([_blocks_from_full(_w_uncat(g_w_cat), True)])
    d_a, recv_w_in = _mm("mm_da", d_proj, w_cat, "nt", BF16, tk=3840, ride=w_in_a2a)
    d_x, d_n1w, d_sc1, d_sh1, d_csc1, d_csh1 = _rowwise_bwd(
        "pre_bwd", f_pre_thru, pre_rows, pre_pars, [[row(d_a, D)], [(d_xres, D, 0, -1, True)]],
        [(0, F32)], [0, 1, 2, 3, 4], ntl, base=0)
    zero4 = jnp.zeros((1, 4 * D), F32)
    d_mod = jnp.concatenate([
        jnp.concatenate([d_sh1, d_sc1, d_g1, d_sh2, d_sc2, d_g2], axis=1),
        jnp.concatenate([d_csh1, d_csc1, zero4], axis=1), jnp.zeros((14, 6 * D), F32)], axis=0)
    d_a16 = _mm("mm_da16", d_mod, full["ada_w"], "nt", F32)
    d_cvec = _rowwise_bwd("silu_c_bwd", f_silu_rows, [(cvec, D, 0, 0)], [], [[row(d_a16, D)]], [(0, F32)], [], 1, tm=16)[0]

    recv = dict(zip(BIG_LATE, recv_late), w_in=recv_w_in[0])
    big_un = {n: _reduce_adam("adam_" + n, recv[n], wts[n][0], mom1[n][0], mom2[n][0], BIG_ADAM_ROWS[n])
              for n in BIG if n != "ada_w"}

    small_g = dict(c_ctx=d_cvec[1], ada_b=d_mod[0] + d_mod[1], norm1_w=d_n1w, gdn_conv_b=d_gcb,
                   gdn_a_log=d_alog16[0, :2 * G_HEADS], gdn_dt_bias=d_dtb16[0, :2 * G_HEADS], gdn_norm_w=d_gnw,
                   ssm_conv_b=d_scb, ssm_a_log=d_alog64[0, :2 * S_HEADS], ssm_dt_bias=d_dtb64[0, :2 * S_HEADS],
                   ssm_d=d_ssd8[0, :S_HEADS], ssm_norm_w=d_snw, norm2_w=d_n2w, norm_f_w=d_nfw,
                   gdn_conv_w=d_gcw.reshape(3, XBC), ssm_conv_w=d_scw.reshape(3, XBC))
    small_names = SMALL + CONV
    factors = [a16[0].astype(F32), d_mod[0], d_mod[1]]
    round8 = lambda r: -(-r // 8) * 8
    packed_rows = lambda arrs: sum(-(-arr.size // PACK_W) for arr in arrs)
    n_small = packed_rows([small_g[n] for n in small_names])
    n_fac = packed_rows(factors)
    rows_small = round8(n_small)
    sg_pack = _pack([small_g[n] for n in small_names] + factors, rows=round8(n_small + n_fac))
    recv_all = _exchange("ag_small_grads", _AllGather([sg_pack]))[0]
    recv_s = recv_all[:, :rows_small]
    fac = recv_all[:, n_small:n_small + n_fac].reshape(N_DEV, -1)
    my_cols = lambda z: lax.dynamic_slice(z, (0, me_i * (6 * D // N_DEV)), (N_DEV, 6 * D // N_DEV))
    lhs = jnp.concatenate([fac[:, :D], jnp.broadcast_to(a16[1:2].astype(F32), (N_DEV, D))], axis=0)
    rhs = jnp.concatenate([my_cols(fac[:, D:7 * D]), my_cols(fac[:, 7 * D:])], axis=0)
    g_ada_w = _mm("mm_gwada", lhs, rhs, "tn", F32)
    big_un["ada_w"] = _reduce_adam("adam_ada_w", g_ada_w[None], wts["ada_w"][0], mom1["ada_w"][0], mom2["ada_w"][0],
                                   BIG_ADAM_ROWS["ada_w"])

    def placed(src, n):
        if n not in CONV:
            return src[n]
        return lax.dynamic_update_slice(jnp.zeros((3, XBC), F32), src[n][0], (0, me_i * CONV_SHARD))

    small_out = _reduce_adam("adam_small", recv_s, *[_pack([placed(src, n) for n in small_names], rows=rows_small)
                                                     for src in (wts, mom1, mom2)], rows_small)
    small_shapes = [wts[n].shape if n in SMALL else (3, XBC) for n in small_names]
    small_un = [_unpack(buf, small_shapes) for buf in small_out]

    res = [{}, {}, {}, {}]
    for kind in range(4):
        for n in BIG:
            res[kind][n] = big_un[n][kind].reshape(wts[n].shape)
        for n, val in zip(small_names, small_un[kind]):
            if n in CONV:
                val = lax.dynamic_slice(val, (0, me_i * CONV_SHARD), (3, CONV_SHARD)).reshape(wts[n].shape)
            res[kind][n] = val
    loss = lax.psum(loss_acc[0, 0], ("x", "y", "c"))
    grad_x = d_x.reshape(x.shape)
    return (loss, grad_x, *[res[0][n] for n in order], *[res[1][n] for n in order], *[res[2][n] for n in order],
            *[res[3][n] for n in order])
```

```python
import functools

import jax
import jax.numpy as jnp
from jax import lax
from jax.experimental import pallas as pl
from jax.experimental.pallas import tpu as pltpu

F32 = jnp.float32
BF16 = jnp.bfloat16
HI = lax.Precision.HIGHEST
HIGH = lax.Precision.HIGH
MESH = pl.DeviceIdType.MESH

D = 1024
CH = 64
TM = 256
EPS = 1e-6
NEG = -1e30
G_HEADS = 8
DK = 128
S_HEADS = 32
S_P = 64
S_GROUPS = 4
S_N = 128
S_INNER = 2048
XBC = 3072
D_FF = 2816
N_DEV = 8
PACK_W = 1024
VMEM_LIMIT = 56 * 1024 * 1024

ADAM_LR = 0.001
ADAM_B1 = 0.9
ADAM_B2 = 0.999
ADAM_EPS = 1e-08
ADAM_WD = 0.01
ADAM_STEP = 10

C_QKV, C_XBC, C_ZS, C_GATE, C_ZG, C_AB, C_DT, C_END = 0, 3072, 6144, 8192, 10240, 11264, 11392, 11520
O_QKV, O_ZG, O_AB, O_ZS, O_XBC, O_DT, O_GATE, O_END = 0, 3072, 4096, 4128, 6176, 9248, 9312, 11360


def _dot(a, b, prec=None):
    return jnp.dot(a, b, precision=prec, preferred_element_type=F32)


def _dot_nt(a, b, prec=None):
    return lax.dot_general(a, b, (((1,), (1,)), ((), ())), precision=prec, preferred_element_type=F32)


def _dot_tn(a, b, prec=None):
    return lax.dot_general(a, b, (((0,), (0,)), ((), ())), precision=prec, preferred_element_type=F32)


def _iota(shape, dim):
    return lax.broadcasted_iota(jnp.int32, shape, dim)


def _rms(x):
    return x * lax.rsqrt(jnp.mean(x * x, axis=-1, keepdims=True) + EPS)


def _l2n(x):
    return x * lax.rsqrt(jnp.sum(x * x, axis=-1, keepdims=True) + EPS)


def _silu(x):
    return x * jax.nn.sigmoid(x)


def _softplus(x):
    return jnp.maximum(x, 0.0) + jnp.log1p(jnp.exp(-jnp.abs(x)))


def _roll_rows(x, s):
    return pltpu.roll(x, s, 0)


def _up_raw(x, keep_up):
    return jnp.where(keep_up > 0.0, _roll_rows(x, 1), 0.0)


def _dn_raw(x, keep_dn):
    return jnp.where(keep_dn > 0.0, _roll_rows(x, x.shape[0] - 1), 0.0)


@jax.custom_vjp
def _shift_up(x, keep_up, keep_dn):
    return _up_raw(x, keep_up)


def _shift_up_fwd(x, keep_up, keep_dn):
    return _up_raw(x, keep_up), (keep_up, keep_dn)


def _shift_up_bwd(res, g):
    keep_up, keep_dn = res
    return _dn_raw(g, keep_dn), jnp.zeros_like(keep_up), jnp.zeros_like(keep_dn)


_shift_up.defvjp(_shift_up_fwd, _shift_up_bwd)


@jax.custom_vjp
def _shift_dn(x, keep_up, keep_dn):
    return _dn_raw(x, keep_dn)


def _shift_dn_fwd(x, keep_up, keep_dn):
    return _dn_raw(x, keep_dn), (keep_up, keep_dn)


def _shift_dn_bwd(res, g):
    keep_up, keep_dn = res
    return _up_raw(g, keep_up), jnp.zeros_like(keep_up), jnp.zeros_like(keep_dn)


_shift_dn.defvjp(_shift_dn_fwd, _shift_dn_bwd)


def _conv_keep(is_ctx, n):
    r = _iota((n, 1), 0)
    pos = jnp.where(is_ctx, r, r & (CH - 1))
    end = jnp.where(is_ctx, n - 1, CH - 1)
    return jnp.where(pos == 0, 0.0, 1.0).astype(F32), jnp.where(pos == end, 0.0, 1.0).astype(F32)


def _conv_silu(u, w3, b, keep_up, keep_dn):
    conv = b + _shift_up(u, keep_up, keep_dn) * w3[0] + u * w3[1] + _shift_dn(u, keep_up, keep_dn) * w3[2]
    return _silu(conv)


def _chunk_tri(n, rev):
    i = _iota((n, n), 0)
    j = _iota((n, n), 1)
    same = (i // CH) == (j // CH)
    seen = (i <= j) if rev else (i >= j)
    return jnp.where(same & seen, 1.0, 0.0).astype(F32)


def _expand_mat(rows, cols, per, base):
    r = _iota((rows, cols), 0)
    c = _iota((rows, cols), 1)
    return jnp.where(r == base + c // per, 1.0, 0.0).astype(F32)


def f_silu_rows(is_ctx, cvec):
    return (_silu(cvec).astype(BF16),)


def f_pre(is_ctx, x, xc, n1w, sc, sh, csc, csh):
    x = jnp.where(is_ctx, xc, x)
    sc_e = jnp.where(is_ctx, csc, sc)
    sh_e = jnp.where(is_ctx, csh, sh)
    a = _rms(x) * n1w * (1.0 + sc_e) + sh_e
    return (a.astype(BF16),)


def f_pre_thru(is_ctx, x, xc, n1w, sc, sh, csc, csh):
    return f_pre(is_ctx, x, xc, n1w, sc, sh, csc, csh)[0], x


def f_gdnprep(is_ctx, qkv_raw, ab_raw, cw, cb, alog, dtb):
    n = qkv_raw.shape[0]
    keep_up, keep_dn = _conv_keep(is_ctx, n)
    s = _conv_silu(qkv_raw, cw, cb, keep_up, keep_dn)
    qs, ks, vs = [], [], []
    for h in range(G_HEADS):
        qs.append(_l2n(s[:, h * DK:(h + 1) * DK]) * (DK ** -0.5))
        ks.append(_l2n(s[:, D + h * DK:D + (h + 1) * DK]))
    q = jnp.concatenate(qs, axis=1)
    k = jnp.concatenate(ks, axis=1)
    v = s[:, 2 * D:3 * D]
    lane = _iota(ab_raw.shape, 1)
    g = jnp.where(lane < 2 * G_HEADS, -jnp.exp(alog) * _softplus(ab_raw + dtb), 0.0)
    gcum = jnp.where(lane < G_HEADS, _dot(_chunk_tri(n, False), g, HI), _dot(_chunk_tri(n, True), g, HI))
    beta = jax.nn.sigmoid(ab_raw)
    return q, k, v, gcum, beta


def f_ssmprep(is_ctx, xbc_raw, dt_raw, cw, cb, alog, dtb):
    n = xbc_raw.shape[0]
    keep_up, keep_dn = _conv_keep(is_ctx, n)
    s = _conv_silu(xbc_raw, cw, cb, keep_up, keep_dn)
    xs = s[:, :S_INNER]
    bm = s[:, S_INNER:S_INNER + S_GROUPS * S_N]
    cm = s[:, S_INNER + S_GROUPS * S_N:]
    lane = _iota(dt_raw.shape, 1)
    dt = jnp.where(lane < 2 * S_HEADS, _softplus(dt_raw + dtb), 0.0)
    da = dt * (-jnp.exp(alog))
    acum = jnp.where(lane < S_HEADS, _dot(_chunk_tri(n, False), da, HI), _dot(_chunk_tri(n, True), da, HI))
    return xs, bm, cm, dt, acum


def f_post(is_ctx, o, zg, y_scan, xs, zs, gnw, ssd8, snw):
    ogs = []
    for h in range(G_HEADS):
        sl = slice(h * DK, (h + 1) * DK)
        ogs.append(_rms(o[:, sl]) * gnw * _silu(zg[:, sl]))
    og = jnp.concatenate(ogs, axis=1)
    row0 = jnp.where(_iota(ssd8.shape, 0) == 0, 1.0, 0.0).astype(F32)
    dexp = jnp.sum(_dot(ssd8 * row0, _expand_mat(128, S_INNER, S_P, 0), HI), axis=0, keepdims=True)
    y = (y_scan + dexp * xs) * _silu(zs)
    gw = S_INNER // S_GROUPS
    ys = jnp.concatenate([_rms(y[:, i * gw:(i + 1) * gw]) * snw[:, i * gw:(i + 1) * gw] for i in range(S_GROUPS)], axis=1)
    return og.astype(BF16), ys.astype(BF16)


def f_merge(is_ctx, gate, pg, ps):
    m = jax.nn.sigmoid(gate[:, :D]) * pg + jax.nn.sigmoid(gate[:, D:]) * ps
    return (m.astype(BF16),)


def f_res1(is_ctx, x, mix, g1, n2w, sc2, sh2):
    h1 = x + g1 * mix
    f = _rms(h1) * n2w * (1.0 + sc2) + sh2
    return h1, f.astype(BF16)


def f_act(is_ctx, u):
    return ((_silu(u[:, :D_FF]) * u[:, D_FF:]).astype(BF16),)


def f_final(is_ctx, h1, ff, tgt, g2, nfw):
    h2 = h1 + g2 * ff
    y = _rms(h2) * nfw
    err = y - tgt
    return (0.5 * jnp.sum(jnp.mean(err * err, axis=-1, keepdims=True), axis=0, keepdims=True),)


def _each(fn, *lists):
    return [fn(*args) for args in zip(*lists)]


def _tri_inverse_all(mats):
    n = mats[0].shape[0]
    eye = jnp.where(_iota((n, n), 0) == _iota((n, n), 1), 1.0, 0.0).astype(F32)
    t = [eye - a for a in mats]
    p = [_dot(a, a, HIGH) for a in mats]
    for r in range(5):
        t = _each(lambda t_, p_: t_ + _dot(t_, p_, HIGH), t, p)
        if r < 4:
            p = [_dot(p_, p_, HIGH) for p_ in p]
    return t


@jax.custom_vjp
def _inverse_given(a, t):
    return t


def _inverse_given_fwd(a, t):
    return t, t


def _inverse_given_bwd(t, g):
    return -_dot_nt(_dot_tn(t, g), t), jnp.zeros_like(t)


_inverse_given.defvjp(_inverse_given_fwd, _inverse_given_bwd)


@jax.custom_vjp
def _dot_high(a, b):
    return _dot(a, b, HIGH)


def _dot_high_fwd(a, b):
    return _dot(a, b, HIGH), (a, b)


def _dot_high_bwd(res, g):
    a, b = res
    return _dot_nt(g, b), _dot_tn(a, g)


_dot_high.defvjp(_dot_high_fwd, _dot_high_bwd)


def gdn_local(qs, ks, vs, gcs, grs, bcs, rev, t_known=None):
    c = qs[0].shape[0]
    ii = _iota((c, c), 0)
    jj = _iota((c, c), 1)
    incl = (ii <= jj) if rev else (ii >= jj)
    strict = (ii < jj) if rev else (ii > jj)
    decay = _each(lambda gc, gr: jnp.exp(jnp.where(incl, gc - gr, NEG)), gcs, grs)
    kb = _each(lambda k, bc: k * bc, ks, bcs)
    a = _each(lambda kb_, k, dc: jnp.where(strict, _dot_nt(kb_, k) * dc, 0.0), kb, ks, decay)
    t = _tri_inverse_all(a) if t_known is None else _each(_inverse_given, a, t_known)
    eg = [jnp.exp(gc) for gc in gcs]
    rhs = _each(lambda kb_, eg_, v, bc: jnp.concatenate([kb_ * eg_, v * bc], axis=1), kb, eg, vs, bcs)
    wu = _each(_dot_high, t, rhs)
    lhs = _each(lambda wu_, q, eg_: jnp.concatenate([wu_[:, :DK], q * eg_], axis=0), wu, qs, eg)
    attn = _each(lambda q, k, dc: _dot_nt(q, k) * dc, qs, ks, decay)
    return wu, attn, lhs, t


def gdn_state(ss, wu, attn, lhs, ks, gcs, rev):
    c = ks[0].shape[0]
    is_last = _iota((c, 1), 0) == (0 if rev else c - 1)
    ws = _each(_dot, lhs, ss)
    v_new = _each(lambda wu_, ws_: wu_[:, DK:] - ws_[:c], wu, ws)
    o = _each(lambda ws_, at, vn: ws_[c:] + _dot(at, vn), ws, attn, v_new)
    gtot = [jnp.sum(jnp.where(is_last, gc, 0.0), axis=0, keepdims=True) for gc in gcs]
    s_new = _each(lambda s, k, gc, gt_, vn: s * jnp.exp(gt_) + _dot_tn(k * jnp.exp(gt_ - gc), vn), ss, ks, gcs, gtot, v_new)
    return s_new, o


def gdn_chunk(ss, qs, ks, vs, gcs, grs, bcs, rev, t_known=None):
    wu, attn, lhs, t = gdn_local(qs, ks, vs, gcs, grs, bcs, rev, t_known)
    s_new, o = gdn_state(ss, wu, attn, lhs, ks, gcs, rev)
    return s_new, o, t


def ssd_pick(dt0s, dt1s, ac0s, ac1s, ar0s, ar1s):
    lo = _iota((CH, 128), 1) < S_P
    pick = lambda u0, u1: jnp.where(lo, u0, u1)
    return _each(pick, dt0s, dt1s), _each(pick, ac0s, ac1s), _each(pick, ar0s, ar1s)


def ssd_local(xs, dts, acs, acr, bgs, cgs, rev):
    c = xs[0].shape[0]
    npair = len(xs)
    grp = [p * len(bgs) // npair for p in range(npair)]
    lane = _iota((c, 128), 1)
    ii = _iota((c, 128), 0)
    jl = lane & (S_P - 1)
    lo = lane < S_P
    seen = (ii <= jl) if rev else (ii >= jl)
    last = 0 if rev else c - 1
    split = lambda z: jnp.concatenate([jnp.where(lo, z, 0.0), jnp.where(lo, 0.0, z)], axis=0)
    cb = _each(lambda bg, cg: _dot_nt(cg, jnp.concatenate([bg, bg], axis=0)), bgs, cgs)
    seg = _each(lambda ac, ar: jnp.exp(jnp.where(seen, ac - ar, NEG)), acs, acr)
    xdt = _each(lambda x, dt: x * dt, xs, dts)
    ydiag = [_dot(cb[grp[p]] * seg[p], split(xdt[p])) for p in range(npair)]
    eac = [jnp.exp(ac) for ac in acs]
    atot = [jnp.sum(jnp.where(ii == last, ac, 0.0), axis=0, keepdims=True) for ac in acs]
    upd = [_dot_tn(bgs[grp[p]], xdt[p] * jnp.exp(atot[p] - acs[p])) for p in range(npair)]
    return ydiag, eac, [jnp.exp(at) for at in atot], upd


def ssd_state(hts, ydiag, eac, etot, upd, cgs):
    npair = len(hts)
    grp = [p * len(cgs) // npair for p in range(npair)]
    y = [ydiag[p] + _dot(cgs[grp[p]], hts[p]) * eac[p] for p in range(npair)]
    h_new = [hts[p] * etot[p] + upd[p] for p in range(npair)]
    return h_new, y


def ssd_chunk(hts, xs, dts, acs, acr, bgs, cgs, rev):
    ydiag, eac, etot, upd = ssd_local(xs, dts, acs, acr, bgs, cgs, rev)
    return ssd_state(hts, ydiag, eac, etot, upd, cgs)


def f_adamw(w, g, m, v):
    m = ADAM_B1 * m + (1.0 - ADAM_B1) * g
    v = ADAM_B2 * v + (1.0 - ADAM_B2) * jnp.square(g)
    m_hat = m / (1.0 - ADAM_B1 ** ADAM_STEP)
    v_hat = v / (1.0 - ADAM_B2 ** ADAM_STEP)
    delta = -ADAM_LR * (m_hat / (jnp.sqrt(v_hat) + ADAM_EPS) + ADAM_WD * w)
    return delta, m, v


def _cparams(sem):
    return pltpu.CompilerParams(dimension_semantics=sem, vmem_limit_bytes=VMEM_LIMIT)


def _pick(n, target):
    if n <= target:
        return n
    best = None
    for t in range(128, target + 1, 128):
        if n % t == 0:
            best = t
    assert best is not None, (n, target)
    return best


def _row_spec(tm, width, colblk, rowoff):
    return pl.BlockSpec((tm, width), lambda i: (i + rowoff, colblk))


def _par_spec(shape):
    nd = len(shape)
    return pl.BlockSpec(tuple(shape), lambda i: (0,) * nd)


def _rowwise(name, fn, rows, pars, outs, ntiles, base=1, tm=TM):
    nr, npar = len(rows), len(pars)

    def body(*refs):
        is_ctx = (pl.program_id(0) + base) == 0
        res = fn(is_ctx, *[r[...] for r in refs[:nr]], *[p[...] for p in refs[nr:nr + npar]])
        for o_ref, r in zip(refs[nr + npar:], res):
            o_ref[...] = r.astype(o_ref.dtype)

    return pl.pallas_call(
        body, grid=(ntiles,), name=name,
        in_specs=[_ct_spec(tm, d) for d in rows] + [_par_spec(p.shape) for p in pars],
        out_specs=[_row_spec(tm, wd, 0, 0) for (wd, _) in outs],
        out_shape=[jax.ShapeDtypeStruct((ntiles * tm, wd), dt) for (wd, dt) in outs],
        compiler_params=_cparams(("arbitrary",)),
    )(*[r[0] for r in rows], *pars)


def _ct_spec(tm, desc):
    _, wd, cb, ro = desc[:4]
    flag = desc[4] if len(desc) > 4 else False
    if flag == "first":
        return pl.BlockSpec((tm, wd), lambda i: (0, cb))
    if flag:
        return pl.BlockSpec((tm, wd), lambda i: (jnp.maximum(i + ro, 0), cb))
    return _row_spec(tm, wd, cb, ro)


def _rowwise_bwd(name, fn, rows, pars, cts, drows, dpars, ntiles, base=1, loss_out=False, tm=TM):
    nr, npar = len(rows), len(pars)
    ct_rows = [d for ct in cts if isinstance(ct, list) for d in ct]
    nct = len(ct_rows)

    def body(*refs):
        i = pl.program_id(0)
        is_ctx = (i + base) == 0
        rows_v = [r[...] for r in refs[:nr]]
        pars_v = [p[...] for p in refs[nr:nr + npar]]
        ct_refs = list(refs[nr + npar:nr + npar + nct])
        out_refs = list(refs[nr + npar + nct:])
        outs, vjp = jax.vjp(lambda rv, pv: fn(is_ctx, *rv, *pv), rows_v, pars_v)

        def ct_value(desc):
            val = ct_refs.pop(0)[...].astype(F32)
            if len(desc) > 4 and desc[4]:
                val = jnp.where(is_ctx, 0.0, val)
            return val

        ct_vals = []
        for o, ct in zip(outs, cts):
            if ct is None:
                ct_vals.append(jnp.zeros_like(o))
            elif isinstance(ct, str):
                ct_vals.append(jnp.ones_like(o))
            else:
                acc = ct_value(ct[0])
                for desc in ct[1:]:
                    acc = acc + ct_value(desc)
                ct_vals.append(acc.astype(o.dtype))
        d_rows, d_pars = vjp(tuple(ct_vals))
        for (ri, _), o_ref in zip(drows, out_refs[:len(drows)]):
            o_ref[...] = d_rows[ri].astype(o_ref.dtype)
        acc_refs = out_refs[len(drows):]
        acc_vals = [d_pars[pi] for pi in dpars]
        if loss_out:
            acc_vals.append(jnp.broadcast_to(outs[0], (8, 128)))

        @pl.when(i == 0)
        def _():
            for o_ref, val in zip(acc_refs, acc_vals):
                o_ref[...] = val

        @pl.when(i > 0)
        def _():
            for o_ref, val in zip(acc_refs, acc_vals):
                o_ref[...] += val

    acc_shapes = [pars[pi].shape for pi in dpars] + ([(8, 128)] if loss_out else [])
    lat_row = lambda ri: len(rows[ri]) > 4 and rows[ri][4] is True
    return pl.pallas_call(
        body, grid=(ntiles,), name=name,
        in_specs=[_ct_spec(tm, d) for d in rows] + [_par_spec(p.shape) for p in pars]
        + [_ct_spec(tm, d) for d in ct_rows],
        out_specs=[_ct_spec(tm, (None, rows[ri][1], 0, -1, True) if lat_row(ri) else (None, rows[ri][1], 0, 0))
                   for (ri, _) in drows] + [_par_spec(s) for s in acc_shapes],
        out_shape=[jax.ShapeDtypeStruct(((ntiles - int(lat_row(ri))) * tm, rows[ri][1]), dt) for (ri, dt) in drows]
        + [jax.ShapeDtypeStruct(tuple(s), F32) for s in acc_shapes],
        compiler_params=_cparams(("arbitrary",)),
    )(*[r[0] for r in rows], *pars, *[r[0] for r in ct_rows])


def _mm(name, a, b, mode, out_dtype, tm=1024, tn=1024, tk=1024, ride=None):
    if mode == "nn":
        (m, kd), (_, n) = a.shape, b.shape
    elif mode == "nt":
        (m, kd), (n, _) = a.shape, b.shape
    else:
        (kd, m), (_, n) = a.shape, b.shape
    tm, tn, tk = _pick(m, tm), _pick(n, tn), _pick(kd, tk)
    nk = kd // tk
    a_spec = {"nn": pl.BlockSpec((tm, tk), lambda i, j, k: (i, k)), "nt": pl.BlockSpec((tm, tk), lambda i, j, k: (i, k)),
              "tn": pl.BlockSpec((tk, tm), lambda i, j, k: (k, i))}[mode]
    b_spec = {"nn": pl.BlockSpec((tk, tn), lambda i, j, k: (k, j)), "nt": pl.BlockSpec((tn, tk), lambda i, j, k: (j, k)),
              "tn": pl.BlockSpec((tk, tn), lambda i, j, k: (k, j))}[mode]
    dot = {"nn": _dot, "nt": _dot_nt, "tn": _dot_tn}[mode]

    if nk == 1:
        def body(a_ref, b_ref, o_ref):
            o_ref[...] = dot(a_ref[...].astype(BF16), b_ref[...].astype(BF16)).astype(o_ref.dtype)
    else:
        def body(a_ref, b_ref, o_ref, acc_ref):
            k = pl.program_id(2)
            part = dot(a_ref[...].astype(BF16), b_ref[...].astype(BF16))

            @pl.when(k == 0)
            def _():
                acc_ref[...] = part

            @pl.when((k > 0) & (k < nk - 1))
            def _():
                acc_ref[...] += part

            @pl.when(k == nk - 1)
            def _():
                o_ref[...] = (acc_ref[...] + part).astype(o_ref.dtype)

    grid = (m // tm, n // tn, nk)
    at = lambda pos: functools.reduce(jnp.logical_and, [pl.program_id(ax) == pos(g) for ax, g in enumerate(grid)])
    r_in, r_out, r_shapes, r_scr, r_ops = _ride_args(ride)
    res = pl.pallas_call(
        _ride(body, ride, 2, 1, lambda: at(lambda g: 0), lambda: at(lambda g: g - 1)), grid=grid, name=name,
        in_specs=[a_spec, b_spec] + r_in,
        out_specs=[pl.BlockSpec((tm, tn), lambda i, j, k: (i, j))] + r_out,
        out_shape=[jax.ShapeDtypeStruct((m, n), out_dtype)] + r_shapes,
        scratch_shapes=([] if nk == 1 else [pltpu.VMEM((tm, tn), F32)]) + r_scr,
        compiler_params=_cparams(("arbitrary", "arbitrary", "arbitrary")),
    )(a, b, *r_ops)
    return res[0] if ride is None else (res[0], res[1:])


def _chunk_index(i, nch, nctx, rev):
    if not rev:
        return i
    return jnp.where(i < nctx, nctx - 1 - i, nch + nctx - 1 - i)


def _gdn_cols(d):
    return [d * G_HEADS + h for h in range(G_HEADS)], [2 * G_HEADS + d * G_HEADS + h for h in range(G_HEADS)]


def _gdn_operands(q_ref, k_ref, v_ref, g_ref, b_ref, d, rows=slice(None)):
    cols_g, cols_b = _gdn_cols(d)
    sls = [slice(h * DK, (h + 1) * DK) for h in range(G_HEADS)]
    gt, bt = g_ref[rows, :], b_ref[rows, :]
    gtt = gt.T
    qs = [q_ref[rows, sl] for sl in sls]
    ks = [k_ref[rows, sl] for sl in sls]
    vs = [v_ref[rows, sl] for sl in sls]
    gcs = [gt[:, cg:cg + 1] for cg in cols_g]
    grs = [gtt[cg:cg + 1, :] for cg in cols_g]
    bcs = [bt[:, cb:cb + 1] for cb in cols_b]
    return sls, qs, ks, vs, gcs, grs, bcs


GDN_FWD_CHUNKS = 4


def _gdn_scan_fwd(name, q, k, v, gcum, beta, d, nctx, ride=None, add=None):
    t = q.shape[0]
    nch = t // CH
    per = GDN_FWD_CHUNKS
    nst = nch // per
    rev = d == 1
    bix = lambda i: _chunk_index(i, nst, nctx // per, rev)
    full = pl.BlockSpec((per * CH, D), lambda i: (bix(i), 0))
    wide = pl.BlockSpec((per * CH, 128), lambda i: (bix(i), 0))
    order = list(range(per - 1, -1, -1)) if rev else list(range(per))

    def body(q_ref, k_ref, v_ref, g_ref, b_ref, *rest):
        add_ref = rest[0] if add is not None else None
        o_ref, ss_ref, ts_ref, s_scr = rest[-4:]

        @pl.when(pl.program_id(0) == 0)
        def _():
            s_scr[...] = jnp.zeros(s_scr.shape, F32)

        ops = [_gdn_operands(q_ref, k_ref, v_ref, g_ref, b_ref, d, slice(c * CH, (c + 1) * CH)) for c in order]
        sls = ops[0][0]
        cat = [sum((o[i] for o in ops), []) for i in range(1, 7)]
        wu, attn, lhs, tinv = gdn_local(*cat, rev)
        ss = [s_scr[h] for h in range(G_HEADS)]
        for n, c in enumerate(order):
            sl = slice(n * G_HEADS, (n + 1) * G_HEADS)
            s_new, o = gdn_state(ss, wu[sl], attn[sl], lhs[sl], cat[1][sl], cat[3][sl], rev)
            for h in range(G_HEADS):
                ss_ref[n, h] = ss[h]
                ts_ref[n, h] = tinv[n * G_HEADS + h]
                rows = slice(c * CH, (c + 1) * CH)
                o_ref[rows, sls[h]] = o[h] if add is None else o[h] + add_ref[rows, sls[h]]
            ss = s_new
        for h in range(G_HEADS):
            s_scr[h] = ss[h]

    r_in, r_out, r_shapes, r_scr, r_ops = _ride_args(ride)
    extra = [] if add is None else [add]
    res = pl.pallas_call(
        _ride(body, ride, 5 + len(extra), 3, lambda: pl.program_id(0) == 0, lambda: pl.program_id(0) == nst - 1), grid=(nst,), name=name,
        in_specs=[full, full, full, wide, wide] + [full] * len(extra) + r_in,
        out_specs=[full, pl.BlockSpec((per, G_HEADS, DK, DK), lambda i: (i, 0, 0, 0)),
                   pl.BlockSpec((per, G_HEADS, CH, CH), lambda i: (i, 0, 0, 0))] + r_out,
        out_shape=[jax.ShapeDtypeStruct((t, D), F32), jax.ShapeDtypeStruct((nch, G_HEADS, DK, DK), F32),
                   jax.ShapeDtypeStruct((nch, G_HEADS, CH, CH), F32)] + r_shapes,
        scratch_shapes=[pltpu.VMEM((G_HEADS, DK, DK), F32)] + r_scr,
        compiler_params=_cparams(("arbitrary",)),
    )(q, k, v, gcum, beta, *extra, *r_ops)
    return res if ride is None else (*res[:3], res[3:])


def _gdn_scan_bwd(name, q, k, v, gcum, beta, ssave, tsave, do, d, nctx, ride=None):
    t = q.shape[0]
    nch = t // CH
    rev = d == 1
    cix = lambda ib: _chunk_index(nch - 1 - ib, nch, nctx, rev)
    full = pl.BlockSpec((CH, D), lambda ib: (cix(ib), 0))
    wide = pl.BlockSpec((CH, 128), lambda ib: (cix(ib), 0))
    do_spec = pl.BlockSpec((CH, D), lambda ib: (jnp.maximum(cix(ib), nctx) - nctx, 0))

    def body(q_ref, k_ref, v_ref, g_ref, b_ref, ss_ref, ts_ref, do_ref, dq_ref, dk_ref, dv_ref, dg_ref, db_ref, ds_scr):
        ib = pl.program_id(0)

        @pl.when(ib == 0)
        def _():
            ds_scr[...] = jnp.zeros(ds_scr.shape, F32)

        sls, qs, ks, vs, gcs, grs, bcs = _gdn_operands(q_ref, k_ref, v_ref, g_ref, b_ref, d)
        t_known = [ts_ref[0, h] for h in range(G_HEADS)]
        cols_g, cols_b = _gdn_cols(d)
        is_lat = cix(ib) >= nctx
        ss = [ss_ref[0, h] for h in range(G_HEADS)]
        do_v = [jnp.where(is_lat, do_ref[:, sl], 0.0) for sl in sls]
        ds_in = [ds_scr[h] for h in range(G_HEADS)]
        _, vjp = jax.vjp(lambda *a: gdn_chunk(*a, rev, t_known)[:2], ss, qs, ks, vs, gcs, grs, bcs)
        ds, dq, dk, dv, dgc, dgr, dbc = vjp((ds_in, do_v))
        lane = _iota((CH, 128), 1)
        sub = _iota((128, CH), 0)
        dg = jnp.zeros((CH, 128), F32)
        dgt = jnp.zeros((128, CH), F32)
        db = jnp.zeros((CH, 128), F32)
        for h in range(G_HEADS):
            ds_scr[h] = ds[h]
            dq_ref[:, sls[h]] = dq[h]
            dk_ref[:, sls[h]] = dk[h]
            dv_ref[:, sls[h]] = dv[h]
            dg = dg + jnp.where(lane == cols_g[h], dgc[h], 0.0)
            dgt = dgt + jnp.where(sub == cols_g[h], dgr[h], 0.0)
            db = db + jnp.where(lane == cols_b[h], dbc[h], 0.0)
        dg_ref[...] = dg + dgt.T
        db_ref[...] = db

    r_in, r_out, r_shapes, r_scr, r_ops = _ride_args(ride)
    res = pl.pallas_call(
        _ride(body, ride, 8, 5, lambda: pl.program_id(0) == 0, lambda: pl.program_id(0) == nch - 1), grid=(nch,), name=name,
        in_specs=[full, full, full, wide, wide,
                  pl.BlockSpec((1, G_HEADS, DK, DK), lambda ib: (nch - 1 - ib, 0, 0, 0)),
                  pl.BlockSpec((1, G_HEADS, CH, CH), lambda ib: (nch - 1 - ib, 0, 0, 0)), do_spec] + r_in,
        out_specs=[full, full, full, wide, wide] + r_out,
        out_shape=[jax.ShapeDtypeStruct((t, D), F32)] * 3 + [jax.ShapeDtypeStruct((t, 128), F32)] * 2 + r_shapes,
        scratch_shapes=[pltpu.VMEM((G_HEADS, DK, DK), F32)] + r_scr,
        compiler_params=_cparams(("arbitrary",)),
    )(q, k, v, gcum, beta, ssave, tsave, do, *r_ops)
    return res if ride is None else (*res[:5], res[5:])


N_PAIRS = S_HEADS // 2


def _ssd_operands(x_ref, dt_ref, ac_ref, b_ref, c_ref, d, rows=slice(None)):
    sls = [slice(p * 128, (p + 1) * 128) for p in range(N_PAIRS)]
    gsl = [slice(g * S_N, (g + 1) * S_N) for g in range(S_GROUPS)]
    cols = [d * S_HEADS + h for h in range(S_HEADS)]
    dtc, acc = dt_ref[rows, :], ac_ref[rows, :]
    act = jnp.concatenate([acc, acc], axis=0).T
    col = lambda z, cc: z[:, cc:cc + 1]
    dts, acs, acr = ssd_pick(
        [col(dtc, cols[2 * p]) for p in range(N_PAIRS)], [col(dtc, cols[2 * p + 1]) for p in range(N_PAIRS)],
        [col(acc, cols[2 * p]) for p in range(N_PAIRS)], [col(acc, cols[2 * p + 1]) for p in range(N_PAIRS)],
        [act[cols[2 * p]:cols[2 * p] + 1, :] for p in range(N_PAIRS)],
        [act[cols[2 * p + 1]:cols[2 * p + 1] + 1, :] for p in range(N_PAIRS)])
    ops = ([x_ref[rows, sl] for sl in sls], dts, acs, acr, [b_ref[rows, gs] for gs in gsl], [c_ref[rows, gs] for gs in gsl])
    return sls, gsl, cols, ops


SSD_FWD_CHUNKS = 4


def _ssd_scan_fwd(name, xs, dtc, acc, bm, cm, d, nctx, add=None):
    t = xs.shape[0]
    nch = t // CH
    per = SSD_FWD_CHUNKS
    nst = nch // per
    rev = d == 1
    bix = lambda i: _chunk_index(i, nst, nctx // per, rev)
    inner = pl.BlockSpec((per * CH, S_INNER), lambda i: (bix(i), 0))
    wide = pl.BlockSpec((per * CH, 128), lambda i: (bix(i), 0))
    grp = pl.BlockSpec((per * CH, S_GROUPS * S_N), lambda i: (bix(i), 0))
    order = list(range(per - 1, -1, -1)) if rev else list(range(per))

    def body(x_ref, dt_ref, ac_ref, b_ref, c_ref, *rest):
        add_ref = rest[0] if add is not None else None
        y_ref, hs_ref, h_scr = rest[-3:]

        @pl.when(pl.program_id(0) == 0)
        def _():
            h_scr[...] = jnp.zeros(h_scr.shape, F32)

        loc, cgs, sls = [], [], None
        for c in order:
            sls, _, _, ops = _ssd_operands(x_ref, dt_ref, ac_ref, b_ref, c_ref, d, slice(c * CH, (c + 1) * CH))
            loc.append(ssd_local(*ops, rev))
            cgs.append(ops[5])
        hts = [h_scr[p] for p in range(N_PAIRS)]
        for n, c in enumerate(order):
            h_new, y = ssd_state(hts, *loc[n], cgs[n])
            for p in range(N_PAIRS):
                hs_ref[n, p] = hts[p]
                rows = slice(c * CH, (c + 1) * CH)
                y_ref[rows, sls[p]] = y[p] if add is None else y[p] + add_ref[rows, sls[p]]
            hts = h_new
        for p in range(N_PAIRS):
            h_scr[p] = hts[p]

    return pl.pallas_call(
        body, grid=(nst,), name=name,
        in_specs=[inner, wide, wide, grp, grp] + ([] if add is None else [inner]),
        out_specs=[inner, pl.BlockSpec((per, N_PAIRS, S_N, 128), lambda i: (i, 0, 0, 0))],
        out_shape=[jax.ShapeDtypeStruct((t, S_INNER), F32), jax.ShapeDtypeStruct((nch, N_PAIRS, S_N, 128), F32)],
        scratch_shapes=[pltpu.VMEM((N_PAIRS, S_N, 128), F32)],
        compiler_params=_cparams(("arbitrary",)),
    )(xs, dtc, acc, bm, cm, *([] if add is None else [add]))


def _ssd_scan_bwd(name, xs, dtc, acc, bm, cm, hsave, dy, d, nctx):
    t = xs.shape[0]
    nch = t // CH
    rev = d == 1
    cix = lambda ib: _chunk_index(nch - 1 - ib, nch, nctx, rev)
    inner = pl.BlockSpec((CH, S_INNER), lambda ib: (cix(ib), 0))
    wide = pl.BlockSpec((CH, 128), lambda ib: (cix(ib), 0))
    grp = pl.BlockSpec((CH, S_GROUPS * S_N), lambda ib: (cix(ib), 0))
    dy_spec = pl.BlockSpec((CH, S_INNER), lambda ib: (jnp.maximum(cix(ib), nctx) - nctx, 0))

    def body(x_ref, dt_ref, ac_ref, b_ref, c_ref, hs_ref, dy_ref, dx_ref, ddt_ref, dac_ref, db_ref, dc_ref, dh_scr):
        ib = pl.program_id(0)

        @pl.when(ib == 0)
        def _():
            dh_scr[...] = jnp.zeros(dh_scr.shape, F32)

        sls, gsl, cols, ops = _ssd_operands(x_ref, dt_ref, ac_ref, b_ref, c_ref, d)
        is_lat = cix(ib) >= nctx
        hts = [hs_ref[0, p] for p in range(N_PAIRS)]
        dy_v = [jnp.where(is_lat, dy_ref[:, sl], 0.0) for sl in sls]
        dh_in = [dh_scr[p] for p in range(N_PAIRS)]
        _, vjp = jax.vjp(lambda *a: ssd_chunk(*a, rev), hts, *ops)
        dh, dx, ddts, dacs, dacr, db, dc = vjp((dh_in, dy_v))
        for p in range(N_PAIRS):
            dh_scr[p] = dh[p]
            dx_ref[:, sls[p]] = dx[p]
        r = _iota((S_INNER, 128), 0)
        e_t = jnp.where(_iota((S_INNER, 128), 1) == d * S_HEADS + r // S_P, 1.0, 0.0).astype(F32)
        ddt_ref[...] = _dot(jnp.concatenate(ddts, axis=1), e_t, HIGH)
        dac_cols = _dot(jnp.concatenate(dacs, axis=1), e_t, HIGH)
        sub = _iota((128, 128), 0)
        lane = _iota((128, 128), 1)
        m = jnp.zeros((128, 128), F32)
        for p in range(N_PAIRS):
            m = m + jnp.where(sub == p, jnp.sum(dacr[p], axis=0, keepdims=True), 0.0)
        mt = m.T
        s0 = jnp.where(lane == d * S_HEADS + 2 * sub, 1.0, 0.0).astype(F32)
        s1 = jnp.where(lane == d * S_HEADS + 2 * sub + 1, 1.0, 0.0).astype(F32)
        dac_ref[...] = dac_cols + _dot(mt[:CH], s0, HIGH) + _dot(mt[CH:], s1, HIGH)
        for g in range(S_GROUPS):
            db_ref[:, gsl[g]] = db[g]
            dc_ref[:, gsl[g]] = dc[g]

    return pl.pallas_call(
        body, grid=(nch,), name=name,
        in_specs=[inner, wide, wide, grp, grp,
                  pl.BlockSpec((1, N_PAIRS, S_N, 128), lambda ib: (nch - 1 - ib, 0, 0, 0)), dy_spec],
        out_specs=[inner, wide, wide, grp, grp],
        out_shape=[jax.ShapeDtypeStruct((t, S_INNER), F32)] + [jax.ShapeDtypeStruct((t, 128), F32)] * 2
        + [jax.ShapeDtypeStruct((t, S_GROUPS * S_N), F32)] * 2,
        scratch_shapes=[pltpu.VMEM((N_PAIRS, S_N, 128), F32)],
        compiler_params=_cparams(("arbitrary",)),
    )(xs, dtc, acc, bm, cm, hsave, dy)


def _mesh_pos():
    return lax.axis_index("x"), lax.axis_index("y"), lax.axis_index("c")


def _hbm_specs(n):
    return [pl.BlockSpec(memory_space=pl.ANY)] * n


def _sem_shapes(nw):
    return [pltpu.SemaphoreType.DMA((nw, 7)), pltpu.SemaphoreType.DMA((nw, 7)), pltpu.SemaphoreType.DMA((nw,))]


class _AllGather:
    def __init__(self, shards):
        self.arrays = list(shards)
        self.out_shapes = [jax.ShapeDtypeStruct((N_DEV,) + xs.shape, xs.dtype) for xs in shards]

    def _parts(self, x_refs, out_refs, sems):
        send_sems, recv_sems, local_sems = sems
        nw = len(x_refs)
        x, y, c = _mesh_pos()
        me, sibling = (x, y, c), (x, y, 1 - c)
        chips = [(1 - x, y), (x, 1 - y), (1 - x, 1 - y)]

        def slot(w, px, py, pc):
            return out_refs[w].at[4 * px + 2 * py + pc]

        def copy(w, k, block, to, src=None):
            return pltpu.make_async_remote_copy(
                src_ref=slot(w, *block) if src is None else src, dst_ref=slot(w, *block),
                send_sem=send_sems.at[w, k], recv_sem=recv_sems.at[w, k], device_id=to, device_id_type=MESH)

        mine = [pltpu.make_async_copy(x_refs[w], slot(w, *me), local_sems.at[w]) for w in range(nw)]
        first = []
        for w in range(nw):
            first.append(copy(w, 0, me, sibling, src=x_refs[w]))
            first += [copy(w, 1 + j, me, (*chip, c), src=x_refs[w]) for j, chip in enumerate(chips)]
        return nw, me, sibling, chips, c, copy, mine, first

    def start(self, x_refs, out_refs, sems):
        _, _, _, _, _, _, mine, first = self._parts(x_refs, out_refs, sems)
        for cp in mine + first:
            cp.start()

    def finish(self, x_refs, out_refs, sems):
        nw, me, sibling, chips, c, copy, mine, first = self._parts(x_refs, out_refs, sems)
        passed = []
        for j, chip in enumerate(chips):
            for w in range(nw):
                copy(w, 1 + j, (*chip, c), me).wait_recv()
                fwd = copy(w, 4 + j, (*chip, c), sibling)
                fwd.start()
                passed.append(fwd)
        for w in range(nw):
            copy(w, 0, sibling, me).wait_recv()
            for j, chip in enumerate(chips):
                copy(w, 4 + j, (*chip, 1 - c), me).wait_recv()
        for cp in first + passed:
            cp.wait_send()
        for cp in mine:
            cp.wait()


class _AllToAll:
    def __init__(self, blocks):
        self.arrays = list(blocks)
        self.out_shapes = [jax.ShapeDtypeStruct(g.shape, g.dtype) for g in blocks]

    def _parts(self, g_refs, out_refs, sems):
        send_sems, recv_sems, local_sems = sems
        nw = len(g_refs)
        x, y, c = _mesh_pos()
        me_i = 4 * x + 2 * y + c
        mine = [pltpu.make_async_copy(g_refs[w].at[me_i], out_refs[w].at[me_i], local_sems.at[w]) for w in range(nw)]
        cps = []
        for k in range(1, N_DEV):
            px = 1 - x if (k >> 2) & 1 else x
            py = 1 - y if (k >> 1) & 1 else y
            pc = 1 - c if k & 1 else c
            for w in range(nw):
                cps.append(pltpu.make_async_remote_copy(
                    src_ref=g_refs[w].at[4 * px + 2 * py + pc], dst_ref=out_refs[w].at[me_i],
                    send_sem=send_sems.at[w, k - 1], recv_sem=recv_sems.at[w, k - 1],
                    device_id=(px, py, pc), device_id_type=MESH))
        return mine, cps

    def start(self, g_refs, out_refs, sems):
        mine, cps = self._parts(g_refs, out_refs, sems)
        for cp in mine + cps:
            cp.start()

    def finish(self, g_refs, out_refs, sems):
        mine, cps = self._parts(g_refs, out_refs, sems)
        for cp in cps + mine:
            cp.wait()


def _exchange(name, ex):
    nw = len(ex.arrays)

    def body(*refs):
        ins, outs, sems = refs[:nw], refs[nw:2 * nw], refs[2 * nw:]
        ex.start(ins, outs, sems)
        ex.finish(ins, outs, sems)

    return pl.pallas_call(body, name=name, out_shape=ex.out_shapes, in_specs=_hbm_specs(nw), out_specs=_hbm_specs(nw),
                          scratch_shapes=_sem_shapes(nw))(*ex.arrays)


def _ride(body, ex, n_in, n_out, is_first, is_last):
    if ex is None:
        return body
    nw = len(ex.arrays)

    def riding(*refs):
        ins, ex_in = refs[:n_in], refs[n_in:n_in + nw]
        outs = refs[n_in + nw:n_in + nw + n_out]
        ex_out = refs[n_in + nw + n_out:n_in + 2 * nw + n_out]
        rest = refs[n_in + 2 * nw + n_out:]
        scratch, sems = rest[:len(rest) - 3], rest[len(rest) - 3:]

        @pl.when(is_first())
        def _():
            ex.start(ex_in, ex_out, sems)

        body(*ins, *outs, *scratch)

        @pl.when(is_last())
        def _():
            ex.finish(ex_in, ex_out, sems)

    return riding


def _ride_args(ex):
    if ex is None:
        return [], [], [], [], []
    nw = len(ex.arrays)
    return _hbm_specs(nw), _hbm_specs(nw), list(ex.out_shapes), _sem_shapes(nw), list(ex.arrays)


def _reduce_adam(name, recv, w, m, v, tm):
    rows, width = w.shape
    nslot = recv.shape[0]

    def body(recv_ref, w_ref, m_ref, v_ref, g_ref, d_ref, m2_ref, v2_ref):
        g = recv_ref[0].astype(F32)
        for s in range(1, nslot):
            g = g + recv_ref[s].astype(F32)
        delta, m2, v2 = f_adamw(w_ref[...], g, m_ref[...], v_ref[...])
        g_ref[...] = g
        d_ref[...] = delta
        m2_ref[...] = m2
        v2_ref[...] = v2

    row = pl.BlockSpec((tm, width), lambda i: (i, 0))
    return pl.pallas_call(
        body, grid=(rows // tm,), name=name,
        in_specs=[pl.BlockSpec((nslot, tm, width), lambda i: (0, i, 0)), row, row, row],
        out_specs=[row] * 4,
        out_shape=[jax.ShapeDtypeStruct((rows, width), F32)] * 4,
        compiler_params=_cparams(("arbitrary",)),
    )(recv, w, m, v)


BIG = ("w_in", "ada_w", "w_br_gdn", "w_br_ssm", "w_out", "w_ffn_in", "w_ffn_out")
BIG_FIRST = ("ada_w", "w_in")
BIG_LATE = ("w_br_gdn", "w_br_ssm", "w_out", "w_ffn_in", "w_ffn_out")
BIG_COL_SHARDED = ("w_in", "ada_w", "w_ffn_in")
BIG_ADAM_ROWS = dict(w_in=128, ada_w=256, w_br_gdn=128, w_br_ssm=256, w_out=128, w_ffn_in=256, w_ffn_out=352)
CONV = ("gdn_conv_w", "ssm_conv_w")
SMALL = ("c_ctx", "ada_b", "norm1_w", "gdn_conv_b", "gdn_a_log", "gdn_dt_bias", "gdn_norm_w", "ssm_conv_b",
         "ssm_a_log", "ssm_dt_bias", "ssm_d", "ssm_norm_w", "norm2_w", "norm_f_w")
CONV_SHARD = XBC // N_DEV


def _to_rows(a):
    flat = a.reshape(-1)
    pad = (-flat.shape[0]) % PACK_W
    if pad:
        flat = jnp.pad(flat, (0, pad))
    return flat.reshape(-1, PACK_W)


def _pack(arrays, rows=None):
    buf = jnp.concatenate([_to_rows(a) for a in arrays], axis=0)
    if rows is not None and rows > buf.shape[0]:
        buf = jnp.pad(buf, ((0, rows - buf.shape[0]), (0, 0)))
    return buf


def _unpack(buf, shapes):
    out, r0 = [], 0
    for shp in shapes:
        n = 1
        for s in shp:
            n *= s
        nr = -(-n // PACK_W)
        out.append(buf[r0:r0 + nr].reshape(-1)[:n].reshape(shp))
        r0 += nr
    return out


def _full_from_blocks(blocks, col_sharded):
    _, r, c = blocks.shape
    if col_sharded:
        return jnp.transpose(blocks, (1, 0, 2)).reshape(r, N_DEV * c)
    return blocks.reshape(N_DEV * r, c)


def _blocks_from_full(full, col_sharded):
    if col_sharded:
        r, c = full.shape[0], full.shape[1] // N_DEV
        return jnp.transpose(full.reshape(r, N_DEV, c), (1, 0, 2))
    return full.reshape(N_DEV, full.shape[0] // N_DEV, full.shape[1])


def _pad_cols(a, n):
    return jnp.pad(a, ((0, 0), (0, n - a.shape[1])))


def _w_cat(w_in):
    return jnp.concatenate([
        w_in[:, O_QKV:O_ZG], w_in[:, O_XBC:O_DT], w_in[:, O_ZS:O_XBC], w_in[:, O_GATE:O_END], w_in[:, O_ZG:O_AB],
        _pad_cols(w_in[:, O_AB:O_ZS], 128), _pad_cols(w_in[:, O_DT:O_GATE], 128)], axis=1)


def _w_uncat(wc):
    return jnp.concatenate([
        wc[:, C_QKV:C_XBC], wc[:, C_ZG:C_AB], wc[:, C_AB:C_AB + (O_ZS - O_AB)], wc[:, C_ZS:C_GATE], wc[:, C_XBC:C_ZS],
        wc[:, C_DT:C_DT + (O_GATE - O_DT)], wc[:, C_GATE:C_ZG]], axis=1)


def _pad_row(vec, n=128):
    vec = vec.reshape(1, -1)
    return _pad_cols(vec, n)


def kernel(x, c, ctx, c_ctx, ada_w, ada_b, norm1_w, w_in, gdn_conv_w, gdn_conv_b, gdn_a_log, gdn_dt_bias, gdn_norm_w, ssm_conv_w, ssm_conv_b, ssm_a_log, ssm_dt_bias, ssm_d, ssm_norm_w, w_br_gdn, w_br_ssm, w_out, norm2_w, w_ffn_in, w_ffn_out, norm_f_w, loss_target, m_c_ctx, m_ada_w, m_ada_b, m_norm1_w, m_w_in, m_gdn_conv_w, m_gdn_conv_b, m_gdn_a_log, m_gdn_dt_bias, m_gdn_norm_w, m_ssm_conv_w, m_ssm_conv_b, m_ssm_a_log, m_ssm_dt_bias, m_ssm_d, m_ssm_norm_w, m_w_br_gdn, m_w_br_ssm, m_w_out, m_norm2_w, m_w_ffn_in, m_w_ffn_out, m_norm_f_w, v_c_ctx, v_ada_w, v_ada_b, v_norm1_w, v_w_in, v_gdn_conv_w, v_gdn_conv_b, v_gdn_a_log, v_gdn_dt_bias, v_gdn_norm_w, v_ssm_conv_w, v_ssm_conv_b, v_ssm_a_log, v_ssm_dt_bias, v_ssm_d, v_ssm_norm_w, v_w_br_gdn, v_w_br_ssm, v_w_out, v_norm2_w, v_w_ffn_in, v_w_ffn_out, v_norm_f_w):
    wts = dict(c_ctx=c_ctx, ada_w=ada_w, ada_b=ada_b, norm1_w=norm1_w, w_in=w_in, gdn_conv_w=gdn_conv_w, gdn_conv_b=gdn_conv_b, gdn_a_log=gdn_a_log, gdn_dt_bias=gdn_dt_bias, gdn_norm_w=gdn_norm_w, ssm_conv_w=ssm_conv_w, ssm_conv_b=ssm_conv_b, ssm_a_log=ssm_a_log, ssm_dt_bias=ssm_dt_bias, ssm_d=ssm_d, ssm_norm_w=ssm_norm_w, w_br_gdn=w_br_gdn, w_br_ssm=w_br_ssm, w_out=w_out, norm2_w=norm2_w, w_ffn_in=w_ffn_in, w_ffn_out=w_ffn_out, norm_f_w=norm_f_w)
    mom1 = dict(c_ctx=m_c_ctx, ada_w=m_ada_w, ada_b=m_ada_b, norm1_w=m_norm1_w, w_in=m_w_in, gdn_conv_w=m_gdn_conv_w, gdn_conv_b=m_gdn_conv_b, gdn_a_log=m_gdn_a_log, gdn_dt_bias=m_gdn_dt_bias, gdn_norm_w=m_gdn_norm_w, ssm_conv_w=m_ssm_conv_w, ssm_conv_b=m_ssm_conv_b, ssm_a_log=m_ssm_a_log, ssm_dt_bias=m_ssm_dt_bias, ssm_d=m_ssm_d, ssm_norm_w=m_ssm_norm_w, w_br_gdn=m_w_br_gdn, w_br_ssm=m_w_br_ssm, w_out=m_w_out, norm2_w=m_norm2_w, w_ffn_in=m_w_ffn_in, w_ffn_out=m_w_ffn_out, norm_f_w=m_norm_f_w)
    mom2 = dict(c_ctx=v_c_ctx, ada_w=v_ada_w, ada_b=v_ada_b, norm1_w=v_norm1_w, w_in=v_w_in, gdn_conv_w=v_gdn_conv_w, gdn_conv_b=v_gdn_conv_b, gdn_a_log=v_gdn_a_log, gdn_dt_bias=v_gdn_dt_bias, gdn_norm_w=v_gdn_norm_w, ssm_conv_w=v_ssm_conv_w, ssm_conv_b=v_ssm_conv_b, ssm_a_log=v_ssm_a_log, ssm_dt_bias=v_ssm_dt_bias, ssm_d=v_ssm_d, ssm_norm_w=v_ssm_norm_w, w_br_gdn=v_w_br_gdn, w_br_ssm=v_w_br_ssm, w_out=v_w_out, norm2_w=v_norm2_w, w_ffn_in=v_w_ffn_in, w_ffn_out=v_w_ffn_out, norm_f_w=v_norm_f_w)
    order = list(wts)

    seq = x.shape[1]
    t = TM + seq
    ntl, nlt, nctx = t // TM, seq // TM, TM // CH

    me_i = 4 * lax.axis_index("x") + 2 * lax.axis_index("y") + lax.axis_index("c")
    conv_sh = _pack([wts[n] for n in CONV], rows=8)
    gathered = _exchange("ag_weights", _AllGather([wts[n][0].astype(BF16) for n in BIG_FIRST] + [conv_sh]))
    full = {n: _full_from_blocks(blk, n in BIG_COL_SHARDED) for n, blk in zip(BIG_FIRST, gathered)}
    late_gather = _AllGather([wts[n][0].astype(BF16) for n in BIG_LATE])
    conv_g = gathered[len(BIG_FIRST)].reshape(N_DEV, -1)
    ncv = 3 * CONV_SHARD
    for i, n in enumerate(CONV):
        off = -(-ncv // PACK_W) * PACK_W * i
        full[n] = jnp.transpose(conv_g[:, off:off + ncv].reshape(N_DEV, 3, CONV_SHARD), (1, 0, 2)).reshape(3, XBC)
    w_cat = _w_cat(full["w_in"])
    gcw = full["gdn_conv_w"].reshape(3, 1, XBC)
    scw = full["ssm_conv_w"].reshape(3, 1, XBC)

    n1w, n2w, nfw = norm1_w.reshape(1, D), norm2_w.reshape(1, D), norm_f_w.reshape(1, D)
    gcb, scb = gdn_conv_b.reshape(1, XBC), ssm_conv_b.reshape(1, XBC)
    alog16, dtb16 = _pad_row(gdn_a_log), _pad_row(gdn_dt_bias)
    alog64, dtb64 = _pad_row(ssm_a_log), _pad_row(ssm_dt_bias)
    gnw = gdn_norm_w.reshape(1, DK)
    ssd8 = jnp.tile(_pad_row(ssm_d), (8, 1))
    snw = ssm_norm_w.reshape(1, S_INNER)
    x2 = x[0]
    tgt = loss_target[0]
    cvec = jnp.concatenate([c, c_ctx.reshape(1, D), jnp.zeros((14, D), F32)], axis=0)

    a16 = _rowwise("silu_c", f_silu_rows, [(cvec, D, 0, 0)], [], [(D, BF16)], 1, tm=16)[0]
    mod = _mm("mm_mod", a16, full["ada_w"], "nn", F32) + ada_b
    sh1, sc1, g1, sh2, sc2, g2 = [mod[0:1, i * D:(i + 1) * D] for i in range(6)]
    csh1, csc1 = mod[1:2, 0:D], mod[1:2, D:2 * D]

    pre_pars = [n1w, sc1, sh1, csc1, csh1]
    pre_rows = [(x2, D, 0, -1, True), (ctx[0], D, 0, 0, "first")]
    a = _rowwise("pre", f_pre, pre_rows, pre_pars, [(D, BF16)], ntl, base=0)[0]
    proj = _mm("mm_proj", a, w_cat, "nn", F32, tm=1408, tn=1280)
    gp_rows = [(proj, XBC, C_QKV // XBC, 0), (proj, 128, C_AB // 128, 0)]
    gp_pars = [gcw, gcb, alog16, dtb16]
    q, k, v, gcum, beta = _rowwise("gdnprep", f_gdnprep, gp_rows, gp_pars, [(D, F32)] * 3 + [(128, F32)] * 2, ntl, base=0)
    sp_rows = [(proj, XBC, C_XBC // XBC, 0), (proj, 128, C_DT // 128, 0)]
    sp_pars = [scw, scb, alog64, dtb64]
    xs, bm, cm, dtc, acc = _rowwise(
        "ssmprep", f_ssmprep, sp_rows, sp_pars, [(S_INNER, F32), (512, F32), (512, F32), (128, F32), (128, F32)], ntl, base=0)
    o0, ss0, ts0, gathered = _gdn_scan_fwd("gdn_fwd0", q, k, v, gcum, beta, 0, nctx, ride=late_gather)
    full.update({n: _full_from_blocks(blk, n in BIG_COL_SHARDED) for n, blk in zip(BIG_LATE, gathered)})
    o_sum, ss1, ts1 = _gdn_scan_fwd("gdn_fwd1", q, k, v, gcum, beta, 1, nctx, add=o0)
    y0, hs0 = _ssd_scan_fwd("ssd_fwd0", xs, dtc, acc, bm, cm, 0, nctx)
    y_sum, hs1 = _ssd_scan_fwd("ssd_fwd1", xs, dtc, acc, bm, cm, 1, nctx, add=y0)
    post_rows = [(o_sum, D, 0, 1), (proj, D, C_ZG // D, 1), (y_sum, S_INNER, 0, 1), (xs, S_INNER, 0, 1),
                 (proj, S_INNER, C_ZS // S_INNER, 1)]
    post_pars = [gnw, ssd8, snw]
    og, ys = _rowwise("post", f_post, post_rows, post_pars, [(D, BF16), (S_INNER, BF16)], nlt)
    pg = _mm("mm_pg", og, full["w_br_gdn"], "nn", F32)
    ps = _mm("mm_ps", ys, full["w_br_ssm"], "nn", F32, tk=2048)
    merge_rows = [(proj, S_INNER, C_GATE // S_INNER, 1), (pg, D, 0, 0), (ps, D, 0, 0)]
    merged = _rowwise("merge", f_merge, merge_rows, [], [(D, BF16)], nlt)[0]
    mix = _mm("mm_mix", merged, full["w_out"], "nn", F32)
    res_rows = [(x2, D, 0, 0), (mix, D, 0, 0)]
    res_pars = [g1, n2w, sc2, sh2]
    h1, f = _rowwise("res1", f_res1, res_rows, res_pars, [(D, F32), (D, BF16)], nlt)
    u = _mm("mm_u", f, full["w_ffn_in"], "nn", F32, tn=1408)
    hact = _rowwise("act", f_act, [(u, 2 * D_FF, 0, 0)], [], [(D_FF, BF16)], nlt)[0]
    ff = _mm("mm_ff", hact, full["w_ffn_out"], "nn", F32, tk=2816)

    fin_rows = [(h1, D, 0, 0), (ff, D, 0, 0), (tgt, D, 0, 0)]
    d_h1a, d_ff, d_g2, d_nfw, loss_acc = _rowwise_bwd(
        "final", f_final, fin_rows, [g2, nfw], ["one"], [(0, F32), (1, BF16)], [0, 1], nlt, loss_out=True)
    d_hact = _mm("mm_dhact", d_ff, full["w_ffn_out"], "nt", BF16, tn=1408)
    g_w_ffn_out = _mm("mm_gwffo", hact, d_ff, "tn", BF16, tm=1408, tk=2048)
    d_u = _rowwise_bwd("act_bwd", f_act, [(u, 2 * D_FF, 0, 0)], [], [[(d_hact, D_FF, 0, 0)]], [(0, BF16)], [], nlt)[0]
    d_f = _mm("mm_df", d_u, full["w_ffn_in"], "nt", BF16, tk=2816)
    g_w_ffn_in = _mm("mm_gwffi", f, d_u, "tn", BF16, tn=1408, tk=2048)
    d_xres, d_mix, d_g1, d_n2w, d_sc2, d_sh2 = _rowwise_bwd(
        "res1_bwd", f_res1, res_rows, res_pars, [[(d_h1a, D, 0, 0)], [(d_f, D, 0, 0)]], [(0, F32), (1, BF16)], [0, 1, 2, 3], nlt)
    d_merged = _mm("mm_dmerged", d_mix, full["w_out"], "nt", BF16)
    g_w_out = _mm("mm_gwout", merged, d_mix, "tn", BF16, tk=2048)
    d_gate, d_pg, d_ps = _rowwise_bwd(
        "merge_bwd", f_merge, merge_rows, [], [[(d_merged, D, 0, 0)]], [(0, BF16), (1, BF16), (2, BF16)], [], nlt)
    d_og = _mm("mm_dog", d_pg, full["w_br_gdn"], "nt", BF16)
    g_w_br_gdn = _mm("mm_gwbrg", og, d_pg, "tn", BF16, tk=2048)
    d_ys = _mm("mm_dys", d_ps, full["w_br_ssm"], "nt", BF16, tn=2048)
    g_w_br_ssm = _mm("mm_gwbrs", ys, d_ps, "tn", BF16, tk=2048)
    d_o, d_zg, d_y, d_xs_post, d_zs, d_gnw, d_ssd8, d_snw = _rowwise_bwd(
        "post_bwd", f_post, post_rows, post_pars, [[(d_og, D, 0, 0)], [(d_ys, S_INNER, 0, 0)]],
        [(0, F32), (1, BF16), (2, F32), (3, F32), (4, BF16)], [0, 1, 2], nlt)
    late_grads = dict(w_br_gdn=g_w_br_gdn, w_br_ssm=g_w_br_ssm, w_out=g_w_out, w_ffn_in=g_w_ffn_in, w_ffn_out=g_w_ffn_out)
    late_a2a = _AllToAll([_blocks_from_full(late_grads[n], n in BIG_COL_SHARDED) for n in BIG_LATE])
    dq0, dk0, dv0, dg0, db0, recv_late = _gdn_scan_bwd("gdn_bwd0", q, k, v, gcum, beta, ss0, ts0, d_o, 0, nctx, ride=late_a2a)
    dq1, dk1, dv1, dg1, db1 = _gdn_scan_bwd("gdn_bwd1", q, k, v, gcum, beta, ss1, ts1, d_o, 1, nctx)
    dxs0, ddt0, dac0, dbm0, dcm0 = _ssd_scan_bwd("ssd_bwd0", xs, dtc, acc, bm, cm, hs0, d_y, 0, nctx)
    dxs1, ddt1, dac1, dbm1, dcm1 = _ssd_scan_bwd("ssd_bwd1", xs, dtc, acc, bm, cm, hs1, d_y, 1, nctx)
    row = lambda arr, wd: (arr, wd, 0, 0)
    d_qkv_raw, d_ab, d_gcw, d_gcb, d_alog16, d_dtb16 = _rowwise_bwd(
        "gdnprep_bwd", f_gdnprep, gp_rows, gp_pars,
        [[row(dq0, D), row(dq1, D)], [row(dk0, D), row(dk1, D)], [row(dv0, D), row(dv1, D)],
         [row(dg0, 128), row(dg1, 128)], [row(db0, 128), row(db1, 128)]],
        [(0, BF16), (1, BF16)], [0, 1, 2, 3], ntl, base=0)
    d_xbc_raw, d_dt, d_scw, d_scb, d_alog64, d_dtb64 = _rowwise_bwd(
        "ssmprep_bwd", f_ssmprep, sp_rows, sp_pars,
        [[row(dxs0, S_INNER), row(dxs1, S_INNER), (d_xs_post, S_INNER, 0, -1, True)], [row(dbm0, 512), row(dbm1, 512)],
         [row(dcm0, 512), row(dcm1, 512)], [row(ddt0, 128), row(ddt1, 128)], [row(dac0, 128), row(dac1, 128)]],
        [(0, BF16), (1, BF16)], [0, 1, 2, 3], ntl, base=0)
    ctx_zero = lambda wd: jnp.zeros((TM, wd), BF16)
    d_proj = jnp.concatenate([
        d_qkv_raw, d_xbc_raw, jnp.concatenate([ctx_zero(S_INNER), d_zs], axis=0),
        jnp.concatenate([ctx_zero(S_INNER), d_gate], axis=0), jnp.concatenate([ctx_zero(D), d_zg], axis=0), d_ab, d_dt], axis=1)
    g_w_cat = _mm("mm_gwcat", a, d_proj, "tn", BF16, tn=768, tk=2816)
    w_in_a2a = _AllToAll([_blocks_from_full(_w_uncat(g_w_cat), True)])
    d_a, recv_w_in = _mm("mm_da", d_proj, w_cat, "nt", BF16, tk=3840, ride=w_in_a2a)
    d_x, d_n1w, d_sc1, d_sh1, d_csc1, d_csh1 = _rowwise_bwd(
        "pre_bwd", f_pre_thru, pre_rows, pre_pars, [[row(d_a, D)], [(d_xres, D, 0, -1, True)]],
        [(0, F32)], [0, 1, 2, 3, 4], ntl, base=0)
    zero4 = jnp.zeros((1, 4 * D), F32)
    d_mod = jnp.concatenate([
        jnp.concatenate([d_sh1, d_sc1, d_g1, d_sh2, d_sc2, d_g2], axis=1),
        jnp.concatenate([d_csh1, d_csc1, zero4], axis=1), jnp.zeros((14, 6 * D), F32)], axis=0)
    d_a16 = _mm("mm_da16", d_mod, full["ada_w"], "nt", F32)
    d_cvec = _rowwise_bwd("silu_c_bwd", f_silu_rows, [(cvec, D, 0, 0)], [], [[row(d_a16, D)]], [(0, F32)], [], 1, tm=16)[0]

    recv = dict(zip(BIG_LATE, recv_late), w_in=recv_w_in[0])
    big_un = {n: _reduce_adam("adam_" + n, recv[n], wts[n][0], mom1[n][0], mom2[n][0], BIG_ADAM_ROWS[n])
              for n in BIG if n != "ada_w"}

    small_g = dict(c_ctx=d_cvec[1], ada_b=d_mod[0] + d_mod[1], norm1_w=d_n1w, gdn_conv_b=d_gcb,
                   gdn_a_log=d_alog16[0, :2 * G_HEADS], gdn_dt_bias=d_dtb16[0, :2 * G_HEADS], gdn_norm_w=d_gnw,
                   ssm_conv_b=d_scb, ssm_a_log=d_alog64[0, :2 * S_HEADS], ssm_dt_bias=d_dtb64[0, :2 * S_HEADS],
                   ssm_d=d_ssd8[0, :S_HEADS], ssm_norm_w=d_snw, norm2_w=d_n2w, norm_f_w=d_nfw,
                   gdn_conv_w=d_gcw.reshape(3, XBC), ssm_conv_w=d_scw.reshape(3, XBC))
    small_names = SMALL + CONV
    factors = [a16[0].astype(F32), d_mod[0], d_mod[1]]
    round8 = lambda r: -(-r // 8) * 8
    packed_rows = lambda arrs: sum(-(-arr.size // PACK_W) for arr in arrs)
    n_small = packed_rows([small_g[n] for n in small_names])
    n_fac = packed_rows(factors)
    rows_small = round8(n_small)
    sg_pack = _pack([small_g[n] for n in small_names] + factors, rows=round8(n_small + n_fac))
    recv_all = _exchange("ag_small_grads", _AllGather([sg_pack]))[0]
    recv_s = recv_all[:, :rows_small]
    fac = recv_all[:, n_small:n_small + n_fac].reshape(N_DEV, -1)
    my_cols = lambda z: lax.dynamic_slice(z, (0, me_i * (6 * D // N_DEV)), (N_DEV, 6 * D // N_DEV))
    lhs = jnp.concatenate([fac[:, :D], jnp.broadcast_to(a16[1:2].astype(F32), (N_DEV, D))], axis=0)
    rhs = jnp.concatenate([my_cols(fac[:, D:7 * D]), my_cols(fac[:, 7 * D:])], axis=0)
    g_ada_w = _mm("mm_gwada", lhs, rhs, "tn", F32)
    big_un["ada_w"] = _reduce_adam("adam_ada_w", g_ada_w[None], wts["ada_w"][0], mom1["ada_w"][0], mom2["ada_w"][0],
                                   BIG_ADAM_ROWS["ada_w"])

    def placed(src, n):
        if n not in CONV:
            return src[n]
        return lax.dynamic_update_slice(jnp.zeros((3, XBC), F32), src[n][0], (0, me_i * CONV_SHARD))

    small_out = _reduce_adam("adam_small", recv_s, *[_pack([placed(src, n) for n in small_names], rows=rows_small)
                                                     for src in (wts, mom1, mom2)], rows_small)
    small_shapes = [wts[n].shape if n in SMALL else (3, XBC) for n in small_names]
    small_un = [_unpack(buf, small_shapes) for buf in small_out]

    res = [{}, {}, {}, {}]
    for kind in range(4):
        for n in BIG:
            res[kind][n] = big_un[n][kind].reshape(wts[n].shape)
        for n, val in zip(small_names, small_un[kind]):
            if n in CONV:
                val = lax.dynamic_slice(val, (0, me_i * CONV_SHARD), (3, CONV_SHARD)).reshape(wts[n].shape)
            res[kind][n] = val
    loss = lax.psum(loss_acc[0, 0], ("x", "y", "c"))
    grad_x = d_x.reshape(x.shape)
    return (loss, grad_x, *[res[0][n] for n in order], *[res[1][n] for n in order], *[res[2][n] for n in order],
            *[res[3][n] for n in order])
```

```python
import functools

import jax
import jax.numpy as jnp
from jax import lax
from jax.experimental import pallas as pl
from jax.experimental.pallas import tpu as pltpu

F32 = jnp.float32
BF16 = jnp.bfloat16
HI = lax.Precision.HIGHEST
HIGH = lax.Precision.HIGH
MESH = pl.DeviceIdType.MESH

D = 1024
CH = 64
TM = 256
EPS = 1e-6
NEG = -1e30
G_HEADS = 8
DK = 128
S_HEADS = 32
S_P = 64
S_GROUPS = 4
S_N = 128
S_INNER = 2048
XBC = 3072
D_FF = 2816
N_DEV = 8
PACK_W = 1024
VMEM_LIMIT = 56 * 1024 * 1024

ADAM_LR = 0.001
ADAM_B1 = 0.9
ADAM_B2 = 0.999
ADAM_EPS = 1e-08
ADAM_WD = 0.01
ADAM_STEP = 10

C_QKV, C_XBC, C_ZS, C_GATE, C_ZG, C_AB, C_DT, C_END = 0, 3072, 6144, 8192, 10240, 11264, 11392, 11520
O_QKV, O_ZG, O_AB, O_ZS, O_XBC, O_DT, O_GATE, O_END = 0, 3072, 4096, 4128, 6176, 9248, 9312, 11360


def _dot(a, b, prec=None):
    return jnp.dot(a, b, precision=prec, preferred_element_type=F32)


def _dot_nt(a, b, prec=None):
    return lax.dot_general(a, b, (((1,), (1,)), ((), ())), precision=prec, preferred_element_type=F32)


def _dot_tn(a, b, prec=None):
    return lax.dot_general(a, b, (((0,), (0,)), ((), ())), precision=prec, preferred_element_type=F32)


def _iota(shape, dim):
    return lax.broadcasted_iota(jnp.int32, shape, dim)


def _rms(x):
    return x * lax.rsqrt(jnp.mean(x * x, axis=-1, keepdims=True) + EPS)


def _l2n(x):
    return x * lax.rsqrt(jnp.sum(x * x, axis=-1, keepdims=True) + EPS)


def _silu(x):
    return x * jax.nn.sigmoid(x)


def _softplus(x):
    return jnp.maximum(x, 0.0) + jnp.log1p(jnp.exp(-jnp.abs(x)))


def _roll_rows(x, s):
    return pltpu.roll(x, s, 0)


def _up_raw(x, keep_up):
    return jnp.where(keep_up > 0.0, _roll_rows(x, 1), 0.0)


def _dn_raw(x, keep_dn):
    return jnp.where(keep_dn > 0.0, _roll_rows(x, x.shape[0] - 1), 0.0)


@jax.custom_vjp
def _shift_up(x, keep_up, keep_dn):
    return _up_raw(x, keep_up)


def _shift_up_fwd(x, keep_up, keep_dn):
    return _up_raw(x, keep_up), (keep_up, keep_dn)


def _shift_up_bwd(res, g):
    keep_up, keep_dn = res
    return _dn_raw(g, keep_dn), jnp.zeros_like(keep_up), jnp.zeros_like(keep_dn)


_shift_up.defvjp(_shift_up_fwd, _shift_up_bwd)


@jax.custom_vjp
def _shift_dn(x, keep_up, keep_dn):
    return _dn_raw(x, keep_dn)


def _shift_dn_fwd(x, keep_up, keep_dn):
    return _dn_raw(x, keep_dn), (keep_up, keep_dn)


def _shift_dn_bwd(res, g):
    keep_up, keep_dn = res
    return _up_raw(g, keep_up), jnp.zeros_like(keep_up), jnp.zeros_like(keep_dn)


_shift_dn.defvjp(_shift_dn_fwd, _shift_dn_bwd)


def _conv_keep(is_ctx, n):
    r = _iota((n, 1), 0)
    pos = jnp.where(is_ctx, r, r & (CH - 1))
    end = jnp.where(is_ctx, n - 1, CH - 1)
    return jnp.where(pos == 0, 0.0, 1.0).astype(F32), jnp.where(pos == end, 0.0, 1.0).astype(F32)


def _conv_silu(u, w3, b, keep_up, keep_dn):
    conv = b + _shift_up(u, keep_up, keep_dn) * w3[0] + u * w3[1] + _shift_dn(u, keep_up, keep_dn) * w3[2]
    return _silu(conv)


def _chunk_tri(n, rev):
    i = _iota((n, n), 0)
    j = _iota((n, n), 1)
    same = (i // CH) == (j // CH)
    seen = (i <= j) if rev else (i >= j)
    return jnp.where(same & seen, 1.0, 0.0).astype(F32)


def _expand_mat(rows, cols, per, base):
    r = _iota((rows, cols), 0)
    c = _iota((rows, cols), 1)
    return jnp.where(r == base + c // per, 1.0, 0.0).astype(F32)


def f_silu_rows(is_ctx, cvec):
    return (_silu(cvec).astype(BF16),)


def f_pre(is_ctx, x, xc, n1w, sc, sh, csc, csh):
    x = jnp.where(is_ctx, xc, x)
    sc_e = jnp.where(is_ctx, csc, sc)
    sh_e = jnp.where(is_ctx, csh, sh)
    a = _rms(x) * n1w * (1.0 + sc_e) + sh_e
    return (a.astype(BF16),)


def f_pre_thru(is_ctx, x, xc, n1w, sc, sh, csc, csh):
    return f_pre(is_ctx, x, xc, n1w, sc, sh, csc, csh)[0], x


def f_gdnprep(is_ctx, qkv_raw, ab_raw, cw, cb, alog, dtb):
    n = qkv_raw.shape[0]
    keep_up, keep_dn = _conv_keep(is_ctx, n)
    s = _conv_silu(qkv_raw, cw, cb, keep_up, keep_dn)
    qs, ks, vs = [], [], []
    for h in range(G_HEADS):
        qs.append(_l2n(s[:, h * DK:(h + 1) * DK]) * (DK ** -0.5))
        ks.append(_l2n(s[:, D + h * DK:D + (h + 1) * DK]))
    q = jnp.concatenate(qs, axis=1)
    k = jnp.concatenate(ks, axis=1)
    v = s[:, 2 * D:3 * D]
    lane = _iota(ab_raw.shape, 1)
    g = jnp.where(lane < 2 * G_HEADS, -jnp.exp(alog) * _softplus(ab_raw + dtb), 0.0)
    gcum = jnp.where(lane < G_HEADS, _dot(_chunk_tri(n, False), g, HI), _dot(_chunk_tri(n, True), g, HI))
    beta = jax.nn.sigmoid(ab_raw)
    return q, k, v, gcum, beta


def f_ssmprep(is_ctx, xbc_raw, dt_raw, cw, cb, alog, dtb):
    n = xbc_raw.shape[0]
    keep_up, keep_dn = _conv_keep(is_ctx, n)
    s = _conv_silu(xbc_raw, cw, cb, keep_up, keep_dn)
    xs = s[:, :S_INNER]
    bm = s[:, S_INNER:S_INNER + S_GROUPS * S_N]
    cm = s[:, S_INNER + S_GROUPS * S_N:]
    lane = _iota(dt_raw.shape, 1)
    dt = jnp.where(lane < 2 * S_HEADS, _softplus(dt_raw + dtb), 0.0)
    da = dt * (-jnp.exp(alog))
    acum = jnp.where(lane < S_HEADS, _dot(_chunk_tri(n, False), da, HI), _dot(_chunk_tri(n, True), da, HI))
    return xs, bm, cm, dt, acum


def f_post(is_ctx, o, zg, y_scan, xs, zs, gnw, ssd8, snw):
    ogs = []
    for h in range(G_HEADS):
        sl = slice(h * DK, (h + 1) * DK)
        ogs.append(_rms(o[:, sl]) * gnw * _silu(zg[:, sl]))
    og = jnp.concatenate(ogs, axis=1)
    row0 = jnp.where(_iota(ssd8.shape, 0) == 0, 1.0, 0.0).astype(F32)
    dexp = jnp.sum(_dot(ssd8 * row0, _expand_mat(128, S_INNER, S_P, 0), HI), axis=0, keepdims=True)
    y = (y_scan + dexp * xs) * _silu(zs)
    gw = S_INNER // S_GROUPS
    ys = jnp.concatenate([_rms(y[:, i * gw:(i + 1) * gw]) * snw[:, i * gw:(i + 1) * gw] for i in range(S_GROUPS)], axis=1)
    return og.astype(BF16), ys.astype(BF16)


def f_merge(is_ctx, gate, pg, ps):
    m = jax.nn.sigmoid(gate[:, :D]) * pg + jax.nn.sigmoid(gate[:, D:]) * ps
    return (m.astype(BF16),)


def f_res1(is_ctx, x, mix, g1, n2w, sc2, sh2):
    h1 = x + g1 * mix
    f = _rms(h1) * n2w * (1.0 + sc2) + sh2
    return h1, f.astype(BF16)


def f_act(is_ctx, u):
    return ((_silu(u[:, :D_FF]) * u[:, D_FF:]).astype(BF16),)


def f_final(is_ctx, h1, ff, tgt, g2, nfw):
    h2 = h1 + g2 * ff
    y = _rms(h2) * nfw
    err = y - tgt
    return (0.5 * jnp.sum(jnp.mean(err * err, axis=-1, keepdims=True), axis=0, keepdims=True),)


def _bf16_dot(kind):
    fwd_op = {"nn": _dot, "nt": _dot_nt, "tn": _dot_tn}[kind]

    @jax.custom_vjp
    def f(a, b):
        return fwd_op(a.astype(BF16), b.astype(BF16))

    def f_fwd(a, b):
        ab, bb = a.astype(BF16), b.astype(BF16)
        return fwd_op(ab, bb), (ab, bb)

    def f_bwd(res, g):
        ab, bb = res
        gb = g.astype(BF16)
        if kind == "nn":
            return _dot_nt(gb, bb), _dot_tn(ab, gb)
        if kind == "nt":
            return _dot(gb, bb), _dot_tn(gb, ab)
        return _dot_nt(bb, gb), _dot(ab, gb)

    f.defvjp(f_fwd, f_bwd)
    return f


_bdot, _bdot_nt, _bdot_tn = _bf16_dot("nn"), _bf16_dot("nt"), _bf16_dot("tn")


def _each(fn, *lists):
    return [fn(*args) for args in zip(*lists)]


def _tri_inverse_all(mats):
    n = mats[0].shape[0]
    eye = jnp.where(_iota((n, n), 0) == _iota((n, n), 1), 1.0, 0.0).astype(F32)
    t = [eye - a for a in mats]
    p = [_dot(a, a, HIGH) for a in mats]
    for r in range(5):
        t = _each(lambda t_, p_: t_ + _dot(t_, p_, HIGH), t, p)
        if r < 4:
            p = [_dot(p_, p_, HIGH) for p_ in p]
    return t


@jax.custom_vjp
def _inverse_given(a, t):
    return t


def _inverse_given_fwd(a, t):
    return t, t


def _inverse_given_bwd(t, g):
    return -_dot_nt(_dot_tn(t, g), t), jnp.zeros_like(t)


_inverse_given.defvjp(_inverse_given_fwd, _inverse_given_bwd)


@jax.custom_vjp
def _dot_high(a, b):
    return _dot(a, b, HIGH)


def _dot_high_fwd(a, b):
    return _dot(a, b, HIGH), (a, b)


def _dot_high_bwd(res, g):
    a, b = res
    return _dot_nt(g, b), _dot_tn(a, g)


_dot_high.defvjp(_dot_high_fwd, _dot_high_bwd)


def gdn_local(qs, ks, vs, gcs, grs, bcs, rev, t_known=None):
    c = qs[0].shape[0]
    ii = _iota((c, c), 0)
    jj = _iota((c, c), 1)
    incl = (ii <= jj) if rev else (ii >= jj)
    strict = (ii < jj) if rev else (ii > jj)
    decay = _each(lambda gc, gr: jnp.exp(jnp.where(incl, gc - gr, NEG)), gcs, grs)
    kb = _each(lambda k, bc: k * bc, ks, bcs)
    a = _each(lambda kb_, k, dc: jnp.where(strict, _bdot_nt(kb_, k) * dc, 0.0), kb, ks, decay)
    t = _tri_inverse_all(a) if t_known is None else _each(_inverse_given, a, t_known)
    eg = [jnp.exp(gc) for gc in gcs]
    rhs = _each(lambda kb_, eg_, v, bc: jnp.concatenate([kb_ * eg_, v * bc], axis=1), kb, eg, vs, bcs)
    wu = _each(_dot_high, t, rhs)
    lhs = _each(lambda wu_, q, eg_: jnp.concatenate([wu_[:, :DK], q * eg_], axis=0), wu, qs, eg)
    attn = _each(lambda q, k, dc: _bdot_nt(q, k) * dc, qs, ks, decay)
    return wu, attn, lhs, t


def gdn_state(ss, wu, attn, lhs, ks, gcs, rev):
    c = ks[0].shape[0]
    is_last = _iota((c, 1), 0) == (0 if rev else c - 1)
    ws = _each(_bdot, lhs, ss)
    v_new = _each(lambda wu_, ws_: wu_[:, DK:] - ws_[:c], wu, ws)
    o = _each(lambda ws_, at, vn: ws_[c:] + _bdot(at, vn), ws, attn, v_new)
    gtot = [jnp.sum(jnp.where(is_last, gc, 0.0), axis=0, keepdims=True) for gc in gcs]
    s_new = _each(lambda s, k, gc, gt_, vn: s * jnp.exp(gt_) + _bdot_tn(k * jnp.exp(gt_ - gc), vn), ss, ks, gcs, gtot, v_new)
    return s_new, o


def gdn_chunk(ss, qs, ks, vs, gcs, grs, bcs, rev, t_known=None):
    wu, attn, lhs, t = gdn_local(qs, ks, vs, gcs, grs, bcs, rev, t_known)
    s_new, o = gdn_state(ss, wu, attn, lhs, ks, gcs, rev)
    return s_new, o, t


def ssd_pick(dt0s, dt1s, ac0s, ac1s, ar0s, ar1s):
    lo = _iota((CH, 128), 1) < S_P
    pick = lambda u0, u1: jnp.where(lo, u0, u1)
    return _each(pick, dt0s, dt1s), _each(pick, ac0s, ac1s), _each(pick, ar0s, ar1s)


def ssd_local(xs, dts, acs, acr, bgs, cgs, rev):
    c = xs[0].shape[0]
    npair = len(xs)
    grp = [p * len(bgs) // npair for p in range(npair)]
    lane = _iota((c, 128), 1)
    ii = _iota((c, 128), 0)
    jl = lane & (S_P - 1)
    lo = lane < S_P
    seen = (ii <= jl) if rev else (ii >= jl)
    last = 0 if rev else c - 1
    split = lambda z: jnp.concatenate([jnp.where(lo, z, 0.0), jnp.where(lo, 0.0, z)], axis=0)
    cb = _each(lambda bg, cg: _bdot_nt(cg, jnp.concatenate([bg, bg], axis=0)), bgs, cgs)
    seg = _each(lambda ac, ar: jnp.exp(jnp.where(seen, ac - ar, NEG)), acs, acr)
    xdt = _each(lambda x, dt: x * dt, xs, dts)
    ydiag = [_bdot(cb[grp[p]] * seg[p], split(xdt[p])) for p in range(npair)]
    eac = [jnp.exp(ac) for ac in acs]
    atot = [jnp.sum(jnp.where(ii == last, ac, 0.0), axis=0, keepdims=True) for ac in acs]
    upd = [_bdot_tn(bgs[grp[p]], xdt[p] * jnp.exp(atot[p] - acs[p])) for p in range(npair)]
    return ydiag, eac, [jnp.exp(at) for at in atot], upd


def ssd_state(hts, ydiag, eac, etot, upd, cgs):
    npair = len(hts)
    grp = [p * len(cgs) // npair for p in range(npair)]
    y = [ydiag[p] + _bdot(cgs[grp[p]], hts[p]) * eac[p] for p in range(npair)]
    h_new = [hts[p] * etot[p] + upd[p] for p in range(npair)]
    return h_new, y


def ssd_chunk(hts, xs, dts, acs, acr, bgs, cgs, rev):
    ydiag, eac, etot, upd = ssd_local(xs, dts, acs, acr, bgs, cgs, rev)
    return ssd_state(hts, ydiag, eac, etot, upd, cgs)


def f_adamw(w, g, m, v):
    m = ADAM_B1 * m + (1.0 - ADAM_B1) * g
    v = ADAM_B2 * v + (1.0 - ADAM_B2) * jnp.square(g)
    m_hat = m / (1.0 - ADAM_B1 ** ADAM_STEP)
    v_hat = v / (1.0 - ADAM_B2 ** ADAM_STEP)
    delta = -ADAM_LR * (m_hat / (jnp.sqrt(v_hat) + ADAM_EPS) + ADAM_WD * w)
    return delta, m, v


def _cparams(sem):
    return pltpu.CompilerParams(dimension_semantics=sem, vmem_limit_bytes=VMEM_LIMIT)


def _pick(n, target):
    if n <= target:
        return n
    best = None
    for t in range(128, target + 1, 128):
        if n % t == 0:
            best = t
    assert best is not None, (n, target)
    return best


def _row_spec(tm, width, colblk, rowoff):
    return pl.BlockSpec((tm, width), lambda i: (i + rowoff, colblk))


def _par_spec(shape):
    nd = len(shape)
    return pl.BlockSpec(tuple(shape), lambda i: (0,) * nd)


def _rowwise(name, fn, rows, pars, outs, ntiles, base=1, tm=TM):
    nr, npar = len(rows), len(pars)

    def body(*refs):
        is_ctx = (pl.program_id(0) + base) == 0
        res = fn(is_ctx, *[r[...] for r in refs[:nr]], *[p[...] for p in refs[nr:nr + npar]])
        for o_ref, r in zip(refs[nr + npar:], res):
            o_ref[...] = r.astype(o_ref.dtype)

    return pl.pallas_call(
        body, grid=(ntiles,), name=name,
        in_specs=[_ct_spec(tm, d) for d in rows] + [_par_spec(p.shape) for p in pars],
        out_specs=[_row_spec(tm, wd, 0, 0) for (wd, _) in outs],
        out_shape=[jax.ShapeDtypeStruct((ntiles * tm, wd), dt) for (wd, dt) in outs],
        compiler_params=_cparams(("arbitrary",)),
    )(*[r[0] for r in rows], *pars)


def _ct_spec(tm, desc):
    _, wd, cb, ro = desc[:4]
    flag = desc[4] if len(desc) > 4 else False
    if flag == "first":
        return pl.BlockSpec((tm, wd), lambda i: (0, cb))
    if flag:
        return pl.BlockSpec((tm, wd), lambda i: (jnp.maximum(i + ro, 0), cb))
    return _row_spec(tm, wd, cb, ro)


def _rowwise_bwd(name, fn, rows, pars, cts, drows, dpars, ntiles, base=1, loss_out=False, tm=TM):
    nr, npar = len(rows), len(pars)
    ct_rows = [d for ct in cts if isinstance(ct, list) for d in ct]
    nct = len(ct_rows)

    def body(*refs):
        i = pl.program_id(0)
        is_ctx = (i + base) == 0
        rows_v = [r[...] for r in refs[:nr]]
        pars_v = [p[...] for p in refs[nr:nr + npar]]
        ct_refs = list(refs[nr + npar:nr + npar + nct])
        out_refs = list(refs[nr + npar + nct:])
        outs, vjp = jax.vjp(lambda rv, pv: fn(is_ctx, *rv, *pv), rows_v, pars_v)

        def ct_value(desc):
            val = ct_refs.pop(0)[...].astype(F32)
            if len(desc) > 4 and desc[4]:
                val = jnp.where(is_ctx, 0.0, val)
            return val

        ct_vals = []
        for o, ct in zip(outs, cts):
            if ct is None:
                ct_vals.append(jnp.zeros_like(o))
            elif isinstance(ct, str):
                ct_vals.append(jnp.ones_like(o))
            else:
                acc = ct_value(ct[0])
                for desc in ct[1:]:
                    acc = acc + ct_value(desc)
                ct_vals.append(acc.astype(o.dtype))
        d_rows, d_pars = vjp(tuple(ct_vals))
        for (ri, _), o_ref in zip(drows, out_refs[:len(drows)]):
            o_ref[...] = d_rows[ri].astype(o_ref.dtype)
        acc_refs = out_refs[len(drows):]
        acc_vals = [d_pars[pi] for pi in dpars]
        if loss_out:
            acc_vals.append(jnp.broadcast_to(outs[0], (8, 128)))

        @pl.when(i == 0)
        def _():
            for o_ref, val in zip(acc_refs, acc_vals):
                o_ref[...] = val

        @pl.when(i > 0)
        def _():
            for o_ref, val in zip(acc_refs, acc_vals):
                o_ref[...] += val

    acc_shapes = [pars[pi].shape for pi in dpars] + ([(8, 128)] if loss_out else [])
    lat_row = lambda ri: len(rows[ri]) > 4 and rows[ri][4] is True
    return pl.pallas_call(
        body, grid=(ntiles,), name=name,
        in_specs=[_ct_spec(tm, d) for d in rows] + [_par_spec(p.shape) for p in pars]
        + [_ct_spec(tm, d) for d in ct_rows],
        out_specs=[_ct_spec(tm, (None, rows[ri][1], 0, -1, True) if lat_row(ri) else (None, rows[ri][1], 0, 0))
                   for (ri, _) in drows] + [_par_spec(s) for s in acc_shapes],
        out_shape=[jax.ShapeDtypeStruct(((ntiles - int(lat_row(ri))) * tm, rows[ri][1]), dt) for (ri, dt) in drows]
        + [jax.ShapeDtypeStruct(tuple(s), F32) for s in acc_shapes],
        compiler_params=_cparams(("arbitrary",)),
    )(*[r[0] for r in rows], *pars, *[r[0] for r in ct_rows])


def _mm(name, a, b, mode, out_dtype, tm=1024, tn=1024, tk=1024, ride=None):
    if mode == "nn":
        (m, kd), (_, n) = a.shape, b.shape
    elif mode == "nt":
        (m, kd), (n, _) = a.shape, b.shape
    else:
        (kd, m), (_, n) = a.shape, b.shape
    tm, tn, tk = _pick(m, tm), _pick(n, tn), _pick(kd, tk)
    nk = kd // tk
    a_spec = {"nn": pl.BlockSpec((tm, tk), lambda i, j, k: (i, k)), "nt": pl.BlockSpec((tm, tk), lambda i, j, k: (i, k)),
              "tn": pl.BlockSpec((tk, tm), lambda i, j, k: (k, i))}[mode]
    b_spec = {"nn": pl.BlockSpec((tk, tn), lambda i, j, k: (k, j)), "nt": pl.BlockSpec((tn, tk), lambda i, j, k: (j, k)),
              "tn": pl.BlockSpec((tk, tn), lambda i, j, k: (k, j))}[mode]
    dot = {"nn": _dot, "nt": _dot_nt, "tn": _dot_tn}[mode]

    if nk == 1:
        def body(a_ref, b_ref, o_ref):
            o_ref[...] = dot(a_ref[...].astype(BF16), b_ref[...].astype(BF16)).astype(o_ref.dtype)
    else:
        def body(a_ref, b_ref, o_ref, acc_ref):
            k = pl.program_id(2)
            part = dot(a_ref[...].astype(BF16), b_ref[...].astype(BF16))

            @pl.when(k == 0)
            def _():
                acc_ref[...] = part

            @pl.when((k > 0) & (k < nk - 1))
            def _():
                acc_ref[...] += part

            @pl.when(k == nk - 1)
            def _():
                o_ref[...] = (acc_ref[...] + part).astype(o_ref.dtype)

    grid = (m // tm, n // tn, nk)
    at = lambda pos: functools.reduce(jnp.logical_and, [pl.program_id(ax) == pos(g) for ax, g in enumerate(grid)])
    r_in, r_out, r_shapes, r_scr, r_ops = _ride_args(ride)
    res = pl.pallas_call(
        _ride(body, ride, 2, 1, lambda: at(lambda g: 0), lambda: at(lambda g: g - 1)), grid=grid, name=name,
        in_specs=[a_spec, b_spec] + r_in,
        out_specs=[pl.BlockSpec((tm, tn), lambda i, j, k: (i, j))] + r_out,
        out_shape=[jax.ShapeDtypeStruct((m, n), out_dtype)] + r_shapes,
        scratch_shapes=([] if nk == 1 else [pltpu.VMEM((tm, tn), F32)]) + r_scr,
        compiler_params=_cparams(("arbitrary", "arbitrary", "arbitrary")),
    )(a, b, *r_ops)
    return res[0] if ride is None else (res[0], res[1:])


def _chunk_index(i, nch, nctx, rev):
    if not rev:
        return i
    return jnp.where(i < nctx, nctx - 1 - i, nch + nctx - 1 - i)


def _gdn_cols(d):
    return [d * G_HEADS + h for h in range(G_HEADS)], [2 * G_HEADS + d * G_HEADS + h for h in range(G_HEADS)]


def _gdn_operands(q_ref, k_ref, v_ref, g_ref, b_ref, d, rows=slice(None)):
    cols_g, cols_b = _gdn_cols(d)
    sls = [slice(h * DK, (h + 1) * DK) for h in range(G_HEADS)]
    gt, bt = g_ref[rows, :], b_ref[rows, :]
    gtt = gt.T
    qs = [q_ref[rows, sl] for sl in sls]
    ks = [k_ref[rows, sl] for sl in sls]
    vs = [v_ref[rows, sl] for sl in sls]
    gcs = [gt[:, cg:cg + 1] for cg in cols_g]
    grs = [gtt[cg:cg + 1, :] for cg in cols_g]
    bcs = [bt[:, cb:cb + 1] for cb in cols_b]
    return sls, qs, ks, vs, gcs, grs, bcs


GDN_FWD_CHUNKS = 4


def _gdn_scan_fwd(name, q, k, v, gcum, beta, d, nctx, ride=None, add=None):
    t = q.shape[0]
    nch = t // CH
    per = GDN_FWD_CHUNKS
    nst = nch // per
    rev = d == 1
    bix = lambda i: _chunk_index(i, nst, nctx // per, rev)
    full = pl.BlockSpec((per * CH, D), lambda i: (bix(i), 0))
    wide = pl.BlockSpec((per * CH, 128), lambda i: (bix(i), 0))
    order = list(range(per - 1, -1, -1)) if rev else list(range(per))

    def body(q_ref, k_ref, v_ref, g_ref, b_ref, *rest):
        add_ref = rest[0] if add is not None else None
        o_ref, ss_ref, ts_ref, s_scr = rest[-4:]

        @pl.when(pl.program_id(0) == 0)
        def _():
            s_scr[...] = jnp.zeros(s_scr.shape, F32)

        ops = [_gdn_operands(q_ref, k_ref, v_ref, g_ref, b_ref, d, slice(c * CH, (c + 1) * CH)) for c in order]
        sls = ops[0][0]
        cat = [sum((o[i] for o in ops), []) for i in range(1, 7)]
        wu, attn, lhs, tinv = gdn_local(*cat, rev)
        ss = [s_scr[h] for h in range(G_HEADS)]
        for n, c in enumerate(order):
            sl = slice(n * G_HEADS, (n + 1) * G_HEADS)
            s_new, o = gdn_state(ss, wu[sl], attn[sl], lhs[sl], cat[1][sl], cat[3][sl], rev)
            for h in range(G_HEADS):
                ss_ref[n, h] = ss[h]
                ts_ref[n, h] = tinv[n * G_HEADS + h]
                rows = slice(c * CH, (c + 1) * CH)
                o_ref[rows, sls[h]] = o[h] if add is None else o[h] + add_ref[rows, sls[h]]
            ss = s_new
        for h in range(G_HEADS):
            s_scr[h] = ss[h]

    r_in, r_out, r_shapes, r_scr, r_ops = _ride_args(ride)
    extra = [] if add is None else [add]
    res = pl.pallas_call(
        _ride(body, ride, 5 + len(extra), 3, lambda: pl.program_id(0) == 0, lambda: pl.program_id(0) == nst - 1), grid=(nst,), name=name,
        in_specs=[full, full, full, wide, wide] + [full] * len(extra) + r_in,
        out_specs=[full, pl.BlockSpec((per, G_HEADS, DK, DK), lambda i: (i, 0, 0, 0)),
                   pl.BlockSpec((per, G_HEADS, CH, CH), lambda i: (i, 0, 0, 0))] + r_out,
        out_shape=[jax.ShapeDtypeStruct((t, D), F32), jax.ShapeDtypeStruct((nch, G_HEADS, DK, DK), F32),
                   jax.ShapeDtypeStruct((nch, G_HEADS, CH, CH), F32)] + r_shapes,
        scratch_shapes=[pltpu.VMEM((G_HEADS, DK, DK), F32)] + r_scr,
        compiler_params=_cparams(("arbitrary",)),
    )(q, k, v, gcum, beta, *extra, *r_ops)
    return res if ride is None else (*res[:3], res[3:])


def _gdn_scan_bwd(name, q, k, v, gcum, beta, ssave, tsave, do, d, nctx, ride=None):
    t = q.shape[0]
    nch = t // CH
    rev = d == 1
    cix = lambda ib: _chunk_index(nch - 1 - ib, nch, nctx, rev)
    full = pl.BlockSpec((CH, D), lambda ib: (cix(ib), 0))
    wide = pl.BlockSpec((CH, 128), lambda ib: (cix(ib), 0))
    do_spec = pl.BlockSpec((CH, D), lambda ib: (jnp.maximum(cix(ib), nctx) - nctx, 0))

    def body(q_ref, k_ref, v_ref, g_ref, b_ref, ss_ref, ts_ref, do_ref, dq_ref, dk_ref, dv_ref, dg_ref, db_ref, ds_scr):
        ib = pl.program_id(0)

        @pl.when(ib == 0)
        def _():
            ds_scr[...] = jnp.zeros(ds_scr.shape, F32)

        sls, qs, ks, vs, gcs, grs, bcs = _gdn_operands(q_ref, k_ref, v_ref, g_ref, b_ref, d)
        t_known = [ts_ref[0, h] for h in range(G_HEADS)]
        cols_g, cols_b = _gdn_cols(d)
        is_lat = cix(ib) >= nctx
        ss = [ss_ref[0, h] for h in range(G_HEADS)]
        do_v = [jnp.where(is_lat, do_ref[:, sl], 0.0) for sl in sls]
        ds_in = [ds_scr[h] for h in range(G_HEADS)]
        _, vjp = jax.vjp(lambda *a: gdn_chunk(*a, rev, t_known)[:2], ss, qs, ks, vs, gcs, grs, bcs)
        ds, dq, dk, dv, dgc, dgr, dbc = vjp((ds_in, do_v))
        lane = _iota((CH, 128), 1)
        sub = _iota((128, CH), 0)
        dg = jnp.zeros((CH, 128), F32)
        dgt = jnp.zeros((128, CH), F32)
        db = jnp.zeros((CH, 128), F32)
        for h in range(G_HEADS):
            ds_scr[h] = ds[h]
            dq_ref[:, sls[h]] = dq[h]
            dk_ref[:, sls[h]] = dk[h]
            dv_ref[:, sls[h]] = dv[h]
            dg = dg + jnp.where(lane == cols_g[h], dgc[h], 0.0)
            dgt = dgt + jnp.where(sub == cols_g[h], dgr[h], 0.0)
            db = db + jnp.where(lane == cols_b[h], dbc[h], 0.0)
        dg_ref[...] = dg + dgt.T
        db_ref[...] = db

    r_in, r_out, r_shapes, r_scr, r_ops = _ride_args(ride)
    res = pl.pallas_call(
        _ride(body, ride, 8, 5, lambda: pl.program_id(0) == 0, lambda: pl.program_id(0) == nch - 1), grid=(nch,), name=name,
        in_specs=[full, full, full, wide, wide,
                  pl.BlockSpec((1, G_HEADS, DK, DK), lambda ib: (nch - 1 - ib, 0, 0, 0)),
                  pl.BlockSpec((1, G_HEADS, CH, CH), lambda ib: (nch - 1 - ib, 0, 0, 0)), do_spec] + r_in,
        out_specs=[full, full, full, wide, wide] + r_out,
        out_shape=[jax.ShapeDtypeStruct((t, D), F32)] * 3 + [jax.ShapeDtypeStruct((t, 128), F32)] * 2 + r_shapes,
        scratch_shapes=[pltpu.VMEM((G_HEADS, DK, DK), F32)] + r_scr,
        compiler_params=_cparams(("arbitrary",)),
    )(q, k, v, gcum, beta, ssave, tsave, do, *r_ops)
    return res if ride is None else (*res[:5], res[5:])


N_PAIRS = S_HEADS // 2


def _ssd_operands(x_ref, dt_ref, ac_ref, b_ref, c_ref, d, rows=slice(None)):
    sls = [slice(p * 128, (p + 1) * 128) for p in range(N_PAIRS)]
    gsl = [slice(g * S_N, (g + 1) * S_N) for g in range(S_GROUPS)]
    cols = [d * S_HEADS + h for h in range(S_HEADS)]
    dtc, acc = dt_ref[rows, :], ac_ref[rows, :]
    act = jnp.concatenate([acc, acc], axis=0).T
    col = lambda z, cc: z[:, cc:cc + 1]
    dts, acs, acr = ssd_pick(
        [col(dtc, cols[2 * p]) for p in range(N_PAIRS)], [col(dtc, cols[2 * p + 1]) for p in range(N_PAIRS)],
        [col(acc, cols[2 * p]) for p in range(N_PAIRS)], [col(acc, cols[2 * p + 1]) for p in range(N_PAIRS)],
        [act[cols[2 * p]:cols[2 * p] + 1, :] for p in range(N_PAIRS)],
        [act[cols[2 * p + 1]:cols[2 * p + 1] + 1, :] for p in range(N_PAIRS)])
    ops = ([x_ref[rows, sl] for sl in sls], dts, acs, acr, [b_ref[rows, gs] for gs in gsl], [c_ref[rows, gs] for gs in gsl])
    return sls, gsl, cols, ops


SSD_FWD_CHUNKS = 4


def _ssd_scan_fwd(name, xs, dtc, acc, bm, cm, d, nctx, add=None):
    t = xs.shape[0]
    nch = t // CH
    per = SSD_FWD_CHUNKS
    nst = nch // per
    rev = d == 1
    bix = lambda i: _chunk_index(i, nst, nctx // per, rev)
    inner = pl.BlockSpec((per * CH, S_INNER), lambda i: (bix(i), 0))
    wide = pl.BlockSpec((per * CH, 128), lambda i: (bix(i), 0))
    grp = pl.BlockSpec((per * CH, S_GROUPS * S_N), lambda i: (bix(i), 0))
    order = list(range(per - 1, -1, -1)) if rev else list(range(per))

    def body(x_ref, dt_ref, ac_ref, b_ref, c_ref, *rest):
        add_ref = rest[0] if add is not None else None
        y_ref, hs_ref, h_scr = rest[-3:]

        @pl.when(pl.program_id(0) == 0)
        def _():
            h_scr[...] = jnp.zeros(h_scr.shape, F32)

        loc, cgs, sls = [], [], None
        for c in order:
            sls, _, _, ops = _ssd_operands(x_ref, dt_ref, ac_ref, b_ref, c_ref, d, slice(c * CH, (c + 1) * CH))
            loc.append(ssd_local(*ops, rev))
            cgs.append(ops[5])
        hts = [h_scr[p] for p in range(N_PAIRS)]
        for n, c in enumerate(order):
            h_new, y = ssd_state(hts, *loc[n], cgs[n])
            for p in range(N_PAIRS):
                hs_ref[n, p] = hts[p]
                rows = slice(c * CH, (c + 1) * CH)
                y_ref[rows, sls[p]] = y[p] if add is None else y[p] + add_ref[rows, sls[p]]
            hts = h_new
        for p in range(N_PAIRS):
            h_scr[p] = hts[p]

    return pl.pallas_call(
        body, grid=(nst,), name=name,
        in_specs=[inner, wide, wide, grp, grp] + ([] if add is None else [inner]),
        out_specs=[inner, pl.BlockSpec((per, N_PAIRS, S_N, 128), lambda i: (i, 0, 0, 0))],
        out_shape=[jax.ShapeDtypeStruct((t, S_INNER), F32), jax.ShapeDtypeStruct((nch, N_PAIRS, S_N, 128), F32)],
        scratch_shapes=[pltpu.VMEM((N_PAIRS, S_N, 128), F32)],
        compiler_params=_cparams(("arbitrary",)),
    )(xs, dtc, acc, bm, cm, *([] if add is None else [add]))


def _ssd_scan_bwd(name, xs, dtc, acc, bm, cm, hsave, dy, d, nctx):
    t = xs.shape[0]
    nch = t // CH
    rev = d == 1
    cix = lambda ib: _chunk_index(nch - 1 - ib, nch, nctx, rev)
    inner = pl.BlockSpec((CH, S_INNER), lambda ib: (cix(ib), 0))
    wide = pl.BlockSpec((CH, 128), lambda ib: (cix(ib), 0))
    grp = pl.BlockSpec((CH, S_GROUPS * S_N), lambda ib: (cix(ib), 0))
    dy_spec = pl.BlockSpec((CH, S_INNER), lambda ib: (jnp.maximum(cix(ib), nctx) - nctx, 0))

    def body(x_ref, dt_ref, ac_ref, b_ref, c_ref, hs_ref, dy_ref, dx_ref, ddt_ref, dac_ref, db_ref, dc_ref, dh_scr):
        ib = pl.program_id(0)

        @pl.when(ib == 0)
        def _():
            dh_scr[...] = jnp.zeros(dh_scr.shape, F32)

        sls, gsl, cols, ops = _ssd_operands(x_ref, dt_ref, ac_ref, b_ref, c_ref, d)
        is_lat = cix(ib) >= nctx
        hts = [hs_ref[0, p] for p in range(N_PAIRS)]
        dy_v = [jnp.where(is_lat, dy_ref[:, sl], 0.0) for sl in sls]
        dh_in = [dh_scr[p] for p in range(N_PAIRS)]
        _, vjp = jax.vjp(lambda *a: ssd_chunk(*a, rev), hts, *ops)
        dh, dx, ddts, dacs, dacr, db, dc = vjp((dh_in, dy_v))
        for p in range(N_PAIRS):
            dh_scr[p] = dh[p]
            dx_ref[:, sls[p]] = dx[p]
        r = _iota((S_INNER, 128), 0)
        e_t = jnp.where(_iota((S_INNER, 128), 1) == d * S_HEADS + r // S_P, 1.0, 0.0).astype(F32)
        ddt_ref[...] = _dot(jnp.concatenate(ddts, axis=1), e_t, HIGH)
        dac_cols = _dot(jnp.concatenate(dacs, axis=1), e_t, HIGH)
        sub = _iota((128, 128), 0)
        lane = _iota((128, 128), 1)
        m = jnp.zeros((128, 128), F32)
        for p in range(N_PAIRS):
            m = m + jnp.where(sub == p, jnp.sum(dacr[p], axis=0, keepdims=True), 0.0)
        mt = m.T
        s0 = jnp.where(lane == d * S_HEADS + 2 * sub, 1.0, 0.0).astype(F32)
        s1 = jnp.where(lane == d * S_HEADS + 2 * sub + 1, 1.0, 0.0).astype(F32)
        dac_ref[...] = dac_cols + _dot(mt[:CH], s0, HIGH) + _dot(mt[CH:], s1, HIGH)
        for g in range(S_GROUPS):
            db_ref[:, gsl[g]] = db[g]
            dc_ref[:, gsl[g]] = dc[g]

    return pl.pallas_call(
        body, grid=(nch,), name=name,
        in_specs=[inner, wide, wide, grp, grp,
                  pl.BlockSpec((1, N_PAIRS, S_N, 128), lambda ib: (nch - 1 - ib, 0, 0, 0)), dy_spec],
        out_specs=[inner, wide, wide, grp, grp],
        out_shape=[jax.ShapeDtypeStruct((t, S_INNER), F32)] + [jax.ShapeDtypeStruct((t, 128), F32)] * 2
        + [jax.ShapeDtypeStruct((t, S_GROUPS * S_N), F32)] * 2,
        scratch_shapes=[pltpu.VMEM((N_PAIRS, S_N, 128), F32)],
        compiler_params=_cparams(("arbitrary",)),
    )(xs, dtc, acc, bm, cm, hsave, dy)


def _mesh_pos():
    return lax.axis_index("x"), lax.axis_index("y"), lax.axis_index("c")


def _hbm_specs(n):
    return [pl.BlockSpec(memory_space=pl.ANY)] * n


def _sem_shapes(nw):
    return [pltpu.SemaphoreType.DMA((nw, 7)), pltpu.SemaphoreType.DMA((nw, 7)), pltpu.SemaphoreType.DMA((nw,))]


class _AllGather:
    def __init__(self, shards):
        self.arrays = list(shards)
        self.out_shapes = [jax.ShapeDtypeStruct((N_DEV,) + xs.shape, xs.dtype) for xs in shards]

    def _parts(self, x_refs, out_refs, sems):
        send_sems, recv_sems, local_sems = sems
        nw = len(x_refs)
        x, y, c = _mesh_pos()
        me, sibling = (x, y, c), (x, y, 1 - c)
        chips = [(1 - x, y), (x, 1 - y), (1 - x, 1 - y)]

        def slot(w, px, py, pc):
            return out_refs[w].at[4 * px + 2 * py + pc]

        def copy(w, k, block, to, src=None):
            return pltpu.make_async_remote_copy(
                src_ref=slot(w, *block) if src is None else src, dst_ref=slot(w, *block),
                send_sem=send_sems.at[w, k], recv_sem=recv_sems.at[w, k], device_id=to, device_id_type=MESH)

        mine = [pltpu.make_async_copy(x_refs[w], slot(w, *me), local_sems.at[w]) for w in range(nw)]
        first = []
        for w in range(nw):
            first.append(copy(w, 0, me, sibling, src=x_refs[w]))
            first += [copy(w, 1 + j, me, (*chip, c), src=x_refs[w]) for j, chip in enumerate(chips)]
        return nw, me, sibling, chips, c, copy, mine, first

    def start(self, x_refs, out_refs, sems):
        _, _, _, _, _, _, mine, first = self._parts(x_refs, out_refs, sems)
        for cp in mine + first:
            cp.start()

    def finish(self, x_refs, out_refs, sems):
        nw, me, sibling, chips, c, copy, mine, first = self._parts(x_refs, out_refs, sems)
        passed = []
        for j, chip in enumerate(chips):
            for w in range(nw):
                copy(w, 1 + j, (*chip, c), me).wait_recv()
                fwd = copy(w, 4 + j, (*chip, c), sibling)
                fwd.start()
                passed.append(fwd)
        for w in range(nw):
            copy(w, 0, sibling, me).wait_recv()
            for j, chip in enumerate(chips):
                copy(w, 4 + j, (*chip, 1 - c), me).wait_recv()
        for cp in first + passed:
            cp.wait_send()
        for cp in mine:
            cp.wait()


class _AllToAll:
    def __init__(self, blocks):
        self.arrays = list(blocks)
        self.out_shapes = [jax.ShapeDtypeStruct(g.shape, g.dtype) for g in blocks]

    def _parts(self, g_refs, out_refs, sems):
        send_sems, recv_sems, local_sems = sems
        nw = len(g_refs)
        x, y, c = _mesh_pos()
        me_i = 4 * x + 2 * y + c
        mine = [pltpu.make_async_copy(g_refs[w].at[me_i], out_refs[w].at[me_i], local_sems.at[w]) for w in range(nw)]
        cps = []
        for k in range(1, N_DEV):
            px = 1 - x if (k >> 2) & 1 else x
            py = 1 - y if (k >> 1) & 1 else y
            pc = 1 - c if k & 1 else c
            for w in range(nw):
                cps.append(pltpu.make_async_remote_copy(
                    src_ref=g_refs[w].at[4 * px + 2 * py + pc], dst_ref=out_refs[w].at[me_i],
                    send_sem=send_sems.at[w, k - 1], recv_sem=recv_sems.at[w, k - 1],
                    device_id=(px, py, pc), device_id_type=MESH))
        return mine, cps

    def start(self, g_refs, out_refs, sems):
        mine, cps = self._parts(g_refs, out_refs, sems)
        for cp in mine + cps:
            cp.start()

    def finish(self, g_refs, out_refs, sems):
        mine, cps = self._parts(g_refs, out_refs, sems)
        for cp in cps + mine:
            cp.wait()


def _exchange(name, ex):
    nw = len(ex.arrays)

    def body(*refs):
        ins, outs, sems = refs[:nw], refs[nw:2 * nw], refs[2 * nw:]
        ex.start(ins, outs, sems)
        ex.finish(ins, outs, sems)

    return pl.pallas_call(body, name=name, out_shape=ex.out_shapes, in_specs=_hbm_specs(nw), out_specs=_hbm_specs(nw),
                          scratch_shapes=_sem_shapes(nw))(*ex.arrays)


def _ride(body, ex, n_in, n_out, is_first, is_last):
    if ex is None:
        return body
    nw = len(ex.arrays)

    def riding(*refs):
        ins, ex_in = refs[:n_in], refs[n_in:n_in + nw]
        outs = refs[n_in + nw:n_in + nw + n_out]
        ex_out = refs[n_in + nw + n_out:n_in + 2 * nw + n_out]
        rest = refs[n_in + 2 * nw + n_out:]
        scratch, sems = rest[:len(rest) - 3], rest[len(rest) - 3:]

        @pl.when(is_first())
        def _():
            ex.start(ex_in, ex_out, sems)

        body(*ins, *outs, *scratch)

        @pl.when(is_last())
        def _():
            ex.finish(ex_in, ex_out, sems)

    return riding


def _ride_args(ex):
    if ex is None:
        return [], [], [], [], []
    nw = len(ex.arrays)
    return _hbm_specs(nw), _hbm_specs(nw), list(ex.out_shapes), _sem_shapes(nw), list(ex.arrays)


def _reduce_adam(name, recv, w, m, v, tm):
    rows, width = w.shape
    nslot = recv.shape[0]

    def body(recv_ref, w_ref, m_ref, v_ref, g_ref, d_ref, m2_ref, v2_ref):
        g = recv_ref[0].astype(F32)
        for s in range(1, nslot):
            g = g + recv_ref[s].astype(F32)
        delta, m2, v2 = f_adamw(w_ref[...], g, m_ref[...], v_ref[...])
        g_ref[...] = g
        d_ref[...] = delta
        m2_ref[...] = m2
        v2_ref[...] = v2

    row = pl.BlockSpec((tm, width), lambda i: (i, 0))
    return pl.pallas_call(
        body, grid=(rows // tm,), name=name,
        in_specs=[pl.BlockSpec((nslot, tm, width), lambda i: (0, i, 0)), row, row, row],
        out_specs=[row] * 4,
        out_shape=[jax.ShapeDtypeStruct((rows, width), F32)] * 4,
        compiler_params=_cparams(("arbitrary",)),
    )(recv, w, m, v)


BIG = ("w_in", "ada_w", "w_br_gdn", "w_br_ssm", "w_out", "w_ffn_in", "w_ffn_out")
BIG_FIRST = ("ada_w", "w_in")
BIG_LATE = ("w_br_gdn", "w_br_ssm", "w_out", "w_ffn_in", "w_ffn_out")
BIG_COL_SHARDED = ("w_in", "ada_w", "w_ffn_in")
BIG_ADAM_ROWS = dict(w_in=128, ada_w=256, w_br_gdn=128, w_br_ssm=256, w_out=128, w_ffn_in=256, w_ffn_out=352)
CONV = ("gdn_conv_w", "ssm_conv_w")
SMALL = ("c_ctx", "ada_b", "norm1_w", "gdn_conv_b", "gdn_a_log", "gdn_dt_bias", "gdn_norm_w", "ssm_conv_b",
         "ssm_a_log", "ssm_dt_bias", "ssm_d", "ssm_norm_w", "norm2_w", "norm_f_w")
CONV_SHARD = XBC // N_DEV


def _to_rows(a):
    flat = a.reshape(-1)
    pad = (-flat.shape[0]) % PACK_W
    if pad:
        flat = jnp.pad(flat, (0, pad))
    return flat.reshape(-1, PACK_W)


def _pack(arrays, rows=None):
    buf = jnp.concatenate([_to_rows(a) for a in arrays], axis=0)
    if rows is not None and rows > buf.shape[0]:
        buf = jnp.pad(buf, ((0, rows - buf.shape[0]), (0, 0)))
    return buf


def _unpack(buf, shapes):
    out, r0 = [], 0
    for shp in shapes:
        n = 1
        for s in shp:
            n *= s
        nr = -(-n // PACK_W)
        out.append(buf[r0:r0 + nr].reshape(-1)[:n].reshape(shp))
        r0 += nr
    return out


def _full_from_blocks(blocks, col_sharded):
    _, r, c = blocks.shape
    if col_sharded:
        return jnp.transpose(blocks, (1, 0, 2)).reshape(r, N_DEV * c)
    return blocks.reshape(N_DEV * r, c)


def _blocks_from_full(full, col_sharded):
    if col_sharded:
        r, c = full.shape[0], full.shape[1] // N_DEV
        return jnp.transpose(full.reshape(r, N_DEV, c), (1, 0, 2))
    return full.reshape(N_DEV, full.shape[0] // N_DEV, full.shape[1])


def _pad_cols(a, n):
    return jnp.pad(a, ((0, 0), (0, n - a.shape[1])))


def _w_cat(w_in):
    return jnp.concatenate([
        w_in[:, O_QKV:O_ZG], w_in[:, O_XBC:O_DT], w_in[:, O_ZS:O_XBC], w_in[:, O_GATE:O_END], w_in[:, O_ZG:O_AB],
        _pad_cols(w_in[:, O_AB:O_ZS], 128), _pad_cols(w_in[:, O_DT:O_GATE], 128)], axis=1)


def _w_uncat(wc):
    return jnp.concatenate([
        wc[:, C_QKV:C_XBC], wc[:, C_ZG:C_AB], wc[:, C_AB:C_AB + (O_ZS - O_AB)], wc[:, C_ZS:C_GATE], wc[:, C_XBC:C_ZS],
        wc[:, C_DT:C_DT + (O_GATE - O_DT)], wc[:, C_GATE:C_ZG]], axis=1)


def _pad_row(vec, n=128):
    vec = vec.reshape(1, -1)
    return _pad_cols(vec, n)


def kernel(x, c, ctx, c_ctx, ada_w, ada_b, norm1_w, w_in, gdn_conv_w, gdn_conv_b, gdn_a_log, gdn_dt_bias, gdn_norm_w, ssm_conv_w, ssm_conv_b, ssm_a_log, ssm_dt_bias, ssm_d, ssm_norm_w, w_br_gdn, w_br_ssm, w_out, norm2_w, w_ffn_in, w_ffn_out, norm_f_w, loss_target, m_c_ctx, m_ada_w, m_ada_b, m_norm1_w, m_w_in, m_gdn_conv_w, m_gdn_conv_b, m_gdn_a_log, m_gdn_dt_bias, m_gdn_norm_w, m_ssm_conv_w, m_ssm_conv_b, m_ssm_a_log, m_ssm_dt_bias, m_ssm_d, m_ssm_norm_w, m_w_br_gdn, m_w_br_ssm, m_w_out, m_norm2_w, m_w_ffn_in, m_w_ffn_out, m_norm_f_w, v_c_ctx, v_ada_w, v_ada_b, v_norm1_w, v_w_in, v_gdn_conv_w, v_gdn_conv_b, v_gdn_a_log, v_gdn_dt_bias, v_gdn_norm_w, v_ssm_conv_w, v_ssm_conv_b, v_ssm_a_log, v_ssm_dt_bias, v_ssm_d, v_ssm_norm_w, v_w_br_gdn, v_w_br_ssm, v_w_out, v_norm2_w, v_w_ffn_in, v_w_ffn_out, v_norm_f_w):
    wts = dict(c_ctx=c_ctx, ada_w=ada_w, ada_b=ada_b, norm1_w=norm1_w, w_in=w_in, gdn_conv_w=gdn_conv_w, gdn_conv_b=gdn_conv_b, gdn_a_log=gdn_a_log, gdn_dt_bias=gdn_dt_bias, gdn_norm_w=gdn_norm_w, ssm_conv_w=ssm_conv_w, ssm_conv_b=ssm_conv_b, ssm_a_log=ssm_a_log, ssm_dt_bias=ssm_dt_bias, ssm_d=ssm_d, ssm_norm_w=ssm_norm_w, w_br_gdn=w_br_gdn, w_br_ssm=w_br_ssm, w_out=w_out, norm2_w=norm2_w, w_ffn_in=w_ffn_in, w_ffn_out=w_ffn_out, norm_f_w=norm_f_w)
    mom1 = dict(c_ctx=m_c_ctx, ada_w=m_ada_w, ada_b=m_ada_b, norm1_w=m_norm1_w, w_in=m_w_in, gdn_conv_w=m_gdn_conv_w, gdn_conv_b=m_gdn_conv_b, gdn_a_log=m_gdn_a_log, gdn_dt_bias=m_gdn_dt_bias, gdn_norm_w=m_gdn_norm_w, ssm_conv_w=m_ssm_conv_w, ssm_conv_b=m_ssm_conv_b, ssm_a_log=m_ssm_a_log, ssm_dt_bias=m_ssm_dt_bias, ssm_d=m_ssm_d, ssm_norm_w=m_ssm_norm_w, w_br_gdn=m_w_br_gdn, w_br_ssm=m_w_br_ssm, w_out=m_w_out, norm2_w=m_norm2_w, w_ffn_in=m_w_ffn_in, w_ffn_out=m_w_ffn_out, norm_f_w=m_norm_f_w)
    mom2 = dict(c_ctx=v_c_ctx, ada_w=v_ada_w, ada_b=v_ada_b, norm1_w=v_norm1_w, w_in=v_w_in, gdn_conv_w=v_gdn_conv_w, gdn_conv_b=v_gdn_conv_b, gdn_a_log=v_gdn_a_log, gdn_dt_bias=v_gdn_dt_bias, gdn_norm_w=v_gdn_norm_w, ssm_conv_w=v_ssm_conv_w, ssm_conv_b=v_ssm_conv_b, ssm_a_log=v_ssm_a_log, ssm_dt_bias=v_ssm_dt_bias, ssm_d=v_ssm_d, ssm_norm_w=v_ssm_norm_w, w_br_gdn=v_w_br_gdn, w_br_ssm=v_w_br_ssm, w_out=v_w_out, norm2_w=v_norm2_w, w_ffn_in=v_w_ffn_in, w_ffn_out=v_w_ffn_out, norm_f_w=v_norm_f_w)
    order = list(wts)

    seq = x.shape[1]
    t = TM + seq
    ntl, nlt, nctx = t // TM, seq // TM, TM // CH

    me_i = 4 * lax.axis_index("x") + 2 * lax.axis_index("y") + lax.axis_index("c")
    conv_sh = _pack([wts[n] for n in CONV], rows=8)
    gathered = _exchange("ag_weights", _AllGather([wts[n][0].astype(BF16) for n in BIG_FIRST] + [conv_sh]))
    full = {n: _full_from_blocks(blk, n in BIG_COL_SHARDED) for n, blk in zip(BIG_FIRST, gathered)}
    late_gather = _AllGather([wts[n][0].astype(BF16) for n in BIG_LATE])
    conv_g = gathered[len(BIG_FIRST)].reshape(N_DEV, -1)
    ncv = 3 * CONV_SHARD
    for i, n in enumerate(CONV):
        off = -(-ncv // PACK_W) * PACK_W * i
        full[n] = jnp.transpose(conv_g[:, off:off + ncv].reshape(N_DEV, 3, CONV_SHARD), (1, 0, 2)).reshape(3, XBC)
    w_cat = _w_cat(full["w_in"])
    gcw = full["gdn_conv_w"].reshape(3, 1, XBC)
    scw = full["ssm_conv_w"].reshape(3, 1, XBC)

    n1w, n2w, nfw = norm1_w.reshape(1, D), norm2_w.reshape(1, D), norm_f_w.reshape(1, D)
    gcb, scb = gdn_conv_b.reshape(1, XBC), ssm_conv_b.reshape(1, XBC)
    alog16, dtb16 = _pad_row(gdn_a_log), _pad_row(gdn_dt_bias)
    alog64, dtb64 = _pad_row(ssm_a_log), _pad_row(ssm_dt_bias)
    gnw = gdn_norm_w.reshape(1, DK)
    ssd8 = jnp.tile(_pad_row(ssm_d), (8, 1))
    snw = ssm_norm_w.reshape(1, S_INNER)
    x2 = x[0]
    tgt = loss_target[0]
    cvec = jnp.concatenate([c, c_ctx.reshape(1, D), jnp.zeros((14, D), F32)], axis=0)

    a16 = _rowwise("silu_c", f_silu_rows, [(cvec, D, 0, 0)], [], [(D, BF16)], 1, tm=16)[0]
    mod = _mm("mm_mod", a16, full["ada_w"], "nn", F32) + ada_b
    sh1, sc1, g1, sh2, sc2, g2 = [mod[0:1, i * D:(i + 1) * D] for i in range(6)]
    csh1, csc1 = mod[1:2, 0:D], mod[1:2, D:2 * D]

    pre_pars = [n1w, sc1, sh1, csc1, csh1]
    pre_rows = [(x2, D, 0, -1, True), (ctx[0], D, 0, 0, "first")]
    a = _rowwise("pre", f_pre, pre_rows, pre_pars, [(D, BF16)], ntl, base=0)[0]
    proj = _mm("mm_proj", a, w_cat, "nn", F32, tm=1408, tn=1280)
    gp_rows = [(proj, XBC, C_QKV // XBC, 0), (proj, 128, C_AB // 128, 0)]
    gp_pars = [gcw, gcb, alog16, dtb16]
    q, k, v, gcum, beta = _rowwise("gdnprep", f_gdnprep, gp_rows, gp_pars, [(D, F32)] * 3 + [(128, F32)] * 2, ntl, base=0)
    sp_rows = [(proj, XBC, C_XBC // XBC, 0), (proj, 128, C_DT // 128, 0)]
    sp_pars = [scw, scb, alog64, dtb64]
    xs, bm, cm, dtc, acc = _rowwise(
        "ssmprep", f_ssmprep, sp_rows, sp_pars, [(S_INNER, F32), (512, F32), (512, F32), (128, F32), (128, F32)], ntl, base=0)
    o0, ss0, ts0, gathered = _gdn_scan_fwd("gdn_fwd0", q, k, v, gcum, beta, 0, nctx, ride=late_gather)
    full.update({n: _full_from_blocks(blk, n in BIG_COL_SHARDED) for n, blk in zip(BIG_LATE, gathered)})
    o_sum, ss1, ts1 = _gdn_scan_fwd("gdn_fwd1", q, k, v, gcum, beta, 1, nctx, add=o0)
    y0, hs0 = _ssd_scan_fwd("ssd_fwd0", xs, dtc, acc, bm, cm, 0, nctx)
    y_sum, hs1 = _ssd_scan_fwd("ssd_fwd1", xs, dtc, acc, bm, cm, 1, nctx, add=y0)
    post_rows = [(o_sum, D, 0, 1), (proj, D, C_ZG // D, 1), (y_sum, S_INNER, 0, 1), (xs, S_INNER, 0, 1),
                 (proj, S_INNER, C_ZS // S_INNER, 1)]
    post_pars = [gnw, ssd8, snw]
    og, ys = _rowwise("post", f_post, post_rows, post_pars, [(D, BF16), (S_INNER, BF16)], nlt)
    pg = _mm("mm_pg", og, full["w_br_gdn"], "nn", F32)
    ps = _mm("mm_ps", ys, full["w_br_ssm"], "nn", F32, tk=2048)
    merge_rows = [(proj, S_INNER, C_GATE // S_INNER, 1), (pg, D, 0, 0), (ps, D, 0, 0)]
    merged = _rowwise("merge", f_merge, merge_rows, [], [(D, BF16)], nlt)[0]
    mix = _mm("mm_mix", merged, full["w_out"], "nn", F32)
    res_rows = [(x2, D, 0, 0), (mix, D, 0, 0)]
    res_pars = [g1, n2w, sc2, sh2]
    h1, f = _rowwise("res1", f_res1, res_rows, res_pars, [(D, F32), (D, BF16)], nlt)
    u = _mm("mm_u", f, full["w_ffn_in"], "nn", F32, tn=1408)
    hact = _rowwise("act", f_act, [(u, 2 * D_FF, 0, 0)], [], [(D_FF, BF16)], nlt)[0]
    ff = _mm("mm_ff", hact, full["w_ffn_out"], "nn", F32, tk=2816)

    fin_rows = [(h1, D, 0, 0), (ff, D, 0, 0), (tgt, D, 0, 0)]
    d_h1a, d_ff, d_g2, d_nfw, loss_acc = _rowwise_bwd(
        "final", f_final, fin_rows, [g2, nfw], ["one"], [(0, F32), (1, BF16)], [0, 1], nlt, loss_out=True)
    d_hact = _mm("mm_dhact", d_ff, full["w_ffn_out"], "nt", BF16, tn=1408)
    g_w_ffn_out = _mm("mm_gwffo", hact, d_ff, "tn", BF16, tm=1408, tk=2048)
    d_u = _rowwise_bwd("act_bwd", f_act, [(u, 2 * D_FF, 0, 0)], [], [[(d_hact, D_FF, 0, 0)]], [(0, BF16)], [], nlt)[0]
    d_f = _mm("mm_df", d_u, full["w_ffn_in"], "nt", BF16, tk=2816)
    g_w_ffn_in = _mm("mm_gwffi", f, d_u, "tn", BF16, tn=1408, tk=2048)
    d_xres, d_mix, d_g1, d_n2w, d_sc2, d_sh2 = _rowwise_bwd(
        "res1_bwd", f_res1, res_rows, res_pars, [[(d_h1a, D, 0, 0)], [(d_f, D, 0, 0)]], [(0, F32), (1, BF16)], [0, 1, 2, 3], nlt)
    d_merged = _mm("mm_dmerged", d_mix, full["w_out"], "nt", BF16)
    g_w_out = _mm("mm_gwout", merged, d_mix, "tn", BF16, tk=2048)
    d_gate, d_pg, d_ps = _rowwise_bwd(
        "merge_bwd", f_merge, merge_rows, [], [[(d_merged, D, 0, 0)]], [(0, BF16), (1, BF16), (2, BF16)], [], nlt)
    d_og = _mm("mm_dog", d_pg, full["w_br_gdn"], "nt", BF16)
    g_w_br_gdn = _mm("mm_gwbrg", og, d_pg, "tn", BF16, tk=2048)
    d_ys = _mm("mm_dys", d_ps, full["w_br_ssm"], "nt", BF16, tn=2048)
    g_w_br_ssm = _mm("mm_gwbrs", ys, d_ps, "tn", BF16, tk=2048)
    d_o, d_zg, d_y, d_xs_post, d_zs, d_gnw, d_ssd8, d_snw = _rowwise_bwd(
        "post_bwd", f_post, post_rows, post_pars, [[(d_og, D, 0, 0)], [(d_ys, S_INNER, 0, 0)]],
        [(0, F32), (1, BF16), (2, F32), (3, F32), (4, BF16)], [0, 1, 2], nlt)
    late_grads = dict(w_br_gdn=g_w_br_gdn, w_br_ssm=g_w_br_ssm, w_out=g_w_out, w_ffn_in=g_w_ffn_in, w_ffn_out=g_w_ffn_out)
    late_a2a = _AllToAll([_blocks_from_full(late_grads[n], n in BIG_COL_SHARDED) for n in BIG_LATE])
    dq0, dk0, dv0, dg0, db0, recv_late = _gdn_scan_bwd("gdn_bwd0", q, k, v, gcum, beta, ss0, ts0, d_o, 0, nctx, ride=late_a2a)
    dq1, dk1, dv1, dg1, db1 = _gdn_scan_bwd("gdn_bwd1", q, k, v, gcum, beta, ss1, ts1, d_o, 1, nctx)
    dxs0, ddt0, dac0, dbm0, dcm0 = _ssd_scan_bwd("ssd_bwd0", xs, dtc, acc, bm, cm, hs0, d_y, 0, nctx)
    dxs1, ddt1, dac1, dbm1, dcm1 = _ssd_scan_bwd("ssd_bwd1", xs, dtc, acc, bm, cm, hs1, d_y, 1, nctx)
    row = lambda arr, wd: (arr, wd, 0, 0)
    d_qkv_raw, d_ab, d_gcw, d_gcb, d_alog16, d_dtb16 = _rowwise_bwd(
        "gdnprep_bwd", f_gdnprep, gp_rows, gp_pars,
        [[row(dq0, D), row(dq1, D)], [row(dk0, D), row(dk1, D)], [row(dv0, D), row(dv1, D)],
         [row(dg0, 128), row(dg1, 128)], [row(db0, 128), row(db1, 128)]],
        [(0, BF16), (1, BF16)], [0, 1, 2, 3], ntl, base=0)
    d_xbc_raw, d_dt, d_scw, d_scb, d_alog64, d_dtb64 = _rowwise_bwd(
        "ssmprep_bwd", f_ssmprep, sp_rows, sp_pars,
        [[row(dxs0, S_INNER), row(dxs1, S_INNER), (d_xs_post, S_INNER, 0, -1, True)], [row(dbm0, 512), row(dbm1, 512)],
         [row(dcm0, 512), row(dcm1, 512)], [row(ddt0, 128), row(ddt1, 128)], [row(dac0, 128), row(dac1, 128)]],
        [(0, BF16), (1, BF16)], [0, 1, 2, 3], ntl, base=0)
    ctx_zero = lambda wd: jnp.zeros((TM, wd), BF16)
    d_proj = jnp.concatenate([
        d_qkv_raw, d_xbc_raw, jnp.concatenate([ctx_zero(S_INNER), d_zs], axis=0),
        jnp.concatenate([ctx_zero(S_INNER), d_gate], axis=0), jnp.concatenate([ctx_zero(D), d_zg], axis=0), d_ab, d_dt], axis=1)
    g_w_cat = _mm("mm_gwcat", a, d_proj, "tn", BF16, tn=768, tk=2816)
    w_in_a2a = _AllToAll([_blocks_from_full(_w_uncat(g_w_cat), True)])
    d_a, recv_w_in = _mm("mm_da", d_proj, w_cat, "nt", BF16, tk=3840, ride=w_in_a2a)
    d_x, d_n1w, d_sc1, d_sh1, d_csc1, d_csh1 = _rowwise_bwd(
        "pre_bwd", f_pre_thru, pre_rows, pre_pars, [[row(d_a, D)], [(d_xres, D, 0, -1, True)]],
        [(0, F32)], [0, 1, 2, 3, 4], ntl, base=0)
    zero4 = jnp.zeros((1, 4 * D), F32)
    d_mod = jnp.concatenate([
        jnp.concatenate([d_sh1, d_sc1, d_g1, d_sh2, d_sc2, d_g2], axis=1),
        jnp.concatenate([d_csh1, d_csc1, zero4], axis=1), jnp.zeros((14, 6 * D), F32)], axis=0)
    d_a16 = _mm("mm_da16", d_mod, full["ada_w"], "nt", F32)
    d_cvec = _rowwise_bwd("silu_c_bwd", f_silu_rows, [(cvec, D, 0, 0)], [], [[row(d_a16, D)]], [(0, F32)], [], 1, tm=16)[0]

    recv = dict(zip(BIG_LATE, recv_late), w_in=recv_w_in[0])
    big_un = {n: _reduce_adam("adam_" + n, recv[n], wts[n][0], mom1[n][0], mom2[n][0], BIG_ADAM_ROWS[n])
              for n in BIG if n != "ada_w"}

    small_g = dict(c_ctx=d_cvec[1], ada_b=d_mod[0] + d_mod[1], norm1_w=d_n1w, gdn_conv_b=d_gcb,
                   gdn_a_log=d_alog16[0, :2 * G_HEADS], gdn_dt_bias=d_dtb16[0, :2 * G_HEADS], gdn_norm_w=d_gnw,
                   ssm_conv_b=d_scb, ssm_a_log=d_alog64[0, :2 * S_HEADS], ssm_dt_bias=d_dtb64[0, :2 * S_HEADS],
                   ssm_d=d_ssd8[0, :S_HEADS], ssm_norm_w=d_snw, norm2_w=d_n2w, norm_f_w=d_nfw,
                   gdn_conv_w=d_gcw.reshape(3, XBC), ssm_conv_w=d_scw.reshape(3, XBC))
    small_names = SMALL + CONV
    factors = [a16[0].astype(F32), d_mod[0], d_mod[1]]
    round8 = lambda r: -(-r // 8) * 8
    packed_rows = lambda arrs: sum(-(-arr.size // PACK_W) for arr in arrs)
    n_small = packed_rows([small_g[n] for n in small_names])
    n_fac = packed_rows(factors)
    rows_small = round8(n_small)
    sg_pack = _pack([small_g[n] for n in small_names] + factors, rows=round8(n_small + n_fac))
    recv_all = _exchange("ag_small_grads", _AllGather([sg_pack]))[0]
    recv_s = recv_all[:, :rows_small]
    fac = recv_all[:, n_small:n_small + n_fac].reshape(N_DEV, -1)
    my_cols = lambda z: lax.dynamic_slice(z, (0, me_i * (6 * D // N_DEV)), (N_DEV, 6 * D // N_DEV))
    lhs = jnp.concatenate([fac[:, :D], jnp.broadcast_to(a16[1:2].astype(F32), (N_DEV, D))], axis=0)
    rhs = jnp.concatenate([my_cols(fac[:, D:7 * D]), my_cols(fac[:, 7 * D:])], axis=0)
    g_ada_w = _mm("mm_gwada", lhs, rhs, "tn", F32)
    big_un["ada_w"] = _reduce_adam("adam_ada_w", g_ada_w[None], wts["ada_w"][0], mom1["ada_w"][0], mom2["ada_w"][0],
                                   BIG_ADAM_ROWS["ada_w"])

    def placed(src, n):
        if n not in CONV:
            return src[n]
        return lax.dynamic_update_slice(jnp.zeros((3, XBC), F32), src[n][0], (0, me_i * CONV_SHARD))

    small_out = _reduce_adam("adam_small", recv_s, *[_pack([placed(src, n) for n in small_names], rows=rows_small)
                                                     for src in (wts, mom1, mom2)], rows_small)
    small_shapes = [wts[n].shape if n in SMALL else (3, XBC) for n in small_names]
    small_un = [_unpack(buf, small_shapes) for buf in small_out]

    res = [{}, {}, {}, {}]
    for kind in range(4):
        for n in BIG:
            res[kind][n] = big_un[n][kind].reshape(wts[n].shape)
        for n, val in zip(small_names, small_un[kind]):
            if n in CONV:
                val = lax.dynamic_slice(val, (0, me_i * CONV_SHARD), (3, CONV_SHARD)).reshape(wts[n].shape)
            res[kind][n] = val
    loss = lax.psum(loss_acc[0, 0], ("x", "y", "c"))
    grad_x = d_x.reshape(x.shape)
    return (loss, grad_x, *[res[0][n] for n in order], *[res[1][n] for n in order], *[res[2][n] for n in order],
            *[res[3][n] for n in order])
```

```python
import functools

import jax
import jax.numpy as jnp
from jax import lax
from jax.experimental import pallas as pl
from jax.experimental.pallas import tpu as pltpu

F32 = jnp.float32
BF16 = jnp.bfloat16
HI = lax.Precision.HIGHEST
HIGH = lax.Precision.HIGH
MESH = pl.DeviceIdType.MESH

D = 1024
CH = 64
TM = 256
EPS = 1e-6
NEG = -1e30
G_HEADS = 8
DK = 128
S_HEADS = 32
S_P = 64
S_GROUPS = 4
S_N = 128
S_INNER = 2048
XBC = 3072
D_FF = 2816
N_DEV = 8
PACK_W = 1024
VMEM_LIMIT = 56 * 1024 * 1024

ADAM_LR = 0.001
ADAM_B1 = 0.9
ADAM_B2 = 0.999
ADAM_EPS = 1e-08
ADAM_WD = 0.01
ADAM_STEP = 10

C_QKV, C_XBC, C_ZS, C_GATE, C_ZG, C_AB, C_DT, C_END = 0, 3072, 6144, 8192, 10240, 11264, 11392, 11520
O_QKV, O_ZG, O_AB, O_ZS, O_XBC, O_DT, O_GATE, O_END = 0, 3072, 4096, 4128, 6176, 9248, 9312, 11360


def _dot(a, b, prec=None):
    return jnp.dot(a, b, precision=prec, preferred_element_type=F32)


def _dot_nt(a, b, prec=None):
    return lax.dot_general(a, b, (((1,), (1,)), ((), ())), precision=prec, preferred_element_type=F32)


def _dot_tn(a, b, prec=None):
    return lax.dot_general(a, b, (((0,), (0,)), ((), ())), precision=prec, preferred_element_type=F32)


def _iota(shape, dim):
    return lax.broadcasted_iota(jnp.int32, shape, dim)


def _rms(x):
    return x * lax.rsqrt(jnp.mean(x * x, axis=-1, keepdims=True) + EPS)


def _l2n(x):
    return x * lax.rsqrt(jnp.sum(x * x, axis=-1, keepdims=True) + EPS)


def _silu(x):
    return x * jax.nn.sigmoid(x)


def _softplus(x):
    return jnp.maximum(x, 0.0) + jnp.log1p(jnp.exp(-jnp.abs(x)))


def _roll_rows(x, s):
    return pltpu.roll(x, s, 0)


def _up_raw(x, keep_up):
    return jnp.where(keep_up > 0.0, _roll_rows(x, 1), 0.0)


def _dn_raw(x, keep_dn):
    return jnp.where(keep_dn > 0.0, _roll_rows(x, x.shape[0] - 1), 0.0)


@jax.custom_vjp
def _shift_up(x, keep_up, keep_dn):
    return _up_raw(x, keep_up)


def _shift_up_fwd(x, keep_up, keep_dn):
    return _up_raw(x, keep_up), (keep_up, keep_dn)


def _shift_up_bwd(res, g):
    keep_up, keep_dn = res
    return _dn_raw(g, keep_dn), jnp.zeros_like(keep_up), jnp.zeros_like(keep_dn)


_shift_up.defvjp(_shift_up_fwd, _shift_up_bwd)


@jax.custom_vjp
def _shift_dn(x, keep_up, keep_dn):
    return _dn_raw(x, keep_dn)


def _shift_dn_fwd(x, keep_up, keep_dn):
    return _dn_raw(x, keep_dn), (keep_up, keep_dn)


def _shift_dn_bwd(res, g):
    keep_up, keep_dn = res
    return _up_raw(g, keep_up), jnp.zeros_like(keep_up), jnp.zeros_like(keep_dn)


_shift_dn.defvjp(_shift_dn_fwd, _shift_dn_bwd)


def _conv_keep(is_ctx, n):
    r = _iota((n, 1), 0)
    pos = jnp.where(is_ctx, r, r & (CH - 1))
    end = jnp.where(is_ctx, n - 1, CH - 1)
    return jnp.where(pos == 0, 0.0, 1.0).astype(F32), jnp.where(pos == end, 0.0, 1.0).astype(F32)


def _conv_silu(u, w3, b, keep_up, keep_dn):
    conv = b + _shift_up(u, keep_up, keep_dn) * w3[0] + u * w3[1] + _shift_dn(u, keep_up, keep_dn) * w3[2]
    return _silu(conv)


def _chunk_tri(n, rev):
    i = _iota((n, n), 0)
    j = _iota((n, n), 1)
    same = (i // CH) == (j // CH)
    seen = (i <= j) if rev else (i >= j)
    return jnp.where(same & seen, 1.0, 0.0).astype(F32)


def _expand_mat(rows, cols, per, base):
    r = _iota((rows, cols), 0)
    c = _iota((rows, cols), 1)
    return jnp.where(r == base + c // per, 1.0, 0.0).astype(F32)


def f_silu_rows(is_ctx, cvec):
    return (_silu(cvec).astype(BF16),)


def f_pre(is_ctx, x, xc, n1w, sc, sh, csc, csh):
    x = jnp.where(is_ctx, xc, x)
    sc_e = jnp.where(is_ctx, csc, sc)
    sh_e = jnp.where(is_ctx, csh, sh)
    a = _rms(x) * n1w * (1.0 + sc_e) + sh_e
    return (a.astype(BF16),)


def f_pre_thru(is_ctx, x, xc, n1w, sc, sh, csc, csh):
    return f_pre(is_ctx, x, xc, n1w, sc, sh, csc, csh)[0], x


def f_gdnprep(is_ctx, qkv_raw, ab_raw, cw, cb, alog, dtb):
    n = qkv_raw.shape[0]
    keep_up, keep_dn = _conv_keep(is_ctx, n)
    s = _conv_silu(qkv_raw, cw, cb, keep_up, keep_dn)
    qs, ks, vs = [], [], []
    for h in range(G_HEADS):
        qs.append(_l2n(s[:, h * DK:(h + 1) * DK]) * (DK ** -0.5))
        ks.append(_l2n(s[:, D + h * DK:D + (h + 1) * DK]))
    q = jnp.concatenate(qs, axis=1)
    k = jnp.concatenate(ks, axis=1)
    v = s[:, 2 * D:3 * D]
    lane = _iota(ab_raw.shape, 1)
    g = jnp.where(lane < 2 * G_HEADS, -jnp.exp(alog) * _softplus(ab_raw + dtb), 0.0)
    gcum = jnp.where(lane < G_HEADS, _dot(_chunk_tri(n, False), g, HI), _dot(_chunk_tri(n, True), g, HI))
    beta = jax.nn.sigmoid(ab_raw)
    return q, k, v, gcum, beta


def f_ssmprep(is_ctx, xbc_raw, dt_raw, cw, cb, alog, dtb):
    n = xbc_raw.shape[0]
    keep_up, keep_dn = _conv_keep(is_ctx, n)
    s = _conv_silu(xbc_raw, cw, cb, keep_up, keep_dn)
    xs = s[:, :S_INNER]
    bm = s[:, S_INNER:S_INNER + S_GROUPS * S_N]
    cm = s[:, S_INNER + S_GROUPS * S_N:]
    lane = _iota(dt_raw.shape, 1)
    dt = jnp.where(lane < 2 * S_HEADS, _softplus(dt_raw + dtb), 0.0)
    da = dt * (-jnp.exp(alog))
    acum = jnp.where(lane < S_HEADS, _dot(_chunk_tri(n, False), da, HI), _dot(_chunk_tri(n, True), da, HI))
    return xs, bm, cm, dt, acum


def f_post(is_ctx, o, zg, y_scan, xs, zs, gnw, ssd8, snw):
    ogs = []
    for h in range(G_HEADS):
        sl = slice(h * DK, (h + 1) * DK)
        ogs.append(_rms(o[:, sl]) * gnw * _silu(zg[:, sl]))
    og = jnp.concatenate(ogs, axis=1)
    row0 = jnp.where(_iota(ssd8.shape, 0) == 0, 1.0, 0.0).astype(F32)
    dexp = jnp.sum(_dot(ssd8 * row0, _expand_mat(128, S_INNER, S_P, 0), HI), axis=0, keepdims=True)
    y = (y_scan + dexp * xs) * _silu(zs)
    gw = S_INNER // S_GROUPS
    ys = jnp.concatenate([_rms(y[:, i * gw:(i + 1) * gw]) * snw[:, i * gw:(i + 1) * gw] for i in range(S_GROUPS)], axis=1)
    return og.astype(BF16), ys.astype(BF16)


def f_merge(is_ctx, gate, pg, ps):
    m = jax.nn.sigmoid(gate[:, :D]) * pg + jax.nn.sigmoid(gate[:, D:]) * ps
    return (m.astype(BF16),)


def f_res1(is_ctx, x, mix, g1, n2w, sc2, sh2):
    h1 = x + g1 * mix
    f = _rms(h1) * n2w * (1.0 + sc2) + sh2
    return h1, f.astype(BF16)


def f_act(is_ctx, u):
    return ((_silu(u[:, :D_FF]) * u[:, D_FF:]).astype(BF16),)


def f_final(is_ctx, h1, ff, tgt, g2, nfw):
    h2 = h1 + g2 * ff
    y = _rms(h2) * nfw
    err = y - tgt
    return (0.5 * jnp.sum(jnp.mean(err * err, axis=-1, keepdims=True), axis=0, keepdims=True),)


def _bf16_dot(kind):
    fwd_op = {"nn": _dot, "nt": _dot_nt, "tn": _dot_tn}[kind]

    @jax.custom_vjp
    def f(a, b):
        return fwd_op(a.astype(BF16), b.astype(BF16))

    def f_fwd(a, b):
        ab, bb = a.astype(BF16), b.astype(BF16)
        return fwd_op(ab, bb), (ab, bb)

    def f_bwd(res, g):
        ab, bb = res
        gb = g.astype(BF16)
        if kind == "nn":
            return _dot_nt(gb, bb), _dot_tn(ab, gb)
        if kind == "nt":
            return _dot(gb, bb), _dot_tn(gb, ab)
        return _dot_nt(bb, gb), _dot(ab, gb)

    f.defvjp(f_fwd, f_bwd)
    return f


_bdot, _bdot_nt, _bdot_tn = _bf16_dot("nn"), _bf16_dot("nt"), _bf16_dot("tn")


def _each(fn, *lists):
    return [fn(*args) for args in zip(*lists)]


def _tri_inverse_all(mats):
    n = mats[0].shape[0]
    eye = jnp.where(_iota((n, n), 0) == _iota((n, n), 1), 1.0, 0.0).astype(F32)
    t = [eye - a for a in mats]
    p = [_dot(a, a, HIGH) for a in mats]
    for r in range(5):
        t = _each(lambda t_, p_: t_ + _dot(t_, p_, HIGH), t, p)
        if r < 4:
            p = [_dot(p_, p_, HIGH) for p_ in p]
    return t


@jax.custom_vjp
def _inverse_given(a, t):
    return t


def _inverse_given_fwd(a, t):
    return t, t


def _inverse_given_bwd(t, g):
    tb = t.astype(BF16)
    return -_dot_nt(_dot_tn(tb, g.astype(BF16)).astype(BF16), tb), jnp.zeros_like(t)


_inverse_given.defvjp(_inverse_given_fwd, _inverse_given_bwd)


@jax.custom_vjp
def _dot_high(a, b):
    return _dot(a, b, HIGH)


def _dot_high_fwd(a, b):
    return _dot(a, b, HIGH), (a, b)


def _dot_high_bwd(res, g):
    a, b = res
    gb = g.astype(BF16)
    return _dot_nt(gb, b.astype(BF16)), _dot_tn(a.astype(BF16), gb)


_dot_high.defvjp(_dot_high_fwd, _dot_high_bwd)


def gdn_local(qs, ks, vs, gcs, grs, bcs, rev, t_known=None):
    c = qs[0].shape[0]
    ii = _iota((c, c), 0)
    jj = _iota((c, c), 1)
    incl = (ii <= jj) if rev else (ii >= jj)
    strict = (ii < jj) if rev else (ii > jj)
    decay = _each(lambda gc, gr: jnp.exp(jnp.where(incl, gc - gr, NEG)), gcs, grs)
    kb = _each(lambda k, bc: k * bc, ks, bcs)
    a = _each(lambda kb_, k, dc: jnp.where(strict, _bdot_nt(kb_, k) * dc, 0.0), kb, ks, decay)
    t = _tri_inverse_all(a) if t_known is None else _each(_inverse_given, a, t_known)
    eg = [jnp.exp(gc) for gc in gcs]
    rhs = _each(lambda kb_, eg_, v, bc: jnp.concatenate([kb_ * eg_, v * bc], axis=1), kb, eg, vs, bcs)
    wu = _each(_dot_high, t, rhs)
    lhs = _each(lambda wu_, q, eg_: jnp.concatenate([wu_[:, :DK], q * eg_], axis=0), wu, qs, eg)
    attn = _each(lambda q, k, dc: _bdot_nt(q, k) * dc, qs, ks, decay)
    return wu, attn, lhs, t


def gdn_state(ss, wu, attn, lhs, ks, gcs, rev):
    c = ks[0].shape[0]
    is_last = _iota((c, 1), 0) == (0 if rev else c - 1)
    ws = _each(_bdot, lhs, ss)
    v_new = _each(lambda wu_, ws_: wu_[:, DK:] - ws_[:c], wu, ws)
    o = _each(lambda ws_, at, vn: ws_[c:] + _bdot(at, vn), ws, attn, v_new)
    gtot = [jnp.sum(jnp.where(is_last, gc, 0.0), axis=0, keepdims=True) for gc in gcs]
    s_new = _each(lambda s, k, gc, gt_, vn: s * jnp.exp(gt_) + _bdot_tn(k * jnp.exp(gt_ - gc), vn), ss, ks, gcs, gtot, v_new)
    return s_new, o


def gdn_chunk(ss, qs, ks, vs, gcs, grs, bcs, rev, t_known=None):
    wu, attn, lhs, t = gdn_local(qs, ks, vs, gcs, grs, bcs, rev, t_known)
    s_new, o = gdn_state(ss, wu, attn, lhs, ks, gcs, rev)
    return s_new, o, t


def ssd_pick(dt0s, dt1s, ac0s, ac1s, ar0s, ar1s):
    lo = _iota((CH, 128), 1) < S_P
    pick = lambda u0, u1: jnp.where(lo, u0, u1)
    return _each(pick, dt0s, dt1s), _each(pick, ac0s, ac1s), _each(pick, ar0s, ar1s)


def ssd_local(xs, dts, acs, acr, bgs, cgs, rev):
    c = xs[0].shape[0]
    npair = len(xs)
    grp = [p * len(bgs) // npair for p in range(npair)]
    lane = _iota((c, 128), 1)
    ii = _iota((c, 128), 0)
    jl = lane & (S_P - 1)
    lo = lane < S_P
    seen = (ii <= jl) if rev else (ii >= jl)
    last = 0 if rev else c - 1
    split = lambda z: jnp.concatenate([jnp.where(lo, z, 0.0), jnp.where(lo, 0.0, z)], axis=0)
    cb = _each(lambda bg, cg: _bdot_nt(cg, jnp.concatenate([bg, bg], axis=0)), bgs, cgs)
    seg = _each(lambda ac, ar: jnp.exp(jnp.where(seen, ac - ar, NEG)), acs, acr)
    xdt = _each(lambda x, dt: x * dt, xs, dts)
    ydiag = [_bdot(cb[grp[p]] * seg[p], split(xdt[p])) for p in range(npair)]
    eac = [jnp.exp(ac) for ac in acs]
    atot = [jnp.sum(jnp.where(ii == last, ac, 0.0), axis=0, keepdims=True) for ac in acs]
    upd = [_bdot_tn(bgs[grp[p]], xdt[p] * jnp.exp(atot[p] - acs[p])) for p in range(npair)]
    return ydiag, eac, [jnp.exp(at) for at in atot], upd


def ssd_state(hts, ydiag, eac, etot, upd, cgs):
    npair = len(hts)
    grp = [p * len(cgs) // npair for p in range(npair)]
    y = [ydiag[p] + _bdot(cgs[grp[p]], hts[p]) * eac[p] for p in range(npair)]
    h_new = [hts[p] * etot[p] + upd[p] for p in range(npair)]
    return h_new, y


def ssd_chunk(hts, xs, dts, acs, acr, bgs, cgs, rev):
    ydiag, eac, etot, upd = ssd_local(xs, dts, acs, acr, bgs, cgs, rev)
    return ssd_state(hts, ydiag, eac, etot, upd, cgs)


def f_adamw(w, g, m, v):
    m = ADAM_B1 * m + (1.0 - ADAM_B1) * g
    v = ADAM_B2 * v + (1.0 - ADAM_B2) * jnp.square(g)
    m_hat = m / (1.0 - ADAM_B1 ** ADAM_STEP)
    v_hat = v / (1.0 - ADAM_B2 ** ADAM_STEP)
    delta = -ADAM_LR * (m_hat / (jnp.sqrt(v_hat) + ADAM_EPS) + ADAM_WD * w)
    return delta, m, v


def _cparams(sem):
    return pltpu.CompilerParams(dimension_semantics=sem, vmem_limit_bytes=VMEM_LIMIT)


def _pick(n, target):
    if n <= target:
        return n
    best = None
    for t in range(128, target + 1, 128):
        if n % t == 0:
            best = t
    assert best is not None, (n, target)
    return best


def _row_spec(tm, width, colblk, rowoff):
    return pl.BlockSpec((tm, width), lambda i: (i + rowoff, colblk))


def _par_spec(shape):
    nd = len(shape)
    return pl.BlockSpec(tuple(shape), lambda i: (0,) * nd)


def _rowwise(name, fn, rows, pars, outs, ntiles, base=1, tm=TM):
    nr, npar = len(rows), len(pars)

    def body(*refs):
        is_ctx = (pl.program_id(0) + base) == 0
        res = fn(is_ctx, *[r[...] for r in refs[:nr]], *[p[...] for p in refs[nr:nr + npar]])
        for o_ref, r in zip(refs[nr + npar:], res):
            o_ref[...] = r.astype(o_ref.dtype)

    return pl.pallas_call(
        body, grid=(ntiles,), name=name,
        in_specs=[_ct_spec(tm, d) for d in rows] + [_par_spec(p.shape) for p in pars],
        out_specs=[_row_spec(tm, wd, 0, 0) for (wd, _) in outs],
        out_shape=[jax.ShapeDtypeStruct((ntiles * tm, wd), dt) for (wd, dt) in outs],
        compiler_params=_cparams(("arbitrary",)),
    )(*[r[0] for r in rows], *pars)


def _ct_spec(tm, desc):
    _, wd, cb, ro = desc[:4]
    flag = desc[4] if len(desc) > 4 else False
    if flag == "first":
        return pl.BlockSpec((tm, wd), lambda i: (0, cb))
    if flag:
        return pl.BlockSpec((tm, wd), lambda i: (jnp.maximum(i + ro, 0), cb))
    return _row_spec(tm, wd, cb, ro)


def _rowwise_bwd(name, fn, rows, pars, cts, drows, dpars, ntiles, base=1, loss_out=False, tm=TM):
    nr, npar = len(rows), len(pars)
    ct_rows = [d for ct in cts if isinstance(ct, list) for d in ct]
    nct = len(ct_rows)

    def body(*refs):
        i = pl.program_id(0)
        is_ctx = (i + base) == 0
        rows_v = [r[...] for r in refs[:nr]]
        pars_v = [p[...] for p in refs[nr:nr + npar]]
        ct_refs = list(refs[nr + npar:nr + npar + nct])
        out_refs = list(refs[nr + npar + nct:])
        outs, vjp = jax.vjp(lambda rv, pv: fn(is_ctx, *rv, *pv), rows_v, pars_v)

        def ct_value(desc):
            val = ct_refs.pop(0)[...].astype(F32)
            if len(desc) > 4 and desc[4]:
                val = jnp.where(is_ctx, 0.0, val)
            return val

        ct_vals = []
        for o, ct in zip(outs, cts):
            if ct is None:
                ct_vals.append(jnp.zeros_like(o))
            elif isinstance(ct, str):
                ct_vals.append(jnp.ones_like(o))
            else:
                acc = ct_value(ct[0])
                for desc in ct[1:]:
                    acc = acc + ct_value(desc)
                ct_vals.append(acc.astype(o.dtype))
        d_rows, d_pars = vjp(tuple(ct_vals))
        for (ri, _), o_ref in zip(drows, out_refs[:len(drows)]):
            o_ref[...] = d_rows[ri].astype(o_ref.dtype)
        acc_refs = out_refs[len(drows):]
        acc_vals = [d_pars[pi] for pi in dpars]
        if loss_out:
            acc_vals.append(jnp.broadcast_to(outs[0], (8, 128)))

        @pl.when(i == 0)
        def _():
            for o_ref, val in zip(acc_refs, acc_vals):
                o_ref[...] = val

        @pl.when(i > 0)
        def _():
            for o_ref, val in zip(acc_refs, acc_vals):
                o_ref[...] += val

    acc_shapes = [pars[pi].shape for pi in dpars] + ([(8, 128)] if loss_out else [])
    lat_row = lambda ri: len(rows[ri]) > 4 and rows[ri][4] is True
    return pl.pallas_call(
        body, grid=(ntiles,), name=name,
        in_specs=[_ct_spec(tm, d) for d in rows] + [_par_spec(p.shape) for p in pars]
        + [_ct_spec(tm, d) for d in ct_rows],
        out_specs=[_ct_spec(tm, (None, rows[ri][1], 0, -1, True) if lat_row(ri) else (None, rows[ri][1], 0, 0))
                   for (ri, _) in drows] + [_par_spec(s) for s in acc_shapes],
        out_shape=[jax.ShapeDtypeStruct(((ntiles - int(lat_row(ri))) * tm, rows[ri][1]), dt) for (ri, dt) in drows]
        + [jax.ShapeDtypeStruct(tuple(s), F32) for s in acc_shapes],
        compiler_params=_cparams(("arbitrary",)),
    )(*[r[0] for r in rows], *pars, *[r[0] for r in ct_rows])


def _mm(name, a, b, mode, out_dtype, tm=1024, tn=1024, tk=1024, ride=None):
    if mode == "nn":
        (m, kd), (_, n) = a.shape, b.shape
    elif mode == "nt":
        (m, kd), (n, _) = a.shape, b.shape
    else:
        (kd, m), (_, n) = a.shape, b.shape
    tm, tn, tk = _pick(m, tm), _pick(n, tn), _pick(kd, tk)
    nk = kd // tk
    a_spec = {"nn": pl.BlockSpec((tm, tk), lambda i, j, k: (i, k)), "nt": pl.BlockSpec((tm, tk), lambda i, j, k: (i, k)),
              "tn": pl.BlockSpec((tk, tm), lambda i, j, k: (k, i))}[mode]
    b_spec = {"nn": pl.BlockSpec((tk, tn), lambda i, j, k: (k, j)), "nt": pl.BlockSpec((tn, tk), lambda i, j, k: (j, k)),
              "tn": pl.BlockSpec((tk, tn), lambda i, j, k: (k, j))}[mode]
    dot = {"nn": _dot, "nt": _dot_nt, "tn": _dot_tn}[mode]

    if nk == 1:
        def body(a_ref, b_ref, o_ref):
            o_ref[...] = dot(a_ref[...].astype(BF16), b_ref[...].astype(BF16)).astype(o_ref.dtype)
    else:
        def body(a_ref, b_ref, o_ref, acc_ref):
            k = pl.program_id(2)
            part = dot(a_ref[...].astype(BF16), b_ref[...].astype(BF16))

            @pl.when(k == 0)
            def _():
                acc_ref[...] = part

            @pl.when((k > 0) & (k < nk - 1))
            def _():
                acc_ref[...] += part

            @pl.when(k == nk - 1)
            def _():
                o_ref[...] = (acc_ref[...] + part).astype(o_ref.dtype)

    grid = (m // tm, n // tn, nk)
    at = lambda pos: functools.reduce(jnp.logical_and, [pl.program_id(ax) == pos(g) for ax, g in enumerate(grid)])
    r_in, r_out, r_shapes, r_scr, r_ops = _ride_args(ride)
    res = pl.pallas_call(
        _ride(body, ride, 2, 1, lambda: at(lambda g: 0), lambda: at(lambda g: g - 1)), grid=grid, name=name,
        in_specs=[a_spec, b_spec] + r_in,
        out_specs=[pl.BlockSpec((tm, tn), lambda i, j, k: (i, j))] + r_out,
        out_shape=[jax.ShapeDtypeStruct((m, n), out_dtype)] + r_shapes,
        scratch_shapes=([] if nk == 1 else [pltpu.VMEM((tm, tn), F32)]) + r_scr,
        compiler_params=_cparams(("arbitrary", "arbitrary", "arbitrary")),
    )(a, b, *r_ops)
    return res[0] if ride is None else (res[0], res[1:])


def _chunk_index(i, nch, nctx, rev):
    if not rev:
        return i
    return jnp.where(i < nctx, nctx - 1 - i, nch + nctx - 1 - i)


def _gdn_cols(d):
    return [d * G_HEADS + h for h in range(G_HEADS)], [2 * G_HEADS + d * G_HEADS + h for h in range(G_HEADS)]


def _gdn_operands(q_ref, k_ref, v_ref, g_ref, b_ref, d, rows=slice(None)):
    cols_g, cols_b = _gdn_cols(d)
    sls = [slice(h * DK, (h + 1) * DK) for h in range(G_HEADS)]
    gt, bt = g_ref[rows, :], b_ref[rows, :]
    gtt = gt.T
    qs = [q_ref[rows, sl] for sl in sls]
    ks = [k_ref[rows, sl] for sl in sls]
    vs = [v_ref[rows, sl] for sl in sls]
    gcs = [gt[:, cg:cg + 1] for cg in cols_g]
    grs = [gtt[cg:cg + 1, :] for cg in cols_g]
    bcs = [bt[:, cb:cb + 1] for cb in cols_b]
    return sls, qs, ks, vs, gcs, grs, bcs


GDN_FWD_CHUNKS = 4


def _gdn_scan_fwd(name, q, k, v, gcum, beta, d, nctx, ride=None, add=None):
    t = q.shape[0]
    nch = t // CH
    per = GDN_FWD_CHUNKS
    nst = nch // per
    rev = d == 1
    bix = lambda i: _chunk_index(i, nst, nctx // per, rev)
    full = pl.BlockSpec((per * CH, D), lambda i: (bix(i), 0))
    wide = pl.BlockSpec((per * CH, 128), lambda i: (bix(i), 0))
    order = list(range(per - 1, -1, -1)) if rev else list(range(per))

    def body(q_ref, k_ref, v_ref, g_ref, b_ref, *rest):
        add_ref = rest[0] if add is not None else None
        o_ref, ss_ref, ts_ref, s_scr = rest[-4:]

        @pl.when(pl.program_id(0) == 0)
        def _():
            s_scr[...] = jnp.zeros(s_scr.shape, F32)

        ops = [_gdn_operands(q_ref, k_ref, v_ref, g_ref, b_ref, d, slice(c * CH, (c + 1) * CH)) for c in order]
        sls = ops[0][0]
        cat = [sum((o[i] for o in ops), []) for i in range(1, 7)]
        wu, attn, lhs, tinv = gdn_local(*cat, rev)
        ss = [s_scr[h] for h in range(G_HEADS)]
        for n, c in enumerate(order):
            sl = slice(n * G_HEADS, (n + 1) * G_HEADS)
            s_new, o = gdn_state(ss, wu[sl], attn[sl], lhs[sl], cat[1][sl], cat[3][sl], rev)
            for h in range(G_HEADS):
                ss_ref[n, h] = ss[h]
                ts_ref[n, h] = tinv[n * G_HEADS + h]
                rows = slice(c * CH, (c + 1) * CH)
                o_ref[rows, sls[h]] = o[h] if add is None else o[h] + add_ref[rows, sls[h]]
            ss = s_new
        for h in range(G_HEADS):
            s_scr[h] = ss[h]

    r_in, r_out, r_shapes, r_scr, r_ops = _ride_args(ride)
    extra = [] if add is None else [add]
    res = pl.pallas_call(
        _ride(body, ride, 5 + len(extra), 3, lambda: pl.program_id(0) == 0, lambda: pl.program_id(0) == nst - 1), grid=(nst,), name=name,
        in_specs=[full, full, full, wide, wide] + [full] * len(extra) + r_in,
        out_specs=[full, pl.BlockSpec((per, G_HEADS, DK, DK), lambda i: (i, 0, 0, 0)),
                   pl.BlockSpec((per, G_HEADS, CH, CH), lambda i: (i, 0, 0, 0))] + r_out,
        out_shape=[jax.ShapeDtypeStruct((t, D), F32), jax.ShapeDtypeStruct((nch, G_HEADS, DK, DK), F32),
                   jax.ShapeDtypeStruct((nch, G_HEADS, CH, CH), F32)] + r_shapes,
        scratch_shapes=[pltpu.VMEM((G_HEADS, DK, DK), F32)] + r_scr,
        compiler_params=_cparams(("arbitrary",)),
    )(q, k, v, gcum, beta, *extra, *r_ops)
    return res if ride is None else (*res[:3], res[3:])


def _gdn_scan_bwd(name, q, k, v, gcum, beta, ssave, tsave, do, d, nctx, ride=None):
    t = q.shape[0]
    nch = t // CH
    rev = d == 1
    cix = lambda ib: _chunk_index(nch - 1 - ib, nch, nctx, rev)
    full = pl.BlockSpec((CH, D), lambda ib: (cix(ib), 0))
    wide = pl.BlockSpec((CH, 128), lambda ib: (cix(ib), 0))
    do_spec = pl.BlockSpec((CH, D), lambda ib: (jnp.maximum(cix(ib), nctx) - nctx, 0))

    def body(q_ref, k_ref, v_ref, g_ref, b_ref, ss_ref, ts_ref, do_ref, dq_ref, dk_ref, dv_ref, dg_ref, db_ref, ds_scr):
        ib = pl.program_id(0)

        @pl.when(ib == 0)
        def _():
            ds_scr[...] = jnp.zeros(ds_scr.shape, F32)

        sls, qs, ks, vs, gcs, grs, bcs = _gdn_operands(q_ref, k_ref, v_ref, g_ref, b_ref, d)
        t_known = [ts_ref[0, h] for h in range(G_HEADS)]
        cols_g, cols_b = _gdn_cols(d)
        is_lat = cix(ib) >= nctx
        ss = [ss_ref[0, h] for h in range(G_HEADS)]
        do_v = [jnp.where(is_lat, do_ref[:, sl], 0.0) for sl in sls]
        ds_in = [ds_scr[h] for h in range(G_HEADS)]
        _, vjp = jax.vjp(lambda *a: gdn_chunk(*a, rev, t_known)[:2], ss, qs, ks, vs, gcs, grs, bcs)
        ds, dq, dk, dv, dgc, dgr, dbc = vjp((ds_in, do_v))
        lane = _iota((CH, 128), 1)
        sub = _iota((128, CH), 0)
        dg = jnp.zeros((CH, 128), F32)
        dgt = jnp.zeros((128, CH), F32)
        db = jnp.zeros((CH, 128), F32)
        for h in range(G_HEADS):
            ds_scr[h] = ds[h]
            dq_ref[:, sls[h]] = dq[h]
            dk_ref[:, sls[h]] = dk[h]
            dv_ref[:, sls[h]] = dv[h]
            dg = dg + jnp.where(lane == cols_g[h], dgc[h], 0.0)
            dgt = dgt + jnp.where(sub == cols_g[h], dgr[h], 0.0)
            db = db + jnp.where(lane == cols_b[h], dbc[h], 0.0)
        dg_ref[...] = dg + dgt.T
        db_ref[...] = db

    r_in, r_out, r_shapes, r_scr, r_ops = _ride_args(ride)
    res = pl.pallas_call(
        _ride(body, ride, 8, 5, lambda: pl.program_id(0) == 0, lambda: pl.program_id(0) == nch - 1), grid=(nch,), name=name,
        in_specs=[full, full, full, wide, wide,
                  pl.BlockSpec((1, G_HEADS, DK, DK), lambda ib: (nch - 1 - ib, 0, 0, 0)),
                  pl.BlockSpec((1, G_HEADS, CH, CH), lambda ib: (nch - 1 - ib, 0, 0, 0)), do_spec] + r_in,
        out_specs=[full, full, full, wide, wide] + r_out,
        out_shape=[jax.ShapeDtypeStruct((t, D), F32)] * 3 + [jax.ShapeDtypeStruct((t, 128), F32)] * 2 + r_shapes,
        scratch_shapes=[pltpu.VMEM((G_HEADS, DK, DK), F32)] + r_scr,
        compiler_params=_cparams(("arbitrary",)),
    )(q, k, v, gcum, beta, ssave, tsave, do, *r_ops)
    return res if ride is None else (*res[:5], res[5:])


N_PAIRS = S_HEADS // 2


def _ssd_operands(x_ref, dt_ref, ac_ref, b_ref, c_ref, d, rows=slice(None)):
    sls = [slice(p * 128, (p + 1) * 128) for p in range(N_PAIRS)]
    gsl = [slice(g * S_N, (g + 1) * S_N) for g in range(S_GROUPS)]
    cols = [d * S_HEADS + h for h in range(S_HEADS)]
    dtc, acc = dt_ref[rows, :], ac_ref[rows, :]
    act = jnp.concatenate([acc, acc], axis=0).T
    col = lambda z, cc: z[:, cc:cc + 1]
    dts, acs, acr = ssd_pick(
        [col(dtc, cols[2 * p]) for p in range(N_PAIRS)], [col(dtc, cols[2 * p + 1]) for p in range(N_PAIRS)],
        [col(acc, cols[2 * p]) for p in range(N_PAIRS)], [col(acc, cols[2 * p + 1]) for p in range(N_PAIRS)],
        [act[cols[2 * p]:cols[2 * p] + 1, :] for p in range(N_PAIRS)],
        [act[cols[2 * p + 1]:cols[2 * p + 1] + 1, :] for p in range(N_PAIRS)])
    ops = ([x_ref[rows, sl] for sl in sls], dts, acs, acr, [b_ref[rows, gs] for gs in gsl], [c_ref[rows, gs] for gs in gsl])
    return sls, gsl, cols, ops


SSD_FWD_CHUNKS = 4


def _ssd_scan_fwd(name, xs, dtc, acc, bm, cm, d, nctx, add=None):
    t = xs.shape[0]
    nch = t // CH
    per = SSD_FWD_CHUNKS
    nst = nch // per
    rev = d == 1
    bix = lambda i: _chunk_index(i, nst, nctx // per, rev)
    inner = pl.BlockSpec((per * CH, S_INNER), lambda i: (bix(i), 0))
    wide = pl.BlockSpec((per * CH, 128), lambda i: (bix(i), 0))
    grp = pl.BlockSpec((per * CH, S_GROUPS * S_N), lambda i: (bix(i), 0))
    order = list(range(per - 1, -1, -1)) if rev else list(range(per))

    def body(x_ref, dt_ref, ac_ref, b_ref, c_ref, *rest):
        add_ref = rest[0] if add is not None else None
        y_ref, hs_ref, h_scr = rest[-3:]

        @pl.when(pl.program_id(0) == 0)
        def _():
            h_scr[...] = jnp.zeros(h_scr.shape, F32)

        loc, cgs, sls = [], [], None
        for c in order:
            sls, _, _, ops = _ssd_operands(x_ref, dt_ref, ac_ref, b_ref, c_ref, d, slice(c * CH, (c + 1) * CH))
            loc.append(ssd_local(*ops, rev))
            cgs.append(ops[5])
        hts = [h_scr[p] for p in range(N_PAIRS)]
        for n, c in enumerate(order):
            h_new, y = ssd_state(hts, *loc[n], cgs[n])
            for p in range(N_PAIRS):
                hs_ref[n, p] = hts[p]
                rows = slice(c * CH, (c + 1) * CH)
                y_ref[rows, sls[p]] = y[p] if add is None else y[p] + add_ref[rows, sls[p]]
            hts = h_new
        for p in range(N_PAIRS):
            h_scr[p] = hts[p]

    return pl.pallas_call(
        body, grid=(nst,), name=name,
        in_specs=[inner, wide, wide, grp, grp] + ([] if add is None else [inner]),
        out_specs=[inner, pl.BlockSpec((per, N_PAIRS, S_N, 128), lambda i: (i, 0, 0, 0))],
        out_shape=[jax.ShapeDtypeStruct((t, S_INNER), F32), jax.ShapeDtypeStruct((nch, N_PAIRS, S_N, 128), F32)],
        scratch_shapes=[pltpu.VMEM((N_PAIRS, S_N, 128), F32)],
        compiler_params=_cparams(("arbitrary",)),
    )(xs, dtc, acc, bm, cm, *([] if add is None else [add]))


def _ssd_scan_bwd(name, xs, dtc, acc, bm, cm, hsave, dy, d, nctx):
    t = xs.shape[0]
    nch = t // CH
    rev = d == 1
    cix = lambda ib: _chunk_index(nch - 1 - ib, nch, nctx, rev)
    inner = pl.BlockSpec((CH, S_INNER), lambda ib: (cix(ib), 0))
    wide = pl.BlockSpec((CH, 128), lambda ib: (cix(ib), 0))
    grp = pl.BlockSpec((CH, S_GROUPS * S_N), lambda ib: (cix(ib), 0))
    dy_spec = pl.BlockSpec((CH, S_INNER), lambda ib: (jnp.maximum(cix(ib), nctx) - nctx, 0))

    def body(x_ref, dt_ref, ac_ref, b_ref, c_ref, hs_ref, dy_ref, dx_ref, ddt_ref, dac_ref, db_ref, dc_ref, dh_scr):
        ib = pl.program_id(0)

        @pl.when(ib == 0)
        def _():
            dh_scr[...] = jnp.zeros(dh_scr.shape, F32)

        sls, gsl, cols, ops = _ssd_operands(x_ref, dt_ref, ac_ref, b_ref, c_ref, d)
        is_lat = cix(ib) >= nctx
        hts = [hs_ref[0, p] for p in range(N_PAIRS)]
        dy_v = [jnp.where(is_lat, dy_ref[:, sl], 0.0) for sl in sls]
        dh_in = [dh_scr[p] for p in range(N_PAIRS)]
        _, vjp = jax.vjp(lambda *a: ssd_chunk(*a, rev), hts, *ops)
        dh, dx, ddts, dacs, dacr, db, dc = vjp((dh_in, dy_v))
        for p in range(N_PAIRS):
            dh_scr[p] = dh[p]
            dx_ref[:, sls[p]] = dx[p]
        r = _iota((S_INNER, 128), 0)
        e_t = jnp.where(_iota((S_INNER, 128), 1) == d * S_HEADS + r // S_P, 1.0, 0.0).astype(F32)
        e_b = e_t.astype(BF16)

        def lane_sums(z):
            hi = z.astype(BF16)
            lo = (z - hi.astype(F32)).astype(BF16)
            return _dot(hi, e_b) + _dot(lo, e_b)

        ddt_ref[...] = lane_sums(jnp.concatenate(ddts, axis=1))
        dac_cols = lane_sums(jnp.concatenate(dacs, axis=1))
        sub = _iota((128, 128), 0)
        lane = _iota((128, 128), 1)
        m = jnp.zeros((128, 128), F32)
        for p in range(N_PAIRS):
            m = m + jnp.where(sub == p, jnp.sum(dacr[p], axis=0, keepdims=True), 0.0)
        mt = m.T
        s0 = jnp.where(lane == d * S_HEADS + 2 * sub, 1.0, 0.0).astype(F32)
        s1 = jnp.where(lane == d * S_HEADS + 2 * sub + 1, 1.0, 0.0).astype(F32)
        dac_ref[...] = dac_cols + _dot(mt[:CH], s0, HIGH) + _dot(mt[CH:], s1, HIGH)
        for g in range(S_GROUPS):
            db_ref[:, gsl[g]] = db[g]
            dc_ref[:, gsl[g]] = dc[g]

    return pl.pallas_call(
        body, grid=(nch,), name=name,
        in_specs=[inner, wide, wide, grp, grp,
                  pl.BlockSpec((1, N_PAIRS, S_N, 128), lambda ib: (nch - 1 - ib, 0, 0, 0)), dy_spec],
        out_specs=[inner, wide, wide, grp, grp],
        out_shape=[jax.ShapeDtypeStruct((t, S_INNER), F32)] + [jax.ShapeDtypeStruct((t, 128), F32)] * 2
        + [jax.ShapeDtypeStruct((t, S_GROUPS * S_N), F32)] * 2,
        scratch_shapes=[pltpu.VMEM((N_PAIRS, S_N, 128), F32)],
        compiler_params=_cparams(("arbitrary",)),
    )(xs, dtc, acc, bm, cm, hsave, dy)


def _mesh_pos():
    return lax.axis_index("x"), lax.axis_index("y"), lax.axis_index("c")


def _hbm_specs(n):
    return [pl.BlockSpec(memory_space=pl.ANY)] * n


def _sem_shapes(nw):
    return [pltpu.SemaphoreType.DMA((nw, 7)), pltpu.SemaphoreType.DMA((nw, 7)), pltpu.SemaphoreType.DMA((nw,))]


class _AllGather:
    def __init__(self, shards):
        self.arrays = list(shards)
        self.out_shapes = [jax.ShapeDtypeStruct((N_DEV,) + xs.shape, xs.dtype) for xs in shards]

    def _parts(self, x_refs, out_refs, sems):
        send_sems, recv_sems, local_sems = sems
        nw = len(x_refs)
        x, y, c = _mesh_pos()
        me, sibling = (x, y, c), (x, y, 1 - c)
        chips = [(1 - x, y), (x, 1 - y), (1 - x, 1 - y)]

        def slot(w, px, py, pc):
            return out_refs[w].at[4 * px + 2 * py + pc]

        def copy(w, k, block, to, src=None):
            return pltpu.make_async_remote_copy(
                src_ref=slot(w, *block) if src is None else src, dst_ref=slot(w, *block),
                send_sem=send_sems.at[w, k], recv_sem=recv_sems.at[w, k], device_id=to, device_id_type=MESH)

        mine = [pltpu.make_async_copy(x_refs[w], slot(w, *me), local_sems.at[w]) for w in range(nw)]
        first = []
        for w in range(nw):
            first.append(copy(w, 0, me, sibling, src=x_refs[w]))
            first += [copy(w, 1 + j, me, (*chip, c), src=x_refs[w]) for j, chip in enumerate(chips)]
        return nw, me, sibling, chips, c, copy, mine, first

    def start(self, x_refs, out_refs, sems):
        _, _, _, _, _, _, mine, first = self._parts(x_refs, out_refs, sems)
        for cp in mine + first:
            cp.start()

    def finish(self, x_refs, out_refs, sems):
        nw, me, sibling, chips, c, copy, mine, first = self._parts(x_refs, out_refs, sems)
        passed = []
        for j, chip in enumerate(chips):
            for w in range(nw):
                copy(w, 1 + j, (*chip, c), me).wait_recv()
                fwd = copy(w, 4 + j, (*chip, c), sibling)
                fwd.start()
                passed.append(fwd)
        for w in range(nw):
            copy(w, 0, sibling, me).wait_recv()
            for j, chip in enumerate(chips):
                copy(w, 4 + j, (*chip, 1 - c), me).wait_recv()
        for cp in first + passed:
            cp.wait_send()
        for cp in mine:
            cp.wait()


class _AllToAll:
    def __init__(self, blocks):
        self.arrays = list(blocks)
        self.out_shapes = [jax.ShapeDtypeStruct(g.shape, g.dtype) for g in blocks]

    def _parts(self, g_refs, out_refs, sems):
        send_sems, recv_sems, local_sems = sems
        nw = len(g_refs)
        x, y, c = _mesh_pos()
        me_i = 4 * x + 2 * y + c
        mine = [pltpu.make_async_copy(g_refs[w].at[me_i], out_refs[w].at[me_i], local_sems.at[w]) for w in range(nw)]
        cps = []
        for k in range(1, N_DEV):
            px = 1 - x if (k >> 2) & 1 else x
            py = 1 - y if (k >> 1) & 1 else y
            pc = 1 - c if k & 1 else c
            for w in range(nw):
                cps.append(pltpu.make_async_remote_copy(
                    src_ref=g_refs[w].at[4 * px + 2 * py + pc], dst_ref=out_refs[w].at[me_i],
                    send_sem=send_sems.at[w, k - 1], recv_sem=recv_sems.at[w, k - 1],
                    device_id=(px, py, pc), device_id_type=MESH))
        return mine, cps

    def start(self, g_refs, out_refs, sems):
        mine, cps = self._parts(g_refs, out_refs, sems)
        for cp in mine + cps:
            cp.start()

    def finish(self, g_refs, out_refs, sems):
        mine, cps = self._parts(g_refs, out_refs, sems)
        for cp in cps + mine:
            cp.wait()


def _exchange(name, ex):
    nw = len(ex.arrays)

    def body(*refs):
        ins, outs, sems = refs[:nw], refs[nw:2 * nw], refs[2 * nw:]
        ex.start(ins, outs, sems)
        ex.finish(ins, outs, sems)

    return pl.pallas_call(body, name=name, out_shape=ex.out_shapes, in_specs=_hbm_specs(nw), out_specs=_hbm_specs(nw),
                          scratch_shapes=_sem_shapes(nw))(*ex.arrays)


def _ride(body, ex, n_in, n_out, is_first, is_last):
    if ex is None:
        return body
    nw = len(ex.arrays)

    def riding(*refs):
        ins, ex_in = refs[:n_in], refs[n_in:n_in + nw]
        outs = refs[n_in + nw:n_in + nw + n_out]
        ex_out = refs[n_in + nw + n_out:n_in + 2 * nw + n_out]
        rest = refs[n_in + 2 * nw + n_out:]
        scratch, sems = rest[:len(rest) - 3], rest[len(rest) - 3:]

        @pl.when(is_first())
        def _():
            ex.start(ex_in, ex_out, sems)

        body(*ins, *outs, *scratch)

        @pl.when(is_last())
        def _():
            ex.finish(ex_in, ex_out, sems)

    return riding


def _ride_args(ex):
    if ex is None:
        return [], [], [], [], []
    nw = len(ex.arrays)
    return _hbm_specs(nw), _hbm_specs(nw), list(ex.out_shapes), _sem_shapes(nw), list(ex.arrays)


def _reduce_adam(name, recv, w, m, v, tm):
    rows, width = w.shape
    nslot = recv.shape[0]

    def body(recv_ref, w_ref, m_ref, v_ref, g_ref, d_ref, m2_ref, v2_ref):
        g = recv_ref[0].astype(F32)
        for s in range(1, nslot):
            g = g + recv_ref[s].astype(F32)
        delta, m2, v2 = f_adamw(w_ref[...], g, m_ref[...], v_ref[...])
        g_ref[...] = g
        d_ref[...] = delta
        m2_ref[...] = m2
        v2_ref[...] = v2

    row = pl.BlockSpec((tm, width), lambda i: (i, 0))
    return pl.pallas_call(
        body, grid=(rows // tm,), name=name,
        in_specs=[pl.BlockSpec((nslot, tm, width), lambda i: (0, i, 0)), row, row, row],
        out_specs=[row] * 4,
        out_shape=[jax.ShapeDtypeStruct((rows, width), F32)] * 4,
        compiler_params=_cparams(("arbitrary",)),
    )(recv, w, m, v)


BIG = ("w_in", "ada_w", "w_br_gdn", "w_br_ssm", "w_out", "w_ffn_in", "w_ffn_out")
BIG_FIRST = ("ada_w", "w_in")
BIG_LATE = ("w_br_gdn", "w_br_ssm", "w_out", "w_ffn_in", "w_ffn_out")
BIG_COL_SHARDED = ("w_in", "ada_w", "w_ffn_in")
BIG_ADAM_ROWS = dict(w_in=128, ada_w=256, w_br_gdn=128, w_br_ssm=256, w_out=128, w_ffn_in=256, w_ffn_out=352)
CONV = ("gdn_conv_w", "ssm_conv_w")
SMALL = ("c_ctx", "ada_b", "norm1_w", "gdn_conv_b", "gdn_a_log", "gdn_dt_bias", "gdn_norm_w", "ssm_conv_b",
         "ssm_a_log", "ssm_dt_bias", "ssm_d", "ssm_norm_w", "norm2_w", "norm_f_w")
CONV_SHARD = XBC // N_DEV


def _to_rows(a):
    flat = a.reshape(-1)
    pad = (-flat.shape[0]) % PACK_W
    if pad:
        flat = jnp.pad(flat, (0, pad))
    return flat.reshape(-1, PACK_W)


def _pack(arrays, rows=None):
    buf = jnp.concatenate([_to_rows(a) for a in arrays], axis=0)
    if rows is not None and rows > buf.shape[0]:
        buf = jnp.pad(buf, ((0, rows - buf.shape[0]), (0, 0)))
    return buf


def _unpack(buf, shapes):
    out, r0 = [], 0
    for shp in shapes:
        n = 1
        for s in shp:
            n *= s
        nr = -(-n // PACK_W)
        out.append(buf[r0:r0 + nr].reshape(-1)[:n].reshape(shp))
        r0 += nr
    return out


def _full_from_blocks(blocks, col_sharded):
    _, r, c = blocks.shape
    if col_sharded:
        return jnp.transpose(blocks, (1, 0, 2)).reshape(r, N_DEV * c)
    return blocks.reshape(N_DEV * r, c)


def _blocks_from_full(full, col_sharded):
    if col_sharded:
        r, c = full.shape[0], full.shape[1] // N_DEV
        return jnp.transpose(full.reshape(r, N_DEV, c), (1, 0, 2))
    return full.reshape(N_DEV, full.shape[0] // N_DEV, full.shape[1])


def _pad_cols(a, n):
    return jnp.pad(a, ((0, 0), (0, n - a.shape[1])))


def _w_cat(w_in):
    return jnp.concatenate([
        w_in[:, O_QKV:O_ZG], w_in[:, O_XBC:O_DT], w_in[:, O_ZS:O_XBC], w_in[:, O_GATE:O_END], w_in[:, O_ZG:O_AB],
        _pad_cols(w_in[:, O_AB:O_ZS], 128), _pad_cols(w_in[:, O_DT:O_GATE], 128)], axis=1)


def _w_uncat(wc):
    return jnp.concatenate([
        wc[:, C_QKV:C_XBC], wc[:, C_ZG:C_AB], wc[:, C_AB:C_AB + (O_ZS - O_AB)], wc[:, C_ZS:C_GATE], wc[:, C_XBC:C_ZS],
        wc[:, C_DT:C_DT + (O_GATE - O_DT)], wc[:, C_GATE:C_ZG]], axis=1)


def _pad_row(vec, n=128):
    vec = vec.reshape(1, -1)
    return _pad_cols(vec, n)


def kernel(x, c, ctx, c_ctx, ada_w, ada_b, norm1_w, w_in, gdn_conv_w, gdn_conv_b, gdn_a_log, gdn_dt_bias, gdn_norm_w, ssm_conv_w, ssm_conv_b, ssm_a_log, ssm_dt_bias, ssm_d, ssm_norm_w, w_br_gdn, w_br_ssm, w_out, norm2_w, w_ffn_in, w_ffn_out, norm_f_w, loss_target, m_c_ctx, m_ada_w, m_ada_b, m_norm1_w, m_w_in, m_gdn_conv_w, m_gdn_conv_b, m_gdn_a_log, m_gdn_dt_bias, m_gdn_norm_w, m_ssm_conv_w, m_ssm_conv_b, m_ssm_a_log, m_ssm_dt_bias, m_ssm_d, m_ssm_norm_w, m_w_br_gdn, m_w_br_ssm, m_w_out, m_norm2_w, m_w_ffn_in, m_w_ffn_out, m_norm_f_w, v_c_ctx, v_ada_w, v_ada_b, v_norm1_w, v_w_in, v_gdn_conv_w, v_gdn_conv_b, v_gdn_a_log, v_gdn_dt_bias, v_gdn_norm_w, v_ssm_conv_w, v_ssm_conv_b, v_ssm_a_log, v_ssm_dt_bias, v_ssm_d, v_ssm_norm_w, v_w_br_gdn, v_w_br_ssm, v_w_out, v_norm2_w, v_w_ffn_in, v_w_ffn_out, v_norm_f_w):
    wts = dict(c_ctx=c_ctx, ada_w=ada_w, ada_b=ada_b, norm1_w=norm1_w, w_in=w_in, gdn_conv_w=gdn_conv_w, gdn_conv_b=gdn_conv_b, gdn_a_log=gdn_a_log, gdn_dt_bias=gdn_dt_bias, gdn_norm_w=gdn_norm_w, ssm_conv_w=ssm_conv_w, ssm_conv_b=ssm_conv_b, ssm_a_log=ssm_a_log, ssm_dt_bias=ssm_dt_bias, ssm_d=ssm_d, ssm_norm_w=ssm_norm_w, w_br_gdn=w_br_gdn, w_br_ssm=w_br_ssm, w_out=w_out, norm2_w=norm2_w, w_ffn_in=w_ffn_in, w_ffn_out=w_ffn_out, norm_f_w=norm_f_w)
    mom1 = dict(c_ctx=m_c_ctx, ada_w=m_ada_w, ada_b=m_ada_b, norm1_w=m_norm1_w, w_in=m_w_in, gdn_conv_w=m_gdn_conv_w, gdn_conv_b=m_gdn_conv_b, gdn_a_log=m_gdn_a_log, gdn_dt_bias=m_gdn_dt_bias, gdn_norm_w=m_gdn_norm_w, ssm_conv_w=m_ssm_conv_w, ssm_conv_b=m_ssm_conv_b, ssm_a_log=m_ssm_a_log, ssm_dt_bias=m_ssm_dt_bias, ssm_d=m_ssm_d, ssm_norm_w=m_ssm_norm_w, w_br_gdn=m_w_br_gdn, w_br_ssm=m_w_br_ssm, w_out=m_w_out, norm2_w=m_norm2_w, w_ffn_in=m_w_ffn_in, w_ffn_out=m_w_ffn_out, norm_f_w=m_norm_f_w)
    mom2 = dict(c_ctx=v_c_ctx, ada_w=v_ada_w, ada_b=v_ada_b, norm1_w=v_norm1_w, w_in=v_w_in, gdn_conv_w=v_gdn_conv_w, gdn_conv_b=v_gdn_conv_b, gdn_a_log=v_gdn_a_log, gdn_dt_bias=v_gdn_dt_bias, gdn_norm_w=v_gdn_norm_w, ssm_conv_w=v_ssm_conv_w, ssm_conv_b=v_ssm_conv_b, ssm_a_log=v_ssm_a_log, ssm_dt_bias=v_ssm_dt_bias, ssm_d=v_ssm_d, ssm_norm_w=v_ssm_norm_w, w_br_gdn=v_w_br_gdn, w_br_ssm=v_w_br_ssm, w_out=v_w_out, norm2_w=v_norm2_w, w_ffn_in=v_w_ffn_in, w_ffn_out=v_w_ffn_out, norm_f_w=v_norm_f_w)
    order = list(wts)

    seq = x.shape[1]
    t = TM + seq
    ntl, nlt, nctx = t // TM, seq // TM, TM // CH

    me_i = 4 * lax.axis_index("x") + 2 * lax.axis_index("y") + lax.axis_index("c")
    conv_sh = _pack([wts[n] for n in CONV], rows=8)
    gathered = _exchange("ag_weights", _AllGather([wts[n][0].astype(BF16) for n in BIG_FIRST] + [conv_sh]))
    full = {n: _full_from_blocks(blk, n in BIG_COL_SHARDED) for n, blk in zip(BIG_FIRST, gathered)}
    late_gather = _AllGather([wts[n][0].astype(BF16) for n in BIG_LATE])
    conv_g = gathered[len(BIG_FIRST)].reshape(N_DEV, -1)
    ncv = 3 * CONV_SHARD
    for i, n in enumerate(CONV):
        off = -(-ncv // PACK_W) * PACK_W * i
        full[n] = jnp.transpose(conv_g[:, off:off + ncv].reshape(N_DEV, 3, CONV_SHARD), (1, 0, 2)).reshape(3, XBC)
    w_cat = _w_cat(full["w_in"])
    gcw = full["gdn_conv_w"].reshape(3, 1, XBC)
    scw = full["ssm_conv_w"].reshape(3, 1, XBC)

    n1w, n2w, nfw = norm1_w.reshape(1, D), norm2_w.reshape(1, D), norm_f_w.reshape(1, D)
    gcb, scb = gdn_conv_b.reshape(1, XBC), ssm_conv_b.reshape(1, XBC)
    alog16, dtb16 = _pad_row(gdn_a_log), _pad_row(gdn_dt_bias)
    alog64, dtb64 = _pad_row(ssm_a_log), _pad_row(ssm_dt_bias)
    gnw = gdn_norm_w.reshape(1, DK)
    ssd8 = jnp.tile(_pad_row(ssm_d), (8, 1))
    snw = ssm_norm_w.reshape(1, S_INNER)
    x2 = x[0]
    tgt = loss_target[0]
    cvec = jnp.concatenate([c, c_ctx.reshape(1, D), jnp.zeros((14, D), F32)], axis=0)

    a16 = _rowwise("silu_c", f_silu_rows, [(cvec, D, 0, 0)], [], [(D, BF16)], 1, tm=16)[0]
    mod = _mm("mm_mod", a16, full["ada_w"], "nn", F32) + ada_b
    sh1, sc1, g1, sh2, sc2, g2 = [mod[0:1, i * D:(i + 1) * D] for i in range(6)]
    csh1, csc1 = mod[1:2, 0:D], mod[1:2, D:2 * D]

    pre_pars = [n1w, sc1, sh1, csc1, csh1]
    pre_rows = [(x2, D, 0, -1, True), (ctx[0], D, 0, 0, "first")]
    a = _rowwise("pre", f_pre, pre_rows, pre_pars, [(D, BF16)], ntl, base=0)[0]
    proj = _mm("mm_proj", a, w_cat, "nn", F32, tm=1408, tn=1280)
    gp_rows = [(proj, XBC, C_QKV // XBC, 0), (proj, 128, C_AB // 128, 0)]
    gp_pars = [gcw, gcb, alog16, dtb16]
    q, k, v, gcum, beta = _rowwise("gdnprep", f_gdnprep, gp_rows, gp_pars, [(D, F32)] * 3 + [(128, F32)] * 2, ntl, base=0)
    sp_rows = [(proj, XBC, C_XBC // XBC, 0), (proj, 128, C_DT // 128, 0)]
    sp_pars = [scw, scb, alog64, dtb64]
    xs, bm, cm, dtc, acc = _rowwise(
        "ssmprep", f_ssmprep, sp_rows, sp_pars, [(S_INNER, F32), (512, F32), (512, F32), (128, F32), (128, F32)], ntl, base=0)
    o0, ss0, ts0, gathered = _gdn_scan_fwd("gdn_fwd0", q, k, v, gcum, beta, 0, nctx, ride=late_gather)
    full.update({n: _full_from_blocks(blk, n in BIG_COL_SHARDED) for n, blk in zip(BIG_LATE, gathered)})
    o_sum, ss1, ts1 = _gdn_scan_fwd("gdn_fwd1", q, k, v, gcum, beta, 1, nctx, add=o0)
    y0, hs0 = _ssd_scan_fwd("ssd_fwd0", xs, dtc, acc, bm, cm, 0, nctx)
    y_sum, hs1 = _ssd_scan_fwd("ssd_fwd1", xs, dtc, acc, bm, cm, 1, nctx, add=y0)
    post_rows = [(o_sum, D, 0, 1), (proj, D, C_ZG // D, 1), (y_sum, S_INNER, 0, 1), (xs, S_INNER, 0, 1),
                 (proj, S_INNER, C_ZS // S_INNER, 1)]
    post_pars = [gnw, ssd8, snw]
    og, ys = _rowwise("post", f_post, post_rows, post_pars, [(D, BF16), (S_INNER, BF16)], nlt)
    pg = _mm("mm_pg", og, full["w_br_gdn"], "nn", F32)
    ps = _mm("mm_ps", ys, full["w_br_ssm"], "nn", F32, tk=2048)
    merge_rows = [(proj, S_INNER, C_GATE // S_INNER, 1), (pg, D, 0, 0), (ps, D, 0, 0)]
    merged = _rowwise("merge", f_merge, merge_rows, [], [(D, BF16)], nlt)[0]
    mix = _mm("mm_mix", merged, full["w_out"], "nn", F32)
    res_rows = [(x2, D, 0, 0), (mix, D, 0, 0)]
    res_pars = [g1, n2w, sc2, sh2]
    h1, f = _rowwise("res1", f_res1, res_rows, res_pars, [(D, F32), (D, BF16)], nlt)
    u = _mm("mm_u", f, full["w_ffn_in"], "nn", F32, tn=1408)
    hact = _rowwise("act", f_act, [(u, 2 * D_FF, 0, 0)], [], [(D_FF, BF16)], nlt)[0]
    ff = _mm("mm_ff", hact, full["w_ffn_out"], "nn", F32, tk=2816)

    fin_rows = [(h1, D, 0, 0), (ff, D, 0, 0), (tgt, D, 0, 0)]
    d_h1a, d_ff, d_g2, d_nfw, loss_acc = _rowwise_bwd(
        "final", f_final, fin_rows, [g2, nfw], ["one"], [(0, F32), (1, BF16)], [0, 1], nlt, loss_out=True)
    d_hact = _mm("mm_dhact", d_ff, full["w_ffn_out"], "nt", BF16, tn=1408)
    g_w_ffn_out = _mm("mm_gwffo", hact, d_ff, "tn", BF16, tm=1408, tk=2048)
    d_u = _rowwise_bwd("act_bwd", f_act, [(u, 2 * D_FF, 0, 0)], [], [[(d_hact, D_FF, 0, 0)]], [(0, BF16)], [], nlt)[0]
    d_f = _mm("mm_df", d_u, full["w_ffn_in"], "nt", BF16, tk=2816)
    g_w_ffn_in = _mm("mm_gwffi", f, d_u, "tn", BF16, tn=1408, tk=2048)
    d_xres, d_mix, d_g1, d_n2w, d_sc2, d_sh2 = _rowwise_bwd(
        "res1_bwd", f_res1, res_rows, res_pars, [[(d_h1a, D, 0, 0)], [(d_f, D, 0, 0)]], [(0, F32), (1, BF16)], [0, 1, 2, 3], nlt)
    d_merged = _mm("mm_dmerged", d_mix, full["w_out"], "nt", BF16)
    g_w_out = _mm("mm_gwout", merged, d_mix, "tn", BF16, tk=2048)
    d_gate, d_pg, d_ps = _rowwise_bwd(
        "merge_bwd", f_merge, merge_rows, [], [[(d_merged, D, 0, 0)]], [(0, BF16), (1, BF16), (2, BF16)], [], nlt)
    d_og = _mm("mm_dog", d_pg, full["w_br_gdn"], "nt", BF16)
    g_w_br_gdn = _mm("mm_gwbrg", og, d_pg, "tn", BF16, tk=2048)
    d_ys = _mm("mm_dys", d_ps, full["w_br_ssm"], "nt", BF16, tn=2048)
    g_w_br_ssm = _mm("mm_gwbrs", ys, d_ps, "tn", BF16, tk=2048)
    d_o, d_zg, d_y, d_xs_post, d_zs, d_gnw, d_ssd8, d_snw = _rowwise_bwd(
        "post_bwd", f_post, post_rows, post_pars, [[(d_og, D, 0, 0)], [(d_ys, S_INNER, 0, 0)]],
        [(0, F32), (1, BF16), (2, F32), (3, F32), (4, BF16)], [0, 1, 2], nlt)
    late_grads = dict(w_br_gdn=g_w_br_gdn, w_br_ssm=g_w_br_ssm, w_out=g_w_out, w_ffn_in=g_w_ffn_in, w_ffn_out=g_w_ffn_out)
    late_a2a = _AllToAll([_blocks_from_full(late_grads[n], n in BIG_COL_SHARDED) for n in BIG_LATE])
    dq0, dk0, dv0, dg0, db0, recv_late = _gdn_scan_bwd("gdn_bwd0", q, k, v, gcum, beta, ss0, ts0, d_o, 0, nctx, ride=late_a2a)
    dq1, dk1, dv1, dg1, db1 = _gdn_scan_bwd("gdn_bwd1", q, k, v, gcum, beta, ss1, ts1, d_o, 1, nctx)
    dxs0, ddt0, dac0, dbm0, dcm0 = _ssd_scan_bwd("ssd_bwd0", xs, dtc, acc, bm, cm, hs0, d_y, 0, nctx)
    dxs1, ddt1, dac1, dbm1, dcm1 = _ssd_scan_bwd("ssd_bwd1", xs, dtc, acc, bm, cm, hs1, d_y, 1, nctx)
    row = lambda arr, wd: (arr, wd, 0, 0)
    d_qkv_raw, d_ab, d_gcw, d_gcb, d_alog16, d_dtb16 = _rowwise_bwd(
        "gdnprep_bwd", f_gdnprep, gp_rows, gp_pars,
        [[row(dq0, D), row(dq1, D)], [row(dk0, D), row(dk1, D)], [row(dv0, D), row(dv1, D)],
         [row(dg0, 128), row(dg1, 128)], [row(db0, 128), row(db1, 128)]],
        [(0, BF16), (1, BF16)], [0, 1, 2, 3], ntl, base=0)
    d_xbc_raw, d_dt, d_scw, d_scb, d_alog64, d_dtb64 = _rowwise_bwd(
        "ssmprep_bwd", f_ssmprep, sp_rows, sp_pars,
        [[row(dxs0, S_INNER), row(dxs1, S_INNER), (d_xs_post, S_INNER, 0, -1, True)], [row(dbm0, 512), row(dbm1, 512)],
         [row(dcm0, 512), row(dcm1, 512)], [row(ddt0, 128), row(ddt1, 128)], [row(dac0, 128), row(dac1, 128)]],
        [(0, BF16), (1, BF16)], [0, 1, 2, 3], ntl, base=0)
    ctx_zero = lambda wd: jnp.zeros((TM, wd), BF16)
    d_proj = jnp.concatenate([
        d_qkv_raw, d_xbc_raw, jnp.concatenate([ctx_zero(S_INNER), d_zs], axis=0),
        jnp.concatenate([ctx_zero(S_INNER), d_gate], axis=0), jnp.concatenate([ctx_zero(D), d_zg], axis=0), d_ab, d_dt], axis=1)
    g_w_cat = _mm("mm_gwcat", a, d_proj, "tn", BF16, tn=768, tk=2816)
    w_in_a2a = _AllToAll([_blocks_from_full(_w_uncat(g_w_cat), True)])
    d_a, recv_w_in = _mm("mm_da", d_proj, w_cat, "nt", BF16, tk=3840, ride=w_in_a2a)
    d_x, d_n1w, d_sc1, d_sh1, d_csc1, d_csh1 = _rowwise_bwd(
        "pre_bwd", f_pre_thru, pre_rows, pre_pars, [[row(d_a, D)], [(d_xres, D, 0, -1, True)]],
        [(0, F32)], [0, 1, 2, 3, 4], ntl, base=0)
    zero4 = jnp.zeros((1, 4 * D), F32)
    d_mod = jnp.concatenate([
        jnp.concatenate([d_sh1, d_sc1, d_g1, d_sh2, d_sc2, d_g2], axis=1),
        jnp.concatenate([d_csh1, d_csc1, zero4], axis=1), jnp.zeros((14, 6 * D), F32)], axis=0)
    d_a16 = _mm("mm_da16", d_mod, full["ada_w"], "nt", F32)
    d_cvec = _rowwise_bwd("silu_c_bwd", f_silu_rows, [(cvec, D, 0, 0)], [], [[row(d_a16, D)]], [(0, F32)], [], 1, tm=16)[0]

    recv = dict(zip(BIG_LATE, recv_late), w_in=recv_w_in[0])
    big_un = {n: _reduce_adam("adam_" + n, recv[n], wts[n][0], mom1[n][0], mom2[n][0], BIG_ADAM_ROWS[n])
              for n in BIG if n != "ada_w"}

    small_g = dict(c_ctx=d_cvec[1], ada_b=d_mod[0] + d_mod[1], norm1_w=d_n1w, gdn_conv_b=d_gcb,
                   gdn_a_log=d_alog16[0, :2 * G_HEADS], gdn_dt_bias=d_dtb16[0, :2 * G_HEADS], gdn_norm_w=d_gnw,
                   ssm_conv_b=d_scb, ssm_a_log=d_alog64[0, :2 * S_HEADS], ssm_dt_bias=d_dtb64[0, :2 * S_HEADS],
                   ssm_d=d_ssd8[0, :S_HEADS], ssm_norm_w=d_snw, norm2_w=d_n2w, norm_f_w=d_nfw,
                   gdn_conv_w=d_gcw.reshape(3, XBC), ssm_conv_w=d_scw.reshape(3, XBC))
    small_names = SMALL + CONV
    factors = [a16[0].astype(F32), d_mod[0], d_mod[1]]
    round8 = lambda r: -(-r // 8) * 8
    packed_rows = lambda arrs: sum(-(-arr.size // PACK_W) for arr in arrs)
    n_small = packed_rows([small_g[n] for n in small_names])
    n_fac = packed_rows(factors)
    rows_small = round8(n_small)
    sg_pack = _pack([small_g[n] for n in small_names] + factors, rows=round8(n_small + n_fac))
    recv_all = _exchange("ag_small_grads", _AllGather([sg_pack]))[0]
    recv_s = recv_all[:, :rows_small]
    fac = recv_all[:, n_small:n_small + n_fac].reshape(N_DEV, -1)
    my_cols = lambda z: lax.dynamic_slice(z, (0, me_i * (6 * D // N_DEV)), (N_DEV, 6 * D // N_DEV))
    lhs = jnp.concatenate([fac[:, :D], jnp.broadcast_to(a16[1:2].astype(F32), (N_DEV, D))], axis=0)
    rhs = jnp.concatenate([my_cols(fac[:, D:7 * D]), my_cols(fac[:, 7 * D:])], axis=0)
    g_ada_w = _mm("mm_gwada", lhs, rhs, "tn", F32)
    big_un["ada_w"] = _reduce_adam("adam_ada_w", g_ada_w[None], wts["ada_w"][0], mom1["ada_w"][0], mom2["ada_w"][0],
                                   BIG_ADAM_ROWS["ada_w"])

    def placed(src, n):
        if n not in CONV:
            return src[n]
        return lax.dynamic_update_slice(jnp.zeros((3, XBC), F32), src[n][0], (0, me_i * CONV_SHARD))

    small_out = _reduce_adam("adam_small", recv_s, *[_pack([placed(src, n) for n in small_names], rows=rows_small)
                                                     for src in (wts, mom1, mom2)], rows_small)
    small_shapes = [wts[n].shape if n in SMALL else (3, XBC) for n in small_names]
    small_un = [_unpack(buf, small_shapes) for buf in small_out]

    res = [{}, {}, {}, {}]
    for kind in range(4):
        for n in BIG:
            res[kind][n] = big_un[n][kind].reshape(wts[n].shape)
        for n, val in zip(small_names, small_un[kind]):
            if n in CONV:
                val = lax.dynamic_slice(val, (0, me_i * CONV_SHARD), (3, CONV_SHARD)).reshape(wts[n].shape)
            res[kind][n] = val
    loss = lax.psum(loss_acc[0, 0], ("x", "y", "c"))
    grad_x = d_x.reshape(x.shape)
    return (loss, grad_x, *[res[0][n] for n in order], *[res[1][n] for n in order], *[res[2][n] for n in order],
            *[res[3][n] for n in order])
```

```python
import functools

import jax
import jax.numpy as jnp
from jax import lax
from jax.experimental import pallas as pl
from jax.experimental.pallas import tpu as pltpu

F32 = jnp.float32
BF16 = jnp.bfloat16
HI = lax.Precision.HIGHEST
HIGH = lax.Precision.HIGH
MESH = pl.DeviceIdType.MESH

D = 1024
CH = 64
TM = 256
EPS = 1e-6
NEG = -1e30
G_HEADS = 8
DK = 128
S_HEADS = 32
S_P = 64
S_GROUPS = 4
S_N = 128
S_INNER = 2048
XBC = 3072
D_FF = 2816
N_DEV = 8
PACK_W = 1024
VMEM_LIMIT = 56 * 1024 * 1024

ADAM_LR = 0.001
ADAM_B1 = 0.9
ADAM_B2 = 0.999
ADAM_EPS = 1e-08
ADAM_WD = 0.01
ADAM_STEP = 10

C_QKV, C_XBC, C_ZS, C_GATE, C_ZG, C_AB, C_DT, C_END = 0, 3072, 6144, 8192, 10240, 11264, 11392, 11520
O_QKV, O_ZG, O_AB, O_ZS, O_XBC, O_DT, O_GATE, O_END = 0, 3072, 4096, 4128, 6176, 9248, 9312, 11360


def _dot(a, b, prec=None):
    return jnp.dot(a, b, precision=prec, preferred_element_type=F32)


def _dot_nt(a, b, prec=None):
    return lax.dot_general(a, b, (((1,), (1,)), ((), ())), precision=prec, preferred_element_type=F32)


def _dot_tn(a, b, prec=None):
    return lax.dot_general(a, b, (((0,), (0,)), ((), ())), precision=prec, preferred_element_type=F32)


def _iota(shape, dim):
    return lax.broadcasted_iota(jnp.int32, shape, dim)


def _rms(x):
    return x * lax.rsqrt(jnp.mean(x * x, axis=-1, keepdims=True) + EPS)


def _l2n(x):
    return x * lax.rsqrt(jnp.sum(x * x, axis=-1, keepdims=True) + EPS)


def _silu(x):
    return x * jax.nn.sigmoid(x)


def _softplus(x):
    return jnp.maximum(x, 0.0) + jnp.log1p(jnp.exp(-jnp.abs(x)))


def _roll_rows(x, s):
    return pltpu.roll(x, s, 0)


def _up_raw(x, keep_up):
    return jnp.where(keep_up > 0.0, _roll_rows(x, 1), 0.0)


def _dn_raw(x, keep_dn):
    return jnp.where(keep_dn > 0.0, _roll_rows(x, x.shape[0] - 1), 0.0)


@jax.custom_vjp
def _shift_up(x, keep_up, keep_dn):
    return _up_raw(x, keep_up)


def _shift_up_fwd(x, keep_up, keep_dn):
    return _up_raw(x, keep_up), (keep_up, keep_dn)


def _shift_up_bwd(res, g):
    keep_up, keep_dn = res
    return _dn_raw(g, keep_dn), jnp.zeros_like(keep_up), jnp.zeros_like(keep_dn)


_shift_up.defvjp(_shift_up_fwd, _shift_up_bwd)


@jax.custom_vjp
def _shift_dn(x, keep_up, keep_dn):
    return _dn_raw(x, keep_dn)


def _shift_dn_fwd(x, keep_up, keep_dn):
    return _dn_raw(x, keep_dn), (keep_up, keep_dn)


def _shift_dn_bwd(res, g):
    keep_up, keep_dn = res
    return _up_raw(g, keep_up), jnp.zeros_like(keep_up), jnp.zeros_like(keep_dn)


_shift_dn.defvjp(_shift_dn_fwd, _shift_dn_bwd)


def _conv_keep(is_ctx, n):
    r = _iota((n, 1), 0)
    pos = jnp.where(is_ctx, r, r & (CH - 1))
    end = jnp.where(is_ctx, n - 1, CH - 1)
    return jnp.where(pos == 0, 0.0, 1.0).astype(F32), jnp.where(pos == end, 0.0, 1.0).astype(F32)


def _conv_silu(u, w3, b, keep_up, keep_dn):
    conv = b + _shift_up(u, keep_up, keep_dn) * w3[0] + u * w3[1] + _shift_dn(u, keep_up, keep_dn) * w3[2]
    return _silu(conv)


def _chunk_tri(n, rev):
    i = _iota((n, n), 0)
    j = _iota((n, n), 1)
    same = (i // CH) == (j // CH)
    seen = (i <= j) if rev else (i >= j)
    return jnp.where(same & seen, 1.0, 0.0).astype(F32)


def _expand_mat(rows, cols, per, base):
    r = _iota((rows, cols), 0)
    c = _iota((rows, cols), 1)
    return jnp.where(r == base + c // per, 1.0, 0.0).astype(F32)


def f_silu_rows(is_ctx, cvec):
    return (_silu(cvec).astype(BF16),)


def f_pre(is_ctx, x, xc, n1w, sc, sh, csc, csh):
    x = jnp.where(is_ctx, xc, x)
    sc_e = jnp.where(is_ctx, csc, sc)
    sh_e = jnp.where(is_ctx, csh, sh)
    a = _rms(x) * n1w * (1.0 + sc_e) + sh_e
    return (a.astype(BF16),)


def f_pre_thru(is_ctx, x, xc, n1w, sc, sh, csc, csh):
    return f_pre(is_ctx, x, xc, n1w, sc, sh, csc, csh)[0], x


def f_gdnprep(is_ctx, qkv_raw, ab_raw, cw, cb, alog, dtb):
    n = qkv_raw.shape[0]
    keep_up, keep_dn = _conv_keep(is_ctx, n)
    s = _conv_silu(qkv_raw, cw, cb, keep_up, keep_dn)
    qs, ks, vs = [], [], []
    for h in range(G_HEADS):
        qs.append(_l2n(s[:, h * DK:(h + 1) * DK]) * (DK ** -0.5))
        ks.append(_l2n(s[:, D + h * DK:D + (h + 1) * DK]))
    q = jnp.concatenate(qs, axis=1)
    k = jnp.concatenate(ks, axis=1)
    v = s[:, 2 * D:3 * D]
    lane = _iota(ab_raw.shape, 1)
    g = jnp.where(lane < 2 * G_HEADS, -jnp.exp(alog) * _softplus(ab_raw + dtb), 0.0)
    gcum = jnp.where(lane < G_HEADS, _dot(_chunk_tri(n, False), g, HI), _dot(_chunk_tri(n, True), g, HI))
    beta = jax.nn.sigmoid(ab_raw)
    return q, k, v, gcum, beta


def f_ssmprep(is_ctx, xbc_raw, dt_raw, cw, cb, alog, dtb):
    n = xbc_raw.shape[0]
    keep_up, keep_dn = _conv_keep(is_ctx, n)
    s = _conv_silu(xbc_raw, cw, cb, keep_up, keep_dn)
    xs = s[:, :S_INNER]
    bm = s[:, S_INNER:S_INNER + S_GROUPS * S_N]
    cm = s[:, S_INNER + S_GROUPS * S_N:]
    lane = _iota(dt_raw.shape, 1)
    dt = jnp.where(lane < 2 * S_HEADS, _softplus(dt_raw + dtb), 0.0)
    da = dt * (-jnp.exp(alog))
    acum = jnp.where(lane < S_HEADS, _dot(_chunk_tri(n, False), da, HI), _dot(_chunk_tri(n, True), da, HI))
    return xs, bm, cm, dt, acum


def f_post(is_ctx, o, zg, y_scan, xs, zs, gnw, ssd8, snw):
    ogs = []
    for h in range(G_HEADS):
        sl = slice(h * DK, (h + 1) * DK)
        ogs.append(_rms(o[:, sl]) * gnw * _silu(zg[:, sl]))
    og = jnp.concatenate(ogs, axis=1)
    row0 = jnp.where(_iota(ssd8.shape, 0) == 0, 1.0, 0.0).astype(F32)
    dexp = jnp.sum(_dot(ssd8 * row0, _expand_mat(128, S_INNER, S_P, 0), HI), axis=0, keepdims=True)
    y = (y_scan + dexp * xs) * _silu(zs)
    gw = S_INNER // S_GROUPS
    ys = jnp.concatenate([_rms(y[:, i * gw:(i + 1) * gw]) * snw[:, i * gw:(i + 1) * gw] for i in range(S_GROUPS)], axis=1)
    return og.astype(BF16), ys.astype(BF16)


def f_merge(is_ctx, gate, pg, ps):
    m = jax.nn.sigmoid(gate[:, :D]) * pg + jax.nn.sigmoid(gate[:, D:]) * ps
    return (m.astype(BF16),)


def f_res1(is_ctx, x, mix, g1, n2w, sc2, sh2):
    h1 = x + g1 * mix
    f = _rms(h1) * n2w * (1.0 + sc2) + sh2
    return h1, f.astype(BF16)


def f_act(is_ctx, u):
    return ((_silu(u[:, :D_FF]) * u[:, D_FF:]).astype(BF16),)


def f_final(is_ctx, h1, ff, tgt, g2, nfw):
    h2 = h1 + g2 * ff
    y = _rms(h2) * nfw
    err = y - tgt
    return (0.5 * jnp.sum(jnp.mean(err * err, axis=-1, keepdims=True), axis=0, keepdims=True),)


def _bf16_dot(kind):
    fwd_op = {"nn": _dot, "nt": _dot_nt, "tn": _dot_tn}[kind]

    @jax.custom_vjp
    def f(a, b):
        return fwd_op(a.astype(BF16), b.astype(BF16))

    def f_fwd(a, b):
        ab, bb = a.astype(BF16), b.astype(BF16)
        return fwd_op(ab, bb), (ab, bb)

    def f_bwd(res, g):
        ab, bb = res
        gb = g.astype(BF16)
        if kind == "nn":
            return _dot_nt(gb, bb), _dot_tn(ab, gb)
        if kind == "nt":
            return _dot(gb, bb), _dot_tn(gb, ab)
        return _dot_nt(bb, gb), _dot(ab, gb)

    f.defvjp(f_fwd, f_bwd)
    return f


_bdot, _bdot_nt, _bdot_tn = _bf16_dot("nn"), _bf16_dot("nt"), _bf16_dot("tn")


def _each(fn, *lists):
    return [fn(*args) for args in zip(*lists)]


def _tri_inverse_all(mats):
    n = mats[0].shape[0]
    eye = jnp.where(_iota((n, n), 0) == _iota((n, n), 1), 1.0, 0.0).astype(F32)
    t = [eye - a for a in mats]
    p = [_dot(a, a, HIGH) for a in mats]
    for r in range(5):
        t = _each(lambda t_, p_: t_ + _dot(t_, p_, HIGH), t, p)
        if r < 4:
            p = [_dot(p_, p_, HIGH) for p_ in p]
    return t


@jax.custom_vjp
def _inverse_given(a, t):
    return t


def _inverse_given_fwd(a, t):
    return t, t


def _inverse_given_bwd(t, g):
    tb = t.astype(BF16)
    return -_dot_nt(_dot_tn(tb, g.astype(BF16)).astype(BF16), tb), jnp.zeros_like(t)


_inverse_given.defvjp(_inverse_given_fwd, _inverse_given_bwd)


@jax.custom_vjp
def _dot_high(a, b):
    return _dot(a, b, HIGH)


def _dot_high_fwd(a, b):
    return _dot(a, b, HIGH), (a, b)


def _dot_high_bwd(res, g):
    a, b = res
    gb = g.astype(BF16)
    return _dot_nt(gb, b.astype(BF16)), _dot_tn(a.astype(BF16), gb)


_dot_high.defvjp(_dot_high_fwd, _dot_high_bwd)


def gdn_local(qs, ks, vs, gcs, grs, bcs, rev, t_known=None):
    c = qs[0].shape[0]
    ii = _iota((c, c), 0)
    jj = _iota((c, c), 1)
    incl = (ii <= jj) if rev else (ii >= jj)
    strict = (ii < jj) if rev else (ii > jj)
    decay = _each(lambda gc, gr: jnp.exp(jnp.where(incl, gc - gr, NEG)), gcs, grs)
    kb = _each(lambda k, bc: k * bc, ks, bcs)
    a = _each(lambda kb_, k, dc: jnp.where(strict, _bdot_nt(kb_, k) * dc, 0.0), kb, ks, decay)
    t = _tri_inverse_all(a) if t_known is None else _each(_inverse_given, a, t_known)
    eg = [jnp.exp(gc) for gc in gcs]
    rhs = _each(lambda kb_, eg_, v, bc: jnp.concatenate([kb_ * eg_, v * bc], axis=1), kb, eg, vs, bcs)
    wu = _each(_dot_high, t, rhs)
    lhs = _each(lambda wu_, q, eg_: jnp.concatenate([wu_[:, :DK], q * eg_], axis=0), wu, qs, eg)
    attn = _each(lambda q, k, dc: _bdot_nt(q, k) * dc, qs, ks, decay)
    return wu, attn, lhs, t


def gdn_state(ss, wu, attn, lhs, ks, gcs, rev):
    c = ks[0].shape[0]
    is_last = _iota((c, 1), 0) == (0 if rev else c - 1)
    ws = _each(_bdot, lhs, ss)
    v_new = _each(lambda wu_, ws_: wu_[:, DK:] - ws_[:c], wu, ws)
    o = _each(lambda ws_, at, vn: ws_[c:] + _bdot(at, vn), ws, attn, v_new)
    gtot = [jnp.sum(jnp.where(is_last, gc, 0.0), axis=0, keepdims=True) for gc in gcs]
    s_new = _each(lambda s, k, gc, gt_, vn: s * jnp.exp(gt_) + _bdot_tn(k * jnp.exp(gt_ - gc), vn), ss, ks, gcs, gtot, v_new)
    return s_new, o


def gdn_chunk(ss, qs, ks, vs, gcs, grs, bcs, rev, t_known=None):
    wu, attn, lhs, t = gdn_local(qs, ks, vs, gcs, grs, bcs, rev, t_known)
    s_new, o = gdn_state(ss, wu, attn, lhs, ks, gcs, rev)
    return s_new, o, t


def ssd_pick(dt0s, dt1s, ac0s, ac1s, ar0s, ar1s):
    lo = _iota((CH, 128), 1) < S_P
    pick = lambda u0, u1: jnp.where(lo, u0, u1)
    return _each(pick, dt0s, dt1s), _each(pick, ac0s, ac1s), _each(pick, ar0s, ar1s)


def ssd_local(xs, dts, acs, acr, bgs, cgs, rev):
    c = xs[0].shape[0]
    npair = len(xs)
    grp = [p * len(bgs) // npair for p in range(npair)]
    lane = _iota((c, 128), 1)
    ii = _iota((c, 128), 0)
    jl = lane & (S_P - 1)
    lo = lane < S_P
    seen = (ii <= jl) if rev else (ii >= jl)
    last = 0 if rev else c - 1
    split = lambda z: jnp.concatenate([jnp.where(lo, z, 0.0), jnp.where(lo, 0.0, z)], axis=0)
    cb = _each(lambda bg, cg: _bdot_nt(cg, jnp.concatenate([bg, bg], axis=0)), bgs, cgs)
    seg = _each(lambda ac, ar: jnp.exp(jnp.where(seen, ac - ar, NEG)), acs, acr)
    xdt = _each(lambda x, dt: x * dt, xs, dts)
    ydiag = [_bdot(cb[grp[p]] * seg[p], split(xdt[p])) for p in range(npair)]
    eac = [jnp.exp(ac) for ac in acs]
    atot = [jnp.sum(jnp.where(ii == last, ac, 0.0), axis=0, keepdims=True) for ac in acs]
    upd = [_bdot_tn(bgs[grp[p]], xdt[p] * jnp.exp(atot[p] - acs[p])) for p in range(npair)]
    return ydiag, eac, [jnp.exp(at) for at in atot], upd


def ssd_state(hts, ydiag, eac, etot, upd, cgs):
    npair = len(hts)
    grp = [p * len(cgs) // npair for p in range(npair)]
    y = [ydiag[p] + _bdot(cgs[grp[p]], hts[p]) * eac[p] for p in range(npair)]
    h_new = [hts[p] * etot[p] + upd[p] for p in range(npair)]
    return h_new, y


def ssd_chunk(hts, xs, dts, acs, acr, bgs, cgs, rev):
    ydiag, eac, etot, upd = ssd_local(xs, dts, acs, acr, bgs, cgs, rev)
    return ssd_state(hts, ydiag, eac, etot, upd, cgs)


def f_adamw(w, g, m, v):
    m = ADAM_B1 * m + (1.0 - ADAM_B1) * g
    v = ADAM_B2 * v + (1.0 - ADAM_B2) * jnp.square(g)
    m_hat = m / (1.0 - ADAM_B1 ** ADAM_STEP)
    v_hat = v / (1.0 - ADAM_B2 ** ADAM_STEP)
    delta = -ADAM_LR * (m_hat / (jnp.sqrt(v_hat) + ADAM_EPS) + ADAM_WD * w)
    return delta, m, v


def _cparams(sem):
    return pltpu.CompilerParams(dimension_semantics=sem, vmem_limit_bytes=VMEM_LIMIT)


def _pick(n, target):
    if n <= target:
        return n
    best = None
    for t in range(128, target + 1, 128):
        if n % t == 0:
            best = t
    assert best is not None, (n, target)
    return best


def _row_spec(tm, width, colblk, rowoff):
    return pl.BlockSpec((tm, width), lambda i: (i + rowoff, colblk))


def _par_spec(shape):
    nd = len(shape)
    return pl.BlockSpec(tuple(shape), lambda i: (0,) * nd)


def _rowwise(name, fn, rows, pars, outs, ntiles, base=1, tm=TM):
    nr, npar = len(rows), len(pars)

    def body(*refs):
        is_ctx = (pl.program_id(0) + base) == 0
        res = fn(is_ctx, *[r[...] for r in refs[:nr]], *[p[...] for p in refs[nr:nr + npar]])
        for o_ref, r in zip(refs[nr + npar:], res):
            o_ref[...] = r.astype(o_ref.dtype)

    return pl.pallas_call(
        body, grid=(ntiles,), name=name,
        in_specs=[_ct_spec(tm, d) for d in rows] + [_par_spec(p.shape) for p in pars],
        out_specs=[_row_spec(tm, wd, 0, 0) for (wd, _) in outs],
        out_shape=[jax.ShapeDtypeStruct((ntiles * tm, wd), dt) for (wd, dt) in outs],
        compiler_params=_cparams(("arbitrary",)),
    )(*[r[0] for r in rows], *pars)


def _ct_spec(tm, desc):
    _, wd, cb, ro = desc[:4]
    flag = desc[4] if len(desc) > 4 else False
    if flag == "first":
        return pl.BlockSpec((tm, wd), lambda i: (0, cb))
    if flag:
        return pl.BlockSpec((tm, wd), lambda i: (jnp.maximum(i + ro, 0), cb))
    return _row_spec(tm, wd, cb, ro)


def _rowwise_bwd(name, fn, rows, pars, cts, drows, dpars, ntiles, base=1, loss_out=False, tm=TM):
    nr, npar = len(rows), len(pars)
    ct_rows = [d for ct in cts if isinstance(ct, list) for d in ct]
    nct = len(ct_rows)

    def body(*refs):
        i = pl.program_id(0)
        is_ctx = (i + base) == 0
        rows_v = [r[...] for r in refs[:nr]]
        pars_v = [p[...] for p in refs[nr:nr + npar]]
        ct_refs = list(refs[nr + npar:nr + npar + nct])
        out_refs = list(refs[nr + npar + nct:])
        outs, vjp = jax.vjp(lambda rv, pv: fn(is_ctx, *rv, *pv), rows_v, pars_v)

        def ct_value(desc):
            val = ct_refs.pop(0)[...].astype(F32)
            if len(desc) > 4 and desc[4]:
                val = jnp.where(is_ctx, 0.0, val)
            return val

        ct_vals = []
        for o, ct in zip(outs, cts):
            if ct is None:
                ct_vals.append(jnp.zeros_like(o))
            elif isinstance(ct, str):
                ct_vals.append(jnp.ones_like(o))
            else:
                acc = ct_value(ct[0])
                for desc in ct[1:]:
                    acc = acc + ct_value(desc)
                ct_vals.append(acc.astype(o.dtype))
        d_rows, d_pars = vjp(tuple(ct_vals))
        for (ri, _), o_ref in zip(drows, out_refs[:len(drows)]):
            o_ref[...] = d_rows[ri].astype(o_ref.dtype)
        acc_refs = out_refs[len(drows):]
        acc_vals = [d_pars[pi] for pi in dpars]
        if loss_out:
            acc_vals.append(jnp.broadcast_to(outs[0], (8, 128)))

        @pl.when(i == 0)
        def _():
            for o_ref, val in zip(acc_refs, acc_vals):
                o_ref[...] = val

        @pl.when(i > 0)
        def _():
            for o_ref, val in zip(acc_refs, acc_vals):
                o_ref[...] += val

    acc_shapes = [pars[pi].shape for pi in dpars] + ([(8, 128)] if loss_out else [])
    lat_row = lambda ri: len(rows[ri]) > 4 and rows[ri][4] is True
    return pl.pallas_call(
        body, grid=(ntiles,), name=name,
        in_specs=[_ct_spec(tm, d) for d in rows] + [_par_spec(p.shape) for p in pars]
        + [_ct_spec(tm, d) for d in ct_rows],
        out_specs=[_ct_spec(tm, (None, rows[ri][1], 0, -1, True) if lat_row(ri) else (None, rows[ri][1], 0, 0))
                   for (ri, _) in drows] + [_par_spec(s) for s in acc_shapes],
        out_shape=[jax.ShapeDtypeStruct(((ntiles - int(lat_row(ri))) * tm, rows[ri][1]), dt) for (ri, dt) in drows]
        + [jax.ShapeDtypeStruct(tuple(s), F32) for s in acc_shapes],
        compiler_params=_cparams(("arbitrary",)),
    )(*[r[0] for r in rows], *pars, *[r[0] for r in ct_rows])


def _mm(name, a, b, mode, out_dtype, tm=1024, tn=1024, tk=1024, ride=None):
    if mode == "nn":
        (m, kd), (_, n) = a.shape, b.shape
    elif mode == "nt":
        (m, kd), (n, _) = a.shape, b.shape
    else:
        (kd, m), (_, n) = a.shape, b.shape
    tm, tn, tk = _pick(m, tm), _pick(n, tn), _pick(kd, tk)
    nk = kd // tk
    a_spec = {"nn": pl.BlockSpec((tm, tk), lambda i, j, k: (i, k)), "nt": pl.BlockSpec((tm, tk), lambda i, j, k: (i, k)),
              "tn": pl.BlockSpec((tk, tm), lambda i, j, k: (k, i))}[mode]
    b_spec = {"nn": pl.BlockSpec((tk, tn), lambda i, j, k: (k, j)), "nt": pl.BlockSpec((tn, tk), lambda i, j, k: (j, k)),
              "tn": pl.BlockSpec((tk, tn), lambda i, j, k: (k, j))}[mode]
    dot = {"nn": _dot, "nt": _dot_nt, "tn": _dot_tn}[mode]

    if nk == 1:
        def body(a_ref, b_ref, o_ref):
            o_ref[...] = dot(a_ref[...].astype(BF16), b_ref[...].astype(BF16)).astype(o_ref.dtype)
    else:
        def body(a_ref, b_ref, o_ref, acc_ref):
            k = pl.program_id(2)
            part = dot(a_ref[...].astype(BF16), b_ref[...].astype(BF16))

            @pl.when(k == 0)
            def _():
                acc_ref[...] = part

            @pl.when((k > 0) & (k < nk - 1))
            def _():
                acc_ref[...] += part

            @pl.when(k == nk - 1)
            def _():
                o_ref[...] = (acc_ref[...] + part).astype(o_ref.dtype)

    grid = (m // tm, n // tn, nk)
    at = lambda pos: functools.reduce(jnp.logical_and, [pl.program_id(ax) == pos(g) for ax, g in enumerate(grid)])
    r_in, r_out, r_shapes, r_scr, r_ops = _ride_args(ride)
    res = pl.pallas_call(
        _ride(body, ride, 2, 1, lambda: at(lambda g: 0), lambda: at(lambda g: g - 1)), grid=grid, name=name,
        in_specs=[a_spec, b_spec] + r_in,
        out_specs=[pl.BlockSpec((tm, tn), lambda i, j, k: (i, j))] + r_out,
        out_shape=[jax.ShapeDtypeStruct((m, n), out_dtype)] + r_shapes,
        scratch_shapes=([] if nk == 1 else [pltpu.VMEM((tm, tn), F32)]) + r_scr,
        compiler_params=_cparams(("arbitrary", "arbitrary", "arbitrary")),
    )(a, b, *r_ops)
    return res[0] if ride is None else (res[0], res[1:])


def _chunk_index(i, nch, nctx, rev):
    if not rev:
        return i
    return jnp.where(i < nctx, nctx - 1 - i, nch + nctx - 1 - i)


def _gdn_cols(d):
    return [d * G_HEADS + h for h in range(G_HEADS)], [2 * G_HEADS + d * G_HEADS + h for h in range(G_HEADS)]


def _gdn_operands(q_ref, k_ref, v_ref, g_ref, b_ref, d, rows=slice(None)):
    cols_g, cols_b = _gdn_cols(d)
    sls = [slice(h * DK, (h + 1) * DK) for h in range(G_HEADS)]
    gt, bt = g_ref[rows, :], b_ref[rows, :]
    gtt = gt.T
    qs = [q_ref[rows, sl] for sl in sls]
    ks = [k_ref[rows, sl] for sl in sls]
    vs = [v_ref[rows, sl] for sl in sls]
    gcs = [gt[:, cg:cg + 1] for cg in cols_g]
    grs = [gtt[cg:cg + 1, :] for cg in cols_g]
    bcs = [bt[:, cb:cb + 1] for cb in cols_b]
    return sls, qs, ks, vs, gcs, grs, bcs


GDN_FWD_CHUNKS = 4


def _gdn_scan_fwd(name, q, k, v, gcum, beta, d, nctx, ride=None, add=None):
    t = q.shape[0]
    nch = t // CH
    per = GDN_FWD_CHUNKS
    nst = nch // per
    rev = d == 1
    bix = lambda i: _chunk_index(i, nst, nctx // per, rev)
    full = pl.BlockSpec((per * CH, D), lambda i: (bix(i), 0))
    wide = pl.BlockSpec((per * CH, 128), lambda i: (bix(i), 0))
    order = list(range(per - 1, -1, -1)) if rev else list(range(per))

    def body(q_ref, k_ref, v_ref, g_ref, b_ref, *rest):
        add_ref = rest[0] if add is not None else None
        o_ref, ss_ref, ts_ref, s_scr = rest[-4:]

        @pl.when(pl.program_id(0) == 0)
        def _():
            s_scr[...] = jnp.zeros(s_scr.shape, F32)

        ops = [_gdn_operands(q_ref, k_ref, v_ref, g_ref, b_ref, d, slice(c * CH, (c + 1) * CH)) for c in order]
        sls = ops[0][0]
        cat = [sum((o[i] for o in ops), []) for i in range(1, 7)]
        wu, attn, lhs, tinv = gdn_local(*cat, rev)
        ss = [s_scr[h] for h in range(G_HEADS)]
        for n, c in enumerate(order):
            sl = slice(n * G_HEADS, (n + 1) * G_HEADS)
            s_new, o = gdn_state(ss, wu[sl], attn[sl], lhs[sl], cat[1][sl], cat[3][sl], rev)
            for h in range(G_HEADS):
                ss_ref[n, h] = ss[h]
                ts_ref[n, h] = tinv[n * G_HEADS + h]
                rows = slice(c * CH, (c + 1) * CH)
                o_ref[rows, sls[h]] = o[h] if add is None else o[h] + add_ref[rows, sls[h]]
            ss = s_new
        for h in range(G_HEADS):
            s_scr[h] = ss[h]

    r_in, r_out, r_shapes, r_scr, r_ops = _ride_args(ride)
    extra = [] if add is None else [add]
    res = pl.pallas_call(
        _ride(body, ride, 5 + len(extra), 3, lambda: pl.program_id(0) == 0, lambda: pl.program_id(0) == nst - 1), grid=(nst,), name=name,
        in_specs=[full, full, full, wide, wide] + [full] * len(extra) + r_in,
        out_specs=[full, pl.BlockSpec((per, G_HEADS, DK, DK), lambda i: (i, 0, 0, 0)),
                   pl.BlockSpec((per, G_HEADS, CH, CH), lambda i: (i, 0, 0, 0))] + r_out,
        out_shape=[jax.ShapeDtypeStruct((t, D), F32), jax.ShapeDtypeStruct((nch, G_HEADS, DK, DK), F32),
                   jax.ShapeDtypeStruct((nch, G_HEADS, CH, CH), F32)] + r_shapes,
        scratch_shapes=[pltpu.VMEM((G_HEADS, DK, DK), F32)] + r_scr,
        compiler_params=_cparams(("arbitrary",)),
    )(q, k, v, gcum, beta, *extra, *r_ops)
    return res if ride is None else (*res[:3], res[3:])


def _gdn_scan_bwd(name, q, k, v, gcum, beta, ssave, tsave, do, d, nctx, ride=None):
    t = q.shape[0]
    nch = t // CH
    rev = d == 1
    cix = lambda ib: _chunk_index(nch - 1 - ib, nch, nctx, rev)
    full = pl.BlockSpec((CH, D), lambda ib: (cix(ib), 0))
    wide = pl.BlockSpec((CH, 128), lambda ib: (cix(ib), 0))
    do_spec = pl.BlockSpec((CH, D), lambda ib: (jnp.maximum(cix(ib), nctx) - nctx, 0))

    def body(q_ref, k_ref, v_ref, g_ref, b_ref, ss_ref, ts_ref, do_ref, dq_ref, dk_ref, dv_ref, dg_ref, db_ref, ds_scr):
        ib = pl.program_id(0)

        @pl.when(ib == 0)
        def _():
            ds_scr[...] = jnp.zeros(ds_scr.shape, F32)

        sls, qs, ks, vs, gcs, grs, bcs = _gdn_operands(q_ref, k_ref, v_ref, g_ref, b_ref, d)
        t_known = [ts_ref[0, h] for h in range(G_HEADS)]
        cols_g, cols_b = _gdn_cols(d)
        is_lat = cix(ib) >= nctx
        ss = [ss_ref[0, h] for h in range(G_HEADS)]
        do_v = [jnp.where(is_lat, do_ref[:, sl], 0.0) for sl in sls]
        ds_in = [ds_scr[h] for h in range(G_HEADS)]
        _, vjp = jax.vjp(lambda *a: gdn_chunk(*a, rev, t_known)[:2], ss, qs, ks, vs, gcs, grs, bcs)
        ds, dq, dk, dv, dgc, dgr, dbc = vjp((ds_in, do_v))
        lane = _iota((CH, 128), 1)
        sub = _iota((128, CH), 0)
        dg = jnp.zeros((CH, 128), F32)
        dgt = jnp.zeros((128, CH), F32)
        db = jnp.zeros((CH, 128), F32)
        for h in range(G_HEADS):
            ds_scr[h] = ds[h]
            dq_ref[:, sls[h]] = dq[h]
            dk_ref[:, sls[h]] = dk[h]
            dv_ref[:, sls[h]] = dv[h]
            dg = dg + jnp.where(lane == cols_g[h], dgc[h], 0.0)
            dgt = dgt + jnp.where(sub == cols_g[h], dgr[h], 0.0)
            db = db + jnp.where(lane == cols_b[h], dbc[h], 0.0)
        dg_ref[...] = dg + dgt.T
        db_ref[...] = db

    r_in, r_out, r_shapes, r_scr, r_ops = _ride_args(ride)
    res = pl.pallas_call(
        _ride(body, ride, 8, 5, lambda: pl.program_id(0) == 0, lambda: pl.program_id(0) == nch - 1), grid=(nch,), name=name,
        in_specs=[full, full, full, wide, wide,
                  pl.BlockSpec((1, G_HEADS, DK, DK), lambda ib: (nch - 1 - ib, 0, 0, 0)),
                  pl.BlockSpec((1, G_HEADS, CH, CH), lambda ib: (nch - 1 - ib, 0, 0, 0)), do_spec] + r_in,
        out_specs=[full, full, full, wide, wide] + r_out,
        out_shape=[jax.ShapeDtypeStruct((t, D), F32)] * 3 + [jax.ShapeDtypeStruct((t, 128), F32)] * 2 + r_shapes,
        scratch_shapes=[pltpu.VMEM((G_HEADS, DK, DK), F32)] + r_scr,
        compiler_params=_cparams(("arbitrary",)),
    )(q, k, v, gcum, beta, ssave, tsave, do, *r_ops)
    return res if ride is None else (*res[:5], res[5:])


N_PAIRS = S_HEADS // 2


def _ssd_operands(x_ref, dt_ref, ac_ref, b_ref, c_ref, d, rows=slice(None)):
    sls = [slice(p * 128, (p + 1) * 128) for p in range(N_PAIRS)]
    gsl = [slice(g * S_N, (g + 1) * S_N) for g in range(S_GROUPS)]
    cols = [d * S_HEADS + h for h in range(S_HEADS)]
    dtc, acc = dt_ref[rows, :], ac_ref[rows, :]
    act = jnp.concatenate([acc, acc], axis=0).T
    col = lambda z, cc: z[:, cc:cc + 1]
    dts, acs, acr = ssd_pick(
        [col(dtc, cols[2 * p]) for p in range(N_PAIRS)], [col(dtc, cols[2 * p + 1]) for p in range(N_PAIRS)],
        [col(acc, cols[2 * p]) for p in range(N_PAIRS)], [col(acc, cols[2 * p + 1]) for p in range(N_PAIRS)],
        [act[cols[2 * p]:cols[2 * p] + 1, :] for p in range(N_PAIRS)],
        [act[cols[2 * p + 1]:cols[2 * p + 1] + 1, :] for p in range(N_PAIRS)])
    ops = ([x_ref[rows, sl] for sl in sls], dts, acs, acr, [b_ref[rows, gs] for gs in gsl], [c_ref[rows, gs] for gs in gsl])
    return sls, gsl, cols, ops


SSD_FWD_CHUNKS = 4


def _ssd_scan_fwd(name, xs, dtc, acc, bm, cm, d, nctx, add=None):
    t = xs.shape[0]
    nch = t // CH
    per = SSD_FWD_CHUNKS
    nst = nch // per
    rev = d == 1
    bix = lambda i: _chunk_index(i, nst, nctx // per, rev)
    inner = pl.BlockSpec((per * CH, S_INNER), lambda i: (bix(i), 0))
    wide = pl.BlockSpec((per * CH, 128), lambda i: (bix(i), 0))
    grp = pl.BlockSpec((per * CH, S_GROUPS * S_N), lambda i: (bix(i), 0))
    order = list(range(per - 1, -1, -1)) if rev else list(range(per))

    def body(x_ref, dt_ref, ac_ref, b_ref, c_ref, *rest):
        add_ref = rest[0] if add is not None else None
        y_ref, hs_ref, h_scr = rest[-3:]

        @pl.when(pl.program_id(0) == 0)
        def _():
            h_scr[...] = jnp.zeros(h_scr.shape, F32)

        loc, cgs, sls = [], [], None
        for c in order:
            sls, _, _, ops = _ssd_operands(x_ref, dt_ref, ac_ref, b_ref, c_ref, d, slice(c * CH, (c + 1) * CH))
            loc.append(ssd_local(*ops, rev))
            cgs.append(ops[5])
        hts = [h_scr[p] for p in range(N_PAIRS)]
        for n, c in enumerate(order):
            h_new, y = ssd_state(hts, *loc[n], cgs[n])
            for p in range(N_PAIRS):
                hs_ref[n, p] = hts[p]
                rows = slice(c * CH, (c + 1) * CH)
                y_ref[rows, sls[p]] = y[p] if add is None else y[p] + add_ref[rows, sls[p]]
            hts = h_new
        for p in range(N_PAIRS):
            h_scr[p] = hts[p]

    return pl.pallas_call(
        body, grid=(nst,), name=name,
        in_specs=[inner, wide, wide, grp, grp] + ([] if add is None else [inner]),
        out_specs=[inner, pl.BlockSpec((per, N_PAIRS, S_N, 128), lambda i: (i, 0, 0, 0))],
        out_shape=[jax.ShapeDtypeStruct((t, S_INNER), F32), jax.ShapeDtypeStruct((nch, N_PAIRS, S_N, 128), F32)],
        scratch_shapes=[pltpu.VMEM((N_PAIRS, S_N, 128), F32)],
        compiler_params=_cparams(("arbitrary",)),
    )(xs, dtc, acc, bm, cm, *([] if add is None else [add]))


def _ssd_scan_bwd(name, xs, dtc, acc, bm, cm, hsave, dy, d, nctx):
    t = xs.shape[0]
    nch = t // CH
    rev = d == 1
    cix = lambda ib: _chunk_index(nch - 1 - ib, nch, nctx, rev)
    inner = pl.BlockSpec((CH, S_INNER), lambda ib: (cix(ib), 0))
    wide = pl.BlockSpec((CH, 128), lambda ib: (cix(ib), 0))
    grp = pl.BlockSpec((CH, S_GROUPS * S_N), lambda ib: (cix(ib), 0))
    dy_spec = pl.BlockSpec((CH, S_INNER), lambda ib: (jnp.maximum(cix(ib), nctx) - nctx, 0))

    def body(x_ref, dt_ref, ac_ref, b_ref, c_ref, hs_ref, dy_ref, dx_ref, ddt_ref, dac_ref, db_ref, dc_ref, dh_scr, e_scr):
        ib = pl.program_id(0)

        @pl.when(ib == 0)
        def _():
            dh_scr[...] = jnp.zeros(dh_scr.shape, F32)
            r = _iota((S_INNER, 128), 0)
            e_scr[...] = jnp.where(_iota((S_INNER, 128), 1) == d * S_HEADS + r // S_P, 1.0, 0.0).astype(BF16)

        sls, gsl, cols, ops = _ssd_operands(x_ref, dt_ref, ac_ref, b_ref, c_ref, d)
        is_lat = cix(ib) >= nctx
        hts = [hs_ref[0, p] for p in range(N_PAIRS)]
        dy_v = [jnp.where(is_lat, dy_ref[:, sl], 0.0) for sl in sls]
        dh_in = [dh_scr[p] for p in range(N_PAIRS)]
        _, vjp = jax.vjp(lambda *a: ssd_chunk(*a, rev), hts, *ops)
        dh, dx, ddts, dacs, dacr, db, dc = vjp((dh_in, dy_v))
        for p in range(N_PAIRS):
            dh_scr[p] = dh[p]
            dx_ref[:, sls[p]] = dx[p]
        e_b = e_scr[...]

        def lane_sums(z):
            hi = z.astype(BF16)
            lo = (z - hi.astype(F32)).astype(BF16)
            return _dot(hi, e_b) + _dot(lo, e_b)

        ddt_ref[...] = lane_sums(jnp.concatenate(ddts, axis=1))
        dac_cols = lane_sums(jnp.concatenate(dacs, axis=1))
        sub = _iota((128, 128), 0)
        lane = _iota((128, 128), 1)
        m = jnp.zeros((128, 128), F32)
        for p in range(N_PAIRS):
            m = m + jnp.where(sub == p, jnp.sum(dacr[p], axis=0, keepdims=True), 0.0)
        mt = m.T
        s0 = jnp.where(lane == d * S_HEADS + 2 * sub, 1.0, 0.0).astype(F32)
        s1 = jnp.where(lane == d * S_HEADS + 2 * sub + 1, 1.0, 0.0).astype(F32)
        dac_ref[...] = dac_cols + _dot(mt[:CH], s0, HIGH) + _dot(mt[CH:], s1, HIGH)
        for g in range(S_GROUPS):
            db_ref[:, gsl[g]] = db[g]
            dc_ref[:, gsl[g]] = dc[g]

    return pl.pallas_call(
        body, grid=(nch,), name=name,
        in_specs=[inner, wide, wide, grp, grp,
                  pl.BlockSpec((1, N_PAIRS, S_N, 128), lambda ib: (nch - 1 - ib, 0, 0, 0)), dy_spec],
        out_specs=[inner, wide, wide, grp, grp],
        out_shape=[jax.ShapeDtypeStruct((t, S_INNER), F32)] + [jax.ShapeDtypeStruct((t, 128), F32)] * 2
        + [jax.ShapeDtypeStruct((t, S_GROUPS * S_N), F32)] * 2,
        scratch_shapes=[pltpu.VMEM((N_PAIRS, S_N, 128), F32), pltpu.VMEM((S_INNER, 128), BF16)],
        compiler_params=_cparams(("arbitrary",)),
    )(xs, dtc, acc, bm, cm, hsave, dy)


def _mesh_pos():
    return lax.axis_index("x"), lax.axis_index("y"), lax.axis_index("c")


def _hbm_specs(n):
    return [pl.BlockSpec(memory_space=pl.ANY)] * n


def _sem_shapes(nw):
    return [pltpu.SemaphoreType.DMA((nw, 7)), pltpu.SemaphoreType.DMA((nw, 7)), pltpu.SemaphoreType.DMA((nw,))]


class _AllGather:
    def __init__(self, shards):
        self.arrays = list(shards)
        self.out_shapes = [jax.ShapeDtypeStruct((N_DEV,) + xs.shape, xs.dtype) for xs in shards]

    def _parts(self, x_refs, out_refs, sems):
        send_sems, recv_sems, local_sems = sems
        nw = len(x_refs)
        x, y, c = _mesh_pos()
        me, sibling = (x, y, c), (x, y, 1 - c)
        chips = [(1 - x, y), (x, 1 - y), (1 - x, 1 - y)]

        def slot(w, px, py, pc):
            return out_refs[w].at[4 * px + 2 * py + pc]

        def copy(w, k, block, to, src=None):
            return pltpu.make_async_remote_copy(
                src_ref=slot(w, *block) if src is None else src, dst_ref=slot(w, *block),
                send_sem=send_sems.at[w, k], recv_sem=recv_sems.at[w, k], device_id=to, device_id_type=MESH)

        mine = [pltpu.make_async_copy(x_refs[w], slot(w, *me), local_sems.at[w]) for w in range(nw)]
        first = []
        for w in range(nw):
            first.append(copy(w, 0, me, sibling, src=x_refs[w]))
            first += [copy(w, 1 + j, me, (*chip, c), src=x_refs[w]) for j, chip in enumerate(chips)]
        return nw, me, sibling, chips, c, copy, mine, first

    def start(self, x_refs, out_refs, sems):
        _, _, _, _, _, _, mine, first = self._parts(x_refs, out_refs, sems)
        for cp in mine + first:
            cp.start()

    def finish(self, x_refs, out_refs, sems):
        nw, me, sibling, chips, c, copy, mine, first = self._parts(x_refs, out_refs, sems)
        passed = []
        for j, chip in enumerate(chips):
            for w in range(nw):
                copy(w, 1 + j, (*chip, c), me).wait_recv()
                fwd = copy(w, 4 + j, (*chip, c), sibling)
                fwd.start()
                passed.append(fwd)
        for w in range(nw):
            copy(w, 0, sibling, me).wait_recv()
            for j, chip in enumerate(chips):
                copy(w, 4 + j, (*chip, 1 - c), me).wait_recv()
        for cp in first + passed:
            cp.wait_send()
        for cp in mine:
            cp.wait()


class _AllToAll:
    def __init__(self, blocks):
        self.arrays = list(blocks)
        self.out_shapes = [jax.ShapeDtypeStruct(g.shape, g.dtype) for g in blocks]

    def _parts(self, g_refs, out_refs, sems):
        send_sems, recv_sems, local_sems = sems
        nw = len(g_refs)
        x, y, c = _mesh_pos()
        me_i = 4 * x + 2 * y + c
        mine = [pltpu.make_async_copy(g_refs[w].at[me_i], out_refs[w].at[me_i], local_sems.at[w]) for w in range(nw)]
        cps = []
        for k in range(1, N_DEV):
            px = 1 - x if (k >> 2) & 1 else x
            py = 1 - y if (k >> 1) & 1 else y
            pc = 1 - c if k & 1 else c
            for w in range(nw):
                cps.append(pltpu.make_async_remote_copy(
                    src_ref=g_refs[w].at[4 * px + 2 * py + pc], dst_ref=out_refs[w].at[me_i],
                    send_sem=send_sems.at[w, k - 1], recv_sem=recv_sems.at[w, k - 1],
                    device_id=(px, py, pc), device_id_type=MESH))
        return mine, cps

    def start(self, g_refs, out_refs, sems):
        mine, cps = self._parts(g_refs, out_refs, sems)
        for cp in mine + cps:
            cp.start()

    def finish(self, g_refs, out_refs, sems):
        mine, cps = self._parts(g_refs, out_refs, sems)
        for cp in cps + mine:
            cp.wait()


def _exchange(name, ex):
    nw = len(ex.arrays)

    def body(*refs):
        ins, outs, sems = refs[:nw], refs[nw:2 * nw], refs[2 * nw:]
        ex.start(ins, outs, sems)
        ex.finish(ins, outs, sems)

    return pl.pallas_call(body, name=name, out_shape=ex.out_shapes, in_specs=_hbm_specs(nw), out_specs=_hbm_specs(nw),
                          scratch_shapes=_sem_shapes(nw))(*ex.arrays)


def _ride(body, ex, n_in, n_out, is_first, is_last):
    if ex is None:
        return body
    nw = len(ex.arrays)

    def riding(*refs):
        ins, ex_in = refs[:n_in], refs[n_in:n_in + nw]
        outs = refs[n_in + nw:n_in + nw + n_out]
        ex_out = refs[n_in + nw + n_out:n_in + 2 * nw + n_out]
        rest = refs[n_in + 2 * nw + n_out:]
        scratch, sems = rest[:len(rest) - 3], rest[len(rest) - 3:]

        @pl.when(is_first())
        def _():
            ex.start(ex_in, ex_out, sems)

        body(*ins, *outs, *scratch)

        @pl.when(is_last())
        def _():
            ex.finish(ex_in, ex_out, sems)

    return riding


def _ride_args(ex):
    if ex is None:
        return [], [], [], [], []
    nw = len(ex.arrays)
    return _hbm_specs(nw), _hbm_specs(nw), list(ex.out_shapes), _sem_shapes(nw), list(ex.arrays)


def _reduce_adam(name, recv, w, m, v, tm):
    rows, width = w.shape
    nslot = recv.shape[0]

    def body(recv_ref, w_ref, m_ref, v_ref, g_ref, d_ref, m2_ref, v2_ref):
        g = recv_ref[0].astype(F32)
        for s in range(1, nslot):
            g = g + recv_ref[s].astype(F32)
        delta, m2, v2 = f_adamw(w_ref[...], g, m_ref[...], v_ref[...])
        g_ref[...] = g
        d_ref[...] = delta
        m2_ref[...] = m2
        v2_ref[...] = v2

    row = pl.BlockSpec((tm, width), lambda i: (i, 0))
    return pl.pallas_call(
        body, grid=(rows // tm,), name=name,
        in_specs=[pl.BlockSpec((nslot, tm, width), lambda i: (0, i, 0)), row, row, row],
        out_specs=[row] * 4,
        out_shape=[jax.ShapeDtypeStruct((rows, width), F32)] * 4,
        compiler_params=_cparams(("arbitrary",)),
    )(recv, w, m, v)


BIG = ("w_in", "ada_w", "w_br_gdn", "w_br_ssm", "w_out", "w_ffn_in", "w_ffn_out")
BIG_FIRST = ("ada_w", "w_in")
BIG_LATE = ("w_br_gdn", "w_br_ssm", "w_out", "w_ffn_in", "w_ffn_out")
BIG_COL_SHARDED = ("w_in", "ada_w", "w_ffn_in")
BIG_ADAM_ROWS = dict(w_in=128, ada_w=256, w_br_gdn=128, w_br_ssm=256, w_out=128, w_ffn_in=256, w_ffn_out=352)
CONV = ("gdn_conv_w", "ssm_conv_w")
SMALL = ("c_ctx", "ada_b", "norm1_w", "gdn_conv_b", "gdn_a_log", "gdn_dt_bias", "gdn_norm_w", "ssm_conv_b",
         "ssm_a_log", "ssm_dt_bias", "ssm_d", "ssm_norm_w", "norm2_w", "norm_f_w")
CONV_SHARD = XBC // N_DEV


def _to_rows(a):
    flat = a.reshape(-1)
    pad = (-flat.shape[0]) % PACK_W
    if pad:
        flat = jnp.pad(flat, (0, pad))
    return flat.reshape(-1, PACK_W)


def _pack(arrays, rows=None):
    buf = jnp.concatenate([_to_rows(a) for a in arrays], axis=0)
    if rows is not None and rows > buf.shape[0]:
        buf = jnp.pad(buf, ((0, rows - buf.shape[0]), (0, 0)))
    return buf


def _unpack(buf, shapes):
    out, r0 = [], 0
    for shp in shapes:
        n = 1
        for s in shp:
            n *= s
        nr = -(-n // PACK_W)
        out.append(buf[r0:r0 + nr].reshape(-1)[:n].reshape(shp))
        r0 += nr
    return out


def _full_from_blocks(blocks, col_sharded):
    _, r, c = blocks.shape
    if col_sharded:
        return jnp.transpose(blocks, (1, 0, 2)).reshape(r, N_DEV * c)
    return blocks.reshape(N_DEV * r, c)


def _blocks_from_full(full, col_sharded):
    if col_sharded:
        r, c = full.shape[0], full.shape[1] // N_DEV
        return jnp.transpose(full.reshape(r, N_DEV, c), (1, 0, 2))
    return full.reshape(N_DEV, full.shape[0] // N_DEV, full.shape[1])


def _pad_cols(a, n):
    return jnp.pad(a, ((0, 0), (0, n - a.shape[1])))


def _w_cat(w_in):
    return jnp.concatenate([
        w_in[:, O_QKV:O_ZG], w_in[:, O_XBC:O_DT], w_in[:, O_ZS:O_XBC], w_in[:, O_GATE:O_END], w_in[:, O_ZG:O_AB],
        _pad_cols(w_in[:, O_AB:O_ZS], 128), _pad_cols(w_in[:, O_DT:O_GATE], 128)], axis=1)


def _w_uncat(wc):
    return jnp.concatenate([
        wc[:, C_QKV:C_XBC], wc[:, C_ZG:C_AB], wc[:, C_AB:C_AB + (O_ZS - O_AB)], wc[:, C_ZS:C_GATE], wc[:, C_XBC:C_ZS],
        wc[:, C_DT:C_DT + (O_GATE - O_DT)], wc[:, C_GATE:C_ZG]], axis=1)


def _pad_row(vec, n=128):
    vec = vec.reshape(1, -1)
    return _pad_cols(vec, n)


def kernel(x, c, ctx, c_ctx, ada_w, ada_b, norm1_w, w_in, gdn_conv_w, gdn_conv_b, gdn_a_log, gdn_dt_bias, gdn_norm_w, ssm_conv_w, ssm_conv_b, ssm_a_log, ssm_dt_bias, ssm_d, ssm_norm_w, w_br_gdn, w_br_ssm, w_out, norm2_w, w_ffn_in, w_ffn_out, norm_f_w, loss_target, m_c_ctx, m_ada_w, m_ada_b, m_norm1_w, m_w_in, m_gdn_conv_w, m_gdn_conv_b, m_gdn_a_log, m_gdn_dt_bias, m_gdn_norm_w, m_ssm_conv_w, m_ssm_conv_b, m_ssm_a_log, m_ssm_dt_bias, m_ssm_d, m_ssm_norm_w, m_w_br_gdn, m_w_br_ssm, m_w_out, m_norm2_w, m_w_ffn_in, m_w_ffn_out, m_norm_f_w, v_c_ctx, v_ada_w, v_ada_b, v_norm1_w, v_w_in, v_gdn_conv_w, v_gdn_conv_b, v_gdn_a_log, v_gdn_dt_bias, v_gdn_norm_w, v_ssm_conv_w, v_ssm_conv_b, v_ssm_a_log, v_ssm_dt_bias, v_ssm_d, v_ssm_norm_w, v_w_br_gdn, v_w_br_ssm, v_w_out, v_norm2_w, v_w_ffn_in, v_w_ffn_out, v_norm_f_w):
    wts = dict(c_ctx=c_ctx, ada_w=ada_w, ada_b=ada_b, norm1_w=norm1_w, w_in=w_in, gdn_conv_w=gdn_conv_w, gdn_conv_b=gdn_conv_b, gdn_a_log=gdn_a_log, gdn_dt_bias=gdn_dt_bias, gdn_norm_w=gdn_norm_w, ssm_conv_w=ssm_conv_w, ssm_conv_b=ssm_conv_b, ssm_a_log=ssm_a_log, ssm_dt_bias=ssm_dt_bias, ssm_d=ssm_d, ssm_norm_w=ssm_norm_w, w_br_gdn=w_br_gdn, w_br_ssm=w_br_ssm, w_out=w_out, norm2_w=norm2_w, w_ffn_in=w_ffn_in, w_ffn_out=w_ffn_out, norm_f_w=norm_f_w)
    mom1 = dict(c_ctx=m_c_ctx, ada_w=m_ada_w, ada_b=m_ada_b, norm1_w=m_norm1_w, w_in=m_w_in, gdn_conv_w=m_gdn_conv_w, gdn_conv_b=m_gdn_conv_b, gdn_a_log=m_gdn_a_log, gdn_dt_bias=m_gdn_dt_bias, gdn_norm_w=m_gdn_norm_w, ssm_conv_w=m_ssm_conv_w, ssm_conv_b=m_ssm_conv_b, ssm_a_log=m_ssm_a_log, ssm_dt_bias=m_ssm_dt_bias, ssm_d=m_ssm_d, ssm_norm_w=m_ssm_norm_w, w_br_gdn=m_w_br_gdn, w_br_ssm=m_w_br_ssm, w_out=m_w_out, norm2_w=m_norm2_w, w_ffn_in=m_w_ffn_in, w_ffn_out=m_w_ffn_out, norm_f_w=m_norm_f_w)
    mom2 = dict(c_ctx=v_c_ctx, ada_w=v_ada_w, ada_b=v_ada_b, norm1_w=v_norm1_w, w_in=v_w_in, gdn_conv_w=v_gdn_conv_w, gdn_conv_b=v_gdn_conv_b, gdn_a_log=v_gdn_a_log, gdn_dt_bias=v_gdn_dt_bias, gdn_norm_w=v_gdn_norm_w, ssm_conv_w=v_ssm_conv_w, ssm_conv_b=v_ssm_conv_b, ssm_a_log=v_ssm_a_log, ssm_dt_bias=v_ssm_dt_bias, ssm_d=v_ssm_d, ssm_norm_w=v_ssm_norm_w, w_br_gdn=v_w_br_gdn, w_br_ssm=v_w_br_ssm, w_out=v_w_out, norm2_w=v_norm2_w, w_ffn_in=v_w_ffn_in, w_ffn_out=v_w_ffn_out, norm_f_w=v_norm_f_w)
    order = list(wts)

    seq = x.shape[1]
    t = TM + seq
    ntl, nlt, nctx = t // TM, seq // TM, TM // CH

    me_i = 4 * lax.axis_index("x") + 2 * lax.axis_index("y") + lax.axis_index("c")
    conv_sh = _pack([wts[n] for n in CONV], rows=8)
    gathered = _exchange("ag_weights", _AllGather([wts[n][0].astype(BF16) for n in BIG_FIRST] + [conv_sh]))
    full = {n: _full_from_blocks(blk, n in BIG_COL_SHARDED) for n, blk in zip(BIG_FIRST, gathered)}
    late_gather = _AllGather([wts[n][0].astype(BF16) for n in BIG_LATE])
    conv_g = gathered[len(BIG_FIRST)].reshape(N_DEV, -1)
    ncv = 3 * CONV_SHARD
    for i, n in enumerate(CONV):
        off = -(-ncv // PACK_W) * PACK_W * i
        full[n] = jnp.transpose(conv_g[:, off:off + ncv].reshape(N_DEV, 3, CONV_SHARD), (1, 0, 2)).reshape(3, XBC)
    w_cat = _w_cat(full["w_in"])
    gcw = full["gdn_conv_w"].reshape(3, 1, XBC)
    scw = full["ssm_conv_w"].reshape(3, 1, XBC)

    n1w, n2w, nfw = norm1_w.reshape(1, D), norm2_w.reshape(1, D), norm_f_w.reshape(1, D)
    gcb, scb = gdn_conv_b.reshape(1, XBC), ssm_conv_b.reshape(1, XBC)
    alog16, dtb16 = _pad_row(gdn_a_log), _pad_row(gdn_dt_bias)
    alog64, dtb64 = _pad_row(ssm_a_log), _pad_row(ssm_dt_bias)
    gnw = gdn_norm_w.reshape(1, DK)
    ssd8 = jnp.tile(_pad_row(ssm_d), (8, 1))
    snw = ssm_norm_w.reshape(1, S_INNER)
    x2 = x[0]
    tgt = loss_target[0]
    cvec = jnp.concatenate([c, c_ctx.reshape(1, D), jnp.zeros((14, D), F32)], axis=0)

    a16 = _rowwise("silu_c", f_silu_rows, [(cvec, D, 0, 0)], [], [(D, BF16)], 1, tm=16)[0]
    mod = _mm("mm_mod", a16, full["ada_w"], "nn", F32) + ada_b
    sh1, sc1, g1, sh2, sc2, g2 = [mod[0:1, i * D:(i + 1) * D] for i in range(6)]
    csh1, csc1 = mod[1:2, 0:D], mod[1:2, D:2 * D]

    pre_pars = [n1w, sc1, sh1, csc1, csh1]
    pre_rows = [(x2, D, 0, -1, True), (ctx[0], D, 0, 0, "first")]
    a = _rowwise("pre", f_pre, pre_rows, pre_pars, [(D, BF16)], ntl, base=0)[0]
    proj = _mm("mm_proj", a, w_cat, "nn", F32, tm=1408, tn=1280)
    gp_rows = [(proj, XBC, C_QKV // XBC, 0), (proj, 128, C_AB // 128, 0)]
    gp_pars = [gcw, gcb, alog16, dtb16]
    q, k, v, gcum, beta = _rowwise("gdnprep", f_gdnprep, gp_rows, gp_pars, [(D, F32)] * 3 + [(128, F32)] * 2, ntl, base=0)
    sp_rows = [(proj, XBC, C_XBC // XBC, 0), (proj, 128, C_DT // 128, 0)]
    sp_pars = [scw, scb, alog64, dtb64]
    xs, bm, cm, dtc, acc = _rowwise(
        "ssmprep", f_ssmprep, sp_rows, sp_pars, [(S_INNER, F32), (512, F32), (512, F32), (128, F32), (128, F32)], ntl, base=0)
    o0, ss0, ts0, gathered = _gdn_scan_fwd("gdn_fwd0", q, k, v, gcum, beta, 0, nctx, ride=late_gather)
    full.update({n: _full_from_blocks(blk, n in BIG_COL_SHARDED) for n, blk in zip(BIG_LATE, gathered)})
    o_sum, ss1, ts1 = _gdn_scan_fwd("gdn_fwd1", q, k, v, gcum, beta, 1, nctx, add=o0)
    y0, hs0 = _ssd_scan_fwd("ssd_fwd0", xs, dtc, acc, bm, cm, 0, nctx)
    y_sum, hs1 = _ssd_scan_fwd("ssd_fwd1", xs, dtc, acc, bm, cm, 1, nctx, add=y0)
    post_rows = [(o_sum, D, 0, 1), (proj, D, C_ZG // D, 1), (y_sum, S_INNER, 0, 1), (xs, S_INNER, 0, 1),
                 (proj, S_INNER, C_ZS // S_INNER, 1)]
    post_pars = [gnw, ssd8, snw]
    og, ys = _rowwise("post", f_post, post_rows, post_pars, [(D, BF16), (S_INNER, BF16)], nlt)
    pg = _mm("mm_pg", og, full["w_br_gdn"], "nn", F32)
    ps = _mm("mm_ps", ys, full["w_br_ssm"], "nn", F32, tk=2048)
    merge_rows = [(proj, S_INNER, C_GATE // S_INNER, 1), (pg, D, 0, 0), (ps, D, 0, 0)]
    merged = _rowwise("merge", f_merge, merge_rows, [], [(D, BF16)], nlt)[0]
    mix = _mm("mm_mix", merged, full["w_out"], "nn", F32)
    res_rows = [(x2, D, 0, 0), (mix, D, 0, 0)]
    res_pars = [g1, n2w, sc2, sh2]
    h1, f = _rowwise("res1", f_res1, res_rows, res_pars, [(D, F32), (D, BF16)], nlt)
    u = _mm("mm_u", f, full["w_ffn_in"], "nn", F32, tn=1408)
    hact = _rowwise("act", f_act, [(u, 2 * D_FF, 0, 0)], [], [(D_FF, BF16)], nlt)[0]
    ff = _mm("mm_ff", hact, full["w_ffn_out"], "nn", F32, tk=2816)

    fin_rows = [(h1, D, 0, 0), (ff, D, 0, 0), (tgt, D, 0, 0)]
    d_h1a, d_ff, d_g2, d_nfw, loss_acc = _rowwise_bwd(
        "final", f_final, fin_rows, [g2, nfw], ["one"], [(0, F32), (1, BF16)], [0, 1], nlt, loss_out=True)
    d_hact = _mm("mm_dhact", d_ff, full["w_ffn_out"], "nt", BF16, tn=1408)
    g_w_ffn_out = _mm("mm_gwffo", hact, d_ff, "tn", BF16, tm=1408, tk=2048)
    d_u = _rowwise_bwd("act_bwd", f_act, [(u, 2 * D_FF, 0, 0)], [], [[(d_hact, D_FF, 0, 0)]], [(0, BF16)], [], nlt)[0]
    d_f = _mm("mm_df", d_u, full["w_ffn_in"], "nt", BF16, tk=2816)
    g_w_ffn_in = _mm("mm_gwffi", f, d_u, "tn", BF16, tn=1408, tk=2048)
    d_xres, d_mix, d_g1, d_n2w, d_sc2, d_sh2 = _rowwise_bwd(
        "res1_bwd", f_res1, res_rows, res_pars, [[(d_h1a, D, 0, 0)], [(d_f, D, 0, 0)]], [(0, F32), (1, BF16)], [0, 1, 2, 3], nlt)
    d_merged = _mm("mm_dmerged", d_mix, full["w_out"], "nt", BF16)
    g_w_out = _mm("mm_gwout", merged, d_mix, "tn", BF16, tk=2048)
    d_gate, d_pg, d_ps = _rowwise_bwd(
        "merge_bwd", f_merge, merge_rows, [], [[(d_merged, D, 0, 0)]], [(0, BF16), (1, BF16), (2, BF16)], [], nlt)
    d_og = _mm("mm_dog", d_pg, full["w_br_gdn"], "nt", BF16)
    g_w_br_gdn = _mm("mm_gwbrg", og, d_pg, "tn", BF16, tk=2048)
    d_ys = _mm("mm_dys", d_ps, full["w_br_ssm"], "nt", BF16, tn=2048)
    g_w_br_ssm = _mm("mm_gwbrs", ys, d_ps, "tn", BF16, tk=2048)
    d_o, d_zg, d_y, d_xs_post, d_zs, d_gnw, d_ssd8, d_snw = _rowwise_bwd(
        "post_bwd", f_post, post_rows, post_pars, [[(d_og, D, 0, 0)], [(d_ys, S_INNER, 0, 0)]],
        [(0, F32), (1, BF16), (2, F32), (3, F32), (4, BF16)], [0, 1, 2], nlt)
    late_grads = dict(w_br_gdn=g_w_br_gdn, w_br_ssm=g_w_br_ssm, w_out=g_w_out, w_ffn_in=g_w_ffn_in, w_ffn_out=g_w_ffn_out)
    late_a2a = _AllToAll([_blocks_from_full(late_grads[n], n in BIG_COL_SHARDED) for n in BIG_LATE])
    dq0, dk0, dv0, dg0, db0, recv_late = _gdn_scan_bwd("gdn_bwd0", q, k, v, gcum, beta, ss0, ts0, d_o, 0, nctx, ride=late_a2a)
    dq1, dk1, dv1, dg1, db1 = _gdn_scan_bwd("gdn_bwd1", q, k, v, gcum, beta, ss1, ts1, d_o, 1, nctx)
    dxs0, ddt0, dac0, dbm0, dcm0 = _ssd_scan_bwd("ssd_bwd0", xs, dtc, acc, bm, cm, hs0, d_y, 0, nctx)
    dxs1, ddt1, dac1, dbm1, dcm1 = _ssd_scan_bwd("ssd_bwd1", xs, dtc, acc, bm, cm, hs1, d_y, 1, nctx)
    row = lambda arr, wd: (arr, wd, 0, 0)
    d_qkv_raw, d_ab, d_gcw, d_gcb, d_alog16, d_dtb16 = _rowwise_bwd(
        "gdnprep_bwd", f_gdnprep, gp_rows, gp_pars,
        [[row(dq0, D), row(dq1, D)], [row(dk0, D), row(dk1, D)], [row(dv0, D), row(dv1, D)],
         [row(dg0, 128), row(dg1, 128)], [row(db0, 128), row(db1, 128)]],
        [(0, BF16), (1, BF16)], [0, 1, 2, 3], ntl, base=0)
    d_xbc_raw, d_dt, d_scw, d_scb, d_alog64, d_dtb64 = _rowwise_bwd(
        "ssmprep_bwd", f_ssmprep, sp_rows, sp_pars,
        [[row(dxs0, S_INNER), row(dxs1, S_INNER), (d_xs_post, S_INNER, 0, -1, True)], [row(dbm0, 512), row(dbm1, 512)],
         [row(dcm0, 512), row(dcm1, 512)], [row(ddt0, 128), row(ddt1, 128)], [row(dac0, 128), row(dac1, 128)]],
        [(0, BF16), (1, BF16)], [0, 1, 2, 3], ntl, base=0)
    ctx_zero = lambda wd: jnp.zeros((TM, wd), BF16)
    d_proj = jnp.concatenate([
        d_qkv_raw, d_xbc_raw, jnp.concatenate([ctx_zero(S_INNER), d_zs], axis=0),
        jnp.concatenate([ctx_zero(S_INNER), d_gate], axis=0), jnp.concatenate([ctx_zero(D), d_zg], axis=0), d_ab, d_dt], axis=1)
    g_w_cat = _mm("mm_gwcat", a, d_proj, "tn", BF16, tn=768, tk=2816)
    w_in_a2a = _AllToAll([_blocks_from_full(_w_uncat(g_w_cat), True)])
    d_a, recv_w_in = _mm("mm_da", d_proj, w_cat, "nt", BF16, tk=3840, ride=w_in_a2a)
    d_x, d_n1w, d_sc1, d_sh1, d_csc1, d_csh1 = _rowwise_bwd(
        "pre_bwd", f_pre_thru, pre_rows, pre_pars, [[row(d_a, D)], [(d_xres, D, 0, -1, True)]],
        [(0, F32)], [0, 1, 2, 3, 4], ntl, base=0)
    zero4 = jnp.zeros((1, 4 * D), F32)
    d_mod = jnp.concatenate([
        jnp.concatenate([d_sh1, d_sc1, d_g1, d_sh2, d_sc2, d_g2], axis=1),
        jnp.concatenate([d_csh1, d_csc1, zero4], axis=1), jnp.zeros((14, 6 * D), F32)], axis=0)
    d_a16 = _mm("mm_da16", d_mod, full["ada_w"], "nt", F32)
    d_cvec = _rowwise_bwd("silu_c_bwd", f_silu_rows, [(cvec, D, 0, 0)], [], [[row(d_a16, D)]], [(0, F32)], [], 1, tm=16)[0]

    recv = dict(zip(BIG_LATE, recv_late), w_in=recv_w_in[0])
    big_un = {n: _reduce_adam("adam_" + n, recv[n], wts[n][0], mom1[n][0], mom2[n][0], BIG_ADAM_ROWS[n])
              for n in BIG if n != "ada_w"}

    small_g = dict(c_ctx=d_cvec[1], ada_b=d_mod[0] + d_mod[1], norm1_w=d_n1w, gdn_conv_b=d_gcb,
                   gdn_a_log=d_alog16[0, :2 * G_HEADS], gdn_dt_bias=d_dtb16[0, :2 * G_HEADS], gdn_norm_w=d_gnw,
                   ssm_conv_b=d_scb, ssm_a_log=d_alog64[0, :2 * S_HEADS], ssm_dt_bias=d_dtb64[0, :2 * S_HEADS],
                   ssm_d=d_ssd8[0, :S_HEADS], ssm_norm_w=d_snw, norm2_w=d_n2w, norm_f_w=d_nfw,
                   gdn_conv_w=d_gcw.reshape(3, XBC), ssm_conv_w=d_scw.reshape(3, XBC))
    small_names = SMALL + CONV
    factors = [a16[0].astype(F32), d_mod[0], d_mod[1]]
    round8 = lambda r: -(-r // 8) * 8
    packed_rows = lambda arrs: sum(-(-arr.size // PACK_W) for arr in arrs)
    n_small = packed_rows([small_g[n] for n in small_names])
    n_fac = packed_rows(factors)
    rows_small = round8(n_small)
    sg_pack = _pack([small_g[n] for n in small_names] + factors, rows=round8(n_small + n_fac))
    recv_all = _exchange("ag_small_grads", _AllGather([sg_pack]))[0]
    recv_s = recv_all[:, :rows_small]
    fac = recv_all[:, n_small:n_small + n_fac].reshape(N_DEV, -1)
    my_cols = lambda z: lax.dynamic_slice(z, (0, me_i * (6 * D // N_DEV)), (N_DEV, 6 * D // N_DEV))
    lhs = jnp.concatenate([fac[:, :D], jnp.broadcast_to(a16[1:2].astype(F32), (N_DEV, D))], axis=0)
    rhs = jnp.concatenate([my_cols(fac[:, D:7 * D]), my_cols(fac[:, 7 * D:])], axis=0)
    g_ada_w = _mm("mm_gwada", lhs, rhs, "tn", F32)
    big_un["ada_w"] = _reduce_adam("adam_ada_w", g_ada_w[None], wts["ada_w"][0], mom1["ada_w"][0], mom2["ada_w"][0],
                                   BIG_ADAM_ROWS["ada_w"])

    def placed(src, n):
        if n not in CONV:
            return src[n]
        return lax.dynamic_update_slice(jnp.zeros((3, XBC), F32), src[n][0], (0, me_i * CONV_SHARD))

    small_out = _reduce_adam("adam_small", recv_s, *[_pack([placed(src, n) for n in small_names], rows=rows_small)
                                                     for src in (wts, mom1, mom2)], rows_small)
    small_shapes = [wts[n].shape if n in SMALL else (3, XBC) for n in small_names]
    small_un = [_unpack(buf, small_shapes) for buf in small_out]

    res = [{}, {}, {}, {}]
    for kind in range(4):
        for n in BIG:
            res[kind][n] = big_un[n][kind].reshape(wts[n].shape)
        for n, val in zip(small_names, small_un[kind]):
            if n in CONV:
                val = lax.dynamic_slice(val, (0, me_i * CONV_SHARD), (3, CONV_SHARD)).reshape(wts[n].shape)
            res[kind][n] = val
    loss = lax.psum(loss_acc[0, 0], ("x", "y", "c"))
    grad_x = d_x.reshape(x.shape)
    return (loss, grad_x, *[res[0][n] for n in order], *[res[1][n] for n in order], *[res[2][n] for n in order],
            *[res[3][n] for n in order])
```
